```python
import math
import jax, jax.numpy as jnp
from jax import lax
import numpy as np

D_MODEL = 1024
BATCH = 8
SEQ = 8192
DEPTH = 2

GRID_W = 64
ROPE_BASE = 10000.0
EPS = 1e-6
Q_BLOCK = 128

N_GROUPS = 4
GROUP_W = D_MODEL // N_GROUPS
MIX_W = N_GROUPS * GROUP_W

A_HEADS = 4
A_NOPE = 64
A_ROPE = 32
A_VDIM = GROUP_W // A_HEADS
A_Q_LORA = 192
A_KV_LORA = 128
A_COLS = A_Q_LORA + A_KV_LORA + A_ROPE

B_HEADS = 4
B_KV_HEADS = 2
B_HDIM = GROUP_W // B_HEADS
B_COLS = (B_HEADS + 2 * B_KV_HEADS) * B_HDIM

C_HEADS = 4
C_HDIM = GROUP_W // C_HEADS
C_NGROUPS = 2
C_STATE = 64
C_CHUNK = 128
C_XBC = GROUP_W + 2 * C_NGROUPS * C_STATE
C_COLS = GROUP_W + C_XBC + 2 * C_HEADS

D_HEADS = 4
D_HDIM = GROUP_W // D_HEADS
D_CHUNK = 64
D_QKV = 3 * GROUP_W
D_COLS = D_QKV + GROUP_W + 4 * D_HEADS

IN_COLS = A_COLS + B_COLS + C_COLS + D_COLS
CONV_W = 3

D_FF = 2816
FFN_CONV_W = 3

kernel_name = "bidir_hybrid_parallel_heads_mla_gqa_ssd_deltanet"


def split_last(t, sizes):
    return jnp.split(t, [int(s) for s in np.cumsum(sizes)[:-1]], axis=-1)


def rms_norm(x, w, eps=EPS):
    x32 = x.astype(jnp.float32)
    y = x32 * lax.rsqrt(jnp.mean(x32 * x32, axis=-1, keepdims=True) + eps)
    return (y * w.astype(jnp.float32)).astype(x.dtype)


def l2_normalize(x, eps=1e-6):
    x32 = x.astype(jnp.float32)
    return (x32 * lax.rsqrt(jnp.sum(x32 * x32, axis=-1, keepdims=True) + eps)).astype(x.dtype)


def dwconv_centred(x, w, b=None):
    K = w.shape[0]
    L = x.shape[1]
    pad = K // 2
    xp = jnp.pad(x, ((0, 0), (pad, pad), (0, 0)))
    y = sum(xp[:, k:k + L] * w[k] for k in range(K))
    return y if b is None else y + b


def axial_rope_tables(seq_len, rot_dim):
    rows = seq_len // GRID_W
    row = jnp.repeat(jnp.arange(rows), GRID_W).astype(jnp.float32)
    col = jnp.tile(jnp.arange(GRID_W), rows).astype(jnp.float32)
    sec = rot_dim // 2
    inv_freq = ROPE_BASE ** (-jnp.arange(0, sec, 2, dtype=jnp.float32) / sec)
    ang_r = row[:, None] * inv_freq
    ang_c = col[:, None] * inv_freq
    ang = jnp.concatenate([ang_r, ang_r, ang_c, ang_c], axis=-1)
    return jnp.cos(ang), jnp.sin(ang)


def apply_axial_rope(x, cos, sin):
    r = x.shape[-1]
    xs = x.reshape(x.shape[:-1] + (2, 2, r // 4))
    rot = jnp.stack([-xs[..., 1, :], xs[..., 0, :]], axis=-2).reshape(x.shape)
    return (x * cos + rot * sin).astype(x.dtype)


def mla_attention(q_nope, q_rope, k_nope, k_rope, v):
    Bsz, L, H, dn = q_nope.shape
    dr = q_rope.shape[-1]
    nb = L // Q_BLOCK
    scale = (dn + dr) ** -0.5
    qn_b = q_nope.reshape(Bsz, nb, Q_BLOCK, H, dn).swapaxes(0, 1)
    qr_b = q_rope.reshape(Bsz, nb, Q_BLOCK, H, dr).swapaxes(0, 1)

    def block(qs):
        qn, qr = qs
        s = jnp.einsum('bqhd,bkhd->bhqk', qn, k_nope) + jnp.einsum('bqhr,bkr->bhqk', qr, k_rope)
        p = jax.nn.softmax(s.astype(jnp.float32) * scale, axis=-1).astype(v.dtype)
        return jnp.einsum('bhqk,bkhd->bqhd', p, v)

    o = lax.map(block, (qn_b, qr_b))
    return o.swapaxes(0, 1).reshape(Bsz, L, H * v.shape[-1])


def gqa_attention(q, k, v):
    Bsz, L, Hq, hd = q.shape
    Hkv = k.shape[2]
    rep = Hq // Hkv
    nb = L // Q_BLOCK
    qb = q.reshape(Bsz, nb, Q_BLOCK, Hkv, rep, hd).swapaxes(0, 1)
    scale = hd ** -0.5

    def block(qi):
        s = jnp.einsum('bqgrd,bkgd->bgrqk', qi, k)
        p = jax.nn.softmax(s.astype(jnp.float32) * scale, axis=-1).astype(v.dtype)
        return jnp.einsum('bgrqk,bkgd->bqgrd', p, v)

    o = lax.map(block, qb)
    return o.swapaxes(0, 1).reshape(Bsz, L, Hq * hd)


def ssd_scan(x, dt, A, Bm, Cm):
    Bsz, L, H, P = x.shape
    N = Bm.shape[-1]
    Q = C_CHUNK
    nc = L // Q
    xc = (x * dt[..., None]).reshape(Bsz, nc, Q, H, P)
    Bc = Bm.reshape(Bsz, nc, Q, H, N)
    Cc = Cm.reshape(Bsz, nc, Q, H, N)
    acum = jnp.cumsum((dt * A).reshape(Bsz, nc, Q, H).transpose(0, 1, 3, 2), axis=-1)
    idx = jnp.arange(Q)
    causal = idx[:, None] >= idx[None, :]
    seg = acum[..., :, None] - acum[..., None, :]
    decay = jnp.exp(jnp.where(causal, seg, -jnp.inf))
    scores = jnp.einsum('bcihn,bcjhn->bchij', Cc, Bc) * decay
    y_diag = jnp.einsum('bchij,bcjhp->bcihp', scores, xc)
    decay_to_end = jnp.exp(acum[..., -1:] - acum)
    states = jnp.einsum('bchj,bcjhn,bcjhp->bchpn', decay_to_end, Bc, xc)
    chunk_decay = jnp.exp(acum[..., -1])

    def step(S, inp):
        st, dec = inp
        return S * dec[..., None, None] + st, S

    S0 = jnp.zeros((Bsz, H, P, N), x.dtype)
    _, S_in = lax.scan(step, S0, (states.swapaxes(0, 1), chunk_decay.swapaxes(0, 1)))
    S_in = S_in.swapaxes(0, 1)
    y_off = jnp.einsum('bcihn,bchpn,bchi->bcihp', Cc, S_in, jnp.exp(acum))
    return (y_diag + y_off).reshape(Bsz, L, H, P)


def gated_delta_rule(q, k, v, g, beta):
    Bsz, L, H, dk = q.shape
    dv = v.shape[-1]
    Q = D_CHUNK
    nc = L // Q

    def chunks(t):
        return t.reshape(Bsz, nc, Q, H, -1).transpose(0, 1, 3, 2, 4)

    qc = chunks(q) * dk ** -0.5
    kc = chunks(k)
    vc = chunks(v)
    bc = beta.reshape(Bsz, nc, Q, H).transpose(0, 1, 3, 2)
    G = jnp.cumsum(g.reshape(Bsz, nc, Q, H).transpose(0, 1, 3, 2), axis=-1)
    idx = jnp.arange(Q)
    lower_strict = idx[:, None] > idx[None, :]
    lower_incl = idx[:, None] >= idx[None, :]
    seg = G[..., :, None] - G[..., None, :]
    decay = jnp.exp(jnp.where(lower_incl, seg, -jnp.inf))
    kb = kc * bc[..., None]
    Lmat = jnp.where(lower_strict, jnp.einsum('bchid,bchjd->bchij', kb, kc) * decay, 0.0)
    eye = jnp.eye(Q, dtype=Lmat.dtype)
    T = lax.linalg.triangular_solve(Lmat + eye, jnp.broadcast_to(eye, Lmat.shape),
                                    left_side=True, lower=True, unit_diagonal=True)
    u_base = jnp.einsum('bchij,bchjd->bchid', T, vc * bc[..., None])
    w = jnp.einsum('bchij,bchjd->bchid', T, kb * jnp.exp(G)[..., None])
    qk = jnp.einsum('bchid,bchjd->bchij', qc, kc) * decay
    q_dec = qc * jnp.exp(G)[..., None]
    k_dec = kc * jnp.exp(G[..., -1:] - G)[..., None]
    g_end = jnp.exp(G[..., -1])

    def step(S, inp):
        u_c, w_c, qk_c, qd_c, kd_c, ge_c = inp
        v_new = u_c - jnp.einsum('bhid,bhdv->bhiv', w_c, S)
        o = jnp.einsum('bhid,bhdv->bhiv', qd_c, S) + jnp.einsum('bhij,bhjv->bhiv', qk_c, v_new)
        S = S * ge_c[..., None, None] + jnp.einsum('bhjd,bhjv->bhdv', kd_c, v_new)
        return S, o

    xs = tuple(t.swapaxes(0, 1) for t in (u_base, w, qk, q_dec, k_dec, g_end))
    S0 = jnp.zeros((Bsz, H, dk, dv), q.dtype)
    _, o = lax.scan(step, S0, xs)
    return o.transpose(1, 0, 3, 2, 4).reshape(Bsz, L, H, dv)


def mla_mixer(p, q_norm, w_uq, kv_norm, w_ukv, out_norm, cos, sin):
    Bsz, L, _ = p.shape
    cq, ckv, kr = split_last(p, [A_Q_LORA, A_KV_LORA, A_ROPE])
    q = (rms_norm(cq, q_norm) @ w_uq).reshape(Bsz, L, A_HEADS, A_NOPE + A_ROPE)
    kv = (rms_norm(ckv, kv_norm) @ w_ukv).reshape(Bsz, L, A_HEADS, A_NOPE + A_VDIM)
    q_nope, q_rope = q[..., :A_NOPE], q[..., A_NOPE:]
    k_nope, v = kv[..., :A_NOPE], kv[..., A_NOPE:]
    q_rope = apply_axial_rope(q_rope, cos[:, None, :], sin[:, None, :])
    k_rope = apply_axial_rope(kr, cos, sin)
    o = mla_attention(q_nope, q_rope, k_nope, k_rope, v)
    return rms_norm(o, out_norm)


def gqa_mixer(p, q_norm, k_norm, out_norm, cos, sin):
    Bsz, L, _ = p.shape
    q, k, v = split_last(p, [B_HEADS * B_HDIM, B_KV_HEADS * B_HDIM, B_KV_HEADS * B_HDIM])
    q = rms_norm(q.reshape(Bsz, L, B_HEADS, B_HDIM), q_norm)
    k = rms_norm(k.reshape(Bsz, L, B_KV_HEADS, B_HDIM), k_norm)
    v = v.reshape(Bsz, L, B_KV_HEADS, B_HDIM)
    q = apply_axial_rope(q, cos[:, None, :], sin[:, None, :])
    k = apply_axial_rope(k, cos[:, None, :], sin[:, None, :])
    return rms_norm(gqa_attention(q, k, v), out_norm)


def mamba2_mixer(p, conv_w, conv_b, a_log, dt_bias, d_skip, out_norm):
    Bsz, L, _ = p.shape
    f32 = jnp.float32
    z, xbc, dt_raw = split_last(p, [GROUP_W, C_XBC, 2 * C_HEADS])
    xbc = jax.nn.silu(dwconv_centred(xbc, conv_w, conv_b))
    xs, Bm, Cm = split_last(xbc, [GROUP_W, C_NGROUPS * C_STATE, C_NGROUPS * C_STATE])
    rep = C_HEADS // C_NGROUPS
    xs = xs.reshape(Bsz, L, C_HEADS, C_HDIM).astype(f32)
    Bm = jnp.repeat(Bm.reshape(Bsz, L, C_NGROUPS, C_STATE), rep, axis=2).astype(f32)
    Cm = jnp.repeat(Cm.reshape(Bsz, L, C_NGROUPS, C_STATE), rep, axis=2).astype(f32)
    dt = jax.nn.softplus((dt_raw.reshape(Bsz, L, 2, C_HEADS) + dt_bias).astype(f32))
    A = -jnp.exp(a_log.astype(f32))
    fl = lambda t: jnp.flip(t, axis=1)
    y_f = ssd_scan(xs, dt[:, :, 0], A[0], Bm, Cm)
    y_b = fl(ssd_scan(fl(xs), fl(dt[:, :, 1]), A[1], fl(Bm), fl(Cm)))
    y = y_f + y_b + xs * d_skip.astype(f32)[:, None]
    y = y.reshape(Bsz, L, GROUP_W).astype(p.dtype)
    return rms_norm(y * jax.nn.silu(z), out_norm)


def deltanet_mixer(p, conv_w, a_log, dt_bias, out_norm):
    Bsz, L, _ = p.shape
    f32 = jnp.float32
    qkv, z, ab = split_last(p, [D_QKV, GROUP_W, 4 * D_HEADS])
    qkv = jax.nn.silu(dwconv_centred(qkv, conv_w))
    q, k, v = split_last(qkv, [GROUP_W, GROUP_W, GROUP_W])
    q = l2_normalize(q.reshape(Bsz, L, D_HEADS, D_HDIM)).astype(f32)
    k = l2_normalize(k.reshape(Bsz, L, D_HEADS, D_HDIM)).astype(f32)
    v = v.reshape(Bsz, L, D_HEADS, D_HDIM).astype(f32)
    ab = ab.reshape(Bsz, L, 4, D_HEADS).astype(f32)
    beta = jax.nn.sigmoid(ab[:, :, 0:2])
    g = -jnp.exp(a_log.astype(f32)) * jax.nn.softplus(ab[:, :, 2:4] + dt_bias.astype(f32))
    fl = lambda t: jnp.flip(t, axis=1)
    o_f = gated_delta_rule(q, k, v, g[:, :, 0], beta[:, :, 0])
    o_b = fl(gated_delta_rule(fl(q), fl(k), fl(v), fl(g[:, :, 1]), fl(beta[:, :, 1])))
    o = rms_norm((o_f + o_b).astype(p.dtype), out_norm)
    o = o * jax.nn.silu(z.reshape(Bsz, L, D_HEADS, D_HDIM))
    return o.reshape(Bsz, L, GROUP_W)


def conv_ffn(h, w_in, conv_w, conv_b, w_out):
    gu = dwconv_centred(h @ w_in, conv_w, conv_b)
    gate, up = split_last(gu, [D_FF, D_FF])
    return (jax.nn.silu(gate) * up) @ w_out


def _fwd_setup_inputs(seed: int = 0) -> dict:
    key = jax.random.key(seed)
    ks = iter(jax.random.split(key, 40))
    nrm = lambda shape, scale: jax.random.normal(next(ks), shape, jnp.float32) * scale
    gain = lambda shape: 1.0 + nrm(shape, 0.05)
    log_a = lambda h: jnp.log(jax.random.uniform(next(ks), (DEPTH, 2, h), jnp.float32, 1.0, 16.0))

    def dt_bias(h):
        dt = jnp.exp(jax.random.uniform(next(ks), (DEPTH, 2, h), jnp.float32, math.log(1e-3), math.log(1e-1)))
        return dt + jnp.log(-jnp.expm1(-dt))

    return {
        "x": nrm((BATCH, SEQ, D_MODEL), 1.0),
        "pre_mix_norm": gain((DEPTH, D_MODEL)),
        "w_in": nrm((DEPTH, D_MODEL, IN_COLS), D_MODEL ** -0.5),
        "a_q_norm": gain((DEPTH, A_Q_LORA)),
        "a_w_uq": nrm((DEPTH, A_Q_LORA, A_HEADS * (A_NOPE + A_ROPE)), A_Q_LORA ** -0.5),
        "a_kv_norm": gain((DEPTH, A_KV_LORA)),
        "a_w_ukv": nrm((DEPTH, A_KV_LORA, A_HEADS * (A_NOPE + A_VDIM)), A_KV_LORA ** -0.5),
        "a_out_norm": gain((DEPTH, GROUP_W)),
        "b_q_norm": gain((DEPTH, B_HDIM)),
        "b_k_norm": gain((DEPTH, B_HDIM)),
        "b_out_norm": gain((DEPTH, GROUP_W)),
        "c_conv_w": nrm((DEPTH, CONV_W, C_XBC), CONV_W ** -0.5),
        "c_conv_b": nrm((DEPTH, C_XBC), 0.02),
        "c_a_log": log_a(C_HEADS),
        "c_dt_bias": dt_bias(C_HEADS),
        "c_d_skip": 1.0 + nrm((DEPTH, C_HEADS), 0.1),
        "c_out_norm": gain((DEPTH, GROUP_W)),
        "d_conv_w": nrm((DEPTH, CONV_W, D_QKV), CONV_W ** -0.5),
        "d_a_log": log_a(D_HEADS),
        "d_dt_bias": dt_bias(D_HEADS),
        "d_out_norm": gain((DEPTH, D_HDIM)),
        "w_out": nrm((DEPTH, MIX_W, D_MODEL), MIX_W ** -0.5),
        "post_mix_norm": gain((DEPTH, D_MODEL)),
        "pre_ffn_norm": gain((DEPTH, D_MODEL)),
        "f_w_in": nrm((DEPTH, D_MODEL, 2 * D_FF), D_MODEL ** -0.5),
        "f_conv_w": nrm((DEPTH, FFN_CONV_W, 2 * D_FF), FFN_CONV_W ** -0.5),
        "f_conv_b": nrm((DEPTH, 2 * D_FF), 0.02),
        "f_w_out": nrm((DEPTH, D_FF, D_MODEL), D_FF ** -0.5),
        "post_ffn_norm": gain((DEPTH, D_MODEL)),
    }


def _fwd_reference(x, pre_mix_norm, w_in, a_q_norm, a_w_uq, a_kv_norm, a_w_ukv, a_out_norm,
              b_q_norm, b_k_norm, b_out_norm, c_conv_w, c_conv_b, c_a_log, c_dt_bias, c_d_skip,
              c_out_norm, d_conv_w, d_a_log, d_dt_bias, d_out_norm, w_out, post_mix_norm,
              pre_ffn_norm, f_w_in, f_conv_w, f_conv_b, f_w_out, post_ffn_norm):
    L = x.shape[1]
    cos_a, sin_a = axial_rope_tables(L, A_ROPE)
    cos_b, sin_b = axial_rope_tables(L, B_HDIM)
    for l in range(DEPTH):
        h = rms_norm(x, pre_mix_norm[l])
        p = h @ w_in[l]
        pa, pb, pc, pd = split_last(p, [A_COLS, B_COLS, C_COLS, D_COLS])
        o_a = mla_mixer(pa, a_q_norm[l], a_w_uq[l], a_kv_norm[l], a_w_ukv[l], a_out_norm[l], cos_a, sin_a)
        o_b = gqa_mixer(pb, b_q_norm[l], b_k_norm[l], b_out_norm[l], cos_b, sin_b)
        o_c = mamba2_mixer(pc, c_conv_w[l], c_conv_b[l], c_a_log[l], c_dt_bias[l], c_d_skip[l], c_out_norm[l])
        o_d = deltanet_mixer(pd, d_conv_w[l], d_a_log[l], d_dt_bias[l], d_out_norm[l])
        o = jnp.concatenate([o_a, o_b, o_c, o_d], axis=-1)
        x = x + rms_norm(o @ w_out[l], post_mix_norm[l])
        h = rms_norm(x, pre_ffn_norm[l])
        x = x + rms_norm(conv_ffn(h, f_w_in[l], f_conv_w[l], f_conv_b[l], f_w_out[l]), post_ffn_norm[l])
    return x


import jax as _jax
import jax.numpy as _jnp

TWIN_FORMAT = 'train_step'
FWD_PARAMS = ['x', 'pre_mix_norm', 'w_in', 'a_q_norm', 'a_w_uq', 'a_kv_norm', 'a_w_ukv', 'a_out_norm', 'b_q_norm', 'b_k_norm', 'b_out_norm', 'c_conv_w', 'c_conv_b', 'c_a_log', 'c_dt_bias', 'c_d_skip', 'c_out_norm', 'd_conv_w', 'd_a_log', 'd_dt_bias', 'd_out_norm', 'w_out', 'post_mix_norm', 'pre_ffn_norm', 'f_w_in', 'f_conv_w', 'f_conv_b', 'f_w_out', 'post_ffn_norm']
TWIN_WEIGHTS = ['pre_mix_norm', 'w_in', 'a_q_norm', 'a_w_uq', 'a_kv_norm', 'a_w_ukv', 'a_out_norm', 'b_q_norm', 'b_k_norm', 'b_out_norm', 'c_conv_w', 'c_conv_b', 'c_a_log', 'c_dt_bias', 'c_d_skip', 'c_out_norm', 'd_conv_w', 'd_a_log', 'd_dt_bias', 'd_out_norm', 'w_out', 'post_mix_norm', 'pre_ffn_norm', 'f_w_in', 'f_conv_w', 'f_conv_b', 'f_w_out', 'post_ffn_norm']
TWIN_DIFF_INPUT = 'x'
TWIN_INPUTS = ['x', 'pre_mix_norm', 'w_in', 'a_q_norm', 'a_w_uq', 'a_kv_norm', 'a_w_ukv', 'a_out_norm', 'b_q_norm', 'b_k_norm', 'b_out_norm', 'c_conv_w', 'c_conv_b', 'c_a_log', 'c_dt_bias', 'c_d_skip', 'c_out_norm', 'd_conv_w', 'd_a_log', 'd_dt_bias', 'd_out_norm', 'w_out', 'post_mix_norm', 'pre_ffn_norm', 'f_w_in', 'f_conv_w', 'f_conv_b', 'f_w_out', 'post_ffn_norm', 'loss_target', 'm_pre_mix_norm', 'm_w_in', 'm_a_q_norm', 'm_a_w_uq', 'm_a_kv_norm', 'm_a_w_ukv', 'm_a_out_norm', 'm_b_q_norm', 'm_b_k_norm', 'm_b_out_norm', 'm_c_conv_w', 'm_c_conv_b', 'm_c_a_log', 'm_c_dt_bias', 'm_c_d_skip', 'm_c_out_norm', 'm_d_conv_w', 'm_d_a_log', 'm_d_dt_bias', 'm_d_out_norm', 'm_w_out', 'm_post_mix_norm', 'm_pre_ffn_norm', 'm_f_w_in', 'm_f_conv_w', 'm_f_conv_b', 'm_f_w_out', 'm_post_ffn_norm', 'v_pre_mix_norm', 'v_w_in', 'v_a_q_norm', 'v_a_w_uq', 'v_a_kv_norm', 'v_a_w_ukv', 'v_a_out_norm', 'v_b_q_norm', 'v_b_k_norm', 'v_b_out_norm', 'v_c_conv_w', 'v_c_conv_b', 'v_c_a_log', 'v_c_dt_bias', 'v_c_d_skip', 'v_c_out_norm', 'v_d_conv_w', 'v_d_a_log', 'v_d_dt_bias', 'v_d_out_norm', 'v_w_out', 'v_post_mix_norm', 'v_pre_ffn_norm', 'v_f_w_in', 'v_f_conv_w', 'v_f_conv_b', 'v_f_w_out', 'v_post_ffn_norm']
TWIN_OUTPUTS = ['loss', 'grad_x', 'grad_pre_mix_norm', 'grad_w_in', 'grad_a_q_norm', 'grad_a_w_uq', 'grad_a_kv_norm', 'grad_a_w_ukv', 'grad_a_out_norm', 'grad_b_q_norm', 'grad_b_k_norm', 'grad_b_out_norm', 'grad_c_conv_w', 'grad_c_conv_b', 'grad_c_a_log', 'grad_c_dt_bias', 'grad_c_d_skip', 'grad_c_out_norm', 'grad_d_conv_w', 'grad_d_a_log', 'grad_d_dt_bias', 'grad_d_out_norm', 'grad_w_out', 'grad_post_mix_norm', 'grad_pre_ffn_norm', 'grad_f_w_in', 'grad_f_conv_w', 'grad_f_conv_b', 'grad_f_w_out', 'grad_post_ffn_norm', 'delta_pre_mix_norm', 'delta_w_in', 'delta_a_q_norm', 'delta_a_w_uq', 'delta_a_kv_norm', 'delta_a_w_ukv', 'delta_a_out_norm', 'delta_b_q_norm', 'delta_b_k_norm', 'delta_b_out_norm', 'delta_c_conv_w', 'delta_c_conv_b', 'delta_c_a_log', 'delta_c_dt_bias', 'delta_c_d_skip', 'delta_c_out_norm', 'delta_d_conv_w', 'delta_d_a_log', 'delta_d_dt_bias', 'delta_d_out_norm', 'delta_w_out', 'delta_post_mix_norm', 'delta_pre_ffn_norm', 'delta_f_w_in', 'delta_f_conv_w', 'delta_f_conv_b', 'delta_f_w_out', 'delta_post_ffn_norm', 'new_m_pre_mix_norm', 'new_m_w_in', 'new_m_a_q_norm', 'new_m_a_w_uq', 'new_m_a_kv_norm', 'new_m_a_w_ukv', 'new_m_a_out_norm', 'new_m_b_q_norm', 'new_m_b_k_norm', 'new_m_b_out_norm', 'new_m_c_conv_w', 'new_m_c_conv_b', 'new_m_c_a_log', 'new_m_c_dt_bias', 'new_m_c_d_skip', 'new_m_c_out_norm', 'new_m_d_conv_w', 'new_m_d_a_log', 'new_m_d_dt_bias', 'new_m_d_out_norm', 'new_m_w_out', 'new_m_post_mix_norm', 'new_m_pre_ffn_norm', 'new_m_f_w_in', 'new_m_f_conv_w', 'new_m_f_conv_b', 'new_m_f_w_out', 'new_m_post_ffn_norm', 'new_v_pre_mix_norm', 'new_v_w_in', 'new_v_a_q_norm', 'new_v_a_w_uq', 'new_v_a_kv_norm', 'new_v_a_w_ukv', 'new_v_a_out_norm', 'new_v_b_q_norm', 'new_v_b_k_norm', 'new_v_b_out_norm', 'new_v_c_conv_w', 'new_v_c_conv_b', 'new_v_c_a_log', 'new_v_c_dt_bias', 'new_v_c_d_skip', 'new_v_c_out_norm', 'new_v_d_conv_w', 'new_v_d_a_log', 'new_v_d_dt_bias', 'new_v_d_out_norm', 'new_v_w_out', 'new_v_post_mix_norm', 'new_v_pre_ffn_norm', 'new_v_f_w_in', 'new_v_f_conv_w', 'new_v_f_conv_b', 'new_v_f_w_out', 'new_v_post_ffn_norm']
TWIN_LEAF_KINDS = {'loss': 'loss', 'grad_x': 'grad_x', 'grad_pre_mix_norm': 'grad_w', 'grad_w_in': 'grad_w', 'grad_a_q_norm': 'grad_w', 'grad_a_w_uq': 'grad_w', 'grad_a_kv_norm': 'grad_w', 'grad_a_w_ukv': 'grad_w', 'grad_a_out_norm': 'grad_w', 'grad_b_q_norm': 'grad_w', 'grad_b_k_norm': 'grad_w', 'grad_b_out_norm': 'grad_w', 'grad_c_conv_w': 'grad_w', 'grad_c_conv_b': 'grad_w', 'grad_c_a_log': 'grad_w', 'grad_c_dt_bias': 'grad_w', 'grad_c_d_skip': 'grad_w', 'grad_c_out_norm': 'grad_w', 'grad_d_conv_w': 'grad_w', 'grad_d_a_log': 'grad_w', 'grad_d_dt_bias': 'grad_w', 'grad_d_out_norm': 'grad_w', 'grad_w_out': 'grad_w', 'grad_post_mix_norm': 'grad_w', 'grad_pre_ffn_norm': 'grad_w', 'grad_f_w_in': 'grad_w', 'grad_f_conv_w': 'grad_w', 'grad_f_conv_b': 'grad_w', 'grad_f_w_out': 'grad_w', 'grad_post_ffn_norm': 'grad_w', 'delta_pre_mix_norm': 'delta_w', 'delta_w_in': 'delta_w', 'delta_a_q_norm': 'delta_w', 'delta_a_w_uq': 'delta_w', 'delta_a_kv_norm': 'delta_w', 'delta_a_w_ukv': 'delta_w', 'delta_a_out_norm': 'delta_w', 'delta_b_q_norm': 'delta_w', 'delta_b_k_norm': 'delta_w', 'delta_b_out_norm': 'delta_w', 'delta_c_conv_w': 'delta_w', 'delta_c_conv_b': 'delta_w', 'delta_c_a_log': 'delta_w', 'delta_c_dt_bias': 'delta_w', 'delta_c_d_skip': 'delta_w', 'delta_c_out_norm': 'delta_w', 'delta_d_conv_w': 'delta_w', 'delta_d_a_log': 'delta_w', 'delta_d_dt_bias': 'delta_w', 'delta_d_out_norm': 'delta_w', 'delta_w_out': 'delta_w', 'delta_post_mix_norm': 'delta_w', 'delta_pre_ffn_norm': 'delta_w', 'delta_f_w_in': 'delta_w', 'delta_f_conv_w': 'delta_w', 'delta_f_conv_b': 'delta_w', 'delta_f_w_out': 'delta_w', 'delta_post_ffn_norm': 'delta_w', 'new_m_pre_mix_norm': 'new_m', 'new_m_w_in': 'new_m', 'new_m_a_q_norm': 'new_m', 'new_m_a_w_uq': 'new_m', 'new_m_a_kv_norm': 'new_m', 'new_m_a_w_ukv': 'new_m', 'new_m_a_out_norm': 'new_m', 'new_m_b_q_norm': 'new_m', 'new_m_b_k_norm': 'new_m', 'new_m_b_out_norm': 'new_m', 'new_m_c_conv_w': 'new_m', 'new_m_c_conv_b': 'new_m', 'new_m_c_a_log': 'new_m', 'new_m_c_dt_bias': 'new_m', 'new_m_c_d_skip': 'new_m', 'new_m_c_out_norm': 'new_m', 'new_m_d_conv_w': 'new_m', 'new_m_d_a_log': 'new_m', 'new_m_d_dt_bias': 'new_m', 'new_m_d_out_norm': 'new_m', 'new_m_w_out': 'new_m', 'new_m_post_mix_norm': 'new_m', 'new_m_pre_ffn_norm': 'new_m', 'new_m_f_w_in': 'new_m', 'new_m_f_conv_w': 'new_m', 'new_m_f_conv_b': 'new_m', 'new_m_f_w_out': 'new_m', 'new_m_post_ffn_norm': 'new_m', 'new_v_pre_mix_norm': 'new_v', 'new_v_w_in': 'new_v', 'new_v_a_q_norm': 'new_v', 'new_v_a_w_uq': 'new_v', 'new_v_a_kv_norm': 'new_v', 'new_v_a_w_ukv': 'new_v', 'new_v_a_out_norm': 'new_v', 'new_v_b_q_norm': 'new_v', 'new_v_b_k_norm': 'new_v', 'new_v_b_out_norm': 'new_v', 'new_v_c_conv_w': 'new_v', 'new_v_c_conv_b': 'new_v', 'new_v_c_a_log': 'new_v', 'new_v_c_dt_bias': 'new_v', 'new_v_c_d_skip': 'new_v', 'new_v_c_out_norm': 'new_v', 'new_v_d_conv_w': 'new_v', 'new_v_d_a_log': 'new_v', 'new_v_d_dt_bias': 'new_v', 'new_v_d_out_norm': 'new_v', 'new_v_w_out': 'new_v', 'new_v_post_mix_norm': 'new_v', 'new_v_pre_ffn_norm': 'new_v', 'new_v_f_w_in': 'new_v', 'new_v_f_conv_w': 'new_v', 'new_v_f_conv_b': 'new_v', 'new_v_f_w_out': 'new_v', 'new_v_post_ffn_norm': 'new_v'}


def _forward(args):
    return _fwd_reference(*[args[k] for k in FWD_PARAMS])


def _output_shape():
    def fwd():
        inp = _fwd_setup_inputs(0)
        return _fwd_reference(*[inp[k] for k in FWD_PARAMS])
    out = _jax.eval_shape(fwd)
    return out.shape, out.dtype

N_MICROBATCH = 1
ADAM_LR = 0.001
ADAM_B1 = 0.9
ADAM_B2 = 0.999
ADAM_EPS = 1e-08
ADAM_WD = 0.01
ADAM_STEP = 10
PER_EXAMPLE_BATCH_AXIS = {'x': 0, 'loss_target': 0}
SHARED_INPUTS = []
_WEIGHT_DTYPES = {'pre_mix_norm': _jnp.float32, 'w_in': _jnp.float32, 'a_q_norm': _jnp.float32, 'a_w_uq': _jnp.float32, 'a_kv_norm': _jnp.float32, 'a_w_ukv': _jnp.float32, 'a_out_norm': _jnp.float32, 'b_q_norm': _jnp.float32, 'b_k_norm': _jnp.float32, 'b_out_norm': _jnp.float32, 'c_conv_w': _jnp.float32, 'c_conv_b': _jnp.float32, 'c_a_log': _jnp.float32, 'c_dt_bias': _jnp.float32, 'c_d_skip': _jnp.float32, 'c_out_norm': _jnp.float32, 'd_conv_w': _jnp.float32, 'd_a_log': _jnp.float32, 'd_dt_bias': _jnp.float32, 'd_out_norm': _jnp.float32, 'w_out': _jnp.float32, 'post_mix_norm': _jnp.float32, 'pre_ffn_norm': _jnp.float32, 'f_w_in': _jnp.float32, 'f_conv_w': _jnp.float32, 'f_conv_b': _jnp.float32, 'f_w_out': _jnp.float32, 'post_ffn_norm': _jnp.float32}
MOMENT_SCALE = {'pre_mix_norm': 6.654326e+00, 'w_in': 4.419561e+00, 'a_q_norm': 2.131286e+00, 'a_w_uq': 1.352979e+00, 'a_kv_norm': 1.068332e+01, 'a_w_ukv': 6.024538e+00, 'a_out_norm': 8.804862e+00, 'b_q_norm': 3.512514e+00, 'b_k_norm': 3.309595e+00, 'b_out_norm': 1.304668e+01, 'c_conv_w': 2.530201e+00, 'c_conv_b': 9.692471e+00, 'c_a_log': 6.895424e+00, 'c_dt_bias': 1.429288e+00, 'c_d_skip': 1.619068e+01, 'c_out_norm': 5.802699e+00, 'd_conv_w': 1.489534e+00, 'd_a_log': 1.971394e+00, 'd_dt_bias': 1.924513e+00, 'd_out_norm': 9.354082e+00, 'w_out': 8.124334e+00, 'post_mix_norm': 6.494323e+01, 'pre_ffn_norm': 3.219141e+00, 'f_w_in': 1.408042e+00, 'f_conv_w': 1.725091e+00, 'f_conv_b': 6.898025e+00, 'f_w_out': 3.077725e+00, 'post_ffn_norm': 6.360323e+01}


def _to_microbatches(a, axis):
    t = _jnp.moveaxis(a, axis, 0)
    t = t.reshape((N_MICROBATCH, t.shape[0] // N_MICROBATCH) + t.shape[1:])
    return _jnp.moveaxis(t, 1, axis + 1)


def setup_inputs(seed: int = 0) -> dict:
    inp = _fwd_setup_inputs(seed)
    key = _jax.random.fold_in(_jax.random.key(seed), 7919)
    shape, _ = _output_shape()
    out = dict(inp)
    out["loss_target"] = _jax.random.normal(_jax.random.fold_in(key, 0), shape, _jnp.float32)
    for i, name in enumerate(TWIN_WEIGHTS):
        w = inp[name].astype(_jnp.float32)
        if MOMENT_SCALE is None:
            s = _jnp.sqrt(_jnp.mean(_jnp.square(w)) + 1e-30)
        else:
            s = MOMENT_SCALE[name]
        km, kv = _jax.random.split(_jax.random.fold_in(key, i + 1))
        out[name] = w
        out["m_" + name] = s * _jax.random.normal(km, w.shape, _jnp.float32)
        out["v_" + name] = (s * s) * _jax.random.uniform(kv, w.shape, _jnp.float32, 0.5, 1.5)
    if N_MICROBATCH > 1:
        for name, axis in PER_EXAMPLE_BATCH_AXIS.items():
            out[name] = _to_microbatches(out[name], axis)
    return {'x': out['x'], 'pre_mix_norm': out['pre_mix_norm'], 'w_in': out['w_in'], 'a_q_norm': out['a_q_norm'], 'a_w_uq': out['a_w_uq'], 'a_kv_norm': out['a_kv_norm'], 'a_w_ukv': out['a_w_ukv'], 'a_out_norm': out['a_out_norm'], 'b_q_norm': out['b_q_norm'], 'b_k_norm': out['b_k_norm'], 'b_out_norm': out['b_out_norm'], 'c_conv_w': out['c_conv_w'], 'c_conv_b': out['c_conv_b'], 'c_a_log': out['c_a_log'], 'c_dt_bias': out['c_dt_bias'], 'c_d_skip': out['c_d_skip'], 'c_out_norm': out['c_out_norm'], 'd_conv_w': out['d_conv_w'], 'd_a_log': out['d_a_log'], 'd_dt_bias': out['d_dt_bias'], 'd_out_norm': out['d_out_norm'], 'w_out': out['w_out'], 'post_mix_norm': out['post_mix_norm'], 'pre_ffn_norm': out['pre_ffn_norm'], 'f_w_in': out['f_w_in'], 'f_conv_w': out['f_conv_w'], 'f_conv_b': out['f_conv_b'], 'f_w_out': out['f_w_out'], 'post_ffn_norm': out['post_ffn_norm'], 'loss_target': out['loss_target'], 'm_pre_mix_norm': out['m_pre_mix_norm'], 'm_w_in': out['m_w_in'], 'm_a_q_norm': out['m_a_q_norm'], 'm_a_w_uq': out['m_a_w_uq'], 'm_a_kv_norm': out['m_a_kv_norm'], 'm_a_w_ukv': out['m_a_w_ukv'], 'm_a_out_norm': out['m_a_out_norm'], 'm_b_q_norm': out['m_b_q_norm'], 'm_b_k_norm': out['m_b_k_norm'], 'm_b_out_norm': out['m_b_out_norm'], 'm_c_conv_w': out['m_c_conv_w'], 'm_c_conv_b': out['m_c_conv_b'], 'm_c_a_log': out['m_c_a_log'], 'm_c_dt_bias': out['m_c_dt_bias'], 'm_c_d_skip': out['m_c_d_skip'], 'm_c_out_norm': out['m_c_out_norm'], 'm_d_conv_w': out['m_d_conv_w'], 'm_d_a_log': out['m_d_a_log'], 'm_d_dt_bias': out['m_d_dt_bias'], 'm_d_out_norm': out['m_d_out_norm'], 'm_w_out': out['m_w_out'], 'm_post_mix_norm': out['m_post_mix_norm'], 'm_pre_ffn_norm': out['m_pre_ffn_norm'], 'm_f_w_in': out['m_f_w_in'], 'm_f_conv_w': out['m_f_conv_w'], 'm_f_conv_b': out['m_f_conv_b'], 'm_f_w_out': out['m_f_w_out'], 'm_post_ffn_norm': out['m_post_ffn_norm'], 'v_pre_mix_norm': out['v_pre_mix_norm'], 'v_w_in': out['v_w_in'], 'v_a_q_norm': out['v_a_q_norm'], 'v_a_w_uq': out['v_a_w_uq'], 'v_a_kv_norm': out['v_a_kv_norm'], 'v_a_w_ukv': out['v_a_w_ukv'], 'v_a_out_norm': out['v_a_out_norm'], 'v_b_q_norm': out['v_b_q_norm'], 'v_b_k_norm': out['v_b_k_norm'], 'v_b_out_norm': out['v_b_out_norm'], 'v_c_conv_w': out['v_c_conv_w'], 'v_c_conv_b': out['v_c_conv_b'], 'v_c_a_log': out['v_c_a_log'], 'v_c_dt_bias': out['v_c_dt_bias'], 'v_c_d_skip': out['v_c_d_skip'], 'v_c_out_norm': out['v_c_out_norm'], 'v_d_conv_w': out['v_d_conv_w'], 'v_d_a_log': out['v_d_a_log'], 'v_d_dt_bias': out['v_d_dt_bias'], 'v_d_out_norm': out['v_d_out_norm'], 'v_w_out': out['v_w_out'], 'v_post_mix_norm': out['v_post_mix_norm'], 'v_pre_ffn_norm': out['v_pre_ffn_norm'], 'v_f_w_in': out['v_f_w_in'], 'v_f_conv_w': out['v_f_conv_w'], 'v_f_conv_b': out['v_f_conv_b'], 'v_f_w_out': out['v_f_w_out'], 'v_post_ffn_norm': out['v_post_ffn_norm']}


def _loss(weights, diff, rest, loss_target):
    with _jax.named_scope("forward"):
        args = {**rest, TWIN_DIFF_INPUT: diff, **{k: w.astype(_WEIGHT_DTYPES[k]) for k, w in weights.items()}}
        y = _forward(args)
    with _jax.named_scope("loss_head"):
        err = _jnp.square(y.astype(_jnp.float32) - loss_target)
        return 0.5 * _jnp.sum(_jnp.mean(err, axis=-1)) if err.ndim else 0.5 * err


def _adamw(w, g, m, v):
    m = ADAM_B1 * m + (1.0 - ADAM_B1) * g
    v = ADAM_B2 * v + (1.0 - ADAM_B2) * _jnp.square(g)
    m_hat = m / (1.0 - ADAM_B1 ** ADAM_STEP)
    v_hat = v / (1.0 - ADAM_B2 ** ADAM_STEP)
    delta = -ADAM_LR * (m_hat / (_jnp.sqrt(v_hat) + ADAM_EPS) + ADAM_WD * w)
    return delta, m, v


def reference(x, pre_mix_norm, w_in, a_q_norm, a_w_uq, a_kv_norm, a_w_ukv, a_out_norm, b_q_norm, b_k_norm, b_out_norm, c_conv_w, c_conv_b, c_a_log, c_dt_bias, c_d_skip, c_out_norm, d_conv_w, d_a_log, d_dt_bias, d_out_norm, w_out, post_mix_norm, pre_ffn_norm, f_w_in, f_conv_w, f_conv_b, f_w_out, post_ffn_norm, loss_target, m_pre_mix_norm, m_w_in, m_a_q_norm, m_a_w_uq, m_a_kv_norm, m_a_w_ukv, m_a_out_norm, m_b_q_norm, m_b_k_norm, m_b_out_norm, m_c_conv_w, m_c_conv_b, m_c_a_log, m_c_dt_bias, m_c_d_skip, m_c_out_norm, m_d_conv_w, m_d_a_log, m_d_dt_bias, m_d_out_norm, m_w_out, m_post_mix_norm, m_pre_ffn_norm, m_f_w_in, m_f_conv_w, m_f_conv_b, m_f_w_out, m_post_ffn_norm, v_pre_mix_norm, v_w_in, v_a_q_norm, v_a_w_uq, v_a_kv_norm, v_a_w_ukv, v_a_out_norm, v_b_q_norm, v_b_k_norm, v_b_out_norm, v_c_conv_w, v_c_conv_b, v_c_a_log, v_c_dt_bias, v_c_d_skip, v_c_out_norm, v_d_conv_w, v_d_a_log, v_d_dt_bias, v_d_out_norm, v_w_out, v_post_mix_norm, v_pre_ffn_norm, v_f_w_in, v_f_conv_w, v_f_conv_b, v_f_w_out, v_post_ffn_norm):
    given = dict(x=x, pre_mix_norm=pre_mix_norm, w_in=w_in, a_q_norm=a_q_norm, a_w_uq=a_w_uq, a_kv_norm=a_kv_norm, a_w_ukv=a_w_ukv, a_out_norm=a_out_norm, b_q_norm=b_q_norm, b_k_norm=b_k_norm, b_out_norm=b_out_norm, c_conv_w=c_conv_w, c_conv_b=c_conv_b, c_a_log=c_a_log, c_dt_bias=c_dt_bias, c_d_skip=c_d_skip, c_out_norm=c_out_norm, d_conv_w=d_conv_w, d_a_log=d_a_log, d_dt_bias=d_dt_bias, d_out_norm=d_out_norm, w_out=w_out, post_mix_norm=post_mix_norm, pre_ffn_norm=pre_ffn_norm, f_w_in=f_w_in, f_conv_w=f_conv_w, f_conv_b=f_conv_b, f_w_out=f_w_out, post_ffn_norm=post_ffn_norm, loss_target=loss_target, m_pre_mix_norm=m_pre_mix_norm, m_w_in=m_w_in, m_a_q_norm=m_a_q_norm, m_a_w_uq=m_a_w_uq, m_a_kv_norm=m_a_kv_norm, m_a_w_ukv=m_a_w_ukv, m_a_out_norm=m_a_out_norm, m_b_q_norm=m_b_q_norm, m_b_k_norm=m_b_k_norm, m_b_out_norm=m_b_out_norm, m_c_conv_w=m_c_conv_w, m_c_conv_b=m_c_conv_b, m_c_a_log=m_c_a_log, m_c_dt_bias=m_c_dt_bias, m_c_d_skip=m_c_d_skip, m_c_out_norm=m_c_out_norm, m_d_conv_w=m_d_conv_w, m_d_a_log=m_d_a_log, m_d_dt_bias=m_d_dt_bias, m_d_out_norm=m_d_out_norm, m_w_out=m_w_out, m_post_mix_norm=m_post_mix_norm, m_pre_ffn_norm=m_pre_ffn_norm, m_f_w_in=m_f_w_in, m_f_conv_w=m_f_conv_w, m_f_conv_b=m_f_conv_b, m_f_w_out=m_f_w_out, m_post_ffn_norm=m_post_ffn_norm, v_pre_mix_norm=v_pre_mix_norm, v_w_in=v_w_in, v_a_q_norm=v_a_q_norm, v_a_w_uq=v_a_w_uq, v_a_kv_norm=v_a_kv_norm, v_a_w_ukv=v_a_w_ukv, v_a_out_norm=v_a_out_norm, v_b_q_norm=v_b_q_norm, v_b_k_norm=v_b_k_norm, v_b_out_norm=v_b_out_norm, v_c_conv_w=v_c_conv_w, v_c_conv_b=v_c_conv_b, v_c_a_log=v_c_a_log, v_c_dt_bias=v_c_dt_bias, v_c_d_skip=v_c_d_skip, v_c_out_norm=v_c_out_norm, v_d_conv_w=v_d_conv_w, v_d_a_log=v_d_a_log, v_d_dt_bias=v_d_dt_bias, v_d_out_norm=v_d_out_norm, v_w_out=v_w_out, v_post_mix_norm=v_post_mix_norm, v_pre_ffn_norm=v_pre_ffn_norm, v_f_w_in=v_f_w_in, v_f_conv_w=v_f_conv_w, v_f_conv_b=v_f_conv_b, v_f_w_out=v_f_w_out, v_post_ffn_norm=v_post_ffn_norm)
    weights = {n: given[n] for n in TWIN_WEIGHTS}
    shared = {n: given[n] for n in SHARED_INPUTS}
    per_example = {n: given[n] for n in ['x']}
    grad_fn = _jax.value_and_grad(_loss, argnums=(0, 1))

    def one_microbatch(ex, loss_target):
        ex = dict(ex)
        diff = ex.pop(TWIN_DIFF_INPUT)
        return grad_fn(weights, diff, {**shared, **ex}, loss_target)

    if N_MICROBATCH == 1:
        loss, (grad_w, grad_x) = one_microbatch(per_example, given["loss_target"])
    else:
        def body(carry, xs):
            loss_sum, grad_sum = carry
            l_k, (gw_k, gx_k) = one_microbatch(xs[0], xs[1])
            with _jax.named_scope("update"):
                return (loss_sum + l_k, _jax.tree.map(_jnp.add, grad_sum, gw_k)), gx_k

        init = (_jnp.zeros((), _jnp.float32), _jax.tree.map(_jnp.zeros_like, weights))
        (loss, grad_w), grad_x = _jax.lax.scan(body, init, (per_example, given["loss_target"]))
    with _jax.named_scope("update"):
        delta_w, new_m, new_v = {}, {}, {}
        for n in TWIN_WEIGHTS:
            delta_w[n], new_m[n], new_v[n] = _adamw(weights[n], grad_w[n], given["m_" + n], given["v_" + n])
    return (loss, grad_x, *[grad_w[n] for n in TWIN_WEIGHTS], *[delta_w[n] for n in TWIN_WEIGHTS],
            *[new_m[n] for n in TWIN_WEIGHTS], *[new_v[n] for n in TWIN_WEIGHTS])
```

```python
import functools
import math

import numpy as np
import jax
import jax.numpy as jnp
from jax import lax
from jax.experimental import pallas as pl
from jax.experimental.pallas import tpu as pltpu

F32 = jnp.float32
BF16 = jnp.bfloat16
MESH = pl.DeviceIdType.MESH
VMEM_LIMIT = 48 * 1024 * 1024
LANE = 128

D_MODEL = 1024
DEPTH = 2
GRID_W = 64
ROPE_BASE = 10000.0
EPS = 1e-6
GROUP_W = 256
HEADS = 4
HD = 64
A_NOPE, A_ROPE, A_Q_LORA, A_KV_LORA = 64, 32, 192, 128
A_COLS = A_Q_LORA + A_KV_LORA + A_ROPE
B_COLS = 512
C_XBC = 512
C_COLS = GROUP_W + C_XBC + 8
D_QKV = 768
D_COLS = D_QKV + GROUP_W + 16
IN_COLS = A_COLS + B_COLS + C_COLS + D_COLS
C_CHUNK = 128
D_CHUNK = 64
D_FF = 2816
ADAM_LR, ADAM_B1, ADAM_B2, ADAM_EPS, ADAM_WD, ADAM_STEP = 0.001, 0.9, 0.999, 1e-08, 0.01, 10

WEIGHTS = ['pre_mix_norm', 'w_in', 'a_q_norm', 'a_w_uq', 'a_kv_norm', 'a_w_ukv', 'a_out_norm', 'b_q_norm',
           'b_k_norm', 'b_out_norm', 'c_conv_w', 'c_conv_b', 'c_a_log', 'c_dt_bias', 'c_d_skip', 'c_out_norm',
           'd_conv_w', 'd_a_log', 'd_dt_bias', 'd_out_norm', 'w_out', 'post_mix_norm', 'pre_ffn_norm', 'f_w_in',
           'f_conv_w', 'f_conv_b', 'f_w_out', 'post_ffn_norm']
SHARD_AXIS = {'w_in': 2, 'a_w_uq': 2, 'a_w_ukv': 2, 'c_conv_w': 2, 'd_conv_w': 2, 'w_out': 1, 'f_w_in': 2,
              'f_conv_w': 2, 'f_w_out': 1}
SHARDED = [n for n in WEIGHTS if n in SHARD_AXIS]
SMALL = [n for n in WEIGHTS if n not in SHARD_AXIS]

P_LAYOUT = [('b_q', 0, 512), ('c_xbc', 512, 512), ('a_cq', 1024, 256), ('b_k', 1280, 256), ('d_qkv', 1536, 768),
            ('b_v', 2304, 256), ('c_z', 2560, 256), ('d_z', 2816, 256), ('a_ckv', 3072, 128), ('a_kr', 3200, 128),
            ('c_dt', 3328, 128), ('d_b', 3456, 128), ('d_a', 3584, 128), ('pad', 3712, 128)]
P_OFF = {n: (o, w) for n, o, w in P_LAYOUT}
P_COLS = 3840
O_COLS = 1536


def _cparams(sem):
    return pltpu.CompilerParams(dimension_semantics=sem, vmem_limit_bytes=VMEM_LIMIT)


def _tile(n, target):
    best = None
    for d in range(LANE, min(n, target) + 1, LANE):
        if n % d == 0:
            best = d
    return best if best is not None else n


_NN = ((1,), (0,))
_NT = ((1,), (1,))
_TN = ((0,), (0,))


def _raw_dot(a, b, dims, hi):
    if hi:
        return lax.dot_general(a, b, (dims, ((), ())), precision=lax.Precision.HIGHEST,
                               preferred_element_type=F32)
    return lax.dot_general(a.astype(BF16), b.astype(BF16), (dims, ((), ())), preferred_element_type=F32)


def _make_dots(hi):
    @jax.custom_vjp
    def nn(a, b):
        return _raw_dot(a, b, _NN, hi)

    @jax.custom_vjp
    def nt(a, b):
        return _raw_dot(a, b, _NT, hi)

    @jax.custom_vjp
    def tn(a, b):
        return _raw_dot(a, b, _TN, hi)

    nn.defvjp(lambda a, b: (nn(a, b), (a, b)), lambda r, g: (nt(g, r[1]), tn(r[0], g)))
    nt.defvjp(lambda a, b: (nt(a, b), (a, b)), lambda r, g: (nn(g, r[1]), tn(g, r[0])))
    tn.defvjp(lambda a, b: (tn(a, b), (a, b)), lambda r, g: (nt(r[1], g), nn(r[0], g)))
    return nn, nt, tn


_nn, _nt, _tn = _make_dots(False)
_nn_hi, _nt_hi, _tn_hi = _make_dots(True)


def _sigmoid(x):
    return 1.0 / (1.0 + jnp.exp(-x))


def _silu(x):
    return x * _sigmoid(x)


def _softplus(x):
    return jnp.maximum(x, 0.0) + jnp.log(1.0 + jnp.exp(-jnp.abs(x)))


def _rms(x, w, n=None):
    n = x.shape[-1] if n is None else n
    ms = jnp.sum(x * x, axis=-1, keepdims=True) * (1.0 / n)
    return x * lax.rsqrt(ms + EPS) * w


def _spec2(T, w, cb=0):
    return pl.BlockSpec((T, w), lambda i: (i, cb))


def _spec3(T, w, lead, cb=0):
    return pl.BlockSpec((None, T, w), lambda i: (lead, i, cb))


def _full_spec(a):
    nd = a.ndim
    return pl.BlockSpec(a.shape, lambda i: (0,) * nd)


def _tw_fwd(name, fn, acts, params, outs, L, T):
    na, npar = len(acts), len(params)

    def kern(*refs):
        a = [r[...].astype(F32) for r in refs[:na]]
        p = [r[...].astype(F32) for r in refs[na:na + npar]]
        res = fn(a, p)
        for r, o in zip(refs[na + npar:], res):
            r[...] = o.astype(r.dtype)

    return pl.pallas_call(
        kern, name=name, grid=(L // T,),
        in_specs=[s for _, s in acts] + [_full_spec(p) for p in params],
        out_specs=[_spec2(T, w) for w, _ in outs],
        out_shape=[jax.ShapeDtypeStruct((L, w), dt) for w, dt in outs],
        compiler_params=_cparams(("arbitrary",)),
    )(*[a for a, _ in acts], *params)


def _tw_bwd(name, fn, acts, params, douts, L, T, act_grad, par_grad, addto=None):
    na, npar, nd = len(acts), len(params), len(douts)
    addto = addto or {}
    add_keys = sorted(addto)
    ga = [k for k in range(na) if act_grad[k]]
    gp = [k for k in range(npar) if par_grad[k]]

    def kern(*refs):
        i = pl.program_id(0)
        a = [r[...].astype(F32) for r in refs[:na]]
        p = [r[...].astype(F32) for r in refs[na:na + npar]]
        g = [r[...].astype(F32) for r in refs[na + npar:na + npar + nd]]
        pos = na + npar + nd
        adds = [r[...].astype(F32) for r in refs[pos:pos + len(add_keys)]]
        pos += len(add_keys)
        da_refs = refs[pos:pos + len(ga)]
        dp_refs = refs[pos + len(ga):]

        def f(ad, pd):
            af, pf = list(a), list(p)
            for k, v in zip(ga, ad):
                af[k] = v
            for k, v in zip(gp, pd):
                pf[k] = v
            return fn(af, pf)

        _, vjp = jax.vjp(f, [a[k] for k in ga], [p[k] for k in gp])
        dad, dpd = vjp(list(g))
        for n, (r, d) in enumerate(zip(da_refs, dad)):
            if n in addto:
                d = d + adds[add_keys.index(n)]
            r[...] = d.astype(r.dtype)

        @pl.when(i == 0)
        def _():
            for r in dp_refs:
                r[...] = jnp.zeros(r.shape, F32)

        for r, d in zip(dp_refs, dpd):
            r[...] += d

    def width(spec):
        return spec.block_shape[-1]

    res = pl.pallas_call(
        kern, name=name, grid=(L // T,),
        in_specs=[s for _, s in acts] + [_full_spec(p) for p in params] + [s for _, s in douts]
        + [addto[k][1] for k in add_keys],
        out_specs=[_spec2(T, width(acts[k][1])) for k in ga] + [_full_spec(params[k]) for k in gp],
        out_shape=[jax.ShapeDtypeStruct((L, width(acts[k][1])), F32) for k in ga]
        + [jax.ShapeDtypeStruct(params[k].shape, F32) for k in gp],
        compiler_params=_cparams(("arbitrary",)),
    )(*[a for a, _ in acts], *params, *[a for a, _ in douts], *[addto[k][0] for k in add_keys])
    return list(res[:len(ga)]), list(res[len(ga):])


def _mm(name, a, b, mode, out_dtype, tm, tn, tk):
    if mode == 'nn':
        (M, K), N = a.shape, b.shape[1]
    elif mode == 'nt':
        (M, K), N = a.shape, b.shape[0]
    else:
        (K, M), N = a.shape, b.shape[1]
    tm, tn, tk = _tile(M, tm), _tile(N, tn), _tile(K, tk)
    nk = K // tk
    if mode == 'nn':
        a_spec = pl.BlockSpec((tm, tk), lambda i, j, k: (i, k))
        b_spec = pl.BlockSpec((tk, tn), lambda i, j, k: (k, j))
        dims = _NN
    elif mode == 'nt':
        a_spec = pl.BlockSpec((tm, tk), lambda i, j, k: (i, k))
        b_spec = pl.BlockSpec((tn, tk), lambda i, j, k: (j, k))
        dims = _NT
    else:
        a_spec = pl.BlockSpec((tk, tm), lambda i, j, k: (k, i))
        b_spec = pl.BlockSpec((tk, tn), lambda i, j, k: (k, j))
        dims = _TN

    def kern(a_ref, b_ref, o_ref, acc):
        k = pl.program_id(2)

        @pl.when(k == 0)
        def _():
            acc[...] = jnp.zeros(acc.shape, F32)

        acc[...] += lax.dot_general(a_ref[...].astype(BF16), b_ref[...].astype(BF16), (dims, ((), ())),
                                    preferred_element_type=F32)

        @pl.when(k == nk - 1)
        def _():
            o_ref[...] = acc[...].astype(o_ref.dtype)

    return pl.pallas_call(
        kern, name=name, grid=(M // tm, N // tn, nk),
        in_specs=[a_spec, b_spec],
        out_specs=pl.BlockSpec((tm, tn), lambda i, j, k: (i, j)),
        out_shape=jax.ShapeDtypeStruct((M, N), out_dtype),
        scratch_shapes=[pltpu.VMEM((tm, tn), F32)],
        compiler_params=_cparams(("arbitrary", "arbitrary", "arbitrary")),
    )(a, b)


def _flash_fwd(name, q, k, v, H, rep, scale, L):
    tq = min(256, L)
    nq = L // tq
    KC = min(2048, L)
    nkc = L // KC

    def kern(q_ref, k_ref, v_ref, o_ref, lse_ref):
        qb = q_ref[...]
        m = jnp.full((tq, 1), -1e30, F32)
        l = jnp.zeros((tq, 1), F32)
        acc = jnp.zeros((tq, LANE), F32)
        for c in range(nkc):
            kb = k_ref[c * KC:(c + 1) * KC, :]
            vb = v_ref[c * KC:(c + 1) * KC, :]
            s = lax.dot_general(qb, kb, (_NT, ((), ())), preferred_element_type=F32) * scale
            mn = jnp.maximum(m, jnp.max(s, axis=-1, keepdims=True))
            al = jnp.exp(m - mn)
            p = jnp.exp(s - mn)
            l = al * l + jnp.sum(p, axis=-1, keepdims=True)
            acc = al * acc + lax.dot_general(p.astype(BF16), vb, (_NN, ((), ())), preferred_element_type=F32)
            m = mn
        o_ref[...] = acc / l
        lse_ref[...] = m + jnp.log(l)

    return pl.pallas_call(
        kern, name=name, grid=(H, nq),
        in_specs=[pl.BlockSpec((tq, LANE), lambda h, i: (i, h)),
                  pl.BlockSpec((L, LANE), lambda h, i: (0, h // rep)),
                  pl.BlockSpec((L, LANE), lambda h, i: (0, h // rep))],
        out_specs=[pl.BlockSpec((tq, LANE), lambda h, i: (i, h)),
                   pl.BlockSpec((tq, 1), lambda h, i: (h * nq + i, 0))],
        out_shape=[jax.ShapeDtypeStruct((L, H * LANE), F32), jax.ShapeDtypeStruct((H * L, 1), F32)],
        compiler_params=_cparams(("arbitrary", "arbitrary")),
    )(q, k, v)


def _flash_bwd(name, q, k, v, o, lse, do, H, rep, scale, L):
    tq = min(256, L)
    nq = L // tq
    KC = min(2048, L)
    nkc = L // KC
    Hkv = H // rep

    def kern(q_ref, k_ref, v_ref, o_ref, lse_ref, do_ref, dq_ref, dk_ref, dv_ref):
        h = pl.program_id(0)
        i = pl.program_id(1)

        @pl.when((i == 0) & (h % rep == 0))
        def _():
            dk_ref[...] = jnp.zeros(dk_ref.shape, F32)
            dv_ref[...] = jnp.zeros(dv_ref.shape, F32)

        qb = q_ref[...]
        do = do_ref[...]
        dob = do.astype(BF16)
        delta = jnp.sum(do * o_ref[...], axis=-1, keepdims=True)
        lse = lse_ref[...]
        dq = jnp.zeros((tq, LANE), F32)
        for c in range(nkc):
            sl = slice(c * KC, (c + 1) * KC)
            kb = k_ref[sl, :]
            vb = v_ref[sl, :]
            s = lax.dot_general(qb, kb, (_NT, ((), ())), preferred_element_type=F32) * scale
            p = jnp.exp(s - lse)
            dp = lax.dot_general(dob, vb, (_NT, ((), ())), preferred_element_type=F32)
            ds = (p * (dp - delta) * scale).astype(BF16)
            dq = dq + lax.dot_general(ds, kb, (_NN, ((), ())), preferred_element_type=F32)
            dk_ref[sl, :] += lax.dot_general(ds, qb, (_TN, ((), ())), preferred_element_type=F32)
            dv_ref[sl, :] += lax.dot_general(p.astype(BF16), dob, (_TN, ((), ())), preferred_element_type=F32)
        dq_ref[...] = dq

    return pl.pallas_call(
        kern, name=name, grid=(H, nq),
        in_specs=[pl.BlockSpec((tq, LANE), lambda h, i: (i, h)),
                  pl.BlockSpec((L, LANE), lambda h, i: (0, h // rep)),
                  pl.BlockSpec((L, LANE), lambda h, i: (0, h // rep)),
                  pl.BlockSpec((tq, LANE), lambda h, i: (i, h)),
                  pl.BlockSpec((tq, 1), lambda h, i: (h * nq + i, 0)),
                  pl.BlockSpec((tq, LANE), lambda h, i: (i, h))],
        out_specs=[pl.BlockSpec((tq, LANE), lambda h, i: (i, h)),
                   pl.BlockSpec((L, LANE), lambda h, i: (0, h // rep)),
                   pl.BlockSpec((L, LANE), lambda h, i: (0, h // rep))],
        out_shape=[jax.ShapeDtypeStruct((L, H * LANE), F32), jax.ShapeDtypeStruct((L, Hkv * LANE), F32),
                   jax.ShapeDtypeStruct((L, Hkv * LANE), F32)],
        compiler_params=_cparams(("arbitrary", "arbitrary")),
    )(q, k, v, o, lse, do)


def _shift_dn(x, first_row):
    row = lax.broadcasted_iota(jnp.int32, x.shape, 0)
    return jnp.where(row == 0, first_row, pltpu.roll(x, 1, 0))


def _shift_up(x, last_row):
    n = x.shape[0]
    row = lax.broadcasted_iota(jnp.int32, x.shape, 0)
    return jnp.where(row == n - 1, last_row, pltpu.roll(x, n - 1, 0))


def _halo_specs(ndim, lead, T, tc, cb0, L):
    r8 = T // 8
    last8 = L // 8 - 1
    if ndim == 2:
        return [pl.BlockSpec((T, tc), lambda j, i: (i, cb0 + j)),
                pl.BlockSpec((8, tc), lambda j, i: (jnp.maximum(i * r8 - 1, 0), cb0 + j)),
                pl.BlockSpec((8, tc), lambda j, i: (jnp.minimum((i + 1) * r8, last8), cb0 + j))]
    return [pl.BlockSpec((None, T, tc), lambda j, i: (lead, i, cb0 + j)),
            pl.BlockSpec((None, 8, tc), lambda j, i: (lead, jnp.maximum(i * r8 - 1, 0), cb0 + j)),
            pl.BlockSpec((None, 8, tc), lambda j, i: (lead, jnp.minimum((i + 1) * r8, last8), cb0 + j))]


def _conv_fwd(name, x, col0, C, w8, act, L, tc):
    T = min(256, L)
    nt = L // T
    cb0 = col0 // tc

    def kern(x_ref, xp_ref, xn_ref, w_ref, o_ref):
        i = pl.program_id(1)
        x = x_ref[...]
        w = w_ref[...]
        pr = jnp.where(i == 0, 0.0, xp_ref[7:8, :])
        nr = jnp.where(i == nt - 1, 0.0, xn_ref[0:1, :])
        pre = _shift_dn(x, pr) * w[0:1] + x * w[1:2] + _shift_up(x, nr) * w[2:3] + w[3:4]
        o_ref[...] = _silu(pre) if act else pre

    return pl.pallas_call(
        kern, name=name, grid=(C // tc, nt),
        in_specs=_halo_specs(2, None, T, tc, cb0, L) + [pl.BlockSpec((8, tc), lambda j, i: (0, j))],
        out_specs=pl.BlockSpec((T, tc), lambda j, i: (i, j)),
        out_shape=jax.ShapeDtypeStruct((L, C), F32),
        compiler_params=_cparams(("arbitrary", "arbitrary")),
    )(x, x, x, w8)


def _conv_bwd(name, x, col0, C, w8, act, gs, L, tc):
    T = min(256, L)
    nt = L // T
    cb0 = col0 // tc
    ng = len(gs)

    def dact(pre, g):
        if not act:
            return g
        s = _sigmoid(pre)
        return g * (s * (1.0 + pre * (1.0 - s)))

    def kern(*refs):
        x_ref, xp_ref, xn_ref, w_ref = refs[:4]
        g_refs = refs[4:4 + 3 * ng]
        dx_ref, dw_ref = refs[4 + 3 * ng:]
        i = pl.program_id(1)
        first = i == 0
        last = i == nt - 1
        x = x_ref[...]
        w = w_ref[...]
        w0, w1, w2, b = w[0:1], w[1:2], w[2:3], w[3:4]
        g = g_refs[0][...]
        gp = g_refs[1][7:8, :]
        gn = g_refs[2][0:1, :]
        for n in range(1, ng):
            g = g + g_refs[3 * n][...]
            gp = gp + g_refs[3 * n + 1][7:8, :]
            gn = gn + g_refs[3 * n + 2][0:1, :]
        pr = jnp.where(first, 0.0, xp_ref[7:8, :])
        pr2 = jnp.where(first, 0.0, xp_ref[6:7, :])
        nr = jnp.where(last, 0.0, xn_ref[0:1, :])
        nr2 = jnp.where(last, 0.0, xn_ref[1:2, :])
        xm1 = _shift_dn(x, pr)
        xp1 = _shift_up(x, nr)
        pre = xm1 * w0 + x * w1 + xp1 * w2 + b
        dpre = dact(pre, g)
        pre_m1 = pr2 * w0 + pr * w1 + x[0:1] * w2 + b
        dpre_m1 = jnp.where(first, 0.0, dact(pre_m1, gp))
        pre_T = x[T - 1:T] * w0 + nr * w1 + nr2 * w2 + b
        dpre_T = jnp.where(last, 0.0, dact(pre_T, gn))
        dx_ref[...] = _shift_up(dpre, dpre_T) * w0 + dpre * w1 + _shift_dn(dpre, dpre_m1) * w2
        row = lax.broadcasted_iota(jnp.int32, (8, tc), 0)
        dw = (jnp.where(row == 0, jnp.sum(dpre * xm1, axis=0, keepdims=True), 0.0)
              + jnp.where(row == 1, jnp.sum(dpre * x, axis=0, keepdims=True), 0.0)
              + jnp.where(row == 2, jnp.sum(dpre * xp1, axis=0, keepdims=True), 0.0)
              + jnp.where(row == 3, jnp.sum(dpre, axis=0, keepdims=True), 0.0))

        @pl.when(first)
        def _():
            dw_ref[...] = jnp.zeros((8, tc), F32)

        dw_ref[...] += dw

    g_specs, g_args = [], []
    for arr, lead in gs:
        g_specs += _halo_specs(arr.ndim, lead, T, tc, 0, L)
        g_args += [arr, arr, arr]
    return pl.pallas_call(
        kern, name=name, grid=(C // tc, nt),
        in_specs=_halo_specs(2, None, T, tc, cb0, L) + [pl.BlockSpec((8, tc), lambda j, i: (0, j))] + g_specs,
        out_specs=[pl.BlockSpec((T, tc), lambda j, i: (i, j)), pl.BlockSpec((8, tc), lambda j, i: (0, j))],
        out_shape=[jax.ShapeDtypeStruct((L, C), F32), jax.ShapeDtypeStruct((8, C), F32)],
        compiler_params=_cparams(("arbitrary", "arbitrary")),
    )(x, x, x, w8, *g_args)


def _glu_fwd(name, gu, L):
    T = min(256, L)
    tc = 1408
    ncb = D_FF // tc

    def kern(g_ref, u_ref, o_ref):
        o_ref[...] = (_silu(g_ref[...]) * u_ref[...]).astype(BF16)

    return pl.pallas_call(
        kern, name=name, grid=(L // T, ncb),
        in_specs=[pl.BlockSpec((T, tc), lambda i, j: (i, j)), pl.BlockSpec((T, tc), lambda i, j: (i, j + ncb))],
        out_specs=pl.BlockSpec((T, tc), lambda i, j: (i, j)),
        out_shape=jax.ShapeDtypeStruct((L, D_FF), BF16),
        compiler_params=_cparams(("arbitrary", "arbitrary")),
    )(gu, gu)


def _glu_bwd(name, gu, da, L):
    T = min(256, L)
    tc = 1408
    ncb = D_FF // tc

    def kern(g_ref, u_ref, da_ref, o_ref):
        half = pl.program_id(1)
        g = g_ref[...]
        s = _sigmoid(g)
        d = da_ref[...]
        dg = d * u_ref[...] * (s * (1.0 + g * (1.0 - s)))
        du = d * (g * s)
        o_ref[...] = jnp.where(half == 0, dg, du)

    return pl.pallas_call(
        kern, name=name, grid=(L // T, 2, ncb),
        in_specs=[pl.BlockSpec((T, tc), lambda i, h, j: (i, j)),
                  pl.BlockSpec((T, tc), lambda i, h, j: (i, j + ncb)),
                  pl.BlockSpec((T, tc), lambda i, h, j: (i, j))],
        out_specs=pl.BlockSpec((T, tc), lambda i, h, j: (i, h * ncb + j)),
        out_shape=jax.ShapeDtypeStruct((L, 2 * D_FF), F32),
        compiler_params=_cparams(("arbitrary", "arbitrary", "arbitrary")),
    )(gu, gu, da)


def _masks(Q, rev):
    ri = lax.broadcasted_iota(jnp.int32, (Q, Q), 0)
    ci = lax.broadcasted_iota(jnp.int32, (Q, Q), 1)
    diff = (ri - ci) * (1 - 2 * rev)
    return diff >= 0, diff > 0


def _lane_pick(v, sel):
    return jnp.sum(v * sel, axis=-1, keepdims=True)


def _ssd_chunk(S, x, B, C, dtraw, alog, dtb, rev):
    Q = dtraw.shape[0]
    incl, _ = _masks(Q, rev)
    tri = incl.astype(F32)
    dt = _softplus(dtraw + dtb)
    a_all = dt * (-jnp.exp(alog))
    acum_all = _nn_hi(tri, a_all)
    total_all = jnp.sum(a_all, axis=0, keepdims=True)
    lane = lax.broadcasted_iota(jnp.int32, (1, LANE), 1)
    ones = jnp.ones((Q, LANE), F32)
    ys, Sn = [], []
    for h in range(HEADS):
        g = h // 2
        sel = (lane == rev * 4 + h).astype(F32)
        acum = _lane_pick(acum_all, sel)
        dth = _lane_pick(dt, sel)
        tot = _lane_pick(total_all, sel)
        seg = acum - _nt_hi(ones, acum_all * sel)
        decay = jnp.exp(jnp.where(incl, seg, -1e30))
        xdt = x[h] * dth
        scores = _nt(C[g], B[g]) * decay
        y_diag = _nn(scores, xdt)
        states = _tn(xdt, B[g] * jnp.exp(tot - acum))
        y_off = _nt(C[g], S[h]) * jnp.exp(acum)
        ys.append(y_diag + y_off)
        Sn.append(S[h] * jnp.exp(tot) + states)
    return ys, Sn


def _inv_unit_raw(Lm):
    Q = Lm.shape[0]
    ri = lax.broadcasted_iota(jnp.int32, (Q, Q), 0)
    ci = lax.broadcasted_iota(jnp.int32, (Q, Q), 1)
    X = (ri == ci).astype(F32) - Lm
    P = _raw_dot(Lm, Lm, _NN, True)
    n = 2
    while n < Q:
        X = X + _raw_dot(X, P, _NN, True)
        n *= 2
        if n < Q:
            P = _raw_dot(P, P, _NN, True)
    return X


@jax.custom_vjp
def _inv_unit(Lm):
    return _inv_unit_raw(Lm)


def _inv_unit_f(Lm):
    T = _inv_unit_raw(Lm)
    return T, T


def _inv_unit_b(T, g):
    return (-_raw_dot(_raw_dot(T, g, _TN, True), T, _NT, True),)


_inv_unit.defvjp(_inv_unit_f, _inv_unit_b)


def _delta_chunk(S, q, k, v, braw, araw, alog, dtb, rev):
    Q = braw.shape[0]
    incl, strict = _masks(Q, rev)
    tri = incl.astype(F32)
    beta_all = _sigmoid(braw)
    g_all = -jnp.exp(alog) * _softplus(araw + dtb)
    G_all = _nn_hi(tri, g_all)
    Gtot_all = jnp.sum(g_all, axis=0, keepdims=True)
    lane = lax.broadcasted_iota(jnp.int32, (1, LANE), 1)
    ones = jnp.ones((Q, LANE), F32)
    os_, Sn = [], []
    for h in range(HEADS):
        sel = (lane == rev * 4 + h).astype(F32)
        G = _lane_pick(G_all, sel)
        bt = _lane_pick(beta_all, sel)
        Gtot = _lane_pick(Gtot_all, sel)
        seg = G - _nt_hi(ones, G_all * sel)
        decay = jnp.exp(jnp.where(incl, seg, -1e30))
        qn = q[h] * lax.rsqrt(jnp.sum(q[h] * q[h], axis=-1, keepdims=True) + 1e-6)
        kn = k[h] * lax.rsqrt(jnp.sum(k[h] * k[h], axis=-1, keepdims=True) + 1e-6)
        qc = qn * (HD ** -0.5)
        kb = kn * bt
        Lm = jnp.where(strict, _nt(kb, kn) * decay, 0.0)
        T = _inv_unit(Lm)
        eG = jnp.exp(G)
        u = _nn(T, v[h] * bt)
        w = _nn(T, kb * eG)
        qk = _nt(qc, kn) * decay
        v_new = u - _nn(w, S[h])
        os_.append(_nn(qc * eG, S[h]) + _nn(qk, v_new))
        Sn.append(S[h] * jnp.exp(Gtot) + _tn(kn * jnp.exp(Gtot - G), v_new))
    return os_, Sn


def _scan_fwd(name, chunk_fn, seqs, rows, Q, L):
    nc = L // Q
    ns = len(seqs)

    def cidx(d, i):
        return jnp.where(d == 0, i, nc - 1 - i)

    def kern(*refs):
        s_refs = refs[:ns]
        r_refs = refs[ns:ns + len(rows)]
        y_ref, ss_ref, S_scr = refs[ns + len(rows):]
        d = pl.program_id(0)
        i = pl.program_id(1)

        @pl.when(i == 0)
        def _():
            S_scr[...] = jnp.zeros(S_scr.shape, F32)

        ss_ref[...] = S_scr[...]
        S = [S_scr[HD * h:HD * (h + 1), :] for h in range(HEADS)]
        ins = []
        for r, (_, w, _, n) in zip(s_refs, seqs):
            pw = w // n
            ins.append([r[:, pw * t:pw * (t + 1)] for t in range(n)] if n > 1 else r[...])
        ys, Sn = chunk_fn(S, *ins, *[r[...] for r in r_refs], d)
        for h in range(HEADS):
            y_ref[:, HD * h:HD * (h + 1)] = ys[h]
            S_scr[HD * h:HD * (h + 1), :] = Sn[h]

    return pl.pallas_call(
        kern, name=name, grid=(2, nc),
        in_specs=[pl.BlockSpec((Q, w), functools.partial(lambda d, i, cb: (cidx(d, i), cb), cb=cb))
                  for _, w, cb, _ in seqs] + [pl.BlockSpec((1, LANE), lambda d, i: (0, 0)) for _ in rows],
        out_specs=[pl.BlockSpec((None, Q, GROUP_W), lambda d, i: (d, cidx(d, i), 0)),
                   pl.BlockSpec((None, None, GROUP_W, HD), lambda d, i: (d, cidx(d, i), 0, 0))],
        out_shape=[jax.ShapeDtypeStruct((2, L, GROUP_W), F32), jax.ShapeDtypeStruct((2, nc, GROUP_W, HD), F32)],
        scratch_shapes=[pltpu.VMEM((GROUP_W, HD), F32)],
        compiler_params=_cparams(("arbitrary", "arbitrary")),
    )(*[a for a, _, _, _ in seqs], *rows)


def _scan_bwd(name, chunk_fn, seqs, rows, ssave, dy, extra, Q, L):
    nc = L // Q
    ns, nr = len(seqs), len(rows)
    has_extra = extra is not None

    def cidx(d, i):
        pos = nc - 1 - i
        return jnp.where(d == 0, pos, nc - 1 - pos)

    def kern(*refs):
        s_refs = refs[:ns]
        r_refs = refs[ns:ns + nr]
        ss_ref, dy_ref = refs[ns + nr:ns + nr + 2]
        pos = ns + nr + 2
        ex_ref = refs[pos] if has_extra else None
        pos += 1 if has_extra else 0
        ds_refs = refs[pos:pos + ns]
        dr_refs = refs[pos + ns:pos + ns + nr]
        dS_scr = refs[pos + ns + nr]
        d = pl.program_id(0)
        i = pl.program_id(1)

        @pl.when(i == 0)
        def _():
            dS_scr[...] = jnp.zeros(dS_scr.shape, F32)

        @pl.when((i == 0) & (d == 0))
        def _():
            for r in dr_refs:
                r[...] = jnp.zeros(r.shape, F32)

        S = [ss_ref[HD * h:HD * (h + 1), :] for h in range(HEADS)]
        dS = [dS_scr[HD * h:HD * (h + 1), :] for h in range(HEADS)]
        dys = [dy_ref[:, HD * h:HD * (h + 1)] for h in range(HEADS)]
        ins = []
        for r, (_, w, _, n) in zip(s_refs, seqs):
            pw = w // n
            ins.append([r[:, pw * t:pw * (t + 1)] for t in range(n)] if n > 1 else r[...])
        rws = [r[...] for r in r_refs]
        _, vjp = jax.vjp(lambda S_, ins_, rws_: chunk_fn(S_, *ins_, *rws_, d), S, ins, rws)
        dS_in, dins, drws = vjp((dys, dS))
        for h in range(HEADS):
            dS_scr[HD * h:HD * (h + 1), :] = dS_in[h]
        for n_, (r, (_, w, _, n)) in enumerate(zip(ds_refs, seqs)):
            if n > 1:
                pw = w // n
                for t in range(n):
                    piece = dins[n_][t]
                    if has_extra and n_ == 0:
                        piece = piece + jnp.where(d == 0, ex_ref[:, pw * t:pw * (t + 1)], 0.0)
                    r[:, pw * t:pw * (t + 1)] = piece
            else:
                r[...] = dins[n_]
        for r, g in zip(dr_refs, drws):
            r[...] += g

    in_specs = [pl.BlockSpec((Q, w), functools.partial(lambda d, i, cb: (cidx(d, i), cb), cb=cb))
                for _, w, cb, _ in seqs]
    in_specs += [pl.BlockSpec((1, LANE), lambda d, i: (0, 0)) for _ in rows]
    in_specs += [pl.BlockSpec((None, None, GROUP_W, HD), lambda d, i: (d, cidx(d, i), 0, 0)),
                 pl.BlockSpec((Q, GROUP_W), lambda d, i: (cidx(d, i), 0))]
    args = [a for a, _, _, _ in seqs] + list(rows) + [ssave, dy]
    if has_extra:
        in_specs.append(pl.BlockSpec((Q, GROUP_W), lambda d, i: (cidx(d, i), 0)))
        args.append(extra)
    res = pl.pallas_call(
        kern, name=name, grid=(2, nc),
        in_specs=in_specs,
        out_specs=[pl.BlockSpec((None, Q, w), lambda d, i: (d, cidx(d, i), 0)) for _, w, _, _ in seqs]
        + [pl.BlockSpec((1, LANE), lambda d, i: (0, 0)) for _ in rows],
        out_shape=[jax.ShapeDtypeStruct((2, L, w), F32) for _, w, _, _ in seqs]
        + [jax.ShapeDtypeStruct((1, LANE), F32) for _ in rows],
        scratch_shapes=[pltpu.VMEM((GROUP_W, HD), F32)],
        compiler_params=_cparams(("arbitrary", "arbitrary")),
    )(*args)
    return list(res[:ns]), list(res[ns:])


def _loss_call(y, tgt, L):
    T = min(256, L)

    def kern(y_ref, t_ref, dy_ref, l_ref):
        i = pl.program_id(0)
        e = y_ref[...] - t_ref[...]
        dy_ref[...] = e * (1.0 / D_MODEL)

        @pl.when(i == 0)
        def _():
            l_ref[...] = jnp.zeros(l_ref.shape, F32)

        part = 0.5 * jnp.sum(jnp.sum(e * e, axis=-1, keepdims=True) * (1.0 / D_MODEL), axis=0, keepdims=True)
        l_ref[...] += jnp.broadcast_to(part, l_ref.shape)

    return pl.pallas_call(
        kern, name="loss_head", grid=(L // T,),
        in_specs=[_spec2(T, D_MODEL), _spec2(T, D_MODEL)],
        out_specs=[_spec2(T, D_MODEL), pl.BlockSpec((8, LANE), lambda i: (0, 0))],
        out_shape=[jax.ShapeDtypeStruct((L, D_MODEL), F32), jax.ShapeDtypeStruct((8, LANE), F32)],
        compiler_params=_cparams(("arbitrary",)),
    )(y, tgt)


_ANY = pl.BlockSpec(memory_space=pl.ANY)


def _coords():
    return lax.axis_index("x"), lax.axis_index("y"), lax.axis_index("c")


def _gather_chips(flat):
    def body(src, out, send_sems, recv_sems, lsem):
        x, y, c = _coords()
        me = 2 * x + y
        peers = [(1 - x, y), (x, 1 - y), (1 - x, 1 - y)]
        local = pltpu.make_async_copy(src, out.at[me], lsem)
        local.start()
        sends = [pltpu.make_async_remote_copy(src_ref=src, dst_ref=out.at[me], send_sem=send_sems.at[k],
                                              recv_sem=recv_sems.at[k], device_id=(px, py, c), device_id_type=MESH)
                 for k, (px, py) in enumerate(peers)]
        for s in sends:
            s.start()
        for k, (px, py) in enumerate(peers):
            pltpu.make_async_remote_copy(src_ref=src, dst_ref=out.at[2 * px + py], send_sem=send_sems.at[k],
                                         recv_sem=recv_sems.at[k], device_id=(px, py, c),
                                         device_id_type=MESH).wait_recv()
        for s in sends:
            s.wait_send()
        local.wait()

    return pl.pallas_call(
        body, name="gather_weights", in_specs=[_ANY], out_specs=_ANY,
        out_shape=jax.ShapeDtypeStruct((4,) + flat.shape, flat.dtype),
        scratch_shapes=[pltpu.SemaphoreType.DMA((3,)), pltpu.SemaphoreType.DMA((3,)), pltpu.SemaphoreType.DMA(())],
    )(flat)


def _scatter_grads(G, gs):
    def body(g_ref, gs_ref, out, outs, send_sems, recv_sems, ssend, srecv, lsems):
        x, y, c = _coords()
        me = 2 * x + y
        dev = 4 * x + 2 * y + c
        peers = [(1 - x, y), (x, 1 - y), (1 - x, 1 - y)]
        loc = [pltpu.make_async_copy(g_ref.at[me], out.at[me], lsems.at[0]),
               pltpu.make_async_copy(gs_ref, outs.at[dev], lsems.at[1])]
        for l_ in loc:
            l_.start()
        sends = [pltpu.make_async_remote_copy(src_ref=g_ref.at[2 * px + py], dst_ref=out.at[me],
                                              send_sem=send_sems.at[k], recv_sem=recv_sems.at[k],
                                              device_id=(px, py, c), device_id_type=MESH)
                 for k, (px, py) in enumerate(peers)]
        others = []
        for mask in range(1, 8):
            px, py, pc = x ^ (mask >> 2), y ^ ((mask >> 1) & 1), c ^ (mask & 1)
            others.append((px, py, pc))
            sends.append(pltpu.make_async_remote_copy(src_ref=gs_ref, dst_ref=outs.at[dev],
                                                      send_sem=ssend.at[mask - 1], recv_sem=srecv.at[mask - 1],
                                                      device_id=(px, py, pc), device_id_type=MESH))
        for s in sends:
            s.start()
        for k, (px, py) in enumerate(peers):
            pltpu.make_async_remote_copy(src_ref=g_ref.at[me], dst_ref=out.at[2 * px + py],
                                         send_sem=send_sems.at[k], recv_sem=recv_sems.at[k],
                                         device_id=(px, py, c), device_id_type=MESH).wait_recv()
        for k, (px, py, pc) in enumerate(others):
            pltpu.make_async_remote_copy(src_ref=gs_ref, dst_ref=outs.at[4 * px + 2 * py + pc],
                                         send_sem=ssend.at[k], recv_sem=srecv.at[k],
                                         device_id=(px, py, pc), device_id_type=MESH).wait_recv()
        for s in sends:
            s.wait_send()
        for l_ in loc:
            l_.wait()

    return pl.pallas_call(
        body, name="scatter_grads", in_specs=[_ANY, _ANY], out_specs=[_ANY, _ANY],
        out_shape=[jax.ShapeDtypeStruct(G.shape, G.dtype), jax.ShapeDtypeStruct((8,) + gs.shape, gs.dtype)],
        scratch_shapes=[pltpu.SemaphoreType.DMA((3,)), pltpu.SemaphoreType.DMA((3,)),
                        pltpu.SemaphoreType.DMA((7,)), pltpu.SemaphoreType.DMA((7,)),
                        pltpu.SemaphoreType.DMA((2,))],
    )(G, gs)


def _swap_cores(part):
    def body(src, out, send_sem, recv_sem, lsem):
        x, y, c = _coords()
        local = pltpu.make_async_copy(src, out.at[c], lsem)
        local.start()
        send = pltpu.make_async_remote_copy(src_ref=src, dst_ref=out.at[c], send_sem=send_sem, recv_sem=recv_sem,
                                            device_id=(x, y, 1 - c), device_id_type=MESH)
        send.start()
        pltpu.make_async_remote_copy(src_ref=src, dst_ref=out.at[1 - c], send_sem=send_sem, recv_sem=recv_sem,
                                     device_id=(x, y, 1 - c), device_id_type=MESH).wait_recv()
        send.wait_send()
        local.wait()

    return pl.pallas_call(
        body, name="swap_cores", in_specs=[_ANY], out_specs=_ANY,
        out_shape=jax.ShapeDtypeStruct((2,) + part.shape, part.dtype),
        scratch_shapes=[pltpu.SemaphoreType.DMA(()), pltpu.SemaphoreType.DMA(()), pltpu.SemaphoreType.DMA(())],
    )(part)


def _sum_slots(recv, tr):
    n, R, W = recv.shape

    def kern(r_ref, o_ref):
        acc = r_ref[0]
        for s in range(1, n):
            acc = acc + r_ref[s]
        o_ref[...] = acc

    return pl.pallas_call(
        kern, name="sum_chip_grads", grid=(R // tr,),
        in_specs=[pl.BlockSpec((n, tr, W), lambda i: (0, i, 0))],
        out_specs=pl.BlockSpec((tr, W), lambda i: (i, 0)),
        out_shape=jax.ShapeDtypeStruct((R, W), F32),
        compiler_params=_cparams(("arbitrary",)),
    )(recv)


def _adamw_call(name, slots, w, m, v, tr):
    n, R, W = slots.shape

    def kern(s_ref, w_ref, m_ref, v_ref, g_ref, d_ref, nm_ref, nv_ref):
        g = s_ref[0]
        for s in range(1, n):
            g = g + s_ref[s]
        m_ = ADAM_B1 * m_ref[...] + (1.0 - ADAM_B1) * g
        v_ = ADAM_B2 * v_ref[...] + (1.0 - ADAM_B2) * (g * g)
        m_hat = m_ / (1.0 - ADAM_B1 ** ADAM_STEP)
        v_hat = v_ / (1.0 - ADAM_B2 ** ADAM_STEP)
        g_ref[...] = g
        d_ref[...] = -ADAM_LR * (m_hat / (jnp.sqrt(v_hat) + ADAM_EPS) + ADAM_WD * w_ref[...])
        nm_ref[...] = m_
        nv_ref[...] = v_

    blk = pl.BlockSpec((tr, W), lambda i: (i, 0))
    return pl.pallas_call(
        kern, name=name, grid=(R // tr,),
        in_specs=[pl.BlockSpec((n, tr, W), lambda i: (0, i, 0)), blk, blk, blk],
        out_specs=[blk, blk, blk, blk],
        out_shape=[jax.ShapeDtypeStruct((R, W), F32)] * 4,
        compiler_params=_cparams(("arbitrary",)),
    )(slots, w, m, v)


def _pack(arrs, width, row_mult):
    flat = jnp.concatenate([a.reshape(-1) for a in arrs])
    n = flat.shape[0]
    rows = -(-n // width)
    rows = -(-rows // row_mult) * row_mult
    return jnp.pad(flat, (0, rows * width - n)).reshape(rows, width)


def _unpack(buf, shapes):
    flat = buf.reshape(-1)
    out, pos = [], 0
    for s in shapes:
        n = int(np.prod(s))
        out.append(flat[pos:pos + n].reshape(s))
        pos += n
    return out


def _rope_angles(L, rot_dim):
    rows = L // GRID_W
    row = jnp.repeat(jnp.arange(rows), GRID_W).astype(F32)
    col = jnp.tile(jnp.arange(GRID_W), rows).astype(F32)
    sec = rot_dim // 2
    inv_freq = ROPE_BASE ** (-jnp.arange(0, sec, 2, dtype=F32) / sec)
    ang_r = row[:, None] * inv_freq
    ang_c = col[:, None] * inv_freq
    ang = jnp.concatenate([ang_r, ang_r, ang_c, ang_c], axis=-1)
    return jnp.cos(ang), jnp.sin(ang)


def _rot_matrix(r):
    R = np.zeros((r, r), np.float32)
    q = r // 4
    for s in range(2):
        for t in range(q):
            lo = s * (r // 2) + t
            hi = lo + q
            R[hi, lo] = -1.0
            R[lo, hi] = 1.0
    return R


def _place_tables(L, cos, sin, width, offsets):
    r = cos.shape[1]
    C = jnp.ones((L, width), F32)
    S = jnp.zeros((L, width), F32)
    Rm = np.zeros((width, width), np.float32)
    R = _rot_matrix(r)
    for o in offsets:
        C = C.at[:, o:o + r].set(cos)
        S = S.at[:, o:o + r].set(sin)
        Rm[o:o + r, o:o + r] = R
    return C, S, jnp.asarray(Rm)


def _head_mean_matrix(width, stride, n):
    M = np.zeros((width, width), np.float32)
    for o in range(0, width, stride):
        M[o:o + n, o:o + n] = 1.0 / n
    return jnp.asarray(M)


def _pad_heads(w, n_heads, real, padded, axis):
    parts = jnp.split(w, n_heads, axis=axis)
    padw = [(0, 0)] * w.ndim
    padw[axis] = (0, padded - real)
    return jnp.concatenate([jnp.pad(p, padw) for p in parts], axis=axis)


def _row128(v):
    v = v.reshape(1, -1)
    return jnp.pad(v, ((0, 0), (0, LANE - v.shape[1])))


def _conv_w8(w, b):
    C = w.shape[1]
    rows = [w, jnp.zeros((1, C), F32) if b is None else b.reshape(1, C), jnp.zeros((4, C), F32)]
    return jnp.concatenate(rows, axis=0)


def _build_layer(W):
    w_in = W['w_in']
    o = 0
    cols = {}
    for name, n in [('a_cq', A_Q_LORA), ('a_ckv', A_KV_LORA), ('a_kr', A_ROPE), ('b_q', 256), ('b_k', 128),
                    ('b_v', 128), ('c_z', 256), ('c_xbc', 512), ('c_dt', 8), ('d_qkv', 768), ('d_z', 256),
                    ('d_b', 8), ('d_a', 8)]:
        cols[name] = w_in[:, o:o + n]
        o += n
    padc = lambda a, lo, width: jnp.pad(a, ((0, 0), (lo, width - lo - a.shape[1])))
    pieces = {
        'b_q': _pad_heads(cols['b_q'], 4, HD, LANE, 1), 'c_xbc': cols['c_xbc'], 'a_cq': padc(cols['a_cq'], 0, 256),
        'b_k': _pad_heads(cols['b_k'], 2, HD, LANE, 1), 'd_qkv': cols['d_qkv'],
        'b_v': _pad_heads(cols['b_v'], 2, HD, LANE, 1), 'c_z': cols['c_z'], 'd_z': cols['d_z'],
        'a_ckv': cols['a_ckv'], 'a_kr': padc(cols['a_kr'], A_NOPE, LANE), 'c_dt': padc(cols['c_dt'], 0, LANE),
        'd_b': padc(cols['d_b'], 0, LANE), 'd_a': padc(cols['d_a'], 0, LANE),
        'pad': jnp.zeros((D_MODEL, LANE), F32)}
    out = {'w_in': jnp.concatenate([pieces[n] for n, _, _ in P_LAYOUT], axis=1)}
    out['a_q_norm'] = padc(W['a_q_norm'].reshape(1, -1), 0, 256)
    wuq = jnp.pad(W['a_w_uq'], ((0, 256 - A_Q_LORA), (0, 0)))
    out['a_w_uq'] = _pad_heads(wuq, 4, A_NOPE + A_ROPE, LANE, 1)
    out['a_kv_norm'] = W['a_kv_norm'].reshape(1, -1)
    ukv = W['a_w_ukv'].reshape(A_KV_LORA, HEADS, 2, HD)
    out['a_w_uk'] = _pad_heads(ukv[:, :, 0, :].reshape(A_KV_LORA, 256), 4, HD, LANE, 1)
    out['a_w_uv'] = _pad_heads(ukv[:, :, 1, :].reshape(A_KV_LORA, 256), 4, HD, LANE, 1)
    out['a_out_norm'] = _pad_heads(W['a_out_norm'].reshape(1, -1), 4, HD, LANE, 1)
    out['b_q_norm'] = _pad_heads(jnp.tile(W['b_q_norm'].reshape(1, -1), (1, 4)), 4, HD, LANE, 1)
    out['b_k_norm'] = _pad_heads(jnp.tile(W['b_k_norm'].reshape(1, -1), (1, 2)), 2, HD, LANE, 1)
    out['b_out_norm'] = _pad_heads(W['b_out_norm'].reshape(1, -1), 4, HD, LANE, 1)
    out['c_conv'] = _conv_w8(W['c_conv_w'], W['c_conv_b'])
    out['c_a_log'] = _row128(W['c_a_log'])
    out['c_dt_bias'] = _row128(W['c_dt_bias'])
    out['c_d_skip'] = jnp.repeat(W['c_d_skip'], HD).reshape(1, -1)
    out['c_out_norm'] = W['c_out_norm'].reshape(1, -1)
    out['d_conv'] = _conv_w8(W['d_conv_w'], None)
    out['d_a_log'] = _row128(W['d_a_log'])
    out['d_dt_bias'] = _row128(W['d_dt_bias'])
    out['d_out_norm'] = jnp.tile(W['d_out_norm'].reshape(1, -1), (1, 4))
    wo = W['w_out']
    out['w_out'] = jnp.concatenate([_pad_heads(wo[0:256], 4, HD, LANE, 0), _pad_heads(wo[256:512], 4, HD, LANE, 0),
                                    wo[512:1024]], axis=0)
    for n in ['pre_mix_norm', 'post_mix_norm', 'pre_ffn_norm', 'post_ffn_norm']:
        out[n] = W[n].reshape(1, -1)
    out['f_w_in'] = W['f_w_in']
    out['f_conv'] = _conv_w8(W['f_conv_w'], W['f_conv_b'])
    out['f_w_out'] = W['f_w_out']
    return out


def _fn_norm_in(a, p):
    return [_rms(a[0], p[0])]


def _fn_resid_norm2(a, p):
    x1 = a[0] + _rms(a[1], p[0])
    return [x1, _rms(x1, p[1])]


def _fn_resid_norm(a, p):
    return [a[0] + _rms(a[1], p[0])]


def _fn_a_prep(a, p):
    cq, ckv, kr, cosq, sinq, cosk, sink = a
    q_norm, w_uq, kv_norm, w_uk, w_uv, rq, rk = p
    q = _nn(_rms(cq, q_norm, A_Q_LORA), w_uq)
    q = q * cosq + _nn_hi(q, rq) * sinq
    kvn = _rms(ckv, kv_norm)
    kr_r = kr * cosk + _nn_hi(kr, rk) * sink
    kk = _nn(kvn, w_uk) + jnp.concatenate([kr_r] * HEADS, axis=1)
    return [q, kk, _nn(kvn, w_uv)]


def _fn_b_prep(a, p):
    q, k, v, cosq, sinq, cosk, sink = a
    q_norm, k_norm, mq, mk, rq, rk = p
    qn = q * lax.rsqrt(_nn_hi(q * q, mq) + EPS) * q_norm
    kn = k * lax.rsqrt(_nn_hi(k * k, mk) + EPS) * k_norm
    return [qn * cosq + _nn_hi(qn, rq) * sinq, kn * cosk + _nn_hi(kn, rk) * sink, v]


def _fn_mixer_post(a, p):
    oa, ob, yc0, yc1, xs, zc, od0, od1, zd = a
    a_norm, b_norm, dskip, c_norm, d_norm, m64 = p
    oc = _rms((yc0 + yc1 + xs * dskip) * _silu(zc), c_norm)
    od = od0 + od1
    odn = od * lax.rsqrt(_nn_hi(od * od, m64) + EPS) * d_norm * _silu(zd)
    return [jnp.concatenate([_rms(oa, a_norm, GROUP_W), _rms(ob, b_norm, GROUP_W), oc, odn], axis=1)]


def _fn_assemble(a, p):
    (dbq, dxbc, dcq, dbk, dqkv, dbv, dzc, dzd, dckv, dkr, ddt0, ddt1, db0, db1, da0, da1) = a
    return [jnp.concatenate([dbq, dxbc, dcq, dbk, dqkv, dbv, dzc, dzd, dckv, dkr, ddt0 + ddt1, db0 + db1,
                             da0 + da1, jnp.zeros_like(dckv)], axis=1)]


def _pspec(T, name):
    off, w = P_OFF[name]
    return _spec2(T, w, off // w)


def _layer_fwd(l, x, h, K, tabs, L, T, next_norm):
    n = f"l{l}_"
    sv = {'x': x, 'h': h}
    p = _mm(n + "in_proj", h, K['w_in'].astype(BF16), 'nn', F32, 512, 768, 1024)
    sv['p'] = p
    a_acts = [(p, _pspec(T, 'a_cq')), (p, _pspec(T, 'a_ckv')), (p, _pspec(T, 'a_kr')),
              (tabs['a_cq'], _spec2(T, 512)), (tabs['a_sq'], _spec2(T, 512)),
              (tabs['a_ck'], _spec2(T, LANE)), (tabs['a_sk'], _spec2(T, LANE))]
    a_pars = [K['a_q_norm'], K['a_w_uq'], K['a_kv_norm'], K['a_w_uk'], K['a_w_uv'], tabs['a_rq'], tabs['a_rk']]
    qa, ka, va = _tw_fwd(n + "a_prep", _fn_a_prep, a_acts, a_pars, [(512, BF16)] * 3, L, T)
    oa, lse_a = _flash_fwd(n + "a_attn", qa, ka, va, HEADS, 1, (A_NOPE + A_ROPE) ** -0.5, L)
    sv.update(a_acts=a_acts, a_pars=a_pars, qa=qa, ka=ka, va=va, oa=oa, lse_a=lse_a)
    b_acts = [(p, _pspec(T, 'b_q')), (p, _pspec(T, 'b_k')), (p, _pspec(T, 'b_v')),
              (tabs['b_cq'], _spec2(T, 512)), (tabs['b_sq'], _spec2(T, 512)),
              (tabs['b_ck'], _spec2(T, 256)), (tabs['b_sk'], _spec2(T, 256))]
    b_pars = [K['b_q_norm'], K['b_k_norm'], tabs['b_mq'], tabs['b_mk'], tabs['b_rq'], tabs['b_rk']]
    qb, kb, vb = _tw_fwd(n + "b_prep", _fn_b_prep, b_acts, b_pars, [(512, BF16), (256, BF16), (256, BF16)], L, T)
    ob, lse_b = _flash_fwd(n + "b_attn", qb, kb, vb, HEADS, 2, HD ** -0.5, L)
    sv.update(b_acts=b_acts, b_pars=b_pars, qb=qb, kb=kb, vb=vb, ob=ob, lse_b=lse_b)
    xbc = _conv_fwd(n + "c_conv", p, P_OFF['c_xbc'][0], C_XBC, K['c_conv'], True, L, 512)
    c_seqs = [(xbc, 256, 0, 4), (xbc, LANE, 2, 2), (xbc, LANE, 3, 2), (p, LANE, P_OFF['c_dt'][0] // LANE, 1)]
    c_rows = [K['c_a_log'], K['c_dt_bias']]
    yc, sc = _scan_fwd(n + "c_ssd", _ssd_chunk, c_seqs, c_rows, C_CHUNK, L)
    sv.update(xbc=xbc, c_seqs=c_seqs, c_rows=c_rows, sc=sc)
    qkv = _conv_fwd(n + "d_conv", p, P_OFF['d_qkv'][0], D_QKV, K['d_conv'], True, L, 768)
    d_seqs = [(qkv, 256, 0, 4), (qkv, 256, 1, 4), (qkv, 256, 2, 4), (p, LANE, P_OFF['d_b'][0] // LANE, 1),
              (p, LANE, P_OFF['d_a'][0] // LANE, 1)]
    d_rows = [K['d_a_log'], K['d_dt_bias']]
    od, sd = _scan_fwd(n + "d_delta", _delta_chunk, d_seqs, d_rows, D_CHUNK, L)
    sv.update(qkv=qkv, d_seqs=d_seqs, d_rows=d_rows, sd=sd)
    m_acts = [(oa, _spec2(T, 512)), (ob, _spec2(T, 512)), (yc, _spec3(T, 256, 0)), (yc, _spec3(T, 256, 1)),
              (xbc, _spec2(T, 256, 0)), (p, _pspec(T, 'c_z')), (od, _spec3(T, 256, 0)), (od, _spec3(T, 256, 1)),
              (p, _pspec(T, 'd_z'))]
    m_pars = [K['a_out_norm'], K['b_out_norm'], K['c_d_skip'], K['c_out_norm'], K['d_out_norm'], tabs['m64']]
    (o,) = _tw_fwd(n + "mixer_post", _fn_mixer_post, m_acts, m_pars, [(O_COLS, BF16)], L, T)
    f1 = _mm(n + "out_proj", o, K['w_out'].astype(BF16), 'nn', F32, 512, 1024, 768)
    r1_pars = [K['post_mix_norm'], K['pre_ffn_norm']]
    x1, h2 = _tw_fwd(n + "resid_mix", _fn_resid_norm2, [(x, _spec2(T, D_MODEL)), (f1, _spec2(T, D_MODEL))], r1_pars,
                     [(D_MODEL, F32), (D_MODEL, BF16)], L, T)
    sv.update(m_acts=m_acts, m_pars=m_pars, o=o, f1=f1, r1_pars=r1_pars, x1=x1, h2=h2)
    u = _mm(n + "ffn_in", h2, K['f_w_in'].astype(BF16), 'nn', F32, 512, 512, 1024)
    gu = _conv_fwd(n + "ffn_conv", u, 0, 2 * D_FF, K['f_conv'], False, L, 1408)
    act = _glu_fwd(n + "ffn_glu", gu, L)
    f2 = _mm(n + "ffn_out", act, K['f_w_out'].astype(BF16), 'nn', F32, 512, 1024, 1408)
    sv.update(u=u, gu=gu, act=act, f2=f2)
    xf = [(x1, _spec2(T, D_MODEL)), (f2, _spec2(T, D_MODEL))]
    if next_norm is None:
        (x2,) = _tw_fwd(n + "resid_ffn", _fn_resid_norm, xf, [K['post_ffn_norm']], [(D_MODEL, F32)], L, T)
        hn = None
    else:
        x2, hn = _tw_fwd(n + "resid_ffn", _fn_resid_norm2, xf, [K['post_ffn_norm'], next_norm],
                         [(D_MODEL, F32), (D_MODEL, BF16)], L, T)
    return x2, hn, sv


def _layer_bwd(l, dx2, dhn, K, sv, tabs, L, T, next_norm):
    n = f"l{l}b_"
    dK = {}
    s2 = lambda w, cb=0: _spec2(T, w, cb)
    xf = [(sv['x1'], s2(D_MODEL)), (sv['f2'], s2(D_MODEL))]
    if next_norm is None:
        (dx1a, df2), (dK['post_ffn_norm'],) = _tw_bwd(n + "resid_ffn", _fn_resid_norm, xf, [K['post_ffn_norm']],
                                                      [(dx2, s2(D_MODEL))], L, T, [True, True], [True])
        dnext = None
    else:
        (dx1a, df2), (dK['post_ffn_norm'], dnext) = _tw_bwd(
            n + "resid_ffn", _fn_resid_norm2, xf, [K['post_ffn_norm'], next_norm],
            [(dx2, s2(D_MODEL)), (dhn, s2(D_MODEL))], L, T, [True, True], [True, True])
    dact = _mm(n + "ffn_out_dx", df2, K['f_w_out'].astype(BF16), 'nt', F32, 512, 1408, 1024)
    dK['f_w_out'] = _mm(n + "ffn_out_dw", sv['act'], df2, 'tn', F32, 1408, 1024, 512)
    dgu = _glu_bwd(n + "ffn_glu", sv['gu'], dact, L)
    du, dK['f_conv'] = _conv_bwd(n + "ffn_conv", sv['u'], 0, 2 * D_FF, K['f_conv'], False, [(dgu, None)], L, 1408)
    dh2 = _mm(n + "ffn_in_dx", du, K['f_w_in'].astype(BF16), 'nt', F32, 512, 1024, 512)
    dK['f_w_in'] = _mm(n + "ffn_in_dw", sv['h2'], du, 'tn', F32, 512, 512, 512)
    (dxa, df1), (dK['post_mix_norm'], dK['pre_ffn_norm']) = _tw_bwd(
        n + "resid_mix", _fn_resid_norm2, [(sv['x'], s2(D_MODEL)), (sv['f1'], s2(D_MODEL))], sv['r1_pars'],
        [(dx1a, s2(D_MODEL)), (dh2, s2(D_MODEL))], L, T, [True, True], [True, True])
    do = _mm(n + "out_proj_dx", df1, K['w_out'].astype(BF16), 'nt', F32, 512, 768, 1024)
    dK['w_out'] = _mm(n + "out_proj_dw", sv['o'], df1, 'tn', F32, 768, 1024, 512)
    (doa, dob, dyc0, _, dxs_skip, dzc, dod0, _, dzd), mp = _tw_bwd(
        n + "mixer_post", _fn_mixer_post, sv['m_acts'], sv['m_pars'], [(do, s2(O_COLS))], L, T,
        [True] * 9, [True] * 5 + [False])
    dK['a_out_norm'], dK['b_out_norm'], dK['c_d_skip'], dK['c_out_norm'], dK['d_out_norm'] = mp
    (dqkv_q, dqkv_k, dqkv_v, db_raw, da_raw), (dK['d_a_log'], dK['d_dt_bias']) = _scan_bwd(
        n + "d_delta", _delta_chunk, sv['d_seqs'], sv['d_rows'], sv['sd'], dod0, None, D_CHUNK, L)
    dqkv_act = jnp.concatenate([dqkv_q, dqkv_k, dqkv_v], axis=2)
    dqkv, dK['d_conv'] = _conv_bwd(n + "d_conv", sv['p'], P_OFF['d_qkv'][0], D_QKV, K['d_conv'], True,
                                   [(dqkv_act, 0), (dqkv_act, 1)], L, 768)
    (dxs, dBm, dCm, ddt_raw), (dK['c_a_log'], dK['c_dt_bias']) = _scan_bwd(
        n + "c_ssd", _ssd_chunk, sv['c_seqs'], sv['c_rows'], sv['sc'], dyc0, dxs_skip, C_CHUNK, L)
    dxbc_act = jnp.concatenate([dxs, dBm, dCm], axis=2)
    dxbc, dK['c_conv'] = _conv_bwd(n + "c_conv", sv['p'], P_OFF['c_xbc'][0], C_XBC, K['c_conv'], True,
                                   [(dxbc_act, 0), (dxbc_act, 1)], L, 512)
    dqb, dkb, dvb = _flash_bwd(n + "b_attn", sv['qb'], sv['kb'], sv['vb'], sv['ob'], sv['lse_b'], dob, HEADS, 2,
                               HD ** -0.5, L)
    (dbq, dbk, dbv), (dK['b_q_norm'], dK['b_k_norm']) = _tw_bwd(
        n + "b_prep", _fn_b_prep, sv['b_acts'], sv['b_pars'], [(dqb, s2(512)), (dkb, s2(256)), (dvb, s2(256))],
        L, T, [True] * 3 + [False] * 4, [True, True] + [False] * 4)
    dqa, dka, dva = _flash_bwd(n + "a_attn", sv['qa'], sv['ka'], sv['va'], sv['oa'], sv['lse_a'], doa, HEADS, 1,
                               (A_NOPE + A_ROPE) ** -0.5, L)
    (dcq, dckv, dkr), ap = _tw_bwd(
        n + "a_prep", _fn_a_prep, sv['a_acts'], sv['a_pars'], [(dqa, s2(512)), (dka, s2(512)), (dva, s2(512))],
        L, T, [True] * 3 + [False] * 4, [True] * 5 + [False] * 2)
    dK['a_q_norm'], dK['a_w_uq'], dK['a_kv_norm'], dK['a_w_uk'], dK['a_w_uv'] = ap
    pieces = [(dbq, s2(512)), (dxbc, s2(512)), (dcq, s2(256)), (dbk, s2(256)), (dqkv, s2(768)), (dbv, s2(256)),
              (dzc, s2(256)), (dzd, s2(256)), (dckv, s2(LANE)), (dkr, s2(LANE)),
              (ddt_raw, _spec3(T, LANE, 0)), (ddt_raw, _spec3(T, LANE, 1)),
              (db_raw, _spec3(T, LANE, 0)), (db_raw, _spec3(T, LANE, 1)),
              (da_raw, _spec3(T, LANE, 0)), (da_raw, _spec3(T, LANE, 1))]
    (dp,) = _tw_fwd(n + "assemble_dp", _fn_assemble, pieces, [], [(P_COLS, BF16)], L, T)
    dh = _mm(n + "in_proj_dx", dp, K['w_in'].astype(BF16), 'nt', F32, 512, 1024, 768)
    dK['w_in'] = _mm(n + "in_proj_dw", sv['h'], dp, 'tn', F32, 512, 768, 512)
    return dxa, dh, dK, dnext


def _tables(L):
    ca, sa = _rope_angles(L, A_ROPE)
    cb, sb = _rope_angles(L, HD)
    t = {}
    t['a_cq'], t['a_sq'], t['a_rq'] = _place_tables(L, ca, sa, 512, [LANE * h + A_NOPE for h in range(4)])
    t['a_ck'], t['a_sk'], t['a_rk'] = _place_tables(L, ca, sa, LANE, [A_NOPE])
    t['b_cq'], t['b_sq'], t['b_rq'] = _place_tables(L, cb, sb, 512, [LANE * h for h in range(4)])
    t['b_ck'], t['b_sk'], t['b_rk'] = _place_tables(L, cb, sb, 256, [LANE * h for h in range(2)])
    t['b_mq'] = _head_mean_matrix(512, LANE, HD)
    t['b_mk'] = _head_mean_matrix(256, LANE, HD)
    t['m64'] = _head_mean_matrix(256, HD, HD)
    return t


def kernel(x, pre_mix_norm, w_in, a_q_norm, a_w_uq, a_kv_norm, a_w_ukv, a_out_norm, b_q_norm, b_k_norm, b_out_norm, c_conv_w, c_conv_b, c_a_log, c_dt_bias, c_d_skip, c_out_norm, d_conv_w, d_a_log, d_dt_bias, d_out_norm, w_out, post_mix_norm, pre_ffn_norm, f_w_in, f_conv_w, f_conv_b, f_w_out, post_ffn_norm, loss_target, m_pre_mix_norm, m_w_in, m_a_q_norm, m_a_w_uq, m_a_kv_norm, m_a_w_ukv, m_a_out_norm, m_b_q_norm, m_b_k_norm, m_b_out_norm, m_c_conv_w, m_c_conv_b, m_c_a_log, m_c_dt_bias, m_c_d_skip, m_c_out_norm, m_d_conv_w, m_d_a_log, m_d_dt_bias, m_d_out_norm, m_w_out, m_post_mix_norm, m_pre_ffn_norm, m_f_w_in, m_f_conv_w, m_f_conv_b, m_f_w_out, m_post_ffn_norm, v_pre_mix_norm, v_w_in, v_a_q_norm, v_a_w_uq, v_a_kv_norm, v_a_w_ukv, v_a_out_norm, v_b_q_norm, v_b_k_norm, v_b_out_norm, v_c_conv_w, v_c_conv_b, v_c_a_log, v_c_dt_bias, v_c_d_skip, v_c_out_norm, v_d_conv_w, v_d_a_log, v_d_dt_bias, v_d_out_norm, v_w_out, v_post_mix_norm, v_pre_ffn_norm, v_f_w_in, v_f_conv_w, v_f_conv_b, v_f_w_out, v_post_ffn_norm):
    loc = locals()
    Wl = {n: loc[n] for n in WEIGHTS}
    Ml = {n: loc['m_' + n] for n in WEIGHTS}
    Vl = {n: loc['v_' + n] for n in WEIGHTS}
    L = x.shape[1]
    T = min(256, L)
    x0 = x.reshape(L, D_MODEL)
    tgt = loss_target.reshape(L, D_MODEL)

    shard_shapes = [Wl[n].shape for n in SHARDED]
    w_flat = _pack([Wl[n] for n in SHARDED], 1024, 256)
    gathered = _gather_chips(w_flat)
    per_chip = [_unpack(gathered[j], shard_shapes) for j in range(4)]
    full = dict(Wl)
    for k, n in enumerate(SHARDED):
        full[n] = jnp.concatenate([per_chip[j][k] for j in range(4)], axis=SHARD_AXIS[n])

    tabs = _tables(L)
    Ks, builds = [], []
    for l in range(DEPTH):
        K, vjp_l = jax.vjp(_build_layer, {n: full[n][l] for n in WEIGHTS})
        Ks.append(K)
        builds.append(vjp_l)

    (h,) = _tw_fwd("l0_norm_in", _fn_norm_in, [(x0, _spec2(T, D_MODEL))], [Ks[0]['pre_mix_norm']],
                   [(D_MODEL, BF16)], L, T)
    xs, saves = x0, []
    for l in range(DEPTH):
        nxt = Ks[l + 1]['pre_mix_norm'] if l + 1 < DEPTH else None
        xs, h, sv = _layer_fwd(l, xs, h, Ks[l], tabs, L, T, nxt)
        saves.append(sv)
    dy, loss_acc = _loss_call(xs, tgt, L)
    loss = lax.psum(loss_acc[0, 0], ("x", "y", "c"))

    grads = [None] * DEPTH
    dx, dhn = dy, None
    for l in reversed(range(DEPTH)):
        nxt = Ks[l + 1]['pre_mix_norm'] if l + 1 < DEPTH else None
        dxa, dh, dK, dnext = _layer_bwd(l, dx, dhn, Ks[l], saves[l], tabs, L, T, nxt)
        if dnext is not None:
            grads[l + 1]['pre_mix_norm'] = dnext
        grads[l] = dK
        if l > 0:
            dx, dhn = dxa, dh
        else:
            (dx_in,), (dpre,) = _tw_bwd("l0b_norm_in", _fn_norm_in, [(x0, _spec2(T, D_MODEL))],
                                        [Ks[0]['pre_mix_norm']], [(dh, _spec2(T, D_MODEL))], L, T, [True], [True],
                                        addto={0: (dxa, _spec2(T, D_MODEL))})
            grads[0]['pre_mix_norm'] = dpre
    gfull = []
    for l in range(DEPTH):
        zero = jax.tree.map(jnp.zeros_like, Ks[l])
        zero.update(grads[l])
        (gl,) = builds[l](zero)
        gfull.append(gl)
    gW = {n: jnp.stack([gfull[l][n] for l in range(DEPTH)]) for n in WEIGHTS}

    G = jnp.stack([_pack([jnp.split(gW[n], 4, axis=SHARD_AXIS[n])[j] for n in SHARDED], 1024, 256)
                   for j in range(4)])
    small_shapes = [Wl[n].shape for n in SMALL]
    gs = _pack([gW[n] for n in SMALL], LANE, 8)
    recv, recv_small = _scatter_grads(G, gs)
    pair = _swap_cores(_sum_slots(recv, 256))
    big = _adamw_call("adamw_sharded", pair, w_flat, _pack([Ml[n] for n in SHARDED], 1024, 256),
                      _pack([Vl[n] for n in SHARDED], 1024, 256), 256)
    small = _adamw_call("adamw_small", recv_small, _pack([Wl[n] for n in SMALL], LANE, 8),
                        _pack([Ml[n] for n in SMALL], LANE, 8), _pack([Vl[n] for n in SMALL], LANE, 8), gs.shape[0])
    res = {}
    for kind, b, s in zip(['grad', 'delta', 'new_m', 'new_v'], big, small):
        for n, a in zip(SHARDED, _unpack(b, shard_shapes)):
            res[kind, n] = a
        for n, a in zip(SMALL, _unpack(s, small_shapes)):
            res[kind, n] = a
    outs = [loss, dx_in.reshape(x.shape)]
    for kind in ['grad', 'delta', 'new_m', 'new_v']:
        outs += [res[kind, n] for n in WEIGHTS]
    return tuple(outs)
```

```python
import functools
import math

import numpy as np
import jax
import jax.numpy as jnp
from jax import lax
from jax.experimental import pallas as pl
from jax.experimental.pallas import tpu as pltpu

F32 = jnp.float32
BF16 = jnp.bfloat16
MESH = pl.DeviceIdType.MESH
VMEM_LIMIT = 48 * 1024 * 1024
LANE = 128

D_MODEL = 1024
DEPTH = 2
GRID_W = 64
ROPE_BASE = 10000.0
EPS = 1e-6
GROUP_W = 256
HEADS = 4
HD = 64
A_NOPE, A_ROPE, A_Q_LORA, A_KV_LORA = 64, 32, 192, 128
A_COLS = A_Q_LORA + A_KV_LORA + A_ROPE
B_COLS = 512
C_XBC = 512
C_COLS = GROUP_W + C_XBC + 8
D_QKV = 768
D_COLS = D_QKV + GROUP_W + 16
IN_COLS = A_COLS + B_COLS + C_COLS + D_COLS
C_CHUNK = 128
D_CHUNK = 64
C_PER_STEP = 1
D_PER_STEP = 2
D_FF = 2816
ADAM_LR, ADAM_B1, ADAM_B2, ADAM_EPS, ADAM_WD, ADAM_STEP = 0.001, 0.9, 0.999, 1e-08, 0.01, 10

WEIGHTS = ['pre_mix_norm', 'w_in', 'a_q_norm', 'a_w_uq', 'a_kv_norm', 'a_w_ukv', 'a_out_norm', 'b_q_norm',
           'b_k_norm', 'b_out_norm', 'c_conv_w', 'c_conv_b', 'c_a_log', 'c_dt_bias', 'c_d_skip', 'c_out_norm',
           'd_conv_w', 'd_a_log', 'd_dt_bias', 'd_out_norm', 'w_out', 'post_mix_norm', 'pre_ffn_norm', 'f_w_in',
           'f_conv_w', 'f_conv_b', 'f_w_out', 'post_ffn_norm']
SHARD_AXIS = {'w_in': 2, 'a_w_uq': 2, 'a_w_ukv': 2, 'c_conv_w': 2, 'd_conv_w': 2, 'w_out': 1, 'f_w_in': 2,
              'f_conv_w': 2, 'f_w_out': 1}
SHARDED = [n for n in WEIGHTS if n in SHARD_AXIS]
SMALL = [n for n in WEIGHTS if n not in SHARD_AXIS]

P_LAYOUT = [('b_q', 0, 512), ('c_xbc', 512, 512), ('a_cq', 1024, 256), ('b_k', 1280, 256), ('d_qkv', 1536, 768),
            ('b_v', 2304, 256), ('c_z', 2560, 256), ('d_z', 2816, 256), ('a_ckv', 3072, 128), ('a_kr', 3200, 128),
            ('c_dt', 3328, 128), ('d_b', 3456, 128), ('d_a', 3584, 128), ('pad', 3712, 128)]
P_OFF = {n: (o, w) for n, o, w in P_LAYOUT}
P_COLS = 3840
O_COLS = 1536


def _cparams(sem):
    return pltpu.CompilerParams(dimension_semantics=sem, vmem_limit_bytes=VMEM_LIMIT)


def _tile(n, target):
    best = None
    for d in range(LANE, min(n, target) + 1, LANE):
        if n % d == 0:
            best = d
    return best if best is not None else n


_NN = ((1,), (0,))
_NT = ((1,), (1,))
_TN = ((0,), (0,))


def _raw_dot(a, b, dims, hi):
    if hi:
        prec = lax.Precision.HIGH if hi == 'high' else lax.Precision.HIGHEST
        return lax.dot_general(a, b, (dims, ((), ())), precision=prec, preferred_element_type=F32)
    return lax.dot_general(a.astype(BF16), b.astype(BF16), (dims, ((), ())), preferred_element_type=F32)


def _make_dots(hi):
    @jax.custom_vjp
    def nn(a, b):
        return _raw_dot(a, b, _NN, hi)

    @jax.custom_vjp
    def nt(a, b):
        return _raw_dot(a, b, _NT, hi)

    @jax.custom_vjp
    def tn(a, b):
        return _raw_dot(a, b, _TN, hi)

    nn.defvjp(lambda a, b: (nn(a, b), (a, b)), lambda r, g: (nt(g, r[1]), tn(r[0], g)))
    nt.defvjp(lambda a, b: (nt(a, b), (a, b)), lambda r, g: (nn(g, r[1]), tn(g, r[0])))
    tn.defvjp(lambda a, b: (tn(a, b), (a, b)), lambda r, g: (nt(r[1], g), nn(r[0], g)))
    return nn, nt, tn


_nn, _nt, _tn = _make_dots(False)
_nn_hi, _nt_hi, _tn_hi = _make_dots(True)
_nn_h3, _nt_h3, _tn_h3 = _make_dots('high')


def _sigmoid(x):
    return 1.0 / (1.0 + jnp.exp(-x))


def _silu(x):
    return x * _sigmoid(x)


def _softplus(x):
    return jnp.maximum(x, 0.0) + jnp.log(1.0 + jnp.exp(-jnp.abs(x)))


def _rms(x, w, n=None):
    n = x.shape[-1] if n is None else n
    ms = jnp.sum(x * x, axis=-1, keepdims=True) * (1.0 / n)
    return x * lax.rsqrt(ms + EPS) * w


def _spec2(T, w, cb=0):
    return pl.BlockSpec((T, w), lambda i: (i, cb))


def _spec3(T, w, lead, cb=0):
    return pl.BlockSpec((None, T, w), lambda i: (lead, i, cb))


def _full_spec(a):
    nd = a.ndim
    return pl.BlockSpec(a.shape, lambda i: (0,) * nd)


def _tw_fwd(name, fn, acts, params, outs, L, T):
    na, npar = len(acts), len(params)

    def kern(*refs):
        a = [r[...].astype(F32) for r in refs[:na]]
        p = [r[...].astype(F32) for r in refs[na:na + npar]]
        res = fn(a, p)
        for r, o in zip(refs[na + npar:], res):
            r[...] = o.astype(r.dtype)

    return pl.pallas_call(
        kern, name=name, grid=(L // T,),
        in_specs=[s for _, s in acts] + [_full_spec(p) for p in params],
        out_specs=[_spec2(T, w) for w, _ in outs],
        out_shape=[jax.ShapeDtypeStruct((L, w), dt) for w, dt in outs],
        compiler_params=_cparams(("arbitrary",)),
    )(*[a for a, _ in acts], *params)


def _tw_bwd(name, fn, acts, params, douts, L, T, act_grad, par_grad, addto=None):
    na, npar, nd = len(acts), len(params), len(douts)
    addto = addto or {}
    add_keys = sorted(addto)
    ga = [k for k in range(na) if act_grad[k]]
    gp = [k for k in range(npar) if par_grad[k]]

    def kern(*refs):
        i = pl.program_id(0)
        a = [r[...].astype(F32) for r in refs[:na]]
        p = [r[...].astype(F32) for r in refs[na:na + npar]]
        g = [r[...].astype(F32) for r in refs[na + npar:na + npar + nd]]
        pos = na + npar + nd
        adds = [r[...].astype(F32) for r in refs[pos:pos + len(add_keys)]]
        pos += len(add_keys)
        da_refs = refs[pos:pos + len(ga)]
        dp_refs = refs[pos + len(ga):]

        def f(ad, pd):
            af, pf = list(a), list(p)
            for k, v in zip(ga, ad):
                af[k] = v
            for k, v in zip(gp, pd):
                pf[k] = v
            return fn(af, pf)

        _, vjp = jax.vjp(f, [a[k] for k in ga], [p[k] for k in gp])
        dad, dpd = vjp(list(g))
        for n, (r, d) in enumerate(zip(da_refs, dad)):
            if n in addto:
                d = d + adds[add_keys.index(n)]
            r[...] = d.astype(r.dtype)

        @pl.when(i == 0)
        def _():
            for r in dp_refs:
                r[...] = jnp.zeros(r.shape, F32)

        for r, d in zip(dp_refs, dpd):
            r[...] += d

    def width(spec):
        return spec.block_shape[-1]

    res = pl.pallas_call(
        kern, name=name, grid=(L // T,),
        in_specs=[s for _, s in acts] + [_full_spec(p) for p in params] + [s for _, s in douts]
        + [addto[k][1] for k in add_keys],
        out_specs=[_spec2(T, width(acts[k][1])) for k in ga] + [_full_spec(params[k]) for k in gp],
        out_shape=[jax.ShapeDtypeStruct((L, width(acts[k][1])), F32) for k in ga]
        + [jax.ShapeDtypeStruct(params[k].shape, F32) for k in gp],
        compiler_params=_cparams(("arbitrary",)),
    )(*[a for a, _ in acts], *params, *[a for a, _ in douts], *[addto[k][0] for k in add_keys])
    return list(res[:len(ga)]), list(res[len(ga):])


def _mm(name, a, b, mode, out_dtype, tm, tn, tk):
    if mode == 'nn':
        (M, K), N = a.shape, b.shape[1]
    elif mode == 'nt':
        (M, K), N = a.shape, b.shape[0]
    else:
        (K, M), N = a.shape, b.shape[1]
    tm, tn, tk = _tile(M, tm), _tile(N, tn), _tile(K, tk)
    nk = K // tk
    if mode == 'nn':
        a_spec = pl.BlockSpec((tm, tk), lambda i, j, k: (i, k))
        b_spec = pl.BlockSpec((tk, tn), lambda i, j, k: (k, j))
        dims = _NN
    elif mode == 'nt':
        a_spec = pl.BlockSpec((tm, tk), lambda i, j, k: (i, k))
        b_spec = pl.BlockSpec((tn, tk), lambda i, j, k: (j, k))
        dims = _NT
    else:
        a_spec = pl.BlockSpec((tk, tm), lambda i, j, k: (k, i))
        b_spec = pl.BlockSpec((tk, tn), lambda i, j, k: (k, j))
        dims = _TN

    def kern(a_ref, b_ref, o_ref, acc):
        k = pl.program_id(2)

        @pl.when(k == 0)
        def _():
            acc[...] = jnp.zeros(acc.shape, F32)

        acc[...] += lax.dot_general(a_ref[...].astype(BF16), b_ref[...].astype(BF16), (dims, ((), ())),
                                    preferred_element_type=F32)

        @pl.when(k == nk - 1)
        def _():
            o_ref[...] = acc[...].astype(o_ref.dtype)

    return pl.pallas_call(
        kern, name=name, grid=(M // tm, N // tn, nk),
        in_specs=[a_spec, b_spec],
        out_specs=pl.BlockSpec((tm, tn), lambda i, j, k: (i, j)),
        out_shape=jax.ShapeDtypeStruct((M, N), out_dtype),
        scratch_shapes=[pltpu.VMEM((tm, tn), F32)],
        compiler_params=_cparams(("arbitrary", "arbitrary", "arbitrary")),
    )(a, b)


def _flash_fwd(name, q, k, v, H, rep, scale, L):
    tq = min(256, L)
    nq = L // tq
    KC = min(2048, L)
    nkc = L // KC

    def kern(q_ref, k_ref, v_ref, o_ref, lse_ref):
        qb = q_ref[...]
        m = jnp.full((tq, 1), -1e30, F32)
        l = jnp.zeros((tq, 1), F32)
        acc = jnp.zeros((tq, LANE), F32)
        for c in range(nkc):
            kb = k_ref[c * KC:(c + 1) * KC, :]
            vb = v_ref[c * KC:(c + 1) * KC, :]
            s = lax.dot_general(qb, kb, (_NT, ((), ())), preferred_element_type=F32) * scale
            mn = jnp.maximum(m, jnp.max(s, axis=-1, keepdims=True))
            al = jnp.exp(m - mn)
            p = jnp.exp(s - mn)
            l = al * l + jnp.sum(p, axis=-1, keepdims=True)
            acc = al * acc + lax.dot_general(p.astype(BF16), vb, (_NN, ((), ())), preferred_element_type=F32)
            m = mn
        o_ref[...] = acc / l
        lse_ref[...] = m + jnp.log(l)

    return pl.pallas_call(
        kern, name=name, grid=(H, nq),
        in_specs=[pl.BlockSpec((tq, LANE), lambda h, i: (i, h)),
                  pl.BlockSpec((L, LANE), lambda h, i: (0, h // rep)),
                  pl.BlockSpec((L, LANE), lambda h, i: (0, h // rep))],
        out_specs=[pl.BlockSpec((tq, LANE), lambda h, i: (i, h)),
                   pl.BlockSpec((tq, 1), lambda h, i: (h * nq + i, 0))],
        out_shape=[jax.ShapeDtypeStruct((L, H * LANE), F32), jax.ShapeDtypeStruct((H * L, 1), F32)],
        compiler_params=_cparams(("arbitrary", "arbitrary")),
    )(q, k, v)


def _flash_bwd(name, q, k, v, o, lse, do, H, rep, scale, L):
    tq = min(256, L)
    nq = L // tq
    KC = min(2048, L)
    nkc = L // KC
    Hkv = H // rep

    def kern(q_ref, k_ref, v_ref, o_ref, lse_ref, do_ref, dq_ref, dk_ref, dv_ref):
        h = pl.program_id(0)
        i = pl.program_id(1)

        @pl.when((i == 0) & (h % rep == 0))
        def _():
            dk_ref[...] = jnp.zeros(dk_ref.shape, F32)
            dv_ref[...] = jnp.zeros(dv_ref.shape, F32)

        qb = q_ref[...]
        do = do_ref[...]
        dob = do.astype(BF16)
        delta = jnp.sum(do * o_ref[...], axis=-1, keepdims=True)
        lse = lse_ref[...]
        dq = jnp.zeros((tq, LANE), F32)
        for c in range(nkc):
            sl = slice(c * KC, (c + 1) * KC)
            kb = k_ref[sl, :]
            vb = v_ref[sl, :]
            s = lax.dot_general(qb, kb, (_NT, ((), ())), preferred_element_type=F32) * scale
            p = jnp.exp(s - lse)
            dp = lax.dot_general(dob, vb, (_NT, ((), ())), preferred_element_type=F32)
            ds = (p * (dp - delta) * scale).astype(BF16)
            dq = dq + lax.dot_general(ds, kb, (_NN, ((), ())), preferred_element_type=F32)
            dk_ref[sl, :] += lax.dot_general(ds, qb, (_TN, ((), ())), preferred_element_type=F32)
            dv_ref[sl, :] += lax.dot_general(p.astype(BF16), dob, (_TN, ((), ())), preferred_element_type=F32)
        dq_ref[...] = dq

    return pl.pallas_call(
        kern, name=name, grid=(H, nq),
        in_specs=[pl.BlockSpec((tq, LANE), lambda h, i: (i, h)),
                  pl.BlockSpec((L, LANE), lambda h, i: (0, h // rep)),
                  pl.BlockSpec((L, LANE), lambda h, i: (0, h // rep)),
                  pl.BlockSpec((tq, LANE), lambda h, i: (i, h)),
                  pl.BlockSpec((tq, 1), lambda h, i: (h * nq + i, 0)),
                  pl.BlockSpec((tq, LANE), lambda h, i: (i, h))],
        out_specs=[pl.BlockSpec((tq, LANE), lambda h, i: (i, h)),
                   pl.BlockSpec((L, LANE), lambda h, i: (0, h // rep)),
                   pl.BlockSpec((L, LANE), lambda h, i: (0, h // rep))],
        out_shape=[jax.ShapeDtypeStruct((L, H * LANE), F32), jax.ShapeDtypeStruct((L, Hkv * LANE), F32),
                   jax.ShapeDtypeStruct((L, Hkv * LANE), F32)],
        compiler_params=_cparams(("arbitrary", "arbitrary")),
    )(q, k, v, o, lse, do)


def _shift_dn(x, first_row):
    row = lax.broadcasted_iota(jnp.int32, x.shape, 0)
    return jnp.where(row == 0, first_row, pltpu.roll(x, 1, 0))


def _shift_up(x, last_row):
    n = x.shape[0]
    row = lax.broadcasted_iota(jnp.int32, x.shape, 0)
    return jnp.where(row == n - 1, last_row, pltpu.roll(x, n - 1, 0))


def _halo_specs(ndim, lead, T, tc, cb0, L):
    r8 = T // 8
    last8 = L // 8 - 1
    if ndim == 2:
        return [pl.BlockSpec((T, tc), lambda j, i: (i, cb0 + j)),
                pl.BlockSpec((8, tc), lambda j, i: (jnp.maximum(i * r8 - 1, 0), cb0 + j)),
                pl.BlockSpec((8, tc), lambda j, i: (jnp.minimum((i + 1) * r8, last8), cb0 + j))]
    return [pl.BlockSpec((None, T, tc), lambda j, i: (lead, i, cb0 + j)),
            pl.BlockSpec((None, 8, tc), lambda j, i: (lead, jnp.maximum(i * r8 - 1, 0), cb0 + j)),
            pl.BlockSpec((None, 8, tc), lambda j, i: (lead, jnp.minimum((i + 1) * r8, last8), cb0 + j))]


def _conv_fwd(name, x, col0, C, w8, act, L, tc):
    T = min(256, L)
    nt = L // T
    cb0 = col0 // tc

    def kern(x_ref, xp_ref, xn_ref, w_ref, o_ref):
        i = pl.program_id(1)
        x = x_ref[...]
        w = w_ref[...]
        pr = jnp.where(i == 0, 0.0, xp_ref[7:8, :])
        nr = jnp.where(i == nt - 1, 0.0, xn_ref[0:1, :])
        pre = _shift_dn(x, pr) * w[0:1] + x * w[1:2] + _shift_up(x, nr) * w[2:3] + w[3:4]
        o_ref[...] = _silu(pre) if act else pre

    return pl.pallas_call(
        kern, name=name, grid=(C // tc, nt),
        in_specs=_halo_specs(2, None, T, tc, cb0, L) + [pl.BlockSpec((8, tc), lambda j, i: (0, j))],
        out_specs=pl.BlockSpec((T, tc), lambda j, i: (i, j)),
        out_shape=jax.ShapeDtypeStruct((L, C), F32),
        compiler_params=_cparams(("arbitrary", "arbitrary")),
    )(x, x, x, w8)


def _conv_bwd(name, x, col0, C, w8, act, gs, L, tc):
    T = min(256, L)
    nt = L // T
    cb0 = col0 // tc
    ng = len(gs)

    def dact(pre, g):
        if not act:
            return g
        s = _sigmoid(pre)
        return g * (s * (1.0 + pre * (1.0 - s)))

    def kern(*refs):
        x_ref, xp_ref, xn_ref, w_ref = refs[:4]
        g_refs = refs[4:4 + 3 * ng]
        dx_ref, dw_ref = refs[4 + 3 * ng:]
        i = pl.program_id(1)
        first = i == 0
        last = i == nt - 1
        x = x_ref[...]
        w = w_ref[...]
        w0, w1, w2, b = w[0:1], w[1:2], w[2:3], w[3:4]
        g = g_refs[0][...]
        gp = g_refs[1][7:8, :]
        gn = g_refs[2][0:1, :]
        for n in range(1, ng):
            g = g + g_refs[3 * n][...]
            gp = gp + g_refs[3 * n + 1][7:8, :]
            gn = gn + g_refs[3 * n + 2][0:1, :]
        pr = jnp.where(first, 0.0, xp_ref[7:8, :])
        pr2 = jnp.where(first, 0.0, xp_ref[6:7, :])
        nr = jnp.where(last, 0.0, xn_ref[0:1, :])
        nr2 = jnp.where(last, 0.0, xn_ref[1:2, :])
        xm1 = _shift_dn(x, pr)
        xp1 = _shift_up(x, nr)
        pre = xm1 * w0 + x * w1 + xp1 * w2 + b
        dpre = dact(pre, g)
        pre_m1 = pr2 * w0 + pr * w1 + x[0:1] * w2 + b
        dpre_m1 = jnp.where(first, 0.0, dact(pre_m1, gp))
        pre_T = x[T - 1:T] * w0 + nr * w1 + nr2 * w2 + b
        dpre_T = jnp.where(last, 0.0, dact(pre_T, gn))
        dx_ref[...] = _shift_up(dpre, dpre_T) * w0 + dpre * w1 + _shift_dn(dpre, dpre_m1) * w2
        row = lax.broadcasted_iota(jnp.int32, (8, tc), 0)
        dw = (jnp.where(row == 0, jnp.sum(dpre * xm1, axis=0, keepdims=True), 0.0)
              + jnp.where(row == 1, jnp.sum(dpre * x, axis=0, keepdims=True), 0.0)
              + jnp.where(row == 2, jnp.sum(dpre * xp1, axis=0, keepdims=True), 0.0)
              + jnp.where(row == 3, jnp.sum(dpre, axis=0, keepdims=True), 0.0))

        @pl.when(first)
        def _():
            dw_ref[...] = jnp.zeros((8, tc), F32)

        dw_ref[...] += dw

    g_specs, g_args = [], []
    for arr, lead in gs:
        g_specs += _halo_specs(arr.ndim, lead, T, tc, 0, L)
        g_args += [arr, arr, arr]
    return pl.pallas_call(
        kern, name=name, grid=(C // tc, nt),
        in_specs=_halo_specs(2, None, T, tc, cb0, L) + [pl.BlockSpec((8, tc), lambda j, i: (0, j))] + g_specs,
        out_specs=[pl.BlockSpec((T, tc), lambda j, i: (i, j)), pl.BlockSpec((8, tc), lambda j, i: (0, j))],
        out_shape=[jax.ShapeDtypeStruct((L, C), F32), jax.ShapeDtypeStruct((8, C), F32)],
        compiler_params=_cparams(("arbitrary", "arbitrary")),
    )(x, x, x, w8, *g_args)


def _glu_fwd(name, gu, L):
    T = min(256, L)
    tc = 1408
    ncb = D_FF // tc

    def kern(g_ref, u_ref, o_ref):
        o_ref[...] = (_silu(g_ref[...]) * u_ref[...]).astype(BF16)

    return pl.pallas_call(
        kern, name=name, grid=(L // T, ncb),
        in_specs=[pl.BlockSpec((T, tc), lambda i, j: (i, j)), pl.BlockSpec((T, tc), lambda i, j: (i, j + ncb))],
        out_specs=pl.BlockSpec((T, tc), lambda i, j: (i, j)),
        out_shape=jax.ShapeDtypeStruct((L, D_FF), BF16),
        compiler_params=_cparams(("arbitrary", "arbitrary")),
    )(gu, gu)


def _glu_bwd(name, gu, da, L):
    T = min(256, L)
    tc = 1408
    ncb = D_FF // tc

    def kern(g_ref, u_ref, da_ref, o_ref):
        half = pl.program_id(1)
        g = g_ref[...]
        s = _sigmoid(g)
        d = da_ref[...]
        dg = d * u_ref[...] * (s * (1.0 + g * (1.0 - s)))
        du = d * (g * s)
        o_ref[...] = jnp.where(half == 0, dg, du)

    return pl.pallas_call(
        kern, name=name, grid=(L // T, 2, ncb),
        in_specs=[pl.BlockSpec((T, tc), lambda i, h, j: (i, j)),
                  pl.BlockSpec((T, tc), lambda i, h, j: (i, j + ncb)),
                  pl.BlockSpec((T, tc), lambda i, h, j: (i, j))],
        out_specs=pl.BlockSpec((T, tc), lambda i, h, j: (i, h * ncb + j)),
        out_shape=jax.ShapeDtypeStruct((L, 2 * D_FF), F32),
        compiler_params=_cparams(("arbitrary", "arbitrary", "arbitrary")),
    )(gu, gu, da)


def _masks(Q, rev):
    ri = lax.broadcasted_iota(jnp.int32, (Q, Q), 0)
    ci = lax.broadcasted_iota(jnp.int32, (Q, Q), 1)
    diff = (ri - ci) * (1 - 2 * rev)
    return diff >= 0, diff > 0


def _lane_pick(v, sel):
    return jnp.sum(v * sel, axis=-1, keepdims=True)


def _head_rows(v_all, Q, rev):
    r = lax.broadcasted_iota(jnp.int32, (HEADS * Q, LANE), 0)
    l = lax.broadcasted_iota(jnp.int32, (HEADS * Q, LANE), 1)
    pick = jnp.zeros((HEADS * Q, LANE), F32)
    for h in range(HEADS):
        pick = jnp.where((r >= h * Q) & (r < (h + 1) * Q) & (l == rev * 4 + h), 1.0, pick)
    return _nt_hi(pick, v_all)


def _ssd_chunk(S, x, B, C, dtraw, alog, dtb, rev):
    Q = dtraw.shape[0]
    incl, _ = _masks(Q, rev)
    tri = incl.astype(F32)
    dt = _softplus(dtraw + dtb)
    a_all = dt * (-jnp.exp(alog))
    acum_all = _nn_hi(tri, a_all)
    total_all = jnp.sum(a_all, axis=0, keepdims=True)
    lane = lax.broadcasted_iota(jnp.int32, (1, LANE), 1)
    rows_all = _head_rows(acum_all, Q, rev)
    ys, Sn = [], []
    for h in range(HEADS):
        g = h // 2
        sel = (lane == rev * 4 + h).astype(F32)
        acum = _lane_pick(acum_all, sel)
        dth = _lane_pick(dt, sel)
        tot = _lane_pick(total_all, sel)
        seg = acum - rows_all[h * Q:(h + 1) * Q, :]
        decay = jnp.exp(jnp.where(incl, seg, -1e30))
        xdt = x[h] * dth
        scores = _nt(C[g], B[g]) * decay
        y_diag = _nn(scores, xdt)
        states = _tn(xdt, B[g] * jnp.exp(tot - acum))
        y_off = _nt(C[g], S[h]) * jnp.exp(acum)
        ys.append(y_diag + y_off)
        Sn.append(S[h] * jnp.exp(tot) + states)
    return ys, Sn


def _inv_unit_raw(Lm):
    Q = Lm.shape[0]
    ri = lax.broadcasted_iota(jnp.int32, (Q, Q), 0)
    ci = lax.broadcasted_iota(jnp.int32, (Q, Q), 1)
    X = (ri == ci).astype(F32) - Lm
    P = _raw_dot(Lm, Lm, _NN, 'high')
    n = 2
    while n < Q:
        X = X + _raw_dot(X, P, _NN, 'high')
        n *= 2
        if n < Q:
            P = _raw_dot(P, P, _NN, 'high')
    return X


@jax.custom_vjp
def _inv_unit(Lm):
    return _inv_unit_raw(Lm)


def _inv_unit_f(Lm):
    T = _inv_unit_raw(Lm)
    return T, T


def _inv_unit_b(T, g):
    return (-_raw_dot(_raw_dot(T, g, _TN, 'high'), T, _NT, 'high'),)


_inv_unit.defvjp(_inv_unit_f, _inv_unit_b)


def _delta_chunk(S, q, k, v, braw, araw, alog, dtb, rev):
    Q = braw.shape[0]
    incl, strict = _masks(Q, rev)
    tri = incl.astype(F32)
    beta_all = _sigmoid(braw)
    g_all = -jnp.exp(alog) * _softplus(araw + dtb)
    G_all = _nn_hi(tri, g_all)
    Gtot_all = jnp.sum(g_all, axis=0, keepdims=True)
    lane = lax.broadcasted_iota(jnp.int32, (1, LANE), 1)
    rows_all = _head_rows(G_all, Q, rev)
    os_, Sn = [], []
    for h in range(HEADS):
        sel = (lane == rev * 4 + h).astype(F32)
        G = _lane_pick(G_all, sel)
        bt = _lane_pick(beta_all, sel)
        Gtot = _lane_pick(Gtot_all, sel)
        seg = G - rows_all[h * Q:(h + 1) * Q, :]
        decay = jnp.exp(jnp.where(incl, seg, -1e30))
        qn = q[h] * lax.rsqrt(jnp.sum(q[h] * q[h], axis=-1, keepdims=True) + 1e-6)
        kn = k[h] * lax.rsqrt(jnp.sum(k[h] * k[h], axis=-1, keepdims=True) + 1e-6)
        qc = qn * (HD ** -0.5)
        kb = kn * bt
        Lm = jnp.where(strict, _nt(kb, kn) * decay, 0.0)
        T = _inv_unit(Lm)
        eG = jnp.exp(G)
        u = _nn(T, v[h] * bt)
        w = _nn(T, kb * eG)
        qk = _nt(qc, kn) * decay
        v_new = u - _nn(w, S[h])
        os_.append(_nn(qc * eG, S[h]) + _nn(qk, v_new))
        Sn.append(S[h] * jnp.exp(Gtot) + _tn(kn * jnp.exp(Gtot - G), v_new))
    return os_, Sn


def _seq_pieces(ref, r0, Q, splits):
    if splits is None:
        return ref[r0:r0 + Q, :]
    return [[ref[r0:r0 + Q, o + w * t:o + w * (t + 1)] for t in range(n)] for o, w, n in splits]


def _store_pieces(ref, r0, Q, splits, vals, extra=None):
    if splits is None:
        ref[r0:r0 + Q, :] = vals
        return
    for g, (o, w, n) in enumerate(splits):
        for t in range(n):
            v = vals[g][t]
            if extra is not None and g == 0:
                v = v + extra[r0:r0 + Q, o + w * t:o + w * (t + 1)]
            ref[r0:r0 + Q, o + w * t:o + w * (t + 1)] = v


def _flat(ins):
    out = []
    for v in ins:
        if isinstance(v, list):
            out.extend(v)
        else:
            out.append(v)
    return out


def _scan_fwd(name, chunk_fn, seqs, rows, Q, L, CH):
    nc = L // Q
    nb = nc // CH
    ns, nr = len(seqs), len(rows)
    BQ = Q * CH

    def kern(*refs):
        s_refs = (refs[:ns], refs[ns:2 * ns])
        r_refs = refs[2 * ns:2 * ns + nr]
        y_refs = refs[2 * ns + nr:2 * ns + nr + 2]
        ss_refs = refs[2 * ns + nr + 2:2 * ns + nr + 4]
        S_scr = refs[2 * ns + nr + 4]
        i = pl.program_id(0)

        @pl.when(i == 0)
        def _():
            S_scr[...] = jnp.zeros(S_scr.shape, F32)

        rws = [r[...] for r in r_refs]
        for d in (0, 1):
            S = [S_scr[d, HD * h:HD * (h + 1), :] for h in range(HEADS)]
            for cc in range(CH):
                c = cc if d == 0 else CH - 1 - cc
                for h in range(HEADS):
                    ss_refs[d][c, HD * h:HD * (h + 1), :] = S[h]
                ins = [_seq_pieces(r, c * Q, Q, sp) for r, (_, _, _, sp) in zip(s_refs[d], seqs)]
                ys, S = chunk_fn(S, *_flat(ins), *rws, d)
                for h in range(HEADS):
                    y_refs[d][c * Q:(c + 1) * Q, HD * h:HD * (h + 1)] = ys[h]
            for h in range(HEADS):
                S_scr[d, HD * h:HD * (h + 1), :] = S[h]

    fwd_specs = [pl.BlockSpec((BQ, w), functools.partial(lambda i, cb: (i, cb), cb=cb)) for _, w, cb, _ in seqs]
    rev_specs = [pl.BlockSpec((BQ, w), functools.partial(lambda i, cb: (nb - 1 - i, cb), cb=cb))
                 for _, w, cb, _ in seqs]
    arrs = [a for a, _, _, _ in seqs]
    return pl.pallas_call(
        kern, name=name, grid=(nb,),
        in_specs=fwd_specs + rev_specs + [pl.BlockSpec((1, LANE), lambda i: (0, 0)) for _ in rows],
        out_specs=[pl.BlockSpec((BQ, GROUP_W), lambda i: (i, 0)),
                   pl.BlockSpec((BQ, GROUP_W), lambda i: (nb - 1 - i, 0)),
                   pl.BlockSpec((CH, GROUP_W, HD), lambda i: (i, 0, 0)),
                   pl.BlockSpec((CH, GROUP_W, HD), lambda i: (nb - 1 - i, 0, 0))],
        out_shape=[jax.ShapeDtypeStruct((L, GROUP_W), F32)] * 2 + [jax.ShapeDtypeStruct((nc, GROUP_W, HD), F32)] * 2,
        scratch_shapes=[pltpu.VMEM((2, GROUP_W, HD), F32)],
        compiler_params=_cparams(("arbitrary",)),
    )(*arrs, *arrs, *rows)


def _scan_bwd(name, chunk_fn, seqs, rows, ssaves, dy, extra, Q, L, CH):
    nc = L // Q
    nb = nc // CH
    BQ = Q * CH
    ns, nr = len(seqs), len(rows)
    has_extra = extra is not None

    def kern(*refs):
        s_refs = (refs[:ns], refs[ns:2 * ns])
        pos = 2 * ns
        r_refs = refs[pos:pos + nr]
        pos += nr
        ss_refs = refs[pos:pos + 2]
        dy_refs = refs[pos + 2:pos + 4]
        pos += 4
        ex_ref = refs[pos] if has_extra else None
        pos += 1 if has_extra else 0
        ds_refs = (refs[pos:pos + ns], refs[pos + ns:pos + 2 * ns])
        pos += 2 * ns
        dr_refs = refs[pos:pos + nr]
        dS_scr = refs[pos + nr]
        i = pl.program_id(0)

        @pl.when(i == 0)
        def _():
            dS_scr[...] = jnp.zeros(dS_scr.shape, F32)
            for r in dr_refs:
                r[...] = jnp.zeros(r.shape, F32)

        rws = [r[...] for r in r_refs]
        dr_acc = [jnp.zeros((1, LANE), F32) for _ in rows]
        for d in (0, 1):
            dS = [dS_scr[d, HD * h:HD * (h + 1), :] for h in range(HEADS)]
            for cc in range(CH):
                c = CH - 1 - cc if d == 0 else cc
                S = [ss_refs[d][c, HD * h:HD * (h + 1), :] for h in range(HEADS)]
                dys = [dy_refs[d][c * Q:(c + 1) * Q, HD * h:HD * (h + 1)] for h in range(HEADS)]
                ins = [_seq_pieces(r, c * Q, Q, sp) for r, (_, _, _, sp) in zip(s_refs[d], seqs)]
                _, vjp = jax.vjp(
                    functools.partial(lambda S_, ins_, rws_, d_: chunk_fn(S_, *_flat(ins_), *rws_, d_), d_=d),
                    S, ins, rws)
                dS, dins, drws = vjp((dys, dS))
                for n_, (r, (_, _, _, sp)) in enumerate(zip(ds_refs[d], seqs)):
                    _store_pieces(r, c * Q, Q, sp, dins[n_],
                                  extra=ex_ref if (has_extra and d == 0 and n_ == 0) else None)
                dr_acc = [a + g for a, g in zip(dr_acc, drws)]
            for h in range(HEADS):
                dS_scr[d, HD * h:HD * (h + 1), :] = dS[h]
        for r, g in zip(dr_refs, dr_acc):
            r[...] += g

    def blk(shape, rev, cb=0):
        nd = len(shape)
        if rev:
            return pl.BlockSpec(shape, lambda i: (i, cb) + (0,) * (nd - 2))
        return pl.BlockSpec(shape, lambda i: (nb - 1 - i, cb) + (0,) * (nd - 2))

    arrs = [a for a, _, _, _ in seqs]
    in_specs = [blk((BQ, w), False, cb) for _, w, cb, _ in seqs] + [blk((BQ, w), True, cb) for _, w, cb, _ in seqs]
    in_specs += [pl.BlockSpec((1, LANE), lambda i: (0, 0)) for _ in rows]
    in_specs += [blk((CH, GROUP_W, HD), False), blk((CH, GROUP_W, HD), True),
                 blk((BQ, GROUP_W), False), blk((BQ, GROUP_W), True)]
    args = arrs + arrs + list(rows) + list(ssaves) + [dy, dy]
    if has_extra:
        in_specs.append(blk((BQ, GROUP_W), False))
        args.append(extra)
    res = pl.pallas_call(
        kern, name=name, grid=(nb,),
        in_specs=in_specs,
        out_specs=[blk((BQ, w), False) for _, w, _, _ in seqs] + [blk((BQ, w), True) for _, w, _, _ in seqs]
        + [pl.BlockSpec((1, LANE), lambda i: (0, 0)) for _ in rows],
        out_shape=[jax.ShapeDtypeStruct((L, w), F32) for _, w, _, _ in seqs] * 2
        + [jax.ShapeDtypeStruct((1, LANE), F32) for _ in rows],
        scratch_shapes=[pltpu.VMEM((2, GROUP_W, HD), F32)],
        compiler_params=_cparams(("arbitrary",)),
    )(*args)
    return list(res[:ns]), list(res[ns:2 * ns]), list(res[2 * ns:])


def _loss_call(y, tgt, L):
    T = min(256, L)

    def kern(y_ref, t_ref, dy_ref, l_ref):
        i = pl.program_id(0)
        e = y_ref[...] - t_ref[...]
        dy_ref[...] = e * (1.0 / D_MODEL)

        @pl.when(i == 0)
        def _():
            l_ref[...] = jnp.zeros(l_ref.shape, F32)

        part = 0.5 * jnp.sum(jnp.sum(e * e, axis=-1, keepdims=True) * (1.0 / D_MODEL), axis=0, keepdims=True)
        l_ref[...] += jnp.broadcast_to(part, l_ref.shape)

    return pl.pallas_call(
        kern, name="loss_head", grid=(L // T,),
        in_specs=[_spec2(T, D_MODEL), _spec2(T, D_MODEL)],
        out_specs=[_spec2(T, D_MODEL), pl.BlockSpec((8, LANE), lambda i: (0, 0))],
        out_shape=[jax.ShapeDtypeStruct((L, D_MODEL), F32), jax.ShapeDtypeStruct((8, LANE), F32)],
        compiler_params=_cparams(("arbitrary",)),
    )(y, tgt)


_ANY = pl.BlockSpec(memory_space=pl.ANY)


def _coords():
    return lax.axis_index("x"), lax.axis_index("y"), lax.axis_index("c")


def _gather_chips(flat):
    def body(src, out, send_sems, recv_sems, lsem):
        x, y, c = _coords()
        me = 2 * x + y
        peers = [(1 - x, y), (x, 1 - y), (1 - x, 1 - y)]
        local = pltpu.make_async_copy(src, out.at[me], lsem)
        local.start()
        sends = [pltpu.make_async_remote_copy(src_ref=src, dst_ref=out.at[me], send_sem=send_sems.at[k],
                                              recv_sem=recv_sems.at[k], device_id=(px, py, c), device_id_type=MESH)
                 for k, (px, py) in enumerate(peers)]
        for s in sends:
            s.start()
        for k, (px, py) in enumerate(peers):
            pltpu.make_async_remote_copy(src_ref=src, dst_ref=out.at[2 * px + py], send_sem=send_sems.at[k],
                                         recv_sem=recv_sems.at[k], device_id=(px, py, c),
                                         device_id_type=MESH).wait_recv()
        for s in sends:
            s.wait_send()
        local.wait()

    return pl.pallas_call(
        body, name="gather_weights", in_specs=[_ANY], out_specs=_ANY,
        out_shape=jax.ShapeDtypeStruct((4,) + flat.shape, flat.dtype),
        scratch_shapes=[pltpu.SemaphoreType.DMA((3,)), pltpu.SemaphoreType.DMA((3,)), pltpu.SemaphoreType.DMA(())],
    )(flat)


def _scatter_grads(G, gs):
    def body(g_ref, gs_ref, out, outs, send_sems, recv_sems, ssend, srecv, lsems):
        x, y, c = _coords()
        me = 2 * x + y
        dev = 4 * x + 2 * y + c
        peers = [(1 - x, y), (x, 1 - y), (1 - x, 1 - y)]
        loc = [pltpu.make_async_copy(g_ref.at[me], out.at[me], lsems.at[0]),
               pltpu.make_async_copy(gs_ref, outs.at[dev], lsems.at[1])]
        for l_ in loc:
            l_.start()
        sends = [pltpu.make_async_remote_copy(src_ref=g_ref.at[2 * px + py], dst_ref=out.at[me],
                                              send_sem=send_sems.at[k], recv_sem=recv_sems.at[k],
                                              device_id=(px, py, c), device_id_type=MESH)
                 for k, (px, py) in enumerate(peers)]
        others = []
        for mask in range(1, 8):
            px, py, pc = x ^ (mask >> 2), y ^ ((mask >> 1) & 1), c ^ (mask & 1)
            others.append((px, py, pc))
            sends.append(pltpu.make_async_remote_copy(src_ref=gs_ref, dst_ref=outs.at[dev],
                                                      send_sem=ssend.at[mask - 1], recv_sem=srecv.at[mask - 1],
                                                      device_id=(px, py, pc), device_id_type=MESH))
        for s in sends:
            s.start()
        for k, (px, py) in enumerate(peers):
            pltpu.make_async_remote_copy(src_ref=g_ref.at[me], dst_ref=out.at[2 * px + py],
                                         send_sem=send_sems.at[k], recv_sem=recv_sems.at[k],
                                         device_id=(px, py, c), device_id_type=MESH).wait_recv()
        for k, (px, py, pc) in enumerate(others):
            pltpu.make_async_remote_copy(src_ref=gs_ref, dst_ref=outs.at[4 * px + 2 * py + pc],
                                         send_sem=ssend.at[k], recv_sem=srecv.at[k],
                                         device_id=(px, py, pc), device_id_type=MESH).wait_recv()
        for s in sends:
            s.wait_send()
        for l_ in loc:
            l_.wait()

    return pl.pallas_call(
        body, name="scatter_grads", in_specs=[_ANY, _ANY], out_specs=[_ANY, _ANY],
        out_shape=[jax.ShapeDtypeStruct(G.shape, G.dtype), jax.ShapeDtypeStruct((8,) + gs.shape, gs.dtype)],
        scratch_shapes=[pltpu.SemaphoreType.DMA((3,)), pltpu.SemaphoreType.DMA((3,)),
                        pltpu.SemaphoreType.DMA((7,)), pltpu.SemaphoreType.DMA((7,)),
                        pltpu.SemaphoreType.DMA((2,))],
    )(G, gs)


def _swap_cores(part):
    def body(src, out, send_sem, recv_sem, lsem):
        x, y, c = _coords()
        local = pltpu.make_async_copy(src, out.at[c], lsem)
        local.start()
        send = pltpu.make_async_remote_copy(src_ref=src, dst_ref=out.at[c], send_sem=send_sem, recv_sem=recv_sem,
                                            device_id=(x, y, 1 - c), device_id_type=MESH)
        send.start()
        pltpu.make_async_remote_copy(src_ref=src, dst_ref=out.at[1 - c], send_sem=send_sem, recv_sem=recv_sem,
                                     device_id=(x, y, 1 - c), device_id_type=MESH).wait_recv()
        send.wait_send()
        local.wait()

    return pl.pallas_call(
        body, name="swap_cores", in_specs=[_ANY], out_specs=_ANY,
        out_shape=jax.ShapeDtypeStruct((2,) + part.shape, part.dtype),
        scratch_shapes=[pltpu.SemaphoreType.DMA(()), pltpu.SemaphoreType.DMA(()), pltpu.SemaphoreType.DMA(())],
    )(part)


def _sum_slots(recv, tr):
    n, R, W = recv.shape

    def kern(r_ref, o_ref):
        acc = r_ref[0]
        for s in range(1, n):
            acc = acc + r_ref[s]
        o_ref[...] = acc

    return pl.pallas_call(
        kern, name="sum_chip_grads", grid=(R // tr,),
        in_specs=[pl.BlockSpec((n, tr, W), lambda i: (0, i, 0))],
        out_specs=pl.BlockSpec((tr, W), lambda i: (i, 0)),
        out_shape=jax.ShapeDtypeStruct((R, W), F32),
        compiler_params=_cparams(("arbitrary",)),
    )(recv)


def _adamw_call(name, slots, w, m, v, tr):
    n, R, W = slots.shape

    def kern(s_ref, w_ref, m_ref, v_ref, g_ref, d_ref, nm_ref, nv_ref):
        g = s_ref[0]
        for s in range(1, n):
            g = g + s_ref[s]
        m_ = ADAM_B1 * m_ref[...] + (1.0 - ADAM_B1) * g
        v_ = ADAM_B2 * v_ref[...] + (1.0 - ADAM_B2) * (g * g)
        m_hat = m_ / (1.0 - ADAM_B1 ** ADAM_STEP)
        v_hat = v_ / (1.0 - ADAM_B2 ** ADAM_STEP)
        g_ref[...] = g
        d_ref[...] = -ADAM_LR * (m_hat / (jnp.sqrt(v_hat) + ADAM_EPS) + ADAM_WD * w_ref[...])
        nm_ref[...] = m_
        nv_ref[...] = v_

    blk = pl.BlockSpec((tr, W), lambda i: (i, 0))
    return pl.pallas_call(
        kern, name=name, grid=(R // tr,),
        in_specs=[pl.BlockSpec((n, tr, W), lambda i: (0, i, 0)), blk, blk, blk],
        out_specs=[blk, blk, blk, blk],
        out_shape=[jax.ShapeDtypeStruct((R, W), F32)] * 4,
        compiler_params=_cparams(("arbitrary",)),
    )(slots, w, m, v)


def _pack(arrs, width, row_mult):
    flat = jnp.concatenate([a.reshape(-1) for a in arrs])
    n = flat.shape[0]
    rows = -(-n // width)
    rows = -(-rows // row_mult) * row_mult
    return jnp.pad(flat, (0, rows * width - n)).reshape(rows, width)


def _unpack(buf, shapes):
    flat = buf.reshape(-1)
    out, pos = [], 0
    for s in shapes:
        n = int(np.prod(s))
        out.append(flat[pos:pos + n].reshape(s))
        pos += n
    return out


def _rope_angles(L, rot_dim):
    rows = L // GRID_W
    row = jnp.repeat(jnp.arange(rows), GRID_W).astype(F32)
    col = jnp.tile(jnp.arange(GRID_W), rows).astype(F32)
    sec = rot_dim // 2
    inv_freq = ROPE_BASE ** (-jnp.arange(0, sec, 2, dtype=F32) / sec)
    ang_r = row[:, None] * inv_freq
    ang_c = col[:, None] * inv_freq
    ang = jnp.concatenate([ang_r, ang_r, ang_c, ang_c], axis=-1)
    return jnp.cos(ang), jnp.sin(ang)


def _rot_matrix(r):
    R = np.zeros((r, r), np.float32)
    q = r // 4
    for s in range(2):
        for t in range(q):
            lo = s * (r // 2) + t
            hi = lo + q
            R[hi, lo] = -1.0
            R[lo, hi] = 1.0
    return R


def _place_tables(L, cos, sin, width, offsets):
    r = cos.shape[1]
    Rm = np.zeros((width, width), np.float32)
    R = _rot_matrix(r)
    cs, ss, pos = [], [], 0
    for o in list(offsets) + [width]:
        if o > pos:
            cs.append(jnp.ones((L, o - pos), F32))
            ss.append(jnp.zeros((L, o - pos), F32))
        if o < width:
            cs.append(cos)
            ss.append(sin)
            Rm[o:o + r, o:o + r] = R
        pos = o + r
    return jnp.concatenate(cs, axis=1), jnp.concatenate(ss, axis=1), jnp.asarray(Rm)


def _head_mean_matrix(width, stride, n):
    M = np.zeros((width, width), np.float32)
    for o in range(0, width, stride):
        M[o:o + n, o:o + n] = 1.0 / n
    return jnp.asarray(M)


def _pad_heads(w, n_heads, real, padded, axis):
    parts = jnp.split(w, n_heads, axis=axis)
    padw = [(0, 0)] * w.ndim
    padw[axis] = (0, padded - real)
    return jnp.concatenate([jnp.pad(p, padw) for p in parts], axis=axis)


def _row128(v):
    v = v.reshape(1, -1)
    return jnp.pad(v, ((0, 0), (0, LANE - v.shape[1])))


def _conv_w8(w, b):
    C = w.shape[1]
    rows = [w, jnp.zeros((1, C), F32) if b is None else b.reshape(1, C), jnp.zeros((4, C), F32)]
    return jnp.concatenate(rows, axis=0)


def _build_layer(W):
    w_in = W['w_in']
    o = 0
    cols = {}
    for name, n in [('a_cq', A_Q_LORA), ('a_ckv', A_KV_LORA), ('a_kr', A_ROPE), ('b_q', 256), ('b_k', 128),
                    ('b_v', 128), ('c_z', 256), ('c_xbc', 512), ('c_dt', 8), ('d_qkv', 768), ('d_z', 256),
                    ('d_b', 8), ('d_a', 8)]:
        cols[name] = w_in[:, o:o + n]
        o += n
    padc = lambda a, lo, width: jnp.pad(a, ((0, 0), (lo, width - lo - a.shape[1])))
    pieces = {
        'b_q': _pad_heads(cols['b_q'], 4, HD, LANE, 1), 'c_xbc': cols['c_xbc'], 'a_cq': padc(cols['a_cq'], 0, 256),
        'b_k': _pad_heads(cols['b_k'], 2, HD, LANE, 1), 'd_qkv': cols['d_qkv'],
        'b_v': _pad_heads(cols['b_v'], 2, HD, LANE, 1), 'c_z': cols['c_z'], 'd_z': cols['d_z'],
        'a_ckv': cols['a_ckv'], 'a_kr': padc(cols['a_kr'], A_NOPE, LANE), 'c_dt': padc(cols['c_dt'], 0, LANE),
        'd_b': padc(cols['d_b'], 0, LANE), 'd_a': padc(cols['d_a'], 0, LANE),
        'pad': jnp.zeros((D_MODEL, LANE), F32)}
    out = {'w_in': jnp.concatenate([pieces[n] for n, _, _ in P_LAYOUT], axis=1)}
    out['a_q_norm'] = padc(W['a_q_norm'].reshape(1, -1), 0, 256)
    wuq = jnp.pad(W['a_w_uq'], ((0, 256 - A_Q_LORA), (0, 0)))
    out['a_w_uq'] = _pad_heads(wuq, 4, A_NOPE + A_ROPE, LANE, 1)
    out['a_kv_norm'] = W['a_kv_norm'].reshape(1, -1)
    ukv = W['a_w_ukv'].reshape(A_KV_LORA, HEADS, 2, HD)
    out['a_w_uk'] = _pad_heads(ukv[:, :, 0, :].reshape(A_KV_LORA, 256), 4, HD, LANE, 1)
    out['a_w_uv'] = _pad_heads(ukv[:, :, 1, :].reshape(A_KV_LORA, 256), 4, HD, LANE, 1)
    out['a_out_norm'] = _pad_heads(W['a_out_norm'].reshape(1, -1), 4, HD, LANE, 1)
    out['b_q_norm'] = _pad_heads(jnp.tile(W['b_q_norm'].reshape(1, -1), (1, 4)), 4, HD, LANE, 1)
    out['b_k_norm'] = _pad_heads(jnp.tile(W['b_k_norm'].reshape(1, -1), (1, 2)), 2, HD, LANE, 1)
    out['b_out_norm'] = _pad_heads(W['b_out_norm'].reshape(1, -1), 4, HD, LANE, 1)
    out['c_conv'] = _conv_w8(W['c_conv_w'], W['c_conv_b'])
    out['c_a_log'] = _row128(W['c_a_log'])
    out['c_dt_bias'] = _row128(W['c_dt_bias'])
    out['c_d_skip'] = jnp.repeat(W['c_d_skip'], HD).reshape(1, -1)
    out['c_out_norm'] = W['c_out_norm'].reshape(1, -1)
    out['d_conv'] = _conv_w8(W['d_conv_w'], None)
    out['d_a_log'] = _row128(W['d_a_log'])
    out['d_dt_bias'] = _row128(W['d_dt_bias'])
    out['d_out_norm'] = jnp.tile(W['d_out_norm'].reshape(1, -1), (1, 4))
    wo = W['w_out']
    out['w_out'] = jnp.concatenate([_pad_heads(wo[0:256], 4, HD, LANE, 0), _pad_heads(wo[256:512], 4, HD, LANE, 0),
                                    wo[512:1024]], axis=0)
    for n in ['pre_mix_norm', 'post_mix_norm', 'pre_ffn_norm', 'post_ffn_norm']:
        out[n] = W[n].reshape(1, -1)
    out['f_w_in'] = W['f_w_in']
    out['f_conv'] = _conv_w8(W['f_conv_w'], W['f_conv_b'])
    out['f_w_out'] = W['f_w_out']
    return out


def _fn_norm_in(a, p):
    return [_rms(a[0], p[0])]


def _fn_resid_norm2(a, p):
    x1 = a[0] + _rms(a[1], p[0])
    return [x1, _rms(x1, p[1])]


def _fn_resid_norm(a, p):
    return [a[0] + _rms(a[1], p[0])]


def _fn_a_prep(a, p):
    cq, ckv, kr, cosq, sinq, cosk, sink = a
    q_norm, w_uq, kv_norm, w_uk, w_uv, rq, rk = p
    q = _nn(_rms(cq, q_norm, A_Q_LORA), w_uq)
    q = q * cosq + _nn_h3(q, rq) * sinq
    kvn = _rms(ckv, kv_norm)
    kr_r = kr * cosk + _nn_h3(kr, rk) * sink
    kk = _nn(kvn, w_uk) + jnp.concatenate([kr_r] * HEADS, axis=1)
    return [q, kk, _nn(kvn, w_uv)]


def _fn_b_prep(a, p):
    q, k, v, cosq, sinq, cosk, sink = a
    q_norm, k_norm, mq, mk, rq, rk = p
    qn = q * lax.rsqrt(_nn_h3(q * q, mq) + EPS) * q_norm
    kn = k * lax.rsqrt(_nn_h3(k * k, mk) + EPS) * k_norm
    return [qn * cosq + _nn_h3(qn, rq) * sinq, kn * cosk + _nn_h3(kn, rk) * sink, v]


def _fn_mixer_post(a, p):
    oa, ob, yc0, yc1, xs, zc, od0, od1, zd = a
    a_norm, b_norm, dskip, c_norm, d_norm, m64 = p
    oc = _rms((yc0 + yc1 + xs * dskip) * _silu(zc), c_norm)
    od = od0 + od1
    odn = od * lax.rsqrt(_nn_h3(od * od, m64) + EPS) * d_norm * _silu(zd)
    return [jnp.concatenate([_rms(oa, a_norm, GROUP_W), _rms(ob, b_norm, GROUP_W), oc, odn], axis=1)]


def _fn_assemble(a, p):
    (dbq, dxbc, dcq, dbk, dqkv, dbv, dzc, dzd, dckv, dkr, ddt0, ddt1, db0, db1, da0, da1) = a
    return [jnp.concatenate([dbq, dxbc, dcq, dbk, dqkv, dbv, dzc, dzd, dckv, dkr, ddt0 + ddt1, db0 + db1,
                             da0 + da1, jnp.zeros_like(dckv)], axis=1)]


def _pspec(T, name):
    off, w = P_OFF[name]
    return _spec2(T, w, off // w)


def _layer_fwd(l, x, h, K, tabs, L, T, next_norm):
    n = f"l{l}_"
    sv = {'x': x, 'h': h}
    p = _mm(n + "in_proj", h, K['w_in'].astype(BF16), 'nn', F32, 512, 768, 1024)
    sv['p'] = p
    a_acts = [(p, _pspec(T, 'a_cq')), (p, _pspec(T, 'a_ckv')), (p, _pspec(T, 'a_kr')),
              (tabs['a_cq'], _spec2(T, 512)), (tabs['a_sq'], _spec2(T, 512)),
              (tabs['a_ck'], _spec2(T, LANE)), (tabs['a_sk'], _spec2(T, LANE))]
    a_pars = [K['a_q_norm'], K['a_w_uq'], K['a_kv_norm'], K['a_w_uk'], K['a_w_uv'], tabs['a_rq'], tabs['a_rk']]
    qa, ka, va = _tw_fwd(n + "a_prep", _fn_a_prep, a_acts, a_pars, [(512, BF16)] * 3, L, T)
    oa, lse_a = _flash_fwd(n + "a_attn", qa, ka, va, HEADS, 1, (A_NOPE + A_ROPE) ** -0.5, L)
    sv.update(a_acts=a_acts, a_pars=a_pars, qa=qa, ka=ka, va=va, oa=oa, lse_a=lse_a)
    b_acts = [(p, _pspec(T, 'b_q')), (p, _pspec(T, 'b_k')), (p, _pspec(T, 'b_v')),
              (tabs['b_cq'], _spec2(T, 512)), (tabs['b_sq'], _spec2(T, 512)),
              (tabs['b_ck'], _spec2(T, 256)), (tabs['b_sk'], _spec2(T, 256))]
    b_pars = [K['b_q_norm'], K['b_k_norm'], tabs['b_mq'], tabs['b_mk'], tabs['b_rq'], tabs['b_rk']]
    qb, kb, vb = _tw_fwd(n + "b_prep", _fn_b_prep, b_acts, b_pars, [(512, BF16), (256, BF16), (256, BF16)], L, T)
    ob, lse_b = _flash_fwd(n + "b_attn", qb, kb, vb, HEADS, 2, HD ** -0.5, L)
    sv.update(b_acts=b_acts, b_pars=b_pars, qb=qb, kb=kb, vb=vb, ob=ob, lse_b=lse_b)
    xbc = _conv_fwd(n + "c_conv", p, P_OFF['c_xbc'][0], C_XBC, K['c_conv'], True, L, 512)
    c_seqs = [(xbc, C_XBC, 0, [(0, HD, 4), (256, HD, 2), (384, HD, 2)]),
              (p, LANE, P_OFF['c_dt'][0] // LANE, None)]
    c_rows = [K['c_a_log'], K['c_dt_bias']]
    yc0, yc1, sc0, sc1 = _scan_fwd(n + "c_ssd", _ssd_chunk, c_seqs, c_rows, C_CHUNK, L, C_PER_STEP)
    sv.update(xbc=xbc, c_seqs=c_seqs, c_rows=c_rows, sc=(sc0, sc1))
    qkv = _conv_fwd(n + "d_conv", p, P_OFF['d_qkv'][0], D_QKV, K['d_conv'], True, L, 768)
    d_seqs = [(qkv, D_QKV, 0, [(0, HD, 4), (256, HD, 4), (512, HD, 4)]),
              (p, LANE, P_OFF['d_b'][0] // LANE, None), (p, LANE, P_OFF['d_a'][0] // LANE, None)]
    d_rows = [K['d_a_log'], K['d_dt_bias']]
    od0, od1, sd0, sd1 = _scan_fwd(n + "d_delta", _delta_chunk, d_seqs, d_rows, D_CHUNK, L, D_PER_STEP)
    sv.update(qkv=qkv, d_seqs=d_seqs, d_rows=d_rows, sd=(sd0, sd1))
    m_acts = [(oa, _spec2(T, 512)), (ob, _spec2(T, 512)), (yc0, _spec2(T, 256)), (yc1, _spec2(T, 256)),
              (xbc, _spec2(T, 256, 0)), (p, _pspec(T, 'c_z')), (od0, _spec2(T, 256)), (od1, _spec2(T, 256)),
              (p, _pspec(T, 'd_z'))]
    m_pars = [K['a_out_norm'], K['b_out_norm'], K['c_d_skip'], K['c_out_norm'], K['d_out_norm'], tabs['m64']]
    (o,) = _tw_fwd(n + "mixer_post", _fn_mixer_post, m_acts, m_pars, [(O_COLS, BF16)], L, T)
    f1 = _mm(n + "out_proj", o, K['w_out'].astype(BF16), 'nn', F32, 512, 1024, 768)
    r1_pars = [K['post_mix_norm'], K['pre_ffn_norm']]
    x1, h2 = _tw_fwd(n + "resid_mix", _fn_resid_norm2, [(x, _spec2(T, D_MODEL)), (f1, _spec2(T, D_MODEL))], r1_pars,
                     [(D_MODEL, F32), (D_MODEL, BF16)], L, T)
    sv.update(m_acts=m_acts, m_pars=m_pars, o=o, f1=f1, r1_pars=r1_pars, x1=x1, h2=h2)
    u = _mm(n + "ffn_in", h2, K['f_w_in'].astype(BF16), 'nn', F32, 512, 512, 1024)
    gu = _conv_fwd(n + "ffn_conv", u, 0, 2 * D_FF, K['f_conv'], False, L, 1408)
    act = _glu_fwd(n + "ffn_glu", gu, L)
    f2 = _mm(n + "ffn_out", act, K['f_w_out'].astype(BF16), 'nn', F32, 512, 1024, 1408)
    sv.update(u=u, gu=gu, act=act, f2=f2)
    xf = [(x1, _spec2(T, D_MODEL)), (f2, _spec2(T, D_MODEL))]
    if next_norm is None:
        (x2,) = _tw_fwd(n + "resid_ffn", _fn_resid_norm, xf, [K['post_ffn_norm']], [(D_MODEL, F32)], L, T)
        hn = None
    else:
        x2, hn = _tw_fwd(n + "resid_ffn", _fn_resid_norm2, xf, [K['post_ffn_norm'], next_norm],
                         [(D_MODEL, F32), (D_MODEL, BF16)], L, T)
    return x2, hn, sv


def _layer_bwd(l, dx2, dhn, K, sv, tabs, L, T, next_norm):
    n = f"l{l}b_"
    dK = {}
    s2 = lambda w, cb=0: _spec2(T, w, cb)
    xf = [(sv['x1'], s2(D_MODEL)), (sv['f2'], s2(D_MODEL))]
    if next_norm is None:
        (dx1a, df2), (dK['post_ffn_norm'],) = _tw_bwd(n + "resid_ffn", _fn_resid_norm, xf, [K['post_ffn_norm']],
                                                      [(dx2, s2(D_MODEL))], L, T, [True, True], [True])
        dnext = None
    else:
        (dx1a, df2), (dK['post_ffn_norm'], dnext) = _tw_bwd(
            n + "resid_ffn", _fn_resid_norm2, xf, [K['post_ffn_norm'], next_norm],
            [(dx2, s2(D_MODEL)), (dhn, s2(D_MODEL))], L, T, [True, True], [True, True])
    dact = _mm(n + "ffn_out_dx", df2, K['f_w_out'].astype(BF16), 'nt', F32, 512, 1408, 1024)
    dK['f_w_out'] = _mm(n + "ffn_out_dw", sv['act'], df2, 'tn', F32, 1408, 1024, 512)
    dgu = _glu_bwd(n + "ffn_glu", sv['gu'], dact, L)
    du, dK['f_conv'] = _conv_bwd(n + "ffn_conv", sv['u'], 0, 2 * D_FF, K['f_conv'], False, [(dgu, None)], L, 1408)
    dh2 = _mm(n + "ffn_in_dx", du, K['f_w_in'].astype(BF16), 'nt', F32, 512, 1024, 512)
    dK['f_w_in'] = _mm(n + "ffn_in_dw", sv['h2'], du, 'tn', F32, 512, 512, 512)
    (dxa, df1), (dK['post_mix_norm'], dK['pre_ffn_norm']) = _tw_bwd(
        n + "resid_mix", _fn_resid_norm2, [(sv['x'], s2(D_MODEL)), (sv['f1'], s2(D_MODEL))], sv['r1_pars'],
        [(dx1a, s2(D_MODEL)), (dh2, s2(D_MODEL))], L, T, [True, True], [True, True])
    do = _mm(n + "out_proj_dx", df1, K['w_out'].astype(BF16), 'nt', F32, 512, 768, 1024)
    dK['w_out'] = _mm(n + "out_proj_dw", sv['o'], df1, 'tn', F32, 768, 1024, 512)
    (doa, dob, dyc0, _, dxs_skip, dzc, dod0, _, dzd), mp = _tw_bwd(
        n + "mixer_post", _fn_mixer_post, sv['m_acts'], sv['m_pars'], [(do, s2(O_COLS))], L, T,
        [True] * 9, [True] * 5 + [False])
    dK['a_out_norm'], dK['b_out_norm'], dK['c_d_skip'], dK['c_out_norm'], dK['d_out_norm'] = mp
    (dqkv0, db0, da0), (dqkv1, db1, da1), (dK['d_a_log'], dK['d_dt_bias']) = _scan_bwd(
        n + "d_delta", _delta_chunk, sv['d_seqs'], sv['d_rows'], sv['sd'], dod0, None, D_CHUNK, L, D_PER_STEP)
    dqkv, dK['d_conv'] = _conv_bwd(n + "d_conv", sv['p'], P_OFF['d_qkv'][0], D_QKV, K['d_conv'], True,
                                   [(dqkv0, None), (dqkv1, None)], L, 768)
    (dxbc0, ddt0), (dxbc1, ddt1), (dK['c_a_log'], dK['c_dt_bias']) = _scan_bwd(
        n + "c_ssd", _ssd_chunk, sv['c_seqs'], sv['c_rows'], sv['sc'], dyc0, dxs_skip, C_CHUNK, L, C_PER_STEP)
    dxbc, dK['c_conv'] = _conv_bwd(n + "c_conv", sv['p'], P_OFF['c_xbc'][0], C_XBC, K['c_conv'], True,
                                   [(dxbc0, None), (dxbc1, None)], L, 512)
    dqb, dkb, dvb = _flash_bwd(n + "b_attn", sv['qb'], sv['kb'], sv['vb'], sv['ob'], sv['lse_b'], dob, HEADS, 2,
                               HD ** -0.5, L)
    (dbq, dbk, dbv), (dK['b_q_norm'], dK['b_k_norm']) = _tw_bwd(
        n + "b_prep", _fn_b_prep, sv['b_acts'], sv['b_pars'], [(dqb, s2(512)), (dkb, s2(256)), (dvb, s2(256))],
        L, T, [True] * 3 + [False] * 4, [True, True] + [False] * 4)
    dqa, dka, dva = _flash_bwd(n + "a_attn", sv['qa'], sv['ka'], sv['va'], sv['oa'], sv['lse_a'], doa, HEADS, 1,
                               (A_NOPE + A_ROPE) ** -0.5, L)
    (dcq, dckv, dkr), ap = _tw_bwd(
        n + "a_prep", _fn_a_prep, sv['a_acts'], sv['a_pars'], [(dqa, s2(512)), (dka, s2(512)), (dva, s2(512))],
        L, T, [True] * 3 + [False] * 4, [True] * 5 + [False] * 2)
    dK['a_q_norm'], dK['a_w_uq'], dK['a_kv_norm'], dK['a_w_uk'], dK['a_w_uv'] = ap
    pieces = [(dbq, s2(512)), (dxbc, s2(512)), (dcq, s2(256)), (dbk, s2(256)), (dqkv, s2(768)), (dbv, s2(256)),
              (dzc, s2(256)), (dzd, s2(256)), (dckv, s2(LANE)), (dkr, s2(LANE)),
              (ddt0, s2(LANE)), (ddt1, s2(LANE)), (db0, s2(LANE)), (db1, s2(LANE)), (da0, s2(LANE)),
              (da1, s2(LANE))]
    (dp,) = _tw_fwd(n + "assemble_dp", _fn_assemble, pieces, [], [(P_COLS, BF16)], L, T)
    dh = _mm(n + "in_proj_dx", dp, K['w_in'].astype(BF16), 'nt', F32, 512, 1024, 768)
    dK['w_in'] = _mm(n + "in_proj_dw", sv['h'], dp, 'tn', F32, 512, 768, 512)
    return dxa, dh, dK, dnext


def _tables(L):
    ca, sa = _rope_angles(L, A_ROPE)
    cb, sb = _rope_angles(L, HD)
    t = {}
    t['a_cq'], t['a_sq'], t['a_rq'] = _place_tables(L, ca, sa, 512, [LANE * h + A_NOPE for h in range(4)])
    t['a_ck'], t['a_sk'], t['a_rk'] = _place_tables(L, ca, sa, LANE, [A_NOPE])
    t['b_cq'], t['b_sq'], t['b_rq'] = _place_tables(L, cb, sb, 512, [LANE * h for h in range(4)])
    t['b_ck'], t['b_sk'], t['b_rk'] = _place_tables(L, cb, sb, 256, [LANE * h for h in range(2)])
    t['b_mq'] = _head_mean_matrix(512, LANE, HD)
    t['b_mk'] = _head_mean_matrix(256, LANE, HD)
    t['m64'] = _head_mean_matrix(256, HD, HD)
    return t


def kernel(x, pre_mix_norm, w_in, a_q_norm, a_w_uq, a_kv_norm, a_w_ukv, a_out_norm, b_q_norm, b_k_norm, b_out_norm, c_conv_w, c_conv_b, c_a_log, c_dt_bias, c_d_skip, c_out_norm, d_conv_w, d_a_log, d_dt_bias, d_out_norm, w_out, post_mix_norm, pre_ffn_norm, f_w_in, f_conv_w, f_conv_b, f_w_out, post_ffn_norm, loss_target, m_pre_mix_norm, m_w_in, m_a_q_norm, m_a_w_uq, m_a_kv_norm, m_a_w_ukv, m_a_out_norm, m_b_q_norm, m_b_k_norm, m_b_out_norm, m_c_conv_w, m_c_conv_b, m_c_a_log, m_c_dt_bias, m_c_d_skip, m_c_out_norm, m_d_conv_w, m_d_a_log, m_d_dt_bias, m_d_out_norm, m_w_out, m_post_mix_norm, m_pre_ffn_norm, m_f_w_in, m_f_conv_w, m_f_conv_b, m_f_w_out, m_post_ffn_norm, v_pre_mix_norm, v_w_in, v_a_q_norm, v_a_w_uq, v_a_kv_norm, v_a_w_ukv, v_a_out_norm, v_b_q_norm, v_b_k_norm, v_b_out_norm, v_c_conv_w, v_c_conv_b, v_c_a_log, v_c_dt_bias, v_c_d_skip, v_c_out_norm, v_d_conv_w, v_d_a_log, v_d_dt_bias, v_d_out_norm, v_w_out, v_post_mix_norm, v_pre_ffn_norm, v_f_w_in, v_f_conv_w, v_f_conv_b, v_f_w_out, v_post_ffn_norm):
    loc = locals()
    Wl = {n: loc[n] for n in WEIGHTS}
    Ml = {n: loc['m_' + n] for n in WEIGHTS}
    Vl = {n: loc['v_' + n] for n in WEIGHTS}
    L = x.shape[1]
    T = min(256, L)
    x0 = x.reshape(L, D_MODEL)
    tgt = loss_target.reshape(L, D_MODEL)

    shard_shapes = [Wl[n].shape for n in SHARDED]
    w_flat = _pack([Wl[n] for n in SHARDED], 1024, 256)
    gathered = _gather_chips(w_flat)
    per_chip = [_unpack(gathered[j], shard_shapes) for j in range(4)]
    full = dict(Wl)
    for k, n in enumerate(SHARDED):
        full[n] = jnp.concatenate([per_chip[j][k] for j in range(4)], axis=SHARD_AXIS[n])

    tabs = _tables(L)
    Ks, builds = [], []
    for l in range(DEPTH):
        K, vjp_l = jax.vjp(_build_layer, {n: full[n][l] for n in WEIGHTS})
        Ks.append(K)
        builds.append(vjp_l)

    (h,) = _tw_fwd("l0_norm_in", _fn_norm_in, [(x0, _spec2(T, D_MODEL))], [Ks[0]['pre_mix_norm']],
                   [(D_MODEL, BF16)], L, T)
    xs, saves = x0, []
    for l in range(DEPTH):
        nxt = Ks[l + 1]['pre_mix_norm'] if l + 1 < DEPTH else None
        xs, h, sv = _layer_fwd(l, xs, h, Ks[l], tabs, L, T, nxt)
        saves.append(sv)
    dy, loss_acc = _loss_call(xs, tgt, L)
    loss = lax.psum(loss_acc[0, 0], ("x", "y", "c"))

    grads = [None] * DEPTH
    dx, dhn = dy, None
    for l in reversed(range(DEPTH)):
        nxt = Ks[l + 1]['pre_mix_norm'] if l + 1 < DEPTH else None
        dxa, dh, dK, dnext = _layer_bwd(l, dx, dhn, Ks[l], saves[l], tabs, L, T, nxt)
        if dnext is not None:
            grads[l + 1]['pre_mix_norm'] = dnext
        grads[l] = dK
        if l > 0:
            dx, dhn = dxa, dh
        else:
            (dx_in,), (dpre,) = _tw_bwd("l0b_norm_in", _fn_norm_in, [(x0, _spec2(T, D_MODEL))],
                                        [Ks[0]['pre_mix_norm']], [(dh, _spec2(T, D_MODEL))], L, T, [True], [True],
                                        addto={0: (dxa, _spec2(T, D_MODEL))})
            grads[0]['pre_mix_norm'] = dpre
    gfull = []
    for l in range(DEPTH):
        zero = jax.tree.map(jnp.zeros_like, Ks[l])
        zero.update(grads[l])
        (gl,) = builds[l](zero)
        gfull.append(gl)
    gW = {n: jnp.stack([gfull[l][n] for l in range(DEPTH)]) for n in WEIGHTS}

    G = jnp.stack([_pack([jnp.split(gW[n], 4, axis=SHARD_AXIS[n])[j] for n in SHARDED], 1024, 256)
                   for j in range(4)])
    small_shapes = [Wl[n].shape for n in SMALL]
    gs = _pack([gW[n] for n in SMALL], LANE, 8)
    recv, recv_small = _scatter_grads(G, gs)
    pair = _swap_cores(_sum_slots(recv, 256))
    big = _adamw_call("adamw_sharded", pair, w_flat, _pack([Ml[n] for n in SHARDED], 1024, 256),
                      _pack([Vl[n] for n in SHARDED], 1024, 256), 256)
    small = _adamw_call("adamw_small", recv_small, _pack([Wl[n] for n in SMALL], LANE, 8),
                        _pack([Ml[n] for n in SMALL], LANE, 8), _pack([Vl[n] for n in SMALL], LANE, 8), gs.shape[0])
    res = {}
    for kind, b, s in zip(['grad', 'delta', 'new_m', 'new_v'], big, small):
        for n, a in zip(SHARDED, _unpack(b, shard_shapes)):
            res[kind, n] = a
        for n, a in zip(SMALL, _unpack(s, small_shapes)):
            res[kind, n] = a
    outs = [loss, dx_in.reshape(x.shape)]
    for kind in ['grad', 'delta', 'new_m', 'new_v']:
        outs += [res[kind, n] for n in WEIGHTS]
    return tuple(outs)
```

```python
import functools
import math

import numpy as np
import jax
import jax.numpy as jnp
from jax import lax
from jax.experimental import pallas as pl
from jax.experimental.pallas import tpu as pltpu

F32 = jnp.float32
BF16 = jnp.bfloat16
MESH = pl.DeviceIdType.MESH
VMEM_LIMIT = 48 * 1024 * 1024
LANE = 128

D_MODEL = 1024
DEPTH = 2
GRID_W = 64
ROPE_BASE = 10000.0
EPS = 1e-6
GROUP_W = 256
HEADS = 4
HD = 64
A_NOPE, A_ROPE, A_Q_LORA, A_KV_LORA = 64, 32, 192, 128
A_COLS = A_Q_LORA + A_KV_LORA + A_ROPE
B_COLS = 512
C_XBC = 512
C_COLS = GROUP_W + C_XBC + 8
D_QKV = 768
D_COLS = D_QKV + GROUP_W + 16
IN_COLS = A_COLS + B_COLS + C_COLS + D_COLS
C_CHUNK = 128
D_CHUNK = 64
C_PER_STEP = 1
D_PER_STEP = 2
D_FF = 2816
ADAM_LR, ADAM_B1, ADAM_B2, ADAM_EPS, ADAM_WD, ADAM_STEP = 0.001, 0.9, 0.999, 1e-08, 0.01, 10

WEIGHTS = ['pre_mix_norm', 'w_in', 'a_q_norm', 'a_w_uq', 'a_kv_norm', 'a_w_ukv', 'a_out_norm', 'b_q_norm',
           'b_k_norm', 'b_out_norm', 'c_conv_w', 'c_conv_b', 'c_a_log', 'c_dt_bias', 'c_d_skip', 'c_out_norm',
           'd_conv_w', 'd_a_log', 'd_dt_bias', 'd_out_norm', 'w_out', 'post_mix_norm', 'pre_ffn_norm', 'f_w_in',
           'f_conv_w', 'f_conv_b', 'f_w_out', 'post_ffn_norm']
SHARD_AXIS = {'w_in': 2, 'a_w_uq': 2, 'a_w_ukv': 2, 'c_conv_w': 2, 'd_conv_w': 2, 'w_out': 1, 'f_w_in': 2,
              'f_conv_w': 2, 'f_w_out': 1}
SHARDED = [n for n in WEIGHTS if n in SHARD_AXIS]
SMALL = [n for n in WEIGHTS if n not in SHARD_AXIS]

P_LAYOUT = [('b_q', 0, 512), ('c_xbc', 512, 512), ('a_cq', 1024, 256), ('b_k', 1280, 256), ('d_qkv', 1536, 768),
            ('b_v', 2304, 256), ('c_z', 2560, 256), ('d_z', 2816, 256), ('a_ckv', 3072, 128), ('a_kr', 3200, 128),
            ('c_dt', 3328, 128), ('d_b', 3456, 128), ('d_a', 3584, 128), ('pad', 3712, 128)]
P_OFF = {n: (o, w) for n, o, w in P_LAYOUT}
P_COLS = 3840
O_COLS = 1536


def _cparams(sem):
    return pltpu.CompilerParams(dimension_semantics=sem, vmem_limit_bytes=VMEM_LIMIT)


def _tile(n, target):
    best = None
    for d in range(LANE, min(n, target) + 1, LANE):
        if n % d == 0:
            best = d
    return best if best is not None else n


_NN = ((1,), (0,))
_NT = ((1,), (1,))
_TN = ((0,), (0,))


def _raw_dot(a, b, dims, hi):
    if hi:
        prec = lax.Precision.HIGH if hi == 'high' else lax.Precision.HIGHEST
        return lax.dot_general(a, b, (dims, ((), ())), precision=prec, preferred_element_type=F32)
    return lax.dot_general(a.astype(BF16), b.astype(BF16), (dims, ((), ())), preferred_element_type=F32)


def _make_dots(hi):
    @jax.custom_vjp
    def nn(a, b):
        return _raw_dot(a, b, _NN, hi)

    @jax.custom_vjp
    def nt(a, b):
        return _raw_dot(a, b, _NT, hi)

    @jax.custom_vjp
    def tn(a, b):
        return _raw_dot(a, b, _TN, hi)

    nn.defvjp(lambda a, b: (nn(a, b), (a, b)), lambda r, g: (nt(g, r[1]), tn(r[0], g)))
    nt.defvjp(lambda a, b: (nt(a, b), (a, b)), lambda r, g: (nn(g, r[1]), tn(g, r[0])))
    tn.defvjp(lambda a, b: (tn(a, b), (a, b)), lambda r, g: (nt(r[1], g), nn(r[0], g)))
    return nn, nt, tn


_nn, _nt, _tn = _make_dots(False)
_nn_hi, _nt_hi, _tn_hi = _make_dots(True)
_nn_h3, _nt_h3, _tn_h3 = _make_dots('high')


def _sigmoid(x):
    return 1.0 / (1.0 + jnp.exp(-x))


def _silu(x):
    return x * _sigmoid(x)


def _softplus(x):
    return jnp.maximum(x, 0.0) + jnp.log(1.0 + jnp.exp(-jnp.abs(x)))


def _rms(x, w, n=None):
    n = x.shape[-1] if n is None else n
    ms = jnp.sum(x * x, axis=-1, keepdims=True) * (1.0 / n)
    return x * lax.rsqrt(ms + EPS) * w


def _spec2(T, w, cb=0):
    return pl.BlockSpec((T, w), lambda i: (i, cb))


def _spec3(T, w, lead, cb=0):
    return pl.BlockSpec((None, T, w), lambda i: (lead, i, cb))


def _full_spec(a):
    nd = a.ndim
    return pl.BlockSpec(a.shape, lambda i: (0,) * nd)


def _tw_fwd(name, fn, acts, params, outs, L, T):
    na, npar = len(acts), len(params)

    def kern(*refs):
        a = [r[...].astype(F32) for r in refs[:na]]
        p = [r[...].astype(F32) for r in refs[na:na + npar]]
        res = fn(a, p)
        for r, o in zip(refs[na + npar:], res):
            r[...] = o.astype(r.dtype)

    return pl.pallas_call(
        kern, name=name, grid=(L // T,),
        in_specs=[s for _, s in acts] + [_full_spec(p) for p in params],
        out_specs=[_spec2(T, w) for w, _ in outs],
        out_shape=[jax.ShapeDtypeStruct((L, w), dt) for w, dt in outs],
        compiler_params=_cparams(("arbitrary",)),
    )(*[a for a, _ in acts], *params)


def _tw_bwd(name, fn, acts, params, douts, L, T, act_grad, par_grad, addto=None):
    na, npar, nd = len(acts), len(params), len(douts)
    addto = addto or {}
    add_keys = sorted(addto)
    ga = [k for k in range(na) if act_grad[k]]
    gp = [k for k in range(npar) if par_grad[k]]

    def kern(*refs):
        i = pl.program_id(0)
        a = [r[...].astype(F32) for r in refs[:na]]
        p = [r[...].astype(F32) for r in refs[na:na + npar]]
        g = [r[...].astype(F32) for r in refs[na + npar:na + npar + nd]]
        pos = na + npar + nd
        adds = [r[...].astype(F32) for r in refs[pos:pos + len(add_keys)]]
        pos += len(add_keys)
        da_refs = refs[pos:pos + len(ga)]
        dp_refs = refs[pos + len(ga):]

        def f(ad, pd):
            af, pf = list(a), list(p)
            for k, v in zip(ga, ad):
                af[k] = v
            for k, v in zip(gp, pd):
                pf[k] = v
            return fn(af, pf)

        _, vjp = jax.vjp(f, [a[k] for k in ga], [p[k] for k in gp])
        dad, dpd = vjp(list(g))
        for n, (r, d) in enumerate(zip(da_refs, dad)):
            if n in addto:
                d = d + adds[add_keys.index(n)]
            r[...] = d.astype(r.dtype)

        @pl.when(i == 0)
        def _():
            for r in dp_refs:
                r[...] = jnp.zeros(r.shape, F32)

        for r, d in zip(dp_refs, dpd):
            r[...] += d

    def width(spec):
        return spec.block_shape[-1]

    res = pl.pallas_call(
        kern, name=name, grid=(L // T,),
        in_specs=[s for _, s in acts] + [_full_spec(p) for p in params] + [s for _, s in douts]
        + [addto[k][1] for k in add_keys],
        out_specs=[_spec2(T, width(acts[k][1])) for k in ga] + [_full_spec(params[k]) for k in gp],
        out_shape=[jax.ShapeDtypeStruct((L, width(acts[k][1])), F32) for k in ga]
        + [jax.ShapeDtypeStruct(params[k].shape, F32) for k in gp],
        compiler_params=_cparams(("arbitrary",)),
    )(*[a for a, _ in acts], *params, *[a for a, _ in douts], *[addto[k][0] for k in add_keys])
    return list(res[:len(ga)]), list(res[len(ga):])


def _mm(name, a, b, mode, out_dtype, tm, tn, tk):
    if mode == 'nn':
        (M, K), N = a.shape, b.shape[1]
    elif mode == 'nt':
        (M, K), N = a.shape, b.shape[0]
    else:
        (K, M), N = a.shape, b.shape[1]
    tm, tn, tk = _tile(M, tm), _tile(N, tn), _tile(K, tk)
    nk = K // tk
    if mode == 'nn':
        a_spec = pl.BlockSpec((tm, tk), lambda i, j, k: (i, k))
        b_spec = pl.BlockSpec((tk, tn), lambda i, j, k: (k, j))
        dims = _NN
    elif mode == 'nt':
        a_spec = pl.BlockSpec((tm, tk), lambda i, j, k: (i, k))
        b_spec = pl.BlockSpec((tn, tk), lambda i, j, k: (j, k))
        dims = _NT
    else:
        a_spec = pl.BlockSpec((tk, tm), lambda i, j, k: (k, i))
        b_spec = pl.BlockSpec((tk, tn), lambda i, j, k: (k, j))
        dims = _TN

    def kern(a_ref, b_ref, o_ref, acc):
        k = pl.program_id(2)

        @pl.when(k == 0)
        def _():
            acc[...] = jnp.zeros(acc.shape, F32)

        acc[...] += lax.dot_general(a_ref[...].astype(BF16), b_ref[...].astype(BF16), (dims, ((), ())),
                                    preferred_element_type=F32)

        @pl.when(k == nk - 1)
        def _():
            o_ref[...] = acc[...].astype(o_ref.dtype)

    return pl.pallas_call(
        kern, name=name, grid=(M // tm, N // tn, nk),
        in_specs=[a_spec, b_spec],
        out_specs=pl.BlockSpec((tm, tn), lambda i, j, k: (i, j)),
        out_shape=jax.ShapeDtypeStruct((M, N), out_dtype),
        scratch_shapes=[pltpu.VMEM((tm, tn), F32)],
        compiler_params=_cparams(("arbitrary", "arbitrary", "arbitrary")),
    )(a, b)


def _flash_fwd(name, q, k, v, H, rep, scale, L):
    tq = min(256, L)
    nq = L // tq
    KC = min(2048, L)
    nkc = L // KC

    def kern(q_ref, k_ref, v_ref, o_ref, lse_ref):
        qb = q_ref[...]
        m = jnp.full((tq, 1), -1e30, F32)
        l = jnp.zeros((tq, 1), F32)
        acc = jnp.zeros((tq, LANE), F32)
        for c in range(nkc):
            kb = k_ref[c * KC:(c + 1) * KC, :]
            vb = v_ref[c * KC:(c + 1) * KC, :]
            s = lax.dot_general(qb, kb, (_NT, ((), ())), preferred_element_type=F32) * scale
            mn = jnp.maximum(m, jnp.max(s, axis=-1, keepdims=True))
            al = jnp.exp(m - mn)
            p = jnp.exp(s - mn)
            l = al * l + jnp.sum(p, axis=-1, keepdims=True)
            acc = al * acc + lax.dot_general(p.astype(BF16), vb, (_NN, ((), ())), preferred_element_type=F32)
            m = mn
        o_ref[...] = acc / l
        lse_ref[...] = m + jnp.log(l)

    return pl.pallas_call(
        kern, name=name, grid=(H, nq),
        in_specs=[pl.BlockSpec((tq, LANE), lambda h, i: (i, h)),
                  pl.BlockSpec((L, LANE), lambda h, i: (0, h // rep)),
                  pl.BlockSpec((L, LANE), lambda h, i: (0, h // rep))],
        out_specs=[pl.BlockSpec((tq, LANE), lambda h, i: (i, h)),
                   pl.BlockSpec((tq, 1), lambda h, i: (h * nq + i, 0))],
        out_shape=[jax.ShapeDtypeStruct((L, H * LANE), F32), jax.ShapeDtypeStruct((H * L, 1), F32)],
        compiler_params=_cparams(("arbitrary", "arbitrary")),
    )(q, k, v)


def _flash_bwd(name, q, k, v, o, lse, do, H, rep, scale, L):
    tq = min(256, L)
    nq = L // tq
    KC = min(2048, L)
    nkc = L // KC
    Hkv = H // rep

    def kern(q_ref, k_ref, v_ref, o_ref, lse_ref, do_ref, dq_ref, dk_ref, dv_ref):
        h = pl.program_id(0)
        i = pl.program_id(1)

        @pl.when((i == 0) & (h % rep == 0))
        def _():
            dk_ref[...] = jnp.zeros(dk_ref.shape, F32)
            dv_ref[...] = jnp.zeros(dv_ref.shape, F32)

        qb = q_ref[...]
        do = do_ref[...]
        dob = do.astype(BF16)
        delta = jnp.sum(do * o_ref[...], axis=-1, keepdims=True)
        lse = lse_ref[...]
        dq = jnp.zeros((tq, LANE), F32)
        for c in range(nkc):
            sl = slice(c * KC, (c + 1) * KC)
            kb = k_ref[sl, :]
            vb = v_ref[sl, :]
            s = lax.dot_general(qb, kb, (_NT, ((), ())), preferred_element_type=F32) * scale
            p = jnp.exp(s - lse)
            dp = lax.dot_general(dob, vb, (_NT, ((), ())), preferred_element_type=F32)
            ds = (p * (dp - delta) * scale).astype(BF16)
            dq = dq + lax.dot_general(ds, kb, (_NN, ((), ())), preferred_element_type=F32)
            dk_ref[sl, :] += lax.dot_general(ds, qb, (_TN, ((), ())), preferred_element_type=F32)
            dv_ref[sl, :] += lax.dot_general(p.astype(BF16), dob, (_TN, ((), ())), preferred_element_type=F32)
        dq_ref[...] = dq

    return pl.pallas_call(
        kern, name=name, grid=(H, nq),
        in_specs=[pl.BlockSpec((tq, LANE), lambda h, i: (i, h)),
                  pl.BlockSpec((L, LANE), lambda h, i: (0, h // rep)),
                  pl.BlockSpec((L, LANE), lambda h, i: (0, h // rep)),
                  pl.BlockSpec((tq, LANE), lambda h, i: (i, h)),
                  pl.BlockSpec((tq, 1), lambda h, i: (h * nq + i, 0)),
                  pl.BlockSpec((tq, LANE), lambda h, i: (i, h))],
        out_specs=[pl.BlockSpec((tq, LANE), lambda h, i: (i, h)),
                   pl.BlockSpec((L, LANE), lambda h, i: (0, h // rep)),
                   pl.BlockSpec((L, LANE), lambda h, i: (0, h // rep))],
        out_shape=[jax.ShapeDtypeStruct((L, H * LANE), F32), jax.ShapeDtypeStruct((L, Hkv * LANE), F32),
                   jax.ShapeDtypeStruct((L, Hkv * LANE), F32)],
        compiler_params=_cparams(("arbitrary", "arbitrary")),
    )(q, k, v, o, lse, do)


def _shift_dn(x, first_row):
    row = lax.broadcasted_iota(jnp.int32, x.shape, 0)
    return jnp.where(row == 0, first_row, pltpu.roll(x, 1, 0))


def _shift_up(x, last_row):
    n = x.shape[0]
    row = lax.broadcasted_iota(jnp.int32, x.shape, 0)
    return jnp.where(row == n - 1, last_row, pltpu.roll(x, n - 1, 0))


def _halo_specs(ndim, lead, T, tc, cb0, L):
    r8 = T // 8
    last8 = L // 8 - 1
    if ndim == 2:
        return [pl.BlockSpec((T, tc), lambda j, i: (i, cb0 + j)),
                pl.BlockSpec((8, tc), lambda j, i: (jnp.maximum(i * r8 - 1, 0), cb0 + j)),
                pl.BlockSpec((8, tc), lambda j, i: (jnp.minimum((i + 1) * r8, last8), cb0 + j))]
    return [pl.BlockSpec((None, T, tc), lambda j, i: (lead, i, cb0 + j)),
            pl.BlockSpec((None, 8, tc), lambda j, i: (lead, jnp.maximum(i * r8 - 1, 0), cb0 + j)),
            pl.BlockSpec((None, 8, tc), lambda j, i: (lead, jnp.minimum((i + 1) * r8, last8), cb0 + j))]


def _conv_fwd(name, x, col0, C, w8, act, L, tc):
    T = min(256, L)
    nt = L // T
    cb0 = col0 // tc

    def kern(x_ref, xp_ref, xn_ref, w_ref, o_ref):
        i = pl.program_id(1)
        x = x_ref[...]
        w = w_ref[...]
        pr = jnp.where(i == 0, 0.0, xp_ref[7:8, :])
        nr = jnp.where(i == nt - 1, 0.0, xn_ref[0:1, :])
        pre = _shift_dn(x, pr) * w[0:1] + x * w[1:2] + _shift_up(x, nr) * w[2:3] + w[3:4]
        o_ref[...] = _silu(pre) if act else pre

    return pl.pallas_call(
        kern, name=name, grid=(C // tc, nt),
        in_specs=_halo_specs(2, None, T, tc, cb0, L) + [pl.BlockSpec((8, tc), lambda j, i: (0, j))],
        out_specs=pl.BlockSpec((T, tc), lambda j, i: (i, j)),
        out_shape=jax.ShapeDtypeStruct((L, C), F32),
        compiler_params=_cparams(("arbitrary", "arbitrary")),
    )(x, x, x, w8)


def _conv_bwd(name, x, col0, C, w8, act, gs, L, tc):
    T = min(256, L)
    nt = L // T
    cb0 = col0 // tc
    ng = len(gs)

    def dact(pre, g):
        if not act:
            return g
        s = _sigmoid(pre)
        return g * (s * (1.0 + pre * (1.0 - s)))

    def kern(*refs):
        x_ref, xp_ref, xn_ref, w_ref = refs[:4]
        g_refs = refs[4:4 + 3 * ng]
        dx_ref, dw_ref = refs[4 + 3 * ng:]
        i = pl.program_id(1)
        first = i == 0
        last = i == nt - 1
        x = x_ref[...]
        w = w_ref[...]
        w0, w1, w2, b = w[0:1], w[1:2], w[2:3], w[3:4]
        g = g_refs[0][...]
        gp = g_refs[1][7:8, :]
        gn = g_refs[2][0:1, :]
        for n in range(1, ng):
            g = g + g_refs[3 * n][...]
            gp = gp + g_refs[3 * n + 1][7:8, :]
            gn = gn + g_refs[3 * n + 2][0:1, :]
        pr = jnp.where(first, 0.0, xp_ref[7:8, :])
        pr2 = jnp.where(first, 0.0, xp_ref[6:7, :])
        nr = jnp.where(last, 0.0, xn_ref[0:1, :])
        nr2 = jnp.where(last, 0.0, xn_ref[1:2, :])
        xm1 = _shift_dn(x, pr)
        xp1 = _shift_up(x, nr)
        pre = xm1 * w0 + x * w1 + xp1 * w2 + b
        dpre = dact(pre, g)
        pre_m1 = pr2 * w0 + pr * w1 + x[0:1] * w2 + b
        dpre_m1 = jnp.where(first, 0.0, dact(pre_m1, gp))
        pre_T = x[T - 1:T] * w0 + nr * w1 + nr2 * w2 + b
        dpre_T = jnp.where(last, 0.0, dact(pre_T, gn))
        dx_ref[...] = _shift_up(dpre, dpre_T) * w0 + dpre * w1 + _shift_dn(dpre, dpre_m1) * w2
        row = lax.broadcasted_iota(jnp.int32, (8, tc), 0)
        dw = (jnp.where(row == 0, jnp.sum(dpre * xm1, axis=0, keepdims=True), 0.0)
              + jnp.where(row == 1, jnp.sum(dpre * x, axis=0, keepdims=True), 0.0)
              + jnp.where(row == 2, jnp.sum(dpre * xp1, axis=0, keepdims=True), 0.0)
              + jnp.where(row == 3, jnp.sum(dpre, axis=0, keepdims=True), 0.0))

        @pl.when(first)
        def _():
            dw_ref[...] = jnp.zeros((8, tc), F32)

        dw_ref[...] += dw

    g_specs, g_args = [], []
    for arr, lead in gs:
        g_specs += _halo_specs(arr.ndim, lead, T, tc, 0, L)
        g_args += [arr, arr, arr]
    return pl.pallas_call(
        kern, name=name, grid=(C // tc, nt),
        in_specs=_halo_specs(2, None, T, tc, cb0, L) + [pl.BlockSpec((8, tc), lambda j, i: (0, j))] + g_specs,
        out_specs=[pl.BlockSpec((T, tc), lambda j, i: (i, j)), pl.BlockSpec((8, tc), lambda j, i: (0, j))],
        out_shape=[jax.ShapeDtypeStruct((L, C), F32), jax.ShapeDtypeStruct((8, C), F32)],
        compiler_params=_cparams(("arbitrary", "arbitrary")),
    )(x, x, x, w8, *g_args)


def _glu_fwd(name, gu, L):
    T = min(256, L)
    tc = 1408
    ncb = D_FF // tc

    def kern(g_ref, u_ref, o_ref):
        o_ref[...] = (_silu(g_ref[...]) * u_ref[...]).astype(BF16)

    return pl.pallas_call(
        kern, name=name, grid=(L // T, ncb),
        in_specs=[pl.BlockSpec((T, tc), lambda i, j: (i, j)), pl.BlockSpec((T, tc), lambda i, j: (i, j + ncb))],
        out_specs=pl.BlockSpec((T, tc), lambda i, j: (i, j)),
        out_shape=jax.ShapeDtypeStruct((L, D_FF), BF16),
        compiler_params=_cparams(("arbitrary", "arbitrary")),
    )(gu, gu)


def _glu_bwd(name, gu, da, L):
    T = min(256, L)
    tc = 1408
    ncb = D_FF // tc

    def kern(g_ref, u_ref, da_ref, o_ref):
        half = pl.program_id(1)
        g = g_ref[...]
        s = _sigmoid(g)
        d = da_ref[...]
        dg = d * u_ref[...] * (s * (1.0 + g * (1.0 - s)))
        du = d * (g * s)
        o_ref[...] = jnp.where(half == 0, dg, du)

    return pl.pallas_call(
        kern, name=name, grid=(L // T, 2, ncb),
        in_specs=[pl.BlockSpec((T, tc), lambda i, h, j: (i, j)),
                  pl.BlockSpec((T, tc), lambda i, h, j: (i, j + ncb)),
                  pl.BlockSpec((T, tc), lambda i, h, j: (i, j))],
        out_specs=pl.BlockSpec((T, tc), lambda i, h, j: (i, h * ncb + j)),
        out_shape=jax.ShapeDtypeStruct((L, 2 * D_FF), F32),
        compiler_params=_cparams(("arbitrary", "arbitrary", "arbitrary")),
    )(gu, gu, da)


def _masks(Q, rev):
    ri = lax.broadcasted_iota(jnp.int32, (Q, Q), 0)
    ci = lax.broadcasted_iota(jnp.int32, (Q, Q), 1)
    diff = (ri - ci) * (1 - 2 * rev)
    return diff >= 0, diff > 0


def _lane_pick(v, sel):
    return jnp.sum(v * sel, axis=-1, keepdims=True)


def _head_rows(v_all, Q, rev):
    r = lax.broadcasted_iota(jnp.int32, (HEADS * Q, LANE), 0)
    l = lax.broadcasted_iota(jnp.int32, (HEADS * Q, LANE), 1)
    pick = jnp.zeros((HEADS * Q, LANE), F32)
    for h in range(HEADS):
        pick = jnp.where((r >= h * Q) & (r < (h + 1) * Q) & (l == rev * 4 + h), 1.0, pick)
    return _nt_hi(pick, v_all)


def _ssd_chunk(S, x, B, C, dtraw, alog, dtb, rev):
    Q = dtraw.shape[0]
    incl, _ = _masks(Q, rev)
    tri = incl.astype(F32)
    dt = _softplus(dtraw + dtb)
    a_all = dt * (-jnp.exp(alog))
    acum_all = _nn_hi(tri, a_all)
    total_all = jnp.sum(a_all, axis=0, keepdims=True)
    lane = lax.broadcasted_iota(jnp.int32, (1, LANE), 1)
    rows_all = _head_rows(acum_all, Q, rev)
    ys, Sn = [], []
    for h in range(HEADS):
        g = h // 2
        sel = (lane == rev * 4 + h).astype(F32)
        acum = _lane_pick(acum_all, sel)
        dth = _lane_pick(dt, sel)
        tot = _lane_pick(total_all, sel)
        seg = acum - rows_all[h * Q:(h + 1) * Q, :]
        decay = jnp.exp(jnp.where(incl, seg, -1e30))
        xdt = x[h] * dth
        Sh = S[HD * h:HD * (h + 1), :]
        scores = _nt(C[g], B[g]) * decay
        y_diag = _nn(scores, xdt)
        states = _tn(xdt, B[g] * jnp.exp(tot - acum))
        y_off = _nt(C[g], Sh) * jnp.exp(acum)
        ys.append(y_diag + y_off)
        Sn.append(Sh * jnp.exp(tot) + states)
    return ys, jnp.concatenate(Sn, axis=0)


def _inv_unit_raw(Lm):
    N = Lm.shape[0]
    Q = D_CHUNK
    ri = lax.broadcasted_iota(jnp.int32, (N, N), 0)
    ci = lax.broadcasted_iota(jnp.int32, (N, N), 1)
    X = (ri == ci).astype(F32) - Lm
    P = _raw_dot(Lm, Lm, _NN, 'high')
    n = 2
    while n < Q:
        X = X + _raw_dot(X, P, _NN, 'high')
        n *= 2
        if n < Q:
            P = _raw_dot(P, P, _NN, 'high')
    return X


@jax.custom_vjp
def _inv_unit(Lm):
    return _inv_unit_raw(Lm)


def _inv_unit_f(Lm):
    T = _inv_unit_raw(Lm)
    return T, T


def _inv_unit_b(T, g):
    return (-_raw_dot(_raw_dot(T, g, _TN, 'high'), T, _NT, 'high'),)


_inv_unit.defvjp(_inv_unit_f, _inv_unit_b)


def _delta_chunk(S, q, k, v, braw, araw, alog, dtb, rev):
    Q = braw.shape[0]
    N = HEADS * Q
    tri = _masks(Q, rev)[0].astype(F32)
    ri = lax.broadcasted_iota(jnp.int32, (N, N), 0)
    ci = lax.broadcasted_iota(jnp.int32, (N, N), 1)
    sh = int(math.log2(Q))
    same = (ri >> sh) == (ci >> sh)
    diff = (ri - ci) * (1 - 2 * rev)
    incl = same & (diff >= 0)
    strict = same & (diff > 0)
    beta_all = _sigmoid(braw)
    g_all = -jnp.exp(alog) * _softplus(araw + dtb)
    G_all = _nn_hi(tri, g_all)
    Gtot_all = jnp.sum(g_all, axis=0, keepdims=True)
    r = lax.broadcasted_iota(jnp.int32, (N, LANE), 0)
    l = lax.broadcasted_iota(jnp.int32, (N, LANE), 1)
    selm = (l == rev * 4 + (r >> sh)).astype(F32)
    rows4 = lambda a: jnp.concatenate([a] * HEADS, axis=0)
    XG = rows4(G_all) * selm
    G = jnp.sum(XG, axis=-1, keepdims=True)
    bt = jnp.sum(rows4(beta_all) * selm, axis=-1, keepdims=True)
    Gtot = jnp.sum(Gtot_all * selm, axis=-1, keepdims=True)
    decay = jnp.exp(jnp.where(incl, G - _nt_hi(jnp.ones((N, LANE), F32), XG), -1e30))
    qs, ks, vs = (jnp.concatenate(t, axis=0) for t in (q, k, v))
    qn = qs * lax.rsqrt(jnp.sum(qs * qs, axis=-1, keepdims=True) + 1e-6)
    kn = ks * lax.rsqrt(jnp.sum(ks * ks, axis=-1, keepdims=True) + 1e-6)
    qc = qn * (HD ** -0.5)
    kb = kn * bt
    T = _inv_unit(jnp.where(strict, _nt(kb, kn) * decay, 0.0))
    eG = jnp.exp(G)
    u = _nn(T, vs * bt)
    w = _nn(T, kb * eG)
    qk = _nt(qc, kn) * decay
    spread = (lax.broadcasted_iota(jnp.int32, (HD, N), 0)
              == (lax.broadcasted_iota(jnp.int32, (HD, N), 1) & (HD - 1))).astype(F32)
    wide = lambda a: jnp.where(same, _nn(a, spread), 0.0)
    v_new = u - _nn(wide(w), S)
    o = _nn(wide(qc * eG), S) + _nn(qk, v_new)
    S_new = S * jnp.exp(Gtot) + _tn(wide(kn * jnp.exp(Gtot - G)), v_new)
    return [o[Q * h:Q * (h + 1), :] for h in range(HEADS)], S_new


def _seq_pieces(ref, r0, Q, splits):
    if splits is None:
        return ref[r0:r0 + Q, :]
    return [[ref[r0:r0 + Q, o + w * t:o + w * (t + 1)] for t in range(n)] for o, w, n in splits]


def _store_pieces(ref, r0, Q, splits, vals, extra=None):
    if splits is None:
        ref[r0:r0 + Q, :] = vals
        return
    for g, (o, w, n) in enumerate(splits):
        for t in range(n):
            v = vals[g][t]
            if extra is not None and g == 0:
                v = v + extra[r0:r0 + Q, o + w * t:o + w * (t + 1)]
            ref[r0:r0 + Q, o + w * t:o + w * (t + 1)] = v


def _flat(ins):
    out = []
    for v in ins:
        if isinstance(v, list):
            out.extend(v)
        else:
            out.append(v)
    return out


def _scan_fwd(name, chunk_fn, seqs, rows, Q, L, CH):
    nc = L // Q
    nb = nc // CH
    ns, nr = len(seqs), len(rows)
    BQ = Q * CH

    def kern(*refs):
        s_refs = (refs[:ns], refs[ns:2 * ns])
        r_refs = refs[2 * ns:2 * ns + nr]
        y_refs = refs[2 * ns + nr:2 * ns + nr + 2]
        ss_refs = refs[2 * ns + nr + 2:2 * ns + nr + 4]
        S_scr = refs[2 * ns + nr + 4]
        i = pl.program_id(0)

        @pl.when(i == 0)
        def _():
            S_scr[...] = jnp.zeros(S_scr.shape, F32)

        rws = [r[...] for r in r_refs]
        for d in (0, 1):
            S = S_scr[d]
            for cc in range(CH):
                c = cc if d == 0 else CH - 1 - cc
                ss_refs[d][c] = S
                ins = [_seq_pieces(r, c * Q, Q, sp) for r, (_, _, _, sp) in zip(s_refs[d], seqs)]
                ys, S = chunk_fn(S, *_flat(ins), *rws, d)
                for h in range(HEADS):
                    y_refs[d][c * Q:(c + 1) * Q, HD * h:HD * (h + 1)] = ys[h]
            S_scr[d] = S

    fwd_specs = [pl.BlockSpec((BQ, w), functools.partial(lambda i, cb: (i, cb), cb=cb)) for _, w, cb, _ in seqs]
    rev_specs = [pl.BlockSpec((BQ, w), functools.partial(lambda i, cb: (nb - 1 - i, cb), cb=cb))
                 for _, w, cb, _ in seqs]
    arrs = [a for a, _, _, _ in seqs]
    return pl.pallas_call(
        kern, name=name, grid=(nb,),
        in_specs=fwd_specs + rev_specs + [pl.BlockSpec((1, LANE), lambda i: (0, 0)) for _ in rows],
        out_specs=[pl.BlockSpec((BQ, GROUP_W), lambda i: (i, 0)),
                   pl.BlockSpec((BQ, GROUP_W), lambda i: (nb - 1 - i, 0)),
                   pl.BlockSpec((CH, GROUP_W, HD), lambda i: (i, 0, 0)),
                   pl.BlockSpec((CH, GROUP_W, HD), lambda i: (nb - 1 - i, 0, 0))],
        out_shape=[jax.ShapeDtypeStruct((L, GROUP_W), F32)] * 2 + [jax.ShapeDtypeStruct((nc, GROUP_W, HD), F32)] * 2,
        scratch_shapes=[pltpu.VMEM((2, GROUP_W, HD), F32)],
        compiler_params=_cparams(("arbitrary",)),
    )(*arrs, *arrs, *rows)


def _scan_bwd(name, chunk_fn, seqs, rows, ssaves, dy, extra, Q, L, CH):
    nc = L // Q
    nb = nc // CH
    BQ = Q * CH
    ns, nr = len(seqs), len(rows)
    has_extra = extra is not None

    def kern(*refs):
        s_refs = (refs[:ns], refs[ns:2 * ns])
        pos = 2 * ns
        r_refs = refs[pos:pos + nr]
        pos += nr
        ss_refs = refs[pos:pos + 2]
        dy_refs = refs[pos + 2:pos + 4]
        pos += 4
        ex_ref = refs[pos] if has_extra else None
        pos += 1 if has_extra else 0
        ds_refs = (refs[pos:pos + ns], refs[pos + ns:pos + 2 * ns])
        pos += 2 * ns
        dr_refs = refs[pos:pos + nr]
        dS_scr = refs[pos + nr]
        i = pl.program_id(0)

        @pl.when(i == 0)
        def _():
            dS_scr[...] = jnp.zeros(dS_scr.shape, F32)
            for r in dr_refs:
                r[...] = jnp.zeros(r.shape, F32)

        rws = [r[...] for r in r_refs]
        dr_acc = [jnp.zeros((1, LANE), F32) for _ in rows]
        for d in (0, 1):
            dS = dS_scr[d]
            for cc in range(CH):
                c = CH - 1 - cc if d == 0 else cc
                S = ss_refs[d][c]
                dys = [dy_refs[d][c * Q:(c + 1) * Q, HD * h:HD * (h + 1)] for h in range(HEADS)]
                ins = [_seq_pieces(r, c * Q, Q, sp) for r, (_, _, _, sp) in zip(s_refs[d], seqs)]
                _, vjp = jax.vjp(
                    functools.partial(lambda S_, ins_, rws_, d_: chunk_fn(S_, *_flat(ins_), *rws_, d_), d_=d),
                    S, ins, rws)
                dS, dins, drws = vjp((dys, dS))
                for n_, (r, (_, _, _, sp)) in enumerate(zip(ds_refs[d], seqs)):
                    _store_pieces(r, c * Q, Q, sp, dins[n_],
                                  extra=ex_ref if (has_extra and d == 0 and n_ == 0) else None)
                dr_acc = [a + g for a, g in zip(dr_acc, drws)]
            dS_scr[d] = dS
        for r, g in zip(dr_refs, dr_acc):
            r[...] += g

    def blk(shape, rev, cb=0):
        nd = len(shape)
        if rev:
            return pl.BlockSpec(shape, lambda i: (i, cb) + (0,) * (nd - 2))
        return pl.BlockSpec(shape, lambda i: (nb - 1 - i, cb) + (0,) * (nd - 2))

    arrs = [a for a, _, _, _ in seqs]
    in_specs = [blk((BQ, w), False, cb) for _, w, cb, _ in seqs] + [blk((BQ, w), True, cb) for _, w, cb, _ in seqs]
    in_specs += [pl.BlockSpec((1, LANE), lambda i: (0, 0)) for _ in rows]
    in_specs += [blk((CH, GROUP_W, HD), False), blk((CH, GROUP_W, HD), True),
                 blk((BQ, GROUP_W), False), blk((BQ, GROUP_W), True)]
    args = arrs + arrs + list(rows) + list(ssaves) + [dy, dy]
    if has_extra:
        in_specs.append(blk((BQ, GROUP_W), False))
        args.append(extra)
    res = pl.pallas_call(
        kern, name=name, grid=(nb,),
        in_specs=in_specs,
        out_specs=[blk((BQ, w), False) for _, w, _, _ in seqs] + [blk((BQ, w), True) for _, w, _, _ in seqs]
        + [pl.BlockSpec((1, LANE), lambda i: (0, 0)) for _ in rows],
        out_shape=[jax.ShapeDtypeStruct((L, w), F32) for _, w, _, _ in seqs] * 2
        + [jax.ShapeDtypeStruct((1, LANE), F32) for _ in rows],
        scratch_shapes=[pltpu.VMEM((2, GROUP_W, HD), F32)],
        compiler_params=_cparams(("arbitrary",)),
    )(*args)
    return list(res[:ns]), list(res[ns:2 * ns]), list(res[2 * ns:])


def _loss_call(y, tgt, L):
    T = min(256, L)

    def kern(y_ref, t_ref, dy_ref, l_ref):
        i = pl.program_id(0)
        e = y_ref[...] - t_ref[...]
        dy_ref[...] = e * (1.0 / D_MODEL)

        @pl.when(i == 0)
        def _():
            l_ref[...] = jnp.zeros(l_ref.shape, F32)

        part = 0.5 * jnp.sum(jnp.sum(e * e, axis=-1, keepdims=True) * (1.0 / D_MODEL), axis=0, keepdims=True)
        l_ref[...] += jnp.broadcast_to(part, l_ref.shape)

    return pl.pallas_call(
        kern, name="loss_head", grid=(L // T,),
        in_specs=[_spec2(T, D_MODEL), _spec2(T, D_MODEL)],
        out_specs=[_spec2(T, D_MODEL), pl.BlockSpec((8, LANE), lambda i: (0, 0))],
        out_shape=[jax.ShapeDtypeStruct((L, D_MODEL), F32), jax.ShapeDtypeStruct((8, LANE), F32)],
        compiler_params=_cparams(("arbitrary",)),
    )(y, tgt)


_ANY = pl.BlockSpec(memory_space=pl.ANY)


def _coords():
    return lax.axis_index("x"), lax.axis_index("y"), lax.axis_index("c")


def _gather_chips(flat):
    def body(src, out, send_sems, recv_sems, lsem):
        x, y, c = _coords()
        me = 2 * x + y
        peers = [(1 - x, y), (x, 1 - y), (1 - x, 1 - y)]
        local = pltpu.make_async_copy(src, out.at[me], lsem)
        local.start()
        sends = [pltpu.make_async_remote_copy(src_ref=src, dst_ref=out.at[me], send_sem=send_sems.at[k],
                                              recv_sem=recv_sems.at[k], device_id=(px, py, c), device_id_type=MESH)
                 for k, (px, py) in enumerate(peers)]
        for s in sends:
            s.start()
        for k, (px, py) in enumerate(peers):
            pltpu.make_async_remote_copy(src_ref=src, dst_ref=out.at[2 * px + py], send_sem=send_sems.at[k],
                                         recv_sem=recv_sems.at[k], device_id=(px, py, c),
                                         device_id_type=MESH).wait_recv()
        for s in sends:
            s.wait_send()
        local.wait()

    return pl.pallas_call(
        body, name="gather_weights", in_specs=[_ANY], out_specs=_ANY,
        out_shape=jax.ShapeDtypeStruct((4,) + flat.shape, flat.dtype),
        scratch_shapes=[pltpu.SemaphoreType.DMA((3,)), pltpu.SemaphoreType.DMA((3,)), pltpu.SemaphoreType.DMA(())],
    )(flat)


def _scatter_grads(G, gs):
    def body(g_ref, gs_ref, out, outs, send_sems, recv_sems, ssend, srecv, lsems):
        x, y, c = _coords()
        me = 2 * x + y
        dev = 4 * x + 2 * y + c
        peers = [(1 - x, y), (x, 1 - y), (1 - x, 1 - y)]
        loc = [pltpu.make_async_copy(g_ref.at[me], out.at[me], lsems.at[0]),
               pltpu.make_async_copy(gs_ref, outs.at[dev], lsems.at[1])]
        for l_ in loc:
            l_.start()
        sends = [pltpu.make_async_remote_copy(src_ref=g_ref.at[2 * px + py], dst_ref=out.at[me],
                                              send_sem=send_sems.at[k], recv_sem=recv_sems.at[k],
                                              device_id=(px, py, c), device_id_type=MESH)
                 for k, (px, py) in enumerate(peers)]
        others = []
        for mask in range(1, 8):
            px, py, pc = x ^ (mask >> 2), y ^ ((mask >> 1) & 1), c ^ (mask & 1)
            others.append((px, py, pc))
            sends.append(pltpu.make_async_remote_copy(src_ref=gs_ref, dst_ref=outs.at[dev],
                                                      send_sem=ssend.at[mask - 1], recv_sem=srecv.at[mask - 1],
                                                      device_id=(px, py, pc), device_id_type=MESH))
        for s in sends:
            s.start()
        for k, (px, py) in enumerate(peers):
            pltpu.make_async_remote_copy(src_ref=g_ref.at[me], dst_ref=out.at[2 * px + py],
                                         send_sem=send_sems.at[k], recv_sem=recv_sems.at[k],
                                         device_id=(px, py, c), device_id_type=MESH).wait_recv()
        for k, (px, py, pc) in enumerate(others):
            pltpu.make_async_remote_copy(src_ref=gs_ref, dst_ref=outs.at[4 * px + 2 * py + pc],
                                         send_sem=ssend.at[k], recv_sem=srecv.at[k],
                                         device_id=(px, py, pc), device_id_type=MESH).wait_recv()
        for s in sends:
            s.wait_send()
        for l_ in loc:
            l_.wait()

    return pl.pallas_call(
        body, name="scatter_grads", in_specs=[_ANY, _ANY], out_specs=[_ANY, _ANY],
        out_shape=[jax.ShapeDtypeStruct(G.shape, G.dtype), jax.ShapeDtypeStruct((8,) + gs.shape, gs.dtype)],
        scratch_shapes=[pltpu.SemaphoreType.DMA((3,)), pltpu.SemaphoreType.DMA((3,)),
                        pltpu.SemaphoreType.DMA((7,)), pltpu.SemaphoreType.DMA((7,)),
                        pltpu.SemaphoreType.DMA((2,))],
    )(G, gs)


def _swap_cores(part):
    def body(src, out, send_sem, recv_sem, lsem):
        x, y, c = _coords()
        local = pltpu.make_async_copy(src, out.at[c], lsem)
        local.start()
        send = pltpu.make_async_remote_copy(src_ref=src, dst_ref=out.at[c], send_sem=send_sem, recv_sem=recv_sem,
                                            device_id=(x, y, 1 - c), device_id_type=MESH)
        send.start()
        pltpu.make_async_remote_copy(src_ref=src, dst_ref=out.at[1 - c], send_sem=send_sem, recv_sem=recv_sem,
                                     device_id=(x, y, 1 - c), device_id_type=MESH).wait_recv()
        send.wait_send()
        local.wait()

    return pl.pallas_call(
        body, name="swap_cores", in_specs=[_ANY], out_specs=_ANY,
        out_shape=jax.ShapeDtypeStruct((2,) + part.shape, part.dtype),
        scratch_shapes=[pltpu.SemaphoreType.DMA(()), pltpu.SemaphoreType.DMA(()), pltpu.SemaphoreType.DMA(())],
    )(part)


def _sum_slots(recv, tr):
    n, R, W = recv.shape

    def kern(r_ref, o_ref):
        acc = r_ref[0]
        for s in range(1, n):
            acc = acc + r_ref[s]
        o_ref[...] = acc

    return pl.pallas_call(
        kern, name="sum_chip_grads", grid=(R // tr,),
        in_specs=[pl.BlockSpec((n, tr, W), lambda i: (0, i, 0))],
        out_specs=pl.BlockSpec((tr, W), lambda i: (i, 0)),
        out_shape=jax.ShapeDtypeStruct((R, W), F32),
        compiler_params=_cparams(("arbitrary",)),
    )(recv)


def _adamw_call(name, slots, w, m, v, tr):
    n, R, W = slots.shape

    def kern(s_ref, w_ref, m_ref, v_ref, g_ref, d_ref, nm_ref, nv_ref):
        g = s_ref[0]
        for s in range(1, n):
            g = g + s_ref[s]
        m_ = ADAM_B1 * m_ref[...] + (1.0 - ADAM_B1) * g
        v_ = ADAM_B2 * v_ref[...] + (1.0 - ADAM_B2) * (g * g)
        m_hat = m_ / (1.0 - ADAM_B1 ** ADAM_STEP)
        v_hat = v_ / (1.0 - ADAM_B2 ** ADAM_STEP)
        g_ref[...] = g
        d_ref[...] = -ADAM_LR * (m_hat / (jnp.sqrt(v_hat) + ADAM_EPS) + ADAM_WD * w_ref[...])
        nm_ref[...] = m_
        nv_ref[...] = v_

    blk = pl.BlockSpec((tr, W), lambda i: (i, 0))
    return pl.pallas_call(
        kern, name=name, grid=(R // tr,),
        in_specs=[pl.BlockSpec((n, tr, W), lambda i: (0, i, 0)), blk, blk, blk],
        out_specs=[blk, blk, blk, blk],
        out_shape=[jax.ShapeDtypeStruct((R, W), F32)] * 4,
        compiler_params=_cparams(("arbitrary",)),
    )(slots, w, m, v)


def _pack(arrs, width, row_mult):
    flat = jnp.concatenate([a.reshape(-1) for a in arrs])
    n = flat.shape[0]
    rows = -(-n // width)
    rows = -(-rows // row_mult) * row_mult
    return jnp.pad(flat, (0, rows * width - n)).reshape(rows, width)


def _unpack(buf, shapes):
    flat = buf.reshape(-1)
    out, pos = [], 0
    for s in shapes:
        n = int(np.prod(s))
        out.append(flat[pos:pos + n].reshape(s))
        pos += n
    return out


def _rope_angles(L, rot_dim):
    rows = L // GRID_W
    row = jnp.repeat(jnp.arange(rows), GRID_W).astype(F32)
    col = jnp.tile(jnp.arange(GRID_W), rows).astype(F32)
    sec = rot_dim // 2
    inv_freq = ROPE_BASE ** (-jnp.arange(0, sec, 2, dtype=F32) / sec)
    ang_r = row[:, None] * inv_freq
    ang_c = col[:, None] * inv_freq
    ang = jnp.concatenate([ang_r, ang_r, ang_c, ang_c], axis=-1)
    return jnp.cos(ang), jnp.sin(ang)


def _rot_matrix(r):
    R = np.zeros((r, r), np.float32)
    q = r // 4
    for s in range(2):
        for t in range(q):
            lo = s * (r // 2) + t
            hi = lo + q
            R[hi, lo] = -1.0
            R[lo, hi] = 1.0
    return R


def _place_tables(L, cos, sin, width, offsets):
    r = cos.shape[1]
    Rm = np.zeros((width, width), np.float32)
    R = _rot_matrix(r)
    cs, ss, pos = [], [], 0
    for o in list(offsets) + [width]:
        if o > pos:
            cs.append(jnp.ones((L, o - pos), F32))
            ss.append(jnp.zeros((L, o - pos), F32))
        if o < width:
            cs.append(cos)
            ss.append(sin)
            Rm[o:o + r, o:o + r] = R
        pos = o + r
    return jnp.concatenate(cs, axis=1), jnp.concatenate(ss, axis=1), jnp.asarray(Rm)


def _head_mean_matrix(width, stride, n):
    M = np.zeros((width, width), np.float32)
    for o in range(0, width, stride):
        M[o:o + n, o:o + n] = 1.0 / n
    return jnp.asarray(M)


def _pad_heads(w, n_heads, real, padded, axis):
    parts = jnp.split(w, n_heads, axis=axis)
    padw = [(0, 0)] * w.ndim
    padw[axis] = (0, padded - real)
    return jnp.concatenate([jnp.pad(p, padw) for p in parts], axis=axis)


def _row128(v):
    v = v.reshape(1, -1)
    return jnp.pad(v, ((0, 0), (0, LANE - v.shape[1])))


def _conv_w8(w, b):
    C = w.shape[1]
    rows = [w, jnp.zeros((1, C), F32) if b is None else b.reshape(1, C), jnp.zeros((4, C), F32)]
    return jnp.concatenate(rows, axis=0)


def _build_layer(W):
    w_in = W['w_in']
    o = 0
    cols = {}
    for name, n in [('a_cq', A_Q_LORA), ('a_ckv', A_KV_LORA), ('a_kr', A_ROPE), ('b_q', 256), ('b_k', 128),
                    ('b_v', 128), ('c_z', 256), ('c_xbc', 512), ('c_dt', 8), ('d_qkv', 768), ('d_z', 256),
                    ('d_b', 8), ('d_a', 8)]:
        cols[name] = w_in[:, o:o + n]
        o += n
    padc = lambda a, lo, width: jnp.pad(a, ((0, 0), (lo, width - lo - a.shape[1])))
    pieces = {
        'b_q': _pad_heads(cols['b_q'], 4, HD, LANE, 1), 'c_xbc': cols['c_xbc'], 'a_cq': padc(cols['a_cq'], 0, 256),
        'b_k': _pad_heads(cols['b_k'], 2, HD, LANE, 1), 'd_qkv': cols['d_qkv'],
        'b_v': _pad_heads(cols['b_v'], 2, HD, LANE, 1), 'c_z': cols['c_z'], 'd_z': cols['d_z'],
        'a_ckv': cols['a_ckv'], 'a_kr': padc(cols['a_kr'], A_NOPE, LANE), 'c_dt': padc(cols['c_dt'], 0, LANE),
        'd_b': padc(cols['d_b'], 0, LANE), 'd_a': padc(cols['d_a'], 0, LANE),
        'pad': jnp.zeros((D_MODEL, LANE), F32)}
    out = {'w_in': jnp.concatenate([pieces[n] for n, _, _ in P_LAYOUT], axis=1)}
    out['a_q_norm'] = padc(W['a_q_norm'].reshape(1, -1), 0, 256)
    wuq = jnp.pad(W['a_w_uq'], ((0, 256 - A_Q_LORA), (0, 0)))
    out['a_w_uq'] = _pad_heads(wuq, 4, A_NOPE + A_ROPE, LANE, 1)
    out['a_kv_norm'] = W['a_kv_norm'].reshape(1, -1)
    ukv = W['a_w_ukv'].reshape(A_KV_LORA, HEADS, 2, HD)
    out['a_w_uk'] = _pad_heads(ukv[:, :, 0, :].reshape(A_KV_LORA, 256), 4, HD, LANE, 1)
    out['a_w_uv'] = _pad_heads(ukv[:, :, 1, :].reshape(A_KV_LORA, 256), 4, HD, LANE, 1)
    out['a_out_norm'] = _pad_heads(W['a_out_norm'].reshape(1, -1), 4, HD, LANE, 1)
    out['b_q_norm'] = _pad_heads(jnp.tile(W['b_q_norm'].reshape(1, -1), (1, 4)), 4, HD, LANE, 1)
    out['b_k_norm'] = _pad_heads(jnp.tile(W['b_k_norm'].reshape(1, -1), (1, 2)), 2, HD, LANE, 1)
    out['b_out_norm'] = _pad_heads(W['b_out_norm'].reshape(1, -1), 4, HD, LANE, 1)
    out['c_conv'] = _conv_w8(W['c_conv_w'], W['c_conv_b'])
    out['c_a_log'] = _row128(W['c_a_log'])
    out['c_dt_bias'] = _row128(W['c_dt_bias'])
    out['c_d_skip'] = jnp.repeat(W['c_d_skip'], HD).reshape(1, -1)
    out['c_out_norm'] = W['c_out_norm'].reshape(1, -1)
    out['d_conv'] = _conv_w8(W['d_conv_w'], None)
    out['d_a_log'] = _row128(W['d_a_log'])
    out['d_dt_bias'] = _row128(W['d_dt_bias'])
    out['d_out_norm'] = jnp.tile(W['d_out_norm'].reshape(1, -1), (1, 4))
    wo = W['w_out']
    out['w_out'] = jnp.concatenate([_pad_heads(wo[0:256], 4, HD, LANE, 0), _pad_heads(wo[256:512], 4, HD, LANE, 0),
                                    wo[512:1024]], axis=0)
    for n in ['pre_mix_norm', 'post_mix_norm', 'pre_ffn_norm', 'post_ffn_norm']:
        out[n] = W[n].reshape(1, -1)
    out['f_w_in'] = W['f_w_in']
    out['f_conv'] = _conv_w8(W['f_conv_w'], W['f_conv_b'])
    out['f_w_out'] = W['f_w_out']
    return out


def _fn_norm_in(a, p):
    return [_rms(a[0], p[0])]


def _fn_resid_norm2(a, p):
    x1 = a[0] + _rms(a[1], p[0])
    return [x1, _rms(x1, p[1])]


def _fn_resid_norm(a, p):
    return [a[0] + _rms(a[1], p[0])]


def _fn_a_prep(a, p):
    cq, ckv, kr, cosq, sinq, cosk, sink = a
    q_norm, w_uq, kv_norm, w_uk, w_uv, rq, rk = p
    q = _nn(_rms(cq, q_norm, A_Q_LORA), w_uq)
    q = q * cosq + _nn_h3(q, rq) * sinq
    kvn = _rms(ckv, kv_norm)
    kr_r = kr * cosk + _nn_h3(kr, rk) * sink
    kk = _nn(kvn, w_uk) + jnp.concatenate([kr_r] * HEADS, axis=1)
    return [q, kk, _nn(kvn, w_uv)]


def _fn_b_prep(a, p):
    q, k, v, cosq, sinq, cosk, sink = a
    q_norm, k_norm, mq, mk, rq, rk = p
    qn = q * lax.rsqrt(_nn_h3(q * q, mq) + EPS) * q_norm
    kn = k * lax.rsqrt(_nn_h3(k * k, mk) + EPS) * k_norm
    return [qn * cosq + _nn_h3(qn, rq) * sinq, kn * cosk + _nn_h3(kn, rk) * sink, v]


def _fn_mixer_post(a, p):
    oa, ob, yc0, yc1, xs, zc, od0, od1, zd = a
    a_norm, b_norm, dskip, c_norm, d_norm, m64 = p
    oc = _rms((yc0 + yc1 + xs * dskip) * _silu(zc), c_norm)
    od = od0 + od1
    odn = od * lax.rsqrt(_nn_h3(od * od, m64) + EPS) * d_norm * _silu(zd)
    return [jnp.concatenate([_rms(oa, a_norm, GROUP_W), _rms(ob, b_norm, GROUP_W), oc, odn], axis=1)]


def _fn_assemble(a, p):
    (dbq, dxbc, dcq, dbk, dqkv, dbv, dzc, dzd, dckv, dkr, ddt0, ddt1, db0, db1, da0, da1) = a
    return [jnp.concatenate([dbq, dxbc, dcq, dbk, dqkv, dbv, dzc, dzd, dckv, dkr, ddt0 + ddt1, db0 + db1,
                             da0 + da1, jnp.zeros_like(dckv)], axis=1)]


def _pspec(T, name):
    off, w = P_OFF[name]
    return _spec2(T, w, off // w)


def _layer_fwd(l, x, h, K, tabs, L, T, next_norm):
    n = f"l{l}_"
    sv = {'x': x, 'h': h}
    p = _mm(n + "in_proj", h, K['w_in'].astype(BF16), 'nn', F32, 512, 768, 1024)
    sv['p'] = p
    a_acts = [(p, _pspec(T, 'a_cq')), (p, _pspec(T, 'a_ckv')), (p, _pspec(T, 'a_kr')),
              (tabs['a_cq'], _spec2(T, 512)), (tabs['a_sq'], _spec2(T, 512)),
              (tabs['a_ck'], _spec2(T, LANE)), (tabs['a_sk'], _spec2(T, LANE))]
    a_pars = [K['a_q_norm'], K['a_w_uq'], K['a_kv_norm'], K['a_w_uk'], K['a_w_uv'], tabs['a_rq'], tabs['a_rk']]
    qa, ka, va = _tw_fwd(n + "a_prep", _fn_a_prep, a_acts, a_pars, [(512, BF16)] * 3, L, T)
    oa, lse_a = _flash_fwd(n + "a_attn", qa, ka, va, HEADS, 1, (A_NOPE + A_ROPE) ** -0.5, L)
    sv.update(a_acts=a_acts, a_pars=a_pars, qa=qa, ka=ka, va=va, oa=oa, lse_a=lse_a)
    b_acts = [(p, _pspec(T, 'b_q')), (p, _pspec(T, 'b_k')), (p, _pspec(T, 'b_v')),
              (tabs['b_cq'], _spec2(T, 512)), (tabs['b_sq'], _spec2(T, 512)),
              (tabs['b_ck'], _spec2(T, 256)), (tabs['b_sk'], _spec2(T, 256))]
    b_pars = [K['b_q_norm'], K['b_k_norm'], tabs['b_mq'], tabs['b_mk'], tabs['b_rq'], tabs['b_rk']]
    qb, kb, vb = _tw_fwd(n + "b_prep", _fn_b_prep, b_acts, b_pars, [(512, BF16), (256, BF16), (256, BF16)], L, T)
    ob, lse_b = _flash_fwd(n + "b_attn", qb, kb, vb, HEADS, 2, HD ** -0.5, L)
    sv.update(b_acts=b_acts, b_pars=b_pars, qb=qb, kb=kb, vb=vb, ob=ob, lse_b=lse_b)
    xbc = _conv_fwd(n + "c_conv", p, P_OFF['c_xbc'][0], C_XBC, K['c_conv'], True, L, 512)
    c_seqs = [(xbc, C_XBC, 0, [(0, HD, 4), (256, HD, 2), (384, HD, 2)]),
              (p, LANE, P_OFF['c_dt'][0] // LANE, None)]
    c_rows = [K['c_a_log'], K['c_dt_bias']]
    yc0, yc1, sc0, sc1 = _scan_fwd(n + "c_ssd", _ssd_chunk, c_seqs, c_rows, C_CHUNK, L, C_PER_STEP)
    sv.update(xbc=xbc, c_seqs=c_seqs, c_rows=c_rows, sc=(sc0, sc1))
    qkv = _conv_fwd(n + "d_conv", p, P_OFF['d_qkv'][0], D_QKV, K['d_conv'], True, L, 768)
    d_seqs = [(qkv, D_QKV, 0, [(0, HD, 4), (256, HD, 4), (512, HD, 4)]),
              (p, LANE, P_OFF['d_b'][0] // LANE, None), (p, LANE, P_OFF['d_a'][0] // LANE, None)]
    d_rows = [K['d_a_log'], K['d_dt_bias']]
    od0, od1, sd0, sd1 = _scan_fwd(n + "d_delta", _delta_chunk, d_seqs, d_rows, D_CHUNK, L, D_PER_STEP)
    sv.update(qkv=qkv, d_seqs=d_seqs, d_rows=d_rows, sd=(sd0, sd1))
    m_acts = [(oa, _spec2(T, 512)), (ob, _spec2(T, 512)), (yc0, _spec2(T, 256)), (yc1, _spec2(T, 256)),
              (xbc, _spec2(T, 256, 0)), (p, _pspec(T, 'c_z')), (od0, _spec2(T, 256)), (od1, _spec2(T, 256)),
              (p, _pspec(T, 'd_z'))]
    m_pars = [K['a_out_norm'], K['b_out_norm'], K['c_d_skip'], K['c_out_norm'], K['d_out_norm'], tabs['m64']]
    (o,) = _tw_fwd(n + "mixer_post", _fn_mixer_post, m_acts, m_pars, [(O_COLS, BF16)], L, T)
    f1 = _mm(n + "out_proj", o, K['w_out'].astype(BF16), 'nn', F32, 512, 1024, 768)
    r1_pars = [K['post_mix_norm'], K['pre_ffn_norm']]
    x1, h2 = _tw_fwd(n + "resid_mix", _fn_resid_norm2, [(x, _spec2(T, D_MODEL)), (f1, _spec2(T, D_MODEL))], r1_pars,
                     [(D_MODEL, F32), (D_MODEL, BF16)], L, T)
    sv.update(m_acts=m_acts, m_pars=m_pars, o=o, f1=f1, r1_pars=r1_pars, x1=x1, h2=h2)
    u = _mm(n + "ffn_in", h2, K['f_w_in'].astype(BF16), 'nn', F32, 512, 512, 1024)
    gu = _conv_fwd(n + "ffn_conv", u, 0, 2 * D_FF, K['f_conv'], False, L, 1408)
    act = _glu_fwd(n + "ffn_glu", gu, L)
    f2 = _mm(n + "ffn_out", act, K['f_w_out'].astype(BF16), 'nn', F32, 512, 1024, 1408)
    sv.update(u=u, gu=gu, act=act, f2=f2)
    xf = [(x1, _spec2(T, D_MODEL)), (f2, _spec2(T, D_MODEL))]
    if next_norm is None:
        (x2,) = _tw_fwd(n + "resid_ffn", _fn_resid_norm, xf, [K['post_ffn_norm']], [(D_MODEL, F32)], L, T)
        hn = None
    else:
        x2, hn = _tw_fwd(n + "resid_ffn", _fn_resid_norm2, xf, [K['post_ffn_norm'], next_norm],
                         [(D_MODEL, F32), (D_MODEL, BF16)], L, T)
    return x2, hn, sv


def _layer_bwd(l, dx2, dhn, K, sv, tabs, L, T, next_norm):
    n = f"l{l}b_"
    dK = {}
    s2 = lambda w, cb=0: _spec2(T, w, cb)
    xf = [(sv['x1'], s2(D_MODEL)), (sv['f2'], s2(D_MODEL))]
    if next_norm is None:
        (dx1a, df2), (dK['post_ffn_norm'],) = _tw_bwd(n + "resid_ffn", _fn_resid_norm, xf, [K['post_ffn_norm']],
                                                      [(dx2, s2(D_MODEL))], L, T, [True, True], [True])
        dnext = None
    else:
        (dx1a, df2), (dK['post_ffn_norm'], dnext) = _tw_bwd(
            n + "resid_ffn", _fn_resid_norm2, xf, [K['post_ffn_norm'], next_norm],
            [(dx2, s2(D_MODEL)), (dhn, s2(D_MODEL))], L, T, [True, True], [True, True])
    dact = _mm(n + "ffn_out_dx", df2, K['f_w_out'].astype(BF16), 'nt', F32, 512, 1408, 1024)
    dK['f_w_out'] = _mm(n + "ffn_out_dw", sv['act'], df2, 'tn', F32, 1408, 1024, 512)
    dgu = _glu_bwd(n + "ffn_glu", sv['gu'], dact, L)
    du, dK['f_conv'] = _conv_bwd(n + "ffn_conv", sv['u'], 0, 2 * D_FF, K['f_conv'], False, [(dgu, None)], L, 1408)
    dh2 = _mm(n + "ffn_in_dx", du, K['f_w_in'].astype(BF16), 'nt', F32, 512, 1024, 512)
    dK['f_w_in'] = _mm(n + "ffn_in_dw", sv['h2'], du, 'tn', F32, 512, 512, 512)
    (dxa, df1), (dK['post_mix_norm'], dK['pre_ffn_norm']) = _tw_bwd(
        n + "resid_mix", _fn_resid_norm2, [(sv['x'], s2(D_MODEL)), (sv['f1'], s2(D_MODEL))], sv['r1_pars'],
        [(dx1a, s2(D_MODEL)), (dh2, s2(D_MODEL))], L, T, [True, True], [True, True])
    do = _mm(n + "out_proj_dx", df1, K['w_out'].astype(BF16), 'nt', F32, 512, 768, 1024)
    dK['w_out'] = _mm(n + "out_proj_dw", sv['o'], df1, 'tn', F32, 768, 1024, 512)
    (doa, dob, dyc0, _, dxs_skip, dzc, dod0, _, dzd), mp = _tw_bwd(
        n + "mixer_post", _fn_mixer_post, sv['m_acts'], sv['m_pars'], [(do, s2(O_COLS))], L, T,
        [True] * 9, [True] * 5 + [False])
    dK['a_out_norm'], dK['b_out_norm'], dK['c_d_skip'], dK['c_out_norm'], dK['d_out_norm'] = mp
    (dqkv0, db0, da0), (dqkv1, db1, da1), (dK['d_a_log'], dK['d_dt_bias']) = _scan_bwd(
        n + "d_delta", _delta_chunk, sv['d_seqs'], sv['d_rows'], sv['sd'], dod0, None, D_CHUNK, L, D_PER_STEP)
    dqkv, dK['d_conv'] = _conv_bwd(n + "d_conv", sv['p'], P_OFF['d_qkv'][0], D_QKV, K['d_conv'], True,
                                   [(dqkv0, None), (dqkv1, None)], L, 768)
    (dxbc0, ddt0), (dxbc1, ddt1), (dK['c_a_log'], dK['c_dt_bias']) = _scan_bwd(
        n + "c_ssd", _ssd_chunk, sv['c_seqs'], sv['c_rows'], sv['sc'], dyc0, dxs_skip, C_CHUNK, L, C_PER_STEP)
    dxbc, dK['c_conv'] = _conv_bwd(n + "c_conv", sv['p'], P_OFF['c_xbc'][0], C_XBC, K['c_conv'], True,
                                   [(dxbc0, None), (dxbc1, None)], L, 512)
    dqb, dkb, dvb = _flash_bwd(n + "b_attn", sv['qb'], sv['kb'], sv['vb'], sv['ob'], sv['lse_b'], dob, HEADS, 2,
                               HD ** -0.5, L)
    (dbq, dbk, dbv), (dK['b_q_norm'], dK['b_k_norm']) = _tw_bwd(
        n + "b_prep", _fn_b_prep, sv['b_acts'], sv['b_pars'], [(dqb, s2(512)), (dkb, s2(256)), (dvb, s2(256))],
        L, T, [True] * 3 + [False] * 4, [True, True] + [False] * 4)
    dqa, dka, dva = _flash_bwd(n + "a_attn", sv['qa'], sv['ka'], sv['va'], sv['oa'], sv['lse_a'], doa, HEADS, 1,
                               (A_NOPE + A_ROPE) ** -0.5, L)
    (dcq, dckv, dkr), ap = _tw_bwd(
        n + "a_prep", _fn_a_prep, sv['a_acts'], sv['a_pars'], [(dqa, s2(512)), (dka, s2(512)), (dva, s2(512))],
        L, T, [True] * 3 + [False] * 4, [True] * 5 + [False] * 2)
    dK['a_q_norm'], dK['a_w_uq'], dK['a_kv_norm'], dK['a_w_uk'], dK['a_w_uv'] = ap
    pieces = [(dbq, s2(512)), (dxbc, s2(512)), (dcq, s2(256)), (dbk, s2(256)), (dqkv, s2(768)), (dbv, s2(256)),
              (dzc, s2(256)), (dzd, s2(256)), (dckv, s2(LANE)), (dkr, s2(LANE)),
              (ddt0, s2(LANE)), (ddt1, s2(LANE)), (db0, s2(LANE)), (db1, s2(LANE)), (da0, s2(LANE)),
              (da1, s2(LANE))]
    (dp,) = _tw_fwd(n + "assemble_dp", _fn_assemble, pieces, [], [(P_COLS, BF16)], L, T)
    dh = _mm(n + "in_proj_dx", dp, K['w_in'].astype(BF16), 'nt', F32, 512, 1024, 768)
    dK['w_in'] = _mm(n + "in_proj_dw", sv['h'], dp, 'tn', F32, 512, 768, 512)
    return dxa, dh, dK, dnext


def _tables(L):
    ca, sa = _rope_angles(L, A_ROPE)
    cb, sb = _rope_angles(L, HD)
    t = {}
    t['a_cq'], t['a_sq'], t['a_rq'] = _place_tables(L, ca, sa, 512, [LANE * h + A_NOPE for h in range(4)])
    t['a_ck'], t['a_sk'], t['a_rk'] = _place_tables(L, ca, sa, LANE, [A_NOPE])
    t['b_cq'], t['b_sq'], t['b_rq'] = _place_tables(L, cb, sb, 512, [LANE * h for h in range(4)])
    t['b_ck'], t['b_sk'], t['b_rk'] = _place_tables(L, cb, sb, 256, [LANE * h for h in range(2)])
    t['b_mq'] = _head_mean_matrix(512, LANE, HD)
    t['b_mk'] = _head_mean_matrix(256, LANE, HD)
    t['m64'] = _head_mean_matrix(256, HD, HD)
    return t


def kernel(x, pre_mix_norm, w_in, a_q_norm, a_w_uq, a_kv_norm, a_w_ukv, a_out_norm, b_q_norm, b_k_norm, b_out_norm, c_conv_w, c_conv_b, c_a_log, c_dt_bias, c_d_skip, c_out_norm, d_conv_w, d_a_log, d_dt_bias, d_out_norm, w_out, post_mix_norm, pre_ffn_norm, f_w_in, f_conv_w, f_conv_b, f_w_out, post_ffn_norm, loss_target, m_pre_mix_norm, m_w_in, m_a_q_norm, m_a_w_uq, m_a_kv_norm, m_a_w_ukv, m_a_out_norm, m_b_q_norm, m_b_k_norm, m_b_out_norm, m_c_conv_w, m_c_conv_b, m_c_a_log, m_c_dt_bias, m_c_d_skip, m_c_out_norm, m_d_conv_w, m_d_a_log, m_d_dt_bias, m_d_out_norm, m_w_out, m_post_mix_norm, m_pre_ffn_norm, m_f_w_in, m_f_conv_w, m_f_conv_b, m_f_w_out, m_post_ffn_norm, v_pre_mix_norm, v_w_in, v_a_q_norm, v_a_w_uq, v_a_kv_norm, v_a_w_ukv, v_a_out_norm, v_b_q_norm, v_b_k_norm, v_b_out_norm, v_c_conv_w, v_c_conv_b, v_c_a_log, v_c_dt_bias, v_c_d_skip, v_c_out_norm, v_d_conv_w, v_d_a_log, v_d_dt_bias, v_d_out_norm, v_w_out, v_post_mix_norm, v_pre_ffn_norm, v_f_w_in, v_f_conv_w, v_f_conv_b, v_f_w_out, v_post_ffn_norm):
    loc = locals()
    Wl = {n: loc[n] for n in WEIGHTS}
    Ml = {n: loc['m_' + n] for n in WEIGHTS}
    Vl = {n: loc['v_' + n] for n in WEIGHTS}
    L = x.shape[1]
    T = min(256, L)
    x0 = x.reshape(L, D_MODEL)
    tgt = loss_target.reshape(L, D_MODEL)

    shard_shapes = [Wl[n].shape for n in SHARDED]
    w_flat = _pack([Wl[n] for n in SHARDED], 1024, 256)
    gathered = _gather_chips(w_flat)
    per_chip = [_unpack(gathered[j], shard_shapes) for j in range(4)]
    full = dict(Wl)
    for k, n in enumerate(SHARDED):
        full[n] = jnp.concatenate([per_chip[j][k] for j in range(4)], axis=SHARD_AXIS[n])

    tabs = _tables(L)
    Ks, builds = [], []
    for l in range(DEPTH):
        K, vjp_l = jax.vjp(_build_layer, {n: full[n][l] for n in WEIGHTS})
        Ks.append(K)
        builds.append(vjp_l)

    (h,) = _tw_fwd("l0_norm_in", _fn_norm_in, [(x0, _spec2(T, D_MODEL))], [Ks[0]['pre_mix_norm']],
                   [(D_MODEL, BF16)], L, T)
    xs, saves = x0, []
    for l in range(DEPTH):
        nxt = Ks[l + 1]['pre_mix_norm'] if l + 1 < DEPTH else None
        xs, h, sv = _layer_fwd(l, xs, h, Ks[l], tabs, L, T, nxt)
        saves.append(sv)
    dy, loss_acc = _loss_call(xs, tgt, L)
    loss = lax.psum(loss_acc[0, 0], ("x", "y", "c"))

    grads = [None] * DEPTH
    dx, dhn = dy, None
    for l in reversed(range(DEPTH)):
        nxt = Ks[l + 1]['pre_mix_norm'] if l + 1 < DEPTH else None
        dxa, dh, dK, dnext = _layer_bwd(l, dx, dhn, Ks[l], saves[l], tabs, L, T, nxt)
        if dnext is not None:
            grads[l + 1]['pre_mix_norm'] = dnext
        grads[l] = dK
        if l > 0:
            dx, dhn = dxa, dh
        else:
            (dx_in,), (dpre,) = _tw_bwd("l0b_norm_in", _fn_norm_in, [(x0, _spec2(T, D_MODEL))],
                                        [Ks[0]['pre_mix_norm']], [(dh, _spec2(T, D_MODEL))], L, T, [True], [True],
                                        addto={0: (dxa, _spec2(T, D_MODEL))})
            grads[0]['pre_mix_norm'] = dpre
    gfull = []
    for l in range(DEPTH):
        zero = jax.tree.map(jnp.zeros_like, Ks[l])
        zero.update(grads[l])
        (gl,) = builds[l](zero)
        gfull.append(gl)
    gW = {n: jnp.stack([gfull[l][n] for l in range(DEPTH)]) for n in WEIGHTS}

    G = jnp.stack([_pack([jnp.split(gW[n], 4, axis=SHARD_AXIS[n])[j] for n in SHARDED], 1024, 256)
                   for j in range(4)])
    small_shapes = [Wl[n].shape for n in SMALL]
    gs = _pack([gW[n] for n in SMALL], LANE, 8)
    recv, recv_small = _scatter_grads(G, gs)
    pair = _swap_cores(_sum_slots(recv, 256))
    big = _adamw_call("adamw_sharded", pair, w_flat, _pack([Ml[n] for n in SHARDED], 1024, 256),
                      _pack([Vl[n] for n in SHARDED], 1024, 256), 256)
    small = _adamw_call("adamw_small", recv_small, _pack([Wl[n] for n in SMALL], LANE, 8),
                        _pack([Ml[n] for n in SMALL], LANE, 8), _pack([Vl[n] for n in SMALL], LANE, 8), gs.shape[0])
    res = {}
    for kind, b, s in zip(['grad', 'delta', 'new_m', 'new_v'], big, small):
        for n, a in zip(SHARDED, _unpack(b, shard_shapes)):
            res[kind, n] = a
        for n, a in zip(SMALL, _unpack(s, small_shapes)):
            res[kind, n] = a
    outs = [loss, dx_in.reshape(x.shape)]
    for kind in ['grad', 'delta', 'new_m', 'new_v']:
        outs += [res[kind, n] for n in WEIGHTS]
    return tuple(outs)
```

```python
import functools
import math

import numpy as np
import jax
import jax.numpy as jnp
from jax import lax
from jax.experimental import pallas as pl
from jax.experimental.pallas import tpu as pltpu

F32 = jnp.float32
BF16 = jnp.bfloat16
MESH = pl.DeviceIdType.MESH
VMEM_LIMIT = 48 * 1024 * 1024
LANE = 128

D_MODEL = 1024
DEPTH = 2
GRID_W = 64
ROPE_BASE = 10000.0
EPS = 1e-6
GROUP_W = 256
HEADS = 4
HD = 64
A_NOPE, A_ROPE, A_Q_LORA, A_KV_LORA = 64, 32, 192, 128
A_COLS = A_Q_LORA + A_KV_LORA + A_ROPE
B_COLS = 512
C_XBC = 512
C_COLS = GROUP_W + C_XBC + 8
D_QKV = 768
D_COLS = D_QKV + GROUP_W + 16
IN_COLS = A_COLS + B_COLS + C_COLS + D_COLS
C_CHUNK = 128
D_CHUNK = 64
C_PER_STEP = 1
D_PER_STEP = 2
D_FF = 2816
ADAM_LR, ADAM_B1, ADAM_B2, ADAM_EPS, ADAM_WD, ADAM_STEP = 0.001, 0.9, 0.999, 1e-08, 0.01, 10

WEIGHTS = ['pre_mix_norm', 'w_in', 'a_q_norm', 'a_w_uq', 'a_kv_norm', 'a_w_ukv', 'a_out_norm', 'b_q_norm',
           'b_k_norm', 'b_out_norm', 'c_conv_w', 'c_conv_b', 'c_a_log', 'c_dt_bias', 'c_d_skip', 'c_out_norm',
           'd_conv_w', 'd_a_log', 'd_dt_bias', 'd_out_norm', 'w_out', 'post_mix_norm', 'pre_ffn_norm', 'f_w_in',
           'f_conv_w', 'f_conv_b', 'f_w_out', 'post_ffn_norm']
SHARD_AXIS = {'w_in': 2, 'a_w_uq': 2, 'a_w_ukv': 2, 'c_conv_w': 2, 'd_conv_w': 2, 'w_out': 1, 'f_w_in': 2,
              'f_conv_w': 2, 'f_w_out': 1}
SHARDED = [n for n in WEIGHTS if n in SHARD_AXIS]
SMALL = [n for n in WEIGHTS if n not in SHARD_AXIS]
MXU_WEIGHTS = ('w_in', 'a_w_uq', 'a_w_ukv', 'w_out', 'f_w_in', 'f_w_out')

P_LAYOUT = [('b_q', 0, 512), ('c_xbc', 512, 512), ('a_cq', 1024, 256), ('b_k', 1280, 256), ('d_qkv', 1536, 768),
            ('b_v', 2304, 256), ('c_z', 2560, 256), ('d_z', 2816, 256), ('a_ckv', 3072, 128), ('a_kr', 3200, 128),
            ('c_dt', 3328, 128), ('d_b', 3456, 128), ('d_a', 3584, 128), ('pad', 3712, 128)]
P_OFF = {n: (o, w) for n, o, w in P_LAYOUT}
P_COLS = 3840
O_COLS = 1536


def _cparams(sem):
    return pltpu.CompilerParams(dimension_semantics=sem, vmem_limit_bytes=VMEM_LIMIT)


def _tile(n, target):
    best = None
    for d in range(LANE, min(n, target) + 1, LANE):
        if n % d == 0:
            best = d
    return best if best is not None else n


_NN = ((1,), (0,))
_NT = ((1,), (1,))
_TN = ((0,), (0,))


def _raw_dot(a, b, dims, hi):
    if hi:
        prec = lax.Precision.HIGH if hi == 'high' else lax.Precision.HIGHEST
        return lax.dot_general(a, b, (dims, ((), ())), precision=prec, preferred_element_type=F32)
    return lax.dot_general(a.astype(BF16), b.astype(BF16), (dims, ((), ())), preferred_element_type=F32)


def _make_dots(hi):
    @jax.custom_vjp
    def nn(a, b):
        return _raw_dot(a, b, _NN, hi)

    @jax.custom_vjp
    def nt(a, b):
        return _raw_dot(a, b, _NT, hi)

    @jax.custom_vjp
    def tn(a, b):
        return _raw_dot(a, b, _TN, hi)

    nn.defvjp(lambda a, b: (nn(a, b), (a, b)), lambda r, g: (nt(g, r[1]), tn(r[0], g)))
    nt.defvjp(lambda a, b: (nt(a, b), (a, b)), lambda r, g: (nn(g, r[1]), tn(g, r[0])))
    tn.defvjp(lambda a, b: (tn(a, b), (a, b)), lambda r, g: (nt(r[1], g), nn(r[0], g)))
    return nn, nt, tn


_nn, _nt, _tn = _make_dots(False)
_nn_hi, _nt_hi, _tn_hi = _make_dots(True)
_nn_h3, _nt_h3, _tn_h3 = _make_dots('high')


def _sigmoid(x):
    return 1.0 / (1.0 + jnp.exp(-x))


def _silu(x):
    return x * _sigmoid(x)


def _softplus(x):
    return jnp.maximum(x, 0.0) + jnp.log(1.0 + jnp.exp(-jnp.abs(x)))


def _rms(x, w, n=None):
    n = x.shape[-1] if n is None else n
    ms = jnp.sum(x * x, axis=-1, keepdims=True) * (1.0 / n)
    return x * lax.rsqrt(ms + EPS) * w


def _spec2(T, w, cb=0):
    return pl.BlockSpec((T, w), lambda i: (i, cb))


def _full_spec(a):
    nd = a.ndim
    return pl.BlockSpec(a.shape, lambda i: (0,) * nd)


def _tw_fwd(name, fn, acts, params, outs, L, T):
    na, npar = len(acts), len(params)

    def kern(*refs):
        a = [r[...].astype(F32) for r in refs[:na]]
        p = [r[...].astype(F32) for r in refs[na:na + npar]]
        res = fn(a, p)
        for r, o in zip(refs[na + npar:], res):
            r[...] = o.astype(r.dtype)

    return pl.pallas_call(
        kern, name=name, grid=(L // T,),
        in_specs=[s for _, s in acts] + [_full_spec(p) for p in params],
        out_specs=[_spec2(T, w) for w, _ in outs],
        out_shape=[jax.ShapeDtypeStruct((L, w), dt) for w, dt in outs],
        compiler_params=_cparams(("arbitrary",)),
    )(*[a for a, _ in acts], *params)


def _tw_bwd(name, fn, acts, params, douts, L, T, act_grad, par_grad, addto=None):
    na, npar, nd = len(acts), len(params), len(douts)
    addto = addto or {}
    add_keys = sorted(addto)
    ga = [k for k in range(na) if act_grad[k]]
    gp = [k for k in range(npar) if par_grad[k]]

    def kern(*refs):
        i = pl.program_id(0)
        a = [r[...].astype(F32) for r in refs[:na]]
        p = [r[...].astype(F32) for r in refs[na:na + npar]]
        g = [r[...].astype(F32) for r in refs[na + npar:na + npar + nd]]
        pos = na + npar + nd
        adds = [r[...].astype(F32) for r in refs[pos:pos + len(add_keys)]]
        pos += len(add_keys)
        da_refs = refs[pos:pos + len(ga)]
        dp_refs = refs[pos + len(ga):]

        def f(ad, pd):
            af, pf = list(a), list(p)
            for k, v in zip(ga, ad):
                af[k] = v
            for k, v in zip(gp, pd):
                pf[k] = v
            return fn(af, pf)

        _, vjp = jax.vjp(f, [a[k] for k in ga], [p[k] for k in gp])
        dad, dpd = vjp(list(g))
        for n, (r, d) in enumerate(zip(da_refs, dad)):
            if n in addto:
                d = d + adds[add_keys.index(n)]
            r[...] = d.astype(r.dtype)

        @pl.when(i == 0)
        def _():
            for r in dp_refs:
                r[...] = jnp.zeros(r.shape, F32)

        for r, d in zip(dp_refs, dpd):
            r[...] += d

    def width(spec):
        return spec.block_shape[-1]

    res = pl.pallas_call(
        kern, name=name, grid=(L // T,),
        in_specs=[s for _, s in acts] + [_full_spec(p) for p in params] + [s for _, s in douts]
        + [addto[k][1] for k in add_keys],
        out_specs=[_spec2(T, width(acts[k][1])) for k in ga] + [_full_spec(params[k]) for k in gp],
        out_shape=[jax.ShapeDtypeStruct((L, width(acts[k][1])), F32) for k in ga]
        + [jax.ShapeDtypeStruct(params[k].shape, F32) for k in gp],
        compiler_params=_cparams(("arbitrary",)),
    )(*[a for a, _ in acts], *params, *[a for a, _ in douts], *[addto[k][0] for k in add_keys])
    return list(res[:len(ga)]), list(res[len(ga):])


def _mm(name, a, b, mode, out_dtype, tm, tn, tk):
    if mode == 'nn':
        (M, K), N = a.shape, b.shape[1]
    elif mode == 'nt':
        (M, K), N = a.shape, b.shape[0]
    else:
        (K, M), N = a.shape, b.shape[1]
    tm, tn, tk = _tile(M, tm), _tile(N, tn), _tile(K, tk)
    nk = K // tk
    if mode == 'nn':
        a_spec = pl.BlockSpec((tm, tk), lambda i, j, k: (i, k))
        b_spec = pl.BlockSpec((tk, tn), lambda i, j, k: (k, j))
        dims = _NN
    elif mode == 'nt':
        a_spec = pl.BlockSpec((tm, tk), lambda i, j, k: (i, k))
        b_spec = pl.BlockSpec((tn, tk), lambda i, j, k: (j, k))
        dims = _NT
    else:
        a_spec = pl.BlockSpec((tk, tm), lambda i, j, k: (k, i))
        b_spec = pl.BlockSpec((tk, tn), lambda i, j, k: (k, j))
        dims = _TN

    def kern(a_ref, b_ref, o_ref, acc):
        k = pl.program_id(2)

        @pl.when(k == 0)
        def _():
            acc[...] = jnp.zeros(acc.shape, F32)

        acc[...] += lax.dot_general(a_ref[...].astype(BF16), b_ref[...].astype(BF16), (dims, ((), ())),
                                    preferred_element_type=F32)

        @pl.when(k == nk - 1)
        def _():
            o_ref[...] = acc[...].astype(o_ref.dtype)

    return pl.pallas_call(
        kern, name=name, grid=(M // tm, N // tn, nk),
        in_specs=[a_spec, b_spec],
        out_specs=pl.BlockSpec((tm, tn), lambda i, j, k: (i, j)),
        out_shape=jax.ShapeDtypeStruct((M, N), out_dtype),
        scratch_shapes=[pltpu.VMEM((tm, tn), F32)],
        compiler_params=_cparams(("arbitrary", "arbitrary", "arbitrary")),
    )(a, b)


def _host(kern, n_in, n_out, side, grid):
    if side is None:
        return kern, [], [], [], [], []
    ni, no = len(side.ins), len(side.out_shapes)

    def hosted(*refs):
        ins, s_in = refs[:n_in], refs[n_in:n_in + ni]
        outs = refs[n_in + ni:n_in + ni + n_out]
        s_out = refs[n_in + ni + n_out:n_in + ni + n_out + no]
        sems = refs[n_in + ni + n_out + no:]
        a, b = pl.program_id(0), pl.program_id(1)

        @pl.when((a == 0) & (b == 0))
        def _():
            side.start(s_in, s_out, sems)

        kern(*ins, *outs)

        @pl.when((a == grid[0] - 1) & (b == grid[1] - 1))
        def _():
            side.finish(s_in, s_out, sems)

    return hosted, [_ANY] * ni, [_ANY] * no, side.out_shapes, side.scratch(), side.ins


def _flash_fwd(name, q, k, v, H, rep, scale, L, side=None):
    tq = min(256, L)
    nq = L // tq
    KC = min(2048, L)
    nkc = L // KC

    def kern(q_ref, k_ref, v_ref, o_ref, lse_ref):
        qb = q_ref[...]
        m = jnp.full((tq, 1), -1e30, F32)
        l = jnp.zeros((tq, 1), F32)
        acc = jnp.zeros((tq, LANE), F32)
        for c in range(nkc):
            kb = k_ref[c * KC:(c + 1) * KC, :]
            vb = v_ref[c * KC:(c + 1) * KC, :]
            s = lax.dot_general(qb, kb, (_NT, ((), ())), preferred_element_type=F32) * scale
            mn = jnp.maximum(m, jnp.max(s, axis=-1, keepdims=True))
            al = jnp.exp(m - mn)
            p = jnp.exp(s - mn)
            l = al * l + jnp.sum(p, axis=-1, keepdims=True)
            acc = al * acc + lax.dot_general(p.astype(BF16), vb, (_NN, ((), ())), preferred_element_type=F32)
            m = mn
        o_ref[...] = acc / l
        lse_ref[...] = m + jnp.log(l)

    kern, s_in, s_out, s_shapes, s_scratch, s_args = _host(kern, 3, 2, side, (H, nq))
    res = pl.pallas_call(
        kern, name=name, grid=(H, nq),
        in_specs=[pl.BlockSpec((tq, LANE), lambda h, i: (i, h)),
                  pl.BlockSpec((L, LANE), lambda h, i: (0, h // rep)),
                  pl.BlockSpec((L, LANE), lambda h, i: (0, h // rep))] + s_in,
        out_specs=[pl.BlockSpec((tq, LANE), lambda h, i: (i, h)),
                   pl.BlockSpec((tq, 1), lambda h, i: (h * nq + i, 0))] + s_out,
        out_shape=[jax.ShapeDtypeStruct((L, H * LANE), F32), jax.ShapeDtypeStruct((H * L, 1), F32)] + s_shapes,
        scratch_shapes=s_scratch,
        compiler_params=_cparams(("arbitrary", "arbitrary")),
    )(q, k, v, *s_args)
    return res[0], res[1], list(res[2:])


def _flash_bwd(name, q, k, v, o, lse, do, H, rep, scale, L, side=None):
    tq = min(256, L)
    nq = L // tq
    KC = min(2048, L)
    nkc = L // KC
    Hkv = H // rep

    def kern(q_ref, k_ref, v_ref, o_ref, lse_ref, do_ref, dq_ref, dk_ref, dv_ref):
        h = pl.program_id(0)
        i = pl.program_id(1)

        @pl.when((i == 0) & (h % rep == 0))
        def _():
            dk_ref[...] = jnp.zeros(dk_ref.shape, F32)
            dv_ref[...] = jnp.zeros(dv_ref.shape, F32)

        qb = q_ref[...]
        do = do_ref[...]
        dob = do.astype(BF16)
        delta = jnp.sum(do * o_ref[...], axis=-1, keepdims=True)
        lse = lse_ref[...]
        dq = jnp.zeros((tq, LANE), F32)
        for c in range(nkc):
            sl = slice(c * KC, (c + 1) * KC)
            kb = k_ref[sl, :]
            vb = v_ref[sl, :]
            s = lax.dot_general(qb, kb, (_NT, ((), ())), preferred_element_type=F32) * scale
            p = jnp.exp(s - lse)
            dp = lax.dot_general(dob, vb, (_NT, ((), ())), preferred_element_type=F32)
            ds = (p * (dp - delta) * scale).astype(BF16)
            dq = dq + lax.dot_general(ds, kb, (_NN, ((), ())), preferred_element_type=F32)
            dk_ref[sl, :] += lax.dot_general(ds, qb, (_TN, ((), ())), preferred_element_type=F32)
            dv_ref[sl, :] += lax.dot_general(p.astype(BF16), dob, (_TN, ((), ())), preferred_element_type=F32)
        dq_ref[...] = dq

    kern, s_in, s_out, s_shapes, s_scratch, s_args = _host(kern, 6, 3, side, (H, nq))
    res = pl.pallas_call(
        kern, name=name, grid=(H, nq),
        in_specs=[pl.BlockSpec((tq, LANE), lambda h, i: (i, h)),
                  pl.BlockSpec((L, LANE), lambda h, i: (0, h // rep)),
                  pl.BlockSpec((L, LANE), lambda h, i: (0, h // rep)),
                  pl.BlockSpec((tq, LANE), lambda h, i: (i, h)),
                  pl.BlockSpec((tq, 1), lambda h, i: (h * nq + i, 0)),
                  pl.BlockSpec((tq, LANE), lambda h, i: (i, h))] + s_in,
        out_specs=[pl.BlockSpec((tq, LANE), lambda h, i: (i, h)),
                   pl.BlockSpec((L, LANE), lambda h, i: (0, h // rep)),
                   pl.BlockSpec((L, LANE), lambda h, i: (0, h // rep))] + s_out,
        out_shape=[jax.ShapeDtypeStruct((L, H * LANE), F32), jax.ShapeDtypeStruct((L, Hkv * LANE), F32),
                   jax.ShapeDtypeStruct((L, Hkv * LANE), F32)] + s_shapes,
        scratch_shapes=s_scratch,
        compiler_params=_cparams(("arbitrary", "arbitrary")),
    )(q, k, v, o, lse, do, *s_args)
    return res[0], res[1], res[2], list(res[3:])


def _shift_dn(x, first_row):
    row = lax.broadcasted_iota(jnp.int32, x.shape, 0)
    return jnp.where(row == 0, first_row, pltpu.roll(x, 1, 0))


def _shift_up(x, last_row):
    n = x.shape[0]
    row = lax.broadcasted_iota(jnp.int32, x.shape, 0)
    return jnp.where(row == n - 1, last_row, pltpu.roll(x, n - 1, 0))


def _halo_specs(ndim, lead, T, tc, cb0, L):
    r8 = T // 8
    last8 = L // 8 - 1
    if ndim == 2:
        return [pl.BlockSpec((T, tc), lambda j, i: (i, cb0 + j)),
                pl.BlockSpec((8, tc), lambda j, i: (jnp.maximum(i * r8 - 1, 0), cb0 + j)),
                pl.BlockSpec((8, tc), lambda j, i: (jnp.minimum((i + 1) * r8, last8), cb0 + j))]
    return [pl.BlockSpec((None, T, tc), lambda j, i: (lead, i, cb0 + j)),
            pl.BlockSpec((None, 8, tc), lambda j, i: (lead, jnp.maximum(i * r8 - 1, 0), cb0 + j)),
            pl.BlockSpec((None, 8, tc), lambda j, i: (lead, jnp.minimum((i + 1) * r8, last8), cb0 + j))]


def _conv_fwd(name, x, col0, C, w8, act, L, tc):
    T = min(256, L)
    nt = L // T
    cb0 = col0 // tc

    def kern(x_ref, xp_ref, xn_ref, w_ref, o_ref):
        i = pl.program_id(1)
        x = x_ref[...]
        w = w_ref[...]
        pr = jnp.where(i == 0, 0.0, xp_ref[7:8, :])
        nr = jnp.where(i == nt - 1, 0.0, xn_ref[0:1, :])
        pre = _shift_dn(x, pr) * w[0:1] + x * w[1:2] + _shift_up(x, nr) * w[2:3] + w[3:4]
        o_ref[...] = _silu(pre) if act else pre

    return pl.pallas_call(
        kern, name=name, grid=(C // tc, nt),
        in_specs=_halo_specs(2, None, T, tc, cb0, L) + [pl.BlockSpec((8, tc), lambda j, i: (0, j))],
        out_specs=pl.BlockSpec((T, tc), lambda j, i: (i, j)),
        out_shape=jax.ShapeDtypeStruct((L, C), F32),
        compiler_params=_cparams(("arbitrary", "arbitrary")),
    )(x, x, x, w8)


def _conv_bwd(name, x, col0, C, w8, act, gs, L, tc):
    T = min(256, L)
    nt = L // T
    cb0 = col0 // tc
    ng = len(gs)

    def dact(pre, g):
        if not act:
            return g
        s = _sigmoid(pre)
        return g * (s * (1.0 + pre * (1.0 - s)))

    def kern(*refs):
        x_ref, xp_ref, xn_ref, w_ref = refs[:4]
        g_refs = refs[4:4 + 3 * ng]
        dx_ref, dw_ref = refs[4 + 3 * ng:]
        i = pl.program_id(1)
        first = i == 0
        last = i == nt - 1
        x = x_ref[...]
        w = w_ref[...]
        w0, w1, w2, b = w[0:1], w[1:2], w[2:3], w[3:4]
        g = g_refs[0][...]
        gp = g_refs[1][7:8, :]
        gn = g_refs[2][0:1, :]
        for n in range(1, ng):
            g = g + g_refs[3 * n][...]
            gp = gp + g_refs[3 * n + 1][7:8, :]
            gn = gn + g_refs[3 * n + 2][0:1, :]
        pr = jnp.where(first, 0.0, xp_ref[7:8, :])
        pr2 = jnp.where(first, 0.0, xp_ref[6:7, :])
        nr = jnp.where(last, 0.0, xn_ref[0:1, :])
        nr2 = jnp.where(last, 0.0, xn_ref[1:2, :])
        xm1 = _shift_dn(x, pr)
        xp1 = _shift_up(x, nr)
        pre = xm1 * w0 + x * w1 + xp1 * w2 + b
        dpre = dact(pre, g)
        pre_m1 = pr2 * w0 + pr * w1 + x[0:1] * w2 + b
        dpre_m1 = jnp.where(first, 0.0, dact(pre_m1, gp))
        pre_T = x[T - 1:T] * w0 + nr * w1 + nr2 * w2 + b
        dpre_T = jnp.where(last, 0.0, dact(pre_T, gn))
        dx_ref[...] = _shift_up(dpre, dpre_T) * w0 + dpre * w1 + _shift_dn(dpre, dpre_m1) * w2
        row = lax.broadcasted_iota(jnp.int32, (8, tc), 0)
        dw = (jnp.where(row == 0, jnp.sum(dpre * xm1, axis=0, keepdims=True), 0.0)
              + jnp.where(row == 1, jnp.sum(dpre * x, axis=0, keepdims=True), 0.0)
              + jnp.where(row == 2, jnp.sum(dpre * xp1, axis=0, keepdims=True), 0.0)
              + jnp.where(row == 3, jnp.sum(dpre, axis=0, keepdims=True), 0.0))

        @pl.when(first)
        def _():
            dw_ref[...] = jnp.zeros((8, tc), F32)

        dw_ref[...] += dw

    g_specs, g_args = [], []
    for arr, lead in gs:
        g_specs += _halo_specs(arr.ndim, lead, T, tc, 0, L)
        g_args += [arr, arr, arr]
    return pl.pallas_call(
        kern, name=name, grid=(C // tc, nt),
        in_specs=_halo_specs(2, None, T, tc, cb0, L) + [pl.BlockSpec((8, tc), lambda j, i: (0, j))] + g_specs,
        out_specs=[pl.BlockSpec((T, tc), lambda j, i: (i, j)), pl.BlockSpec((8, tc), lambda j, i: (0, j))],
        out_shape=[jax.ShapeDtypeStruct((L, C), F32), jax.ShapeDtypeStruct((8, C), F32)],
        compiler_params=_cparams(("arbitrary", "arbitrary")),
    )(x, x, x, w8, *g_args)


def _glu_fwd(name, gu, L):
    T = min(256, L)
    tc = 1408
    ncb = D_FF // tc

    def kern(g_ref, u_ref, o_ref):
        o_ref[...] = (_silu(g_ref[...]) * u_ref[...]).astype(BF16)

    return pl.pallas_call(
        kern, name=name, grid=(L // T, ncb),
        in_specs=[pl.BlockSpec((T, tc), lambda i, j: (i, j)), pl.BlockSpec((T, tc), lambda i, j: (i, j + ncb))],
        out_specs=pl.BlockSpec((T, tc), lambda i, j: (i, j)),
        out_shape=jax.ShapeDtypeStruct((L, D_FF), BF16),
        compiler_params=_cparams(("arbitrary", "arbitrary")),
    )(gu, gu)


def _glu_bwd(name, gu, da, L):
    T = min(256, L)
    tc = 1408
    ncb = D_FF // tc

    def kern(g_ref, u_ref, da_ref, o_ref):
        half = pl.program_id(1)
        g = g_ref[...]
        s = _sigmoid(g)
        d = da_ref[...]
        dg = d * u_ref[...] * (s * (1.0 + g * (1.0 - s)))
        du = d * (g * s)
        o_ref[...] = jnp.where(half == 0, dg, du)

    return pl.pallas_call(
        kern, name=name, grid=(L // T, 2, ncb),
        in_specs=[pl.BlockSpec((T, tc), lambda i, h, j: (i, j)),
                  pl.BlockSpec((T, tc), lambda i, h, j: (i, j + ncb)),
                  pl.BlockSpec((T, tc), lambda i, h, j: (i, j))],
        out_specs=pl.BlockSpec((T, tc), lambda i, h, j: (i, h * ncb + j)),
        out_shape=jax.ShapeDtypeStruct((L, 2 * D_FF), F32),
        compiler_params=_cparams(("arbitrary", "arbitrary", "arbitrary")),
    )(gu, gu, da)


def _masks(Q, rev):
    ri = lax.broadcasted_iota(jnp.int32, (Q, Q), 0)
    ci = lax.broadcasted_iota(jnp.int32, (Q, Q), 1)
    diff = (ri - ci) * (1 - 2 * rev)
    return diff >= 0, diff > 0


def _lane_pick(v, sel):
    return jnp.sum(v * sel, axis=-1, keepdims=True)


def _head_rows(v_all, Q, rev):
    r = lax.broadcasted_iota(jnp.int32, (HEADS * Q, LANE), 0)
    l = lax.broadcasted_iota(jnp.int32, (HEADS * Q, LANE), 1)
    pick = jnp.zeros((HEADS * Q, LANE), F32)
    for h in range(HEADS):
        pick = jnp.where((r >= h * Q) & (r < (h + 1) * Q) & (l == rev * 4 + h), 1.0, pick)
    return _nt_hi(pick, v_all)


def _ssd_chunk(S, x, B, C, dtraw, alog, dtb, rev):
    Q = dtraw.shape[0]
    incl, _ = _masks(Q, rev)
    tri = incl.astype(F32)
    dt = _softplus(dtraw + dtb)
    a_all = dt * (-jnp.exp(alog))
    acum_all = _nn_hi(tri, a_all)
    total_all = jnp.sum(a_all, axis=0, keepdims=True)
    lane = lax.broadcasted_iota(jnp.int32, (1, LANE), 1)
    rows_all = _head_rows(acum_all, Q, rev)
    ys, Sn = [], []
    for h in range(HEADS):
        g = h // 2
        sel = (lane == rev * 4 + h).astype(F32)
        acum = _lane_pick(acum_all, sel)
        dth = _lane_pick(dt, sel)
        tot = _lane_pick(total_all, sel)
        seg = acum - rows_all[h * Q:(h + 1) * Q, :]
        decay = jnp.exp(jnp.where(incl, seg, -1e30))
        xdt = x[h] * dth
        Sh = S[HD * h:HD * (h + 1), :]
        scores = _nt(C[g], B[g]) * decay
        y_diag = _nn(scores, xdt)
        states = _tn(xdt, B[g] * jnp.exp(tot - acum))
        y_off = _nt(C[g], Sh) * jnp.exp(acum)
        ys.append(y_diag + y_off)
        Sn.append(Sh * jnp.exp(tot) + states)
    return ys, jnp.concatenate(Sn, axis=0)


def _inv_unit_raw(Lm):
    N = Lm.shape[0]
    Q = D_CHUNK
    ri = lax.broadcasted_iota(jnp.int32, (N, N), 0)
    ci = lax.broadcasted_iota(jnp.int32, (N, N), 1)
    X = (ri == ci).astype(F32) - Lm
    P = _raw_dot(Lm, Lm, _NN, 'high')
    n = 2
    while n < Q:
        X = X + _raw_dot(X, P, _NN, 'high')
        n *= 2
        if n < Q:
            P = _raw_dot(P, P, _NN, 'high')
    return X


@jax.custom_vjp
def _inv_unit(Lm):
    return _inv_unit_raw(Lm)


def _inv_unit_f(Lm):
    T = _inv_unit_raw(Lm)
    return T, T


def _inv_unit_b(T, g):
    return (-_raw_dot(_raw_dot(T, g, _TN, 'high'), T, _NT, 'high'),)


_inv_unit.defvjp(_inv_unit_f, _inv_unit_b)


def _delta_chunk(S, q, k, v, braw, araw, alog, dtb, rev):
    Q = braw.shape[0]
    N = HEADS * Q
    tri = _masks(Q, rev)[0].astype(F32)
    ri = lax.broadcasted_iota(jnp.int32, (N, N), 0)
    ci = lax.broadcasted_iota(jnp.int32, (N, N), 1)
    sh = int(math.log2(Q))
    same = (ri >> sh) == (ci >> sh)
    diff = (ri - ci) * (1 - 2 * rev)
    incl = same & (diff >= 0)
    strict = same & (diff > 0)
    beta_all = _sigmoid(braw)
    g_all = -jnp.exp(alog) * _softplus(araw + dtb)
    G_all = _nn_hi(tri, g_all)
    Gtot_all = jnp.sum(g_all, axis=0, keepdims=True)
    r = lax.broadcasted_iota(jnp.int32, (N, LANE), 0)
    l = lax.broadcasted_iota(jnp.int32, (N, LANE), 1)
    selm = (l == rev * 4 + (r >> sh)).astype(F32)
    rows4 = lambda a: jnp.concatenate([a] * HEADS, axis=0)
    XG = rows4(G_all) * selm
    G = jnp.sum(XG, axis=-1, keepdims=True)
    bt = jnp.sum(rows4(beta_all) * selm, axis=-1, keepdims=True)
    Gtot = jnp.sum(Gtot_all * selm, axis=-1, keepdims=True)
    decay = jnp.exp(jnp.where(incl, G - _nt_hi(jnp.ones((N, LANE), F32), XG), -1e30))
    qs, ks, vs = (jnp.concatenate(t, axis=0) for t in (q, k, v))
    qn = qs * lax.rsqrt(jnp.sum(qs * qs, axis=-1, keepdims=True) + 1e-6)
    kn = ks * lax.rsqrt(jnp.sum(ks * ks, axis=-1, keepdims=True) + 1e-6)
    qc = qn * (HD ** -0.5)
    kb = kn * bt
    T = _inv_unit(jnp.where(strict, _nt(kb, kn) * decay, 0.0))
    eG = jnp.exp(G)
    u = _nn(T, vs * bt)
    w = _nn(T, kb * eG)
    qk = _nt(qc, kn) * decay
    spread = (lax.broadcasted_iota(jnp.int32, (HD, N), 0)
              == (lax.broadcasted_iota(jnp.int32, (HD, N), 1) & (HD - 1))).astype(F32)
    wide = lambda a: jnp.where(same, _nn(a, spread), 0.0)
    v_new = u - _nn(wide(w), S)
    o = _nn(wide(qc * eG), S) + _nn(qk, v_new)
    S_new = S * jnp.exp(Gtot) + _tn(wide(kn * jnp.exp(Gtot - G)), v_new)
    return [o[Q * h:Q * (h + 1), :] for h in range(HEADS)], S_new


def _seq_pieces(ref, r0, Q, splits):
    if splits is None:
        return ref[r0:r0 + Q, :]
    return [[ref[r0:r0 + Q, o + w * t:o + w * (t + 1)] for t in range(n)] for o, w, n in splits]


def _store_pieces(ref, r0, Q, splits, vals, extra=None):
    if splits is None:
        ref[r0:r0 + Q, :] = vals
        return
    for g, (o, w, n) in enumerate(splits):
        for t in range(n):
            v = vals[g][t]
            if extra is not None and g == 0:
                v = v + extra[r0:r0 + Q, o + w * t:o + w * (t + 1)]
            ref[r0:r0 + Q, o + w * t:o + w * (t + 1)] = v


def _flat(ins):
    out = []
    for v in ins:
        if isinstance(v, list):
            out.extend(v)
        else:
            out.append(v)
    return out


def _scan_fwd(name, chunk_fn, seqs, rows, Q, L, CH):
    nc = L // Q
    nb = nc // CH
    ns, nr = len(seqs), len(rows)
    BQ = Q * CH

    def kern(*refs):
        s_refs = (refs[:ns], refs[ns:2 * ns])
        r_refs = refs[2 * ns:2 * ns + nr]
        y_refs = refs[2 * ns + nr:2 * ns + nr + 2]
        ss_refs = refs[2 * ns + nr + 2:2 * ns + nr + 4]
        S_scr = refs[2 * ns + nr + 4]
        i = pl.program_id(0)

        @pl.when(i == 0)
        def _():
            S_scr[...] = jnp.zeros(S_scr.shape, F32)

        rws = [r[...] for r in r_refs]
        for d in (0, 1):
            S = S_scr[d]
            for cc in range(CH):
                c = cc if d == 0 else CH - 1 - cc
                ss_refs[d][c] = S
                ins = [_seq_pieces(r, c * Q, Q, sp) for r, (_, _, _, sp) in zip(s_refs[d], seqs)]
                ys, S = chunk_fn(S, *_flat(ins), *rws, d)
                for h in range(HEADS):
                    y_refs[d][c * Q:(c + 1) * Q, HD * h:HD * (h + 1)] = ys[h]
            S_scr[d] = S

    fwd_specs = [pl.BlockSpec((BQ, w), functools.partial(lambda i, cb: (i, cb), cb=cb)) for _, w, cb, _ in seqs]
    rev_specs = [pl.BlockSpec((BQ, w), functools.partial(lambda i, cb: (nb - 1 - i, cb), cb=cb))
                 for _, w, cb, _ in seqs]
    arrs = [a for a, _, _, _ in seqs]
    return pl.pallas_call(
        kern, name=name, grid=(nb,),
        in_specs=fwd_specs + rev_specs + [pl.BlockSpec((1, LANE), lambda i: (0, 0)) for _ in rows],
        out_specs=[pl.BlockSpec((BQ, GROUP_W), lambda i: (i, 0)),
                   pl.BlockSpec((BQ, GROUP_W), lambda i: (nb - 1 - i, 0)),
                   pl.BlockSpec((CH, GROUP_W, HD), lambda i: (i, 0, 0)),
                   pl.BlockSpec((CH, GROUP_W, HD), lambda i: (nb - 1 - i, 0, 0))],
        out_shape=[jax.ShapeDtypeStruct((L, GROUP_W), F32)] * 2 + [jax.ShapeDtypeStruct((nc, GROUP_W, HD), F32)] * 2,
        scratch_shapes=[pltpu.VMEM((2, GROUP_W, HD), F32)],
        compiler_params=_cparams(("arbitrary",)),
    )(*arrs, *arrs, *rows)


def _scan_bwd(name, chunk_fn, seqs, rows, ssaves, dy, extra, Q, L, CH):
    nc = L // Q
    nb = nc // CH
    BQ = Q * CH
    ns, nr = len(seqs), len(rows)
    has_extra = extra is not None

    def kern(*refs):
        s_refs = (refs[:ns], refs[ns:2 * ns])
        pos = 2 * ns
        r_refs = refs[pos:pos + nr]
        pos += nr
        ss_refs = refs[pos:pos + 2]
        dy_refs = refs[pos + 2:pos + 4]
        pos += 4
        ex_ref = refs[pos] if has_extra else None
        pos += 1 if has_extra else 0
        ds_refs = (refs[pos:pos + ns], refs[pos + ns:pos + 2 * ns])
        pos += 2 * ns
        dr_refs = refs[pos:pos + nr]
        dS_scr = refs[pos + nr]
        i = pl.program_id(0)

        @pl.when(i == 0)
        def _():
            dS_scr[...] = jnp.zeros(dS_scr.shape, F32)
            for r in dr_refs:
                r[...] = jnp.zeros(r.shape, F32)

        rws = [r[...] for r in r_refs]
        dr_acc = [jnp.zeros((1, LANE), F32) for _ in rows]
        for d in (0, 1):
            dS = dS_scr[d]
            for cc in range(CH):
                c = CH - 1 - cc if d == 0 else cc
                S = ss_refs[d][c]
                dys = [dy_refs[d][c * Q:(c + 1) * Q, HD * h:HD * (h + 1)] for h in range(HEADS)]
                ins = [_seq_pieces(r, c * Q, Q, sp) for r, (_, _, _, sp) in zip(s_refs[d], seqs)]
                _, vjp = jax.vjp(
                    functools.partial(lambda S_, ins_, rws_, d_: chunk_fn(S_, *_flat(ins_), *rws_, d_), d_=d),
                    S, ins, rws)
                dS, dins, drws = vjp((dys, dS))
                for n_, (r, (_, _, _, sp)) in enumerate(zip(ds_refs[d], seqs)):
                    _store_pieces(r, c * Q, Q, sp, dins[n_],
                                  extra=ex_ref if (has_extra and d == 0 and n_ == 0) else None)
                dr_acc = [a + g for a, g in zip(dr_acc, drws)]
            dS_scr[d] = dS
        for r, g in zip(dr_refs, dr_acc):
            r[...] += g

    def blk(shape, rev, cb=0):
        nd = len(shape)
        if rev:
            return pl.BlockSpec(shape, lambda i: (i, cb) + (0,) * (nd - 2))
        return pl.BlockSpec(shape, lambda i: (nb - 1 - i, cb) + (0,) * (nd - 2))

    arrs = [a for a, _, _, _ in seqs]
    in_specs = [blk((BQ, w), False, cb) for _, w, cb, _ in seqs] + [blk((BQ, w), True, cb) for _, w, cb, _ in seqs]
    in_specs += [pl.BlockSpec((1, LANE), lambda i: (0, 0)) for _ in rows]
    in_specs += [blk((CH, GROUP_W, HD), False), blk((CH, GROUP_W, HD), True),
                 blk((BQ, GROUP_W), False), blk((BQ, GROUP_W), True)]
    args = arrs + arrs + list(rows) + list(ssaves) + [dy, dy]
    if has_extra:
        in_specs.append(blk((BQ, GROUP_W), False))
        args.append(extra)
    res = pl.pallas_call(
        kern, name=name, grid=(nb,),
        in_specs=in_specs,
        out_specs=[blk((BQ, w), False) for _, w, _, _ in seqs] + [blk((BQ, w), True) for _, w, _, _ in seqs]
        + [pl.BlockSpec((1, LANE), lambda i: (0, 0)) for _ in rows],
        out_shape=[jax.ShapeDtypeStruct((L, w), F32) for _, w, _, _ in seqs] * 2
        + [jax.ShapeDtypeStruct((1, LANE), F32) for _ in rows],
        scratch_shapes=[pltpu.VMEM((2, GROUP_W, HD), F32)],
        compiler_params=_cparams(("arbitrary",)),
    )(*args)
    return list(res[:ns]), list(res[ns:2 * ns]), list(res[2 * ns:])


def _loss_call(y, tgt, L):
    T = min(256, L)

    def kern(y_ref, t_ref, dy_ref, l_ref):
        i = pl.program_id(0)
        e = y_ref[...] - t_ref[...]
        dy_ref[...] = e * (1.0 / D_MODEL)

        @pl.when(i == 0)
        def _():
            l_ref[...] = jnp.zeros(l_ref.shape, F32)

        part = 0.5 * jnp.sum(jnp.sum(e * e, axis=-1, keepdims=True) * (1.0 / D_MODEL), axis=0, keepdims=True)
        l_ref[...] += jnp.broadcast_to(part, l_ref.shape)

    return pl.pallas_call(
        kern, name="loss_head", grid=(L // T,),
        in_specs=[_spec2(T, D_MODEL), _spec2(T, D_MODEL)],
        out_specs=[_spec2(T, D_MODEL), pl.BlockSpec((8, LANE), lambda i: (0, 0))],
        out_shape=[jax.ShapeDtypeStruct((L, D_MODEL), F32), jax.ShapeDtypeStruct((8, LANE), F32)],
        compiler_params=_cparams(("arbitrary",)),
    )(y, tgt)


_ANY = pl.BlockSpec(memory_space=pl.ANY)


def _coords():
    return lax.axis_index("x"), lax.axis_index("y"), lax.axis_index("c")


class _Copies:
    def __init__(self, ins, out_shapes, copies_fn, n_remote, n_local):
        self.ins, self.out_shapes, self.copies_fn = list(ins), list(out_shapes), copies_fn
        self.n_remote, self.n_local = n_remote, n_local

    def scratch(self):
        return [pltpu.SemaphoreType.DMA((self.n_remote,)), pltpu.SemaphoreType.DMA((self.n_remote,)),
                pltpu.SemaphoreType.DMA((self.n_local,))]

    def _descr(self, in_refs, out_refs, sems):
        send_sems, recv_sems, lsems = sems
        remote, local = self.copies_fn(list(in_refs), list(out_refs))
        assert len(remote) == self.n_remote and len(local) == self.n_local
        mk = lambda k, src, dst, peer: pltpu.make_async_remote_copy(
            src_ref=src, dst_ref=dst, send_sem=send_sems.at[k], recv_sem=recv_sems.at[k], device_id=peer,
            device_id_type=MESH)
        sends = [mk(k, src, dst, peer) for k, (src, dst, _, peer) in enumerate(remote)]
        recvs = [mk(k, src, land, peer) for k, (src, _, land, peer) in enumerate(remote)]
        locs = [pltpu.make_async_copy(src, dst, lsems.at[k]) for k, (src, dst) in enumerate(local)]
        return sends, recvs, locs

    def start(self, in_refs, out_refs, sems):
        sends, _, locs = self._descr(in_refs, out_refs, sems)
        for c in locs + sends:
            c.start()

    def finish(self, in_refs, out_refs, sems):
        sends, recvs, locs = self._descr(in_refs, out_refs, sems)
        for c in recvs:
            c.wait_recv()
        for c in sends:
            c.wait_send()
        for c in locs:
            c.wait()

    def call(self, name):
        ni, no = len(self.ins), len(self.out_shapes)

        def body(*refs):
            self.start(refs[:ni], refs[ni:ni + no], refs[ni + no:])
            self.finish(refs[:ni], refs[ni:ni + no], refs[ni + no:])

        return pl.pallas_call(body, name=name, in_specs=[_ANY] * ni, out_specs=[_ANY] * no,
                              out_shape=self.out_shapes, scratch_shapes=self.scratch())(*self.ins)


def _chip_peers(x, y):
    return [(1 - x, y), (x, 1 - y), (1 - x, 1 - y)]


def _gather_copies(arrs):
    def copies_fn(ins, outs):
        x, y, c = _coords()
        me = 2 * x + y
        remote, local = [], []
        for src, out in zip(ins, outs):
            local.append((src, out.at[me]))
            for px, py in _chip_peers(x, y):
                remote.append((src, out.at[me], out.at[2 * px + py], (px, py, c)))
        return remote, local

    shapes = [jax.ShapeDtypeStruct((4,) + a.shape, a.dtype) for a in arrs]
    return _Copies(arrs, shapes, copies_fn, 3 * len(arrs), len(arrs))


def _scatter_copies(Gs, small):
    nb = len(Gs)

    def copies_fn(ins, outs):
        x, y, c = _coords()
        me = 2 * x + y
        remote, local = [], []
        for g, out in zip(ins[:nb], outs[:nb]):
            local.append((g.at[me], out.at[me]))
            for px, py in _chip_peers(x, y):
                remote.append((g.at[2 * px + py], out.at[me], out.at[2 * px + py], (px, py, c)))
        if small is not None:
            dev = 4 * x + 2 * y + c
            gs, outs_ = ins[nb], outs[nb]
            local.append((gs, outs_.at[dev]))
            for mask in range(1, 8):
                px, py, pc = x ^ (mask >> 2), y ^ ((mask >> 1) & 1), c ^ (mask & 1)
                remote.append((gs, outs_.at[dev], outs_.at[4 * px + 2 * py + pc], (px, py, pc)))
        return remote, local

    ins = list(Gs) + ([small] if small is not None else [])
    shapes = [jax.ShapeDtypeStruct(g.shape, g.dtype) for g in Gs]
    if small is not None:
        shapes.append(jax.ShapeDtypeStruct((8,) + small.shape, small.dtype))
    extra = 1 if small is not None else 0
    return _Copies(ins, shapes, copies_fn, 3 * nb + 7 * extra, nb + extra)


SWAP_STREAMS = 8


def _row_chunks(rows):
    k = SWAP_STREAMS
    if rows % (8 * k) == 0 and rows >= 64 * k:
        return [(q * (rows // k), rows // k) for q in range(k)]
    return [(0, rows)]


def _swap_copies(parts):
    flat = [p for per_layer in parts for p in per_layer]
    n = sum(len(_row_chunks(p.shape[0])) for p in flat)

    def copies_fn(ins, outs):
        x, y, c = _coords()
        remote, local = [], []
        k = 0
        for out, per_layer in zip(outs, parts):
            for l, p in enumerate(per_layer):
                src = ins[k]
                k += 1
                for r0, nr in _row_chunks(p.shape[0]):
                    rows = pl.ds(r0, nr)
                    local.append((src.at[rows], out.at[c, l, rows]))
                    remote.append((src.at[rows], out.at[c, l, rows], out.at[1 - c, l, rows], (x, y, 1 - c)))
        return remote, local

    shapes = [jax.ShapeDtypeStruct((2, len(pp)) + pp[0].shape, pp[0].dtype) for pp in parts]
    return _Copies(flat, shapes, copies_fn, n, n)


def _row_tile(rows):
    best = rows
    for d in range(8, min(rows, 256) + 1, 8):
        if rows % d == 0:
            best = d
    return best


def _sum_slots(name, recv):
    n, R, W = recv.shape
    tr = _row_tile(R)

    def kern(r_ref, o_ref):
        acc = r_ref[0]
        for s in range(1, n):
            acc = acc + r_ref[s]
        o_ref[...] = acc

    return pl.pallas_call(
        kern, name=name, grid=(R // tr,),
        in_specs=[pl.BlockSpec((n, tr, W), lambda i: (0, i, 0))],
        out_specs=pl.BlockSpec((tr, W), lambda i: (i, 0)),
        out_shape=jax.ShapeDtypeStruct((R, W), F32),
        compiler_params=_cparams(("arbitrary",)),
    )(recv)


def _adamw_call(name, slots, w, m, v):
    n, R, W = slots.shape
    tr = _row_tile(R)

    def kern(s_ref, w_ref, m_ref, v_ref, g_ref, d_ref, nm_ref, nv_ref):
        g = s_ref[0]
        for s in range(1, n):
            g = g + s_ref[s]
        m_ = ADAM_B1 * m_ref[...] + (1.0 - ADAM_B1) * g
        v_ = ADAM_B2 * v_ref[...] + (1.0 - ADAM_B2) * (g * g)
        m_hat = m_ / (1.0 - ADAM_B1 ** ADAM_STEP)
        v_hat = v_ / (1.0 - ADAM_B2 ** ADAM_STEP)
        g_ref[...] = g
        d_ref[...] = -ADAM_LR * (m_hat / (jnp.sqrt(v_hat) + ADAM_EPS) + ADAM_WD * w_ref[...])
        nm_ref[...] = m_
        nv_ref[...] = v_

    blk = pl.BlockSpec((tr, W), lambda i: (i, 0))
    return pl.pallas_call(
        kern, name=name, grid=(R // tr,),
        in_specs=[pl.BlockSpec((n, tr, W), lambda i: (0, i, 0)), blk, blk, blk],
        out_specs=[blk, blk, blk, blk],
        out_shape=[jax.ShapeDtypeStruct((R, W), F32)] * 4,
        compiler_params=_cparams(("arbitrary",)),
    )(slots, w, m, v)


def _pack(arrs, width, row_mult):
    flat = jnp.concatenate([a.reshape(-1) for a in arrs])
    n = flat.shape[0]
    rows = -(-n // width)
    rows = -(-rows // row_mult) * row_mult
    return jnp.pad(flat, (0, rows * width - n)).reshape(rows, width)


def _unpack(buf, shapes):
    flat = buf.reshape(-1)
    out, pos = [], 0
    for s in shapes:
        n = int(np.prod(s))
        out.append(flat[pos:pos + n].reshape(s))
        pos += n
    return out


def _rope_angles(L, rot_dim):
    rows = L // GRID_W
    row = jnp.repeat(jnp.arange(rows), GRID_W).astype(F32)
    col = jnp.tile(jnp.arange(GRID_W), rows).astype(F32)
    sec = rot_dim // 2
    inv_freq = ROPE_BASE ** (-jnp.arange(0, sec, 2, dtype=F32) / sec)
    ang_r = row[:, None] * inv_freq
    ang_c = col[:, None] * inv_freq
    ang = jnp.concatenate([ang_r, ang_r, ang_c, ang_c], axis=-1)
    return jnp.cos(ang), jnp.sin(ang)


def _rot_matrix(r):
    R = np.zeros((r, r), np.float32)
    q = r // 4
    for s in range(2):
        for t in range(q):
            lo = s * (r // 2) + t
            hi = lo + q
            R[hi, lo] = -1.0
            R[lo, hi] = 1.0
    return R


def _place_tables(L, cos, sin, width, offsets):
    r = cos.shape[1]
    Rm = np.zeros((width, width), np.float32)
    R = _rot_matrix(r)
    cs, ss, pos = [], [], 0
    for o in list(offsets) + [width]:
        if o > pos:
            cs.append(jnp.ones((L, o - pos), F32))
            ss.append(jnp.zeros((L, o - pos), F32))
        if o < width:
            cs.append(cos)
            ss.append(sin)
            Rm[o:o + r, o:o + r] = R
        pos = o + r
    return jnp.concatenate(cs, axis=1), jnp.concatenate(ss, axis=1), jnp.asarray(Rm)


def _head_mean_matrix(width, stride, n):
    M = np.zeros((width, width), np.float32)
    for o in range(0, width, stride):
        M[o:o + n, o:o + n] = 1.0 / n
    return jnp.asarray(M)


def _pad_heads(w, n_heads, real, padded, axis):
    parts = jnp.split(w, n_heads, axis=axis)
    padw = [(0, 0)] * w.ndim
    padw[axis] = (0, padded - real)
    return jnp.concatenate([jnp.pad(p, padw) for p in parts], axis=axis)


def _row128(v):
    v = v.reshape(1, -1)
    return jnp.pad(v, ((0, 0), (0, LANE - v.shape[1])))


def _conv_w8(w, b):
    C = w.shape[1]
    rows = [w, jnp.zeros((1, C), F32) if b is None else b.reshape(1, C), jnp.zeros((4, C), F32)]
    return jnp.concatenate(rows, axis=0)


def _build_layer(W):
    w_in = W['w_in']
    o = 0
    cols = {}
    for name, n in [('a_cq', A_Q_LORA), ('a_ckv', A_KV_LORA), ('a_kr', A_ROPE), ('b_q', 256), ('b_k', 128),
                    ('b_v', 128), ('c_z', 256), ('c_xbc', 512), ('c_dt', 8), ('d_qkv', 768), ('d_z', 256),
                    ('d_b', 8), ('d_a', 8)]:
        cols[name] = w_in[:, o:o + n]
        o += n
    padc = lambda a, lo, width: jnp.pad(a, ((0, 0), (lo, width - lo - a.shape[1])))
    pieces = {
        'b_q': _pad_heads(cols['b_q'], 4, HD, LANE, 1), 'c_xbc': cols['c_xbc'], 'a_cq': padc(cols['a_cq'], 0, 256),
        'b_k': _pad_heads(cols['b_k'], 2, HD, LANE, 1), 'd_qkv': cols['d_qkv'],
        'b_v': _pad_heads(cols['b_v'], 2, HD, LANE, 1), 'c_z': cols['c_z'], 'd_z': cols['d_z'],
        'a_ckv': cols['a_ckv'], 'a_kr': padc(cols['a_kr'], A_NOPE, LANE), 'c_dt': padc(cols['c_dt'], 0, LANE),
        'd_b': padc(cols['d_b'], 0, LANE), 'd_a': padc(cols['d_a'], 0, LANE),
        'pad': jnp.zeros((D_MODEL, LANE), w_in.dtype)}
    out = {'w_in': jnp.concatenate([pieces[n] for n, _, _ in P_LAYOUT], axis=1)}
    out['a_q_norm'] = padc(W['a_q_norm'].reshape(1, -1), 0, 256)
    wuq = jnp.pad(W['a_w_uq'], ((0, 256 - A_Q_LORA), (0, 0)))
    out['a_w_uq'] = _pad_heads(wuq, 4, A_NOPE + A_ROPE, LANE, 1)
    out['a_kv_norm'] = W['a_kv_norm'].reshape(1, -1)
    ukv = W['a_w_ukv'].reshape(A_KV_LORA, HEADS, 2, HD)
    out['a_w_uk'] = _pad_heads(ukv[:, :, 0, :].reshape(A_KV_LORA, 256), 4, HD, LANE, 1)
    out['a_w_uv'] = _pad_heads(ukv[:, :, 1, :].reshape(A_KV_LORA, 256), 4, HD, LANE, 1)
    out['a_out_norm'] = _pad_heads(W['a_out_norm'].reshape(1, -1), 4, HD, LANE, 1)
    out['b_q_norm'] = _pad_heads(jnp.tile(W['b_q_norm'].reshape(1, -1), (1, 4)), 4, HD, LANE, 1)
    out['b_k_norm'] = _pad_heads(jnp.tile(W['b_k_norm'].reshape(1, -1), (1, 2)), 2, HD, LANE, 1)
    out['b_out_norm'] = _pad_heads(W['b_out_norm'].reshape(1, -1), 4, HD, LANE, 1)
    out['c_conv'] = _conv_w8(W['c_conv_w'], W['c_conv_b'])
    out['c_a_log'] = _row128(W['c_a_log'])
    out['c_dt_bias'] = _row128(W['c_dt_bias'])
    out['c_d_skip'] = jnp.repeat(W['c_d_skip'], HD).reshape(1, -1)
    out['c_out_norm'] = W['c_out_norm'].reshape(1, -1)
    out['d_conv'] = _conv_w8(W['d_conv_w'], None)
    out['d_a_log'] = _row128(W['d_a_log'])
    out['d_dt_bias'] = _row128(W['d_dt_bias'])
    out['d_out_norm'] = jnp.tile(W['d_out_norm'].reshape(1, -1), (1, 4))
    wo = W['w_out']
    out['w_out'] = jnp.concatenate([_pad_heads(wo[0:256], 4, HD, LANE, 0), _pad_heads(wo[256:512], 4, HD, LANE, 0),
                                    wo[512:1024]], axis=0)
    for n in ['pre_mix_norm', 'post_mix_norm', 'pre_ffn_norm', 'post_ffn_norm']:
        out[n] = W[n].reshape(1, -1)
    out['f_w_in'] = W['f_w_in']
    out['f_conv'] = _conv_w8(W['f_conv_w'], W['f_conv_b'])
    out['f_w_out'] = W['f_w_out']
    return out


def _fn_norm_in(a, p):
    return [_rms(a[0], p[0])]


def _fn_resid_norm2(a, p):
    x1 = a[0] + _rms(a[1], p[0])
    return [x1, _rms(x1, p[1])]


def _fn_resid_norm(a, p):
    return [a[0] + _rms(a[1], p[0])]


def _fn_a_prep(a, p):
    cq, ckv, kr, cosq, sinq, cosk, sink = a
    q_norm, w_uq, kv_norm, w_uk, w_uv, rq, rk = p
    q = _nn(_rms(cq, q_norm, A_Q_LORA), w_uq)
    q = q * cosq + _nn_h3(q, rq) * sinq
    kvn = _rms(ckv, kv_norm)
    kr_r = kr * cosk + _nn_h3(kr, rk) * sink
    kk = _nn(kvn, w_uk) + jnp.concatenate([kr_r] * HEADS, axis=1)
    return [q, kk, _nn(kvn, w_uv)]


def _fn_b_prep(a, p):
    q, k, v, cosq, sinq, cosk, sink = a
    q_norm, k_norm, mq, mk, rq, rk = p
    qn = q * lax.rsqrt(_nn_h3(q * q, mq) + EPS) * q_norm
    kn = k * lax.rsqrt(_nn_h3(k * k, mk) + EPS) * k_norm
    return [qn * cosq + _nn_h3(qn, rq) * sinq, kn * cosk + _nn_h3(kn, rk) * sink, v]


def _fn_mixer_post(a, p):
    oa, ob, yc0, yc1, xs, zc, od0, od1, zd = a
    a_norm, b_norm, dskip, c_norm, d_norm, m64 = p
    oc = _rms((yc0 + yc1 + xs * dskip) * _silu(zc), c_norm)
    od = od0 + od1
    odn = od * lax.rsqrt(_nn_h3(od * od, m64) + EPS) * d_norm * _silu(zd)
    return [jnp.concatenate([_rms(oa, a_norm, GROUP_W), _rms(ob, b_norm, GROUP_W), oc, odn], axis=1)]


def _fn_assemble(a, p):
    (dbq, dxbc, dcq, dbk, dqkv, dbv, dzc, dzd, dckv, dkr, ddt0, ddt1, db0, db1, da0, da1) = a
    return [jnp.concatenate([dbq, dxbc, dcq, dbk, dqkv, dbv, dzc, dzd, dckv, dkr, ddt0 + ddt1, db0 + db1,
                             da0 + da1, jnp.zeros_like(dckv)], axis=1)]


def _pspec(T, name):
    off, w = P_OFF[name]
    return _spec2(T, w, off // w)


def _layer_fwd(l, x, h, K, tabs, L, T, next_norm, side=None):
    n = f"l{l}_"
    sv = {'x': x, 'h': h}
    p = _mm(n + "in_proj", h, K['w_in'].astype(BF16), 'nn', F32, 512, 768, 1024)
    sv['p'] = p
    a_acts = [(p, _pspec(T, 'a_cq')), (p, _pspec(T, 'a_ckv')), (p, _pspec(T, 'a_kr')),
              (tabs['a_cq'], _spec2(T, 512)), (tabs['a_sq'], _spec2(T, 512)),
              (tabs['a_ck'], _spec2(T, LANE)), (tabs['a_sk'], _spec2(T, LANE))]
    a_pars = [K['a_q_norm'], K['a_w_uq'], K['a_kv_norm'], K['a_w_uk'], K['a_w_uv'], tabs['a_rq'], tabs['a_rk']]
    qa, ka, va = _tw_fwd(n + "a_prep", _fn_a_prep, a_acts, a_pars, [(512, BF16)] * 3, L, T)
    oa, lse_a, sv['side'] = _flash_fwd(n + "a_attn", qa, ka, va, HEADS, 1, (A_NOPE + A_ROPE) ** -0.5, L, side)
    sv.update(a_acts=a_acts, a_pars=a_pars, qa=qa, ka=ka, va=va, oa=oa, lse_a=lse_a)
    b_acts = [(p, _pspec(T, 'b_q')), (p, _pspec(T, 'b_k')), (p, _pspec(T, 'b_v')),
              (tabs['b_cq'], _spec2(T, 512)), (tabs['b_sq'], _spec2(T, 512)),
              (tabs['b_ck'], _spec2(T, 256)), (tabs['b_sk'], _spec2(T, 256))]
    b_pars = [K['b_q_norm'], K['b_k_norm'], tabs['b_mq'], tabs['b_mk'], tabs['b_rq'], tabs['b_rk']]
    qb, kb, vb = _tw_fwd(n + "b_prep", _fn_b_prep, b_acts, b_pars, [(512, BF16), (256, BF16), (256, BF16)], L, T)
    ob, lse_b, _ = _flash_fwd(n + "b_attn", qb, kb, vb, HEADS, 2, HD ** -0.5, L)
    sv.update(b_acts=b_acts, b_pars=b_pars, qb=qb, kb=kb, vb=vb, ob=ob, lse_b=lse_b)
    xbc = _conv_fwd(n + "c_conv", p, P_OFF['c_xbc'][0], C_XBC, K['c_conv'], True, L, 512)
    c_seqs = [(xbc, C_XBC, 0, [(0, HD, 4), (256, HD, 2), (384, HD, 2)]),
              (p, LANE, P_OFF['c_dt'][0] // LANE, None)]
    c_rows = [K['c_a_log'], K['c_dt_bias']]
    yc0, yc1, sc0, sc1 = _scan_fwd(n + "c_ssd", _ssd_chunk, c_seqs, c_rows, C_CHUNK, L, C_PER_STEP)
    sv.update(xbc=xbc, c_seqs=c_seqs, c_rows=c_rows, sc=(sc0, sc1))
    qkv = _conv_fwd(n + "d_conv", p, P_OFF['d_qkv'][0], D_QKV, K['d_conv'], True, L, 768)
    d_seqs = [(qkv, D_QKV, 0, [(0, HD, 4), (256, HD, 4), (512, HD, 4)]),
              (p, LANE, P_OFF['d_b'][0] // LANE, None), (p, LANE, P_OFF['d_a'][0] // LANE, None)]
    d_rows = [K['d_a_log'], K['d_dt_bias']]
    od0, od1, sd0, sd1 = _scan_fwd(n + "d_delta", _delta_chunk, d_seqs, d_rows, D_CHUNK, L, D_PER_STEP)
    sv.update(qkv=qkv, d_seqs=d_seqs, d_rows=d_rows, sd=(sd0, sd1))
    m_acts = [(oa, _spec2(T, 512)), (ob, _spec2(T, 512)), (yc0, _spec2(T, 256)), (yc1, _spec2(T, 256)),
              (xbc, _spec2(T, 256, 0)), (p, _pspec(T, 'c_z')), (od0, _spec2(T, 256)), (od1, _spec2(T, 256)),
              (p, _pspec(T, 'd_z'))]
    m_pars = [K['a_out_norm'], K['b_out_norm'], K['c_d_skip'], K['c_out_norm'], K['d_out_norm'], tabs['m64']]
    (o,) = _tw_fwd(n + "mixer_post", _fn_mixer_post, m_acts, m_pars, [(O_COLS, BF16)], L, T)
    f1 = _mm(n + "out_proj", o, K['w_out'].astype(BF16), 'nn', F32, 512, 1024, 768)
    r1_pars = [K['post_mix_norm'], K['pre_ffn_norm']]
    x1, h2 = _tw_fwd(n + "resid_mix", _fn_resid_norm2, [(x, _spec2(T, D_MODEL)), (f1, _spec2(T, D_MODEL))], r1_pars,
                     [(D_MODEL, F32), (D_MODEL, BF16)], L, T)
    sv.update(m_acts=m_acts, m_pars=m_pars, o=o, f1=f1, r1_pars=r1_pars, x1=x1, h2=h2)
    u = _mm(n + "ffn_in", h2, K['f_w_in'].astype(BF16), 'nn', F32, 512, 512, 1024)
    gu = _conv_fwd(n + "ffn_conv", u, 0, 2 * D_FF, K['f_conv'], False, L, 1408)
    act = _glu_fwd(n + "ffn_glu", gu, L)
    f2 = _mm(n + "ffn_out", act, K['f_w_out'].astype(BF16), 'nn', F32, 512, 1024, 1408)
    sv.update(u=u, gu=gu, act=act, f2=f2)
    xf = [(x1, _spec2(T, D_MODEL)), (f2, _spec2(T, D_MODEL))]
    if next_norm is None:
        (x2,) = _tw_fwd(n + "resid_ffn", _fn_resid_norm, xf, [K['post_ffn_norm']], [(D_MODEL, F32)], L, T)
        hn = None
    else:
        x2, hn = _tw_fwd(n + "resid_ffn", _fn_resid_norm2, xf, [K['post_ffn_norm'], next_norm],
                         [(D_MODEL, F32), (D_MODEL, BF16)], L, T)
    return x2, hn, sv


def _layer_bwd(l, dx2, dhn, K, sv, tabs, L, T, next_norm, side_b=None, side_a=None):
    n = f"l{l}b_"
    dK = {}
    s2 = lambda w, cb=0: _spec2(T, w, cb)
    xf = [(sv['x1'], s2(D_MODEL)), (sv['f2'], s2(D_MODEL))]
    if next_norm is None:
        (dx1a, df2), (dK['post_ffn_norm'],) = _tw_bwd(n + "resid_ffn", _fn_resid_norm, xf, [K['post_ffn_norm']],
                                                      [(dx2, s2(D_MODEL))], L, T, [True, True], [True])
        dnext = None
    else:
        (dx1a, df2), (dK['post_ffn_norm'], dnext) = _tw_bwd(
            n + "resid_ffn", _fn_resid_norm2, xf, [K['post_ffn_norm'], next_norm],
            [(dx2, s2(D_MODEL)), (dhn, s2(D_MODEL))], L, T, [True, True], [True, True])
    dact = _mm(n + "ffn_out_dx", df2, K['f_w_out'].astype(BF16), 'nt', F32, 512, 1408, 1024)
    dK['f_w_out'] = _mm(n + "ffn_out_dw", sv['act'], df2, 'tn', F32, 1408, 1024, 512)
    dgu = _glu_bwd(n + "ffn_glu", sv['gu'], dact, L)
    du, dK['f_conv'] = _conv_bwd(n + "ffn_conv", sv['u'], 0, 2 * D_FF, K['f_conv'], False, [(dgu, None)], L, 1408)
    dh2 = _mm(n + "ffn_in_dx", du, K['f_w_in'].astype(BF16), 'nt', F32, 512, 1024, 512)
    dK['f_w_in'] = _mm(n + "ffn_in_dw", sv['h2'], du, 'tn', F32, 512, 512, 512)
    (dxa, df1), (dK['post_mix_norm'], dK['pre_ffn_norm']) = _tw_bwd(
        n + "resid_mix", _fn_resid_norm2, [(sv['x'], s2(D_MODEL)), (sv['f1'], s2(D_MODEL))], sv['r1_pars'],
        [(dx1a, s2(D_MODEL)), (dh2, s2(D_MODEL))], L, T, [True, True], [True, True])
    do = _mm(n + "out_proj_dx", df1, K['w_out'].astype(BF16), 'nt', F32, 512, 768, 1024)
    dK['w_out'] = _mm(n + "out_proj_dw", sv['o'], df1, 'tn', F32, 768, 1024, 512)
    (doa, dob, dyc0, _, dxs_skip, dzc, dod0, _, dzd), mp = _tw_bwd(
        n + "mixer_post", _fn_mixer_post, sv['m_acts'], sv['m_pars'], [(do, s2(O_COLS))], L, T,
        [True] * 9, [True] * 5 + [False])
    dK['a_out_norm'], dK['b_out_norm'], dK['c_d_skip'], dK['c_out_norm'], dK['d_out_norm'] = mp
    (dqkv0, db0, da0), (dqkv1, db1, da1), (dK['d_a_log'], dK['d_dt_bias']) = _scan_bwd(
        n + "d_delta", _delta_chunk, sv['d_seqs'], sv['d_rows'], sv['sd'], dod0, None, D_CHUNK, L, D_PER_STEP)
    dqkv, dK['d_conv'] = _conv_bwd(n + "d_conv", sv['p'], P_OFF['d_qkv'][0], D_QKV, K['d_conv'], True,
                                   [(dqkv0, None), (dqkv1, None)], L, 768)
    (dxbc0, ddt0), (dxbc1, ddt1), (dK['c_a_log'], dK['c_dt_bias']) = _scan_bwd(
        n + "c_ssd", _ssd_chunk, sv['c_seqs'], sv['c_rows'], sv['sc'], dyc0, dxs_skip, C_CHUNK, L, C_PER_STEP)
    dxbc, dK['c_conv'] = _conv_bwd(n + "c_conv", sv['p'], P_OFF['c_xbc'][0], C_XBC, K['c_conv'], True,
                                   [(dxbc0, None), (dxbc1, None)], L, 512)
    dqb, dkb, dvb, got_b = _flash_bwd(n + "b_attn", sv['qb'], sv['kb'], sv['vb'], sv['ob'], sv['lse_b'], dob, HEADS,
                                      2, HD ** -0.5, L, side_b(dK) if side_b else None)
    (dbq, dbk, dbv), (dK['b_q_norm'], dK['b_k_norm']) = _tw_bwd(
        n + "b_prep", _fn_b_prep, sv['b_acts'], sv['b_pars'], [(dqb, s2(512)), (dkb, s2(256)), (dvb, s2(256))],
        L, T, [True] * 3 + [False] * 4, [True, True] + [False] * 4)
    dqa, dka, dva, got_a = _flash_bwd(n + "a_attn", sv['qa'], sv['ka'], sv['va'], sv['oa'], sv['lse_a'], doa, HEADS,
                                      1, (A_NOPE + A_ROPE) ** -0.5, L, side_a(dK) if side_a else None)
    (dcq, dckv, dkr), ap = _tw_bwd(
        n + "a_prep", _fn_a_prep, sv['a_acts'], sv['a_pars'], [(dqa, s2(512)), (dka, s2(512)), (dva, s2(512))],
        L, T, [True] * 3 + [False] * 4, [True] * 5 + [False] * 2)
    dK['a_q_norm'], dK['a_w_uq'], dK['a_kv_norm'], dK['a_w_uk'], dK['a_w_uv'] = ap
    pieces = [(dbq, s2(512)), (dxbc, s2(512)), (dcq, s2(256)), (dbk, s2(256)), (dqkv, s2(768)), (dbv, s2(256)),
              (dzc, s2(256)), (dzd, s2(256)), (dckv, s2(LANE)), (dkr, s2(LANE)),
              (ddt0, s2(LANE)), (ddt1, s2(LANE)), (db0, s2(LANE)), (db1, s2(LANE)), (da0, s2(LANE)),
              (da1, s2(LANE))]
    (dp,) = _tw_fwd(n + "assemble_dp", _fn_assemble, pieces, [], [(P_COLS, BF16)], L, T)
    dh = _mm(n + "in_proj_dx", dp, K['w_in'].astype(BF16), 'nt', F32, 512, 1024, 768)
    dK['w_in'] = _mm(n + "in_proj_dw", sv['h'], dp, 'tn', F32, 512, 768, 512)
    return dxa, dh, dK, dnext, got_b, got_a


def _tables(L):
    ca, sa = _rope_angles(L, A_ROPE)
    cb, sb = _rope_angles(L, HD)
    t = {}
    t['a_cq'], t['a_sq'], t['a_rq'] = _place_tables(L, ca, sa, 512, [LANE * h + A_NOPE for h in range(4)])
    t['a_ck'], t['a_sk'], t['a_rk'] = _place_tables(L, ca, sa, LANE, [A_NOPE])
    t['b_cq'], t['b_sq'], t['b_rq'] = _place_tables(L, cb, sb, 512, [LANE * h for h in range(4)])
    t['b_ck'], t['b_sk'], t['b_rk'] = _place_tables(L, cb, sb, 256, [LANE * h for h in range(2)])
    t['b_mq'] = _head_mean_matrix(512, LANE, HD)
    t['b_mk'] = _head_mean_matrix(256, LANE, HD)
    t['m64'] = _head_mean_matrix(256, HD, HD)
    return t


def kernel(x, pre_mix_norm, w_in, a_q_norm, a_w_uq, a_kv_norm, a_w_ukv, a_out_norm, b_q_norm, b_k_norm, b_out_norm, c_conv_w, c_conv_b, c_a_log, c_dt_bias, c_d_skip, c_out_norm, d_conv_w, d_a_log, d_dt_bias, d_out_norm, w_out, post_mix_norm, pre_ffn_norm, f_w_in, f_conv_w, f_conv_b, f_w_out, post_ffn_norm, loss_target, m_pre_mix_norm, m_w_in, m_a_q_norm, m_a_w_uq, m_a_kv_norm, m_a_w_ukv, m_a_out_norm, m_b_q_norm, m_b_k_norm, m_b_out_norm, m_c_conv_w, m_c_conv_b, m_c_a_log, m_c_dt_bias, m_c_d_skip, m_c_out_norm, m_d_conv_w, m_d_a_log, m_d_dt_bias, m_d_out_norm, m_w_out, m_post_mix_norm, m_pre_ffn_norm, m_f_w_in, m_f_conv_w, m_f_conv_b, m_f_w_out, m_post_ffn_norm, v_pre_mix_norm, v_w_in, v_a_q_norm, v_a_w_uq, v_a_kv_norm, v_a_w_ukv, v_a_out_norm, v_b_q_norm, v_b_k_norm, v_b_out_norm, v_c_conv_w, v_c_conv_b, v_c_a_log, v_c_dt_bias, v_c_d_skip, v_c_out_norm, v_d_conv_w, v_d_a_log, v_d_dt_bias, v_d_out_norm, v_w_out, v_post_mix_norm, v_pre_ffn_norm, v_f_w_in, v_f_conv_w, v_f_conv_b, v_f_w_out, v_post_ffn_norm):
    loc = locals()
    Wl = {n: loc[n] for n in WEIGHTS}
    Ml = {n: loc['m_' + n] for n in WEIGHTS}
    Vl = {n: loc['v_' + n] for n in WEIGHTS}
    L = x.shape[1]
    T = min(256, L)
    x0 = x.reshape(L, D_MODEL)
    tgt = loss_target.reshape(L, D_MODEL)

    def shards(l):
        return [Wl[n][l].astype(BF16) if n in MXU_WEIGHTS else Wl[n][l] for n in SHARDED]

    def layer_weights(l, gathered):
        W = {n: Wl[n][l] for n in SMALL}
        for n, g in zip(SHARDED, gathered):
            W[n] = jnp.concatenate([g[j] for j in range(4)], axis=SHARD_AXIS[n] - 1)
        return W

    def chip_blocks(g, n):
        return jnp.stack(jnp.split(g, 4, axis=SHARD_AXIS[n] - 1))

    tabs = _tables(L)
    norm_in = [Wl['pre_mix_norm'][l].reshape(1, -1) for l in range(DEPTH)]
    def layer_shape(n):
        s = list(Wl[n].shape[1:])
        if n in SHARD_AXIS:
            s[SHARD_AXIS[n] - 1] *= 4
        return tuple(s)

    unbuild = jax.vjp(_build_layer, {n: jnp.zeros(layer_shape(n), F32) for n in WEIGHTS})[1]

    (h,) = _tw_fwd("l0_norm_in", _fn_norm_in, [(x0, _spec2(T, D_MODEL))], [norm_in[0]], [(D_MODEL, BF16)], L, T)
    gathered = _gather_copies(shards(0)).call("gather_l0")
    xs, saves, Ks = x0, [], []
    for l in range(DEPTH):
        Ks.append(_build_layer(layer_weights(l, gathered)))
        last = l + 1 == DEPTH
        xs, h, sv = _layer_fwd(l, xs, h, Ks[l], tabs, L, T, None if last else norm_in[l + 1],
                               None if last else _gather_copies(shards(l + 1)))
        gathered = sv['side']
        saves.append(sv)
    dy, loss_acc = _loss_call(xs, tgt, L)
    loss = lax.psum(loss_acc[0, 0], ("x", "y", "c"))

    ffn = ['f_w_in', 'f_conv_w', 'f_w_out']
    rest = [n for n in SHARDED if n not in ffn]

    def ffn_side(dK):
        g = {'f_w_in': dK['f_w_in'], 'f_conv_w': dK['f_conv'][0:3], 'f_w_out': dK['f_w_out']}
        return _scatter_copies([chip_blocks(g[n], n) for n in ffn], None)

    def rest_blocks(dK):
        full = dict(dK)
        full.setdefault('pre_mix_norm', jnp.zeros((1, D_MODEL), F32))
        (g,) = unbuild(full)
        return [chip_blocks(g[n], n) for n in rest]

    grads = [None] * DEPTH
    recv = {}
    dx, dhn = dy, None
    for l in reversed(range(DEPTH)):
        last = l + 1 == DEPTH
        side_b = None if last else (lambda dK, up=grads[l + 1]: _scatter_copies(rest_blocks(up), None))
        dxa, dh, dK, dnext, got_b, got_a = _layer_bwd(l, dx, dhn, Ks[l], saves[l], tabs, L, T,
                                                      None if last else norm_in[l + 1], side_b, ffn_side)
        recv.update({(l, n): r for n, r in zip(ffn, got_a)})
        if not last:
            recv.update({(l + 1, n): r for n, r in zip(rest, got_b)})
            grads[l + 1]['pre_mix_norm'] = dnext
        grads[l] = dK
        dx, dhn = dxa, dh
    (dx_in,), (grads[0]['pre_mix_norm'],) = _tw_bwd(
        "l0b_norm_in", _fn_norm_in, [(x0, _spec2(T, D_MODEL))], [norm_in[0]], [(dhn, _spec2(T, D_MODEL))], L, T,
        [True], [True], addto={0: (dx, _spec2(T, D_MODEL))})
    small_shapes = [Wl[n].shape for n in SMALL]
    gfull = [unbuild(grads[l])[0] for l in range(DEPTH)]
    gs = _pack([jnp.stack([gfull[l][n] for l in range(DEPTH)]) for n in SMALL], LANE, 8)
    *got, recv_small = _scatter_copies(rest_blocks(grads[0]), gs).call("scatter_last")
    recv.update({(0, n): r for n, r in zip(rest, got)})

    as2d = lambda a: a.reshape(-1, a.shape[-1])
    parts = [[_sum_slots(f"sum_{n}_{l}", recv[l, n].reshape(4, -1, recv[l, n].shape[-1])) for l in range(DEPTH)]
             for n in SHARDED]
    pairs = _swap_copies(parts).call("swap_cores")
    kinds = ['grad', 'delta', 'new_m', 'new_v']
    res = {}
    for n, pair in zip(SHARDED, pairs):
        upd = _adamw_call("adamw_" + n, pair.reshape(2, -1, pair.shape[-1]), as2d(Wl[n]), as2d(Ml[n]), as2d(Vl[n]))
        for kind, a in zip(kinds, upd):
            res[kind, n] = a.reshape(Wl[n].shape)
    small = _adamw_call("adamw_small", recv_small, _pack([Wl[n] for n in SMALL], LANE, 8),
                        _pack([Ml[n] for n in SMALL], LANE, 8), _pack([Vl[n] for n in SMALL], LANE, 8))
    for kind, s in zip(kinds, small):
        for n, a in zip(SMALL, _unpack(s, small_shapes)):
            res[kind, n] = a
    outs = [loss, dx_in.reshape(x.shape)]
    for kind in ['grad', 'delta', 'new_m', 'new_v']:
        outs += [res[kind, n] for n in WEIGHTS]
    return tuple(outs)
```

```python
import functools
import math

import numpy as np
import jax
import jax.numpy as jnp
from jax import lax
from jax.experimental import pallas as pl
from jax.experimental.pallas import tpu as pltpu

F32 = jnp.float32
BF16 = jnp.bfloat16
MESH = pl.DeviceIdType.MESH
VMEM_LIMIT = 48 * 1024 * 1024
LANE = 128

D_MODEL = 1024
DEPTH = 2
GRID_W = 64
ROPE_BASE = 10000.0
EPS = 1e-6
GROUP_W = 256
HEADS = 4
HD = 64
A_NOPE, A_ROPE, A_Q_LORA, A_KV_LORA = 64, 32, 192, 128
A_COLS = A_Q_LORA + A_KV_LORA + A_ROPE
B_COLS = 512
C_XBC = 512
C_COLS = GROUP_W + C_XBC + 8
D_QKV = 768
D_COLS = D_QKV + GROUP_W + 16
IN_COLS = A_COLS + B_COLS + C_COLS + D_COLS
C_CHUNK = 128
D_CHUNK = 64
C_PER_STEP = 1
D_PER_STEP = 2
D_FF = 2816
ADAM_LR, ADAM_B1, ADAM_B2, ADAM_EPS, ADAM_WD, ADAM_STEP = 0.001, 0.9, 0.999, 1e-08, 0.01, 10

WEIGHTS = ['pre_mix_norm', 'w_in', 'a_q_norm', 'a_w_uq', 'a_kv_norm', 'a_w_ukv', 'a_out_norm', 'b_q_norm',
           'b_k_norm', 'b_out_norm', 'c_conv_w', 'c_conv_b', 'c_a_log', 'c_dt_bias', 'c_d_skip', 'c_out_norm',
           'd_conv_w', 'd_a_log', 'd_dt_bias', 'd_out_norm', 'w_out', 'post_mix_norm', 'pre_ffn_norm', 'f_w_in',
           'f_conv_w', 'f_conv_b', 'f_w_out', 'post_ffn_norm']
SHARD_AXIS = {'w_in': 2, 'a_w_uq': 2, 'a_w_ukv': 2, 'c_conv_w': 2, 'd_conv_w': 2, 'w_out': 1, 'f_w_in': 2,
              'f_conv_w': 2, 'f_w_out': 1}
SHARDED = [n for n in WEIGHTS if n in SHARD_AXIS]
SMALL = [n for n in WEIGHTS if n not in SHARD_AXIS]
MXU_WEIGHTS = ('w_in', 'a_w_uq', 'a_w_ukv', 'w_out', 'f_w_in', 'f_w_out')

P_LAYOUT = [('b_q', 0, 512), ('c_xbc', 512, 512), ('a_cq', 1024, 256), ('b_k', 1280, 256), ('d_qkv', 1536, 768),
            ('b_v', 2304, 256), ('c_z', 2560, 256), ('d_z', 2816, 256), ('a_ckv', 3072, 128), ('a_kr', 3200, 128),
            ('c_dt', 3328, 128), ('d_b', 3456, 128), ('d_a', 3584, 128), ('pad', 3712, 128)]
P_OFF = {n: (o, w) for n, o, w in P_LAYOUT}
P_COLS = 3840
O_COLS = 1536


def _cparams(sem):
    return pltpu.CompilerParams(dimension_semantics=sem, vmem_limit_bytes=VMEM_LIMIT)


def _tile(n, target):
    best = None
    for d in range(LANE, min(n, target) + 1, LANE):
        if n % d == 0:
            best = d
    return best if best is not None else n


_NN = ((1,), (0,))
_NT = ((1,), (1,))
_TN = ((0,), (0,))


def _raw_dot(a, b, dims, hi):
    if hi:
        prec = lax.Precision.HIGH if hi == 'high' else lax.Precision.HIGHEST
        return lax.dot_general(a, b, (dims, ((), ())), precision=prec, preferred_element_type=F32)
    return lax.dot_general(a.astype(BF16), b.astype(BF16), (dims, ((), ())), preferred_element_type=F32)


def _make_dots(hi):
    @jax.custom_vjp
    def nn(a, b):
        return _raw_dot(a, b, _NN, hi)

    @jax.custom_vjp
    def nt(a, b):
        return _raw_dot(a, b, _NT, hi)

    @jax.custom_vjp
    def tn(a, b):
        return _raw_dot(a, b, _TN, hi)

    nn.defvjp(lambda a, b: (nn(a, b), (a, b)), lambda r, g: (nt(g, r[1]), tn(r[0], g)))
    nt.defvjp(lambda a, b: (nt(a, b), (a, b)), lambda r, g: (nn(g, r[1]), tn(g, r[0])))
    tn.defvjp(lambda a, b: (tn(a, b), (a, b)), lambda r, g: (nt(r[1], g), nn(r[0], g)))
    return nn, nt, tn


_nn, _nt, _tn = _make_dots(False)
_nn_hi, _nt_hi, _tn_hi = _make_dots(True)
_nn_h3, _nt_h3, _tn_h3 = _make_dots('high')


def _sigmoid(x):
    return 1.0 / (1.0 + jnp.exp(-x))


def _silu(x):
    return x * _sigmoid(x)


def _softplus(x):
    return jnp.maximum(x, 0.0) + jnp.log(1.0 + jnp.exp(-jnp.abs(x)))


def _rms(x, w, n=None):
    n = x.shape[-1] if n is None else n
    ms = jnp.sum(x * x, axis=-1, keepdims=True) * (1.0 / n)
    return x * lax.rsqrt(ms + EPS) * w


def _spec2(T, w, cb=0):
    return pl.BlockSpec((T, w), lambda i: (i, cb))


def _full_spec(a):
    nd = a.ndim
    return pl.BlockSpec(a.shape, lambda i: (0,) * nd)


def _tw_fwd(name, fn, acts, params, outs, L, T):
    na, npar = len(acts), len(params)

    def kern(*refs):
        a = [r[...].astype(F32) for r in refs[:na]]
        p = [r[...].astype(F32) for r in refs[na:na + npar]]
        res = fn(a, p)
        for r, o in zip(refs[na + npar:], res):
            r[...] = o.astype(r.dtype)

    return pl.pallas_call(
        kern, name=name, grid=(L // T,),
        in_specs=[s for _, s in acts] + [_full_spec(p) for p in params],
        out_specs=[_spec2(T, w) for w, _ in outs],
        out_shape=[jax.ShapeDtypeStruct((L, w), dt) for w, dt in outs],
        compiler_params=_cparams(("arbitrary",)),
    )(*[a for a, _ in acts], *params)


def _tw_bwd(name, fn, acts, params, douts, L, T, act_grad, par_grad, addto=None):
    na, npar, nd = len(acts), len(params), len(douts)
    addto = addto or {}
    add_keys = sorted(addto)
    ga = [k for k in range(na) if act_grad[k]]
    gp = [k for k in range(npar) if par_grad[k]]

    def kern(*refs):
        i = pl.program_id(0)
        a = [r[...].astype(F32) for r in refs[:na]]
        p = [r[...].astype(F32) for r in refs[na:na + npar]]
        g = [r[...].astype(F32) for r in refs[na + npar:na + npar + nd]]
        pos = na + npar + nd
        adds = [r[...].astype(F32) for r in refs[pos:pos + len(add_keys)]]
        pos += len(add_keys)
        da_refs = refs[pos:pos + len(ga)]
        dp_refs = refs[pos + len(ga):]

        def f(ad, pd):
            af, pf = list(a), list(p)
            for k, v in zip(ga, ad):
                af[k] = v
            for k, v in zip(gp, pd):
                pf[k] = v
            return fn(af, pf)

        _, vjp = jax.vjp(f, [a[k] for k in ga], [p[k] for k in gp])
        dad, dpd = vjp(list(g))
        for n, (r, d) in enumerate(zip(da_refs, dad)):
            if n in addto:
                d = d + adds[add_keys.index(n)]
            r[...] = d.astype(r.dtype)

        @pl.when(i == 0)
        def _():
            for r in dp_refs:
                r[...] = jnp.zeros(r.shape, F32)

        for r, d in zip(dp_refs, dpd):
            r[...] += d

    def width(spec):
        return spec.block_shape[-1]

    res = pl.pallas_call(
        kern, name=name, grid=(L // T,),
        in_specs=[s for _, s in acts] + [_full_spec(p) for p in params] + [s for _, s in douts]
        + [addto[k][1] for k in add_keys],
        out_specs=[_spec2(T, width(acts[k][1])) for k in ga] + [_full_spec(params[k]) for k in gp],
        out_shape=[jax.ShapeDtypeStruct((L, width(acts[k][1])), F32) for k in ga]
        + [jax.ShapeDtypeStruct(params[k].shape, F32) for k in gp],
        compiler_params=_cparams(("arbitrary",)),
    )(*[a for a, _ in acts], *params, *[a for a, _ in douts], *[addto[k][0] for k in add_keys])
    return list(res[:len(ga)]), list(res[len(ga):])


def _mm(name, a, b, mode, out_dtype, tm, tn, tk):
    halves_a = a.shape[-1] if (a.ndim == 3 and mode == 'nt') else None
    halves_b = b.shape[-1] if (b.ndim == 3 and mode == 'tn') else None
    if mode == 'nn':
        (M, K), N = a.shape, b.shape[1]
    elif mode == 'nt':
        M, K, N = a.shape[-2], (2 * halves_a if halves_a else a.shape[1]), b.shape[0]
    else:
        (K, M), N = a.shape, (2 * halves_b if halves_b else b.shape[1])
    tm = _tile(M, tm)
    tn = _tile(halves_b or N, tn)
    tk = _tile(halves_a or K, tk)
    nk = K // tk
    if mode == 'nn':
        a_spec = pl.BlockSpec((tm, tk), lambda i, j, k: (i, k))
        b_spec = pl.BlockSpec((tk, tn), lambda i, j, k: (k, j))
        dims = _NN
    elif mode == 'nt':
        a_spec = pl.BlockSpec((tm, tk), lambda i, j, k: (i, k))
        if halves_a:
            per = halves_a // tk
            a_spec = pl.BlockSpec((None, tm, tk), lambda i, j, k: (k // per, i, k % per))
        b_spec = pl.BlockSpec((tn, tk), lambda i, j, k: (j, k))
        dims = _NT
    else:
        a_spec = pl.BlockSpec((tk, tm), lambda i, j, k: (k, i))
        b_spec = pl.BlockSpec((tk, tn), lambda i, j, k: (k, j))
        if halves_b:
            per = halves_b // tn
            b_spec = pl.BlockSpec((None, tk, tn), lambda i, j, k: (j // per, k, j % per))
        dims = _TN

    def kern(a_ref, b_ref, o_ref, acc):
        k = pl.program_id(2)

        @pl.when(k == 0)
        def _():
            acc[...] = jnp.zeros(acc.shape, F32)

        acc[...] += lax.dot_general(a_ref[...].astype(BF16), b_ref[...].astype(BF16), (dims, ((), ())),
                                    preferred_element_type=F32)

        @pl.when(k == nk - 1)
        def _():
            o_ref[...] = acc[...].astype(o_ref.dtype)

    return pl.pallas_call(
        kern, name=name, grid=(M // tm, N // tn, nk),
        in_specs=[a_spec, b_spec],
        out_specs=pl.BlockSpec((tm, tn), lambda i, j, k: (i, j)),
        out_shape=jax.ShapeDtypeStruct((M, N), out_dtype),
        scratch_shapes=[pltpu.VMEM((tm, tn), F32)],
        compiler_params=_cparams(("arbitrary", "arbitrary", "arbitrary")),
    )(a, b)


def _host(kern, n_in, n_out, side, grid):
    if side is None:
        return kern, [], [], [], [], []
    ni, no = len(side.ins), len(side.out_shapes)

    def hosted(*refs):
        ins, s_in = refs[:n_in], refs[n_in:n_in + ni]
        outs = refs[n_in + ni:n_in + ni + n_out]
        s_out = refs[n_in + ni + n_out:n_in + ni + n_out + no]
        sems = refs[n_in + ni + n_out + no:]
        a, b = pl.program_id(0), pl.program_id(1)

        @pl.when((a == 0) & (b == 0))
        def _():
            side.start(s_in, s_out, sems)

        kern(*ins, *outs)

        @pl.when((a == grid[0] - 1) & (b == grid[1] - 1))
        def _():
            side.finish(s_in, s_out, sems)

    return hosted, [_ANY] * ni, [_ANY] * no, side.out_shapes, side.scratch(), side.ins


def _flash_fwd(name, q, k, v, H, rep, scale, L, side=None):
    tq = min(256, L)
    nq = L // tq
    KC = min(2048, L)
    nkc = L // KC

    def kern(q_ref, k_ref, v_ref, o_ref, lse_ref):
        qb = q_ref[...]
        m = jnp.full((tq, 1), -1e30, F32)
        l = jnp.zeros((tq, 1), F32)
        acc = jnp.zeros((tq, LANE), F32)
        for c in range(nkc):
            kb = k_ref[c * KC:(c + 1) * KC, :]
            vb = v_ref[c * KC:(c + 1) * KC, :]
            s = lax.dot_general(qb, kb, (_NT, ((), ())), preferred_element_type=F32) * scale
            mn = jnp.maximum(m, jnp.max(s, axis=-1, keepdims=True))
            al = jnp.exp(m - mn)
            p = jnp.exp(s - mn)
            l = al * l + jnp.sum(p, axis=-1, keepdims=True)
            acc = al * acc + lax.dot_general(p.astype(BF16), vb, (_NN, ((), ())), preferred_element_type=F32)
            m = mn
        o_ref[...] = acc / l
        lse_ref[...] = m + jnp.log(l)

    kern, s_in, s_out, s_shapes, s_scratch, s_args = _host(kern, 3, 2, side, (H, nq))
    res = pl.pallas_call(
        kern, name=name, grid=(H, nq),
        in_specs=[pl.BlockSpec((tq, LANE), lambda h, i: (i, h)),
                  pl.BlockSpec((L, LANE), lambda h, i: (0, h // rep)),
                  pl.BlockSpec((L, LANE), lambda h, i: (0, h // rep))] + s_in,
        out_specs=[pl.BlockSpec((tq, LANE), lambda h, i: (i, h)),
                   pl.BlockSpec((tq, 1), lambda h, i: (h * nq + i, 0))] + s_out,
        out_shape=[jax.ShapeDtypeStruct((L, H * LANE), F32), jax.ShapeDtypeStruct((H * L, 1), F32)] + s_shapes,
        scratch_shapes=s_scratch,
        compiler_params=_cparams(("arbitrary", "arbitrary")),
    )(q, k, v, *s_args)
    return res[0], res[1], list(res[2:])


def _flash_bwd(name, q, k, v, o, lse, do, H, rep, scale, L, side=None):
    tq = min(256, L)
    nq = L // tq
    KC = min(2048, L)
    nkc = L // KC
    Hkv = H // rep

    def kern(q_ref, k_ref, v_ref, o_ref, lse_ref, do_ref, dq_ref, dk_ref, dv_ref):
        h = pl.program_id(0)
        i = pl.program_id(1)

        @pl.when((i == 0) & (h % rep == 0))
        def _():
            dk_ref[...] = jnp.zeros(dk_ref.shape, F32)
            dv_ref[...] = jnp.zeros(dv_ref.shape, F32)

        qb = q_ref[...]
        do = do_ref[...]
        dob = do.astype(BF16)
        delta = jnp.sum(do * o_ref[...], axis=-1, keepdims=True)
        lse = lse_ref[...]
        dq = jnp.zeros((tq, LANE), F32)
        for c in range(nkc):
            sl = slice(c * KC, (c + 1) * KC)
            kb = k_ref[sl, :]
            vb = v_ref[sl, :]
            s = lax.dot_general(qb, kb, (_NT, ((), ())), preferred_element_type=F32) * scale
            p = jnp.exp(s - lse)
            dp = lax.dot_general(dob, vb, (_NT, ((), ())), preferred_element_type=F32)
            ds = (p * (dp - delta) * scale).astype(BF16)
            dq = dq + lax.dot_general(ds, kb, (_NN, ((), ())), preferred_element_type=F32)
            dk_ref[sl, :] += lax.dot_general(ds, qb, (_TN, ((), ())), preferred_element_type=F32)
            dv_ref[sl, :] += lax.dot_general(p.astype(BF16), dob, (_TN, ((), ())), preferred_element_type=F32)
        dq_ref[...] = dq

    kern, s_in, s_out, s_shapes, s_scratch, s_args = _host(kern, 6, 3, side, (H, nq))
    res = pl.pallas_call(
        kern, name=name, grid=(H, nq),
        in_specs=[pl.BlockSpec((tq, LANE), lambda h, i: (i, h)),
                  pl.BlockSpec((L, LANE), lambda h, i: (0, h // rep)),
                  pl.BlockSpec((L, LANE), lambda h, i: (0, h // rep)),
                  pl.BlockSpec((tq, LANE), lambda h, i: (i, h)),
                  pl.BlockSpec((tq, 1), lambda h, i: (h * nq + i, 0)),
                  pl.BlockSpec((tq, LANE), lambda h, i: (i, h))] + s_in,
        out_specs=[pl.BlockSpec((tq, LANE), lambda h, i: (i, h)),
                   pl.BlockSpec((L, LANE), lambda h, i: (0, h // rep)),
                   pl.BlockSpec((L, LANE), lambda h, i: (0, h // rep))] + s_out,
        out_shape=[jax.ShapeDtypeStruct((L, H * LANE), F32), jax.ShapeDtypeStruct((L, Hkv * LANE), F32),
                   jax.ShapeDtypeStruct((L, Hkv * LANE), F32)] + s_shapes,
        scratch_shapes=s_scratch,
        compiler_params=_cparams(("arbitrary", "arbitrary")),
    )(q, k, v, o, lse, do, *s_args)
    return res[0], res[1], res[2], list(res[3:])


def _shift_dn(x, first_row):
    row = lax.broadcasted_iota(jnp.int32, x.shape, 0)
    return jnp.where(row == 0, first_row, pltpu.roll(x, 1, 0))


def _shift_up(x, last_row):
    n = x.shape[0]
    row = lax.broadcasted_iota(jnp.int32, x.shape, 0)
    return jnp.where(row == n - 1, last_row, pltpu.roll(x, n - 1, 0))


def _halo_specs(ndim, lead, T, tc, cb0, L):
    r8 = T // 8
    last8 = L // 8 - 1
    if ndim == 2:
        return [pl.BlockSpec((T, tc), lambda j, i: (i, cb0 + j)),
                pl.BlockSpec((8, tc), lambda j, i: (jnp.maximum(i * r8 - 1, 0), cb0 + j)),
                pl.BlockSpec((8, tc), lambda j, i: (jnp.minimum((i + 1) * r8, last8), cb0 + j))]
    return [pl.BlockSpec((None, T, tc), lambda j, i: (lead, i, cb0 + j)),
            pl.BlockSpec((None, 8, tc), lambda j, i: (lead, jnp.maximum(i * r8 - 1, 0), cb0 + j)),
            pl.BlockSpec((None, 8, tc), lambda j, i: (lead, jnp.minimum((i + 1) * r8, last8), cb0 + j))]


def _conv_rows(x_ref, xp_ref, xn_ref, w, first, last):
    x = x_ref[...]
    T = x.shape[0]
    w0, w1, w2, b = w[0:1], w[1:2], w[2:3], w[3:4]
    pr = jnp.where(first, 0.0, xp_ref[7:8, :])
    pr2 = jnp.where(first, 0.0, xp_ref[6:7, :])
    nr = jnp.where(last, 0.0, xn_ref[0:1, :])
    nr2 = jnp.where(last, 0.0, xn_ref[1:2, :])
    xm1 = _shift_dn(x, pr)
    xp1 = _shift_up(x, nr)
    pre = xm1 * w0 + x * w1 + xp1 * w2 + b
    pre_m1 = pr2 * w0 + pr * w1 + x[0:1] * w2 + b
    pre_T = x[T - 1:T] * w0 + nr * w1 + nr2 * w2 + b
    return x, xm1, xp1, pre, pre_m1, pre_T


def _conv_grads(dpre, dpre_m1, dpre_T, x, xm1, xp1, w):
    dx = _shift_up(dpre, dpre_T) * w[0:1] + dpre * w[1:2] + _shift_dn(dpre, dpre_m1) * w[2:3]
    row = lax.broadcasted_iota(jnp.int32, (8, x.shape[1]), 0)
    dw = (jnp.where(row == 0, jnp.sum(dpre * xm1, axis=0, keepdims=True), 0.0)
          + jnp.where(row == 1, jnp.sum(dpre * x, axis=0, keepdims=True), 0.0)
          + jnp.where(row == 2, jnp.sum(dpre * xp1, axis=0, keepdims=True), 0.0)
          + jnp.where(row == 3, jnp.sum(dpre, axis=0, keepdims=True), 0.0))
    return dx, dw


def _conv_fwd(name, x, col0, C, w8, act, L, tc):
    T = min(256, L)
    nt = L // T
    cb0 = col0 // tc

    def kern(x_ref, xp_ref, xn_ref, w_ref, o_ref):
        i = pl.program_id(1)
        x = x_ref[...]
        w = w_ref[...]
        pr = jnp.where(i == 0, 0.0, xp_ref[7:8, :])
        nr = jnp.where(i == nt - 1, 0.0, xn_ref[0:1, :])
        pre = _shift_dn(x, pr) * w[0:1] + x * w[1:2] + _shift_up(x, nr) * w[2:3] + w[3:4]
        o_ref[...] = _silu(pre) if act else pre

    return pl.pallas_call(
        kern, name=name, grid=(C // tc, nt),
        in_specs=_halo_specs(2, None, T, tc, cb0, L) + [pl.BlockSpec((8, tc), lambda j, i: (0, j))],
        out_specs=pl.BlockSpec((T, tc), lambda j, i: (i, j)),
        out_shape=jax.ShapeDtypeStruct((L, C), F32),
        compiler_params=_cparams(("arbitrary", "arbitrary")),
    )(x, x, x, w8)


def _conv_bwd(name, x, col0, C, w8, act, gs, L, tc):
    T = min(256, L)
    nt = L // T
    cb0 = col0 // tc
    ng = len(gs)

    def dact(pre, g):
        if not act:
            return g
        s = _sigmoid(pre)
        return g * (s * (1.0 + pre * (1.0 - s)))

    def kern(*refs):
        x_ref, xp_ref, xn_ref, w_ref = refs[:4]
        g_refs = refs[4:4 + 3 * ng]
        dx_ref, dw_ref = refs[4 + 3 * ng:]
        i = pl.program_id(1)
        first = i == 0
        last = i == nt - 1
        w = w_ref[...]
        g = g_refs[0][...]
        gp = g_refs[1][7:8, :]
        gn = g_refs[2][0:1, :]
        for n in range(1, ng):
            g = g + g_refs[3 * n][...]
            gp = gp + g_refs[3 * n + 1][7:8, :]
            gn = gn + g_refs[3 * n + 2][0:1, :]
        x, xm1, xp1, pre, pre_m1, pre_T = _conv_rows(x_ref, xp_ref, xn_ref, w, first, last)
        dpre_m1 = jnp.where(first, 0.0, dact(pre_m1, gp))
        dpre_T = jnp.where(last, 0.0, dact(pre_T, gn))
        dx_ref[...], dw = _conv_grads(dact(pre, g), dpre_m1, dpre_T, x, xm1, xp1, w)

        @pl.when(first)
        def _():
            dw_ref[...] = jnp.zeros((8, tc), F32)

        dw_ref[...] += dw

    g_specs, g_args = [], []
    for arr, lead in gs:
        g_specs += _halo_specs(arr.ndim, lead, T, tc, 0, L)
        g_args += [arr, arr, arr]
    return pl.pallas_call(
        kern, name=name, grid=(C // tc, nt),
        in_specs=_halo_specs(2, None, T, tc, cb0, L) + [pl.BlockSpec((8, tc), lambda j, i: (0, j))] + g_specs,
        out_specs=[pl.BlockSpec((T, tc), lambda j, i: (i, j)), pl.BlockSpec((8, tc), lambda j, i: (0, j))],
        out_shape=[jax.ShapeDtypeStruct((L, C), F32), jax.ShapeDtypeStruct((8, C), F32)],
        compiler_params=_cparams(("arbitrary", "arbitrary")),
    )(x, x, x, w8, *g_args)


FFN_TC = 1408


def _ffn_gate_fwd(name, u, w8, L):
    T = min(256, L)
    nt = L // T
    ncb = D_FF // FFN_TC

    def kern(xg, xgp, xgn, xu, xup, xun, wg_ref, wu_ref, o_ref):
        i = pl.program_id(1)
        pre_g = _conv_rows(xg, xgp, xgn, wg_ref[...], i == 0, i == nt - 1)[3]
        pre_u = _conv_rows(xu, xup, xun, wu_ref[...], i == 0, i == nt - 1)[3]
        o_ref[...] = (_silu(pre_g) * pre_u).astype(BF16)

    return pl.pallas_call(
        kern, name=name, grid=(ncb, nt),
        in_specs=_halo_specs(2, None, T, FFN_TC, 0, L) + _halo_specs(2, None, T, FFN_TC, ncb, L)
        + [pl.BlockSpec((8, FFN_TC), lambda j, i: (0, j)), pl.BlockSpec((8, FFN_TC), lambda j, i: (0, j + ncb))],
        out_specs=pl.BlockSpec((T, FFN_TC), lambda j, i: (i, j)),
        out_shape=jax.ShapeDtypeStruct((L, D_FF), BF16),
        compiler_params=_cparams(("arbitrary", "arbitrary")),
    )(u, u, u, u, u, u, w8, w8)


def _ffn_gate_bwd(name, u, w8, da, L):
    T = min(128, L)
    nt = L // T
    ncb = D_FF // FFN_TC

    def kern(xg, xgp, xgn, xu, xup, xun, wg_ref, wu_ref, d_ref, dp_ref, dn_ref, du_ref, dwg_ref, dwu_ref):
        i = pl.program_id(1)
        first = i == 0
        last = i == nt - 1
        wg = wg_ref[...]
        wu = wu_ref[...]
        g, gm1, gp1, pg, pg_m1, pg_T = _conv_rows(xg, xgp, xgn, wg, first, last)
        v, vm1, vp1, pu, pu_m1, pu_T = _conv_rows(xu, xup, xun, wu, first, last)

        def dpre(pg_, pu_, d):
            s = _sigmoid(pg_)
            return d * pu_ * (s * (1.0 + pg_ * (1.0 - s))), d * (pg_ * s)

        dg, dv = dpre(pg, pu, d_ref[...])
        dg_m1, dv_m1 = dpre(pg_m1, pu_m1, jnp.where(first, 0.0, dp_ref[7:8, :]))
        dg_T, dv_T = dpre(pg_T, pu_T, jnp.where(last, 0.0, dn_ref[0:1, :]))
        du_ref[0], dwg = _conv_grads(dg, dg_m1, dg_T, g, gm1, gp1, wg)
        du_ref[1], dwu = _conv_grads(dv, dv_m1, dv_T, v, vm1, vp1, wu)

        @pl.when(first)
        def _():
            dwg_ref[...] = jnp.zeros(dwg_ref.shape, F32)
            dwu_ref[...] = jnp.zeros(dwu_ref.shape, F32)

        dwg_ref[...] += dwg
        dwu_ref[...] += dwu

    wspec = pl.BlockSpec((8, FFN_TC), lambda j, i: (0, j))
    du, dwg, dwu = pl.pallas_call(
        kern, name=name, grid=(ncb, nt),
        in_specs=_halo_specs(2, None, T, FFN_TC, 0, L) + _halo_specs(2, None, T, FFN_TC, ncb, L)
        + [wspec, pl.BlockSpec((8, FFN_TC), lambda j, i: (0, j + ncb))] + _halo_specs(2, None, T, FFN_TC, 0, L),
        out_specs=[pl.BlockSpec((2, T, FFN_TC), lambda j, i: (0, i, j)), wspec, wspec],
        out_shape=[jax.ShapeDtypeStruct((2, L, D_FF), F32), jax.ShapeDtypeStruct((8, D_FF), F32),
                   jax.ShapeDtypeStruct((8, D_FF), F32)],
        compiler_params=_cparams(("arbitrary", "arbitrary")),
    )(u, u, u, u, u, u, w8, w8, da, da, da)
    return du, jnp.concatenate([dwg, dwu], axis=1)


def _masks(Q, rev):
    ri = lax.broadcasted_iota(jnp.int32, (Q, Q), 0)
    ci = lax.broadcasted_iota(jnp.int32, (Q, Q), 1)
    diff = (ri - ci) * (1 - 2 * rev)
    return diff >= 0, diff > 0


def _lane_pick(v, sel):
    return jnp.sum(v * sel, axis=-1, keepdims=True)


def _head_rows(v_all, Q, rev):
    r = lax.broadcasted_iota(jnp.int32, (HEADS * Q, LANE), 0)
    l = lax.broadcasted_iota(jnp.int32, (HEADS * Q, LANE), 1)
    pick = jnp.zeros((HEADS * Q, LANE), F32)
    for h in range(HEADS):
        pick = jnp.where((r >= h * Q) & (r < (h + 1) * Q) & (l == rev * 4 + h), 1.0, pick)
    return _nt_hi(pick, v_all)


def _ssd_chunk(S, x, B, C, dtraw, alog, dtb, rev, kept=None):
    Q = dtraw.shape[0]
    incl, _ = _masks(Q, rev)
    tri = incl.astype(F32)
    dt = _softplus(dtraw + dtb)
    a_all = dt * (-jnp.exp(alog))
    acum_all = _nn_hi(tri, a_all)
    total_all = jnp.sum(a_all, axis=0, keepdims=True)
    lane = lax.broadcasted_iota(jnp.int32, (1, LANE), 1)
    rows_all = _head_rows(acum_all, Q, rev)
    ys, Sn = [], []
    for h in range(HEADS):
        g = h // 2
        sel = (lane == rev * 4 + h).astype(F32)
        acum = _lane_pick(acum_all, sel)
        dth = _lane_pick(dt, sel)
        tot = _lane_pick(total_all, sel)
        seg = acum - rows_all[h * Q:(h + 1) * Q, :]
        decay = jnp.exp(jnp.where(incl, seg, -1e30))
        xdt = x[h] * dth
        Sh = S[HD * h:HD * (h + 1), :]
        scores = _nt(C[g], B[g]) * decay
        y_diag = _nn(scores, xdt)
        states = _tn(xdt, B[g] * jnp.exp(tot - acum))
        y_off = _nt(C[g], Sh) * jnp.exp(acum)
        ys.append(y_diag + y_off)
        Sn.append(Sh * jnp.exp(tot) + states)
    return ys, jnp.concatenate(Sn, axis=0), []


def _inv_unit_raw(Lm):
    N = Lm.shape[0]
    Q = D_CHUNK
    ri = lax.broadcasted_iota(jnp.int32, (N, N), 0)
    ci = lax.broadcasted_iota(jnp.int32, (N, N), 1)
    X = (ri == ci).astype(F32) - Lm
    P = _raw_dot(Lm, Lm, _NN, False)
    n = 2
    while n < Q:
        X = X + _raw_dot(X, P, _NN, False)
        n *= 2
        if n < Q:
            P = _raw_dot(P, P, _NN, False)
    return X


@jax.custom_vjp
def _inv_unit(Lm, T_saved):
    return _inv_unit_raw(Lm) if T_saved is None else T_saved


def _inv_unit_f(Lm, T_saved):
    T = _inv_unit_raw(Lm) if T_saved is None else T_saved
    return T, T


def _inv_unit_b(T, g):
    return -_raw_dot(_raw_dot(T, g, _TN, False), T, _NT, False), None


_inv_unit.defvjp(_inv_unit_f, _inv_unit_b)


def _delta_chunk(S, q, k, v, braw, araw, alog, dtb, rev, kept=None):
    Q = braw.shape[0]
    N = HEADS * Q
    tri = _masks(Q, rev)[0].astype(F32)
    ri = lax.broadcasted_iota(jnp.int32, (N, N), 0)
    ci = lax.broadcasted_iota(jnp.int32, (N, N), 1)
    sh = int(math.log2(Q))
    same = (ri >> sh) == (ci >> sh)
    diff = (ri - ci) * (1 - 2 * rev)
    incl = same & (diff >= 0)
    strict = same & (diff > 0)
    beta_all = _sigmoid(braw)
    g_all = -jnp.exp(alog) * _softplus(araw + dtb)
    G_all = _nn_hi(tri, g_all)
    Gtot_all = jnp.sum(g_all, axis=0, keepdims=True)
    r = lax.broadcasted_iota(jnp.int32, (N, LANE), 0)
    l = lax.broadcasted_iota(jnp.int32, (N, LANE), 1)
    selm = (l == rev * 4 + (r >> sh)).astype(F32)
    rows4 = lambda a: jnp.concatenate([a] * HEADS, axis=0)
    XG = rows4(G_all) * selm
    G = jnp.sum(XG, axis=-1, keepdims=True)
    bt = jnp.sum(rows4(beta_all) * selm, axis=-1, keepdims=True)
    Gtot = jnp.sum(Gtot_all * selm, axis=-1, keepdims=True)
    decay = jnp.exp(jnp.where(incl, G - _nt_h3(jnp.ones((N, LANE), F32), XG), -1e30))
    qs, ks, vs = (jnp.concatenate(t, axis=0) for t in (q, k, v))
    qn = qs * lax.rsqrt(jnp.sum(qs * qs, axis=-1, keepdims=True) + 1e-6)
    kn = ks * lax.rsqrt(jnp.sum(ks * ks, axis=-1, keepdims=True) + 1e-6)
    qc = qn * (HD ** -0.5)
    kb = kn * bt
    T = _inv_unit(jnp.where(strict, _nt(kb, kn) * decay, 0.0), None if kept is None else kept[0])
    eG = jnp.exp(G)
    u = _nn(T, vs * bt)
    w = _nn(T, kb * eG)
    qk = _nt(qc, kn) * decay
    spread = (lax.broadcasted_iota(jnp.int32, (HD, N), 0)
              == (lax.broadcasted_iota(jnp.int32, (HD, N), 1) & (HD - 1))).astype(F32)
    wide = lambda a: jnp.where(same, _nn(a, spread), 0.0)
    v_new = u - _nn(wide(w), S)
    o = _nn(wide(qc * eG), S) + _nn(qk, v_new)
    S_new = S * jnp.exp(Gtot) + _tn(wide(kn * jnp.exp(Gtot - G)), v_new)
    return [o[Q * h:Q * (h + 1), :] for h in range(HEADS)], S_new, [T]


def _seq_pieces(ref, r0, Q, splits):
    if splits is None:
        return ref[r0:r0 + Q, :]
    return [[ref[r0:r0 + Q, o + w * t:o + w * (t + 1)] for t in range(n)] for o, w, n in splits]


def _store_pieces(ref, r0, Q, splits, vals, extra=None):
    if splits is None:
        ref[r0:r0 + Q, :] = vals
        return
    for g, (o, w, n) in enumerate(splits):
        for t in range(n):
            v = vals[g][t]
            if extra is not None and g == 0:
                v = v + extra[r0:r0 + Q, o + w * t:o + w * (t + 1)]
            ref[r0:r0 + Q, o + w * t:o + w * (t + 1)] = v


def _flat(ins):
    out = []
    for v in ins:
        if isinstance(v, list):
            out.extend(v)
        else:
            out.append(v)
    return out


def _scan_fwd(name, chunk_fn, seqs, rows, Q, L, CH, kept_shapes=()):
    nc = L // Q
    nb = nc // CH
    ns, nr = len(seqs), len(rows)
    nk = 1 + len(kept_shapes)
    BQ = Q * CH

    def kern(*refs):
        s_refs = (refs[:ns], refs[ns:2 * ns])
        r_refs = refs[2 * ns:2 * ns + nr]
        pos = 2 * ns + nr
        y_refs = refs[pos:pos + 2]
        k_refs = (refs[pos + 2:pos + 2 + nk], refs[pos + 2 + nk:pos + 2 + 2 * nk])
        S_scr = refs[pos + 2 + 2 * nk]
        i = pl.program_id(0)

        @pl.when(i == 0)
        def _():
            S_scr[...] = jnp.zeros(S_scr.shape, F32)

        rws = [r[...] for r in r_refs]
        for d in (0, 1):
            S = S_scr[d]
            for cc in range(CH):
                c = cc if d == 0 else CH - 1 - cc
                k_refs[d][0][c] = S
                ins = [_seq_pieces(r, c * Q, Q, sp) for r, (_, _, _, sp) in zip(s_refs[d], seqs)]
                ys, S, kept = chunk_fn(S, *_flat(ins), *rws, d)
                for r, v in zip(k_refs[d][1:], kept):
                    r[c] = v
                for h in range(HEADS):
                    y_refs[d][c * Q:(c + 1) * Q, HD * h:HD * (h + 1)] = ys[h]
            S_scr[d] = S

    fwd_specs = [pl.BlockSpec((BQ, w), functools.partial(lambda i, cb: (i, cb), cb=cb)) for _, w, cb, _ in seqs]
    rev_specs = [pl.BlockSpec((BQ, w), functools.partial(lambda i, cb: (nb - 1 - i, cb), cb=cb))
                 for _, w, cb, _ in seqs]
    arrs = [a for a, _, _, _ in seqs]
    k_shapes = [(GROUP_W, HD)] + list(kept_shapes)
    res = pl.pallas_call(
        kern, name=name, grid=(nb,),
        in_specs=fwd_specs + rev_specs + [pl.BlockSpec((1, LANE), lambda i: (0, 0)) for _ in rows],
        out_specs=[pl.BlockSpec((BQ, GROUP_W), lambda i: (i, 0)),
                   pl.BlockSpec((BQ, GROUP_W), lambda i: (nb - 1 - i, 0))]
        + [pl.BlockSpec((CH,) + s, lambda i: (i, 0, 0)) for s in k_shapes]
        + [pl.BlockSpec((CH,) + s, lambda i: (nb - 1 - i, 0, 0)) for s in k_shapes],
        out_shape=[jax.ShapeDtypeStruct((L, GROUP_W), F32)] * 2
        + [jax.ShapeDtypeStruct((nc,) + s, F32) for s in k_shapes] * 2,
        scratch_shapes=[pltpu.VMEM((2, GROUP_W, HD), F32)],
        compiler_params=_cparams(("arbitrary",)),
    )(*arrs, *arrs, *rows)
    return res[0], res[1], list(res[2:2 + nk]), list(res[2 + nk:])


def _scan_bwd(name, chunk_fn, seqs, rows, ssaves, dy, extra, Q, L, CH):
    nc = L // Q
    nb = nc // CH
    BQ = Q * CH
    ns, nr = len(seqs), len(rows)
    nk = len(ssaves[0])
    has_extra = extra is not None

    def kern(*refs):
        s_refs = (refs[:ns], refs[ns:2 * ns])
        pos = 2 * ns
        r_refs = refs[pos:pos + nr]
        pos += nr
        k_refs = (refs[pos:pos + nk], refs[pos + nk:pos + 2 * nk])
        pos += 2 * nk
        dy_refs = refs[pos:pos + 2]
        pos += 2
        ex_ref = refs[pos] if has_extra else None
        pos += 1 if has_extra else 0
        ds_refs = (refs[pos:pos + ns], refs[pos + ns:pos + 2 * ns])
        pos += 2 * ns
        dr_refs = refs[pos:pos + nr]
        dS_scr = refs[pos + nr]
        i = pl.program_id(0)

        @pl.when(i == 0)
        def _():
            dS_scr[...] = jnp.zeros(dS_scr.shape, F32)
            for r in dr_refs:
                r[...] = jnp.zeros(r.shape, F32)

        rws = [r[...] for r in r_refs]
        dr_acc = [jnp.zeros((1, LANE), F32) for _ in rows]
        for d in (0, 1):
            dS = dS_scr[d]
            for cc in range(CH):
                c = CH - 1 - cc if d == 0 else cc
                S = k_refs[d][0][c]
                kept = [r[c] for r in k_refs[d][1:]]
                dys = [dy_refs[d][c * Q:(c + 1) * Q, HD * h:HD * (h + 1)] for h in range(HEADS)]
                ins = [_seq_pieces(r, c * Q, Q, sp) for r, (_, _, _, sp) in zip(s_refs[d], seqs)]
                _, vjp = jax.vjp(
                    functools.partial(
                        lambda S_, ins_, rws_, d_, kept_: chunk_fn(S_, *_flat(ins_), *rws_, d_, kept_)[:2],
                        d_=d, kept_=kept),
                    S, ins, rws)
                dS, dins, drws = vjp((dys, dS))
                for n_, (r, (_, _, _, sp)) in enumerate(zip(ds_refs[d], seqs)):
                    _store_pieces(r, c * Q, Q, sp, dins[n_],
                                  extra=ex_ref if (has_extra and d == 0 and n_ == 0) else None)
                dr_acc = [a + g for a, g in zip(dr_acc, drws)]
            dS_scr[d] = dS
        for r, g in zip(dr_refs, dr_acc):
            r[...] += g

    def blk(shape, rev, cb=0):
        nd = len(shape)
        if rev:
            return pl.BlockSpec(shape, lambda i: (i, cb) + (0,) * (nd - 2))
        return pl.BlockSpec(shape, lambda i: (nb - 1 - i, cb) + (0,) * (nd - 2))

    arrs = [a for a, _, _, _ in seqs]
    in_specs = [blk((BQ, w), False, cb) for _, w, cb, _ in seqs] + [blk((BQ, w), True, cb) for _, w, cb, _ in seqs]
    in_specs += [pl.BlockSpec((1, LANE), lambda i: (0, 0)) for _ in rows]
    in_specs += [blk((CH,) + a.shape[1:], False) for a in ssaves[0]]
    in_specs += [blk((CH,) + a.shape[1:], True) for a in ssaves[1]]
    in_specs += [blk((BQ, GROUP_W), False), blk((BQ, GROUP_W), True)]
    args = arrs + arrs + list(rows) + list(ssaves[0]) + list(ssaves[1]) + [dy, dy]
    if has_extra:
        in_specs.append(blk((BQ, GROUP_W), False))
        args.append(extra)
    res = pl.pallas_call(
        kern, name=name, grid=(nb,),
        in_specs=in_specs,
        out_specs=[blk((BQ, w), False) for _, w, _, _ in seqs] + [blk((BQ, w), True) for _, w, _, _ in seqs]
        + [pl.BlockSpec((1, LANE), lambda i: (0, 0)) for _ in rows],
        out_shape=[jax.ShapeDtypeStruct((L, w), F32) for _, w, _, _ in seqs] * 2
        + [jax.ShapeDtypeStruct((1, LANE), F32) for _ in rows],
        scratch_shapes=[pltpu.VMEM((2, GROUP_W, HD), F32)],
        compiler_params=_cparams(("arbitrary",)),
    )(*args)
    return list(res[:ns]), list(res[ns:2 * ns]), list(res[2 * ns:])


def _loss_call(y, tgt, L):
    T = min(256, L)

    def kern(y_ref, t_ref, dy_ref, l_ref):
        i = pl.program_id(0)
        e = y_ref[...] - t_ref[...]
        dy_ref[...] = e * (1.0 / D_MODEL)

        @pl.when(i == 0)
        def _():
            l_ref[...] = jnp.zeros(l_ref.shape, F32)

        part = 0.5 * jnp.sum(jnp.sum(e * e, axis=-1, keepdims=True) * (1.0 / D_MODEL), axis=0, keepdims=True)
        l_ref[...] += jnp.broadcast_to(part, l_ref.shape)

    return pl.pallas_call(
        kern, name="loss_head", grid=(L // T,),
        in_specs=[_spec2(T, D_MODEL), _spec2(T, D_MODEL)],
        out_specs=[_spec2(T, D_MODEL), pl.BlockSpec((8, LANE), lambda i: (0, 0))],
        out_shape=[jax.ShapeDtypeStruct((L, D_MODEL), F32), jax.ShapeDtypeStruct((8, LANE), F32)],
        compiler_params=_cparams(("arbitrary",)),
    )(y, tgt)


_ANY = pl.BlockSpec(memory_space=pl.ANY)


def _coords():
    return lax.axis_index("x"), lax.axis_index("y"), lax.axis_index("c")


class _Copies:
    def __init__(self, ins, out_shapes, copies_fn, n_remote, n_local):
        self.ins, self.out_shapes, self.copies_fn = list(ins), list(out_shapes), copies_fn
        self.n_remote, self.n_local = n_remote, n_local

    def scratch(self):
        return [pltpu.SemaphoreType.DMA((self.n_remote,)), pltpu.SemaphoreType.DMA((self.n_remote,)),
                pltpu.SemaphoreType.DMA((self.n_local,))]

    def _descr(self, in_refs, out_refs, sems):
        send_sems, recv_sems, lsems = sems
        remote, local = self.copies_fn(list(in_refs), list(out_refs))
        assert len(remote) == self.n_remote and len(local) == self.n_local
        mk = lambda k, src, dst, peer: pltpu.make_async_remote_copy(
            src_ref=src, dst_ref=dst, send_sem=send_sems.at[k], recv_sem=recv_sems.at[k], device_id=peer,
            device_id_type=MESH)
        sends = [mk(k, src, dst, peer) for k, (src, dst, _, peer) in enumerate(remote)]
        recvs = [mk(k, src, land, peer) for k, (src, _, land, peer) in enumerate(remote)]
        locs = [pltpu.make_async_copy(src, dst, lsems.at[k]) for k, (src, dst) in enumerate(local)]
        return sends, recvs, locs

    def start(self, in_refs, out_refs, sems):
        sends, _, locs = self._descr(in_refs, out_refs, sems)
        for c in locs + sends:
            c.start()

    def finish(self, in_refs, out_refs, sems):
        sends, recvs, locs = self._descr(in_refs, out_refs, sems)
        for c in recvs:
            c.wait_recv()
        for c in sends:
            c.wait_send()
        for c in locs:
            c.wait()

    def call(self, name):
        ni, no = len(self.ins), len(self.out_shapes)

        def body(*refs):
            self.start(refs[:ni], refs[ni:ni + no], refs[ni + no:])
            self.finish(refs[:ni], refs[ni:ni + no], refs[ni + no:])

        return pl.pallas_call(body, name=name, in_specs=[_ANY] * ni, out_specs=[_ANY] * no,
                              out_shape=self.out_shapes, scratch_shapes=self.scratch())(*self.ins)


def _chip_peers(x, y):
    return [(1 - x, y), (x, 1 - y), (1 - x, 1 - y)]


def _gather_copies(arrs):
    def copies_fn(ins, outs):
        x, y, c = _coords()
        me = 2 * x + y
        remote, local = [], []
        for src, out in zip(ins, outs):
            local.append((src, out.at[me]))
            for px, py in _chip_peers(x, y):
                remote.append((src, out.at[me], out.at[2 * px + py], (px, py, c)))
        return remote, local

    shapes = [jax.ShapeDtypeStruct((4,) + a.shape, a.dtype) for a in arrs]
    return _Copies(arrs, shapes, copies_fn, 3 * len(arrs), len(arrs))


def _scatter_copies(Gs, small):
    nb = len(Gs)

    def copies_fn(ins, outs):
        x, y, c = _coords()
        me = 2 * x + y
        remote, local = [], []
        for g, out in zip(ins[:nb], outs[:nb]):
            local.append((g.at[me], out.at[me]))
            for px, py in _chip_peers(x, y):
                remote.append((g.at[2 * px + py], out.at[me], out.at[2 * px + py], (px, py, c)))
        if small is not None:
            dev = 4 * x + 2 * y + c
            gs, outs_ = ins[nb], outs[nb]
            local.append((gs, outs_.at[dev]))
            for mask in range(1, 8):
                px, py, pc = x ^ (mask >> 2), y ^ ((mask >> 1) & 1), c ^ (mask & 1)
                remote.append((gs, outs_.at[dev], outs_.at[4 * px + 2 * py + pc], (px, py, pc)))
        return remote, local

    ins = list(Gs) + ([small] if small is not None else [])
    shapes = [jax.ShapeDtypeStruct(g.shape, g.dtype) for g in Gs]
    if small is not None:
        shapes.append(jax.ShapeDtypeStruct((8,) + small.shape, small.dtype))
    extra = 1 if small is not None else 0
    return _Copies(ins, shapes, copies_fn, 3 * nb + 7 * extra, nb + extra)


SWAP_STREAMS = 8


def _row_chunks(rows):
    k = SWAP_STREAMS
    if rows % (8 * k) == 0 and rows >= 64 * k:
        return [(q * (rows // k), rows // k) for q in range(k)]
    return [(0, rows)]


def _swap_copies(parts):
    flat = [p for per_layer in parts for p in per_layer]
    n = sum(len(_row_chunks(p.shape[0])) for p in flat)

    def copies_fn(ins, outs):
        x, y, c = _coords()
        remote, local = [], []
        k = 0
        for out, per_layer in zip(outs, parts):
            for l, p in enumerate(per_layer):
                src = ins[k]
                k += 1
                for r0, nr in _row_chunks(p.shape[0]):
                    rows = pl.ds(r0, nr)
                    local.append((src.at[rows], out.at[c, l, rows]))
                    remote.append((src.at[rows], out.at[c, l, rows], out.at[1 - c, l, rows], (x, y, 1 - c)))
        return remote, local

    shapes = [jax.ShapeDtypeStruct((2, len(pp)) + pp[0].shape, pp[0].dtype) for pp in parts]
    return _Copies(flat, shapes, copies_fn, n, n)


def _row_tile(rows):
    best = rows
    for d in range(8, min(rows, 256) + 1, 8):
        if rows % d == 0:
            best = d
    return best


def _sum_slots(name, recv):
    n, R, W = recv.shape
    tr = _row_tile(R)

    def kern(r_ref, o_ref):
        acc = r_ref[0]
        for s in range(1, n):
            acc = acc + r_ref[s]
        o_ref[...] = acc

    return pl.pallas_call(
        kern, name=name, grid=(R // tr,),
        in_specs=[pl.BlockSpec((n, tr, W), lambda i: (0, i, 0))],
        out_specs=pl.BlockSpec((tr, W), lambda i: (i, 0)),
        out_shape=jax.ShapeDtypeStruct((R, W), F32),
        compiler_params=_cparams(("arbitrary",)),
    )(recv)


def _adamw_call(name, slots, w, m, v):
    n, R, W = slots.shape
    tr = _row_tile(R)

    def kern(s_ref, w_ref, m_ref, v_ref, g_ref, d_ref, nm_ref, nv_ref):
        g = s_ref[0]
        for s in range(1, n):
            g = g + s_ref[s]
        m_ = ADAM_B1 * m_ref[...] + (1.0 - ADAM_B1) * g
        v_ = ADAM_B2 * v_ref[...] + (1.0 - ADAM_B2) * (g * g)
        m_hat = m_ / (1.0 - ADAM_B1 ** ADAM_STEP)
        v_hat = v_ / (1.0 - ADAM_B2 ** ADAM_STEP)
        g_ref[...] = g
        d_ref[...] = -ADAM_LR * (m_hat / (jnp.sqrt(v_hat) + ADAM_EPS) + ADAM_WD * w_ref[...])
        nm_ref[...] = m_
        nv_ref[...] = v_

    blk = pl.BlockSpec((tr, W), lambda i: (i, 0))
    return pl.pallas_call(
        kern, name=name, grid=(R // tr,),
        in_specs=[pl.BlockSpec((n, tr, W), lambda i: (0, i, 0)), blk, blk, blk],
        out_specs=[blk, blk, blk, blk],
        out_shape=[jax.ShapeDtypeStruct((R, W), F32)] * 4,
        compiler_params=_cparams(("arbitrary",)),
    )(slots, w, m, v)


def _pack(arrs, width, row_mult):
    flat = jnp.concatenate([a.reshape(-1) for a in arrs])
    n = flat.shape[0]
    rows = -(-n // width)
    rows = -(-rows // row_mult) * row_mult
    return jnp.pad(flat, (0, rows * width - n)).reshape(rows, width)


def _unpack(buf, shapes):
    flat = buf.reshape(-1)
    out, pos = [], 0
    for s in shapes:
        n = int(np.prod(s))
        out.append(flat[pos:pos + n].reshape(s))
        pos += n
    return out


def _rope_angles(L, rot_dim):
    rows = L // GRID_W
    row = jnp.repeat(jnp.arange(rows), GRID_W).astype(F32)
    col = jnp.tile(jnp.arange(GRID_W), rows).astype(F32)
    sec = rot_dim // 2
    inv_freq = ROPE_BASE ** (-jnp.arange(0, sec, 2, dtype=F32) / sec)
    ang_r = row[:, None] * inv_freq
    ang_c = col[:, None] * inv_freq
    ang = jnp.concatenate([ang_r, ang_r, ang_c, ang_c], axis=-1)
    return jnp.cos(ang), jnp.sin(ang)


def _rot_matrix(r):
    R = np.zeros((r, r), np.float32)
    q = r // 4
    for s in range(2):
        for t in range(q):
            lo = s * (r // 2) + t
            hi = lo + q
            R[hi, lo] = -1.0
            R[lo, hi] = 1.0
    return R


def _place_tables(L, cos, sin, width, offsets):
    r = cos.shape[1]
    Rm = np.zeros((width, width), np.float32)
    R = _rot_matrix(r)
    cs, ss, pos = [], [], 0
    for o in list(offsets) + [width]:
        if o > pos:
            cs.append(jnp.ones((L, o - pos), F32))
            ss.append(jnp.zeros((L, o - pos), F32))
        if o < width:
            cs.append(cos)
            ss.append(sin)
            Rm[o:o + r, o:o + r] = R
        pos = o + r
    return jnp.concatenate(cs, axis=1), jnp.concatenate(ss, axis=1), jnp.asarray(Rm)


def _head_mean_matrix(width, stride, n):
    M = np.zeros((width, width), np.float32)
    for o in range(0, width, stride):
        M[o:o + n, o:o + n] = 1.0 / n
    return jnp.asarray(M)


def _pad_heads(w, n_heads, real, padded, axis):
    parts = jnp.split(w, n_heads, axis=axis)
    padw = [(0, 0)] * w.ndim
    padw[axis] = (0, padded - real)
    return jnp.concatenate([jnp.pad(p, padw) for p in parts], axis=axis)


def _row128(v):
    v = v.reshape(1, -1)
    return jnp.pad(v, ((0, 0), (0, LANE - v.shape[1])))


def _conv_w8(w, b):
    C = w.shape[1]
    rows = [w, jnp.zeros((1, C), F32) if b is None else b.reshape(1, C), jnp.zeros((4, C), F32)]
    return jnp.concatenate(rows, axis=0)


def _build_layer(W):
    w_in = W['w_in']
    o = 0
    cols = {}
    for name, n in [('a_cq', A_Q_LORA), ('a_ckv', A_KV_LORA), ('a_kr', A_ROPE), ('b_q', 256), ('b_k', 128),
                    ('b_v', 128), ('c_z', 256), ('c_xbc', 512), ('c_dt', 8), ('d_qkv', 768), ('d_z', 256),
                    ('d_b', 8), ('d_a', 8)]:
        cols[name] = w_in[:, o:o + n]
        o += n
    padc = lambda a, lo, width: jnp.pad(a, ((0, 0), (lo, width - lo - a.shape[1])))
    pieces = {
        'b_q': _pad_heads(cols['b_q'], 4, HD, LANE, 1), 'c_xbc': cols['c_xbc'], 'a_cq': padc(cols['a_cq'], 0, 256),
        'b_k': _pad_heads(cols['b_k'], 2, HD, LANE, 1), 'd_qkv': cols['d_qkv'],
        'b_v': _pad_heads(cols['b_v'], 2, HD, LANE, 1), 'c_z': cols['c_z'], 'd_z': cols['d_z'],
        'a_ckv': cols['a_ckv'], 'a_kr': padc(cols['a_kr'], A_NOPE, LANE), 'c_dt': padc(cols['c_dt'], 0, LANE),
        'd_b': padc(cols['d_b'], 0, LANE), 'd_a': padc(cols['d_a'], 0, LANE),
        'pad': jnp.zeros((D_MODEL, LANE), w_in.dtype)}
    out = {'w_in': jnp.concatenate([pieces[n] for n, _, _ in P_LAYOUT], axis=1)}
    out['a_q_norm'] = padc(W['a_q_norm'].reshape(1, -1), 0, 256)
    wuq = jnp.pad(W['a_w_uq'], ((0, 256 - A_Q_LORA), (0, 0)))
    out['a_w_uq'] = _pad_heads(wuq, 4, A_NOPE + A_ROPE, LANE, 1)
    out['a_kv_norm'] = W['a_kv_norm'].reshape(1, -1)
    ukv = W['a_w_ukv'].reshape(A_KV_LORA, HEADS, 2, HD)
    out['a_w_uk'] = _pad_heads(ukv[:, :, 0, :].reshape(A_KV_LORA, 256), 4, HD, LANE, 1)
    out['a_w_uv'] = _pad_heads(ukv[:, :, 1, :].reshape(A_KV_LORA, 256), 4, HD, LANE, 1)
    out['a_out_norm'] = _pad_heads(W['a_out_norm'].reshape(1, -1), 4, HD, LANE, 1)
    out['b_q_norm'] = _pad_heads(jnp.tile(W['b_q_norm'].reshape(1, -1), (1, 4)), 4, HD, LANE, 1)
    out['b_k_norm'] = _pad_heads(jnp.tile(W['b_k_norm'].reshape(1, -1), (1, 2)), 2, HD, LANE, 1)
    out['b_out_norm'] = _pad_heads(W['b_out_norm'].reshape(1, -1), 4, HD, LANE, 1)
    out['c_conv'] = _conv_w8(W['c_conv_w'], W['c_conv_b'])
    out['c_a_log'] = _row128(W['c_a_log'])
    out['c_dt_bias'] = _row128(W['c_dt_bias'])
    out['c_d_skip'] = jnp.repeat(W['c_d_skip'], HD).reshape(1, -1)
    out['c_out_norm'] = W['c_out_norm'].reshape(1, -1)
    out['d_conv'] = _conv_w8(W['d_conv_w'], None)
    out['d_a_log'] = _row128(W['d_a_log'])
    out['d_dt_bias'] = _row128(W['d_dt_bias'])
    out['d_out_norm'] = jnp.tile(W['d_out_norm'].reshape(1, -1), (1, 4))
    wo = W['w_out']
    out['w_out'] = jnp.concatenate([_pad_heads(wo[0:256], 4, HD, LANE, 0), _pad_heads(wo[256:512], 4, HD, LANE, 0),
                                    wo[512:1024]], axis=0)
    for n in ['pre_mix_norm', 'post_mix_norm', 'pre_ffn_norm', 'post_ffn_norm']:
        out[n] = W[n].reshape(1, -1)
    out['f_w_in'] = W['f_w_in']
    out['f_conv'] = _conv_w8(W['f_conv_w'], W['f_conv_b'])
    out['f_w_out'] = W['f_w_out']
    return out


def _fn_norm_in(a, p):
    return [_rms(a[0], p[0])]


def _fn_resid_norm2(a, p):
    x1 = a[0] + _rms(a[1], p[0])
    return [x1, _rms(x1, p[1])]


def _fn_resid_norm(a, p):
    return [a[0] + _rms(a[1], p[0])]


def _fn_a_prep(a, p):
    cq, ckv, kr, cosq, sinq, cosk, sink = a
    q_norm, w_uq, kv_norm, w_uk, w_uv, rq, rk = p
    q = _nn(_rms(cq, q_norm, A_Q_LORA), w_uq)
    q = q * cosq + _nn_h3(q, rq) * sinq
    kvn = _rms(ckv, kv_norm)
    kr_r = kr * cosk + _nn_h3(kr, rk) * sink
    kk = _nn(kvn, w_uk) + jnp.concatenate([kr_r] * HEADS, axis=1)
    return [q, kk, _nn(kvn, w_uv)]


def _fn_b_prep(a, p):
    q, k, v, cosq, sinq, cosk, sink = a
    q_norm, k_norm, mq, mk, rq, rk = p
    qn = q * lax.rsqrt(_nn_h3(q * q, mq) + EPS) * q_norm
    kn = k * lax.rsqrt(_nn_h3(k * k, mk) + EPS) * k_norm
    return [qn * cosq + _nn_h3(qn, rq) * sinq, kn * cosk + _nn_h3(kn, rk) * sink, v]


def _fn_mixer_post(a, p):
    oa, ob, yc0, yc1, xs, zc, od0, od1, zd = a
    a_norm, b_norm, dskip, c_norm, d_norm, m64 = p
    oc = _rms((yc0 + yc1 + xs * dskip) * _silu(zc), c_norm)
    od = od0 + od1
    odn = od * lax.rsqrt(_nn_h3(od * od, m64) + EPS) * d_norm * _silu(zd)
    return [jnp.concatenate([_rms(oa, a_norm, GROUP_W), _rms(ob, b_norm, GROUP_W), oc, odn], axis=1)]


def _fn_assemble(a, p):
    (dbq, dxbc, dcq, dbk, dqkv, dbv, dzc, dzd, dckv, dkr, ddt0, ddt1, db0, db1, da0, da1) = a
    return [jnp.concatenate([dbq, dxbc, dcq, dbk, dqkv, dbv, dzc, dzd, dckv, dkr, ddt0 + ddt1, db0 + db1,
                             da0 + da1, jnp.zeros_like(dckv)], axis=1)]


def _pspec(T, name):
    off, w = P_OFF[name]
    return _spec2(T, w, off // w)


def _layer_fwd(l, x, h, K, tabs, L, T, next_norm, side=None):
    n = f"l{l}_"
    sv = {'x': x, 'h': h}
    p = _mm(n + "in_proj", h, K['w_in'].astype(BF16), 'nn', F32, 512, 768, 1024)
    sv['p'] = p
    a_acts = [(p, _pspec(T, 'a_cq')), (p, _pspec(T, 'a_ckv')), (p, _pspec(T, 'a_kr')),
              (tabs['a_cq'], _spec2(T, 512)), (tabs['a_sq'], _spec2(T, 512)),
              (tabs['a_ck'], _spec2(T, LANE)), (tabs['a_sk'], _spec2(T, LANE))]
    a_pars = [K['a_q_norm'], K['a_w_uq'], K['a_kv_norm'], K['a_w_uk'], K['a_w_uv'], tabs['a_rq'], tabs['a_rk']]
    qa, ka, va = _tw_fwd(n + "a_prep", _fn_a_prep, a_acts, a_pars, [(512, BF16)] * 3, L, T)
    oa, lse_a, sv['side'] = _flash_fwd(n + "a_attn", qa, ka, va, HEADS, 1, (A_NOPE + A_ROPE) ** -0.5, L, side)
    sv.update(a_acts=a_acts, a_pars=a_pars, qa=qa, ka=ka, va=va, oa=oa, lse_a=lse_a)
    b_acts = [(p, _pspec(T, 'b_q')), (p, _pspec(T, 'b_k')), (p, _pspec(T, 'b_v')),
              (tabs['b_cq'], _spec2(T, 512)), (tabs['b_sq'], _spec2(T, 512)),
              (tabs['b_ck'], _spec2(T, 256)), (tabs['b_sk'], _spec2(T, 256))]
    b_pars = [K['b_q_norm'], K['b_k_norm'], tabs['b_mq'], tabs['b_mk'], tabs['b_rq'], tabs['b_rk']]
    qb, kb, vb = _tw_fwd(n + "b_prep", _fn_b_prep, b_acts, b_pars, [(512, BF16), (256, BF16), (256, BF16)], L, T)
    ob, lse_b, _ = _flash_fwd(n + "b_attn", qb, kb, vb, HEADS, 2, HD ** -0.5, L)
    sv.update(b_acts=b_acts, b_pars=b_pars, qb=qb, kb=kb, vb=vb, ob=ob, lse_b=lse_b)
    xbc = _conv_fwd(n + "c_conv", p, P_OFF['c_xbc'][0], C_XBC, K['c_conv'], True, L, 512)
    c_seqs = [(xbc, C_XBC, 0, [(0, HD, 4), (256, HD, 2), (384, HD, 2)]),
              (p, LANE, P_OFF['c_dt'][0] // LANE, None)]
    c_rows = [K['c_a_log'], K['c_dt_bias']]
    yc0, yc1, sc0, sc1 = _scan_fwd(n + "c_ssd", _ssd_chunk, c_seqs, c_rows, C_CHUNK, L, C_PER_STEP)
    sv.update(xbc=xbc, c_seqs=c_seqs, c_rows=c_rows, sc=(sc0, sc1))
    qkv = _conv_fwd(n + "d_conv", p, P_OFF['d_qkv'][0], D_QKV, K['d_conv'], True, L, 768)
    d_seqs = [(qkv, D_QKV, 0, [(0, HD, 4), (256, HD, 4), (512, HD, 4)]),
              (p, LANE, P_OFF['d_b'][0] // LANE, None), (p, LANE, P_OFF['d_a'][0] // LANE, None)]
    d_rows = [K['d_a_log'], K['d_dt_bias']]
    od0, od1, sd0, sd1 = _scan_fwd(n + "d_delta", _delta_chunk, d_seqs, d_rows, D_CHUNK, L, D_PER_STEP,
                                   [(HEADS * D_CHUNK, HEADS * D_CHUNK)])
    sv.update(qkv=qkv, d_seqs=d_seqs, d_rows=d_rows, sd=(sd0, sd1))
    m_acts = [(oa, _spec2(T, 512)), (ob, _spec2(T, 512)), (yc0, _spec2(T, 256)), (yc1, _spec2(T, 256)),
              (xbc, _spec2(T, 256, 0)), (p, _pspec(T, 'c_z')), (od0, _spec2(T, 256)), (od1, _spec2(T, 256)),
              (p, _pspec(T, 'd_z'))]
    m_pars = [K['a_out_norm'], K['b_out_norm'], K['c_d_skip'], K['c_out_norm'], K['d_out_norm'], tabs['m64']]
    (o,) = _tw_fwd(n + "mixer_post", _fn_mixer_post, m_acts, m_pars, [(O_COLS, BF16)], L, T)
    f1 = _mm(n + "out_proj", o, K['w_out'].astype(BF16), 'nn', F32, 512, 1024, 768)
    r1_pars = [K['post_mix_norm'], K['pre_ffn_norm']]
    x1, h2 = _tw_fwd(n + "resid_mix", _fn_resid_norm2, [(x, _spec2(T, D_MODEL)), (f1, _spec2(T, D_MODEL))], r1_pars,
                     [(D_MODEL, F32), (D_MODEL, BF16)], L, T)
    sv.update(m_acts=m_acts, m_pars=m_pars, o=o, f1=f1, r1_pars=r1_pars, x1=x1, h2=h2)
    u = _mm(n + "ffn_in", h2, K['f_w_in'].astype(BF16), 'nn', F32, 512, 512, 1024)
    act = _ffn_gate_fwd(n + "ffn_gate", u, K['f_conv'], L)
    f2 = _mm(n + "ffn_out", act, K['f_w_out'].astype(BF16), 'nn', F32, 512, 1024, 1408)
    sv.update(u=u, act=act, f2=f2)
    xf = [(x1, _spec2(T, D_MODEL)), (f2, _spec2(T, D_MODEL))]
    if next_norm is None:
        (x2,) = _tw_fwd(n + "resid_ffn", _fn_resid_norm, xf, [K['post_ffn_norm']], [(D_MODEL, F32)], L, T)
        hn = None
    else:
        x2, hn = _tw_fwd(n + "resid_ffn", _fn_resid_norm2, xf, [K['post_ffn_norm'], next_norm],
                         [(D_MODEL, F32), (D_MODEL, BF16)], L, T)
    return x2, hn, sv


def _layer_bwd(l, dx2, dhn, K, sv, tabs, L, T, next_norm, side_b=None, side_a=None):
    n = f"l{l}b_"
    dK = {}
    s2 = lambda w, cb=0: _spec2(T, w, cb)
    xf = [(sv['x1'], s2(D_MODEL)), (sv['f2'], s2(D_MODEL))]
    if next_norm is None:
        (dx1a, df2), (dK['post_ffn_norm'],) = _tw_bwd(n + "resid_ffn", _fn_resid_norm, xf, [K['post_ffn_norm']],
                                                      [(dx2, s2(D_MODEL))], L, T, [True, True], [True])
        dnext = None
    else:
        (dx1a, df2), (dK['post_ffn_norm'], dnext) = _tw_bwd(
            n + "resid_ffn", _fn_resid_norm2, xf, [K['post_ffn_norm'], next_norm],
            [(dx2, s2(D_MODEL)), (dhn, s2(D_MODEL))], L, T, [True, True], [True, True])
    dact = _mm(n + "ffn_out_dx", df2, K['f_w_out'].astype(BF16), 'nt', F32, 512, 1408, 1024)
    dK['f_w_out'] = _mm(n + "ffn_out_dw", sv['act'], df2, 'tn', F32, 1408, 1024, 512)
    du, dK['f_conv'] = _ffn_gate_bwd(n + "ffn_gate", sv['u'], K['f_conv'], dact, L)
    dh2 = _mm(n + "ffn_in_dx", du, K['f_w_in'].astype(BF16), 'nt', F32, 512, 1024, 1408)
    dK['f_w_in'] = _mm(n + "ffn_in_dw", sv['h2'], du, 'tn', F32, 512, 1408, 512)
    (dxa, df1), (dK['post_mix_norm'], dK['pre_ffn_norm']) = _tw_bwd(
        n + "resid_mix", _fn_resid_norm2, [(sv['x'], s2(D_MODEL)), (sv['f1'], s2(D_MODEL))], sv['r1_pars'],
        [(dx1a, s2(D_MODEL)), (dh2, s2(D_MODEL))], L, T, [True, True], [True, True])
    do = _mm(n + "out_proj_dx", df1, K['w_out'].astype(BF16), 'nt', F32, 512, 768, 1024)
    dK['w_out'] = _mm(n + "out_proj_dw", sv['o'], df1, 'tn', F32, 768, 1024, 512)
    (doa, dob, dyc0, _, dxs_skip, dzc, dod0, _, dzd), mp = _tw_bwd(
        n + "mixer_post", _fn_mixer_post, sv['m_acts'], sv['m_pars'], [(do, s2(O_COLS))], L, T,
        [True] * 9, [True] * 5 + [False])
    dK['a_out_norm'], dK['b_out_norm'], dK['c_d_skip'], dK['c_out_norm'], dK['d_out_norm'] = mp
    (dqkv0, db0, da0), (dqkv1, db1, da1), (dK['d_a_log'], dK['d_dt_bias']) = _scan_bwd(
        n + "d_delta", _delta_chunk, sv['d_seqs'], sv['d_rows'], sv['sd'], dod0, None, D_CHUNK, L, D_PER_STEP)
    dqkv, dK['d_conv'] = _conv_bwd(n + "d_conv", sv['p'], P_OFF['d_qkv'][0], D_QKV, K['d_conv'], True,
                                   [(dqkv0, None), (dqkv1, None)], L, 768)
    (dxbc0, ddt0), (dxbc1, ddt1), (dK['c_a_log'], dK['c_dt_bias']) = _scan_bwd(
        n + "c_ssd", _ssd_chunk, sv['c_seqs'], sv['c_rows'], sv['sc'], dyc0, dxs_skip, C_CHUNK, L, C_PER_STEP)
    dxbc, dK['c_conv'] = _conv_bwd(n + "c_conv", sv['p'], P_OFF['c_xbc'][0], C_XBC, K['c_conv'], True,
                                   [(dxbc0, None), (dxbc1, None)], L, 512)
    dqb, dkb, dvb, got_b = _flash_bwd(n + "b_attn", sv['qb'], sv['kb'], sv['vb'], sv['ob'], sv['lse_b'], dob, HEADS,
                                      2, HD ** -0.5, L, side_b(dK) if side_b else None)
    (dbq, dbk, dbv), (dK['b_q_norm'], dK['b_k_norm']) = _tw_bwd(
        n + "b_prep", _fn_b_prep, sv['b_acts'], sv['b_pars'], [(dqb, s2(512)), (dkb, s2(256)), (dvb, s2(256))],
        L, T, [True] * 3 + [False] * 4, [True, True] + [False] * 4)
    dqa, dka, dva, got_a = _flash_bwd(n + "a_attn", sv['qa'], sv['ka'], sv['va'], sv['oa'], sv['lse_a'], doa, HEADS,
                                      1, (A_NOPE + A_ROPE) ** -0.5, L, side_a(dK) if side_a else None)
    (dcq, dckv, dkr), ap = _tw_bwd(
        n + "a_prep", _fn_a_prep, sv['a_acts'], sv['a_pars'], [(dqa, s2(512)), (dka, s2(512)), (dva, s2(512))],
        L, T, [True] * 3 + [False] * 4, [True] * 5 + [False] * 2)
    dK['a_q_norm'], dK['a_w_uq'], dK['a_kv_norm'], dK['a_w_uk'], dK['a_w_uv'] = ap
    pieces = [(dbq, s2(512)), (dxbc, s2(512)), (dcq, s2(256)), (dbk, s2(256)), (dqkv, s2(768)), (dbv, s2(256)),
              (dzc, s2(256)), (dzd, s2(256)), (dckv, s2(LANE)), (dkr, s2(LANE)),
              (ddt0, s2(LANE)), (ddt1, s2(LANE)), (db0, s2(LANE)), (db1, s2(LANE)), (da0, s2(LANE)),
              (da1, s2(LANE))]
    (dp,) = _tw_fwd(n + "assemble_dp", _fn_assemble, pieces, [], [(P_COLS, BF16)], L, T)
    dh = _mm(n + "in_proj_dx", dp, K['w_in'].astype(BF16), 'nt', F32, 512, 1024, 768)
    dK['w_in'] = _mm(n + "in_proj_dw", sv['h'], dp, 'tn', F32, 512, 1280, 512)
    return dxa, dh, dK, dnext, got_b, got_a


def _tables(L):
    ca, sa = _rope_angles(L, A_ROPE)
    cb, sb = _rope_angles(L, HD)
    t = {}
    t['a_cq'], t['a_sq'], t['a_rq'] = _place_tables(L, ca, sa, 512, [LANE * h + A_NOPE for h in range(4)])
    t['a_ck'], t['a_sk'], t['a_rk'] = _place_tables(L, ca, sa, LANE, [A_NOPE])
    t['b_cq'], t['b_sq'], t['b_rq'] = _place_tables(L, cb, sb, 512, [LANE * h for h in range(4)])
    t['b_ck'], t['b_sk'], t['b_rk'] = _place_tables(L, cb, sb, 256, [LANE * h for h in range(2)])
    t['b_mq'] = _head_mean_matrix(512, LANE, HD)
    t['b_mk'] = _head_mean_matrix(256, LANE, HD)
    t['m64'] = _head_mean_matrix(256, HD, HD)
    return t


def kernel(x, pre_mix_norm, w_in, a_q_norm, a_w_uq, a_kv_norm, a_w_ukv, a_out_norm, b_q_norm, b_k_norm, b_out_norm, c_conv_w, c_conv_b, c_a_log, c_dt_bias, c_d_skip, c_out_norm, d_conv_w, d_a_log, d_dt_bias, d_out_norm, w_out, post_mix_norm, pre_ffn_norm, f_w_in, f_conv_w, f_conv_b, f_w_out, post_ffn_norm, loss_target, m_pre_mix_norm, m_w_in, m_a_q_norm, m_a_w_uq, m_a_kv_norm, m_a_w_ukv, m_a_out_norm, m_b_q_norm, m_b_k_norm, m_b_out_norm, m_c_conv_w, m_c_conv_b, m_c_a_log, m_c_dt_bias, m_c_d_skip, m_c_out_norm, m_d_conv_w, m_d_a_log, m_d_dt_bias, m_d_out_norm, m_w_out, m_post_mix_norm, m_pre_ffn_norm, m_f_w_in, m_f_conv_w, m_f_conv_b, m_f_w_out, m_post_ffn_norm, v_pre_mix_norm, v_w_in, v_a_q_norm, v_a_w_uq, v_a_kv_norm, v_a_w_ukv, v_a_out_norm, v_b_q_norm, v_b_k_norm, v_b_out_norm, v_c_conv_w, v_c_conv_b, v_c_a_log, v_c_dt_bias, v_c_d_skip, v_c_out_norm, v_d_conv_w, v_d_a_log, v_d_dt_bias, v_d_out_norm, v_w_out, v_post_mix_norm, v_pre_ffn_norm, v_f_w_in, v_f_conv_w, v_f_conv_b, v_f_w_out, v_post_ffn_norm):
    loc = locals()
    Wl = {n: loc[n] for n in WEIGHTS}
    Ml = {n: loc['m_' + n] for n in WEIGHTS}
    Vl = {n: loc['v_' + n] for n in WEIGHTS}
    L = x.shape[1]
    T = min(256, L)
    x0 = x.reshape(L, D_MODEL)
    tgt = loss_target.reshape(L, D_MODEL)

    def shards(l):
        return [Wl[n][l].astype(BF16) if n in MXU_WEIGHTS else Wl[n][l] for n in SHARDED]

    def layer_weights(l, gathered):
        W = {n: Wl[n][l] for n in SMALL}
        for n, g in zip(SHARDED, gathered):
            W[n] = jnp.concatenate([g[j] for j in range(4)], axis=SHARD_AXIS[n] - 1)
        return W

    def chip_blocks(g, n):
        return jnp.stack(jnp.split(g, 4, axis=SHARD_AXIS[n] - 1))

    tabs = _tables(L)
    norm_in = [Wl['pre_mix_norm'][l].reshape(1, -1) for l in range(DEPTH)]
    def layer_shape(n):
        s = list(Wl[n].shape[1:])
        if n in SHARD_AXIS:
            s[SHARD_AXIS[n] - 1] *= 4
        return tuple(s)

    unbuild = jax.vjp(_build_layer, {n: jnp.zeros(layer_shape(n), F32) for n in WEIGHTS})[1]

    (h,) = _tw_fwd("l0_norm_in", _fn_norm_in, [(x0, _spec2(T, D_MODEL))], [norm_in[0]], [(D_MODEL, BF16)], L, T)
    gathered = _gather_copies(shards(0)).call("gather_l0")
    xs, saves, Ks = x0, [], []
    for l in range(DEPTH):
        Ks.append(_build_layer(layer_weights(l, gathered)))
        last = l + 1 == DEPTH
        xs, h, sv = _layer_fwd(l, xs, h, Ks[l], tabs, L, T, None if last else norm_in[l + 1],
                               None if last else _gather_copies(shards(l + 1)))
        gathered = sv['side']
        saves.append(sv)
    dy, loss_acc = _loss_call(xs, tgt, L)
    loss = lax.psum(loss_acc[0, 0], ("x", "y", "c"))

    ffn = ['f_w_in', 'f_conv_w', 'f_w_out']
    rest = [n for n in SHARDED if n not in ffn]

    def ffn_side(dK):
        g = {'f_w_in': dK['f_w_in'], 'f_conv_w': dK['f_conv'][0:3], 'f_w_out': dK['f_w_out']}
        return _scatter_copies([chip_blocks(g[n], n) for n in ffn], None)

    def rest_blocks(dK):
        full = dict(dK)
        full.setdefault('pre_mix_norm', jnp.zeros((1, D_MODEL), F32))
        (g,) = unbuild(full)
        return [chip_blocks(g[n], n) for n in rest]

    grads = [None] * DEPTH
    recv = {}
    dx, dhn = dy, None
    for l in reversed(range(DEPTH)):
        last = l + 1 == DEPTH
        side_b = None if last else (lambda dK, up=grads[l + 1]: _scatter_copies(rest_blocks(up), None))
        dxa, dh, dK, dnext, got_b, got_a = _layer_bwd(l, dx, dhn, Ks[l], saves[l], tabs, L, T,
                                                      None if last else norm_in[l + 1], side_b, ffn_side)
        recv.update({(l, n): r for n, r in zip(ffn, got_a)})
        if not last:
            recv.update({(l + 1, n): r for n, r in zip(rest, got_b)})
            grads[l + 1]['pre_mix_norm'] = dnext
        grads[l] = dK
        dx, dhn = dxa, dh
    (dx_in,), (grads[0]['pre_mix_norm'],) = _tw_bwd(
        "l0b_norm_in", _fn_norm_in, [(x0, _spec2(T, D_MODEL))], [norm_in[0]], [(dhn, _spec2(T, D_MODEL))], L, T,
        [True], [True], addto={0: (dx, _spec2(T, D_MODEL))})
    small_shapes = [Wl[n].shape for n in SMALL]
    gfull = [unbuild(grads[l])[0] for l in range(DEPTH)]
    gs = _pack([jnp.stack([gfull[l][n] for l in range(DEPTH)]) for n in SMALL], LANE, 8)
    *got, recv_small = _scatter_copies(rest_blocks(grads[0]), gs).call("scatter_last")
    recv.update({(0, n): r for n, r in zip(rest, got)})

    as2d = lambda a: a.reshape(-1, a.shape[-1])
    parts = [[_sum_slots(f"sum_{n}_{l}", recv[l, n].reshape(4, -1, recv[l, n].shape[-1])) for l in range(DEPTH)]
             for n in SHARDED]
    pairs = _swap_copies(parts).call("swap_cores")
    kinds = ['grad', 'delta', 'new_m', 'new_v']
    res = {}
    for n, pair in zip(SHARDED, pairs):
        upd = _adamw_call("adamw_" + n, pair.reshape(2, -1, pair.shape[-1]), as2d(Wl[n]), as2d(Ml[n]), as2d(Vl[n]))
        for kind, a in zip(kinds, upd):
            res[kind, n] = a.reshape(Wl[n].shape)
    small = _adamw_call("adamw_small", recv_small, _pack([Wl[n] for n in SMALL], LANE, 8),
                        _pack([Ml[n] for n in SMALL], LANE, 8), _pack([Vl[n] for n in SMALL], LANE, 8))
    for kind, s in zip(kinds, small):
        for n, a in zip(SMALL, _unpack(s, small_shapes)):
            res[kind, n] = a
    outs = [loss, dx_in.reshape(x.shape)]
    for kind in ['grad', 'delta', 'new_m', 'new_v']:
        outs += [res[kind, n] for n in WEIGHTS]
    return tuple(outs)
```

```python
import functools
import math

import numpy as np
import jax
import jax.numpy as jnp
from jax import lax
from jax.experimental import pallas as pl
from jax.experimental.pallas import tpu as pltpu

F32 = jnp.float32
BF16 = jnp.bfloat16
MESH = pl.DeviceIdType.MESH
VMEM_LIMIT = 48 * 1024 * 1024
LANE = 128

D_MODEL = 1024
DEPTH = 2
GRID_W = 64
ROPE_BASE = 10000.0
EPS = 1e-6
GROUP_W = 256
HEADS = 4
HD = 64
A_NOPE, A_ROPE, A_Q_LORA, A_KV_LORA = 64, 32, 192, 128
A_COLS = A_Q_LORA + A_KV_LORA + A_ROPE
B_COLS = 512
C_XBC = 512
C_COLS = GROUP_W + C_XBC + 8
D_QKV = 768
D_COLS = D_QKV + GROUP_W + 16
IN_COLS = A_COLS + B_COLS + C_COLS + D_COLS
C_CHUNK = 128
D_CHUNK = 64
C_PER_STEP = 1
D_PER_STEP = 2
D_FF = 2816
ADAM_LR, ADAM_B1, ADAM_B2, ADAM_EPS, ADAM_WD, ADAM_STEP = 0.001, 0.9, 0.999, 1e-08, 0.01, 10

WEIGHTS = ['pre_mix_norm', 'w_in', 'a_q_norm', 'a_w_uq', 'a_kv_norm', 'a_w_ukv', 'a_out_norm', 'b_q_norm',
           'b_k_norm', 'b_out_norm', 'c_conv_w', 'c_conv_b', 'c_a_log', 'c_dt_bias', 'c_d_skip', 'c_out_norm',
           'd_conv_w', 'd_a_log', 'd_dt_bias', 'd_out_norm', 'w_out', 'post_mix_norm', 'pre_ffn_norm', 'f_w_in',
           'f_conv_w', 'f_conv_b', 'f_w_out', 'post_ffn_norm']
SHARD_AXIS = {'w_in': 2, 'a_w_uq': 2, 'a_w_ukv': 2, 'c_conv_w': 2, 'd_conv_w': 2, 'w_out': 1, 'f_w_in': 2,
              'f_conv_w': 2, 'f_w_out': 1}
SHARDED = [n for n in WEIGHTS if n in SHARD_AXIS]
SMALL = [n for n in WEIGHTS if n not in SHARD_AXIS]
MXU_WEIGHTS = ('w_in', 'a_w_uq', 'a_w_ukv', 'w_out', 'f_w_in', 'f_w_out')

P_LAYOUT = [('b_q', 0, 512), ('c_xbc', 512, 512), ('a_cq', 1024, 256), ('b_k', 1280, 256), ('d_qkv', 1536, 768),
            ('b_v', 2304, 256), ('c_z', 2560, 256), ('d_z', 2816, 256), ('a_ckv', 3072, 128), ('a_kr', 3200, 128),
            ('c_dt', 3328, 128), ('d_b', 3456, 128), ('d_a', 3584, 128), ('pad', 3712, 128)]
P_OFF = {n: (o, w) for n, o, w in P_LAYOUT}
P_COLS = 3840
O_COLS = 1536


def _cparams(sem):
    return pltpu.CompilerParams(dimension_semantics=sem, vmem_limit_bytes=VMEM_LIMIT)


def _tile(n, target):
    best = None
    for d in range(LANE, min(n, target) + 1, LANE):
        if n % d == 0:
            best = d
    return best if best is not None else n


_NN = ((1,), (0,))
_NT = ((1,), (1,))
_TN = ((0,), (0,))


def _raw_dot(a, b, dims, hi):
    if hi:
        prec = lax.Precision.HIGH if hi == 'high' else lax.Precision.HIGHEST
        return lax.dot_general(a, b, (dims, ((), ())), precision=prec, preferred_element_type=F32)
    return lax.dot_general(a.astype(BF16), b.astype(BF16), (dims, ((), ())), preferred_element_type=F32)


def _make_dots(hi):
    @jax.custom_vjp
    def nn(a, b):
        return _raw_dot(a, b, _NN, hi)

    @jax.custom_vjp
    def nt(a, b):
        return _raw_dot(a, b, _NT, hi)

    @jax.custom_vjp
    def tn(a, b):
        return _raw_dot(a, b, _TN, hi)

    nn.defvjp(lambda a, b: (nn(a, b), (a, b)), lambda r, g: (nt(g, r[1]), tn(r[0], g)))
    nt.defvjp(lambda a, b: (nt(a, b), (a, b)), lambda r, g: (nn(g, r[1]), tn(g, r[0])))
    tn.defvjp(lambda a, b: (tn(a, b), (a, b)), lambda r, g: (nt(r[1], g), nn(r[0], g)))
    return nn, nt, tn


_nn, _nt, _tn = _make_dots(False)
_nn_hi, _nt_hi, _tn_hi = _make_dots(True)
_nn_h3, _nt_h3, _tn_h3 = _make_dots('high')


def _sigmoid(x):
    return 1.0 / (1.0 + jnp.exp(-x))


def _silu(x):
    return x * _sigmoid(x)


def _softplus(x):
    return jnp.maximum(x, 0.0) + jnp.log(1.0 + jnp.exp(-jnp.abs(x)))


def _rms(x, w, n=None):
    n = x.shape[-1] if n is None else n
    ms = jnp.sum(x * x, axis=-1, keepdims=True) * (1.0 / n)
    return x * lax.rsqrt(ms + EPS) * w


def _spec2(T, w, cb=0):
    return pl.BlockSpec((T, w), lambda i: (i, cb))


def _full_spec(a):
    nd = a.ndim
    return pl.BlockSpec(a.shape, lambda i: (0,) * nd)


def _tw_fwd(name, fn, acts, params, outs, L, T):
    na, npar = len(acts), len(params)

    def kern(*refs):
        a = [r[...].astype(F32) for r in refs[:na]]
        p = [r[...].astype(F32) for r in refs[na:na + npar]]
        res = fn(a, p)
        for r, o in zip(refs[na + npar:], res):
            r[...] = o.astype(r.dtype)

    return pl.pallas_call(
        kern, name=name, grid=(L // T,),
        in_specs=[s for _, s in acts] + [_full_spec(p) for p in params],
        out_specs=[_spec2(T, w) for w, _ in outs],
        out_shape=[jax.ShapeDtypeStruct((L, w), dt) for w, dt in outs],
        compiler_params=_cparams(("arbitrary",)),
    )(*[a for a, _ in acts], *params)


def _tw_bwd(name, fn, acts, params, douts, L, T, act_grad, par_grad, addto=None):
    na, npar, nd = len(acts), len(params), len(douts)
    addto = addto or {}
    add_keys = sorted(addto)
    ga = [k for k in range(na) if act_grad[k]]
    gp = [k for k in range(npar) if par_grad[k]]

    def kern(*refs):
        i = pl.program_id(0)
        a = [r[...].astype(F32) for r in refs[:na]]
        p = [r[...].astype(F32) for r in refs[na:na + npar]]
        g = [r[...].astype(F32) for r in refs[na + npar:na + npar + nd]]
        pos = na + npar + nd
        adds = [r[...].astype(F32) for r in refs[pos:pos + len(add_keys)]]
        pos += len(add_keys)
        da_refs = refs[pos:pos + len(ga)]
        dp_refs = refs[pos + len(ga):]

        def f(ad, pd):
            af, pf = list(a), list(p)
            for k, v in zip(ga, ad):
                af[k] = v
            for k, v in zip(gp, pd):
                pf[k] = v
            return fn(af, pf)

        _, vjp = jax.vjp(f, [a[k] for k in ga], [p[k] for k in gp])
        dad, dpd = vjp(list(g))
        for n, (r, d) in enumerate(zip(da_refs, dad)):
            if n in addto:
                d = d + adds[add_keys.index(n)]
            r[...] = d.astype(r.dtype)

        @pl.when(i == 0)
        def _():
            for r in dp_refs:
                r[...] = jnp.zeros(r.shape, F32)

        for r, d in zip(dp_refs, dpd):
            r[...] += d

    def width(spec):
        return spec.block_shape[-1]

    res = pl.pallas_call(
        kern, name=name, grid=(L // T,),
        in_specs=[s for _, s in acts] + [_full_spec(p) for p in params] + [s for _, s in douts]
        + [addto[k][1] for k in add_keys],
        out_specs=[_spec2(T, width(acts[k][1])) for k in ga] + [_full_spec(params[k]) for k in gp],
        out_shape=[jax.ShapeDtypeStruct((L, width(acts[k][1])), F32) for k in ga]
        + [jax.ShapeDtypeStruct(params[k].shape, F32) for k in gp],
        compiler_params=_cparams(("arbitrary",)),
    )(*[a for a, _ in acts], *params, *[a for a, _ in douts], *[addto[k][0] for k in add_keys])
    return list(res[:len(ga)]), list(res[len(ga):])


def _mm(name, a, b, mode, out_dtype, tm, tn, tk):
    halves_a = a.shape[-1] if (a.ndim == 3 and mode == 'nt') else None
    halves_b = b.shape[-1] if (b.ndim == 3 and mode == 'tn') else None
    if mode == 'nn':
        (M, K), N = a.shape, b.shape[1]
    elif mode == 'nt':
        M, K, N = a.shape[-2], (2 * halves_a if halves_a else a.shape[1]), b.shape[0]
    else:
        (K, M), N = a.shape, (2 * halves_b if halves_b else b.shape[1])
    tm = _tile(M, tm)
    tn = _tile(halves_b or N, tn)
    tk = _tile(halves_a or K, tk)
    nk = K // tk
    if mode == 'nn':
        a_spec = pl.BlockSpec((tm, tk), lambda i, j, k: (i, k))
        b_spec = pl.BlockSpec((tk, tn), lambda i, j, k: (k, j))
        dims = _NN
    elif mode == 'nt':
        a_spec = pl.BlockSpec((tm, tk), lambda i, j, k: (i, k))
        if halves_a:
            per = halves_a // tk
            a_spec = pl.BlockSpec((None, tm, tk), lambda i, j, k: (k // per, i, k % per))
        b_spec = pl.BlockSpec((tn, tk), lambda i, j, k: (j, k))
        dims = _NT
    else:
        a_spec = pl.BlockSpec((tk, tm), lambda i, j, k: (k, i))
        b_spec = pl.BlockSpec((tk, tn), lambda i, j, k: (k, j))
        if halves_b:
            per = halves_b // tn
            b_spec = pl.BlockSpec((None, tk, tn), lambda i, j, k: (j // per, k, j % per))
        dims = _TN

    def kern(a_ref, b_ref, o_ref, acc):
        k = pl.program_id(2)

        @pl.when(k == 0)
        def _():
            acc[...] = jnp.zeros(acc.shape, F32)

        acc[...] += lax.dot_general(a_ref[...].astype(BF16), b_ref[...].astype(BF16), (dims, ((), ())),
                                    preferred_element_type=F32)

        @pl.when(k == nk - 1)
        def _():
            o_ref[...] = acc[...].astype(o_ref.dtype)

    return pl.pallas_call(
        kern, name=name, grid=(M // tm, N // tn, nk),
        in_specs=[a_spec, b_spec],
        out_specs=pl.BlockSpec((tm, tn), lambda i, j, k: (i, j)),
        out_shape=jax.ShapeDtypeStruct((M, N), out_dtype),
        scratch_shapes=[pltpu.VMEM((tm, tn), F32)],
        compiler_params=_cparams(("arbitrary", "arbitrary", "arbitrary")),
    )(a, b)


def _host(kern, n_in, n_out, side, grid, n_scratch=0):
    if side is None:
        return kern, [], [], [], [], []
    ni, no = len(side.ins), len(side.out_shapes)

    def hosted(*refs):
        ins, s_in = refs[:n_in], refs[n_in:n_in + ni]
        pos = n_in + ni
        outs, s_out = refs[pos:pos + n_out], refs[pos + n_out:pos + n_out + no]
        pos += n_out + no
        own, sems = refs[pos:pos + n_scratch], refs[pos + n_scratch:]
        ids = [pl.program_id(d) for d in range(len(grid))]
        first = functools.reduce(lambda a, b: a & b, [i == 0 for i in ids])
        last = functools.reduce(lambda a, b: a & b, [i == g - 1 for i, g in zip(ids, grid)])

        @pl.when(first)
        def _():
            side.start(s_in, s_out, sems)

        kern(*ins, *outs, *own)

        @pl.when(last)
        def _():
            side.finish(s_in, s_out, sems)

    return hosted, [_ANY] * ni, [_ANY] * no, side.out_shapes, side.scratch(), side.ins


def _flash_fwd(name, q, k, v, H, rep, scale, L, side=None):
    tq = min(512, L)
    nq = L // tq
    KC = min(2048, L)
    nkc = L // KC
    log2e = 1.0 / math.log(2.0)

    def kern(q_ref, k_ref, v_ref, o_ref, lse_ref):
        qb = q_ref[...]
        m = jnp.full((tq, 1), -1e30, F32)
        l = jnp.zeros((tq, 1), F32)
        acc = jnp.zeros((tq, LANE), F32)
        for c in range(nkc):
            kb = k_ref[c * KC:(c + 1) * KC, :]
            vb = v_ref[c * KC:(c + 1) * KC, :]
            s = lax.dot_general(qb, kb, (_NT, ((), ())), preferred_element_type=F32) * (scale * log2e)
            mn = jnp.maximum(m, jnp.max(s, axis=-1, keepdims=True))
            al = jnp.exp2(m - mn)
            p = jnp.exp2(s - mn)
            l = al * l + jnp.sum(p, axis=-1, keepdims=True)
            acc = al * acc + lax.dot_general(p.astype(BF16), vb, (_NN, ((), ())), preferred_element_type=F32)
            m = mn
        o_ref[...] = acc / l
        lse_ref[...] = m * math.log(2.0) + jnp.log(l)

    kern, s_in, s_out, s_shapes, s_scratch, s_args = _host(kern, 3, 2, side, (H, nq))
    res = pl.pallas_call(
        kern, name=name, grid=(H, nq),
        in_specs=[pl.BlockSpec((tq, LANE), lambda h, i: (i, h)),
                  pl.BlockSpec((L, LANE), lambda h, i: (0, h // rep)),
                  pl.BlockSpec((L, LANE), lambda h, i: (0, h // rep))] + s_in,
        out_specs=[pl.BlockSpec((tq, LANE), lambda h, i: (i, h)),
                   pl.BlockSpec((tq, 1), lambda h, i: (h * nq + i, 0))] + s_out,
        out_shape=[jax.ShapeDtypeStruct((L, H * LANE), F32), jax.ShapeDtypeStruct((H * L, 1), F32)] + s_shapes,
        scratch_shapes=s_scratch,
        compiler_params=_cparams(("arbitrary", "arbitrary")),
    )(q, k, v, *s_args)
    return res[0], res[1], list(res[2:])


def _flash_bwd(name, q, k, v, o, lse, do, H, rep, scale, L, side=None):
    tq = min(256, L)
    nq = L // tq
    KC = min(2048, L)
    nkc = L // KC
    Hkv = H // rep

    def kern(q_ref, k_ref, v_ref, o_ref, lse_ref, do_ref, dq_ref, dk_ref, dv_ref):
        h = pl.program_id(0)
        i = pl.program_id(1)

        @pl.when((i == 0) & (h % rep == 0))
        def _():
            dk_ref[...] = jnp.zeros(dk_ref.shape, F32)
            dv_ref[...] = jnp.zeros(dv_ref.shape, F32)

        qb = q_ref[...]
        do = do_ref[...]
        dob = do.astype(BF16)
        delta = jnp.sum(do * o_ref[...], axis=-1, keepdims=True)
        lse = lse_ref[...]
        dq = jnp.zeros((tq, LANE), F32)
        for c in range(nkc):
            sl = slice(c * KC, (c + 1) * KC)
            kb = k_ref[sl, :]
            vb = v_ref[sl, :]
            s = lax.dot_general(qb, kb, (_NT, ((), ())), preferred_element_type=F32) * scale
            p = jnp.exp(s - lse)
            dp = lax.dot_general(dob, vb, (_NT, ((), ())), preferred_element_type=F32)
            ds = (p * (dp - delta) * scale).astype(BF16)
            dq = dq + lax.dot_general(ds, kb, (_NN, ((), ())), preferred_element_type=F32)
            dk_ref[sl, :] += lax.dot_general(ds, qb, (_TN, ((), ())), preferred_element_type=F32)
            dv_ref[sl, :] += lax.dot_general(p.astype(BF16), dob, (_TN, ((), ())), preferred_element_type=F32)
        dq_ref[...] = dq

    kern, s_in, s_out, s_shapes, s_scratch, s_args = _host(kern, 6, 3, side, (H, nq))
    res = pl.pallas_call(
        kern, name=name, grid=(H, nq),
        in_specs=[pl.BlockSpec((tq, LANE), lambda h, i: (i, h)),
                  pl.BlockSpec((L, LANE), lambda h, i: (0, h // rep)),
                  pl.BlockSpec((L, LANE), lambda h, i: (0, h // rep)),
                  pl.BlockSpec((tq, LANE), lambda h, i: (i, h)),
                  pl.BlockSpec((tq, 1), lambda h, i: (h * nq + i, 0)),
                  pl.BlockSpec((tq, LANE), lambda h, i: (i, h))] + s_in,
        out_specs=[pl.BlockSpec((tq, LANE), lambda h, i: (i, h)),
                   pl.BlockSpec((L, LANE), lambda h, i: (0, h // rep)),
                   pl.BlockSpec((L, LANE), lambda h, i: (0, h // rep))] + s_out,
        out_shape=[jax.ShapeDtypeStruct((L, H * LANE), F32), jax.ShapeDtypeStruct((L, Hkv * LANE), F32),
                   jax.ShapeDtypeStruct((L, Hkv * LANE), F32)] + s_shapes,
        scratch_shapes=s_scratch,
        compiler_params=_cparams(("arbitrary", "arbitrary")),
    )(q, k, v, o, lse, do, *s_args)
    return res[0], res[1], res[2], list(res[3:])


def _shift_dn(x, first_row):
    row = lax.broadcasted_iota(jnp.int32, x.shape, 0)
    return jnp.where(row == 0, first_row, pltpu.roll(x, 1, 0))


def _shift_up(x, last_row):
    n = x.shape[0]
    row = lax.broadcasted_iota(jnp.int32, x.shape, 0)
    return jnp.where(row == n - 1, last_row, pltpu.roll(x, n - 1, 0))


def _halo_specs(ndim, lead, T, tc, cb0, L):
    r8 = T // 8
    last8 = L // 8 - 1
    if ndim == 2:
        return [pl.BlockSpec((T, tc), lambda j, i: (i, cb0 + j)),
                pl.BlockSpec((8, tc), lambda j, i: (jnp.maximum(i * r8 - 1, 0), cb0 + j)),
                pl.BlockSpec((8, tc), lambda j, i: (jnp.minimum((i + 1) * r8, last8), cb0 + j))]
    return [pl.BlockSpec((None, T, tc), lambda j, i: (lead, i, cb0 + j)),
            pl.BlockSpec((None, 8, tc), lambda j, i: (lead, jnp.maximum(i * r8 - 1, 0), cb0 + j)),
            pl.BlockSpec((None, 8, tc), lambda j, i: (lead, jnp.minimum((i + 1) * r8, last8), cb0 + j))]


def _conv_rows(x_ref, xp_ref, xn_ref, w, first, last):
    x = x_ref[...]
    T = x.shape[0]
    w0, w1, w2, b = w[0:1], w[1:2], w[2:3], w[3:4]
    pr = jnp.where(first, 0.0, xp_ref[7:8, :])
    pr2 = jnp.where(first, 0.0, xp_ref[6:7, :])
    nr = jnp.where(last, 0.0, xn_ref[0:1, :])
    nr2 = jnp.where(last, 0.0, xn_ref[1:2, :])
    xm1 = _shift_dn(x, pr)
    xp1 = _shift_up(x, nr)
    pre = xm1 * w0 + x * w1 + xp1 * w2 + b
    pre_m1 = pr2 * w0 + pr * w1 + x[0:1] * w2 + b
    pre_T = x[T - 1:T] * w0 + nr * w1 + nr2 * w2 + b
    return x, xm1, xp1, pre, pre_m1, pre_T


def _conv_grads(dpre, dpre_m1, dpre_T, x, xm1, xp1, w):
    dx = _shift_up(dpre, dpre_T) * w[0:1] + dpre * w[1:2] + _shift_dn(dpre, dpre_m1) * w[2:3]
    row = lax.broadcasted_iota(jnp.int32, (8, x.shape[1]), 0)
    dw = (jnp.where(row == 0, jnp.sum(dpre * xm1, axis=0, keepdims=True), 0.0)
          + jnp.where(row == 1, jnp.sum(dpre * x, axis=0, keepdims=True), 0.0)
          + jnp.where(row == 2, jnp.sum(dpre * xp1, axis=0, keepdims=True), 0.0)
          + jnp.where(row == 3, jnp.sum(dpre, axis=0, keepdims=True), 0.0))
    return dx, dw


def _conv_fwd(name, x, col0, C, w8, act, L, tc):
    T = min(256, L)
    nt = L // T
    cb0 = col0 // tc

    def kern(x_ref, xp_ref, xn_ref, w_ref, o_ref):
        i = pl.program_id(1)
        x = x_ref[...]
        w = w_ref[...]
        pr = jnp.where(i == 0, 0.0, xp_ref[7:8, :])
        nr = jnp.where(i == nt - 1, 0.0, xn_ref[0:1, :])
        pre = _shift_dn(x, pr) * w[0:1] + x * w[1:2] + _shift_up(x, nr) * w[2:3] + w[3:4]
        o_ref[...] = _silu(pre) if act else pre

    return pl.pallas_call(
        kern, name=name, grid=(C // tc, nt),
        in_specs=_halo_specs(2, None, T, tc, cb0, L) + [pl.BlockSpec((8, tc), lambda j, i: (0, j))],
        out_specs=pl.BlockSpec((T, tc), lambda j, i: (i, j)),
        out_shape=jax.ShapeDtypeStruct((L, C), F32),
        compiler_params=_cparams(("arbitrary", "arbitrary")),
    )(x, x, x, w8)


def _conv_bwd(name, x, col0, C, w8, act, gs, L, tc):
    T = min(256, L)
    nt = L // T
    cb0 = col0 // tc
    ng = len(gs)

    def dact(pre, g):
        if not act:
            return g
        s = _sigmoid(pre)
        return g * (s * (1.0 + pre * (1.0 - s)))

    def kern(*refs):
        x_ref, xp_ref, xn_ref, w_ref = refs[:4]
        g_refs = refs[4:4 + 3 * ng]
        dx_ref, dw_ref = refs[4 + 3 * ng:]
        i = pl.program_id(1)
        first = i == 0
        last = i == nt - 1
        w = w_ref[...]
        g = g_refs[0][...]
        gp = g_refs[1][7:8, :]
        gn = g_refs[2][0:1, :]
        for n in range(1, ng):
            g = g + g_refs[3 * n][...]
            gp = gp + g_refs[3 * n + 1][7:8, :]
            gn = gn + g_refs[3 * n + 2][0:1, :]
        x, xm1, xp1, pre, pre_m1, pre_T = _conv_rows(x_ref, xp_ref, xn_ref, w, first, last)
        dpre_m1 = jnp.where(first, 0.0, dact(pre_m1, gp))
        dpre_T = jnp.where(last, 0.0, dact(pre_T, gn))
        dx_ref[...], dw = _conv_grads(dact(pre, g), dpre_m1, dpre_T, x, xm1, xp1, w)

        @pl.when(first)
        def _():
            dw_ref[...] = jnp.zeros((8, tc), F32)

        dw_ref[...] += dw

    g_specs, g_args = [], []
    for arr, lead in gs:
        g_specs += _halo_specs(arr.ndim, lead, T, tc, 0, L)
        g_args += [arr, arr, arr]
    return pl.pallas_call(
        kern, name=name, grid=(C // tc, nt),
        in_specs=_halo_specs(2, None, T, tc, cb0, L) + [pl.BlockSpec((8, tc), lambda j, i: (0, j))] + g_specs,
        out_specs=[pl.BlockSpec((T, tc), lambda j, i: (i, j)), pl.BlockSpec((8, tc), lambda j, i: (0, j))],
        out_shape=[jax.ShapeDtypeStruct((L, C), F32), jax.ShapeDtypeStruct((8, C), F32)],
        compiler_params=_cparams(("arbitrary", "arbitrary")),
    )(x, x, x, w8, *g_args)


FFN_TC = 1408


def _ffn_gate_fwd(name, u, w8, L):
    T = min(256, L)
    nt = L // T
    ncb = D_FF // FFN_TC

    def kern(xg, xgp, xgn, xu, xup, xun, wg_ref, wu_ref, o_ref):
        i = pl.program_id(1)
        pre_g = _conv_rows(xg, xgp, xgn, wg_ref[...], i == 0, i == nt - 1)[3]
        pre_u = _conv_rows(xu, xup, xun, wu_ref[...], i == 0, i == nt - 1)[3]
        o_ref[...] = (_silu(pre_g) * pre_u).astype(BF16)

    return pl.pallas_call(
        kern, name=name, grid=(ncb, nt),
        in_specs=_halo_specs(2, None, T, FFN_TC, 0, L) + _halo_specs(2, None, T, FFN_TC, ncb, L)
        + [pl.BlockSpec((8, FFN_TC), lambda j, i: (0, j)), pl.BlockSpec((8, FFN_TC), lambda j, i: (0, j + ncb))],
        out_specs=pl.BlockSpec((T, FFN_TC), lambda j, i: (i, j)),
        out_shape=jax.ShapeDtypeStruct((L, D_FF), BF16),
        compiler_params=_cparams(("arbitrary", "arbitrary")),
    )(u, u, u, u, u, u, w8, w8)


def _ffn_gate_bwd(name, u, w8, da, L):
    T = min(128, L)
    nt = L // T
    ncb = D_FF // FFN_TC

    def kern(xg, xgp, xgn, xu, xup, xun, wg_ref, wu_ref, d_ref, dp_ref, dn_ref, du_ref, dwg_ref, dwu_ref):
        i = pl.program_id(1)
        first = i == 0
        last = i == nt - 1
        wg = wg_ref[...]
        wu = wu_ref[...]
        g, gm1, gp1, pg, pg_m1, pg_T = _conv_rows(xg, xgp, xgn, wg, first, last)
        v, vm1, vp1, pu, pu_m1, pu_T = _conv_rows(xu, xup, xun, wu, first, last)

        def dpre(pg_, pu_, d):
            s = _sigmoid(pg_)
            return d * pu_ * (s * (1.0 + pg_ * (1.0 - s))), d * (pg_ * s)

        dg, dv = dpre(pg, pu, d_ref[...])
        dg_m1, dv_m1 = dpre(pg_m1, pu_m1, jnp.where(first, 0.0, dp_ref[7:8, :]))
        dg_T, dv_T = dpre(pg_T, pu_T, jnp.where(last, 0.0, dn_ref[0:1, :]))
        du_ref[0], dwg = _conv_grads(dg, dg_m1, dg_T, g, gm1, gp1, wg)
        du_ref[1], dwu = _conv_grads(dv, dv_m1, dv_T, v, vm1, vp1, wu)

        @pl.when(first)
        def _():
            dwg_ref[...] = jnp.zeros(dwg_ref.shape, F32)
            dwu_ref[...] = jnp.zeros(dwu_ref.shape, F32)

        dwg_ref[...] += dwg
        dwu_ref[...] += dwu

    wspec = pl.BlockSpec((8, FFN_TC), lambda j, i: (0, j))
    du, dwg, dwu = pl.pallas_call(
        kern, name=name, grid=(ncb, nt),
        in_specs=_halo_specs(2, None, T, FFN_TC, 0, L) + _halo_specs(2, None, T, FFN_TC, ncb, L)
        + [wspec, pl.BlockSpec((8, FFN_TC), lambda j, i: (0, j + ncb))] + _halo_specs(2, None, T, FFN_TC, 0, L),
        out_specs=[pl.BlockSpec((2, T, FFN_TC), lambda j, i: (0, i, j)), wspec, wspec],
        out_shape=[jax.ShapeDtypeStruct((2, L, D_FF), F32), jax.ShapeDtypeStruct((8, D_FF), F32),
                   jax.ShapeDtypeStruct((8, D_FF), F32)],
        compiler_params=_cparams(("arbitrary", "arbitrary")),
    )(u, u, u, u, u, u, w8, w8, da, da, da)
    return du, jnp.concatenate([dwg, dwu], axis=1)


def _masks(Q, rev):
    ri = lax.broadcasted_iota(jnp.int32, (Q, Q), 0)
    ci = lax.broadcasted_iota(jnp.int32, (Q, Q), 1)
    diff = (ri - ci) * (1 - 2 * rev)
    return diff >= 0, diff > 0


def _lane_pick(v, sel):
    return jnp.sum(v * sel, axis=-1, keepdims=True)


def _head_rows(v_all, Q, rev):
    r = lax.broadcasted_iota(jnp.int32, (HEADS * Q, LANE), 0)
    l = lax.broadcasted_iota(jnp.int32, (HEADS * Q, LANE), 1)
    pick = jnp.zeros((HEADS * Q, LANE), F32)
    for h in range(HEADS):
        pick = jnp.where((r >= h * Q) & (r < (h + 1) * Q) & (l == rev * 4 + h), 1.0, pick)
    return _nt_hi(pick, v_all)


def _ssd_chunk(S, x, B, C, dtraw, alog, dtb, rev, kept=None):
    Q = dtraw.shape[0]
    incl, _ = _masks(Q, rev)
    tri = incl.astype(F32)
    dt = _softplus(dtraw + dtb)
    a_all = dt * (-jnp.exp(alog))
    acum_all = _nn_hi(tri, a_all)
    total_all = jnp.sum(a_all, axis=0, keepdims=True)
    lane = lax.broadcasted_iota(jnp.int32, (1, LANE), 1)
    rows_all = _head_rows(acum_all, Q, rev)
    ys, Sn = [], []
    for h in range(HEADS):
        g = h // 2
        sel = (lane == rev * 4 + h).astype(F32)
        acum = _lane_pick(acum_all, sel)
        dth = _lane_pick(dt, sel)
        tot = _lane_pick(total_all, sel)
        seg = acum - rows_all[h * Q:(h + 1) * Q, :]
        decay = jnp.exp(jnp.where(incl, seg, -1e30))
        xdt = x[h] * dth
        Sh = S[HD * h:HD * (h + 1), :]
        scores = _nt(C[g], B[g]) * decay
        y_diag = _nn(scores, xdt)
        states = _tn(xdt, B[g] * jnp.exp(tot - acum))
        y_off = _nt(C[g], Sh) * jnp.exp(acum)
        ys.append(y_diag + y_off)
        Sn.append(Sh * jnp.exp(tot) + states)
    return ys, jnp.concatenate(Sn, axis=0), []


def _inv_unit_raw(Lm):
    N = Lm.shape[0]
    Q = D_CHUNK
    ri = lax.broadcasted_iota(jnp.int32, (N, N), 0)
    ci = lax.broadcasted_iota(jnp.int32, (N, N), 1)
    X = (ri == ci).astype(F32) - Lm
    P = _raw_dot(Lm, Lm, _NN, False)
    n = 2
    while n < Q:
        X = X + _raw_dot(X, P, _NN, False)
        n *= 2
        if n < Q:
            P = _raw_dot(P, P, _NN, False)
    return X


@jax.custom_vjp
def _inv_unit(Lm, T_saved):
    return _inv_unit_raw(Lm) if T_saved is None else T_saved


def _inv_unit_f(Lm, T_saved):
    T = _inv_unit_raw(Lm) if T_saved is None else T_saved
    return T, T


def _inv_unit_b(T, g):
    return -_raw_dot(_raw_dot(T, g, _TN, False), T, _NT, False), None


_inv_unit.defvjp(_inv_unit_f, _inv_unit_b)


def _delta_chunk(S, q, k, v, braw, araw, alog, dtb, rev, kept=None):
    Q = braw.shape[0]
    N = HEADS * Q
    tri = _masks(Q, rev)[0].astype(F32)
    ri = lax.broadcasted_iota(jnp.int32, (N, N), 0)
    ci = lax.broadcasted_iota(jnp.int32, (N, N), 1)
    sh = int(math.log2(Q))
    same = (ri >> sh) == (ci >> sh)
    diff = (ri - ci) * (1 - 2 * rev)
    incl = same & (diff >= 0)
    strict = same & (diff > 0)
    beta_all = _sigmoid(braw)
    g_all = -jnp.exp(alog) * _softplus(araw + dtb)
    G_all = _nn_hi(tri, g_all)
    Gtot_all = jnp.sum(g_all, axis=0, keepdims=True)
    r = lax.broadcasted_iota(jnp.int32, (N, LANE), 0)
    l = lax.broadcasted_iota(jnp.int32, (N, LANE), 1)
    selm = (l == rev * 4 + (r >> sh)).astype(F32)
    rows4 = lambda a: jnp.concatenate([a] * HEADS, axis=0)
    XG = rows4(G_all) * selm
    G = jnp.sum(XG, axis=-1, keepdims=True)
    bt = jnp.sum(rows4(beta_all) * selm, axis=-1, keepdims=True)
    Gtot = jnp.sum(Gtot_all * selm, axis=-1, keepdims=True)
    decay = jnp.exp(jnp.where(incl, G - _nt_h3(jnp.ones((N, LANE), F32), XG), -1e30))
    qs, ks, vs = (jnp.concatenate(t, axis=0) for t in (q, k, v))
    qn = qs * lax.rsqrt(jnp.sum(qs * qs, axis=-1, keepdims=True) + 1e-6)
    kn = ks * lax.rsqrt(jnp.sum(ks * ks, axis=-1, keepdims=True) + 1e-6)
    qc = qn * (HD ** -0.5)
    kb = kn * bt
    T = _inv_unit(jnp.where(strict, _nt(kb, kn) * decay, 0.0), None if kept is None else kept[0])
    eG = jnp.exp(G)
    u = _nn(T, vs * bt)
    w = _nn(T, kb * eG)
    qk = _nt(qc, kn) * decay
    spread = (lax.broadcasted_iota(jnp.int32, (HD, N), 0)
              == (lax.broadcasted_iota(jnp.int32, (HD, N), 1) & (HD - 1))).astype(F32)
    wide = lambda a: jnp.where(same, _nn(a, spread), 0.0)
    v_new = u - _nn(wide(w), S)
    o = _nn(wide(qc * eG), S) + _nn(qk, v_new)
    S_new = S * jnp.exp(Gtot) + _tn(wide(kn * jnp.exp(Gtot - G)), v_new)
    return [o[Q * h:Q * (h + 1), :] for h in range(HEADS)], S_new, [T]


def _seq_pieces(ref, r0, Q, splits):
    if splits is None:
        return ref[r0:r0 + Q, :]
    return [[ref[r0:r0 + Q, o + w * t:o + w * (t + 1)] for t in range(n)] for o, w, n in splits]


def _store_pieces(ref, r0, Q, splits, vals, extra=None):
    if splits is None:
        ref[r0:r0 + Q, :] = vals
        return
    for g, (o, w, n) in enumerate(splits):
        for t in range(n):
            v = vals[g][t]
            if extra is not None and g == 0:
                v = v + extra[r0:r0 + Q, o + w * t:o + w * (t + 1)]
            ref[r0:r0 + Q, o + w * t:o + w * (t + 1)] = v


def _flat(ins):
    out = []
    for v in ins:
        if isinstance(v, list):
            out.extend(v)
        else:
            out.append(v)
    return out


def _scan_fwd(name, chunk_fn, seqs, rows, Q, L, CH, kept_shapes=()):
    nc = L // Q
    nb = nc // CH
    ns, nr = len(seqs), len(rows)
    nk = 1 + len(kept_shapes)
    BQ = Q * CH

    def kern(*refs):
        s_refs = (refs[:ns], refs[ns:2 * ns])
        r_refs = refs[2 * ns:2 * ns + nr]
        pos = 2 * ns + nr
        y_refs = refs[pos:pos + 2]
        k_refs = (refs[pos + 2:pos + 2 + nk], refs[pos + 2 + nk:pos + 2 + 2 * nk])
        S_scr = refs[pos + 2 + 2 * nk]
        i = pl.program_id(0)

        @pl.when(i == 0)
        def _():
            S_scr[...] = jnp.zeros(S_scr.shape, F32)

        rws = [r[...] for r in r_refs]
        for d in (0, 1):
            S = S_scr[d]
            for cc in range(CH):
                c = cc if d == 0 else CH - 1 - cc
                k_refs[d][0][c] = S
                ins = [_seq_pieces(r, c * Q, Q, sp) for r, (_, _, _, sp) in zip(s_refs[d], seqs)]
                ys, S, kept = chunk_fn(S, *_flat(ins), *rws, d)
                for r, v in zip(k_refs[d][1:], kept):
                    r[c] = v
                for h in range(HEADS):
                    y_refs[d][c * Q:(c + 1) * Q, HD * h:HD * (h + 1)] = ys[h]
            S_scr[d] = S

    fwd_specs = [pl.BlockSpec((BQ, w), functools.partial(lambda i, cb: (i, cb), cb=cb)) for _, w, cb, _ in seqs]
    rev_specs = [pl.BlockSpec((BQ, w), functools.partial(lambda i, cb: (nb - 1 - i, cb), cb=cb))
                 for _, w, cb, _ in seqs]
    arrs = [a for a, _, _, _ in seqs]
    k_shapes = [(GROUP_W, HD)] + list(kept_shapes)
    res = pl.pallas_call(
        kern, name=name, grid=(nb,),
        in_specs=fwd_specs + rev_specs + [pl.BlockSpec((1, LANE), lambda i: (0, 0)) for _ in rows],
        out_specs=[pl.BlockSpec((BQ, GROUP_W), lambda i: (i, 0)),
                   pl.BlockSpec((BQ, GROUP_W), lambda i: (nb - 1 - i, 0))]
        + [pl.BlockSpec((CH,) + s, lambda i: (i, 0, 0)) for s in k_shapes]
        + [pl.BlockSpec((CH,) + s, lambda i: (nb - 1 - i, 0, 0)) for s in k_shapes],
        out_shape=[jax.ShapeDtypeStruct((L, GROUP_W), F32)] * 2
        + [jax.ShapeDtypeStruct((nc,) + s, F32) for s in k_shapes] * 2,
        scratch_shapes=[pltpu.VMEM((2, GROUP_W, HD), F32)],
        compiler_params=_cparams(("arbitrary",)),
    )(*arrs, *arrs, *rows)
    return res[0], res[1], list(res[2:2 + nk]), list(res[2 + nk:])


def _scan_bwd(name, chunk_fn, seqs, rows, ssaves, dy, extra, Q, L, CH, side=None):
    nc = L // Q
    nb = nc // CH
    BQ = Q * CH
    ns, nr = len(seqs), len(rows)
    nk = len(ssaves[0])
    has_extra = extra is not None

    def kern(*refs):
        s_refs = (refs[:ns], refs[ns:2 * ns])
        pos = 2 * ns
        r_refs = refs[pos:pos + nr]
        pos += nr
        k_refs = (refs[pos:pos + nk], refs[pos + nk:pos + 2 * nk])
        pos += 2 * nk
        dy_refs = refs[pos:pos + 2]
        pos += 2
        ex_ref = refs[pos] if has_extra else None
        pos += 1 if has_extra else 0
        ds_refs = (refs[pos:pos + ns], refs[pos + ns:pos + 2 * ns])
        pos += 2 * ns
        dr_refs = refs[pos:pos + nr]
        dS_scr = refs[pos + nr]
        i = pl.program_id(0)

        @pl.when(i == 0)
        def _():
            dS_scr[...] = jnp.zeros(dS_scr.shape, F32)
            for r in dr_refs:
                r[...] = jnp.zeros(r.shape, F32)

        rws = [r[...] for r in r_refs]
        dr_acc = [jnp.zeros((1, LANE), F32) for _ in rows]
        for d in (0, 1):
            dS = dS_scr[d]
            for cc in range(CH):
                c = CH - 1 - cc if d == 0 else cc
                S = k_refs[d][0][c]
                kept = [r[c] for r in k_refs[d][1:]]
                dys = [dy_refs[d][c * Q:(c + 1) * Q, HD * h:HD * (h + 1)] for h in range(HEADS)]
                ins = [_seq_pieces(r, c * Q, Q, sp) for r, (_, _, _, sp) in zip(s_refs[d], seqs)]
                _, vjp = jax.vjp(
                    functools.partial(
                        lambda S_, ins_, rws_, d_, kept_: chunk_fn(S_, *_flat(ins_), *rws_, d_, kept_)[:2],
                        d_=d, kept_=kept),
                    S, ins, rws)
                dS, dins, drws = vjp((dys, dS))
                for n_, (r, (_, _, _, sp)) in enumerate(zip(ds_refs[d], seqs)):
                    _store_pieces(r, c * Q, Q, sp, dins[n_],
                                  extra=ex_ref if (has_extra and d == 0 and n_ == 0) else None)
                dr_acc = [a + g for a, g in zip(dr_acc, drws)]
            dS_scr[d] = dS
        for r, g in zip(dr_refs, dr_acc):
            r[...] += g

    def blk(shape, rev, cb=0):
        nd = len(shape)
        if rev:
            return pl.BlockSpec(shape, lambda i: (i, cb) + (0,) * (nd - 2))
        return pl.BlockSpec(shape, lambda i: (nb - 1 - i, cb) + (0,) * (nd - 2))

    arrs = [a for a, _, _, _ in seqs]
    in_specs = [blk((BQ, w), False, cb) for _, w, cb, _ in seqs] + [blk((BQ, w), True, cb) for _, w, cb, _ in seqs]
    in_specs += [pl.BlockSpec((1, LANE), lambda i: (0, 0)) for _ in rows]
    in_specs += [blk((CH,) + a.shape[1:], False) for a in ssaves[0]]
    in_specs += [blk((CH,) + a.shape[1:], True) for a in ssaves[1]]
    in_specs += [blk((BQ, GROUP_W), False), blk((BQ, GROUP_W), True)]
    args = arrs + arrs + list(rows) + list(ssaves[0]) + list(ssaves[1]) + [dy, dy]
    if has_extra:
        in_specs.append(blk((BQ, GROUP_W), False))
        args.append(extra)
    kern, s_in, s_out, s_shapes, s_scratch, s_args = _host(kern, len(args), 2 * ns + nr, side, (nb,), 1)
    res = pl.pallas_call(
        kern, name=name, grid=(nb,),
        in_specs=in_specs + s_in,
        out_specs=[blk((BQ, w), False) for _, w, _, _ in seqs] + [blk((BQ, w), True) for _, w, _, _ in seqs]
        + [pl.BlockSpec((1, LANE), lambda i: (0, 0)) for _ in rows] + s_out,
        out_shape=[jax.ShapeDtypeStruct((L, w), F32) for _, w, _, _ in seqs] * 2
        + [jax.ShapeDtypeStruct((1, LANE), F32) for _ in rows] + s_shapes,
        scratch_shapes=[pltpu.VMEM((2, GROUP_W, HD), F32)] + s_scratch,
        compiler_params=_cparams(("arbitrary",)),
    )(*args, *s_args)
    return list(res[:ns]), list(res[ns:2 * ns]), list(res[2 * ns:2 * ns + nr]), list(res[2 * ns + nr:])


def _loss_call(y, tgt, L):
    T = min(256, L)

    def kern(y_ref, t_ref, dy_ref, l_ref):
        i = pl.program_id(0)
        e = y_ref[...] - t_ref[...]
        dy_ref[...] = e * (1.0 / D_MODEL)

        @pl.when(i == 0)
        def _():
            l_ref[...] = jnp.zeros(l_ref.shape, F32)

        part = 0.5 * jnp.sum(jnp.sum(e * e, axis=-1, keepdims=True) * (1.0 / D_MODEL), axis=0, keepdims=True)
        l_ref[...] += jnp.broadcast_to(part, l_ref.shape)

    return pl.pallas_call(
        kern, name="loss_head", grid=(L // T,),
        in_specs=[_spec2(T, D_MODEL), _spec2(T, D_MODEL)],
        out_specs=[_spec2(T, D_MODEL), pl.BlockSpec((8, LANE), lambda i: (0, 0))],
        out_shape=[jax.ShapeDtypeStruct((L, D_MODEL), F32), jax.ShapeDtypeStruct((8, LANE), F32)],
        compiler_params=_cparams(("arbitrary",)),
    )(y, tgt)


_ANY = pl.BlockSpec(memory_space=pl.ANY)


def _coords():
    return lax.axis_index("x"), lax.axis_index("y"), lax.axis_index("c")


class _Copies:
    def __init__(self, ins, out_shapes, copies_fn, n_remote, n_local):
        self.ins, self.out_shapes, self.copies_fn = list(ins), list(out_shapes), copies_fn
        self.n_remote, self.n_local = n_remote, n_local

    def scratch(self):
        return [pltpu.SemaphoreType.DMA((self.n_remote,)), pltpu.SemaphoreType.DMA((self.n_remote,)),
                pltpu.SemaphoreType.DMA((self.n_local,))]

    def _descr(self, in_refs, out_refs, sems):
        send_sems, recv_sems, lsems = sems
        remote, local = self.copies_fn(list(in_refs), list(out_refs))
        assert len(remote) == self.n_remote and len(local) == self.n_local
        mk = lambda k, src, dst, peer: pltpu.make_async_remote_copy(
            src_ref=src, dst_ref=dst, send_sem=send_sems.at[k], recv_sem=recv_sems.at[k], device_id=peer,
            device_id_type=MESH)
        sends = [mk(k, src, dst, peer) for k, (src, dst, _, peer) in enumerate(remote)]
        recvs = [mk(k, src, land, peer) for k, (src, _, land, peer) in enumerate(remote)]
        locs = [pltpu.make_async_copy(src, dst, lsems.at[k]) for k, (src, dst) in enumerate(local)]
        return sends, recvs, locs

    def start(self, in_refs, out_refs, sems):
        sends, _, locs = self._descr(in_refs, out_refs, sems)
        for c in locs + sends:
            c.start()

    def finish(self, in_refs, out_refs, sems):
        sends, recvs, locs = self._descr(in_refs, out_refs, sems)
        for c in recvs:
            c.wait_recv()
        for c in sends:
            c.wait_send()
        for c in locs:
            c.wait()

    def call(self, name):
        ni, no = len(self.ins), len(self.out_shapes)

        def body(*refs):
            self.start(refs[:ni], refs[ni:ni + no], refs[ni + no:])
            self.finish(refs[:ni], refs[ni:ni + no], refs[ni + no:])

        return pl.pallas_call(body, name=name, in_specs=[_ANY] * ni, out_specs=[_ANY] * no,
                              out_shape=self.out_shapes, scratch_shapes=self.scratch())(*self.ins)


def _chip_peers(x, y):
    return [(1 - x, y), (x, 1 - y), (1 - x, 1 - y)]


def _gather_copies(arrs):
    def copies_fn(ins, outs):
        x, y, c = _coords()
        me = 2 * x + y
        remote, local = [], []
        for src, out in zip(ins, outs):
            local.append((src, out.at[me]))
            for px, py in _chip_peers(x, y):
                remote.append((src, out.at[me], out.at[2 * px + py], (px, py, c)))
        return remote, local

    shapes = [jax.ShapeDtypeStruct((4,) + a.shape, a.dtype) for a in arrs]
    return _Copies(arrs, shapes, copies_fn, 3 * len(arrs), len(arrs))


def _scatter_copies(Gs, small):
    nb = len(Gs)

    def copies_fn(ins, outs):
        x, y, c = _coords()
        me = 2 * x + y
        remote, local = [], []
        for g, out in zip(ins[:nb], outs[:nb]):
            local.append((g.at[me], out.at[me]))
            for px, py in _chip_peers(x, y):
                remote.append((g.at[2 * px + py], out.at[me], out.at[2 * px + py], (px, py, c)))
        if small is not None:
            dev = 4 * x + 2 * y + c
            gs, outs_ = ins[nb], outs[nb]
            local.append((gs, outs_.at[dev]))
            for mask in range(1, 8):
                px, py, pc = x ^ (mask >> 2), y ^ ((mask >> 1) & 1), c ^ (mask & 1)
                remote.append((gs, outs_.at[dev], outs_.at[4 * px + 2 * py + pc], (px, py, pc)))
        return remote, local

    ins = list(Gs) + ([small] if small is not None else [])
    shapes = [jax.ShapeDtypeStruct(g.shape, g.dtype) for g in Gs]
    if small is not None:
        shapes.append(jax.ShapeDtypeStruct((8,) + small.shape, small.dtype))
    extra = 1 if small is not None else 0
    return _Copies(ins, shapes, copies_fn, 3 * nb + 7 * extra, nb + extra)


SWAP_STREAMS = 8


def _row_chunks(rows):
    k = SWAP_STREAMS
    if rows % (8 * k) == 0 and rows >= 64 * k:
        return [(q * (rows // k), rows // k) for q in range(k)]
    return [(0, rows)]


def _swap_copies(parts):
    chunks = [_row_chunks(p.shape[0]) for p in parts]
    n = sum(len(ch) for ch in chunks)

    def copies_fn(ins, outs):
        x, y, c = _coords()
        remote, local = [], []
        for src, out, ch in zip(ins, outs, chunks):
            for r0, nr in ch:
                rows = pl.ds(r0, nr)
                local.append((src.at[rows], out.at[c, rows]))
                remote.append((src.at[rows], out.at[c, rows], out.at[1 - c, rows], (x, y, 1 - c)))
        return remote, local

    shapes = [jax.ShapeDtypeStruct((2,) + p.shape, p.dtype) for p in parts]
    return _Copies(parts, shapes, copies_fn, n, n)


def _merge_copies(sets):
    ins = [a for s in sets for a in s.ins]
    shapes = [o for s in sets for o in s.out_shapes]

    def copies_fn(in_refs, out_refs):
        remote, local, pi, po = [], [], 0, 0
        for s in sets:
            r, l = s.copies_fn(in_refs[pi:pi + len(s.ins)], out_refs[po:po + len(s.out_shapes)])
            remote += r
            local += l
            pi += len(s.ins)
            po += len(s.out_shapes)
        return remote, local

    return _Copies(ins, shapes, copies_fn, sum(s.n_remote for s in sets), sum(s.n_local for s in sets))


def _row_tile(rows):
    best = rows
    for d in range(8, min(rows, 256) + 1, 8):
        if rows % d == 0:
            best = d
    return best


def _sum_slots(name, recv):
    n, R, W = recv.shape
    tr = _row_tile(R)

    def kern(r_ref, o_ref):
        acc = r_ref[0]
        for s in range(1, n):
            acc = acc + r_ref[s]
        o_ref[...] = acc

    return pl.pallas_call(
        kern, name=name, grid=(R // tr,),
        in_specs=[pl.BlockSpec((n, tr, W), lambda i: (0, i, 0))],
        out_specs=pl.BlockSpec((tr, W), lambda i: (i, 0)),
        out_shape=jax.ShapeDtypeStruct((R, W), F32),
        compiler_params=_cparams(("arbitrary",)),
    )(recv)


def _adamw_call(name, slots, w, m, v):
    nl = len(slots)
    n, R, W = slots[0].shape
    tr = _row_tile(R)
    nr = R // tr

    def kern(*refs):
        s_refs = refs[:nl]
        w_ref, m_ref, v_ref, g_ref, d_ref, nm_ref, nv_ref = refs[nl:]
        layer = pl.program_id(0)
        g = s_refs[0][0]
        for s in range(1, n):
            g = g + s_refs[0][s]
        for l in range(1, nl):
            gl = s_refs[l][0]
            for s in range(1, n):
                gl = gl + s_refs[l][s]
            g = jnp.where(layer == l, gl, g)
        m_ = ADAM_B1 * m_ref[...] + (1.0 - ADAM_B1) * g
        v_ = ADAM_B2 * v_ref[...] + (1.0 - ADAM_B2) * (g * g)
        m_hat = m_ / (1.0 - ADAM_B1 ** ADAM_STEP)
        v_hat = v_ / (1.0 - ADAM_B2 ** ADAM_STEP)
        g_ref[...] = g
        d_ref[...] = -ADAM_LR * (m_hat / (jnp.sqrt(v_hat) + ADAM_EPS) + ADAM_WD * w_ref[...])
        nm_ref[...] = m_
        nv_ref[...] = v_

    blk = pl.BlockSpec((None, tr, W), lambda l, i: (l, i, 0))
    return pl.pallas_call(
        kern, name=name, grid=(nl, nr),
        in_specs=[pl.BlockSpec((n, tr, W), lambda l, i: (0, i, 0)) for _ in slots] + [blk, blk, blk],
        out_specs=[blk, blk, blk, blk],
        out_shape=[jax.ShapeDtypeStruct((nl, R, W), F32)] * 4,
        compiler_params=_cparams(("arbitrary", "arbitrary")),
    )(*slots, w, m, v)


def _pack(arrs, width, row_mult):
    flat = jnp.concatenate([a.reshape(-1) for a in arrs])
    n = flat.shape[0]
    rows = -(-n // width)
    rows = -(-rows // row_mult) * row_mult
    return jnp.pad(flat, (0, rows * width - n)).reshape(rows, width)


def _unpack(buf, shapes):
    flat = buf.reshape(-1)
    out, pos = [], 0
    for s in shapes:
        n = int(np.prod(s))
        out.append(flat[pos:pos + n].reshape(s))
        pos += n
    return out


def _rope_angles(L, rot_dim):
    rows = L // GRID_W
    row = jnp.repeat(jnp.arange(rows), GRID_W).astype(F32)
    col = jnp.tile(jnp.arange(GRID_W), rows).astype(F32)
    sec = rot_dim // 2
    inv_freq = ROPE_BASE ** (-jnp.arange(0, sec, 2, dtype=F32) / sec)
    ang_r = row[:, None] * inv_freq
    ang_c = col[:, None] * inv_freq
    ang = jnp.concatenate([ang_r, ang_r, ang_c, ang_c], axis=-1)
    return jnp.cos(ang), jnp.sin(ang)


def _rot_matrix(r):
    R = np.zeros((r, r), np.float32)
    q = r // 4
    for s in range(2):
        for t in range(q):
            lo = s * (r // 2) + t
            hi = lo + q
            R[hi, lo] = -1.0
            R[lo, hi] = 1.0
    return R


def _place_tables(L, cos, sin, width, offsets):
    r = cos.shape[1]
    Rm = np.zeros((width, width), np.float32)
    R = _rot_matrix(r)
    cs, ss, pos = [], [], 0
    for o in list(offsets) + [width]:
        if o > pos:
            cs.append(jnp.ones((L, o - pos), F32))
            ss.append(jnp.zeros((L, o - pos), F32))
        if o < width:
            cs.append(cos)
            ss.append(sin)
            Rm[o:o + r, o:o + r] = R
        pos = o + r
    return jnp.concatenate(cs, axis=1), jnp.concatenate(ss, axis=1), jnp.asarray(Rm)


def _head_mean_matrix(width, stride, n):
    M = np.zeros((width, width), np.float32)
    for o in range(0, width, stride):
        M[o:o + n, o:o + n] = 1.0 / n
    return jnp.asarray(M)


def _pad_heads(w, n_heads, real, padded, axis):
    parts = jnp.split(w, n_heads, axis=axis)
    padw = [(0, 0)] * w.ndim
    padw[axis] = (0, padded - real)
    return jnp.concatenate([jnp.pad(p, padw) for p in parts], axis=axis)


def _row128(v):
    v = v.reshape(1, -1)
    return jnp.pad(v, ((0, 0), (0, LANE - v.shape[1])))


def _conv_w8(w, b):
    C = w.shape[1]
    rows = [w, jnp.zeros((1, C), F32) if b is None else b.reshape(1, C), jnp.zeros((4, C), F32)]
    return jnp.concatenate(rows, axis=0)


def _build_layer(W):
    w_in = W['w_in']
    o = 0
    cols = {}
    for name, n in [('a_cq', A_Q_LORA), ('a_ckv', A_KV_LORA), ('a_kr', A_ROPE), ('b_q', 256), ('b_k', 128),
                    ('b_v', 128), ('c_z', 256), ('c_xbc', 512), ('c_dt', 8), ('d_qkv', 768), ('d_z', 256),
                    ('d_b', 8), ('d_a', 8)]:
        cols[name] = w_in[:, o:o + n]
        o += n
    padc = lambda a, lo, width: jnp.pad(a, ((0, 0), (lo, width - lo - a.shape[1])))
    pieces = {
        'b_q': _pad_heads(cols['b_q'], 4, HD, LANE, 1), 'c_xbc': cols['c_xbc'], 'a_cq': padc(cols['a_cq'], 0, 256),
        'b_k': _pad_heads(cols['b_k'], 2, HD, LANE, 1), 'd_qkv': cols['d_qkv'],
        'b_v': _pad_heads(cols['b_v'], 2, HD, LANE, 1), 'c_z': cols['c_z'], 'd_z': cols['d_z'],
        'a_ckv': cols['a_ckv'], 'a_kr': padc(cols['a_kr'], A_NOPE, LANE), 'c_dt': padc(cols['c_dt'], 0, LANE),
        'd_b': padc(cols['d_b'], 0, LANE), 'd_a': padc(cols['d_a'], 0, LANE),
        'pad': jnp.zeros((D_MODEL, LANE), w_in.dtype)}
    out = {'w_in': jnp.concatenate([pieces[n] for n, _, _ in P_LAYOUT], axis=1)}
    out['a_q_norm'] = padc(W['a_q_norm'].reshape(1, -1), 0, 256)
    wuq = jnp.pad(W['a_w_uq'], ((0, 256 - A_Q_LORA), (0, 0)))
    out['a_w_uq'] = _pad_heads(wuq, 4, A_NOPE + A_ROPE, LANE, 1)
    out['a_kv_norm'] = W['a_kv_norm'].reshape(1, -1)
    ukv = W['a_w_ukv'].reshape(A_KV_LORA, HEADS, 2, HD)
    out['a_w_uk'] = _pad_heads(ukv[:, :, 0, :].reshape(A_KV_LORA, 256), 4, HD, LANE, 1)
    out['a_w_uv'] = _pad_heads(ukv[:, :, 1, :].reshape(A_KV_LORA, 256), 4, HD, LANE, 1)
    out['a_out_norm'] = _pad_heads(W['a_out_norm'].reshape(1, -1), 4, HD, LANE, 1)
    out['b_q_norm'] = _pad_heads(jnp.tile(W['b_q_norm'].reshape(1, -1), (1, 4)), 4, HD, LANE, 1)
    out['b_k_norm'] = _pad_heads(jnp.tile(W['b_k_norm'].reshape(1, -1), (1, 2)), 2, HD, LANE, 1)
    out['b_out_norm'] = _pad_heads(W['b_out_norm'].reshape(1, -1), 4, HD, LANE, 1)
    out['c_conv'] = _conv_w8(W['c_conv_w'], W['c_conv_b'])
    out['c_a_log'] = _row128(W['c_a_log'])
    out['c_dt_bias'] = _row128(W['c_dt_bias'])
    out['c_d_skip'] = jnp.repeat(W['c_d_skip'], HD).reshape(1, -1)
    out['c_out_norm'] = W['c_out_norm'].reshape(1, -1)
    out['d_conv'] = _conv_w8(W['d_conv_w'], None)
    out['d_a_log'] = _row128(W['d_a_log'])
    out['d_dt_bias'] = _row128(W['d_dt_bias'])
    out['d_out_norm'] = jnp.tile(W['d_out_norm'].reshape(1, -1), (1, 4))
    wo = W['w_out']
    out['w_out'] = jnp.concatenate([_pad_heads(wo[0:256], 4, HD, LANE, 0), _pad_heads(wo[256:512], 4, HD, LANE, 0),
                                    wo[512:1024]], axis=0)
    for n in ['pre_mix_norm', 'post_mix_norm', 'pre_ffn_norm', 'post_ffn_norm']:
        out[n] = W[n].reshape(1, -1)
    out['f_w_in'] = W['f_w_in']
    out['f_conv'] = _conv_w8(W['f_conv_w'], W['f_conv_b'])
    out['f_w_out'] = W['f_w_out']
    return out


def _fn_norm_in(a, p):
    return [_rms(a[0], p[0])]


def _fn_resid_norm2(a, p):
    x1 = a[0] + _rms(a[1], p[0])
    return [x1, _rms(x1, p[1])]


def _fn_resid_norm(a, p):
    return [a[0] + _rms(a[1], p[0])]


def _fn_a_prep(a, p):
    cq, ckv, kr, cosq, sinq, cosk, sink = a
    q_norm, w_uq, kv_norm, w_uk, w_uv, rq, rk = p
    q = _nn(_rms(cq, q_norm, A_Q_LORA), w_uq)
    q = q * cosq + _nn_h3(q, rq) * sinq
    kvn = _rms(ckv, kv_norm)
    kr_r = kr * cosk + _nn_h3(kr, rk) * sink
    kk = _nn(kvn, w_uk) + jnp.concatenate([kr_r] * HEADS, axis=1)
    return [q, kk, _nn(kvn, w_uv)]


def _fn_b_prep(a, p):
    q, k, v, cosq, sinq, cosk, sink = a
    q_norm, k_norm, mq, mk, rq, rk = p
    qn = q * lax.rsqrt(_nn_h3(q * q, mq) + EPS) * q_norm
    kn = k * lax.rsqrt(_nn_h3(k * k, mk) + EPS) * k_norm
    return [qn * cosq + _nn_h3(qn, rq) * sinq, kn * cosk + _nn_h3(kn, rk) * sink, v]


def _fn_mixer_post(a, p):
    oa, ob, yc0, yc1, xs, zc, od0, od1, zd = a
    a_norm, b_norm, dskip, c_norm, d_norm, m64 = p
    oc = _rms((yc0 + yc1 + xs * dskip) * _silu(zc), c_norm)
    od = od0 + od1
    odn = od * lax.rsqrt(_nn_h3(od * od, m64) + EPS) * d_norm * _silu(zd)
    return [jnp.concatenate([_rms(oa, a_norm, GROUP_W), _rms(ob, b_norm, GROUP_W), oc, odn], axis=1)]


def _fn_assemble(a, p):
    (dbq, dxbc, dcq, dbk, dqkv, dbv, dzc, dzd, dckv, dkr, ddt0, ddt1, db0, db1, da0, da1) = a
    return [jnp.concatenate([dbq, dxbc, dcq, dbk, dqkv, dbv, dzc, dzd, dckv, dkr, ddt0 + ddt1, db0 + db1,
                             da0 + da1, jnp.zeros_like(dckv)], axis=1)]


def _pspec(T, name):
    off, w = P_OFF[name]
    return _spec2(T, w, off // w)


def _layer_fwd(l, x, h, K, tabs, L, T, next_norm, side=None):
    n = f"l{l}_"
    sv = {'x': x, 'h': h}
    p = _mm(n + "in_proj", h, K['w_in'].astype(BF16), 'nn', F32, 512, 768, 1024)
    sv['p'] = p
    a_acts = [(p, _pspec(T, 'a_cq')), (p, _pspec(T, 'a_ckv')), (p, _pspec(T, 'a_kr')),
              (tabs['a_cq'], _spec2(T, 512)), (tabs['a_sq'], _spec2(T, 512)),
              (tabs['a_ck'], _spec2(T, LANE)), (tabs['a_sk'], _spec2(T, LANE))]
    a_pars = [K['a_q_norm'], K['a_w_uq'], K['a_kv_norm'], K['a_w_uk'], K['a_w_uv'], tabs['a_rq'], tabs['a_rk']]
    qa, ka, va = _tw_fwd(n + "a_prep", _fn_a_prep, a_acts, a_pars, [(512, BF16)] * 3, L, T)
    oa, lse_a, sv['side'] = _flash_fwd(n + "a_attn", qa, ka, va, HEADS, 1, (A_NOPE + A_ROPE) ** -0.5, L, side)
    sv.update(a_acts=a_acts, a_pars=a_pars, qa=qa, ka=ka, va=va, oa=oa, lse_a=lse_a)
    b_acts = [(p, _pspec(T, 'b_q')), (p, _pspec(T, 'b_k')), (p, _pspec(T, 'b_v')),
              (tabs['b_cq'], _spec2(T, 512)), (tabs['b_sq'], _spec2(T, 512)),
              (tabs['b_ck'], _spec2(T, 256)), (tabs['b_sk'], _spec2(T, 256))]
    b_pars = [K['b_q_norm'], K['b_k_norm'], tabs['b_mq'], tabs['b_mk'], tabs['b_rq'], tabs['b_rk']]
    qb, kb, vb = _tw_fwd(n + "b_prep", _fn_b_prep, b_acts, b_pars, [(512, BF16), (256, BF16), (256, BF16)], L, T)
    ob, lse_b, _ = _flash_fwd(n + "b_attn", qb, kb, vb, HEADS, 2, HD ** -0.5, L)
    sv.update(b_acts=b_acts, b_pars=b_pars, qb=qb, kb=kb, vb=vb, ob=ob, lse_b=lse_b)
    xbc = _conv_fwd(n + "c_conv", p, P_OFF['c_xbc'][0], C_XBC, K['c_conv'], True, L, 512)
    c_seqs = [(xbc, C_XBC, 0, [(0, HD, 4), (256, HD, 2), (384, HD, 2)]),
              (p, LANE, P_OFF['c_dt'][0] // LANE, None)]
    c_rows = [K['c_a_log'], K['c_dt_bias']]
    yc0, yc1, sc0, sc1 = _scan_fwd(n + "c_ssd", _ssd_chunk, c_seqs, c_rows, C_CHUNK, L, C_PER_STEP)
    sv.update(xbc=xbc, c_seqs=c_seqs, c_rows=c_rows, sc=(sc0, sc1))
    qkv = _conv_fwd(n + "d_conv", p, P_OFF['d_qkv'][0], D_QKV, K['d_conv'], True, L, 768)
    d_seqs = [(qkv, D_QKV, 0, [(0, HD, 4), (256, HD, 4), (512, HD, 4)]),
              (p, LANE, P_OFF['d_b'][0] // LANE, None), (p, LANE, P_OFF['d_a'][0] // LANE, None)]
    d_rows = [K['d_a_log'], K['d_dt_bias']]
    od0, od1, sd0, sd1 = _scan_fwd(n + "d_delta", _delta_chunk, d_seqs, d_rows, D_CHUNK, L, D_PER_STEP,
                                   [(HEADS * D_CHUNK, HEADS * D_CHUNK)])
    sv.update(qkv=qkv, d_seqs=d_seqs, d_rows=d_rows, sd=(sd0, sd1))
    m_acts = [(oa, _spec2(T, 512)), (ob, _spec2(T, 512)), (yc0, _spec2(T, 256)), (yc1, _spec2(T, 256)),
              (xbc, _spec2(T, 256, 0)), (p, _pspec(T, 'c_z')), (od0, _spec2(T, 256)), (od1, _spec2(T, 256)),
              (p, _pspec(T, 'd_z'))]
    m_pars = [K['a_out_norm'], K['b_out_norm'], K['c_d_skip'], K['c_out_norm'], K['d_out_norm'], tabs['m64']]
    (o,) = _tw_fwd(n + "mixer_post", _fn_mixer_post, m_acts, m_pars, [(O_COLS, BF16)], L, T)
    f1 = _mm(n + "out_proj", o, K['w_out'].astype(BF16), 'nn', F32, 512, 1024, 768)
    r1_pars = [K['post_mix_norm'], K['pre_ffn_norm']]
    x1, h2 = _tw_fwd(n + "resid_mix", _fn_resid_norm2, [(x, _spec2(T, D_MODEL)), (f1, _spec2(T, D_MODEL))], r1_pars,
                     [(D_MODEL, F32), (D_MODEL, BF16)], L, T)
    sv.update(m_acts=m_acts, m_pars=m_pars, o=o, f1=f1, r1_pars=r1_pars, x1=x1, h2=h2)
    u = _mm(n + "ffn_in", h2, K['f_w_in'].astype(BF16), 'nn', F32, 512, 512, 1024)
    act = _ffn_gate_fwd(n + "ffn_gate", u, K['f_conv'], L)
    f2 = _mm(n + "ffn_out", act, K['f_w_out'].astype(BF16), 'nn', F32, 512, 1024, 1408)
    sv.update(u=u, act=act, f2=f2)
    xf = [(x1, _spec2(T, D_MODEL)), (f2, _spec2(T, D_MODEL))]
    if next_norm is None:
        (x2,) = _tw_fwd(n + "resid_ffn", _fn_resid_norm, xf, [K['post_ffn_norm']], [(D_MODEL, F32)], L, T)
        hn = None
    else:
        x2, hn = _tw_fwd(n + "resid_ffn", _fn_resid_norm2, xf, [K['post_ffn_norm'], next_norm],
                         [(D_MODEL, F32), (D_MODEL, BF16)], L, T)
    return x2, hn, sv


def _layer_bwd(l, dx2, dhn, K, sv, tabs, L, T, next_norm, hosts=None):
    n = f"l{l}b_"
    dK = {}
    hosts = hosts or {}
    got = {}
    side = lambda name: hosts[name](dK, got) if name in hosts else None
    s2 = lambda w, cb=0: _spec2(T, w, cb)
    xf = [(sv['x1'], s2(D_MODEL)), (sv['f2'], s2(D_MODEL))]
    if next_norm is None:
        (dx1a, df2), (dK['post_ffn_norm'],) = _tw_bwd(n + "resid_ffn", _fn_resid_norm, xf, [K['post_ffn_norm']],
                                                      [(dx2, s2(D_MODEL))], L, T, [True, True], [True])
        dnext = None
    else:
        (dx1a, df2), (dK['post_ffn_norm'], dnext) = _tw_bwd(
            n + "resid_ffn", _fn_resid_norm2, xf, [K['post_ffn_norm'], next_norm],
            [(dx2, s2(D_MODEL)), (dhn, s2(D_MODEL))], L, T, [True, True], [True, True])
    dact = _mm(n + "ffn_out_dx", df2, K['f_w_out'].astype(BF16), 'nt', F32, 512, 1408, 1024)
    dK['f_w_out'] = _mm(n + "ffn_out_dw", sv['act'], df2, 'tn', F32, 1408, 1024, 512)
    du, dK['f_conv'] = _ffn_gate_bwd(n + "ffn_gate", sv['u'], K['f_conv'], dact, L)
    dh2 = _mm(n + "ffn_in_dx", du, K['f_w_in'].astype(BF16), 'nt', F32, 512, 1024, 1408)
    dK['f_w_in'] = _mm(n + "ffn_in_dw", sv['h2'], du, 'tn', F32, 512, 1408, 512)
    (dxa, df1), (dK['post_mix_norm'], dK['pre_ffn_norm']) = _tw_bwd(
        n + "resid_mix", _fn_resid_norm2, [(sv['x'], s2(D_MODEL)), (sv['f1'], s2(D_MODEL))], sv['r1_pars'],
        [(dx1a, s2(D_MODEL)), (dh2, s2(D_MODEL))], L, T, [True, True], [True, True])
    do = _mm(n + "out_proj_dx", df1, K['w_out'].astype(BF16), 'nt', F32, 512, 768, 1024)
    dK['w_out'] = _mm(n + "out_proj_dw", sv['o'], df1, 'tn', F32, 768, 1024, 512)
    (doa, dob, dyc0, _, dxs_skip, dzc, dod0, _, dzd), mp = _tw_bwd(
        n + "mixer_post", _fn_mixer_post, sv['m_acts'], sv['m_pars'], [(do, s2(O_COLS))], L, T,
        [True] * 9, [True] * 5 + [False])
    dK['a_out_norm'], dK['b_out_norm'], dK['c_d_skip'], dK['c_out_norm'], dK['d_out_norm'] = mp
    (dqkv0, db0, da0), (dqkv1, db1, da1), (dK['d_a_log'], dK['d_dt_bias']), got['d_delta'] = _scan_bwd(
        n + "d_delta", _delta_chunk, sv['d_seqs'], sv['d_rows'], sv['sd'], dod0, None, D_CHUNK, L, D_PER_STEP,
        side('d_delta'))
    dqkv, dK['d_conv'] = _conv_bwd(n + "d_conv", sv['p'], P_OFF['d_qkv'][0], D_QKV, K['d_conv'], True,
                                   [(dqkv0, None), (dqkv1, None)], L, 768)
    (dxbc0, ddt0), (dxbc1, ddt1), (dK['c_a_log'], dK['c_dt_bias']), _ = _scan_bwd(
        n + "c_ssd", _ssd_chunk, sv['c_seqs'], sv['c_rows'], sv['sc'], dyc0, dxs_skip, C_CHUNK, L, C_PER_STEP)
    dxbc, dK['c_conv'] = _conv_bwd(n + "c_conv", sv['p'], P_OFF['c_xbc'][0], C_XBC, K['c_conv'], True,
                                   [(dxbc0, None), (dxbc1, None)], L, 512)
    dqb, dkb, dvb, got['b_attn'] = _flash_bwd(n + "b_attn", sv['qb'], sv['kb'], sv['vb'], sv['ob'], sv['lse_b'],
                                              dob, HEADS, 2, HD ** -0.5, L, side('b_attn'))
    (dbq, dbk, dbv), (dK['b_q_norm'], dK['b_k_norm']) = _tw_bwd(
        n + "b_prep", _fn_b_prep, sv['b_acts'], sv['b_pars'], [(dqb, s2(512)), (dkb, s2(256)), (dvb, s2(256))],
        L, T, [True] * 3 + [False] * 4, [True, True] + [False] * 4)
    dqa, dka, dva, got['a_attn'] = _flash_bwd(n + "a_attn", sv['qa'], sv['ka'], sv['va'], sv['oa'], sv['lse_a'],
                                              doa, HEADS, 1, (A_NOPE + A_ROPE) ** -0.5, L, side('a_attn'))
    (dcq, dckv, dkr), ap = _tw_bwd(
        n + "a_prep", _fn_a_prep, sv['a_acts'], sv['a_pars'], [(dqa, s2(512)), (dka, s2(512)), (dva, s2(512))],
        L, T, [True] * 3 + [False] * 4, [True] * 5 + [False] * 2)
    dK['a_q_norm'], dK['a_w_uq'], dK['a_kv_norm'], dK['a_w_uk'], dK['a_w_uv'] = ap
    pieces = [(dbq, s2(512)), (dxbc, s2(512)), (dcq, s2(256)), (dbk, s2(256)), (dqkv, s2(768)), (dbv, s2(256)),
              (dzc, s2(256)), (dzd, s2(256)), (dckv, s2(LANE)), (dkr, s2(LANE)),
              (ddt0, s2(LANE)), (ddt1, s2(LANE)), (db0, s2(LANE)), (db1, s2(LANE)), (da0, s2(LANE)),
              (da1, s2(LANE))]
    (dp,) = _tw_fwd(n + "assemble_dp", _fn_assemble, pieces, [], [(P_COLS, BF16)], L, T)
    dh = _mm(n + "in_proj_dx", dp, K['w_in'].astype(BF16), 'nt', F32, 512, 1024, 768)
    dK['w_in'] = _mm(n + "in_proj_dw", sv['h'], dp, 'tn', F32, 512, 1280, 512)
    return dxa, dh, dK, dnext, got


def _tables(L):
    ca, sa = _rope_angles(L, A_ROPE)
    cb, sb = _rope_angles(L, HD)
    t = {}
    t['a_cq'], t['a_sq'], t['a_rq'] = _place_tables(L, ca, sa, 512, [LANE * h + A_NOPE for h in range(4)])
    t['a_ck'], t['a_sk'], t['a_rk'] = _place_tables(L, ca, sa, LANE, [A_NOPE])
    t['b_cq'], t['b_sq'], t['b_rq'] = _place_tables(L, cb, sb, 512, [LANE * h for h in range(4)])
    t['b_ck'], t['b_sk'], t['b_rk'] = _place_tables(L, cb, sb, 256, [LANE * h for h in range(2)])
    t['b_mq'] = _head_mean_matrix(512, LANE, HD)
    t['b_mk'] = _head_mean_matrix(256, LANE, HD)
    t['m64'] = _head_mean_matrix(256, HD, HD)
    return t


def kernel(x, pre_mix_norm, w_in, a_q_norm, a_w_uq, a_kv_norm, a_w_ukv, a_out_norm, b_q_norm, b_k_norm, b_out_norm, c_conv_w, c_conv_b, c_a_log, c_dt_bias, c_d_skip, c_out_norm, d_conv_w, d_a_log, d_dt_bias, d_out_norm, w_out, post_mix_norm, pre_ffn_norm, f_w_in, f_conv_w, f_conv_b, f_w_out, post_ffn_norm, loss_target, m_pre_mix_norm, m_w_in, m_a_q_norm, m_a_w_uq, m_a_kv_norm, m_a_w_ukv, m_a_out_norm, m_b_q_norm, m_b_k_norm, m_b_out_norm, m_c_conv_w, m_c_conv_b, m_c_a_log, m_c_dt_bias, m_c_d_skip, m_c_out_norm, m_d_conv_w, m_d_a_log, m_d_dt_bias, m_d_out_norm, m_w_out, m_post_mix_norm, m_pre_ffn_norm, m_f_w_in, m_f_conv_w, m_f_conv_b, m_f_w_out, m_post_ffn_norm, v_pre_mix_norm, v_w_in, v_a_q_norm, v_a_w_uq, v_a_kv_norm, v_a_w_ukv, v_a_out_norm, v_b_q_norm, v_b_k_norm, v_b_out_norm, v_c_conv_w, v_c_conv_b, v_c_a_log, v_c_dt_bias, v_c_d_skip, v_c_out_norm, v_d_conv_w, v_d_a_log, v_d_dt_bias, v_d_out_norm, v_w_out, v_post_mix_norm, v_pre_ffn_norm, v_f_w_in, v_f_conv_w, v_f_conv_b, v_f_w_out, v_post_ffn_norm):
    loc = locals()
    Wl = {n: loc[n] for n in WEIGHTS}
    Ml = {n: loc['m_' + n] for n in WEIGHTS}
    Vl = {n: loc['v_' + n] for n in WEIGHTS}
    L = x.shape[1]
    T = min(256, L)
    x0 = x.reshape(L, D_MODEL)
    tgt = loss_target.reshape(L, D_MODEL)

    def shards(l):
        return [Wl[n][l].astype(BF16) if n in MXU_WEIGHTS else Wl[n][l] for n in SHARDED]

    def layer_weights(l, gathered):
        W = {n: Wl[n][l] for n in SMALL}
        for n, g in zip(SHARDED, gathered):
            W[n] = jnp.concatenate([g[j] for j in range(4)], axis=SHARD_AXIS[n] - 1)
        return W

    def chip_blocks(g, n):
        return jnp.stack(jnp.split(g, 4, axis=SHARD_AXIS[n] - 1))

    tabs = _tables(L)
    norm_in = [Wl['pre_mix_norm'][l].reshape(1, -1) for l in range(DEPTH)]
    def layer_shape(n):
        s = list(Wl[n].shape[1:])
        if n in SHARD_AXIS:
            s[SHARD_AXIS[n] - 1] *= 4
        return tuple(s)

    unbuild = jax.vjp(_build_layer, {n: jnp.zeros(layer_shape(n), F32) for n in WEIGHTS})[1]

    (h,) = _tw_fwd("l0_norm_in", _fn_norm_in, [(x0, _spec2(T, D_MODEL))], [norm_in[0]], [(D_MODEL, BF16)], L, T)
    gathered = _gather_copies(shards(0)).call("gather_l0")
    xs, saves, Ks = x0, [], []
    for l in range(DEPTH):
        Ks.append(_build_layer(layer_weights(l, gathered)))
        last = l + 1 == DEPTH
        xs, h, sv = _layer_fwd(l, xs, h, Ks[l], tabs, L, T, None if last else norm_in[l + 1],
                               None if last else _gather_copies(shards(l + 1)))
        gathered = sv['side']
        saves.append(sv)
    dy, loss_acc = _loss_call(xs, tgt, L)
    loss = lax.psum(loss_acc[0, 0], ("x", "y", "c"))

    ffn = ['f_w_in', 'f_conv_w', 'f_w_out']
    rest = [n for n in SHARDED if n not in ffn]

    def ffn_side(dK):
        g = {'f_w_in': dK['f_w_in'], 'f_conv_w': dK['f_conv'][0:3], 'f_w_out': dK['f_w_out']}
        return _scatter_copies([chip_blocks(g[n], n) for n in ffn], None)

    def rest_blocks(dK):
        full = dict(dK)
        full.setdefault('pre_mix_norm', jnp.zeros((1, D_MODEL), F32))
        (g,) = unbuild(full)
        return [chip_blocks(g[n], n) for n in rest]

    def chip_sums(l, names, recvs):
        return [_sum_slots(f"sum_{n}_{l}", r.reshape(4, -1, r.shape[-1])) for n, r in zip(names, recvs)]

    grads = [None] * DEPTH
    pairs = {}
    dx, dhn = dy, None
    for l in reversed(range(DEPTH)):
        last = l + 1 == DEPTH

        def host_scatter(dK, got, up=None if last else grads[l + 1]):
            sets = [ffn_side(dK)] + ([] if up is None else [_scatter_copies(rest_blocks(up), None)])
            return _merge_copies(sets)

        def host_swap(dK, got, l=l, last=last):
            r = got['d_delta']
            parts = chip_sums(l, ffn, r[:len(ffn)]) + ([] if last else chip_sums(l + 1, rest, r[len(ffn):]))
            return _swap_copies(parts)

        dxa, dh, dK, dnext, got = _layer_bwd(l, dx, dhn, Ks[l], saves[l], tabs, L, T,
                                             None if last else norm_in[l + 1],
                                             {'d_delta': host_scatter, 'b_attn': host_swap})
        pairs.update({(l, n): p for n, p in zip(ffn, got['b_attn'])})
        if not last:
            pairs.update({(l + 1, n): p for n, p in zip(rest, got['b_attn'][len(ffn):])})
            grads[l + 1]['pre_mix_norm'] = dnext
        grads[l] = dK
        dx, dhn = dxa, dh
    (dx_in,), (grads[0]['pre_mix_norm'],) = _tw_bwd(
        "l0b_norm_in", _fn_norm_in, [(x0, _spec2(T, D_MODEL))], [norm_in[0]], [(dhn, _spec2(T, D_MODEL))], L, T,
        [True], [True], addto={0: (dx, _spec2(T, D_MODEL))})
    small_shapes = [Wl[n].shape for n in SMALL]
    gfull = [unbuild(grads[l])[0] for l in range(DEPTH)]
    gs = _pack([jnp.stack([gfull[l][n] for l in range(DEPTH)]) for n in SMALL], LANE, 8)
    *got0, recv_small = _scatter_copies(rest_blocks(grads[0]), gs).call("scatter_last")
    pairs.update({(0, n): p for n, p in zip(rest, _swap_copies(chip_sums(0, rest, got0)).call("swap_last"))})

    kinds = ['grad', 'delta', 'new_m', 'new_v']
    res = {}
    for n in SHARDED:
        upd = _adamw_call("adamw_" + n, [pairs[l, n] for l in range(DEPTH)], Wl[n], Ml[n], Vl[n])
        for kind, a in zip(kinds, upd):
            res[kind, n] = a
    small = _adamw_call("adamw_small", [recv_small], *[_pack([W_[n] for n in SMALL], LANE, 8)[None]
                                                       for W_ in (Wl, Ml, Vl)])
    for kind, s in zip(kinds, small):
        for n, a in zip(SMALL, _unpack(s, small_shapes)):
            res[kind, n] = a
    outs = [loss, dx_in.reshape(x.shape)]
    for kind in ['grad', 'delta', 'new_m', 'new_v']:
        outs += [res[kind, n] for n in WEIGHTS]
    return tuple(outs)
```

```python
import functools
import math

import numpy as np
import jax
import jax.numpy as jnp
from jax import lax
from jax.experimental import pallas as pl
from jax.experimental.pallas import tpu as pltpu

F32 = jnp.float32
BF16 = jnp.bfloat16
MESH = pl.DeviceIdType.MESH
VMEM_LIMIT = 48 * 1024 * 1024
LANE = 128

D_MODEL = 1024
DEPTH = 2
GRID_W = 64
ROPE_BASE = 10000.0
EPS = 1e-6
GROUP_W = 256
HEADS = 4
HD = 64
A_NOPE, A_ROPE, A_Q_LORA, A_KV_LORA = 64, 32, 192, 128
A_COLS = A_Q_LORA + A_KV_LORA + A_ROPE
B_COLS = 512
C_XBC = 512
C_COLS = GROUP_W + C_XBC + 8
D_QKV = 768
D_COLS = D_QKV + GROUP_W + 16
IN_COLS = A_COLS + B_COLS + C_COLS + D_COLS
C_CHUNK = 128
D_CHUNK = 64
C_PER_STEP = 1
D_PER_STEP = 2
D_FF = 2816
ADAM_LR, ADAM_B1, ADAM_B2, ADAM_EPS, ADAM_WD, ADAM_STEP = 0.001, 0.9, 0.999, 1e-08, 0.01, 10

WEIGHTS = ['pre_mix_norm', 'w_in', 'a_q_norm', 'a_w_uq', 'a_kv_norm', 'a_w_ukv', 'a_out_norm', 'b_q_norm',
           'b_k_norm', 'b_out_norm', 'c_conv_w', 'c_conv_b', 'c_a_log', 'c_dt_bias', 'c_d_skip', 'c_out_norm',
           'd_conv_w', 'd_a_log', 'd_dt_bias', 'd_out_norm', 'w_out', 'post_mix_norm', 'pre_ffn_norm', 'f_w_in',
           'f_conv_w', 'f_conv_b', 'f_w_out', 'post_ffn_norm']
SHARD_AXIS = {'w_in': 2, 'a_w_uq': 2, 'a_w_ukv': 2, 'c_conv_w': 2, 'd_conv_w': 2, 'w_out': 1, 'f_w_in': 2,
              'f_conv_w': 2, 'f_w_out': 1}
SHARDED = [n for n in WEIGHTS if n in SHARD_AXIS]
SMALL = [n for n in WEIGHTS if n not in SHARD_AXIS]
MXU_WEIGHTS = ('w_in', 'a_w_uq', 'a_w_ukv', 'w_out', 'f_w_in', 'f_w_out')

P_LAYOUT = [('b_q', 0, 512), ('c_xbc', 512, 512), ('a_cq', 1024, 256), ('b_k', 1280, 256), ('d_qkv', 1536, 768),
            ('b_v', 2304, 256), ('c_z', 2560, 256), ('d_z', 2816, 256), ('a_ckv', 3072, 128), ('a_kr', 3200, 128),
            ('c_dt', 3328, 128), ('d_b', 3456, 128), ('d_a', 3584, 128), ('pad', 3712, 128)]
P_OFF = {n: (o, w) for n, o, w in P_LAYOUT}
P_COLS = 3840
O_COLS = 1536


def _cparams(sem):
    return pltpu.CompilerParams(dimension_semantics=sem, vmem_limit_bytes=VMEM_LIMIT)


def _tile(n, target):
    best = None
    for d in range(LANE, min(n, target) + 1, LANE):
        if n % d == 0:
            best = d
    return best if best is not None else n


_NN = ((1,), (0,))
_NT = ((1,), (1,))
_TN = ((0,), (0,))


def _raw_dot(a, b, dims, hi):
    if hi:
        prec = lax.Precision.HIGH if hi == 'high' else lax.Precision.HIGHEST
        return lax.dot_general(a, b, (dims, ((), ())), precision=prec, preferred_element_type=F32)
    return lax.dot_general(a.astype(BF16), b.astype(BF16), (dims, ((), ())), preferred_element_type=F32)


def _make_dots(hi):
    @jax.custom_vjp
    def nn(a, b):
        return _raw_dot(a, b, _NN, hi)

    @jax.custom_vjp
    def nt(a, b):
        return _raw_dot(a, b, _NT, hi)

    @jax.custom_vjp
    def tn(a, b):
        return _raw_dot(a, b, _TN, hi)

    nn.defvjp(lambda a, b: (nn(a, b), (a, b)), lambda r, g: (nt(g, r[1]), tn(r[0], g)))
    nt.defvjp(lambda a, b: (nt(a, b), (a, b)), lambda r, g: (nn(g, r[1]), tn(g, r[0])))
    tn.defvjp(lambda a, b: (tn(a, b), (a, b)), lambda r, g: (nt(r[1], g), nn(r[0], g)))
    return nn, nt, tn


_nn, _nt, _tn = _make_dots(False)
_nn_hi, _nt_hi, _tn_hi = _make_dots(True)
_nn_h3, _nt_h3, _tn_h3 = _make_dots('high')


def _sigmoid(x):
    return 1.0 / (1.0 + jnp.exp(-x))


def _silu(x):
    return x * _sigmoid(x)


def _softplus(x):
    return jnp.maximum(x, 0.0) + jnp.log(1.0 + jnp.exp(-jnp.abs(x)))


def _rms(x, w, n=None):
    n = x.shape[-1] if n is None else n
    ms = jnp.sum(x * x, axis=-1, keepdims=True) * (1.0 / n)
    return x * lax.rsqrt(ms + EPS) * w


def _spec2(T, w, cb=0):
    return pl.BlockSpec((T, w), lambda i: (i, cb))


def _full_spec(a):
    nd = a.ndim
    return pl.BlockSpec(a.shape, lambda i: (0,) * nd)


def _tw_fwd(name, fn, acts, params, outs, L, T):
    na, npar = len(acts), len(params)

    def kern(*refs):
        a = [r[...].astype(F32) for r in refs[:na]]
        p = [r[...].astype(F32) for r in refs[na:na + npar]]
        res = fn(a, p)
        for r, o in zip(refs[na + npar:], res):
            r[...] = o.astype(r.dtype)

    return pl.pallas_call(
        kern, name=name, grid=(L // T,),
        in_specs=[s for _, s in acts] + [_full_spec(p) for p in params],
        out_specs=[_spec2(T, w) for w, _ in outs],
        out_shape=[jax.ShapeDtypeStruct((L, w), dt) for w, dt in outs],
        compiler_params=_cparams(("arbitrary",)),
    )(*[a for a, _ in acts], *params)


def _tw_bwd(name, fn, acts, params, douts, L, T, act_grad, par_grad, addto=None):
    na, npar, nd = len(acts), len(params), len(douts)
    addto = addto or {}
    add_keys = sorted(addto)
    ga = [k for k in range(na) if act_grad[k]]
    gp = [k for k in range(npar) if par_grad[k]]

    def kern(*refs):
        i = pl.program_id(0)
        a = [r[...].astype(F32) for r in refs[:na]]
        p = [r[...].astype(F32) for r in refs[na:na + npar]]
        g = [r[...].astype(F32) for r in refs[na + npar:na + npar + nd]]
        pos = na + npar + nd
        adds = [r[...].astype(F32) for r in refs[pos:pos + len(add_keys)]]
        pos += len(add_keys)
        da_refs = refs[pos:pos + len(ga)]
        dp_refs = refs[pos + len(ga):]

        def f(ad, pd):
            af, pf = list(a), list(p)
            for k, v in zip(ga, ad):
                af[k] = v
            for k, v in zip(gp, pd):
                pf[k] = v
            return fn(af, pf)

        _, vjp = jax.vjp(f, [a[k] for k in ga], [p[k] for k in gp])
        dad, dpd = vjp(list(g))
        for n, (r, d) in enumerate(zip(da_refs, dad)):
            if n in addto:
                d = d + adds[add_keys.index(n)]
            r[...] = d.astype(r.dtype)

        @pl.when(i == 0)
        def _():
            for r in dp_refs:
                r[...] = jnp.zeros(r.shape, F32)

        for r, d in zip(dp_refs, dpd):
            r[...] += d

    def width(spec):
        return spec.block_shape[-1]

    res = pl.pallas_call(
        kern, name=name, grid=(L // T,),
        in_specs=[s for _, s in acts] + [_full_spec(p) for p in params] + [s for _, s in douts]
        + [addto[k][1] for k in add_keys],
        out_specs=[_spec2(T, width(acts[k][1])) for k in ga] + [_full_spec(params[k]) for k in gp],
        out_shape=[jax.ShapeDtypeStruct((L, width(acts[k][1])), F32) for k in ga]
        + [jax.ShapeDtypeStruct(params[k].shape, F32) for k in gp],
        compiler_params=_cparams(("arbitrary",)),
    )(*[a for a, _ in acts], *params, *[a for a, _ in douts], *[addto[k][0] for k in add_keys])
    return list(res[:len(ga)]), list(res[len(ga):])


def _mm(name, a, b, mode, out_dtype, tm, tn, tk):
    halves_a = a.shape[-1] if (a.ndim == 3 and mode == 'nt') else None
    halves_b = b.shape[-1] if (b.ndim == 3 and mode == 'tn') else None
    if mode == 'nn':
        (M, K), N = a.shape, b.shape[1]
    elif mode == 'nt':
        M, K, N = a.shape[-2], (2 * halves_a if halves_a else a.shape[1]), b.shape[0]
    else:
        (K, M), N = a.shape, (2 * halves_b if halves_b else b.shape[1])
    tm = _tile(M, tm)
    tn = _tile(halves_b or N, tn)
    tk = _tile(halves_a or K, tk)
    nk = K // tk
    if mode == 'nn':
        a_spec = pl.BlockSpec((tm, tk), lambda i, j, k: (i, k))
        b_spec = pl.BlockSpec((tk, tn), lambda i, j, k: (k, j))
        dims = _NN
    elif mode == 'nt':
        a_spec = pl.BlockSpec((tm, tk), lambda i, j, k: (i, k))
        if halves_a:
            per = halves_a // tk
            a_spec = pl.BlockSpec((None, tm, tk), lambda i, j, k: (k // per, i, k % per))
        b_spec = pl.BlockSpec((tn, tk), lambda i, j, k: (j, k))
        dims = _NT
    else:
        a_spec = pl.BlockSpec((tk, tm), lambda i, j, k: (k, i))
        b_spec = pl.BlockSpec((tk, tn), lambda i, j, k: (k, j))
        if halves_b:
            per = halves_b // tn
            b_spec = pl.BlockSpec((None, tk, tn), lambda i, j, k: (j // per, k, j % per))
        dims = _TN

    def kern(a_ref, b_ref, o_ref, acc):
        k = pl.program_id(2)

        @pl.when(k == 0)
        def _():
            acc[...] = jnp.zeros(acc.shape, F32)

        acc[...] += lax.dot_general(a_ref[...].astype(BF16), b_ref[...].astype(BF16), (dims, ((), ())),
                                    preferred_element_type=F32)

        @pl.when(k == nk - 1)
        def _():
            o_ref[...] = acc[...].astype(o_ref.dtype)

    return pl.pallas_call(
        kern, name=name, grid=(M // tm, N // tn, nk),
        in_specs=[a_spec, b_spec],
        out_specs=pl.BlockSpec((tm, tn), lambda i, j, k: (i, j)),
        out_shape=jax.ShapeDtypeStruct((M, N), out_dtype),
        scratch_shapes=[pltpu.VMEM((tm, tn), F32)],
        compiler_params=_cparams(("arbitrary", "arbitrary", "arbitrary")),
    )(a, b)


def _host(kern, n_in, n_out, side, grid, n_scratch=0):
    if side is None:
        return kern, [], [], [], [], []
    ni, no = len(side.ins), len(side.out_shapes)

    def hosted(*refs):
        ins, s_in = refs[:n_in], refs[n_in:n_in + ni]
        pos = n_in + ni
        outs, s_out = refs[pos:pos + n_out], refs[pos + n_out:pos + n_out + no]
        pos += n_out + no
        own, sems = refs[pos:pos + n_scratch], refs[pos + n_scratch:]
        ids = [pl.program_id(d) for d in range(len(grid))]
        first = functools.reduce(lambda a, b: a & b, [i == 0 for i in ids])
        last = functools.reduce(lambda a, b: a & b, [i == g - 1 for i, g in zip(ids, grid)])

        @pl.when(first)
        def _():
            side.start(s_in, s_out, sems)

        kern(*ins, *outs, *own)

        @pl.when(last)
        def _():
            side.finish(s_in, s_out, sems)

    return hosted, [_ANY] * ni, [_ANY] * no, side.out_shapes, side.scratch(), side.ins


def _flash_fwd(name, q, k, v, H, rep, scale, L, side=None):
    tq = min(512, L)
    nq = L // tq
    KC = min(2048, L)
    nkc = L // KC
    log2e = 1.0 / math.log(2.0)

    def kern(q_ref, k_ref, v_ref, o_ref, lse_ref):
        qb = q_ref[...]
        m = jnp.full((tq, 1), -1e30, F32)
        l = jnp.zeros((tq, 1), F32)
        acc = jnp.zeros((tq, LANE), F32)
        for c in range(nkc):
            kb = k_ref[c * KC:(c + 1) * KC, :]
            vb = v_ref[c * KC:(c + 1) * KC, :]
            s = lax.dot_general(qb, kb, (_NT, ((), ())), preferred_element_type=F32) * (scale * log2e)
            mn = jnp.maximum(m, jnp.max(s, axis=-1, keepdims=True))
            al = jnp.exp2(m - mn)
            p = jnp.exp2(s - mn)
            l = al * l + jnp.sum(p, axis=-1, keepdims=True)
            acc = al * acc + lax.dot_general(p.astype(BF16), vb, (_NN, ((), ())), preferred_element_type=F32)
            m = mn
        o_ref[...] = acc / l
        lse_ref[...] = m * math.log(2.0) + jnp.log(l)

    kern, s_in, s_out, s_shapes, s_scratch, s_args = _host(kern, 3, 2, side, (H, nq))
    res = pl.pallas_call(
        kern, name=name, grid=(H, nq),
        in_specs=[pl.BlockSpec((tq, LANE), lambda h, i: (i, h)),
                  pl.BlockSpec((L, LANE), lambda h, i: (0, h // rep)),
                  pl.BlockSpec((L, LANE), lambda h, i: (0, h // rep))] + s_in,
        out_specs=[pl.BlockSpec((tq, LANE), lambda h, i: (i, h)),
                   pl.BlockSpec((tq, 1), lambda h, i: (h * nq + i, 0))] + s_out,
        out_shape=[jax.ShapeDtypeStruct((L, H * LANE), F32), jax.ShapeDtypeStruct((H * L, 1), F32)] + s_shapes,
        scratch_shapes=s_scratch,
        compiler_params=_cparams(("arbitrary", "arbitrary")),
    )(q, k, v, *s_args)
    return res[0], res[1], list(res[2:])


def _flash_bwd(name, q, k, v, o, lse, do, H, rep, scale, L, side=None):
    tq = min(256, L)
    nq = L // tq
    KC = min(2048, L)
    nkc = L // KC
    Hkv = H // rep

    def kern(q_ref, k_ref, v_ref, o_ref, lse_ref, do_ref, dq_ref, dk_ref, dv_ref):
        h = pl.program_id(0)
        i = pl.program_id(1)

        @pl.when((i == 0) & (h % rep == 0))
        def _():
            dk_ref[...] = jnp.zeros(dk_ref.shape, F32)
            dv_ref[...] = jnp.zeros(dv_ref.shape, F32)

        qb = q_ref[...]
        do = do_ref[...]
        dob = do.astype(BF16)
        delta = jnp.sum(do * o_ref[...], axis=-1, keepdims=True)
        lse = lse_ref[...]
        dq = jnp.zeros((tq, LANE), F32)
        for c in range(nkc):
            sl = slice(c * KC, (c + 1) * KC)
            kb = k_ref[sl, :]
            vb = v_ref[sl, :]
            s = lax.dot_general(qb, kb, (_NT, ((), ())), preferred_element_type=F32) * scale
            p = jnp.exp(s - lse)
            dp = lax.dot_general(dob, vb, (_NT, ((), ())), preferred_element_type=F32)
            ds = (p * (dp - delta) * scale).astype(BF16)
            dq = dq + lax.dot_general(ds, kb, (_NN, ((), ())), preferred_element_type=F32)
            dk_ref[sl, :] += lax.dot_general(ds, qb, (_TN, ((), ())), preferred_element_type=F32)
            dv_ref[sl, :] += lax.dot_general(p.astype(BF16), dob, (_TN, ((), ())), preferred_element_type=F32)
        dq_ref[...] = dq

    kern, s_in, s_out, s_shapes, s_scratch, s_args = _host(kern, 6, 3, side, (H, nq))
    res = pl.pallas_call(
        kern, name=name, grid=(H, nq),
        in_specs=[pl.BlockSpec((tq, LANE), lambda h, i: (i, h)),
                  pl.BlockSpec((L, LANE), lambda h, i: (0, h // rep)),
                  pl.BlockSpec((L, LANE), lambda h, i: (0, h // rep)),
                  pl.BlockSpec((tq, LANE), lambda h, i: (i, h)),
                  pl.BlockSpec((tq, 1), lambda h, i: (h * nq + i, 0)),
                  pl.BlockSpec((tq, LANE), lambda h, i: (i, h))] + s_in,
        out_specs=[pl.BlockSpec((tq, LANE), lambda h, i: (i, h)),
                   pl.BlockSpec((L, LANE), lambda h, i: (0, h // rep)),
                   pl.BlockSpec((L, LANE), lambda h, i: (0, h // rep))] + s_out,
        out_shape=[jax.ShapeDtypeStruct((L, H * LANE), F32), jax.ShapeDtypeStruct((L, Hkv * LANE), F32),
                   jax.ShapeDtypeStruct((L, Hkv * LANE), F32)] + s_shapes,
        scratch_shapes=s_scratch,
        compiler_params=_cparams(("arbitrary", "arbitrary")),
    )(q, k, v, o, lse, do, *s_args)
    return res[0], res[1], res[2], list(res[3:])


def _shift_dn(x, first_row):
    row = lax.broadcasted_iota(jnp.int32, x.shape, 0)
    return jnp.where(row == 0, first_row, pltpu.roll(x, 1, 0))


def _shift_up(x, last_row):
    n = x.shape[0]
    row = lax.broadcasted_iota(jnp.int32, x.shape, 0)
    return jnp.where(row == n - 1, last_row, pltpu.roll(x, n - 1, 0))


def _halo_specs(ndim, lead, T, tc, cb0, L):
    r8 = T // 8
    last8 = L // 8 - 1
    if ndim == 2:
        return [pl.BlockSpec((T, tc), lambda j, i: (i, cb0 + j)),
                pl.BlockSpec((8, tc), lambda j, i: (jnp.maximum(i * r8 - 1, 0), cb0 + j)),
                pl.BlockSpec((8, tc), lambda j, i: (jnp.minimum((i + 1) * r8, last8), cb0 + j))]
    return [pl.BlockSpec((None, T, tc), lambda j, i: (lead, i, cb0 + j)),
            pl.BlockSpec((None, 8, tc), lambda j, i: (lead, jnp.maximum(i * r8 - 1, 0), cb0 + j)),
            pl.BlockSpec((None, 8, tc), lambda j, i: (lead, jnp.minimum((i + 1) * r8, last8), cb0 + j))]


def _conv_rows(x_ref, xp_ref, xn_ref, w, first, last):
    x = x_ref[...]
    T = x.shape[0]
    w0, w1, w2, b = w[0:1], w[1:2], w[2:3], w[3:4]
    pr = jnp.where(first, 0.0, xp_ref[7:8, :])
    pr2 = jnp.where(first, 0.0, xp_ref[6:7, :])
    nr = jnp.where(last, 0.0, xn_ref[0:1, :])
    nr2 = jnp.where(last, 0.0, xn_ref[1:2, :])
    xm1 = _shift_dn(x, pr)
    xp1 = _shift_up(x, nr)
    pre = xm1 * w0 + x * w1 + xp1 * w2 + b
    pre_m1 = pr2 * w0 + pr * w1 + x[0:1] * w2 + b
    pre_T = x[T - 1:T] * w0 + nr * w1 + nr2 * w2 + b
    return x, xm1, xp1, pre, pre_m1, pre_T


def _conv_grads(dpre, dpre_m1, dpre_T, x, xm1, xp1, w):
    dx = _shift_up(dpre, dpre_T) * w[0:1] + dpre * w[1:2] + _shift_dn(dpre, dpre_m1) * w[2:3]
    row = lax.broadcasted_iota(jnp.int32, (8, x.shape[1]), 0)
    dw = (jnp.where(row == 0, jnp.sum(dpre * xm1, axis=0, keepdims=True), 0.0)
          + jnp.where(row == 1, jnp.sum(dpre * x, axis=0, keepdims=True), 0.0)
          + jnp.where(row == 2, jnp.sum(dpre * xp1, axis=0, keepdims=True), 0.0)
          + jnp.where(row == 3, jnp.sum(dpre, axis=0, keepdims=True), 0.0))
    return dx, dw


def _conv_fwd(name, x, col0, C, w8, act, L, tc):
    T = min(256, L)
    nt = L // T
    cb0 = col0 // tc

    def kern(x_ref, xp_ref, xn_ref, w_ref, o_ref):
        i = pl.program_id(1)
        x = x_ref[...]
        w = w_ref[...]
        pr = jnp.where(i == 0, 0.0, xp_ref[7:8, :])
        nr = jnp.where(i == nt - 1, 0.0, xn_ref[0:1, :])
        pre = _shift_dn(x, pr) * w[0:1] + x * w[1:2] + _shift_up(x, nr) * w[2:3] + w[3:4]
        o_ref[...] = _silu(pre) if act else pre

    return pl.pallas_call(
        kern, name=name, grid=(C // tc, nt),
        in_specs=_halo_specs(2, None, T, tc, cb0, L) + [pl.BlockSpec((8, tc), lambda j, i: (0, j))],
        out_specs=pl.BlockSpec((T, tc), lambda j, i: (i, j)),
        out_shape=jax.ShapeDtypeStruct((L, C), F32),
        compiler_params=_cparams(("arbitrary", "arbitrary")),
    )(x, x, x, w8)


def _conv_bwd(name, x, col0, C, w8, act, gs, L, tc):
    T = min(256, L)
    nt = L // T
    cb0 = col0 // tc
    ng = len(gs)

    def dact(pre, g):
        if not act:
            return g
        s = _sigmoid(pre)
        return g * (s * (1.0 + pre * (1.0 - s)))

    def kern(*refs):
        x_ref, xp_ref, xn_ref, w_ref = refs[:4]
        g_refs = refs[4:4 + 3 * ng]
        dx_ref, dw_ref = refs[4 + 3 * ng:]
        i = pl.program_id(1)
        first = i == 0
        last = i == nt - 1
        w = w_ref[...]
        g = g_refs[0][...]
        gp = g_refs[1][7:8, :]
        gn = g_refs[2][0:1, :]
        for n in range(1, ng):
            g = g + g_refs[3 * n][...]
            gp = gp + g_refs[3 * n + 1][7:8, :]
            gn = gn + g_refs[3 * n + 2][0:1, :]
        x, xm1, xp1, pre, pre_m1, pre_T = _conv_rows(x_ref, xp_ref, xn_ref, w, first, last)
        dpre_m1 = jnp.where(first, 0.0, dact(pre_m1, gp))
        dpre_T = jnp.where(last, 0.0, dact(pre_T, gn))
        dx_ref[...], dw = _conv_grads(dact(pre, g), dpre_m1, dpre_T, x, xm1, xp1, w)

        @pl.when(first)
        def _():
            dw_ref[...] = jnp.zeros((8, tc), F32)

        dw_ref[...] += dw

    g_specs, g_args = [], []
    for arr, lead in gs:
        g_specs += _halo_specs(arr.ndim, lead, T, tc, 0, L)
        g_args += [arr, arr, arr]
    return pl.pallas_call(
        kern, name=name, grid=(C // tc, nt),
        in_specs=_halo_specs(2, None, T, tc, cb0, L) + [pl.BlockSpec((8, tc), lambda j, i: (0, j))] + g_specs,
        out_specs=[pl.BlockSpec((T, tc), lambda j, i: (i, j)), pl.BlockSpec((8, tc), lambda j, i: (0, j))],
        out_shape=[jax.ShapeDtypeStruct((L, C), F32), jax.ShapeDtypeStruct((8, C), F32)],
        compiler_params=_cparams(("arbitrary", "arbitrary")),
    )(x, x, x, w8, *g_args)


FFN_TC = 1408


def _ffn_gate_fwd(name, u, w8, L):
    T = min(256, L)
    nt = L // T
    ncb = D_FF // FFN_TC

    def kern(xg, xgp, xgn, xu, xup, xun, wg_ref, wu_ref, o_ref):
        i = pl.program_id(1)
        pre_g = _conv_rows(xg, xgp, xgn, wg_ref[...], i == 0, i == nt - 1)[3]
        pre_u = _conv_rows(xu, xup, xun, wu_ref[...], i == 0, i == nt - 1)[3]
        o_ref[...] = (_silu(pre_g) * pre_u).astype(BF16)

    return pl.pallas_call(
        kern, name=name, grid=(ncb, nt),
        in_specs=_halo_specs(2, None, T, FFN_TC, 0, L) + _halo_specs(2, None, T, FFN_TC, ncb, L)
        + [pl.BlockSpec((8, FFN_TC), lambda j, i: (0, j)), pl.BlockSpec((8, FFN_TC), lambda j, i: (0, j + ncb))],
        out_specs=pl.BlockSpec((T, FFN_TC), lambda j, i: (i, j)),
        out_shape=jax.ShapeDtypeStruct((L, D_FF), BF16),
        compiler_params=_cparams(("arbitrary", "arbitrary")),
    )(u, u, u, u, u, u, w8, w8)


def _ffn_gate_bwd(name, u, w8, da, L):
    T = min(128, L)
    nt = L // T
    ncb = D_FF // FFN_TC

    def kern(xg, xgp, xgn, xu, xup, xun, wg_ref, wu_ref, d_ref, dp_ref, dn_ref, du_ref, dwg_ref, dwu_ref):
        i = pl.program_id(1)
        first = i == 0
        last = i == nt - 1
        wg = wg_ref[...]
        wu = wu_ref[...]
        g, gm1, gp1, pg, pg_m1, pg_T = _conv_rows(xg, xgp, xgn, wg, first, last)
        v, vm1, vp1, pu, pu_m1, pu_T = _conv_rows(xu, xup, xun, wu, first, last)

        def dpre(pg_, pu_, d):
            s = _sigmoid(pg_)
            return d * pu_ * (s * (1.0 + pg_ * (1.0 - s))), d * (pg_ * s)

        dg, dv = dpre(pg, pu, d_ref[...])
        dg_m1, dv_m1 = dpre(pg_m1, pu_m1, jnp.where(first, 0.0, dp_ref[7:8, :]))
        dg_T, dv_T = dpre(pg_T, pu_T, jnp.where(last, 0.0, dn_ref[0:1, :]))
        du_ref[0], dwg = _conv_grads(dg, dg_m1, dg_T, g, gm1, gp1, wg)
        du_ref[1], dwu = _conv_grads(dv, dv_m1, dv_T, v, vm1, vp1, wu)

        @pl.when(first)
        def _():
            dwg_ref[...] = jnp.zeros(dwg_ref.shape, F32)
            dwu_ref[...] = jnp.zeros(dwu_ref.shape, F32)

        dwg_ref[...] += dwg
        dwu_ref[...] += dwu

    wspec = pl.BlockSpec((8, FFN_TC), lambda j, i: (0, j))
    du, dwg, dwu = pl.pallas_call(
        kern, name=name, grid=(ncb, nt),
        in_specs=_halo_specs(2, None, T, FFN_TC, 0, L) + _halo_specs(2, None, T, FFN_TC, ncb, L)
        + [wspec, pl.BlockSpec((8, FFN_TC), lambda j, i: (0, j + ncb))] + _halo_specs(2, None, T, FFN_TC, 0, L),
        out_specs=[pl.BlockSpec((2, T, FFN_TC), lambda j, i: (0, i, j)), wspec, wspec],
        out_shape=[jax.ShapeDtypeStruct((2, L, D_FF), F32), jax.ShapeDtypeStruct((8, D_FF), F32),
                   jax.ShapeDtypeStruct((8, D_FF), F32)],
        compiler_params=_cparams(("arbitrary", "arbitrary")),
    )(u, u, u, u, u, u, w8, w8, da, da, da)
    return du, jnp.concatenate([dwg, dwu], axis=1)


def _masks(Q, rev):
    ri = lax.broadcasted_iota(jnp.int32, (Q, Q), 0)
    ci = lax.broadcasted_iota(jnp.int32, (Q, Q), 1)
    diff = (ri - ci) * (1 - 2 * rev)
    return diff >= 0, diff > 0


def _lane_pick(v, sel):
    return jnp.sum(v * sel, axis=-1, keepdims=True)


def _head_rows(v_all, Q, rev):
    r = lax.broadcasted_iota(jnp.int32, (HEADS * Q, LANE), 0)
    l = lax.broadcasted_iota(jnp.int32, (HEADS * Q, LANE), 1)
    pick = jnp.zeros((HEADS * Q, LANE), F32)
    for h in range(HEADS):
        pick = jnp.where((r >= h * Q) & (r < (h + 1) * Q) & (l == rev * 4 + h), 1.0, pick)
    return _nt_hi(pick, v_all)


def _ssd_chunk(S, x, B, C, dtraw, alog, dtb, rev, kept=None):
    Q = dtraw.shape[0]
    incl, _ = _masks(Q, rev)
    tri = incl.astype(F32)
    dt = _softplus(dtraw + dtb)
    a_all = dt * (-jnp.exp(alog))
    acum_all = _nn_hi(tri, a_all)
    total_all = jnp.sum(a_all, axis=0, keepdims=True)
    lane = lax.broadcasted_iota(jnp.int32, (1, LANE), 1)
    rows_all = _head_rows(acum_all, Q, rev)
    ys, Sn = [], []
    for h in range(HEADS):
        g = h // 2
        sel = (lane == rev * 4 + h).astype(F32)
        acum = _lane_pick(acum_all, sel)
        dth = _lane_pick(dt, sel)
        tot = _lane_pick(total_all, sel)
        seg = acum - rows_all[h * Q:(h + 1) * Q, :]
        decay = jnp.exp(jnp.where(incl, seg, -1e30))
        xdt = x[h] * dth
        Sh = S[HD * h:HD * (h + 1), :]
        scores = _nt(C[g], B[g]) * decay
        y_diag = _nn(scores, xdt)
        states = _tn(xdt, B[g] * jnp.exp(tot - acum))
        y_off = _nt(C[g], Sh) * jnp.exp(acum)
        ys.append(y_diag + y_off)
        Sn.append(Sh * jnp.exp(tot) + states)
    return ys, jnp.concatenate(Sn, axis=0), []


def _inv_unit_raw(Lm):
    N = Lm.shape[0]
    Q = D_CHUNK
    ri = lax.broadcasted_iota(jnp.int32, (N, N), 0)
    ci = lax.broadcasted_iota(jnp.int32, (N, N), 1)
    X = (ri == ci).astype(F32) - Lm
    P = _raw_dot(Lm, Lm, _NN, False)
    n = 2
    while n < Q:
        X = X + _raw_dot(X, P, _NN, False)
        n *= 2
        if n < Q:
            P = _raw_dot(P, P, _NN, False)
    return X


@jax.custom_vjp
def _inv_unit(Lm, T_saved):
    return _inv_unit_raw(Lm) if T_saved is None else T_saved


def _inv_unit_f(Lm, T_saved):
    T = _inv_unit_raw(Lm) if T_saved is None else T_saved
    return T, T


def _inv_unit_b(T, g):
    return -_raw_dot(_raw_dot(T, g, _TN, False), T, _NT, False), None


_inv_unit.defvjp(_inv_unit_f, _inv_unit_b)


def _delta_chunk(S, q, k, v, braw, araw, alog, dtb, rev, kept=None):
    Q = braw.shape[0]
    N = HEADS * Q
    tri = _masks(Q, rev)[0].astype(F32)
    ri = lax.broadcasted_iota(jnp.int32, (N, N), 0)
    ci = lax.broadcasted_iota(jnp.int32, (N, N), 1)
    sh = int(math.log2(Q))
    same = (ri >> sh) == (ci >> sh)
    diff = (ri - ci) * (1 - 2 * rev)
    incl = same & (diff >= 0)
    strict = same & (diff > 0)
    beta_all = _sigmoid(braw)
    g_all = -jnp.exp(alog) * _softplus(araw + dtb)
    G_all = _nn_hi(tri, g_all)
    Gtot_all = jnp.sum(g_all, axis=0, keepdims=True)
    r = lax.broadcasted_iota(jnp.int32, (N, LANE), 0)
    l = lax.broadcasted_iota(jnp.int32, (N, LANE), 1)
    selm = (l == rev * 4 + (r >> sh)).astype(F32)
    rows4 = lambda a: jnp.concatenate([a] * HEADS, axis=0)
    XG = rows4(G_all) * selm
    G = jnp.sum(XG, axis=-1, keepdims=True)
    bt = jnp.sum(rows4(beta_all) * selm, axis=-1, keepdims=True)
    Gtot = jnp.sum(Gtot_all * selm, axis=-1, keepdims=True)
    decay = jnp.exp(jnp.where(incl, G - _nt_h3(jnp.ones((N, LANE), F32), XG), -1e30))
    qs, ks, vs = (jnp.concatenate(t, axis=0) for t in (q, k, v))
    qn = qs * lax.rsqrt(jnp.sum(qs * qs, axis=-1, keepdims=True) + 1e-6)
    kn = ks * lax.rsqrt(jnp.sum(ks * ks, axis=-1, keepdims=True) + 1e-6)
    qc = qn * (HD ** -0.5)
    kb = kn * bt
    T = _inv_unit(jnp.where(strict, _nt(kb, kn) * decay, 0.0), None if kept is None else kept[0])
    eG = jnp.exp(G)
    u = _nn(T, vs * bt)
    w = _nn(T, kb * eG)
    qk = _nt(qc, kn) * decay
    spread = (lax.broadcasted_iota(jnp.int32, (HD, N), 0)
              == (lax.broadcasted_iota(jnp.int32, (HD, N), 1) & (HD - 1))).astype(F32)
    wide = lambda a: jnp.where(same, _nn(a, spread), 0.0)
    v_new = u - _nn(wide(w), S)
    o = _nn(wide(qc * eG), S) + _nn(qk, v_new)
    S_new = S * jnp.exp(Gtot) + _tn(wide(kn * jnp.exp(Gtot - G)), v_new)
    return [o[Q * h:Q * (h + 1), :] for h in range(HEADS)], S_new, [T]


def _seq_pieces(ref, r0, Q, splits):
    if splits is None:
        return ref[r0:r0 + Q, :]
    return [[ref[r0:r0 + Q, o + w * t:o + w * (t + 1)] for t in range(n)] for o, w, n in splits]


def _store_pieces(ref, r0, Q, splits, vals, extra=None):
    if splits is None:
        ref[r0:r0 + Q, :] = vals
        return
    for g, (o, w, n) in enumerate(splits):
        for t in range(n):
            v = vals[g][t]
            if extra is not None and g == 0:
                v = v + extra[r0:r0 + Q, o + w * t:o + w * (t + 1)]
            ref[r0:r0 + Q, o + w * t:o + w * (t + 1)] = v


def _flat(ins):
    out = []
    for v in ins:
        if isinstance(v, list):
            out.extend(v)
        else:
            out.append(v)
    return out


def _scan_fwd(name, chunk_fn, seqs, rows, Q, L, CH, kept_shapes=()):
    nc = L // Q
    nb = nc // CH
    ns, nr = len(seqs), len(rows)
    nk = 1 + len(kept_shapes)
    BQ = Q * CH

    def kern(*refs):
        s_refs = (refs[:ns], refs[ns:2 * ns])
        r_refs = refs[2 * ns:2 * ns + nr]
        pos = 2 * ns + nr
        y_refs = refs[pos:pos + 2]
        k_refs = (refs[pos + 2:pos + 2 + nk], refs[pos + 2 + nk:pos + 2 + 2 * nk])
        S_scr = refs[pos + 2 + 2 * nk]
        i = pl.program_id(0)

        @pl.when(i == 0)
        def _():
            S_scr[...] = jnp.zeros(S_scr.shape, F32)

        rws = [r[...] for r in r_refs]
        for d in (0, 1):
            S = S_scr[d]
            for cc in range(CH):
                c = cc if d == 0 else CH - 1 - cc
                k_refs[d][0][c] = S
                ins = [_seq_pieces(r, c * Q, Q, sp) for r, (_, _, _, sp) in zip(s_refs[d], seqs)]
                ys, S, kept = chunk_fn(S, *_flat(ins), *rws, d)
                for r, v in zip(k_refs[d][1:], kept):
                    r[c] = v
                for h in range(HEADS):
                    y_refs[d][c * Q:(c + 1) * Q, HD * h:HD * (h + 1)] = ys[h]
            S_scr[d] = S

    fwd_specs = [pl.BlockSpec((BQ, w), functools.partial(lambda i, cb: (i, cb), cb=cb)) for _, w, cb, _ in seqs]
    rev_specs = [pl.BlockSpec((BQ, w), functools.partial(lambda i, cb: (nb - 1 - i, cb), cb=cb))
                 for _, w, cb, _ in seqs]
    arrs = [a for a, _, _, _ in seqs]
    k_shapes = [(GROUP_W, HD)] + list(kept_shapes)
    res = pl.pallas_call(
        kern, name=name, grid=(nb,),
        in_specs=fwd_specs + rev_specs + [pl.BlockSpec((1, LANE), lambda i: (0, 0)) for _ in rows],
        out_specs=[pl.BlockSpec((BQ, GROUP_W), lambda i: (i, 0)),
                   pl.BlockSpec((BQ, GROUP_W), lambda i: (nb - 1 - i, 0))]
        + [pl.BlockSpec((CH,) + s, lambda i: (i, 0, 0)) for s in k_shapes]
        + [pl.BlockSpec((CH,) + s, lambda i: (nb - 1 - i, 0, 0)) for s in k_shapes],
        out_shape=[jax.ShapeDtypeStruct((L, GROUP_W), F32)] * 2
        + [jax.ShapeDtypeStruct((nc,) + s, F32) for s in k_shapes] * 2,
        scratch_shapes=[pltpu.VMEM((2, GROUP_W, HD), F32)],
        compiler_params=_cparams(("arbitrary",)),
    )(*arrs, *arrs, *rows)
    return res[0], res[1], list(res[2:2 + nk]), list(res[2 + nk:])


def _scan_bwd(name, chunk_fn, seqs, rows, ssaves, dy, extra, Q, L, CH, side=None):
    nc = L // Q
    nb = nc // CH
    BQ = Q * CH
    ns, nr = len(seqs), len(rows)
    nk = len(ssaves[0])
    has_extra = extra is not None

    def kern(*refs):
        s_refs = (refs[:ns], refs[ns:2 * ns])
        pos = 2 * ns
        r_refs = refs[pos:pos + nr]
        pos += nr
        k_refs = (refs[pos:pos + nk], refs[pos + nk:pos + 2 * nk])
        pos += 2 * nk
        dy_refs = refs[pos:pos + 2]
        pos += 2
        ex_ref = refs[pos] if has_extra else None
        pos += 1 if has_extra else 0
        ds_refs = (refs[pos:pos + ns], refs[pos + ns:pos + 2 * ns])
        pos += 2 * ns
        dr_refs = refs[pos:pos + nr]
        dS_scr = refs[pos + nr]
        i = pl.program_id(0)

        @pl.when(i == 0)
        def _():
            dS_scr[...] = jnp.zeros(dS_scr.shape, F32)
            for r in dr_refs:
                r[...] = jnp.zeros(r.shape, F32)

        rws = [r[...] for r in r_refs]
        dr_acc = [jnp.zeros((1, LANE), F32) for _ in rows]
        for d in (0, 1):
            dS = dS_scr[d]
            for cc in range(CH):
                c = CH - 1 - cc if d == 0 else cc
                S = k_refs[d][0][c]
                kept = [r[c] for r in k_refs[d][1:]]
                dys = [dy_refs[d][c * Q:(c + 1) * Q, HD * h:HD * (h + 1)] for h in range(HEADS)]
                ins = [_seq_pieces(r, c * Q, Q, sp) for r, (_, _, _, sp) in zip(s_refs[d], seqs)]
                _, vjp = jax.vjp(
                    functools.partial(
                        lambda S_, ins_, rws_, d_, kept_: chunk_fn(S_, *_flat(ins_), *rws_, d_, kept_)[:2],
                        d_=d, kept_=kept),
                    S, ins, rws)
                dS, dins, drws = vjp((dys, dS))
                for n_, (r, (_, _, _, sp)) in enumerate(zip(ds_refs[d], seqs)):
                    _store_pieces(r, c * Q, Q, sp, dins[n_],
                                  extra=ex_ref if (has_extra and d == 0 and n_ == 0) else None)
                dr_acc = [a + g for a, g in zip(dr_acc, drws)]
            dS_scr[d] = dS
        for r, g in zip(dr_refs, dr_acc):
            r[...] += g

    def blk(shape, rev, cb=0):
        nd = len(shape)
        if rev:
            return pl.BlockSpec(shape, lambda i: (i, cb) + (0,) * (nd - 2))
        return pl.BlockSpec(shape, lambda i: (nb - 1 - i, cb) + (0,) * (nd - 2))

    arrs = [a for a, _, _, _ in seqs]
    in_specs = [blk((BQ, w), False, cb) for _, w, cb, _ in seqs] + [blk((BQ, w), True, cb) for _, w, cb, _ in seqs]
    in_specs += [pl.BlockSpec((1, LANE), lambda i: (0, 0)) for _ in rows]
    in_specs += [blk((CH,) + a.shape[1:], False) for a in ssaves[0]]
    in_specs += [blk((CH,) + a.shape[1:], True) for a in ssaves[1]]
    in_specs += [blk((BQ, GROUP_W), False), blk((BQ, GROUP_W), True)]
    args = arrs + arrs + list(rows) + list(ssaves[0]) + list(ssaves[1]) + [dy, dy]
    if has_extra:
        in_specs.append(blk((BQ, GROUP_W), False))
        args.append(extra)
    kern, s_in, s_out, s_shapes, s_scratch, s_args = _host(kern, len(args), 2 * ns + nr, side, (nb,), 1)
    res = pl.pallas_call(
        kern, name=name, grid=(nb,),
        in_specs=in_specs + s_in,
        out_specs=[blk((BQ, w), False) for _, w, _, _ in seqs] + [blk((BQ, w), True) for _, w, _, _ in seqs]
        + [pl.BlockSpec((1, LANE), lambda i: (0, 0)) for _ in rows] + s_out,
        out_shape=[jax.ShapeDtypeStruct((L, w), F32) for _, w, _, _ in seqs] * 2
        + [jax.ShapeDtypeStruct((1, LANE), F32) for _ in rows] + s_shapes,
        scratch_shapes=[pltpu.VMEM((2, GROUP_W, HD), F32)] + s_scratch,
        compiler_params=_cparams(("arbitrary",)),
    )(*args, *s_args)
    return list(res[:ns]), list(res[ns:2 * ns]), list(res[2 * ns:2 * ns + nr]), list(res[2 * ns + nr:])


def _loss_call(y, tgt, L):
    T = min(256, L)

    def kern(y_ref, t_ref, dy_ref, l_ref):
        i = pl.program_id(0)
        e = y_ref[...] - t_ref[...]
        dy_ref[...] = e * (1.0 / D_MODEL)

        @pl.when(i == 0)
        def _():
            l_ref[...] = jnp.zeros(l_ref.shape, F32)

        part = 0.5 * jnp.sum(jnp.sum(e * e, axis=-1, keepdims=True) * (1.0 / D_MODEL), axis=0, keepdims=True)
        l_ref[...] += jnp.broadcast_to(part, l_ref.shape)

    return pl.pallas_call(
        kern, name="loss_head", grid=(L // T,),
        in_specs=[_spec2(T, D_MODEL), _spec2(T, D_MODEL)],
        out_specs=[_spec2(T, D_MODEL), pl.BlockSpec((8, LANE), lambda i: (0, 0))],
        out_shape=[jax.ShapeDtypeStruct((L, D_MODEL), F32), jax.ShapeDtypeStruct((8, LANE), F32)],
        compiler_params=_cparams(("arbitrary",)),
    )(y, tgt)


_ANY = pl.BlockSpec(memory_space=pl.ANY)


def _coords():
    return lax.axis_index("x"), lax.axis_index("y"), lax.axis_index("c")


class _Copies:
    def __init__(self, ins, out_shapes, copies_fn, n_remote, n_local):
        self.ins, self.out_shapes, self.copies_fn = list(ins), list(out_shapes), copies_fn
        self.n_remote, self.n_local = n_remote, n_local

    def scratch(self):
        return [pltpu.SemaphoreType.DMA((self.n_remote,)), pltpu.SemaphoreType.DMA((self.n_remote,)),
                pltpu.SemaphoreType.DMA((self.n_local,))]

    def _descr(self, in_refs, out_refs, sems):
        send_sems, recv_sems, lsems = sems
        remote, local = self.copies_fn(list(in_refs), list(out_refs))
        assert len(remote) == self.n_remote and len(local) == self.n_local
        mk = lambda k, src, dst, peer: pltpu.make_async_remote_copy(
            src_ref=src, dst_ref=dst, send_sem=send_sems.at[k], recv_sem=recv_sems.at[k], device_id=peer,
            device_id_type=MESH)
        sends = [mk(k, src, dst, peer) for k, (src, dst, _, peer) in enumerate(remote)]
        recvs = [mk(k, src, land, peer) for k, (src, _, land, peer) in enumerate(remote)]
        locs = [pltpu.make_async_copy(src, dst, lsems.at[k]) for k, (src, dst) in enumerate(local)]
        return sends, recvs, locs

    def start(self, in_refs, out_refs, sems):
        sends, _, locs = self._descr(in_refs, out_refs, sems)
        for c in locs + sends:
            c.start()

    def finish(self, in_refs, out_refs, sems):
        sends, recvs, locs = self._descr(in_refs, out_refs, sems)
        for c in recvs:
            c.wait_recv()
        for c in sends:
            c.wait_send()
        for c in locs:
            c.wait()

    def call(self, name):
        ni, no = len(self.ins), len(self.out_shapes)

        def body(*refs):
            self.start(refs[:ni], refs[ni:ni + no], refs[ni + no:])
            self.finish(refs[:ni], refs[ni:ni + no], refs[ni + no:])

        return pl.pallas_call(body, name=name, in_specs=[_ANY] * ni, out_specs=[_ANY] * no,
                              out_shape=self.out_shapes, scratch_shapes=self.scratch())(*self.ins)


def _chip_peers(x, y):
    return [(1 - x, y), (x, 1 - y), (1 - x, 1 - y)]


def _gather_copies(arrs):
    def copies_fn(ins, outs):
        x, y, c = _coords()
        me = 2 * x + y
        remote, local = [], []
        for src, out in zip(ins, outs):
            local.append((src, out.at[me]))
            for px, py in _chip_peers(x, y):
                remote.append((src, out.at[me], out.at[2 * px + py], (px, py, c)))
        return remote, local

    shapes = [jax.ShapeDtypeStruct((4,) + a.shape, a.dtype) for a in arrs]
    return _Copies(arrs, shapes, copies_fn, 3 * len(arrs), len(arrs))


def _scatter_copies(Gs, small):
    nb = len(Gs)

    def copies_fn(ins, outs):
        x, y, c = _coords()
        me = 2 * x + y
        remote, local = [], []
        for g, out in zip(ins[:nb], outs[:nb]):
            local.append((g.at[me], out.at[me]))
            for px, py in _chip_peers(x, y):
                remote.append((g.at[2 * px + py], out.at[me], out.at[2 * px + py], (px, py, c)))
        if small is not None:
            dev = 4 * x + 2 * y + c
            gs, outs_ = ins[nb], outs[nb]
            local.append((gs, outs_.at[dev]))
            for mask in range(1, 8):
                px, py, pc = x ^ (mask >> 2), y ^ ((mask >> 1) & 1), c ^ (mask & 1)
                remote.append((gs, outs_.at[dev], outs_.at[4 * px + 2 * py + pc], (px, py, pc)))
        return remote, local

    ins = list(Gs) + ([small] if small is not None else [])
    shapes = [jax.ShapeDtypeStruct(g.shape, g.dtype) for g in Gs]
    if small is not None:
        shapes.append(jax.ShapeDtypeStruct((8,) + small.shape, small.dtype))
    extra = 1 if small is not None else 0
    return _Copies(ins, shapes, copies_fn, 3 * nb + 7 * extra, nb + extra)


SWAP_STREAMS = 8


def _row_chunks(rows):
    k = SWAP_STREAMS
    if rows % (8 * k) == 0 and rows >= 64 * k:
        return [(q * (rows // k), rows // k) for q in range(k)]
    return [(0, rows)]


def _swap_copies(parts):
    chunks = [_row_chunks(p.shape[0]) for p in parts]
    n = sum(len(ch) for ch in chunks)

    def copies_fn(ins, outs):
        x, y, c = _coords()
        remote, local = [], []
        for src, out, ch in zip(ins, outs, chunks):
            for r0, nr in ch:
                rows = pl.ds(r0, nr)
                local.append((src.at[rows], out.at[c, rows]))
                remote.append((src.at[rows], out.at[c, rows], out.at[1 - c, rows], (x, y, 1 - c)))
        return remote, local

    shapes = [jax.ShapeDtypeStruct((2,) + p.shape, p.dtype) for p in parts]
    return _Copies(parts, shapes, copies_fn, n, n)


def _merge_copies(sets):
    ins = [a for s in sets for a in s.ins]
    shapes = [o for s in sets for o in s.out_shapes]

    def copies_fn(in_refs, out_refs):
        remote, local, pi, po = [], [], 0, 0
        for s in sets:
            r, l = s.copies_fn(in_refs[pi:pi + len(s.ins)], out_refs[po:po + len(s.out_shapes)])
            remote += r
            local += l
            pi += len(s.ins)
            po += len(s.out_shapes)
        return remote, local

    return _Copies(ins, shapes, copies_fn, sum(s.n_remote for s in sets), sum(s.n_local for s in sets))


def _row_tile(rows):
    best = rows
    for d in range(8, min(rows, 256) + 1, 8):
        if rows % d == 0:
            best = d
    return best


def _sum_slots(name, recv):
    n, R, W = recv.shape
    tr = _row_tile(R)

    def kern(r_ref, o_ref):
        acc = r_ref[0].astype(F32)
        for s in range(1, n):
            acc = acc + r_ref[s].astype(F32)
        o_ref[...] = acc

    return pl.pallas_call(
        kern, name=name, grid=(R // tr,),
        in_specs=[pl.BlockSpec((n, tr, W), lambda i: (0, i, 0))],
        out_specs=pl.BlockSpec((tr, W), lambda i: (i, 0)),
        out_shape=jax.ShapeDtypeStruct((R, W), F32),
        compiler_params=_cparams(("arbitrary",)),
    )(recv)


def _adamw_call(name, slots, w, m, v):
    nl = len(slots)
    n, R, W = slots[0].shape
    tr = _row_tile(R)
    nr = R // tr

    def kern(*refs):
        s_refs = refs[:nl]
        w_ref, m_ref, v_ref, g_ref, d_ref, nm_ref, nv_ref = refs[nl:]
        layer = pl.program_id(0)
        g = s_refs[0][0]
        for s in range(1, n):
            g = g + s_refs[0][s]
        for l in range(1, nl):
            gl = s_refs[l][0]
            for s in range(1, n):
                gl = gl + s_refs[l][s]
            g = jnp.where(layer == l, gl, g)
        m_ = ADAM_B1 * m_ref[...] + (1.0 - ADAM_B1) * g
        v_ = ADAM_B2 * v_ref[...] + (1.0 - ADAM_B2) * (g * g)
        m_hat = m_ / (1.0 - ADAM_B1 ** ADAM_STEP)
        v_hat = v_ / (1.0 - ADAM_B2 ** ADAM_STEP)
        g_ref[...] = g
        d_ref[...] = -ADAM_LR * (m_hat / (jnp.sqrt(v_hat) + ADAM_EPS) + ADAM_WD * w_ref[...])
        nm_ref[...] = m_
        nv_ref[...] = v_

    blk = pl.BlockSpec((None, tr, W), lambda l, i: (l, i, 0))
    return pl.pallas_call(
        kern, name=name, grid=(nl, nr),
        in_specs=[pl.BlockSpec((n, tr, W), lambda l, i: (0, i, 0)) for _ in slots] + [blk, blk, blk],
        out_specs=[blk, blk, blk, blk],
        out_shape=[jax.ShapeDtypeStruct((nl, R, W), F32)] * 4,
        compiler_params=_cparams(("arbitrary", "arbitrary")),
    )(*slots, w, m, v)


def _pack(arrs, width, row_mult):
    flat = jnp.concatenate([a.reshape(-1) for a in arrs])
    n = flat.shape[0]
    rows = -(-n // width)
    rows = -(-rows // row_mult) * row_mult
    return jnp.pad(flat, (0, rows * width - n)).reshape(rows, width)


def _unpack(buf, shapes):
    flat = buf.reshape(-1)
    out, pos = [], 0
    for s in shapes:
        n = int(np.prod(s))
        out.append(flat[pos:pos + n].reshape(s))
        pos += n
    return out


def _rope_angles(L, rot_dim):
    rows = L // GRID_W
    row = jnp.repeat(jnp.arange(rows), GRID_W).astype(F32)
    col = jnp.tile(jnp.arange(GRID_W), rows).astype(F32)
    sec = rot_dim // 2
    inv_freq = ROPE_BASE ** (-jnp.arange(0, sec, 2, dtype=F32) / sec)
    ang_r = row[:, None] * inv_freq
    ang_c = col[:, None] * inv_freq
    ang = jnp.concatenate([ang_r, ang_r, ang_c, ang_c], axis=-1)
    return jnp.cos(ang), jnp.sin(ang)


def _rot_matrix(r):
    R = np.zeros((r, r), np.float32)
    q = r // 4
    for s in range(2):
        for t in range(q):
            lo = s * (r // 2) + t
            hi = lo + q
            R[hi, lo] = -1.0
            R[lo, hi] = 1.0
    return R


def _place_tables(L, cos, sin, width, offsets):
    r = cos.shape[1]
    Rm = np.zeros((width, width), np.float32)
    R = _rot_matrix(r)
    cs, ss, pos = [], [], 0
    for o in list(offsets) + [width]:
        if o > pos:
            cs.append(jnp.ones((L, o - pos), F32))
            ss.append(jnp.zeros((L, o - pos), F32))
        if o < width:
            cs.append(cos)
            ss.append(sin)
            Rm[o:o + r, o:o + r] = R
        pos = o + r
    return jnp.concatenate(cs, axis=1), jnp.concatenate(ss, axis=1), jnp.asarray(Rm)


def _head_mean_matrix(width, stride, n):
    M = np.zeros((width, width), np.float32)
    for o in range(0, width, stride):
        M[o:o + n, o:o + n] = 1.0 / n
    return jnp.asarray(M)


def _pad_heads(w, n_heads, real, padded, axis):
    parts = jnp.split(w, n_heads, axis=axis)
    padw = [(0, 0)] * w.ndim
    padw[axis] = (0, padded - real)
    return jnp.concatenate([jnp.pad(p, padw) for p in parts], axis=axis)


def _row128(v):
    v = v.reshape(1, -1)
    return jnp.pad(v, ((0, 0), (0, LANE - v.shape[1])))


def _conv_w8(w, b):
    C = w.shape[1]
    rows = [w, jnp.zeros((1, C), F32) if b is None else b.reshape(1, C), jnp.zeros((4, C), F32)]
    return jnp.concatenate(rows, axis=0)


def _build_layer(W):
    w_in = W['w_in']
    o = 0
    cols = {}
    for name, n in [('a_cq', A_Q_LORA), ('a_ckv', A_KV_LORA), ('a_kr', A_ROPE), ('b_q', 256), ('b_k', 128),
                    ('b_v', 128), ('c_z', 256), ('c_xbc', 512), ('c_dt', 8), ('d_qkv', 768), ('d_z', 256),
                    ('d_b', 8), ('d_a', 8)]:
        cols[name] = w_in[:, o:o + n]
        o += n
    padc = lambda a, lo, width: jnp.pad(a, ((0, 0), (lo, width - lo - a.shape[1])))
    pieces = {
        'b_q': _pad_heads(cols['b_q'], 4, HD, LANE, 1), 'c_xbc': cols['c_xbc'], 'a_cq': padc(cols['a_cq'], 0, 256),
        'b_k': _pad_heads(cols['b_k'], 2, HD, LANE, 1), 'd_qkv': cols['d_qkv'],
        'b_v': _pad_heads(cols['b_v'], 2, HD, LANE, 1), 'c_z': cols['c_z'], 'd_z': cols['d_z'],
        'a_ckv': cols['a_ckv'], 'a_kr': padc(cols['a_kr'], A_NOPE, LANE), 'c_dt': padc(cols['c_dt'], 0, LANE),
        'd_b': padc(cols['d_b'], 0, LANE), 'd_a': padc(cols['d_a'], 0, LANE),
        'pad': jnp.zeros((D_MODEL, LANE), w_in.dtype)}
    out = {'w_in': jnp.concatenate([pieces[n] for n, _, _ in P_LAYOUT], axis=1)}
    out['a_q_norm'] = padc(W['a_q_norm'].reshape(1, -1), 0, 256)
    wuq = jnp.pad(W['a_w_uq'], ((0, 256 - A_Q_LORA), (0, 0)))
    out['a_w_uq'] = _pad_heads(wuq, 4, A_NOPE + A_ROPE, LANE, 1)
    out['a_kv_norm'] = W['a_kv_norm'].reshape(1, -1)
    ukv = W['a_w_ukv'].reshape(A_KV_LORA, HEADS, 2, HD)
    out['a_w_uk'] = _pad_heads(ukv[:, :, 0, :].reshape(A_KV_LORA, 256), 4, HD, LANE, 1)
    out['a_w_uv'] = _pad_heads(ukv[:, :, 1, :].reshape(A_KV_LORA, 256), 4, HD, LANE, 1)
    out['a_out_norm'] = _pad_heads(W['a_out_norm'].reshape(1, -1), 4, HD, LANE, 1)
    out['b_q_norm'] = _pad_heads(jnp.tile(W['b_q_norm'].reshape(1, -1), (1, 4)), 4, HD, LANE, 1)
    out['b_k_norm'] = _pad_heads(jnp.tile(W['b_k_norm'].reshape(1, -1), (1, 2)), 2, HD, LANE, 1)
    out['b_out_norm'] = _pad_heads(W['b_out_norm'].reshape(1, -1), 4, HD, LANE, 1)
    out['c_conv'] = _conv_w8(W['c_conv_w'], W['c_conv_b'])
    out['c_a_log'] = _row128(W['c_a_log'])
    out['c_dt_bias'] = _row128(W['c_dt_bias'])
    out['c_d_skip'] = jnp.repeat(W['c_d_skip'], HD).reshape(1, -1)
    out['c_out_norm'] = W['c_out_norm'].reshape(1, -1)
    out['d_conv'] = _conv_w8(W['d_conv_w'], None)
    out['d_a_log'] = _row128(W['d_a_log'])
    out['d_dt_bias'] = _row128(W['d_dt_bias'])
    out['d_out_norm'] = jnp.tile(W['d_out_norm'].reshape(1, -1), (1, 4))
    wo = W['w_out']
    out['w_out'] = jnp.concatenate([_pad_heads(wo[0:256], 4, HD, LANE, 0), _pad_heads(wo[256:512], 4, HD, LANE, 0),
                                    wo[512:1024]], axis=0)
    for n in ['pre_mix_norm', 'post_mix_norm', 'pre_ffn_norm', 'post_ffn_norm']:
        out[n] = W[n].reshape(1, -1)
    out['f_w_in'] = W['f_w_in']
    out['f_conv'] = _conv_w8(W['f_conv_w'], W['f_conv_b'])
    out['f_w_out'] = W['f_w_out']
    return out


def _fn_norm_in(a, p):
    return [_rms(a[0], p[0])]


def _fn_resid_norm2(a, p):
    x1 = a[0] + _rms(a[1], p[0])
    return [x1, _rms(x1, p[1])]


def _fn_resid_norm(a, p):
    return [a[0] + _rms(a[1], p[0])]


def _fn_a_prep(a, p):
    cq, ckv, kr, cosq, sinq, cosk, sink = a
    q_norm, w_uq, kv_norm, w_uk, w_uv, rq, rk = p
    q = _nn(_rms(cq, q_norm, A_Q_LORA), w_uq)
    q = q * cosq + _nn_h3(q, rq) * sinq
    kvn = _rms(ckv, kv_norm)
    kr_r = kr * cosk + _nn_h3(kr, rk) * sink
    kk = _nn(kvn, w_uk) + jnp.concatenate([kr_r] * HEADS, axis=1)
    return [q, kk, _nn(kvn, w_uv)]


def _fn_b_prep(a, p):
    q, k, v, cosq, sinq, cosk, sink = a
    q_norm, k_norm, mq, mk, rq, rk = p
    qn = q * lax.rsqrt(_nn_h3(q * q, mq) + EPS) * q_norm
    kn = k * lax.rsqrt(_nn_h3(k * k, mk) + EPS) * k_norm
    return [qn * cosq + _nn_h3(qn, rq) * sinq, kn * cosk + _nn_h3(kn, rk) * sink, v]


def _fn_mixer_post(a, p):
    oa, ob, yc0, yc1, xs, zc, od0, od1, zd = a
    a_norm, b_norm, dskip, c_norm, d_norm, m64 = p
    oc = _rms((yc0 + yc1 + xs * dskip) * _silu(zc), c_norm)
    od = od0 + od1
    odn = od * lax.rsqrt(_nn_h3(od * od, m64) + EPS) * d_norm * _silu(zd)
    return [jnp.concatenate([_rms(oa, a_norm, GROUP_W), _rms(ob, b_norm, GROUP_W), oc, odn], axis=1)]


def _fn_assemble(a, p):
    (dbq, dxbc, dcq, dbk, dqkv, dbv, dzc, dzd, dckv, dkr, ddt0, ddt1, db0, db1, da0, da1) = a
    return [jnp.concatenate([dbq, dxbc, dcq, dbk, dqkv, dbv, dzc, dzd, dckv, dkr, ddt0 + ddt1, db0 + db1,
                             da0 + da1, jnp.zeros_like(dckv)], axis=1)]


def _pspec(T, name):
    off, w = P_OFF[name]
    return _spec2(T, w, off // w)


def _layer_fwd(l, x, h, K, tabs, L, T, next_norm, side=None):
    n = f"l{l}_"
    sv = {'x': x, 'h': h}
    p = _mm(n + "in_proj", h, K['w_in'].astype(BF16), 'nn', F32, 1024, 1280, 1024)
    sv['p'] = p
    a_acts = [(p, _pspec(T, 'a_cq')), (p, _pspec(T, 'a_ckv')), (p, _pspec(T, 'a_kr')),
              (tabs['a_cq'], _spec2(T, 512)), (tabs['a_sq'], _spec2(T, 512)),
              (tabs['a_ck'], _spec2(T, LANE)), (tabs['a_sk'], _spec2(T, LANE))]
    a_pars = [K['a_q_norm'], K['a_w_uq'], K['a_kv_norm'], K['a_w_uk'], K['a_w_uv'], tabs['a_rq'], tabs['a_rk']]
    qa, ka, va = _tw_fwd(n + "a_prep", _fn_a_prep, a_acts, a_pars, [(512, BF16)] * 3, L, T)
    oa, lse_a, sv['side'] = _flash_fwd(n + "a_attn", qa, ka, va, HEADS, 1, (A_NOPE + A_ROPE) ** -0.5, L, side)
    sv.update(a_acts=a_acts, a_pars=a_pars, qa=qa, ka=ka, va=va, oa=oa, lse_a=lse_a)
    b_acts = [(p, _pspec(T, 'b_q')), (p, _pspec(T, 'b_k')), (p, _pspec(T, 'b_v')),
              (tabs['b_cq'], _spec2(T, 512)), (tabs['b_sq'], _spec2(T, 512)),
              (tabs['b_ck'], _spec2(T, 256)), (tabs['b_sk'], _spec2(T, 256))]
    b_pars = [K['b_q_norm'], K['b_k_norm'], tabs['b_mq'], tabs['b_mk'], tabs['b_rq'], tabs['b_rk']]
    qb, kb, vb = _tw_fwd(n + "b_prep", _fn_b_prep, b_acts, b_pars, [(512, BF16), (256, BF16), (256, BF16)], L, T)
    ob, lse_b, _ = _flash_fwd(n + "b_attn", qb, kb, vb, HEADS, 2, HD ** -0.5, L)
    sv.update(b_acts=b_acts, b_pars=b_pars, qb=qb, kb=kb, vb=vb, ob=ob, lse_b=lse_b)
    xbc = _conv_fwd(n + "c_conv", p, P_OFF['c_xbc'][0], C_XBC, K['c_conv'], True, L, 512)
    c_seqs = [(xbc, C_XBC, 0, [(0, HD, 4), (256, HD, 2), (384, HD, 2)]),
              (p, LANE, P_OFF['c_dt'][0] // LANE, None)]
    c_rows = [K['c_a_log'], K['c_dt_bias']]
    yc0, yc1, sc0, sc1 = _scan_fwd(n + "c_ssd", _ssd_chunk, c_seqs, c_rows, C_CHUNK, L, C_PER_STEP)
    sv.update(xbc=xbc, c_seqs=c_seqs, c_rows=c_rows, sc=(sc0, sc1))
    qkv = _conv_fwd(n + "d_conv", p, P_OFF['d_qkv'][0], D_QKV, K['d_conv'], True, L, 768)
    d_seqs = [(qkv, D_QKV, 0, [(0, HD, 4), (256, HD, 4), (512, HD, 4)]),
              (p, LANE, P_OFF['d_b'][0] // LANE, None), (p, LANE, P_OFF['d_a'][0] // LANE, None)]
    d_rows = [K['d_a_log'], K['d_dt_bias']]
    od0, od1, sd0, sd1 = _scan_fwd(n + "d_delta", _delta_chunk, d_seqs, d_rows, D_CHUNK, L, D_PER_STEP,
                                   [(HEADS * D_CHUNK, HEADS * D_CHUNK)])
    sv.update(qkv=qkv, d_seqs=d_seqs, d_rows=d_rows, sd=(sd0, sd1))
    m_acts = [(oa, _spec2(T, 512)), (ob, _spec2(T, 512)), (yc0, _spec2(T, 256)), (yc1, _spec2(T, 256)),
              (xbc, _spec2(T, 256, 0)), (p, _pspec(T, 'c_z')), (od0, _spec2(T, 256)), (od1, _spec2(T, 256)),
              (p, _pspec(T, 'd_z'))]
    m_pars = [K['a_out_norm'], K['b_out_norm'], K['c_d_skip'], K['c_out_norm'], K['d_out_norm'], tabs['m64']]
    (o,) = _tw_fwd(n + "mixer_post", _fn_mixer_post, m_acts, m_pars, [(O_COLS, BF16)], L, T)
    f1 = _mm(n + "out_proj", o, K['w_out'].astype(BF16), 'nn', F32, 1024, 1024, 1536)
    r1_pars = [K['post_mix_norm'], K['pre_ffn_norm']]
    x1, h2 = _tw_fwd(n + "resid_mix", _fn_resid_norm2, [(x, _spec2(T, D_MODEL)), (f1, _spec2(T, D_MODEL))], r1_pars,
                     [(D_MODEL, F32), (D_MODEL, BF16)], L, T)
    sv.update(m_acts=m_acts, m_pars=m_pars, o=o, f1=f1, r1_pars=r1_pars, x1=x1, h2=h2)
    u = _mm(n + "ffn_in", h2, K['f_w_in'].astype(BF16), 'nn', F32, 1024, 1408, 1024)
    act = _ffn_gate_fwd(n + "ffn_gate", u, K['f_conv'], L)
    f2 = _mm(n + "ffn_out", act, K['f_w_out'].astype(BF16), 'nn', F32, 1024, 1024, 1408)
    sv.update(u=u, act=act, f2=f2)
    xf = [(x1, _spec2(T, D_MODEL)), (f2, _spec2(T, D_MODEL))]
    if next_norm is None:
        (x2,) = _tw_fwd(n + "resid_ffn", _fn_resid_norm, xf, [K['post_ffn_norm']], [(D_MODEL, F32)], L, T)
        hn = None
    else:
        x2, hn = _tw_fwd(n + "resid_ffn", _fn_resid_norm2, xf, [K['post_ffn_norm'], next_norm],
                         [(D_MODEL, F32), (D_MODEL, BF16)], L, T)
    return x2, hn, sv


def _layer_bwd(l, dx2, dhn, K, sv, tabs, L, T, next_norm, hosts=None):
    n = f"l{l}b_"
    dK = {}
    hosts = hosts or {}
    got = {}
    side = lambda name: hosts[name](dK, got) if name in hosts else None
    s2 = lambda w, cb=0: _spec2(T, w, cb)
    xf = [(sv['x1'], s2(D_MODEL)), (sv['f2'], s2(D_MODEL))]
    if next_norm is None:
        (dx1a, df2), (dK['post_ffn_norm'],) = _tw_bwd(n + "resid_ffn", _fn_resid_norm, xf, [K['post_ffn_norm']],
                                                      [(dx2, s2(D_MODEL))], L, T, [True, True], [True])
        dnext = None
    else:
        (dx1a, df2), (dK['post_ffn_norm'], dnext) = _tw_bwd(
            n + "resid_ffn", _fn_resid_norm2, xf, [K['post_ffn_norm'], next_norm],
            [(dx2, s2(D_MODEL)), (dhn, s2(D_MODEL))], L, T, [True, True], [True, True])
    dact = _mm(n + "ffn_out_dx", df2, K['f_w_out'].astype(BF16), 'nt', F32, 1024, 1408, 1024)
    dK['f_w_out'] = _mm(n + "ffn_out_dw", sv['act'], df2, 'tn', F32, 1408, 1024, 1024)
    du, dK['f_conv'] = _ffn_gate_bwd(n + "ffn_gate", sv['u'], K['f_conv'], dact, L)
    dh2 = _mm(n + "ffn_in_dx", du, K['f_w_in'].astype(BF16), 'nt', F32, 1024, 1024, 1408)
    dK['f_w_in'] = _mm(n + "ffn_in_dw", sv['h2'], du, 'tn', F32, 1024, 1408, 1024)
    (dxa, df1), (dK['post_mix_norm'], dK['pre_ffn_norm']) = _tw_bwd(
        n + "resid_mix", _fn_resid_norm2, [(sv['x'], s2(D_MODEL)), (sv['f1'], s2(D_MODEL))], sv['r1_pars'],
        [(dx1a, s2(D_MODEL)), (dh2, s2(D_MODEL))], L, T, [True, True], [True, True])
    do = _mm(n + "out_proj_dx", df1, K['w_out'].astype(BF16), 'nt', F32, 1024, 1536, 1024)
    dK['w_out'] = _mm(n + "out_proj_dw", sv['o'], df1, 'tn', F32, 1536, 1024, 1024)
    (doa, dob, dyc0, _, dxs_skip, dzc, dod0, _, dzd), mp = _tw_bwd(
        n + "mixer_post", _fn_mixer_post, sv['m_acts'], sv['m_pars'], [(do, s2(O_COLS))], L, T,
        [True] * 9, [True] * 5 + [False])
    dK['a_out_norm'], dK['b_out_norm'], dK['c_d_skip'], dK['c_out_norm'], dK['d_out_norm'] = mp
    (dqkv0, db0, da0), (dqkv1, db1, da1), (dK['d_a_log'], dK['d_dt_bias']), got['d_delta'] = _scan_bwd(
        n + "d_delta", _delta_chunk, sv['d_seqs'], sv['d_rows'], sv['sd'], dod0, None, D_CHUNK, L, D_PER_STEP,
        side('d_delta'))
    dqkv, dK['d_conv'] = _conv_bwd(n + "d_conv", sv['p'], P_OFF['d_qkv'][0], D_QKV, K['d_conv'], True,
                                   [(dqkv0, None), (dqkv1, None)], L, 768)
    (dxbc0, ddt0), (dxbc1, ddt1), (dK['c_a_log'], dK['c_dt_bias']), _ = _scan_bwd(
        n + "c_ssd", _ssd_chunk, sv['c_seqs'], sv['c_rows'], sv['sc'], dyc0, dxs_skip, C_CHUNK, L, C_PER_STEP)
    dxbc, dK['c_conv'] = _conv_bwd(n + "c_conv", sv['p'], P_OFF['c_xbc'][0], C_XBC, K['c_conv'], True,
                                   [(dxbc0, None), (dxbc1, None)], L, 512)
    dqb, dkb, dvb, got['b_attn'] = _flash_bwd(n + "b_attn", sv['qb'], sv['kb'], sv['vb'], sv['ob'], sv['lse_b'],
                                              dob, HEADS, 2, HD ** -0.5, L, side('b_attn'))
    (dbq, dbk, dbv), (dK['b_q_norm'], dK['b_k_norm']) = _tw_bwd(
        n + "b_prep", _fn_b_prep, sv['b_acts'], sv['b_pars'], [(dqb, s2(512)), (dkb, s2(256)), (dvb, s2(256))],
        L, T, [True] * 3 + [False] * 4, [True, True] + [False] * 4)
    dqa, dka, dva, got['a_attn'] = _flash_bwd(n + "a_attn", sv['qa'], sv['ka'], sv['va'], sv['oa'], sv['lse_a'],
                                              doa, HEADS, 1, (A_NOPE + A_ROPE) ** -0.5, L, side('a_attn'))
    (dcq, dckv, dkr), ap = _tw_bwd(
        n + "a_prep", _fn_a_prep, sv['a_acts'], sv['a_pars'], [(dqa, s2(512)), (dka, s2(512)), (dva, s2(512))],
        L, T, [True] * 3 + [False] * 4, [True] * 5 + [False] * 2)
    dK['a_q_norm'], dK['a_w_uq'], dK['a_kv_norm'], dK['a_w_uk'], dK['a_w_uv'] = ap
    pieces = [(dbq, s2(512)), (dxbc, s2(512)), (dcq, s2(256)), (dbk, s2(256)), (dqkv, s2(768)), (dbv, s2(256)),
              (dzc, s2(256)), (dzd, s2(256)), (dckv, s2(LANE)), (dkr, s2(LANE)),
              (ddt0, s2(LANE)), (ddt1, s2(LANE)), (db0, s2(LANE)), (db1, s2(LANE)), (da0, s2(LANE)),
              (da1, s2(LANE))]
    (dp,) = _tw_fwd(n + "assemble_dp", _fn_assemble, pieces, [], [(P_COLS, BF16)], L, T)
    dh = _mm(n + "in_proj_dx", dp, K['w_in'].astype(BF16), 'nt', F32, 1024, 1024, 1280)
    dK['w_in'] = _mm(n + "in_proj_dw", sv['h'], dp, 'tn', F32, 1024, 1280, 1024)
    return dxa, dh, dK, dnext, got


def _tables(L):
    ca, sa = _rope_angles(L, A_ROPE)
    cb, sb = _rope_angles(L, HD)
    t = {}
    t['a_cq'], t['a_sq'], t['a_rq'] = _place_tables(L, ca, sa, 512, [LANE * h + A_NOPE for h in range(4)])
    t['a_ck'], t['a_sk'], t['a_rk'] = _place_tables(L, ca, sa, LANE, [A_NOPE])
    t['b_cq'], t['b_sq'], t['b_rq'] = _place_tables(L, cb, sb, 512, [LANE * h for h in range(4)])
    t['b_ck'], t['b_sk'], t['b_rk'] = _place_tables(L, cb, sb, 256, [LANE * h for h in range(2)])
    t['b_mq'] = _head_mean_matrix(512, LANE, HD)
    t['b_mk'] = _head_mean_matrix(256, LANE, HD)
    t['m64'] = _head_mean_matrix(256, HD, HD)
    return t


def kernel(x, pre_mix_norm, w_in, a_q_norm, a_w_uq, a_kv_norm, a_w_ukv, a_out_norm, b_q_norm, b_k_norm, b_out_norm, c_conv_w, c_conv_b, c_a_log, c_dt_bias, c_d_skip, c_out_norm, d_conv_w, d_a_log, d_dt_bias, d_out_norm, w_out, post_mix_norm, pre_ffn_norm, f_w_in, f_conv_w, f_conv_b, f_w_out, post_ffn_norm, loss_target, m_pre_mix_norm, m_w_in, m_a_q_norm, m_a_w_uq, m_a_kv_norm, m_a_w_ukv, m_a_out_norm, m_b_q_norm, m_b_k_norm, m_b_out_norm, m_c_conv_w, m_c_conv_b, m_c_a_log, m_c_dt_bias, m_c_d_skip, m_c_out_norm, m_d_conv_w, m_d_a_log, m_d_dt_bias, m_d_out_norm, m_w_out, m_post_mix_norm, m_pre_ffn_norm, m_f_w_in, m_f_conv_w, m_f_conv_b, m_f_w_out, m_post_ffn_norm, v_pre_mix_norm, v_w_in, v_a_q_norm, v_a_w_uq, v_a_kv_norm, v_a_w_ukv, v_a_out_norm, v_b_q_norm, v_b_k_norm, v_b_out_norm, v_c_conv_w, v_c_conv_b, v_c_a_log, v_c_dt_bias, v_c_d_skip, v_c_out_norm, v_d_conv_w, v_d_a_log, v_d_dt_bias, v_d_out_norm, v_w_out, v_post_mix_norm, v_pre_ffn_norm, v_f_w_in, v_f_conv_w, v_f_conv_b, v_f_w_out, v_post_ffn_norm):
    loc = locals()
    Wl = {n: loc[n] for n in WEIGHTS}
    Ml = {n: loc['m_' + n] for n in WEIGHTS}
    Vl = {n: loc['v_' + n] for n in WEIGHTS}
    L = x.shape[1]
    T = min(256, L)
    x0 = x.reshape(L, D_MODEL)
    tgt = loss_target.reshape(L, D_MODEL)

    def shards(l):
        return [Wl[n][l].astype(BF16) if n in MXU_WEIGHTS else Wl[n][l] for n in SHARDED]

    def layer_weights(l, gathered):
        W = {n: Wl[n][l] for n in SMALL}
        for n, g in zip(SHARDED, gathered):
            W[n] = jnp.concatenate([g[j] for j in range(4)], axis=SHARD_AXIS[n] - 1)
        return W

    def chip_blocks(g, n):
        return jnp.stack(jnp.split(g, 4, axis=SHARD_AXIS[n] - 1))

    tabs = _tables(L)
    norm_in = [Wl['pre_mix_norm'][l].reshape(1, -1) for l in range(DEPTH)]
    def layer_shape(n):
        s = list(Wl[n].shape[1:])
        if n in SHARD_AXIS:
            s[SHARD_AXIS[n] - 1] *= 4
        return tuple(s)

    unbuild = jax.vjp(_build_layer, {n: jnp.zeros(layer_shape(n), F32) for n in WEIGHTS})[1]

    (h,) = _tw_fwd("l0_norm_in", _fn_norm_in, [(x0, _spec2(T, D_MODEL))], [norm_in[0]], [(D_MODEL, BF16)], L, T)
    gathered = _gather_copies(shards(0)).call("gather_l0")
    xs, saves, Ks = x0, [], []
    for l in range(DEPTH):
        Ks.append(_build_layer(layer_weights(l, gathered)))
        last = l + 1 == DEPTH
        xs, h, sv = _layer_fwd(l, xs, h, Ks[l], tabs, L, T, None if last else norm_in[l + 1],
                               None if last else _gather_copies(shards(l + 1)))
        gathered = sv['side']
        saves.append(sv)
    dy, loss_acc = _loss_call(xs, tgt, L)
    loss = lax.psum(loss_acc[0, 0], ("x", "y", "c"))

    ffn = ['f_w_in', 'f_conv_w', 'f_w_out', 'w_out']
    rest = [n for n in SHARDED if n not in ffn]

    def ffn_side(dK):
        only_w_out = {k: (dK[k] if k == 'w_out' else jnp.zeros(v.shape, F32)) for k, v in Ks[0].items()}
        g = {'f_w_in': dK['f_w_in'], 'f_conv_w': dK['f_conv'][0:3], 'f_w_out': dK['f_w_out'],
             'w_out': unbuild(only_w_out)[0]['w_out']}
        return _scatter_copies([chip_blocks(g[n], n) for n in ffn], None)

    def rest_blocks(dK):
        full = dict(dK)
        full.setdefault('pre_mix_norm', jnp.zeros((1, D_MODEL), F32))
        (g,) = unbuild(full)
        return [chip_blocks(g[n], n) for n in rest]

    def chip_sums(l, names, recvs):
        return [_sum_slots(f"sum_{n}_{l}", r.reshape(4, -1, r.shape[-1])) for n, r in zip(names, recvs)]

    grads = [None] * DEPTH
    pairs = {}
    dx, dhn = dy, None
    for l in reversed(range(DEPTH)):
        last = l + 1 == DEPTH

        def host_scatter(dK, got, up=None if last else grads[l + 1]):
            sets = [ffn_side(dK)] + ([] if up is None else [_scatter_copies(rest_blocks(up), None)])
            return _merge_copies(sets)

        def host_swap(dK, got, l=l, last=last):
            r = got['d_delta']
            parts = chip_sums(l, ffn, r[:len(ffn)]) + ([] if last else chip_sums(l + 1, rest, r[len(ffn):]))
            return _swap_copies(parts)

        dxa, dh, dK, dnext, got = _layer_bwd(l, dx, dhn, Ks[l], saves[l], tabs, L, T,
                                             None if last else norm_in[l + 1],
                                             {'d_delta': host_scatter, 'b_attn': host_swap})
        pairs.update({(l, n): p for n, p in zip(ffn, got['b_attn'])})
        if not last:
            pairs.update({(l + 1, n): p for n, p in zip(rest, got['b_attn'][len(ffn):])})
            grads[l + 1]['pre_mix_norm'] = dnext
        grads[l] = dK
        dx, dhn = dxa, dh
    (dx_in,), (grads[0]['pre_mix_norm'],) = _tw_bwd(
        "l0b_norm_in", _fn_norm_in, [(x0, _spec2(T, D_MODEL))], [norm_in[0]], [(dhn, _spec2(T, D_MODEL))], L, T,
        [True], [True], addto={0: (dx, _spec2(T, D_MODEL))})
    small_shapes = [Wl[n].shape for n in SMALL]
    gfull = [unbuild(grads[l])[0] for l in range(DEPTH)]
    gs = _pack([jnp.stack([gfull[l][n] for l in range(DEPTH)]) for n in SMALL], LANE, 8)
    *got0, recv_small = _scatter_copies([b.astype(BF16) for b in rest_blocks(grads[0])], gs).call("scatter_last")
    pairs.update({(0, n): p for n, p in zip(rest, _swap_copies(chip_sums(0, rest, got0)).call("swap_last"))})

    kinds = ['grad', 'delta', 'new_m', 'new_v']
    res = {}
    for n in SHARDED:
        upd = _adamw_call("adamw_" + n, [pairs[l, n] for l in range(DEPTH)], Wl[n], Ml[n], Vl[n])
        for kind, a in zip(kinds, upd):
            res[kind, n] = a
    small = _adamw_call("adamw_small", [recv_small], *[_pack([W_[n] for n in SMALL], LANE, 8)[None]
                                                       for W_ in (Wl, Ml, Vl)])
    for kind, s in zip(kinds, small):
        for n, a in zip(SMALL, _unpack(s, small_shapes)):
            res[kind, n] = a
    outs = [loss, dx_in.reshape(x.shape)]
    for kind in ['grad', 'delta', 'new_m', 'new_v']:
        outs += [res[kind, n] for n in WEIGHTS]
    return tuple(outs)
```

```python
import functools
import math

import numpy as np
import jax
import jax.numpy as jnp
from jax import lax
from jax.experimental import pallas as pl
from jax.experimental.pallas import tpu as pltpu

F32 = jnp.float32
BF16 = jnp.bfloat16
MESH = pl.DeviceIdType.MESH
VMEM_LIMIT = 48 * 1024 * 1024
LANE = 128

D_MODEL = 1024
DEPTH = 2
GRID_W = 64
ROPE_BASE = 10000.0
EPS = 1e-6
GROUP_W = 256
HEADS = 4
HD = 64
A_NOPE, A_ROPE, A_Q_LORA, A_KV_LORA = 64, 32, 192, 128
A_COLS = A_Q_LORA + A_KV_LORA + A_ROPE
B_COLS = 512
C_XBC = 512
C_COLS = GROUP_W + C_XBC + 8
D_QKV = 768
D_COLS = D_QKV + GROUP_W + 16
IN_COLS = A_COLS + B_COLS + C_COLS + D_COLS
C_CHUNK = 128
D_CHUNK = 64
C_PER_STEP = 1
D_PER_STEP = 2
D_FF = 2816
ADAM_LR, ADAM_B1, ADAM_B2, ADAM_EPS, ADAM_WD, ADAM_STEP = 0.001, 0.9, 0.999, 1e-08, 0.01, 10

WEIGHTS = ['pre_mix_norm', 'w_in', 'a_q_norm', 'a_w_uq', 'a_kv_norm', 'a_w_ukv', 'a_out_norm', 'b_q_norm',
           'b_k_norm', 'b_out_norm', 'c_conv_w', 'c_conv_b', 'c_a_log', 'c_dt_bias', 'c_d_skip', 'c_out_norm',
           'd_conv_w', 'd_a_log', 'd_dt_bias', 'd_out_norm', 'w_out', 'post_mix_norm', 'pre_ffn_norm', 'f_w_in',
           'f_conv_w', 'f_conv_b', 'f_w_out', 'post_ffn_norm']
SHARD_AXIS = {'w_in': 2, 'a_w_uq': 2, 'a_w_ukv': 2, 'c_conv_w': 2, 'd_conv_w': 2, 'w_out': 1, 'f_w_in': 2,
              'f_conv_w': 2, 'f_w_out': 1}
SHARDED = [n for n in WEIGHTS if n in SHARD_AXIS]
SMALL = [n for n in WEIGHTS if n not in SHARD_AXIS]
MXU_WEIGHTS = ('w_in', 'a_w_uq', 'a_w_ukv', 'w_out', 'f_w_in', 'f_w_out')

P_LAYOUT = [('b_q', 0, 512), ('c_xbc', 512, 512), ('a_cq', 1024, 256), ('b_k', 1280, 256), ('d_qkv', 1536, 768),
            ('b_v', 2304, 256), ('c_z', 2560, 256), ('d_z', 2816, 256), ('a_ckv', 3072, 128), ('a_kr', 3200, 128),
            ('c_dt', 3328, 128), ('d_b', 3456, 128), ('d_a', 3584, 128), ('pad', 3712, 128)]
P_OFF = {n: (o, w) for n, o, w in P_LAYOUT}
P_COLS = 3840
O_COLS = 1536


def _cparams(sem):
    return pltpu.CompilerParams(dimension_semantics=sem, vmem_limit_bytes=VMEM_LIMIT)


def _tile(n, target):
    best = None
    for d in range(LANE, min(n, target) + 1, LANE):
        if n % d == 0:
            best = d
    return best if best is not None else n


_NN = ((1,), (0,))
_NT = ((1,), (1,))
_TN = ((0,), (0,))


def _raw_dot(a, b, dims, hi):
    if hi:
        prec = lax.Precision.HIGH if hi == 'high' else lax.Precision.HIGHEST
        return lax.dot_general(a, b, (dims, ((), ())), precision=prec, preferred_element_type=F32)
    return lax.dot_general(a.astype(BF16), b.astype(BF16), (dims, ((), ())), preferred_element_type=F32)


def _make_dots(hi):
    @jax.custom_vjp
    def nn(a, b):
        return _raw_dot(a, b, _NN, hi)

    @jax.custom_vjp
    def nt(a, b):
        return _raw_dot(a, b, _NT, hi)

    @jax.custom_vjp
    def tn(a, b):
        return _raw_dot(a, b, _TN, hi)

    nn.defvjp(lambda a, b: (nn(a, b), (a, b)), lambda r, g: (nt(g, r[1]), tn(r[0], g)))
    nt.defvjp(lambda a, b: (nt(a, b), (a, b)), lambda r, g: (nn(g, r[1]), tn(g, r[0])))
    tn.defvjp(lambda a, b: (tn(a, b), (a, b)), lambda r, g: (nt(r[1], g), nn(r[0], g)))
    return nn, nt, tn


_nn, _nt, _tn = _make_dots(False)
_nn_hi, _nt_hi, _tn_hi = _make_dots(True)
_nn_h3, _nt_h3, _tn_h3 = _make_dots('high')


def _sigmoid(x):
    return 1.0 / (1.0 + jnp.exp(-x))


def _silu(x):
    return x * _sigmoid(x)


def _softplus(x):
    return jnp.maximum(x, 0.0) + jnp.log(1.0 + jnp.exp(-jnp.abs(x)))


def _rms(x, w, n=None):
    n = x.shape[-1] if n is None else n
    ms = jnp.sum(x * x, axis=-1, keepdims=True) * (1.0 / n)
    return x * lax.rsqrt(ms + EPS) * w


def _spec2(T, w, cb=0):
    return pl.BlockSpec((T, w), lambda i: (i, cb))


def _full_spec(a):
    nd = a.ndim
    return pl.BlockSpec(a.shape, lambda i: (0,) * nd)


def _tw_fwd(name, fn, acts, params, outs, L, T):
    na, npar = len(acts), len(params)

    def kern(*refs):
        a = [r[...].astype(F32) for r in refs[:na]]
        p = [r[...].astype(F32) for r in refs[na:na + npar]]
        res = fn(a, p)
        for r, o in zip(refs[na + npar:], res):
            r[...] = o.astype(r.dtype)

    return pl.pallas_call(
        kern, name=name, grid=(L // T,),
        in_specs=[s for _, s in acts] + [_full_spec(p) for p in params],
        out_specs=[_spec2(T, w) for w, _ in outs],
        out_shape=[jax.ShapeDtypeStruct((L, w), dt) for w, dt in outs],
        compiler_params=_cparams(("arbitrary",)),
    )(*[a for a, _ in acts], *params)


def _tw_bwd(name, fn, acts, params, douts, L, T, act_grad, par_grad, addto=None):
    na, npar, nd = len(acts), len(params), len(douts)
    addto = addto or {}
    add_keys = sorted(addto)
    ga = [k for k in range(na) if act_grad[k]]
    gp = [k for k in range(npar) if par_grad[k]]

    def kern(*refs):
        i = pl.program_id(0)
        a = [r[...].astype(F32) for r in refs[:na]]
        p = [r[...].astype(F32) for r in refs[na:na + npar]]
        g = [r[...].astype(F32) for r in refs[na + npar:na + npar + nd]]
        pos = na + npar + nd
        adds = [r[...].astype(F32) for r in refs[pos:pos + len(add_keys)]]
        pos += len(add_keys)
        da_refs = refs[pos:pos + len(ga)]
        dp_refs = refs[pos + len(ga):]

        def f(ad, pd):
            af, pf = list(a), list(p)
            for k, v in zip(ga, ad):
                af[k] = v
            for k, v in zip(gp, pd):
                pf[k] = v
            return fn(af, pf)

        _, vjp = jax.vjp(f, [a[k] for k in ga], [p[k] for k in gp])
        dad, dpd = vjp(list(g))
        for n, (r, d) in enumerate(zip(da_refs, dad)):
            if n in addto:
                d = d + adds[add_keys.index(n)]
            r[...] = d.astype(r.dtype)

        @pl.when(i == 0)
        def _():
            for r in dp_refs:
                r[...] = jnp.zeros(r.shape, F32)

        for r, d in zip(dp_refs, dpd):
            r[...] += d

    def width(spec):
        return spec.block_shape[-1]

    res = pl.pallas_call(
        kern, name=name, grid=(L // T,),
        in_specs=[s for _, s in acts] + [_full_spec(p) for p in params] + [s for _, s in douts]
        + [addto[k][1] for k in add_keys],
        out_specs=[_spec2(T, width(acts[k][1])) for k in ga] + [_full_spec(params[k]) for k in gp],
        out_shape=[jax.ShapeDtypeStruct((L, width(acts[k][1])), F32) for k in ga]
        + [jax.ShapeDtypeStruct(params[k].shape, F32) for k in gp],
        compiler_params=_cparams(("arbitrary",)),
    )(*[a for a, _ in acts], *params, *[a for a, _ in douts], *[addto[k][0] for k in add_keys])
    return list(res[:len(ga)]), list(res[len(ga):])


def _mm(name, a, b, mode, out_dtype, tm, tn, tk):
    halves_a = a.shape[-1] if (a.ndim == 3 and mode == 'nt') else None
    halves_b = b.shape[-1] if (b.ndim == 3 and mode == 'tn') else None
    if mode == 'nn':
        (M, K), N = a.shape, b.shape[1]
    elif mode == 'nt':
        M, K, N = a.shape[-2], (2 * halves_a if halves_a else a.shape[1]), b.shape[0]
    else:
        (K, M), N = a.shape, (2 * halves_b if halves_b else b.shape[1])
    tm = _tile(M, tm)
    tn = _tile(halves_b or N, tn)
    tk = _tile(halves_a or K, tk)
    nk = K // tk
    if mode == 'nn':
        a_spec = pl.BlockSpec((tm, tk), lambda i, j, k: (i, k))
        b_spec = pl.BlockSpec((tk, tn), lambda i, j, k: (k, j))
        dims = _NN
    elif mode == 'nt':
        a_spec = pl.BlockSpec((tm, tk), lambda i, j, k: (i, k))
        if halves_a:
            per = halves_a // tk
            a_spec = pl.BlockSpec((None, tm, tk), lambda i, j, k: (k // per, i, k % per))
        b_spec = pl.BlockSpec((tn, tk), lambda i, j, k: (j, k))
        dims = _NT
    else:
        a_spec = pl.BlockSpec((tk, tm), lambda i, j, k: (k, i))
        b_spec = pl.BlockSpec((tk, tn), lambda i, j, k: (k, j))
        if halves_b:
            per = halves_b // tn
            b_spec = pl.BlockSpec((None, tk, tn), lambda i, j, k: (j // per, k, j % per))
        dims = _TN

    def kern(a_ref, b_ref, o_ref, acc):
        k = pl.program_id(2)

        @pl.when(k == 0)
        def _():
            acc[...] = jnp.zeros(acc.shape, F32)

        acc[...] += lax.dot_general(a_ref[...].astype(BF16), b_ref[...].astype(BF16), (dims, ((), ())),
                                    preferred_element_type=F32)

        @pl.when(k == nk - 1)
        def _():
            o_ref[...] = acc[...].astype(o_ref.dtype)

    return pl.pallas_call(
        kern, name=name, grid=(M // tm, N // tn, nk),
        in_specs=[a_spec, b_spec],
        out_specs=pl.BlockSpec((tm, tn), lambda i, j, k: (i, j)),
        out_shape=jax.ShapeDtypeStruct((M, N), out_dtype),
        scratch_shapes=[pltpu.VMEM((tm, tn), F32)],
        compiler_params=_cparams(("arbitrary", "arbitrary", "arbitrary")),
    )(a, b)


def _host(kern, n_in, n_out, side, grid, n_scratch=0):
    if side is None:
        return kern, [], [], [], [], []
    ni, no = len(side.ins), len(side.out_shapes)

    def hosted(*refs):
        ins, s_in = refs[:n_in], refs[n_in:n_in + ni]
        pos = n_in + ni
        outs, s_out = refs[pos:pos + n_out], refs[pos + n_out:pos + n_out + no]
        pos += n_out + no
        own, sems = refs[pos:pos + n_scratch], refs[pos + n_scratch:]
        ids = [pl.program_id(d) for d in range(len(grid))]
        first = functools.reduce(lambda a, b: a & b, [i == 0 for i in ids])
        last = functools.reduce(lambda a, b: a & b, [i == g - 1 for i, g in zip(ids, grid)])

        @pl.when(first)
        def _():
            side.start(s_in, s_out, sems)

        kern(*ins, *outs, *own)

        @pl.when(last)
        def _():
            side.finish(s_in, s_out, sems)

    return hosted, [_ANY] * ni, [_ANY] * no, side.out_shapes, side.scratch(), side.ins


def _flash_fwd(name, q, k, v, H, rep, scale, L, side=None):
    tq = min(512, L)
    nq = L // tq
    KC = min(2048, L)
    nkc = L // KC
    log2e = 1.0 / math.log(2.0)

    def kern(q_ref, k_ref, v_ref, o_ref, lse_ref):
        qb = q_ref[...]
        m = jnp.full((tq, 1), -1e30, F32)
        l = jnp.zeros((tq, 1), F32)
        acc = jnp.zeros((tq, LANE), F32)
        for c in range(nkc):
            kb = k_ref[c * KC:(c + 1) * KC, :]
            vb = v_ref[c * KC:(c + 1) * KC, :]
            s = lax.dot_general(qb, kb, (_NT, ((), ())), preferred_element_type=F32) * (scale * log2e)
            mn = jnp.maximum(m, jnp.max(s, axis=-1, keepdims=True))
            al = jnp.exp2(m - mn)
            p = jnp.exp2(s - mn)
            l = al * l + jnp.sum(p, axis=-1, keepdims=True)
            acc = al * acc + lax.dot_general(p.astype(BF16), vb, (_NN, ((), ())), preferred_element_type=F32)
            m = mn
        o_ref[...] = acc / l
        lse_ref[...] = m * math.log(2.0) + jnp.log(l)

    kern, s_in, s_out, s_shapes, s_scratch, s_args = _host(kern, 3, 2, side, (H, nq))
    res = pl.pallas_call(
        kern, name=name, grid=(H, nq),
        in_specs=[pl.BlockSpec((tq, LANE), lambda h, i: (i, h)),
                  pl.BlockSpec((L, LANE), lambda h, i: (0, h // rep)),
                  pl.BlockSpec((L, LANE), lambda h, i: (0, h // rep))] + s_in,
        out_specs=[pl.BlockSpec((tq, LANE), lambda h, i: (i, h)),
                   pl.BlockSpec((tq, 1), lambda h, i: (h * nq + i, 0))] + s_out,
        out_shape=[jax.ShapeDtypeStruct((L, H * LANE), F32), jax.ShapeDtypeStruct((H * L, 1), F32)] + s_shapes,
        scratch_shapes=s_scratch,
        compiler_params=_cparams(("arbitrary", "arbitrary")),
    )(q, k, v, *s_args)
    return res[0], res[1], list(res[2:])


def _flash_bwd(name, q, k, v, o, lse, do, H, rep, scale, L, side=None):
    tq = min(256, L)
    nq = L // tq
    KC = min(2048, L)
    nkc = L // KC
    Hkv = H // rep

    def kern(q_ref, k_ref, v_ref, o_ref, lse_ref, do_ref, dq_ref, dk_ref, dv_ref):
        h = pl.program_id(0)
        i = pl.program_id(1)

        @pl.when((i == 0) & (h % rep == 0))
        def _():
            dk_ref[...] = jnp.zeros(dk_ref.shape, F32)
            dv_ref[...] = jnp.zeros(dv_ref.shape, F32)

        qb = q_ref[...]
        do = do_ref[...]
        dob = do.astype(BF16)
        delta = jnp.sum(do * o_ref[...], axis=-1, keepdims=True)
        lse = lse_ref[...]
        dq = jnp.zeros((tq, LANE), F32)
        for c in range(nkc):
            sl = slice(c * KC, (c + 1) * KC)
            kb = k_ref[sl, :]
            vb = v_ref[sl, :]
            s = lax.dot_general(qb, kb, (_NT, ((), ())), preferred_element_type=F32) * scale
            p = jnp.exp(s - lse)
            dp = lax.dot_general(dob, vb, (_NT, ((), ())), preferred_element_type=F32)
            ds = (p * (dp - delta) * scale).astype(BF16)
            dq = dq + lax.dot_general(ds, kb, (_NN, ((), ())), preferred_element_type=F32)
            dk_ref[sl, :] += lax.dot_general(ds, qb, (_TN, ((), ())), preferred_element_type=F32)
            dv_ref[sl, :] += lax.dot_general(p.astype(BF16), dob, (_TN, ((), ())), preferred_element_type=F32)
        dq_ref[...] = dq

    kern, s_in, s_out, s_shapes, s_scratch, s_args = _host(kern, 6, 3, side, (H, nq))
    res = pl.pallas_call(
        kern, name=name, grid=(H, nq),
        in_specs=[pl.BlockSpec((tq, LANE), lambda h, i: (i, h)),
                  pl.BlockSpec((L, LANE), lambda h, i: (0, h // rep)),
                  pl.BlockSpec((L, LANE), lambda h, i: (0, h // rep)),
                  pl.BlockSpec((tq, LANE), lambda h, i: (i, h)),
                  pl.BlockSpec((tq, 1), lambda h, i: (h * nq + i, 0)),
                  pl.BlockSpec((tq, LANE), lambda h, i: (i, h))] + s_in,
        out_specs=[pl.BlockSpec((tq, LANE), lambda h, i: (i, h)),
                   pl.BlockSpec((L, LANE), lambda h, i: (0, h // rep)),
                   pl.BlockSpec((L, LANE), lambda h, i: (0, h // rep))] + s_out,
        out_shape=[jax.ShapeDtypeStruct((L, H * LANE), F32), jax.ShapeDtypeStruct((L, Hkv * LANE), F32),
                   jax.ShapeDtypeStruct((L, Hkv * LANE), F32)] + s_shapes,
        scratch_shapes=s_scratch,
        compiler_params=_cparams(("arbitrary", "arbitrary")),
    )(q, k, v, o, lse, do, *s_args)
    return res[0], res[1], res[2], list(res[3:])


def _shift_dn(x, first_row):
    row = lax.broadcasted_iota(jnp.int32, x.shape, 0)
    return jnp.where(row == 0, first_row, pltpu.roll(x, 1, 0))


def _shift_up(x, last_row):
    n = x.shape[0]
    row = lax.broadcasted_iota(jnp.int32, x.shape, 0)
    return jnp.where(row == n - 1, last_row, pltpu.roll(x, n - 1, 0))


def _halo_specs(ndim, lead, T, tc, cb0, L):
    r8 = T // 8
    last8 = L // 8 - 1
    if ndim == 2:
        return [pl.BlockSpec((T, tc), lambda j, i: (i, cb0 + j)),
                pl.BlockSpec((8, tc), lambda j, i: (jnp.maximum(i * r8 - 1, 0), cb0 + j)),
                pl.BlockSpec((8, tc), lambda j, i: (jnp.minimum((i + 1) * r8, last8), cb0 + j))]
    return [pl.BlockSpec((None, T, tc), lambda j, i: (lead, i, cb0 + j)),
            pl.BlockSpec((None, 8, tc), lambda j, i: (lead, jnp.maximum(i * r8 - 1, 0), cb0 + j)),
            pl.BlockSpec((None, 8, tc), lambda j, i: (lead, jnp.minimum((i + 1) * r8, last8), cb0 + j))]


def _conv_rows(x_ref, xp_ref, xn_ref, w, first, last):
    x = x_ref[...]
    T = x.shape[0]
    w0, w1, w2, b = w[0:1], w[1:2], w[2:3], w[3:4]
    pr = jnp.where(first, 0.0, xp_ref[7:8, :])
    pr2 = jnp.where(first, 0.0, xp_ref[6:7, :])
    nr = jnp.where(last, 0.0, xn_ref[0:1, :])
    nr2 = jnp.where(last, 0.0, xn_ref[1:2, :])
    xm1 = _shift_dn(x, pr)
    xp1 = _shift_up(x, nr)
    pre = xm1 * w0 + x * w1 + xp1 * w2 + b
    pre_m1 = pr2 * w0 + pr * w1 + x[0:1] * w2 + b
    pre_T = x[T - 1:T] * w0 + nr * w1 + nr2 * w2 + b
    return x, xm1, xp1, pre, pre_m1, pre_T


def _conv_grads(dpre, dpre_m1, dpre_T, x, xm1, xp1, w):
    dx = _shift_up(dpre, dpre_T) * w[0:1] + dpre * w[1:2] + _shift_dn(dpre, dpre_m1) * w[2:3]
    row = lax.broadcasted_iota(jnp.int32, (8, x.shape[1]), 0)
    dw = (jnp.where(row == 0, jnp.sum(dpre * xm1, axis=0, keepdims=True), 0.0)
          + jnp.where(row == 1, jnp.sum(dpre * x, axis=0, keepdims=True), 0.0)
          + jnp.where(row == 2, jnp.sum(dpre * xp1, axis=0, keepdims=True), 0.0)
          + jnp.where(row == 3, jnp.sum(dpre, axis=0, keepdims=True), 0.0))
    return dx, dw


def _conv_fwd(name, x, col0, C, w8, act, L, tc):
    T = min(256, L)
    nt = L // T
    cb0 = col0 // tc

    def kern(x_ref, xp_ref, xn_ref, w_ref, o_ref):
        i = pl.program_id(1)
        x = x_ref[...]
        w = w_ref[...]
        pr = jnp.where(i == 0, 0.0, xp_ref[7:8, :])
        nr = jnp.where(i == nt - 1, 0.0, xn_ref[0:1, :])
        pre = _shift_dn(x, pr) * w[0:1] + x * w[1:2] + _shift_up(x, nr) * w[2:3] + w[3:4]
        o_ref[...] = _silu(pre) if act else pre

    return pl.pallas_call(
        kern, name=name, grid=(C // tc, nt),
        in_specs=_halo_specs(2, None, T, tc, cb0, L) + [pl.BlockSpec((8, tc), lambda j, i: (0, j))],
        out_specs=pl.BlockSpec((T, tc), lambda j, i: (i, j)),
        out_shape=jax.ShapeDtypeStruct((L, C), F32),
        compiler_params=_cparams(("arbitrary", "arbitrary")),
    )(x, x, x, w8)


def _conv_bwd(name, x, col0, C, w8, act, gs, L, tc):
    T = min(256, L)
    nt = L // T
    cb0 = col0 // tc
    ng = len(gs)

    def dact(pre, g):
        if not act:
            return g
        s = _sigmoid(pre)
        return g * (s * (1.0 + pre * (1.0 - s)))

    def kern(*refs):
        x_ref, xp_ref, xn_ref, w_ref = refs[:4]
        g_refs = refs[4:4 + 3 * ng]
        dx_ref, dw_ref = refs[4 + 3 * ng:]
        i = pl.program_id(1)
        first = i == 0
        last = i == nt - 1
        w = w_ref[...]
        g = g_refs[0][...]
        gp = g_refs[1][7:8, :]
        gn = g_refs[2][0:1, :]
        for n in range(1, ng):
            g = g + g_refs[3 * n][...]
            gp = gp + g_refs[3 * n + 1][7:8, :]
            gn = gn + g_refs[3 * n + 2][0:1, :]
        x, xm1, xp1, pre, pre_m1, pre_T = _conv_rows(x_ref, xp_ref, xn_ref, w, first, last)
        dpre_m1 = jnp.where(first, 0.0, dact(pre_m1, gp))
        dpre_T = jnp.where(last, 0.0, dact(pre_T, gn))
        dx_ref[...], dw = _conv_grads(dact(pre, g), dpre_m1, dpre_T, x, xm1, xp1, w)

        @pl.when(first)
        def _():
            dw_ref[...] = jnp.zeros((8, tc), F32)

        dw_ref[...] += dw

    g_specs, g_args = [], []
    for arr, lead in gs:
        g_specs += _halo_specs(arr.ndim, lead, T, tc, 0, L)
        g_args += [arr, arr, arr]
    return pl.pallas_call(
        kern, name=name, grid=(C // tc, nt),
        in_specs=_halo_specs(2, None, T, tc, cb0, L) + [pl.BlockSpec((8, tc), lambda j, i: (0, j))] + g_specs,
        out_specs=[pl.BlockSpec((T, tc), lambda j, i: (i, j)), pl.BlockSpec((8, tc), lambda j, i: (0, j))],
        out_shape=[jax.ShapeDtypeStruct((L, C), F32), jax.ShapeDtypeStruct((8, C), F32)],
        compiler_params=_cparams(("arbitrary", "arbitrary")),
    )(x, x, x, w8, *g_args)


FFN_TC = 1408


def _ffn_gate_fwd(name, u, w8, L):
    T = min(256, L)
    nt = L // T
    ncb = D_FF // FFN_TC

    def kern(xg, xgp, xgn, xu, xup, xun, wg_ref, wu_ref, o_ref):
        i = pl.program_id(1)
        pre_g = _conv_rows(xg, xgp, xgn, wg_ref[...], i == 0, i == nt - 1)[3]
        pre_u = _conv_rows(xu, xup, xun, wu_ref[...], i == 0, i == nt - 1)[3]
        o_ref[...] = (_silu(pre_g) * pre_u).astype(BF16)

    return pl.pallas_call(
        kern, name=name, grid=(ncb, nt),
        in_specs=_halo_specs(2, None, T, FFN_TC, 0, L) + _halo_specs(2, None, T, FFN_TC, ncb, L)
        + [pl.BlockSpec((8, FFN_TC), lambda j, i: (0, j)), pl.BlockSpec((8, FFN_TC), lambda j, i: (0, j + ncb))],
        out_specs=pl.BlockSpec((T, FFN_TC), lambda j, i: (i, j)),
        out_shape=jax.ShapeDtypeStruct((L, D_FF), BF16),
        compiler_params=_cparams(("arbitrary", "arbitrary")),
    )(u, u, u, u, u, u, w8, w8)


def _ffn_gate_bwd(name, u, w8, da, L):
    T = min(128, L)
    nt = L // T
    ncb = D_FF // FFN_TC

    def kern(xg, xgp, xgn, xu, xup, xun, wg_ref, wu_ref, d_ref, dp_ref, dn_ref, du_ref, dwg_ref, dwu_ref):
        i = pl.program_id(1)
        first = i == 0
        last = i == nt - 1
        wg = wg_ref[...]
        wu = wu_ref[...]
        g, gm1, gp1, pg, pg_m1, pg_T = _conv_rows(xg, xgp, xgn, wg, first, last)
        v, vm1, vp1, pu, pu_m1, pu_T = _conv_rows(xu, xup, xun, wu, first, last)

        def dpre(pg_, pu_, d):
            s = _sigmoid(pg_)
            return d * pu_ * (s * (1.0 + pg_ * (1.0 - s))), d * (pg_ * s)

        dg, dv = dpre(pg, pu, d_ref[...])
        dg_m1, dv_m1 = dpre(pg_m1, pu_m1, jnp.where(first, 0.0, dp_ref[7:8, :]))
        dg_T, dv_T = dpre(pg_T, pu_T, jnp.where(last, 0.0, dn_ref[0:1, :]))
        du_ref[0], dwg = _conv_grads(dg, dg_m1, dg_T, g, gm1, gp1, wg)
        du_ref[1], dwu = _conv_grads(dv, dv_m1, dv_T, v, vm1, vp1, wu)

        @pl.when(first)
        def _():
            dwg_ref[...] = jnp.zeros(dwg_ref.shape, F32)
            dwu_ref[...] = jnp.zeros(dwu_ref.shape, F32)

        dwg_ref[...] += dwg
        dwu_ref[...] += dwu

    wspec = pl.BlockSpec((8, FFN_TC), lambda j, i: (0, j))
    du, dwg, dwu = pl.pallas_call(
        kern, name=name, grid=(ncb, nt),
        in_specs=_halo_specs(2, None, T, FFN_TC, 0, L) + _halo_specs(2, None, T, FFN_TC, ncb, L)
        + [wspec, pl.BlockSpec((8, FFN_TC), lambda j, i: (0, j + ncb))] + _halo_specs(2, None, T, FFN_TC, 0, L),
        out_specs=[pl.BlockSpec((2, T, FFN_TC), lambda j, i: (0, i, j)), wspec, wspec],
        out_shape=[jax.ShapeDtypeStruct((2, L, D_FF), F32), jax.ShapeDtypeStruct((8, D_FF), F32),
                   jax.ShapeDtypeStruct((8, D_FF), F32)],
        compiler_params=_cparams(("arbitrary", "arbitrary")),
    )(u, u, u, u, u, u, w8, w8, da, da, da)
    return du, jnp.concatenate([dwg, dwu], axis=1)


def _masks(Q, rev):
    ri = lax.broadcasted_iota(jnp.int32, (Q, Q), 0)
    ci = lax.broadcasted_iota(jnp.int32, (Q, Q), 1)
    diff = (ri - ci) * (1 - 2 * rev)
    return diff >= 0, diff > 0


def _lane_pick(v, sel):
    return jnp.sum(v * sel, axis=-1, keepdims=True)


def _head_rows(v_all, Q, rev):
    r = lax.broadcasted_iota(jnp.int32, (HEADS * Q, LANE), 0)
    l = lax.broadcasted_iota(jnp.int32, (HEADS * Q, LANE), 1)
    pick = jnp.zeros((HEADS * Q, LANE), F32)
    for h in range(HEADS):
        pick = jnp.where((r >= h * Q) & (r < (h + 1) * Q) & (l == rev * 4 + h), 1.0, pick)
    return _nt_hi(pick, v_all)


def _ssd_chunk(S, x, B, C, dtraw, alog, dtb, rev, kept=None):
    Q = dtraw.shape[0]
    incl, _ = _masks(Q, rev)
    tri = incl.astype(F32)
    dt = _softplus(dtraw + dtb)
    a_all = dt * (-jnp.exp(alog))
    acum_all = _nn_hi(tri, a_all)
    total_all = jnp.sum(a_all, axis=0, keepdims=True)
    lane = lax.broadcasted_iota(jnp.int32, (1, LANE), 1)
    rows_all = _head_rows(acum_all, Q, rev)
    ys, Sn = [], []
    for h in range(HEADS):
        g = h // 2
        sel = (lane == rev * 4 + h).astype(F32)
        acum = _lane_pick(acum_all, sel)
        dth = _lane_pick(dt, sel)
        tot = _lane_pick(total_all, sel)
        seg = acum - rows_all[h * Q:(h + 1) * Q, :]
        decay = jnp.exp(jnp.where(incl, seg, -1e30))
        xdt = x[h] * dth
        Sh = S[HD * h:HD * (h + 1), :]
        scores = _nt(C[g], B[g]) * decay
        y_diag = _nn(scores, xdt)
        states = _tn(xdt, B[g] * jnp.exp(tot - acum))
        y_off = _nt(C[g], Sh) * jnp.exp(acum)
        ys.append(y_diag + y_off)
        Sn.append(Sh * jnp.exp(tot) + states)
    return ys, jnp.concatenate(Sn, axis=0), []


def _inv_unit_raw(Lm):
    N = Lm.shape[0]
    Q = D_CHUNK
    ri = lax.broadcasted_iota(jnp.int32, (N, N), 0)
    ci = lax.broadcasted_iota(jnp.int32, (N, N), 1)
    X = (ri == ci).astype(F32) - Lm
    P = _raw_dot(Lm, Lm, _NN, False)
    n = 2
    while n < Q:
        X = X + _raw_dot(X, P, _NN, False)
        n *= 2
        if n < Q:
            P = _raw_dot(P, P, _NN, False)
    return X


@jax.custom_vjp
def _inv_unit(Lm, T_saved):
    return _inv_unit_raw(Lm) if T_saved is None else T_saved


def _inv_unit_f(Lm, T_saved):
    T = _inv_unit_raw(Lm) if T_saved is None else T_saved
    return T, T


def _inv_unit_b(T, g):
    return -_raw_dot(_raw_dot(T, g, _TN, False), T, _NT, False), None


_inv_unit.defvjp(_inv_unit_f, _inv_unit_b)


def _delta_chunk(S, q, k, v, braw, araw, alog, dtb, rev, kept=None):
    Q = braw.shape[0]
    N = HEADS * Q
    tri = _masks(Q, rev)[0].astype(F32)
    ri = lax.broadcasted_iota(jnp.int32, (N, N), 0)
    ci = lax.broadcasted_iota(jnp.int32, (N, N), 1)
    sh = int(math.log2(Q))
    same = (ri >> sh) == (ci >> sh)
    diff = (ri - ci) * (1 - 2 * rev)
    incl = same & (diff >= 0)
    strict = same & (diff > 0)
    beta_all = _sigmoid(braw)
    g_all = -jnp.exp(alog) * _softplus(araw + dtb)
    G_all = _nn_hi(tri, g_all)
    Gtot_all = jnp.sum(g_all, axis=0, keepdims=True)
    r = lax.broadcasted_iota(jnp.int32, (N, LANE), 0)
    l = lax.broadcasted_iota(jnp.int32, (N, LANE), 1)
    selm = (l == rev * 4 + (r >> sh)).astype(F32)
    rows4 = lambda a: jnp.concatenate([a] * HEADS, axis=0)
    XG = rows4(G_all) * selm
    G = jnp.sum(XG, axis=-1, keepdims=True)
    bt = jnp.sum(rows4(beta_all) * selm, axis=-1, keepdims=True)
    Gtot = jnp.sum(Gtot_all * selm, axis=-1, keepdims=True)
    decay = jnp.exp(jnp.where(incl, G - _nt_h3(jnp.ones((N, LANE), F32), XG), -1e30))
    qs, ks, vs = (jnp.concatenate(t, axis=0) for t in (q, k, v))
    qn = qs * lax.rsqrt(jnp.sum(qs * qs, axis=-1, keepdims=True) + 1e-6)
    kn = ks * lax.rsqrt(jnp.sum(ks * ks, axis=-1, keepdims=True) + 1e-6)
    qc = qn * (HD ** -0.5)
    kb = kn * bt
    T = _inv_unit(jnp.where(strict, _nt(kb, kn) * decay, 0.0), None if kept is None else kept[0])
    eG = jnp.exp(G)
    u = _nn(T, vs * bt)
    w = _nn(T, kb * eG)
    qk = _nt(qc, kn) * decay
    spread = (lax.broadcasted_iota(jnp.int32, (HD, N), 0)
              == (lax.broadcasted_iota(jnp.int32, (HD, N), 1) & (HD - 1))).astype(F32)
    wide = lambda a: jnp.where(same, _nn(a, spread), 0.0)
    v_new = u - _nn(wide(w), S)
    o = _nn(wide(qc * eG), S) + _nn(qk, v_new)
    S_new = S * jnp.exp(Gtot) + _tn(wide(kn * jnp.exp(Gtot - G)), v_new)
    return [o[Q * h:Q * (h + 1), :] for h in range(HEADS)], S_new, [T]


def _seq_pieces(ref, r0, Q, splits):
    if splits is None:
        return ref[r0:r0 + Q, :]
    return [[ref[r0:r0 + Q, o + w * t:o + w * (t + 1)] for t in range(n)] for o, w, n in splits]


def _store_pieces(ref, r0, Q, splits, vals, extra=None):
    if splits is None:
        ref[r0:r0 + Q, :] = vals
        return
    for g, (o, w, n) in enumerate(splits):
        for t in range(n):
            v = vals[g][t]
            if extra is not None and g == 0:
                v = v + extra[r0:r0 + Q, o + w * t:o + w * (t + 1)]
            ref[r0:r0 + Q, o + w * t:o + w * (t + 1)] = v


def _flat(ins):
    out = []
    for v in ins:
        if isinstance(v, list):
            out.extend(v)
        else:
            out.append(v)
    return out


def _scan_fwd(name, chunk_fn, seqs, rows, Q, L, CH, kept_shapes=()):
    nc = L // Q
    nb = nc // CH
    ns, nr = len(seqs), len(rows)
    nk = 1 + len(kept_shapes)
    BQ = Q * CH

    def kern(*refs):
        s_refs = (refs[:ns], refs[ns:2 * ns])
        r_refs = refs[2 * ns:2 * ns + nr]
        pos = 2 * ns + nr
        y_refs = refs[pos:pos + 2]
        k_refs = (refs[pos + 2:pos + 2 + nk], refs[pos + 2 + nk:pos + 2 + 2 * nk])
        S_scr = refs[pos + 2 + 2 * nk]
        i = pl.program_id(0)

        @pl.when(i == 0)
        def _():
            S_scr[...] = jnp.zeros(S_scr.shape, F32)

        rws = [r[...] for r in r_refs]
        for d in (0, 1):
            S = S_scr[d]
            for cc in range(CH):
                c = cc if d == 0 else CH - 1 - cc
                k_refs[d][0][c] = S
                ins = [_seq_pieces(r, c * Q, Q, sp) for r, (_, _, _, sp) in zip(s_refs[d], seqs)]
                ys, S, kept = chunk_fn(S, *_flat(ins), *rws, d)
                for r, v in zip(k_refs[d][1:], kept):
                    r[c] = v
                for h in range(HEADS):
                    y_refs[d][c * Q:(c + 1) * Q, HD * h:HD * (h + 1)] = ys[h]
            S_scr[d] = S

    fwd_specs = [pl.BlockSpec((BQ, w), functools.partial(lambda i, cb: (i, cb), cb=cb)) for _, w, cb, _ in seqs]
    rev_specs = [pl.BlockSpec((BQ, w), functools.partial(lambda i, cb: (nb - 1 - i, cb), cb=cb))
                 for _, w, cb, _ in seqs]
    arrs = [a for a, _, _, _ in seqs]
    k_shapes = [(GROUP_W, HD)] + list(kept_shapes)
    res = pl.pallas_call(
        kern, name=name, grid=(nb,),
        in_specs=fwd_specs + rev_specs + [pl.BlockSpec((1, LANE), lambda i: (0, 0)) for _ in rows],
        out_specs=[pl.BlockSpec((BQ, GROUP_W), lambda i: (i, 0)),
                   pl.BlockSpec((BQ, GROUP_W), lambda i: (nb - 1 - i, 0))]
        + [pl.BlockSpec((CH,) + s, lambda i: (i, 0, 0)) for s in k_shapes]
        + [pl.BlockSpec((CH,) + s, lambda i: (nb - 1 - i, 0, 0)) for s in k_shapes],
        out_shape=[jax.ShapeDtypeStruct((L, GROUP_W), F32)] * 2
        + [jax.ShapeDtypeStruct((nc,) + s, F32) for s in k_shapes] * 2,
        scratch_shapes=[pltpu.VMEM((2, GROUP_W, HD), F32)],
        compiler_params=_cparams(("arbitrary",)),
    )(*arrs, *arrs, *rows)
    return res[0], res[1], list(res[2:2 + nk]), list(res[2 + nk:])


def _scan_bwd(name, chunk_fn, seqs, rows, ssaves, dy, extra, Q, L, CH, side=None):
    nc = L // Q
    nb = nc // CH
    BQ = Q * CH
    ns, nr = len(seqs), len(rows)
    nk = len(ssaves[0])
    has_extra = extra is not None

    def kern(*refs):
        s_refs = (refs[:ns], refs[ns:2 * ns])
        pos = 2 * ns
        r_refs = refs[pos:pos + nr]
        pos += nr
        k_refs = (refs[pos:pos + nk], refs[pos + nk:pos + 2 * nk])
        pos += 2 * nk
        dy_refs = refs[pos:pos + 2]
        pos += 2
        ex_ref = refs[pos] if has_extra else None
        pos += 1 if has_extra else 0
        ds_refs = (refs[pos:pos + ns], refs[pos + ns:pos + 2 * ns])
        pos += 2 * ns
        dr_refs = refs[pos:pos + nr]
        dS_scr = refs[pos + nr]
        i = pl.program_id(0)

        @pl.when(i == 0)
        def _():
            dS_scr[...] = jnp.zeros(dS_scr.shape, F32)
            for r in dr_refs:
                r[...] = jnp.zeros(r.shape, F32)

        rws = [r[...] for r in r_refs]
        dr_acc = [jnp.zeros((1, LANE), F32) for _ in rows]
        for d in (0, 1):
            dS = dS_scr[d]
            for cc in range(CH):
                c = CH - 1 - cc if d == 0 else cc
                S = k_refs[d][0][c]
                kept = [r[c] for r in k_refs[d][1:]]
                dys = [dy_refs[d][c * Q:(c + 1) * Q, HD * h:HD * (h + 1)] for h in range(HEADS)]
                ins = [_seq_pieces(r, c * Q, Q, sp) for r, (_, _, _, sp) in zip(s_refs[d], seqs)]
                _, vjp = jax.vjp(
                    functools.partial(
                        lambda S_, ins_, rws_, d_, kept_: chunk_fn(S_, *_flat(ins_), *rws_, d_, kept_)[:2],
                        d_=d, kept_=kept),
                    S, ins, rws)
                dS, dins, drws = vjp((dys, dS))
                for n_, (r, (_, _, _, sp)) in enumerate(zip(ds_refs[d], seqs)):
                    _store_pieces(r, c * Q, Q, sp, dins[n_],
                                  extra=ex_ref if (has_extra and d == 0 and n_ == 0) else None)
                dr_acc = [a + g for a, g in zip(dr_acc, drws)]
            dS_scr[d] = dS
        for r, g in zip(dr_refs, dr_acc):
            r[...] += g

    def blk(shape, rev, cb=0):
        nd = len(shape)
        if rev:
            return pl.BlockSpec(shape, lambda i: (i, cb) + (0,) * (nd - 2))
        return pl.BlockSpec(shape, lambda i: (nb - 1 - i, cb) + (0,) * (nd - 2))

    arrs = [a for a, _, _, _ in seqs]
    in_specs = [blk((BQ, w), False, cb) for _, w, cb, _ in seqs] + [blk((BQ, w), True, cb) for _, w, cb, _ in seqs]
    in_specs += [pl.BlockSpec((1, LANE), lambda i: (0, 0)) for _ in rows]
    in_specs += [blk((CH,) + a.shape[1:], False) for a in ssaves[0]]
    in_specs += [blk((CH,) + a.shape[1:], True) for a in ssaves[1]]
    in_specs += [blk((BQ, GROUP_W), False), blk((BQ, GROUP_W), True)]
    args = arrs + arrs + list(rows) + list(ssaves[0]) + list(ssaves[1]) + [dy, dy]
    if has_extra:
        in_specs.append(blk((BQ, GROUP_W), False))
        args.append(extra)
    kern, s_in, s_out, s_shapes, s_scratch, s_args = _host(kern, len(args), 2 * ns + nr, side, (nb,), 1)
    res = pl.pallas_call(
        kern, name=name, grid=(nb,),
        in_specs=in_specs + s_in,
        out_specs=[blk((BQ, w), False) for _, w, _, _ in seqs] + [blk((BQ, w), True) for _, w, _, _ in seqs]
        + [pl.BlockSpec((1, LANE), lambda i: (0, 0)) for _ in rows] + s_out,
        out_shape=[jax.ShapeDtypeStruct((L, w), F32) for _, w, _, _ in seqs] * 2
        + [jax.ShapeDtypeStruct((1, LANE), F32) for _ in rows] + s_shapes,
        scratch_shapes=[pltpu.VMEM((2, GROUP_W, HD), F32)] + s_scratch,
        compiler_params=_cparams(("arbitrary",)),
    )(*args, *s_args)
    return list(res[:ns]), list(res[ns:2 * ns]), list(res[2 * ns:2 * ns + nr]), list(res[2 * ns + nr:])


def _loss_call(y, tgt, L):
    T = min(256, L)

    def kern(y_ref, t_ref, dy_ref, l_ref):
        i = pl.program_id(0)
        e = y_ref[...] - t_ref[...]
        dy_ref[...] = e * (1.0 / D_MODEL)

        @pl.when(i == 0)
        def _():
            l_ref[...] = jnp.zeros(l_ref.shape, F32)

        part = 0.5 * jnp.sum(jnp.sum(e * e, axis=-1, keepdims=True) * (1.0 / D_MODEL), axis=0, keepdims=True)
        l_ref[...] += jnp.broadcast_to(part, l_ref.shape)

    return pl.pallas_call(
        kern, name="loss_head", grid=(L // T,),
        in_specs=[_spec2(T, D_MODEL), _spec2(T, D_MODEL)],
        out_specs=[_spec2(T, D_MODEL), pl.BlockSpec((8, LANE), lambda i: (0, 0))],
        out_shape=[jax.ShapeDtypeStruct((L, D_MODEL), F32), jax.ShapeDtypeStruct((8, LANE), F32)],
        compiler_params=_cparams(("arbitrary",)),
    )(y, tgt)


_ANY = pl.BlockSpec(memory_space=pl.ANY)


def _coords():
    return lax.axis_index("x"), lax.axis_index("y"), lax.axis_index("c")


class _Copies:
    def __init__(self, ins, out_shapes, copies_fn, n_remote, n_local):
        self.ins, self.out_shapes, self.copies_fn = list(ins), list(out_shapes), copies_fn
        self.n_remote, self.n_local = n_remote, n_local

    def scratch(self):
        return [pltpu.SemaphoreType.DMA((self.n_remote,)), pltpu.SemaphoreType.DMA((self.n_remote,)),
                pltpu.SemaphoreType.DMA((self.n_local,))]

    def _descr(self, in_refs, out_refs, sems):
        send_sems, recv_sems, lsems = sems
        remote, local = self.copies_fn(list(in_refs), list(out_refs))
        assert len(remote) == self.n_remote and len(local) == self.n_local
        mk = lambda k, src, dst, peer: pltpu.make_async_remote_copy(
            src_ref=src, dst_ref=dst, send_sem=send_sems.at[k], recv_sem=recv_sems.at[k], device_id=peer,
            device_id_type=MESH)
        sends = [mk(k, src, dst, peer) for k, (src, dst, _, peer) in enumerate(remote)]
        recvs = [mk(k, src, land, peer) for k, (src, _, land, peer) in enumerate(remote)]
        locs = [pltpu.make_async_copy(src, dst, lsems.at[k]) for k, (src, dst) in enumerate(local)]
        return sends, recvs, locs

    def start(self, in_refs, out_refs, sems):
        sends, _, locs = self._descr(in_refs, out_refs, sems)
        for c in locs + sends:
            c.start()

    def finish(self, in_refs, out_refs, sems):
        sends, recvs, locs = self._descr(in_refs, out_refs, sems)
        for c in recvs:
            c.wait_recv()
        for c in sends:
            c.wait_send()
        for c in locs:
            c.wait()

    def call(self, name):
        ni, no = len(self.ins), len(self.out_shapes)

        def body(*refs):
            self.start(refs[:ni], refs[ni:ni + no], refs[ni + no:])
            self.finish(refs[:ni], refs[ni:ni + no], refs[ni + no:])

        return pl.pallas_call(body, name=name, in_specs=[_ANY] * ni, out_specs=[_ANY] * no,
                              out_shape=self.out_shapes, scratch_shapes=self.scratch())(*self.ins)


def _chip_peers(x, y):
    return [(1 - x, y), (x, 1 - y), (1 - x, 1 - y)]


def _gather_copies(arrs):
    def copies_fn(ins, outs):
        x, y, c = _coords()
        me = 2 * x + y
        remote, local = [], []
        for src, out in zip(ins, outs):
            local.append((src, out.at[me]))
            for px, py in _chip_peers(x, y):
                remote.append((src, out.at[me], out.at[2 * px + py], (px, py, c)))
        return remote, local

    shapes = [jax.ShapeDtypeStruct((4,) + a.shape, a.dtype) for a in arrs]
    return _Copies(arrs, shapes, copies_fn, 3 * len(arrs), len(arrs))


def _scatter_copies(Gs, small):
    nb = len(Gs)

    def copies_fn(ins, outs):
        x, y, c = _coords()
        me = 2 * x + y
        remote, local = [], []
        for g, out in zip(ins[:nb], outs[:nb]):
            local.append((g.at[me], out.at[me]))
            for px, py in _chip_peers(x, y):
                remote.append((g.at[2 * px + py], out.at[me], out.at[2 * px + py], (px, py, c)))
        if small is not None:
            dev = 4 * x + 2 * y + c
            gs, outs_ = ins[nb], outs[nb]
            local.append((gs, outs_.at[dev]))
            for mask in range(1, 8):
                px, py, pc = x ^ (mask >> 2), y ^ ((mask >> 1) & 1), c ^ (mask & 1)
                remote.append((gs, outs_.at[dev], outs_.at[4 * px + 2 * py + pc], (px, py, pc)))
        return remote, local

    ins = list(Gs) + ([small] if small is not None else [])
    shapes = [jax.ShapeDtypeStruct(g.shape, g.dtype) for g in Gs]
    if small is not None:
        shapes.append(jax.ShapeDtypeStruct((8,) + small.shape, small.dtype))
    extra = 1 if small is not None else 0
    return _Copies(ins, shapes, copies_fn, 3 * nb + 7 * extra, nb + extra)


SWAP_STREAMS = 8


def _row_chunks(rows):
    k = SWAP_STREAMS
    if rows % (8 * k) == 0 and rows >= 64 * k:
        return [(q * (rows // k), rows // k) for q in range(k)]
    return [(0, rows)]


def _swap_copies(parts):
    chunks = [_row_chunks(p.shape[0]) for p in parts]
    n = sum(len(ch) for ch in chunks)

    def copies_fn(ins, outs):
        x, y, c = _coords()
        remote, local = [], []
        for src, out, ch in zip(ins, outs, chunks):
            for r0, nr in ch:
                rows = pl.ds(r0, nr)
                local.append((src.at[rows], out.at[c, rows]))
                remote.append((src.at[rows], out.at[c, rows], out.at[1 - c, rows], (x, y, 1 - c)))
        return remote, local

    shapes = [jax.ShapeDtypeStruct((2,) + p.shape, p.dtype) for p in parts]
    return _Copies(parts, shapes, copies_fn, n, n)


def _merge_copies(sets):
    ins = [a for s in sets for a in s.ins]
    shapes = [o for s in sets for o in s.out_shapes]

    def copies_fn(in_refs, out_refs):
        remote, local, pi, po = [], [], 0, 0
        for s in sets:
            r, l = s.copies_fn(in_refs[pi:pi + len(s.ins)], out_refs[po:po + len(s.out_shapes)])
            remote += r
            local += l
            pi += len(s.ins)
            po += len(s.out_shapes)
        return remote, local

    return _Copies(ins, shapes, copies_fn, sum(s.n_remote for s in sets), sum(s.n_local for s in sets))


def _row_tile(rows):
    best = rows
    for d in range(8, min(rows, 256) + 1, 8):
        if rows % d == 0:
            best = d
    return best


def _sum_slots(name, recv):
    n, R, W = recv.shape
    tr = _row_tile(R)

    def kern(r_ref, o_ref):
        acc = r_ref[0].astype(F32)
        for s in range(1, n):
            acc = acc + r_ref[s].astype(F32)
        o_ref[...] = acc

    return pl.pallas_call(
        kern, name=name, grid=(R // tr,),
        in_specs=[pl.BlockSpec((n, tr, W), lambda i: (0, i, 0))],
        out_specs=pl.BlockSpec((tr, W), lambda i: (i, 0)),
        out_shape=jax.ShapeDtypeStruct((R, W), F32),
        compiler_params=_cparams(("arbitrary",)),
    )(recv)


def _adamw_call(name, slots, w, m, v):
    nl = len(slots)
    n, R, W = slots[0].shape
    tr = _row_tile(R)
    nr = R // tr

    def kern(*refs):
        s_refs = refs[:nl]
        w_ref, m_ref, v_ref, g_ref, d_ref, nm_ref, nv_ref = refs[nl:]
        layer = pl.program_id(0)
        g = s_refs[0][0]
        for s in range(1, n):
            g = g + s_refs[0][s]
        for l in range(1, nl):
            gl = s_refs[l][0]
            for s in range(1, n):
                gl = gl + s_refs[l][s]
            g = jnp.where(layer == l, gl, g)
        m_ = ADAM_B1 * m_ref[...] + (1.0 - ADAM_B1) * g
        v_ = ADAM_B2 * v_ref[...] + (1.0 - ADAM_B2) * (g * g)
        m_hat = m_ / (1.0 - ADAM_B1 ** ADAM_STEP)
        v_hat = v_ / (1.0 - ADAM_B2 ** ADAM_STEP)
        g_ref[...] = g
        d_ref[...] = -ADAM_LR * (m_hat / (jnp.sqrt(v_hat) + ADAM_EPS) + ADAM_WD * w_ref[...])
        nm_ref[...] = m_
        nv_ref[...] = v_

    blk = pl.BlockSpec((None, tr, W), lambda l, i: (l, i, 0))
    return pl.pallas_call(
        kern, name=name, grid=(nl, nr),
        in_specs=[pl.BlockSpec((n, tr, W), lambda l, i: (0, i, 0)) for _ in slots] + [blk, blk, blk],
        out_specs=[blk, blk, blk, blk],
        out_shape=[jax.ShapeDtypeStruct((nl, R, W), F32)] * 4,
        compiler_params=_cparams(("arbitrary", "arbitrary")),
    )(*slots, w, m, v)


def _pack(arrs, width, row_mult):
    flat = jnp.concatenate([a.reshape(-1) for a in arrs])
    n = flat.shape[0]
    rows = -(-n // width)
    rows = -(-rows // row_mult) * row_mult
    return jnp.pad(flat, (0, rows * width - n)).reshape(rows, width)


def _unpack(buf, shapes):
    flat = buf.reshape(-1)
    out, pos = [], 0
    for s in shapes:
        n = int(np.prod(s))
        out.append(flat[pos:pos + n].reshape(s))
        pos += n
    return out


def _rope_angles(L, rot_dim):
    rows = L // GRID_W
    row = jnp.repeat(jnp.arange(rows), GRID_W).astype(F32)
    col = jnp.tile(jnp.arange(GRID_W), rows).astype(F32)
    sec = rot_dim // 2
    inv_freq = ROPE_BASE ** (-jnp.arange(0, sec, 2, dtype=F32) / sec)
    ang_r = row[:, None] * inv_freq
    ang_c = col[:, None] * inv_freq
    ang = jnp.concatenate([ang_r, ang_r, ang_c, ang_c], axis=-1)
    return jnp.cos(ang), jnp.sin(ang)


def _rot_matrix(r):
    R = np.zeros((r, r), np.float32)
    q = r // 4
    for s in range(2):
        for t in range(q):
            lo = s * (r // 2) + t
            hi = lo + q
            R[hi, lo] = -1.0
            R[lo, hi] = 1.0
    return R


def _place_tables(L, cos, sin, width, offsets):
    r = cos.shape[1]
    Rm = np.zeros((width, width), np.float32)
    R = _rot_matrix(r)
    cs, ss, pos = [], [], 0
    for o in list(offsets) + [width]:
        if o > pos:
            cs.append(jnp.ones((L, o - pos), F32))
            ss.append(jnp.zeros((L, o - pos), F32))
        if o < width:
            cs.append(cos)
            ss.append(sin)
            Rm[o:o + r, o:o + r] = R
        pos = o + r
    return jnp.concatenate(cs, axis=1), jnp.concatenate(ss, axis=1), jnp.asarray(Rm)


def _head_mean_matrix(width, stride, n):
    M = np.zeros((width, width), np.float32)
    for o in range(0, width, stride):
        M[o:o + n, o:o + n] = 1.0 / n
    return jnp.asarray(M)


def _pad_heads(w, n_heads, real, padded, axis):
    parts = jnp.split(w, n_heads, axis=axis)
    padw = [(0, 0)] * w.ndim
    padw[axis] = (0, padded - real)
    return jnp.concatenate([jnp.pad(p, padw) for p in parts], axis=axis)


def _row128(v):
    v = v.reshape(1, -1)
    return jnp.pad(v, ((0, 0), (0, LANE - v.shape[1])))


def _conv_w8(w, b):
    C = w.shape[1]
    rows = [w, jnp.zeros((1, C), F32) if b is None else b.reshape(1, C), jnp.zeros((4, C), F32)]
    return jnp.concatenate(rows, axis=0)


def _build_layer(W):
    w_in = W['w_in']
    o = 0
    cols = {}
    for name, n in [('a_cq', A_Q_LORA), ('a_ckv', A_KV_LORA), ('a_kr', A_ROPE), ('b_q', 256), ('b_k', 128),
                    ('b_v', 128), ('c_z', 256), ('c_xbc', 512), ('c_dt', 8), ('d_qkv', 768), ('d_z', 256),
                    ('d_b', 8), ('d_a', 8)]:
        cols[name] = w_in[:, o:o + n]
        o += n
    padc = lambda a, lo, width: jnp.pad(a, ((0, 0), (lo, width - lo - a.shape[1])))
    pieces = {
        'b_q': _pad_heads(cols['b_q'], 4, HD, LANE, 1), 'c_xbc': cols['c_xbc'], 'a_cq': padc(cols['a_cq'], 0, 256),
        'b_k': _pad_heads(cols['b_k'], 2, HD, LANE, 1), 'd_qkv': cols['d_qkv'],
        'b_v': _pad_heads(cols['b_v'], 2, HD, LANE, 1), 'c_z': cols['c_z'], 'd_z': cols['d_z'],
        'a_ckv': cols['a_ckv'], 'a_kr': padc(cols['a_kr'], A_NOPE, LANE), 'c_dt': padc(cols['c_dt'], 0, LANE),
        'd_b': padc(cols['d_b'], 0, LANE), 'd_a': padc(cols['d_a'], 0, LANE),
        'pad': jnp.zeros((D_MODEL, LANE), w_in.dtype)}
    out = {'w_in': jnp.concatenate([pieces[n] for n, _, _ in P_LAYOUT], axis=1)}
    out['a_q_norm'] = padc(W['a_q_norm'].reshape(1, -1), 0, 256)
    wuq = jnp.pad(W['a_w_uq'], ((0, 256 - A_Q_LORA), (0, 0)))
    out['a_w_uq'] = _pad_heads(wuq, 4, A_NOPE + A_ROPE, LANE, 1)
    out['a_kv_norm'] = W['a_kv_norm'].reshape(1, -1)
    ukv = W['a_w_ukv'].reshape(A_KV_LORA, HEADS, 2, HD)
    out['a_w_uk'] = _pad_heads(ukv[:, :, 0, :].reshape(A_KV_LORA, 256), 4, HD, LANE, 1)
    out['a_w_uv'] = _pad_heads(ukv[:, :, 1, :].reshape(A_KV_LORA, 256), 4, HD, LANE, 1)
    out['a_out_norm'] = _pad_heads(W['a_out_norm'].reshape(1, -1), 4, HD, LANE, 1)
    out['b_q_norm'] = _pad_heads(jnp.tile(W['b_q_norm'].reshape(1, -1), (1, 4)), 4, HD, LANE, 1)
    out['b_k_norm'] = _pad_heads(jnp.tile(W['b_k_norm'].reshape(1, -1), (1, 2)), 2, HD, LANE, 1)
    out['b_out_norm'] = _pad_heads(W['b_out_norm'].reshape(1, -1), 4, HD, LANE, 1)
    out['c_conv'] = _conv_w8(W['c_conv_w'], W['c_conv_b'])
    out['c_a_log'] = _row128(W['c_a_log'])
    out['c_dt_bias'] = _row128(W['c_dt_bias'])
    out['c_d_skip'] = jnp.repeat(W['c_d_skip'], HD).reshape(1, -1)
    out['c_out_norm'] = W['c_out_norm'].reshape(1, -1)
    out['d_conv'] = _conv_w8(W['d_conv_w'], None)
    out['d_a_log'] = _row128(W['d_a_log'])
    out['d_dt_bias'] = _row128(W['d_dt_bias'])
    out['d_out_norm'] = jnp.tile(W['d_out_norm'].reshape(1, -1), (1, 4))
    wo = W['w_out']
    out['w_out'] = jnp.concatenate([_pad_heads(wo[0:256], 4, HD, LANE, 0), _pad_heads(wo[256:512], 4, HD, LANE, 0),
                                    wo[512:1024]], axis=0)
    for n in ['pre_mix_norm', 'post_mix_norm', 'pre_ffn_norm', 'post_ffn_norm']:
        out[n] = W[n].reshape(1, -1)
    out['f_w_in'] = W['f_w_in']
    out['f_conv'] = _conv_w8(W['f_conv_w'], W['f_conv_b'])
    out['f_w_out'] = W['f_w_out']
    return out


def _fn_norm_in(a, p):
    return [_rms(a[0], p[0])]


def _fn_resid_norm2(a, p):
    x1 = a[0] + _rms(a[1], p[0])
    return [x1, _rms(x1, p[1])]


def _fn_resid_norm(a, p):
    return [a[0] + _rms(a[1], p[0])]


def _fn_a_prep(a, p):
    cq, ckv, kr, cosq, sinq, cosk, sink = a
    q_norm, w_uq, kv_norm, w_uk, w_uv, rq, rk = p
    q = _nn(_rms(cq, q_norm, A_Q_LORA), w_uq)
    q = q * cosq + _nn_h3(q, rq) * sinq
    kvn = _rms(ckv, kv_norm)
    kr_r = kr * cosk + _nn_h3(kr, rk) * sink
    kk = _nn(kvn, w_uk) + jnp.concatenate([kr_r] * HEADS, axis=1)
    return [q, kk, _nn(kvn, w_uv)]


def _fn_b_prep(a, p):
    q, k, v, cosq, sinq, cosk, sink = a
    q_norm, k_norm, mq, mk, rq, rk = p
    qn = q * lax.rsqrt(_nn_h3(q * q, mq) + EPS) * q_norm
    kn = k * lax.rsqrt(_nn_h3(k * k, mk) + EPS) * k_norm
    return [qn * cosq + _nn_h3(qn, rq) * sinq, kn * cosk + _nn_h3(kn, rk) * sink, v]


def _fn_mixer_post(a, p):
    oa, ob, yc0, yc1, xs, zc, od0, od1, zd = a
    a_norm, b_norm, dskip, c_norm, d_norm, m64 = p
    oc = _rms((yc0 + yc1 + xs * dskip) * _silu(zc), c_norm)
    od = od0 + od1
    odn = od * lax.rsqrt(_nn_h3(od * od, m64) + EPS) * d_norm * _silu(zd)
    return [jnp.concatenate([_rms(oa, a_norm, GROUP_W), _rms(ob, b_norm, GROUP_W), oc, odn], axis=1)]


def _fn_assemble(a, p):
    (dbq, dxbc, dcq, dbk, dqkv, dbv, dzc, dzd, dckv, dkr, ddt0, ddt1, db0, db1, da0, da1) = a
    return [jnp.concatenate([dbq, dxbc, dcq, dbk, dqkv, dbv, dzc, dzd, dckv, dkr, ddt0 + ddt1, db0 + db1,
                             da0 + da1, jnp.zeros_like(dckv)], axis=1)]


def _pspec(T, name):
    off, w = P_OFF[name]
    return _spec2(T, w, off // w)


def _layer_fwd(l, x, h, K, tabs, L, T, next_norm, side_a=None, late=None, side_b=None):
    n = f"l{l}_"
    sv = {'x': x, 'h': h}
    p = _mm(n + "in_proj", h, K['w_in'].astype(BF16), 'nn', F32, 1024, 1280, 1024)
    sv['p'] = p
    a_acts = [(p, _pspec(T, 'a_cq')), (p, _pspec(T, 'a_ckv')), (p, _pspec(T, 'a_kr')),
              (tabs['a_cq'], _spec2(T, 512)), (tabs['a_sq'], _spec2(T, 512)),
              (tabs['a_ck'], _spec2(T, LANE)), (tabs['a_sk'], _spec2(T, LANE))]
    a_pars = [K['a_q_norm'], K['a_w_uq'], K['a_kv_norm'], K['a_w_uk'], K['a_w_uv'], tabs['a_rq'], tabs['a_rk']]
    qa, ka, va = _tw_fwd(n + "a_prep", _fn_a_prep, a_acts, a_pars, [(512, BF16)] * 3, L, T)
    oa, lse_a, got_a = _flash_fwd(n + "a_attn", qa, ka, va, HEADS, 1, (A_NOPE + A_ROPE) ** -0.5, L, side_a)
    if late is not None:
        K = {**K, **late(got_a)}
    sv.update(a_acts=a_acts, a_pars=a_pars, qa=qa, ka=ka, va=va, oa=oa, lse_a=lse_a, K=K)
    b_acts = [(p, _pspec(T, 'b_q')), (p, _pspec(T, 'b_k')), (p, _pspec(T, 'b_v')),
              (tabs['b_cq'], _spec2(T, 512)), (tabs['b_sq'], _spec2(T, 512)),
              (tabs['b_ck'], _spec2(T, 256)), (tabs['b_sk'], _spec2(T, 256))]
    b_pars = [K['b_q_norm'], K['b_k_norm'], tabs['b_mq'], tabs['b_mk'], tabs['b_rq'], tabs['b_rk']]
    qb, kb, vb = _tw_fwd(n + "b_prep", _fn_b_prep, b_acts, b_pars, [(512, BF16), (256, BF16), (256, BF16)], L, T)
    ob, lse_b, sv['side'] = _flash_fwd(n + "b_attn", qb, kb, vb, HEADS, 2, HD ** -0.5, L, side_b)
    sv.update(b_acts=b_acts, b_pars=b_pars, qb=qb, kb=kb, vb=vb, ob=ob, lse_b=lse_b)
    xbc = _conv_fwd(n + "c_conv", p, P_OFF['c_xbc'][0], C_XBC, K['c_conv'], True, L, 512)
    c_seqs = [(xbc, C_XBC, 0, [(0, HD, 4), (256, HD, 2), (384, HD, 2)]),
              (p, LANE, P_OFF['c_dt'][0] // LANE, None)]
    c_rows = [K['c_a_log'], K['c_dt_bias']]
    yc0, yc1, sc0, sc1 = _scan_fwd(n + "c_ssd", _ssd_chunk, c_seqs, c_rows, C_CHUNK, L, C_PER_STEP)
    sv.update(xbc=xbc, c_seqs=c_seqs, c_rows=c_rows, sc=(sc0, sc1))
    qkv = _conv_fwd(n + "d_conv", p, P_OFF['d_qkv'][0], D_QKV, K['d_conv'], True, L, 768)
    d_seqs = [(qkv, D_QKV, 0, [(0, HD, 4), (256, HD, 4), (512, HD, 4)]),
              (p, LANE, P_OFF['d_b'][0] // LANE, None), (p, LANE, P_OFF['d_a'][0] // LANE, None)]
    d_rows = [K['d_a_log'], K['d_dt_bias']]
    od0, od1, sd0, sd1 = _scan_fwd(n + "d_delta", _delta_chunk, d_seqs, d_rows, D_CHUNK, L, D_PER_STEP,
                                   [(HEADS * D_CHUNK, HEADS * D_CHUNK)])
    sv.update(qkv=qkv, d_seqs=d_seqs, d_rows=d_rows, sd=(sd0, sd1))
    m_acts = [(oa, _spec2(T, 512)), (ob, _spec2(T, 512)), (yc0, _spec2(T, 256)), (yc1, _spec2(T, 256)),
              (xbc, _spec2(T, 256, 0)), (p, _pspec(T, 'c_z')), (od0, _spec2(T, 256)), (od1, _spec2(T, 256)),
              (p, _pspec(T, 'd_z'))]
    m_pars = [K['a_out_norm'], K['b_out_norm'], K['c_d_skip'], K['c_out_norm'], K['d_out_norm'], tabs['m64']]
    (o,) = _tw_fwd(n + "mixer_post", _fn_mixer_post, m_acts, m_pars, [(O_COLS, BF16)], L, T)
    f1 = _mm(n + "out_proj", o, K['w_out'].astype(BF16), 'nn', F32, 1024, 1024, 1536)
    r1_pars = [K['post_mix_norm'], K['pre_ffn_norm']]
    x1, h2 = _tw_fwd(n + "resid_mix", _fn_resid_norm2, [(x, _spec2(T, D_MODEL)), (f1, _spec2(T, D_MODEL))], r1_pars,
                     [(D_MODEL, F32), (D_MODEL, BF16)], L, T)
    sv.update(m_acts=m_acts, m_pars=m_pars, o=o, f1=f1, r1_pars=r1_pars, x1=x1, h2=h2)
    u = _mm(n + "ffn_in", h2, K['f_w_in'].astype(BF16), 'nn', F32, 1024, 1408, 1024)
    act = _ffn_gate_fwd(n + "ffn_gate", u, K['f_conv'], L)
    f2 = _mm(n + "ffn_out", act, K['f_w_out'].astype(BF16), 'nn', F32, 1024, 1024, 1408)
    sv.update(u=u, act=act, f2=f2)
    xf = [(x1, _spec2(T, D_MODEL)), (f2, _spec2(T, D_MODEL))]
    if next_norm is None:
        (x2,) = _tw_fwd(n + "resid_ffn", _fn_resid_norm, xf, [K['post_ffn_norm']], [(D_MODEL, F32)], L, T)
        hn = None
    else:
        x2, hn = _tw_fwd(n + "resid_ffn", _fn_resid_norm2, xf, [K['post_ffn_norm'], next_norm],
                         [(D_MODEL, F32), (D_MODEL, BF16)], L, T)
    return x2, hn, sv


def _layer_bwd(l, dx2, dhn, K, sv, tabs, L, T, next_norm, hosts=None):
    n = f"l{l}b_"
    dK = {}
    hosts = hosts or {}
    got = {}
    side = lambda name: hosts[name](dK, got) if name in hosts else None
    s2 = lambda w, cb=0: _spec2(T, w, cb)
    xf = [(sv['x1'], s2(D_MODEL)), (sv['f2'], s2(D_MODEL))]
    if next_norm is None:
        (dx1a, df2), (dK['post_ffn_norm'],) = _tw_bwd(n + "resid_ffn", _fn_resid_norm, xf, [K['post_ffn_norm']],
                                                      [(dx2, s2(D_MODEL))], L, T, [True, True], [True])
        dnext = None
    else:
        (dx1a, df2), (dK['post_ffn_norm'], dnext) = _tw_bwd(
            n + "resid_ffn", _fn_resid_norm2, xf, [K['post_ffn_norm'], next_norm],
            [(dx2, s2(D_MODEL)), (dhn, s2(D_MODEL))], L, T, [True, True], [True, True])
    dact = _mm(n + "ffn_out_dx", df2, K['f_w_out'].astype(BF16), 'nt', F32, 1024, 1408, 1024)
    dK['f_w_out'] = _mm(n + "ffn_out_dw", sv['act'], df2, 'tn', F32, 1408, 1024, 1024)
    du, dK['f_conv'] = _ffn_gate_bwd(n + "ffn_gate", sv['u'], K['f_conv'], dact, L)
    dh2 = _mm(n + "ffn_in_dx", du, K['f_w_in'].astype(BF16), 'nt', F32, 1024, 1024, 1408)
    dK['f_w_in'] = _mm(n + "ffn_in_dw", sv['h2'], du, 'tn', F32, 1024, 1408, 1024)
    (dxa, df1), (dK['post_mix_norm'], dK['pre_ffn_norm']) = _tw_bwd(
        n + "resid_mix", _fn_resid_norm2, [(sv['x'], s2(D_MODEL)), (sv['f1'], s2(D_MODEL))], sv['r1_pars'],
        [(dx1a, s2(D_MODEL)), (dh2, s2(D_MODEL))], L, T, [True, True], [True, True])
    do = _mm(n + "out_proj_dx", df1, K['w_out'].astype(BF16), 'nt', F32, 1024, 1536, 1024)
    dK['w_out'] = _mm(n + "out_proj_dw", sv['o'], df1, 'tn', F32, 1536, 1024, 1024)
    (doa, dob, dyc0, _, dxs_skip, dzc, dod0, _, dzd), mp = _tw_bwd(
        n + "mixer_post", _fn_mixer_post, sv['m_acts'], sv['m_pars'], [(do, s2(O_COLS))], L, T,
        [True] * 9, [True] * 5 + [False])
    dK['a_out_norm'], dK['b_out_norm'], dK['c_d_skip'], dK['c_out_norm'], dK['d_out_norm'] = mp
    (dqkv0, db0, da0), (dqkv1, db1, da1), (dK['d_a_log'], dK['d_dt_bias']), got['d_delta'] = _scan_bwd(
        n + "d_delta", _delta_chunk, sv['d_seqs'], sv['d_rows'], sv['sd'], dod0, None, D_CHUNK, L, D_PER_STEP,
        side('d_delta'))
    dqkv, dK['d_conv'] = _conv_bwd(n + "d_conv", sv['p'], P_OFF['d_qkv'][0], D_QKV, K['d_conv'], True,
                                   [(dqkv0, None), (dqkv1, None)], L, 768)
    (dxbc0, ddt0), (dxbc1, ddt1), (dK['c_a_log'], dK['c_dt_bias']), _ = _scan_bwd(
        n + "c_ssd", _ssd_chunk, sv['c_seqs'], sv['c_rows'], sv['sc'], dyc0, dxs_skip, C_CHUNK, L, C_PER_STEP)
    dxbc, dK['c_conv'] = _conv_bwd(n + "c_conv", sv['p'], P_OFF['c_xbc'][0], C_XBC, K['c_conv'], True,
                                   [(dxbc0, None), (dxbc1, None)], L, 512)
    dqb, dkb, dvb, got['b_attn'] = _flash_bwd(n + "b_attn", sv['qb'], sv['kb'], sv['vb'], sv['ob'], sv['lse_b'],
                                              dob, HEADS, 2, HD ** -0.5, L, side('b_attn'))
    (dbq, dbk, dbv), (dK['b_q_norm'], dK['b_k_norm']) = _tw_bwd(
        n + "b_prep", _fn_b_prep, sv['b_acts'], sv['b_pars'], [(dqb, s2(512)), (dkb, s2(256)), (dvb, s2(256))],
        L, T, [True] * 3 + [False] * 4, [True, True] + [False] * 4)
    dqa, dka, dva, got['a_attn'] = _flash_bwd(n + "a_attn", sv['qa'], sv['ka'], sv['va'], sv['oa'], sv['lse_a'],
                                              doa, HEADS, 1, (A_NOPE + A_ROPE) ** -0.5, L, side('a_attn'))
    (dcq, dckv, dkr), ap = _tw_bwd(
        n + "a_prep", _fn_a_prep, sv['a_acts'], sv['a_pars'], [(dqa, s2(512)), (dka, s2(512)), (dva, s2(512))],
        L, T, [True] * 3 + [False] * 4, [True] * 5 + [False] * 2)
    dK['a_q_norm'], dK['a_w_uq'], dK['a_kv_norm'], dK['a_w_uk'], dK['a_w_uv'] = ap
    pieces = [(dbq, s2(512)), (dxbc, s2(512)), (dcq, s2(256)), (dbk, s2(256)), (dqkv, s2(768)), (dbv, s2(256)),
              (dzc, s2(256)), (dzd, s2(256)), (dckv, s2(LANE)), (dkr, s2(LANE)),
              (ddt0, s2(LANE)), (ddt1, s2(LANE)), (db0, s2(LANE)), (db1, s2(LANE)), (da0, s2(LANE)),
              (da1, s2(LANE))]
    (dp,) = _tw_fwd(n + "assemble_dp", _fn_assemble, pieces, [], [(P_COLS, BF16)], L, T)
    dh = _mm(n + "in_proj_dx", dp, K['w_in'].astype(BF16), 'nt', F32, 1024, 1024, 1280)
    dK['w_in'] = _mm(n + "in_proj_dw", sv['h'], dp, 'tn', F32, 1024, 1280, 1024)
    return dxa, dh, dK, dnext, got


def _tables(L):
    ca, sa = _rope_angles(L, A_ROPE)
    cb, sb = _rope_angles(L, HD)
    t = {}
    t['a_cq'], t['a_sq'], t['a_rq'] = _place_tables(L, ca, sa, 512, [LANE * h + A_NOPE for h in range(4)])
    t['a_ck'], t['a_sk'], t['a_rk'] = _place_tables(L, ca, sa, LANE, [A_NOPE])
    t['b_cq'], t['b_sq'], t['b_rq'] = _place_tables(L, cb, sb, 512, [LANE * h for h in range(4)])
    t['b_ck'], t['b_sk'], t['b_rk'] = _place_tables(L, cb, sb, 256, [LANE * h for h in range(2)])
    t['b_mq'] = _head_mean_matrix(512, LANE, HD)
    t['b_mk'] = _head_mean_matrix(256, LANE, HD)
    t['m64'] = _head_mean_matrix(256, HD, HD)
    return t


def kernel(x, pre_mix_norm, w_in, a_q_norm, a_w_uq, a_kv_norm, a_w_ukv, a_out_norm, b_q_norm, b_k_norm, b_out_norm, c_conv_w, c_conv_b, c_a_log, c_dt_bias, c_d_skip, c_out_norm, d_conv_w, d_a_log, d_dt_bias, d_out_norm, w_out, post_mix_norm, pre_ffn_norm, f_w_in, f_conv_w, f_conv_b, f_w_out, post_ffn_norm, loss_target, m_pre_mix_norm, m_w_in, m_a_q_norm, m_a_w_uq, m_a_kv_norm, m_a_w_ukv, m_a_out_norm, m_b_q_norm, m_b_k_norm, m_b_out_norm, m_c_conv_w, m_c_conv_b, m_c_a_log, m_c_dt_bias, m_c_d_skip, m_c_out_norm, m_d_conv_w, m_d_a_log, m_d_dt_bias, m_d_out_norm, m_w_out, m_post_mix_norm, m_pre_ffn_norm, m_f_w_in, m_f_conv_w, m_f_conv_b, m_f_w_out, m_post_ffn_norm, v_pre_mix_norm, v_w_in, v_a_q_norm, v_a_w_uq, v_a_kv_norm, v_a_w_ukv, v_a_out_norm, v_b_q_norm, v_b_k_norm, v_b_out_norm, v_c_conv_w, v_c_conv_b, v_c_a_log, v_c_dt_bias, v_c_d_skip, v_c_out_norm, v_d_conv_w, v_d_a_log, v_d_dt_bias, v_d_out_norm, v_w_out, v_post_mix_norm, v_pre_ffn_norm, v_f_w_in, v_f_conv_w, v_f_conv_b, v_f_w_out, v_post_ffn_norm):
    loc = locals()
    Wl = {n: loc[n] for n in WEIGHTS}
    Ml = {n: loc['m_' + n] for n in WEIGHTS}
    Vl = {n: loc['v_' + n] for n in WEIGHTS}
    L = x.shape[1]
    T = min(512, L)
    x0 = x.reshape(L, D_MODEL)
    tgt = loss_target.reshape(L, D_MODEL)

    first = ['w_in', 'a_w_uq', 'a_w_ukv', 'c_conv_w', 'd_conv_w']
    later = [n for n in SHARDED if n not in first]
    late_keys = ['w_out', 'f_w_in', 'f_conv', 'f_w_out']

    def shards(l, names):
        return [Wl[n][l].astype(BF16) if n in MXU_WEIGHTS else Wl[n][l] for n in names]

    def layer_weights(l, names, gathered):
        W = {n: Wl[n][l] for n in SMALL}
        for n in SHARDED:
            W[n] = jnp.zeros(layer_shape(n), BF16 if n in MXU_WEIGHTS else F32)
        for n, g in zip(names, gathered):
            W[n] = jnp.concatenate([g[j] for j in range(4)], axis=SHARD_AXIS[n] - 1)
        return W

    def chip_blocks(g, n):
        return jnp.stack(jnp.split(g, 4, axis=SHARD_AXIS[n] - 1))

    tabs = _tables(L)
    norm_in = [Wl['pre_mix_norm'][l].reshape(1, -1) for l in range(DEPTH)]
    def layer_shape(n):
        s = list(Wl[n].shape[1:])
        if n in SHARD_AXIS:
            s[SHARD_AXIS[n] - 1] *= 4
        return tuple(s)

    unbuild = jax.vjp(_build_layer, {n: jnp.zeros(layer_shape(n), F32) for n in WEIGHTS})[1]

    (h,) = _tw_fwd("l0_norm_in", _fn_norm_in, [(x0, _spec2(T, D_MODEL))], [norm_in[0]], [(D_MODEL, BF16)], L, T)
    gathered = _gather_copies(shards(0, first)).call("gather_l0")
    xs, saves, Ks = x0, [], []
    for l in range(DEPTH):
        last = l + 1 == DEPTH
        nxt = None if last else _gather_copies(shards(l + 1, SHARDED))
        if l == 0:
            def late(got):
                K_late = _build_layer(layer_weights(0, later, got))
                return {k: K_late[k] for k in late_keys}

            xs, h, sv = _layer_fwd(l, xs, h, _build_layer(layer_weights(0, first, gathered)), tabs, L, T,
                                   None if last else norm_in[l + 1], _gather_copies(shards(0, later)), late, nxt)
        else:
            xs, h, sv = _layer_fwd(l, xs, h, _build_layer(layer_weights(l, SHARDED, gathered)), tabs, L, T,
                                   None if last else norm_in[l + 1], None, None, nxt)
        gathered = sv['side']
        Ks.append(sv['K'])
        saves.append(sv)
    dy, loss_acc = _loss_call(xs, tgt, L)
    loss = lax.psum(loss_acc[0, 0], ("x", "y", "c"))

    ffn = ['f_w_in', 'f_conv_w', 'f_w_out', 'w_out']
    rest = [n for n in SHARDED if n not in ffn]

    def ffn_side(dK):
        only_w_out = {k: (dK[k] if k == 'w_out' else jnp.zeros(v.shape, F32)) for k, v in Ks[0].items()}
        g = {'f_w_in': dK['f_w_in'], 'f_conv_w': dK['f_conv'][0:3], 'f_w_out': dK['f_w_out'],
             'w_out': unbuild(only_w_out)[0]['w_out']}
        return _scatter_copies([chip_blocks(g[n], n) for n in ffn], None)

    def rest_blocks(dK):
        full = dict(dK)
        full.setdefault('pre_mix_norm', jnp.zeros((1, D_MODEL), F32))
        (g,) = unbuild(full)
        return [chip_blocks(g[n], n) for n in rest]

    def chip_sums(l, names, recvs):
        return [_sum_slots(f"sum_{n}_{l}", r.reshape(4, -1, r.shape[-1])) for n, r in zip(names, recvs)]

    grads = [None] * DEPTH
    pairs = {}
    dx, dhn = dy, None
    for l in reversed(range(DEPTH)):
        last = l + 1 == DEPTH

        def host_scatter(dK, got, up=None if last else grads[l + 1]):
            sets = [ffn_side(dK)] + ([] if up is None else [_scatter_copies(rest_blocks(up), None)])
            return _merge_copies(sets)

        def host_swap(dK, got, l=l, last=last):
            r = got['d_delta']
            parts = chip_sums(l, ffn, r[:len(ffn)]) + ([] if last else chip_sums(l + 1, rest, r[len(ffn):]))
            return _swap_copies(parts)

        dxa, dh, dK, dnext, got = _layer_bwd(l, dx, dhn, Ks[l], saves[l], tabs, L, T,
                                             None if last else norm_in[l + 1],
                                             {'d_delta': host_scatter, 'b_attn': host_swap})
        pairs.update({(l, n): p for n, p in zip(ffn, got['b_attn'])})
        if not last:
            pairs.update({(l + 1, n): p for n, p in zip(rest, got['b_attn'][len(ffn):])})
            grads[l + 1]['pre_mix_norm'] = dnext
        grads[l] = dK
        dx, dhn = dxa, dh
    (dx_in,), (grads[0]['pre_mix_norm'],) = _tw_bwd(
        "l0b_norm_in", _fn_norm_in, [(x0, _spec2(T, D_MODEL))], [norm_in[0]], [(dhn, _spec2(T, D_MODEL))], L, T,
        [True], [True], addto={0: (dx, _spec2(T, D_MODEL))})
    small_shapes = [Wl[n].shape for n in SMALL]
    gfull = [unbuild(grads[l])[0] for l in range(DEPTH)]
    gs = _pack([jnp.stack([gfull[l][n] for l in range(DEPTH)]) for n in SMALL], LANE, 8)
    *got0, recv_small = _scatter_copies([b.astype(BF16) for b in rest_blocks(grads[0])], gs).call("scatter_last")
    pairs.update({(0, n): p for n, p in zip(rest, _swap_copies(chip_sums(0, rest, got0)).call("swap_last"))})

    kinds = ['grad', 'delta', 'new_m', 'new_v']
    res = {}
    for n in SHARDED:
        upd = _adamw_call("adamw_" + n, [pairs[l, n] for l in range(DEPTH)], Wl[n], Ml[n], Vl[n])
        for kind, a in zip(kinds, upd):
            res[kind, n] = a
    small = _adamw_call("adamw_small", [recv_small], *[_pack([W_[n] for n in SMALL], LANE, 8)[None]
                                                       for W_ in (Wl, Ml, Vl)])
    for kind, s in zip(kinds, small):
        for n, a in zip(SMALL, _unpack(s, small_shapes)):
            res[kind, n] = a
    outs = [loss, dx_in.reshape(x.shape)]
    for kind in ['grad', 'delta', 'new_m', 'new_v']:
        outs += [res[kind, n] for n in WEIGHTS]
    return tuple(outs)
```

```python
import functools
import math

import numpy as np
import jax
import jax.numpy as jnp
from jax import lax
from jax.experimental import pallas as pl
from jax.experimental.pallas import tpu as pltpu

F32 = jnp.float32
BF16 = jnp.bfloat16
MESH = pl.DeviceIdType.MESH
VMEM_LIMIT = 48 * 1024 * 1024
LANE = 128

D_MODEL = 1024
DEPTH = 2
GRID_W = 64
ROPE_BASE = 10000.0
EPS = 1e-6
GROUP_W = 256
HEADS = 4
HD = 64
A_NOPE, A_ROPE, A_Q_LORA, A_KV_LORA = 64, 32, 192, 128
A_COLS = A_Q_LORA + A_KV_LORA + A_ROPE
B_COLS = 512
C_XBC = 512
C_COLS = GROUP_W + C_XBC + 8
D_QKV = 768
D_COLS = D_QKV + GROUP_W + 16
IN_COLS = A_COLS + B_COLS + C_COLS + D_COLS
C_CHUNK = 128
D_CHUNK = 64
C_PER_STEP = 2
D_PER_STEP = 2
D_FF = 2816
ADAM_LR, ADAM_B1, ADAM_B2, ADAM_EPS, ADAM_WD, ADAM_STEP = 0.001, 0.9, 0.999, 1e-08, 0.01, 10

WEIGHTS = ['pre_mix_norm', 'w_in', 'a_q_norm', 'a_w_uq', 'a_kv_norm', 'a_w_ukv', 'a_out_norm', 'b_q_norm',
           'b_k_norm', 'b_out_norm', 'c_conv_w', 'c_conv_b', 'c_a_log', 'c_dt_bias', 'c_d_skip', 'c_out_norm',
           'd_conv_w', 'd_a_log', 'd_dt_bias', 'd_out_norm', 'w_out', 'post_mix_norm', 'pre_ffn_norm', 'f_w_in',
           'f_conv_w', 'f_conv_b', 'f_w_out', 'post_ffn_norm']
SHARD_AXIS = {'w_in': 2, 'a_w_uq': 2, 'a_w_ukv': 2, 'c_conv_w': 2, 'd_conv_w': 2, 'w_out': 1, 'f_w_in': 2,
              'f_conv_w': 2, 'f_w_out': 1}
SHARDED = [n for n in WEIGHTS if n in SHARD_AXIS]
SMALL = [n for n in WEIGHTS if n not in SHARD_AXIS]
MXU_WEIGHTS = ('w_in', 'a_w_uq', 'a_w_ukv', 'w_out', 'f_w_in', 'f_w_out')

P_LAYOUT = [('b_q', 0, 512), ('c_xbc', 512, 512), ('a_cq', 1024, 256), ('b_k', 1280, 256), ('d_qkv', 1536, 768),
            ('b_v', 2304, 256), ('c_z', 2560, 256), ('d_z', 2816, 256), ('a_ckv', 3072, 128), ('a_kr', 3200, 128),
            ('c_dt', 3328, 128), ('d_b', 3456, 128), ('d_a', 3584, 128), ('pad', 3712, 128)]
P_OFF = {n: (o, w) for n, o, w in P_LAYOUT}
P_COLS = 3840
O_COLS = 1536


def _cparams(sem):
    return pltpu.CompilerParams(dimension_semantics=sem, vmem_limit_bytes=VMEM_LIMIT)


def _tile(n, target):
    best = None
    for d in range(LANE, min(n, target) + 1, LANE):
        if n % d == 0:
            best = d
    return best if best is not None else n


_NN = ((1,), (0,))
_NT = ((1,), (1,))
_TN = ((0,), (0,))


def _raw_dot(a, b, dims, hi):
    if hi:
        prec = lax.Precision.HIGH if hi == 'high' else lax.Precision.HIGHEST
        return lax.dot_general(a, b, (dims, ((), ())), precision=prec, preferred_element_type=F32)
    return lax.dot_general(a.astype(BF16), b.astype(BF16), (dims, ((), ())), preferred_element_type=F32)


def _make_dots(hi):
    @jax.custom_vjp
    def nn(a, b):
        return _raw_dot(a, b, _NN, hi)

    @jax.custom_vjp
    def nt(a, b):
        return _raw_dot(a, b, _NT, hi)

    @jax.custom_vjp
    def tn(a, b):
        return _raw_dot(a, b, _TN, hi)

    nn.defvjp(lambda a, b: (nn(a, b), (a, b)), lambda r, g: (nt(g, r[1]), tn(r[0], g)))
    nt.defvjp(lambda a, b: (nt(a, b), (a, b)), lambda r, g: (nn(g, r[1]), tn(g, r[0])))
    tn.defvjp(lambda a, b: (tn(a, b), (a, b)), lambda r, g: (nt(r[1], g), nn(r[0], g)))
    return nn, nt, tn


_nn, _nt, _tn = _make_dots(False)
_nn_hi, _nt_hi, _tn_hi = _make_dots(True)
_nn_h3, _nt_h3, _tn_h3 = _make_dots('high')


def _sigmoid(x):
    return 1.0 / (1.0 + jnp.exp(-x))


def _silu(x):
    return x * _sigmoid(x)


def _softplus(x):
    return jnp.maximum(x, 0.0) + jnp.log(1.0 + jnp.exp(-jnp.abs(x)))


def _rms(x, w, n=None):
    n = x.shape[-1] if n is None else n
    ms = jnp.sum(x * x, axis=-1, keepdims=True) * (1.0 / n)
    return x * lax.rsqrt(ms + EPS) * w


def _spec2(T, w, cb=0):
    return pl.BlockSpec((T, w), lambda i: (i, cb))


def _full_spec(a):
    nd = a.ndim
    return pl.BlockSpec(a.shape, lambda i: (0,) * nd)


def _tw_fwd(name, fn, acts, params, outs, L, T):
    na, npar = len(acts), len(params)

    def kern(*refs):
        a = [r[...].astype(F32) for r in refs[:na]]
        p = [r[...].astype(F32) for r in refs[na:na + npar]]
        res = fn(a, p)
        for r, o in zip(refs[na + npar:], res):
            r[...] = o.astype(r.dtype)

    return pl.pallas_call(
        kern, name=name, grid=(L // T,),
        in_specs=[s for _, s in acts] + [_full_spec(p) for p in params],
        out_specs=[_spec2(T, w) for w, _ in outs],
        out_shape=[jax.ShapeDtypeStruct((L, w), dt) for w, dt in outs],
        compiler_params=_cparams(("arbitrary",)),
    )(*[a for a, _ in acts], *params)


def _tw_bwd(name, fn, acts, params, douts, L, T, act_grad, par_grad, addto=None):
    na, npar, nd = len(acts), len(params), len(douts)
    addto = addto or {}
    add_keys = sorted(addto)
    ga = [k for k in range(na) if act_grad[k]]
    gp = [k for k in range(npar) if par_grad[k]]

    def kern(*refs):
        i = pl.program_id(0)
        a = [r[...].astype(F32) for r in refs[:na]]
        p = [r[...].astype(F32) for r in refs[na:na + npar]]
        g = [r[...].astype(F32) for r in refs[na + npar:na + npar + nd]]
        pos = na + npar + nd
        adds = [r[...].astype(F32) for r in refs[pos:pos + len(add_keys)]]
        pos += len(add_keys)
        da_refs = refs[pos:pos + len(ga)]
        dp_refs = refs[pos + len(ga):]

        def f(ad, pd):
            af, pf = list(a), list(p)
            for k, v in zip(ga, ad):
                af[k] = v
            for k, v in zip(gp, pd):
                pf[k] = v
            return fn(af, pf)

        _, vjp = jax.vjp(f, [a[k] for k in ga], [p[k] for k in gp])
        dad, dpd = vjp(list(g))
        for n, (r, d) in enumerate(zip(da_refs, dad)):
            if n in addto:
                d = d + adds[add_keys.index(n)]
            r[...] = d.astype(r.dtype)

        @pl.when(i == 0)
        def _():
            for r in dp_refs:
                r[...] = jnp.zeros(r.shape, F32)

        for r, d in zip(dp_refs, dpd):
            r[...] += d

    def width(spec):
        return spec.block_shape[-1]

    res = pl.pallas_call(
        kern, name=name, grid=(L // T,),
        in_specs=[s for _, s in acts] + [_full_spec(p) for p in params] + [s for _, s in douts]
        + [addto[k][1] for k in add_keys],
        out_specs=[_spec2(T, width(acts[k][1])) for k in ga] + [_full_spec(params[k]) for k in gp],
        out_shape=[jax.ShapeDtypeStruct((L, width(acts[k][1])), F32) for k in ga]
        + [jax.ShapeDtypeStruct(params[k].shape, F32) for k in gp],
        compiler_params=_cparams(("arbitrary",)),
    )(*[a for a, _ in acts], *params, *[a for a, _ in douts], *[addto[k][0] for k in add_keys])
    return list(res[:len(ga)]), list(res[len(ga):])


def _mm(name, a, b, mode, out_dtype, tm, tn, tk):
    halves_a = a.shape[-1] if (a.ndim == 3 and mode == 'nt') else None
    halves_b = b.shape[-1] if (b.ndim == 3 and mode == 'tn') else None
    if mode == 'nn':
        (M, K), N = a.shape, b.shape[1]
    elif mode == 'nt':
        M, K, N = a.shape[-2], (2 * halves_a if halves_a else a.shape[1]), b.shape[0]
    else:
        (K, M), N = a.shape, (2 * halves_b if halves_b else b.shape[1])
    tm = _tile(M, tm)
    tn = _tile(halves_b or N, tn)
    tk = _tile(halves_a or K, tk)
    nk = K // tk
    if mode == 'nn':
        a_spec = pl.BlockSpec((tm, tk), lambda i, j, k: (i, k))
        b_spec = pl.BlockSpec((tk, tn), lambda i, j, k: (k, j))
        dims = _NN
    elif mode == 'nt':
        a_spec = pl.BlockSpec((tm, tk), lambda i, j, k: (i, k))
        if halves_a:
            per = halves_a // tk
            a_spec = pl.BlockSpec((None, tm, tk), lambda i, j, k: (k // per, i, k % per))
        b_spec = pl.BlockSpec((tn, tk), lambda i, j, k: (j, k))
        dims = _NT
    else:
        a_spec = pl.BlockSpec((tk, tm), lambda i, j, k: (k, i))
        b_spec = pl.BlockSpec((tk, tn), lambda i, j, k: (k, j))
        if halves_b:
            per = halves_b // tn
            b_spec = pl.BlockSpec((None, tk, tn), lambda i, j, k: (j // per, k, j % per))
        dims = _TN

    def kern(a_ref, b_ref, o_ref, acc):
        k = pl.program_id(2)

        @pl.when(k == 0)
        def _():
            acc[...] = jnp.zeros(acc.shape, F32)

        acc[...] += lax.dot_general(a_ref[...].astype(BF16), b_ref[...].astype(BF16), (dims, ((), ())),
                                    preferred_element_type=F32)

        @pl.when(k == nk - 1)
        def _():
            o_ref[...] = acc[...].astype(o_ref.dtype)

    return pl.pallas_call(
        kern, name=name, grid=(M // tm, N // tn, nk),
        in_specs=[a_spec, b_spec],
        out_specs=pl.BlockSpec((tm, tn), lambda i, j, k: (i, j)),
        out_shape=jax.ShapeDtypeStruct((M, N), out_dtype),
        scratch_shapes=[pltpu.VMEM((tm, tn), F32)],
        compiler_params=_cparams(("arbitrary", "arbitrary", "arbitrary")),
    )(a, b)


def _host(kern, n_in, n_out, side, grid, n_scratch=0):
    if side is None:
        return kern, [], [], [], [], []
    ni, no = len(side.ins), len(side.out_shapes)

    def hosted(*refs):
        ins, s_in = refs[:n_in], refs[n_in:n_in + ni]
        pos = n_in + ni
        outs, s_out = refs[pos:pos + n_out], refs[pos + n_out:pos + n_out + no]
        pos += n_out + no
        own, sems = refs[pos:pos + n_scratch], refs[pos + n_scratch:]
        ids = [pl.program_id(d) for d in range(len(grid))]
        first = functools.reduce(lambda a, b: a & b, [i == 0 for i in ids])
        last = functools.reduce(lambda a, b: a & b, [i == g - 1 for i, g in zip(ids, grid)])

        @pl.when(first)
        def _():
            side.start(s_in, s_out, sems)

        kern(*ins, *outs, *own)

        @pl.when(last)
        def _():
            side.finish(s_in, s_out, sems)

    return hosted, [_ANY] * ni, [_ANY] * no, side.out_shapes, side.scratch(), side.ins


def _flash_fwd(name, q, k, v, H, rep, scale, L, side=None):
    tq = min(512, L)
    nq = L // tq
    KC = min(2048, L)
    nkc = L // KC
    log2e = 1.0 / math.log(2.0)

    def kern(q_ref, k_ref, v_ref, o_ref, lse_ref):
        qb = q_ref[...]
        m = jnp.full((tq, 1), -1e30, F32)
        l = jnp.zeros((tq, 1), F32)
        acc = jnp.zeros((tq, LANE), F32)
        for c in range(nkc):
            kb = k_ref[c * KC:(c + 1) * KC, :]
            vb = v_ref[c * KC:(c + 1) * KC, :]
            s = lax.dot_general(qb, kb, (_NT, ((), ())), preferred_element_type=F32) * (scale * log2e)
            mn = jnp.maximum(m, jnp.max(s, axis=-1, keepdims=True))
            al = jnp.exp2(m - mn)
            p = jnp.exp2(s - mn)
            l = al * l + jnp.sum(p, axis=-1, keepdims=True)
            acc = al * acc + lax.dot_general(p.astype(BF16), vb, (_NN, ((), ())), preferred_element_type=F32)
            m = mn
        o_ref[...] = acc / l
        lse_ref[...] = m * math.log(2.0) + jnp.log(l)

    kern, s_in, s_out, s_shapes, s_scratch, s_args = _host(kern, 3, 2, side, (H, nq))
    res = pl.pallas_call(
        kern, name=name, grid=(H, nq),
        in_specs=[pl.BlockSpec((tq, LANE), lambda h, i: (i, h)),
                  pl.BlockSpec((L, LANE), lambda h, i: (0, h // rep)),
                  pl.BlockSpec((L, LANE), lambda h, i: (0, h // rep))] + s_in,
        out_specs=[pl.BlockSpec((tq, LANE), lambda h, i: (i, h)),
                   pl.BlockSpec((tq, 1), lambda h, i: (h * nq + i, 0))] + s_out,
        out_shape=[jax.ShapeDtypeStruct((L, H * LANE), F32), jax.ShapeDtypeStruct((H * L, 1), F32)] + s_shapes,
        scratch_shapes=s_scratch,
        compiler_params=_cparams(("arbitrary", "arbitrary")),
    )(q, k, v, *s_args)
    return res[0], res[1], list(res[2:])


def _flash_bwd(name, q, k, v, o, lse, do, H, rep, scale, L, side=None):
    tq = min(256, L)
    nq = L // tq
    KC = min(2048, L)
    nkc = L // KC
    Hkv = H // rep

    def kern(q_ref, k_ref, v_ref, o_ref, lse_ref, do_ref, dq_ref, dk_ref, dv_ref):
        h = pl.program_id(0)
        i = pl.program_id(1)

        @pl.when((i == 0) & (h % rep == 0))
        def _():
            dk_ref[...] = jnp.zeros(dk_ref.shape, F32)
            dv_ref[...] = jnp.zeros(dv_ref.shape, F32)

        qb = q_ref[...]
        do = do_ref[...]
        dob = do.astype(BF16)
        delta = jnp.sum(do * o_ref[...], axis=-1, keepdims=True)
        lse = lse_ref[...]
        dq = jnp.zeros((tq, LANE), F32)
        for c in range(nkc):
            sl = slice(c * KC, (c + 1) * KC)
            kb = k_ref[sl, :]
            vb = v_ref[sl, :]
            s = lax.dot_general(qb, kb, (_NT, ((), ())), preferred_element_type=F32) * scale
            p = jnp.exp(s - lse)
            dp = lax.dot_general(dob, vb, (_NT, ((), ())), preferred_element_type=F32)
            ds = (p * (dp - delta) * scale).astype(BF16)
            dq = dq + lax.dot_general(ds, kb, (_NN, ((), ())), preferred_element_type=F32)
            dk_ref[sl, :] += lax.dot_general(ds, qb, (_TN, ((), ())), preferred_element_type=F32)
            dv_ref[sl, :] += lax.dot_general(p.astype(BF16), dob, (_TN, ((), ())), preferred_element_type=F32)
        dq_ref[...] = dq

    kern, s_in, s_out, s_shapes, s_scratch, s_args = _host(kern, 6, 3, side, (H, nq))
    res = pl.pallas_call(
        kern, name=name, grid=(H, nq),
        in_specs=[pl.BlockSpec((tq, LANE), lambda h, i: (i, h)),
                  pl.BlockSpec((L, LANE), lambda h, i: (0, h // rep)),
                  pl.BlockSpec((L, LANE), lambda h, i: (0, h // rep)),
                  pl.BlockSpec((tq, LANE), lambda h, i: (i, h)),
                  pl.BlockSpec((tq, 1), lambda h, i: (h * nq + i, 0)),
                  pl.BlockSpec((tq, LANE), lambda h, i: (i, h))] + s_in,
        out_specs=[pl.BlockSpec((tq, LANE), lambda h, i: (i, h)),
                   pl.BlockSpec((L, LANE), lambda h, i: (0, h // rep)),
                   pl.BlockSpec((L, LANE), lambda h, i: (0, h // rep))] + s_out,
        out_shape=[jax.ShapeDtypeStruct((L, H * LANE), F32), jax.ShapeDtypeStruct((L, Hkv * LANE), F32),
                   jax.ShapeDtypeStruct((L, Hkv * LANE), F32)] + s_shapes,
        scratch_shapes=s_scratch,
        compiler_params=_cparams(("arbitrary", "arbitrary")),
    )(q, k, v, o, lse, do, *s_args)
    return res[0], res[1], res[2], list(res[3:])


def _shift_dn(x, first_row):
    row = lax.broadcasted_iota(jnp.int32, x.shape, 0)
    return jnp.where(row == 0, first_row, pltpu.roll(x, 1, 0))


def _shift_up(x, last_row):
    n = x.shape[0]
    row = lax.broadcasted_iota(jnp.int32, x.shape, 0)
    return jnp.where(row == n - 1, last_row, pltpu.roll(x, n - 1, 0))


def _halo_specs(ndim, lead, T, tc, cb0, L):
    r8 = T // 8
    last8 = L // 8 - 1
    if ndim == 2:
        return [pl.BlockSpec((T, tc), lambda j, i: (i, cb0 + j)),
                pl.BlockSpec((8, tc), lambda j, i: (jnp.maximum(i * r8 - 1, 0), cb0 + j)),
                pl.BlockSpec((8, tc), lambda j, i: (jnp.minimum((i + 1) * r8, last8), cb0 + j))]
    return [pl.BlockSpec((None, T, tc), lambda j, i: (lead, i, cb0 + j)),
            pl.BlockSpec((None, 8, tc), lambda j, i: (lead, jnp.maximum(i * r8 - 1, 0), cb0 + j)),
            pl.BlockSpec((None, 8, tc), lambda j, i: (lead, jnp.minimum((i + 1) * r8, last8), cb0 + j))]


def _conv_rows(x_ref, xp_ref, xn_ref, w, first, last):
    x = x_ref[...]
    T = x.shape[0]
    w0, w1, w2, b = w[0:1], w[1:2], w[2:3], w[3:4]
    pr = jnp.where(first, 0.0, xp_ref[7:8, :])
    pr2 = jnp.where(first, 0.0, xp_ref[6:7, :])
    nr = jnp.where(last, 0.0, xn_ref[0:1, :])
    nr2 = jnp.where(last, 0.0, xn_ref[1:2, :])
    xm1 = _shift_dn(x, pr)
    xp1 = _shift_up(x, nr)
    pre = xm1 * w0 + x * w1 + xp1 * w2 + b
    pre_m1 = pr2 * w0 + pr * w1 + x[0:1] * w2 + b
    pre_T = x[T - 1:T] * w0 + nr * w1 + nr2 * w2 + b
    return x, xm1, xp1, pre, pre_m1, pre_T


def _conv_grads(dpre, dpre_m1, dpre_T, x, xm1, xp1, w):
    dx = _shift_up(dpre, dpre_T) * w[0:1] + dpre * w[1:2] + _shift_dn(dpre, dpre_m1) * w[2:3]
    row = lax.broadcasted_iota(jnp.int32, (8, x.shape[1]), 0)
    dw = (jnp.where(row == 0, jnp.sum(dpre * xm1, axis=0, keepdims=True), 0.0)
          + jnp.where(row == 1, jnp.sum(dpre * x, axis=0, keepdims=True), 0.0)
          + jnp.where(row == 2, jnp.sum(dpre * xp1, axis=0, keepdims=True), 0.0)
          + jnp.where(row == 3, jnp.sum(dpre, axis=0, keepdims=True), 0.0))
    return dx, dw


def _conv_fwd(name, x, col0, C, w8, act, L, tc):
    T = min(256, L)
    nt = L // T
    cb0 = col0 // tc

    def kern(x_ref, xp_ref, xn_ref, w_ref, o_ref):
        i = pl.program_id(1)
        x = x_ref[...]
        w = w_ref[...]
        pr = jnp.where(i == 0, 0.0, xp_ref[7:8, :])
        nr = jnp.where(i == nt - 1, 0.0, xn_ref[0:1, :])
        pre = _shift_dn(x, pr) * w[0:1] + x * w[1:2] + _shift_up(x, nr) * w[2:3] + w[3:4]
        o_ref[...] = _silu(pre) if act else pre

    return pl.pallas_call(
        kern, name=name, grid=(C // tc, nt),
        in_specs=_halo_specs(2, None, T, tc, cb0, L) + [pl.BlockSpec((8, tc), lambda j, i: (0, j))],
        out_specs=pl.BlockSpec((T, tc), lambda j, i: (i, j)),
        out_shape=jax.ShapeDtypeStruct((L, C), F32),
        compiler_params=_cparams(("arbitrary", "arbitrary")),
    )(x, x, x, w8)


def _conv_bwd(name, x, col0, C, w8, act, gs, L, tc):
    T = min(256, L)
    nt = L // T
    cb0 = col0 // tc
    ng = len(gs)

    def dact(pre, g):
        if not act:
            return g
        s = _sigmoid(pre)
        return g * (s * (1.0 + pre * (1.0 - s)))

    def kern(*refs):
        x_ref, xp_ref, xn_ref, w_ref = refs[:4]
        g_refs = refs[4:4 + 3 * ng]
        dx_ref, dw_ref = refs[4 + 3 * ng:]
        i = pl.program_id(1)
        first = i == 0
        last = i == nt - 1
        w = w_ref[...]
        g = g_refs[0][...]
        gp = g_refs[1][7:8, :]
        gn = g_refs[2][0:1, :]
        for n in range(1, ng):
            g = g + g_refs[3 * n][...]
            gp = gp + g_refs[3 * n + 1][7:8, :]
            gn = gn + g_refs[3 * n + 2][0:1, :]
        x, xm1, xp1, pre, pre_m1, pre_T = _conv_rows(x_ref, xp_ref, xn_ref, w, first, last)
        dpre_m1 = jnp.where(first, 0.0, dact(pre_m1, gp))
        dpre_T = jnp.where(last, 0.0, dact(pre_T, gn))
        dx_ref[...], dw = _conv_grads(dact(pre, g), dpre_m1, dpre_T, x, xm1, xp1, w)

        @pl.when(first)
        def _():
            dw_ref[...] = jnp.zeros((8, tc), F32)

        dw_ref[...] += dw

    g_specs, g_args = [], []
    for arr, lead in gs:
        g_specs += _halo_specs(arr.ndim, lead, T, tc, 0, L)
        g_args += [arr, arr, arr]
    return pl.pallas_call(
        kern, name=name, grid=(C // tc, nt),
        in_specs=_halo_specs(2, None, T, tc, cb0, L) + [pl.BlockSpec((8, tc), lambda j, i: (0, j))] + g_specs,
        out_specs=[pl.BlockSpec((T, tc), lambda j, i: (i, j)), pl.BlockSpec((8, tc), lambda j, i: (0, j))],
        out_shape=[jax.ShapeDtypeStruct((L, C), F32), jax.ShapeDtypeStruct((8, C), F32)],
        compiler_params=_cparams(("arbitrary", "arbitrary")),
    )(x, x, x, w8, *g_args)


FFN_TC = 1408


def _ffn_gate_fwd(name, u, w8, L):
    T = min(256, L)
    nt = L // T
    ncb = D_FF // FFN_TC

    def kern(xg, xgp, xgn, xu, xup, xun, wg_ref, wu_ref, o_ref):
        i = pl.program_id(1)
        pre_g = _conv_rows(xg, xgp, xgn, wg_ref[...], i == 0, i == nt - 1)[3]
        pre_u = _conv_rows(xu, xup, xun, wu_ref[...], i == 0, i == nt - 1)[3]
        o_ref[...] = (_silu(pre_g) * pre_u).astype(BF16)

    return pl.pallas_call(
        kern, name=name, grid=(ncb, nt),
        in_specs=_halo_specs(2, None, T, FFN_TC, 0, L) + _halo_specs(2, None, T, FFN_TC, ncb, L)
        + [pl.BlockSpec((8, FFN_TC), lambda j, i: (0, j)), pl.BlockSpec((8, FFN_TC), lambda j, i: (0, j + ncb))],
        out_specs=pl.BlockSpec((T, FFN_TC), lambda j, i: (i, j)),
        out_shape=jax.ShapeDtypeStruct((L, D_FF), BF16),
        compiler_params=_cparams(("arbitrary", "arbitrary")),
    )(u, u, u, u, u, u, w8, w8)


def _ffn_gate_bwd(name, u, w8, da, L):
    T = min(128, L)
    nt = L // T
    ncb = D_FF // FFN_TC

    def kern(xg, xgp, xgn, xu, xup, xun, wg_ref, wu_ref, d_ref, dp_ref, dn_ref, du_ref, dwg_ref, dwu_ref):
        i = pl.program_id(1)
        first = i == 0
        last = i == nt - 1
        wg = wg_ref[...]
        wu = wu_ref[...]
        g, gm1, gp1, pg, pg_m1, pg_T = _conv_rows(xg, xgp, xgn, wg, first, last)
        v, vm1, vp1, pu, pu_m1, pu_T = _conv_rows(xu, xup, xun, wu, first, last)

        def dpre(pg_, pu_, d):
            s = _sigmoid(pg_)
            return d * pu_ * (s * (1.0 + pg_ * (1.0 - s))), d * (pg_ * s)

        dg, dv = dpre(pg, pu, d_ref[...])
        dg_m1, dv_m1 = dpre(pg_m1, pu_m1, jnp.where(first, 0.0, dp_ref[7:8, :]))
        dg_T, dv_T = dpre(pg_T, pu_T, jnp.where(last, 0.0, dn_ref[0:1, :]))
        du_ref[0], dwg = _conv_grads(dg, dg_m1, dg_T, g, gm1, gp1, wg)
        du_ref[1], dwu = _conv_grads(dv, dv_m1, dv_T, v, vm1, vp1, wu)

        @pl.when(first)
        def _():
            dwg_ref[...] = jnp.zeros(dwg_ref.shape, F32)
            dwu_ref[...] = jnp.zeros(dwu_ref.shape, F32)

        dwg_ref[...] += dwg
        dwu_ref[...] += dwu

    wspec = pl.BlockSpec((8, FFN_TC), lambda j, i: (0, j))
    du, dwg, dwu = pl.pallas_call(
        kern, name=name, grid=(ncb, nt),
        in_specs=_halo_specs(2, None, T, FFN_TC, 0, L) + _halo_specs(2, None, T, FFN_TC, ncb, L)
        + [wspec, pl.BlockSpec((8, FFN_TC), lambda j, i: (0, j + ncb))] + _halo_specs(2, None, T, FFN_TC, 0, L),
        out_specs=[pl.BlockSpec((2, T, FFN_TC), lambda j, i: (0, i, j)), wspec, wspec],
        out_shape=[jax.ShapeDtypeStruct((2, L, D_FF), F32), jax.ShapeDtypeStruct((8, D_FF), F32),
                   jax.ShapeDtypeStruct((8, D_FF), F32)],
        compiler_params=_cparams(("arbitrary", "arbitrary")),
    )(u, u, u, u, u, u, w8, w8, da, da, da)
    return du, jnp.concatenate([dwg, dwu], axis=1)


def _masks(Q, rev):
    ri = lax.broadcasted_iota(jnp.int32, (Q, Q), 0)
    ci = lax.broadcasted_iota(jnp.int32, (Q, Q), 1)
    diff = (ri - ci) * (1 - 2 * rev)
    return diff >= 0, diff > 0


def _lane_pick(v, sel):
    return jnp.sum(v * sel, axis=-1, keepdims=True)


def _head_rows(v_all, Q, rev):
    r = lax.broadcasted_iota(jnp.int32, (HEADS * Q, LANE), 0)
    l = lax.broadcasted_iota(jnp.int32, (HEADS * Q, LANE), 1)
    pick = jnp.zeros((HEADS * Q, LANE), F32)
    for h in range(HEADS):
        pick = jnp.where((r >= h * Q) & (r < (h + 1) * Q) & (l == rev * 4 + h), 1.0, pick)
    return _nt_hi(pick, v_all)


def _ssd_chunk(S, x, B, C, dtraw, alog, dtb, rev, kept=None):
    Q = dtraw.shape[0]
    incl, _ = _masks(Q, rev)
    tri = incl.astype(F32)
    dt = _softplus(dtraw + dtb)
    a_all = dt * (-jnp.exp(alog))
    acum_all = _nn_hi(tri, a_all)
    total_all = jnp.sum(a_all, axis=0, keepdims=True)
    lane = lax.broadcasted_iota(jnp.int32, (1, LANE), 1)
    rows_all = _head_rows(acum_all, Q, rev)
    ys, Sn = [], []
    for h in range(HEADS):
        g = h // 2
        sel = (lane == rev * 4 + h).astype(F32)
        acum = _lane_pick(acum_all, sel)
        dth = _lane_pick(dt, sel)
        tot = _lane_pick(total_all, sel)
        seg = acum - rows_all[h * Q:(h + 1) * Q, :]
        decay = jnp.exp(jnp.where(incl, seg, -1e30))
        xdt = x[h] * dth
        Sh = S[HD * h:HD * (h + 1), :]
        scores = _nt(C[g], B[g]) * decay
        y_diag = _nn(scores, xdt)
        states = _tn(xdt, B[g] * jnp.exp(tot - acum))
        y_off = _nt(C[g], Sh) * jnp.exp(acum)
        ys.append(y_diag + y_off)
        Sn.append(Sh * jnp.exp(tot) + states)
    return ys, jnp.concatenate(Sn, axis=0), []


def _inv_unit_raw(Lm):
    N = Lm.shape[0]
    Q = D_CHUNK
    ri = lax.broadcasted_iota(jnp.int32, (N, N), 0)
    ci = lax.broadcasted_iota(jnp.int32, (N, N), 1)
    X = (ri == ci).astype(F32) - Lm
    P = _raw_dot(Lm, Lm, _NN, False)
    n = 4
    while n < Q:
        XP = _raw_dot(jnp.concatenate([X, P], axis=0), P, _NN, False)
        X, P = X + XP[:N], XP[N:]
        n *= 2
    return X + _raw_dot(X, P, _NN, False)


@jax.custom_vjp
def _inv_unit(Lm, T_saved):
    return _inv_unit_raw(Lm) if T_saved is None else T_saved


def _inv_unit_f(Lm, T_saved):
    T = _inv_unit_raw(Lm) if T_saved is None else T_saved
    return T, T


def _inv_unit_b(T, g):
    return -_raw_dot(_raw_dot(T, g, _TN, False), T, _NT, False), None


_inv_unit.defvjp(_inv_unit_f, _inv_unit_b)


def _delta_chunk(S, q, k, v, braw, araw, alog, dtb, rev, kept=None):
    Q = braw.shape[0]
    N = HEADS * Q
    tri = _masks(Q, rev)[0].astype(F32)
    ri = lax.broadcasted_iota(jnp.int32, (N, N), 0)
    ci = lax.broadcasted_iota(jnp.int32, (N, N), 1)
    sh = int(math.log2(Q))
    same = (ri >> sh) == (ci >> sh)
    diff = (ri - ci) * (1 - 2 * rev)
    incl = same & (diff >= 0)
    strict = same & (diff > 0)
    beta_all = _sigmoid(braw)
    g_all = -jnp.exp(alog) * _softplus(araw + dtb)
    G_all = _nn_hi(tri, g_all)
    Gtot_all = jnp.sum(g_all, axis=0, keepdims=True)
    r = lax.broadcasted_iota(jnp.int32, (N, LANE), 0)
    l = lax.broadcasted_iota(jnp.int32, (N, LANE), 1)
    selm = (l == rev * 4 + (r >> sh)).astype(F32)
    rows4 = lambda a: jnp.concatenate([a] * HEADS, axis=0)
    XG = rows4(G_all) * selm
    G = jnp.sum(XG, axis=-1, keepdims=True)
    bt = jnp.sum(rows4(beta_all) * selm, axis=-1, keepdims=True)
    Gtot = jnp.sum(Gtot_all * selm, axis=-1, keepdims=True)
    decay = jnp.exp(jnp.where(incl, G - _nt_h3(jnp.ones((N, LANE), F32), XG), -1e30))
    qs, ks, vs = (jnp.concatenate(t, axis=0) for t in (q, k, v))
    qn = qs * lax.rsqrt(jnp.sum(qs * qs, axis=-1, keepdims=True) + 1e-6)
    kn = ks * lax.rsqrt(jnp.sum(ks * ks, axis=-1, keepdims=True) + 1e-6)
    qc = qn * (HD ** -0.5)
    kb = kn * bt
    T = _inv_unit(jnp.where(strict, _nt(kb, kn) * decay, 0.0), None if kept is None else kept[0])
    eG = jnp.exp(G)
    u = _nn(T, vs * bt)
    w = _nn(T, kb * eG)
    qk = _nt(qc, kn) * decay
    spread = (lax.broadcasted_iota(jnp.int32, (HD, N), 0)
              == (lax.broadcasted_iota(jnp.int32, (HD, N), 1) & (HD - 1))).astype(F32)
    wide = lambda a: jnp.where(same, _nn(a, spread), 0.0)
    v_new = u - _nn(wide(w), S)
    o = _nn(wide(qc * eG), S) + _nn(qk, v_new)
    S_new = S * jnp.exp(Gtot) + _tn(wide(kn * jnp.exp(Gtot - G)), v_new)
    return [o[Q * h:Q * (h + 1), :] for h in range(HEADS)], S_new, [T]


def _seq_pieces(ref, r0, Q, splits):
    if splits is None:
        return ref[r0:r0 + Q, :]
    return [[ref[r0:r0 + Q, o + w * t:o + w * (t + 1)] for t in range(n)] for o, w, n in splits]


def _store_pieces(ref, r0, Q, splits, vals, extra=None):
    if splits is None:
        ref[r0:r0 + Q, :] = vals
        return
    for g, (o, w, n) in enumerate(splits):
        for t in range(n):
            v = vals[g][t]
            if extra is not None and g == 0:
                v = v + extra[r0:r0 + Q, o + w * t:o + w * (t + 1)]
            ref[r0:r0 + Q, o + w * t:o + w * (t + 1)] = v


def _flat(ins):
    out = []
    for v in ins:
        if isinstance(v, list):
            out.extend(v)
        else:
            out.append(v)
    return out


def _scan_fwd(name, chunk_fn, seqs, rows, Q, L, CH, kept_shapes=()):
    nc = L // Q
    nb = nc // CH
    ns, nr = len(seqs), len(rows)
    nk = 1 + len(kept_shapes)
    BQ = Q * CH

    def kern(*refs):
        s_refs = (refs[:ns], refs[ns:2 * ns])
        r_refs = refs[2 * ns:2 * ns + nr]
        pos = 2 * ns + nr
        y_refs = refs[pos:pos + 2]
        k_refs = (refs[pos + 2:pos + 2 + nk], refs[pos + 2 + nk:pos + 2 + 2 * nk])
        S_scr = refs[pos + 2 + 2 * nk]
        i = pl.program_id(0)

        @pl.when(i == 0)
        def _():
            S_scr[...] = jnp.zeros(S_scr.shape, F32)

        rws = [r[...] for r in r_refs]
        for d in (0, 1):
            S = S_scr[d]
            for cc in range(CH):
                c = cc if d == 0 else CH - 1 - cc
                k_refs[d][0][c] = S
                ins = [_seq_pieces(r, c * Q, Q, sp) for r, (_, _, _, sp) in zip(s_refs[d], seqs)]
                ys, S, kept = chunk_fn(S, *_flat(ins), *rws, d)
                for r, v in zip(k_refs[d][1:], kept):
                    r[c] = v
                for h in range(HEADS):
                    y_refs[d][c * Q:(c + 1) * Q, HD * h:HD * (h + 1)] = ys[h]
            S_scr[d] = S

    fwd_specs = [pl.BlockSpec((BQ, w), functools.partial(lambda i, cb: (i, cb), cb=cb)) for _, w, cb, _ in seqs]
    rev_specs = [pl.BlockSpec((BQ, w), functools.partial(lambda i, cb: (nb - 1 - i, cb), cb=cb))
                 for _, w, cb, _ in seqs]
    arrs = [a for a, _, _, _ in seqs]
    k_shapes = [(GROUP_W, HD)] + list(kept_shapes)
    res = pl.pallas_call(
        kern, name=name, grid=(nb,),
        in_specs=fwd_specs + rev_specs + [pl.BlockSpec((1, LANE), lambda i: (0, 0)) for _ in rows],
        out_specs=[pl.BlockSpec((BQ, GROUP_W), lambda i: (i, 0)),
                   pl.BlockSpec((BQ, GROUP_W), lambda i: (nb - 1 - i, 0))]
        + [pl.BlockSpec((CH,) + s, lambda i: (i, 0, 0)) for s in k_shapes]
        + [pl.BlockSpec((CH,) + s, lambda i: (nb - 1 - i, 0, 0)) for s in k_shapes],
        out_shape=[jax.ShapeDtypeStruct((L, GROUP_W), F32)] * 2
        + [jax.ShapeDtypeStruct((nc,) + s, F32) for s in k_shapes] * 2,
        scratch_shapes=[pltpu.VMEM((2, GROUP_W, HD), F32)],
        compiler_params=_cparams(("arbitrary",)),
    )(*arrs, *arrs, *rows)
    return res[0], res[1], list(res[2:2 + nk]), list(res[2 + nk:])


def _scan_bwd(name, chunk_fn, seqs, rows, ssaves, dy, extra, Q, L, CH, side=None):
    nc = L // Q
    nb = nc // CH
    BQ = Q * CH
    ns, nr = len(seqs), len(rows)
    nk = len(ssaves[0])
    has_extra = extra is not None

    def kern(*refs):
        s_refs = (refs[:ns], refs[ns:2 * ns])
        pos = 2 * ns
        r_refs = refs[pos:pos + nr]
        pos += nr
        k_refs = (refs[pos:pos + nk], refs[pos + nk:pos + 2 * nk])
        pos += 2 * nk
        dy_refs = refs[pos:pos + 2]
        pos += 2
        ex_ref = refs[pos] if has_extra else None
        pos += 1 if has_extra else 0
        ds_refs = (refs[pos:pos + ns], refs[pos + ns:pos + 2 * ns])
        pos += 2 * ns
        dr_refs = refs[pos:pos + nr]
        dS_scr = refs[pos + nr]
        i = pl.program_id(0)

        @pl.when(i == 0)
        def _():
            dS_scr[...] = jnp.zeros(dS_scr.shape, F32)
            for r in dr_refs:
                r[...] = jnp.zeros(r.shape, F32)

        rws = [r[...] for r in r_refs]
        dr_acc = [jnp.zeros((1, LANE), F32) for _ in rows]
        for d in (0, 1):
            dS = dS_scr[d]
            for cc in range(CH):
                c = CH - 1 - cc if d == 0 else cc
                S = k_refs[d][0][c]
                kept = [r[c] for r in k_refs[d][1:]]
                dys = [dy_refs[d][c * Q:(c + 1) * Q, HD * h:HD * (h + 1)] for h in range(HEADS)]
                ins = [_seq_pieces(r, c * Q, Q, sp) for r, (_, _, _, sp) in zip(s_refs[d], seqs)]
                _, vjp = jax.vjp(
                    functools.partial(
                        lambda S_, ins_, rws_, d_, kept_: chunk_fn(S_, *_flat(ins_), *rws_, d_, kept_)[:2],
                        d_=d, kept_=kept),
                    S, ins, rws)
                dS, dins, drws = vjp((dys, dS))
                for n_, (r, (_, _, _, sp)) in enumerate(zip(ds_refs[d], seqs)):
                    _store_pieces(r, c * Q, Q, sp, dins[n_],
                                  extra=ex_ref if (has_extra and d == 0 and n_ == 0) else None)
                dr_acc = [a + g for a, g in zip(dr_acc, drws)]
            dS_scr[d] = dS
        for r, g in zip(dr_refs, dr_acc):
            r[...] += g

    def blk(shape, rev, cb=0):
        nd = len(shape)
        if rev:
            return pl.BlockSpec(shape, lambda i: (i, cb) + (0,) * (nd - 2))
        return pl.BlockSpec(shape, lambda i: (nb - 1 - i, cb) + (0,) * (nd - 2))

    arrs = [a for a, _, _, _ in seqs]
    in_specs = [blk((BQ, w), False, cb) for _, w, cb, _ in seqs] + [blk((BQ, w), True, cb) for _, w, cb, _ in seqs]
    in_specs += [pl.BlockSpec((1, LANE), lambda i: (0, 0)) for _ in rows]
    in_specs += [blk((CH,) + a.shape[1:], False) for a in ssaves[0]]
    in_specs += [blk((CH,) + a.shape[1:], True) for a in ssaves[1]]
    in_specs += [blk((BQ, GROUP_W), False), blk((BQ, GROUP_W), True)]
    args = arrs + arrs + list(rows) + list(ssaves[0]) + list(ssaves[1]) + [dy, dy]
    if has_extra:
        in_specs.append(blk((BQ, GROUP_W), False))
        args.append(extra)
    kern, s_in, s_out, s_shapes, s_scratch, s_args = _host(kern, len(args), 2 * ns + nr, side, (nb,), 1)
    res = pl.pallas_call(
        kern, name=name, grid=(nb,),
        in_specs=in_specs + s_in,
        out_specs=[blk((BQ, w), False) for _, w, _, _ in seqs] + [blk((BQ, w), True) for _, w, _, _ in seqs]
        + [pl.BlockSpec((1, LANE), lambda i: (0, 0)) for _ in rows] + s_out,
        out_shape=[jax.ShapeDtypeStruct((L, w), F32) for _, w, _, _ in seqs] * 2
        + [jax.ShapeDtypeStruct((1, LANE), F32) for _ in rows] + s_shapes,
        scratch_shapes=[pltpu.VMEM((2, GROUP_W, HD), F32)] + s_scratch,
        compiler_params=_cparams(("arbitrary",)),
    )(*args, *s_args)
    return list(res[:ns]), list(res[ns:2 * ns]), list(res[2 * ns:2 * ns + nr]), list(res[2 * ns + nr:])


def _loss_call(y, tgt, L):
    T = min(256, L)

    def kern(y_ref, t_ref, dy_ref, l_ref):
        i = pl.program_id(0)
        e = y_ref[...] - t_ref[...]
        dy_ref[...] = e * (1.0 / D_MODEL)

        @pl.when(i == 0)
        def _():
            l_ref[...] = jnp.zeros(l_ref.shape, F32)

        part = 0.5 * jnp.sum(jnp.sum(e * e, axis=-1, keepdims=True) * (1.0 / D_MODEL), axis=0, keepdims=True)
        l_ref[...] += jnp.broadcast_to(part, l_ref.shape)

    return pl.pallas_call(
        kern, name="loss_head", grid=(L // T,),
        in_specs=[_spec2(T, D_MODEL), _spec2(T, D_MODEL)],
        out_specs=[_spec2(T, D_MODEL), pl.BlockSpec((8, LANE), lambda i: (0, 0))],
        out_shape=[jax.ShapeDtypeStruct((L, D_MODEL), F32), jax.ShapeDtypeStruct((8, LANE), F32)],
        compiler_params=_cparams(("arbitrary",)),
    )(y, tgt)


_ANY = pl.BlockSpec(memory_space=pl.ANY)


def _coords():
    return lax.axis_index("x"), lax.axis_index("y"), lax.axis_index("c")


class _Copies:
    def __init__(self, ins, out_shapes, copies_fn, n_remote, n_local):
        self.ins, self.out_shapes, self.copies_fn = list(ins), list(out_shapes), copies_fn
        self.n_remote, self.n_local = n_remote, n_local

    def scratch(self):
        return [pltpu.SemaphoreType.DMA((self.n_remote,)), pltpu.SemaphoreType.DMA((self.n_remote,)),
                pltpu.SemaphoreType.DMA((self.n_local,))]

    def _descr(self, in_refs, out_refs, sems):
        send_sems, recv_sems, lsems = sems
        remote, local = self.copies_fn(list(in_refs), list(out_refs))
        assert len(remote) == self.n_remote and len(local) == self.n_local
        mk = lambda k, src, dst, peer: pltpu.make_async_remote_copy(
            src_ref=src, dst_ref=dst, send_sem=send_sems.at[k], recv_sem=recv_sems.at[k], device_id=peer,
            device_id_type=MESH)
        sends = [mk(k, src, dst, peer) for k, (src, dst, _, peer) in enumerate(remote)]
        recvs = [mk(k, src, land, peer) for k, (src, _, land, peer) in enumerate(remote)]
        locs = [pltpu.make_async_copy(src, dst, lsems.at[k]) for k, (src, dst) in enumerate(local)]
        return sends, recvs, locs

    def start(self, in_refs, out_refs, sems):
        sends, _, locs = self._descr(in_refs, out_refs, sems)
        for c in locs + sends:
            c.start()

    def finish(self, in_refs, out_refs, sems):
        sends, recvs, locs = self._descr(in_refs, out_refs, sems)
        for c in recvs:
            c.wait_recv()
        for c in sends:
            c.wait_send()
        for c in locs:
            c.wait()

    def call(self, name):
        ni, no = len(self.ins), len(self.out_shapes)

        def body(*refs):
            self.start(refs[:ni], refs[ni:ni + no], refs[ni + no:])
            self.finish(refs[:ni], refs[ni:ni + no], refs[ni + no:])

        return pl.pallas_call(body, name=name, in_specs=[_ANY] * ni, out_specs=[_ANY] * no,
                              out_shape=self.out_shapes, scratch_shapes=self.scratch())(*self.ins)


def _chip_peers(x, y):
    return [(1 - x, y), (x, 1 - y), (1 - x, 1 - y)]


def _gather_copies(arrs):
    def copies_fn(ins, outs):
        x, y, c = _coords()
        me = 2 * x + y
        remote, local = [], []
        for src, out in zip(ins, outs):
            local.append((src, out.at[me]))
            for px, py in _chip_peers(x, y):
                remote.append((src, out.at[me], out.at[2 * px + py], (px, py, c)))
        return remote, local

    shapes = [jax.ShapeDtypeStruct((4,) + a.shape, a.dtype) for a in arrs]
    return _Copies(arrs, shapes, copies_fn, 3 * len(arrs), len(arrs))


def _scatter_copies(Gs, small):
    nb = len(Gs)

    def copies_fn(ins, outs):
        x, y, c = _coords()
        me = 2 * x + y
        remote, local = [], []
        for g, out in zip(ins[:nb], outs[:nb]):
            local.append((g.at[me], out.at[me]))
            for px, py in _chip_peers(x, y):
                remote.append((g.at[2 * px + py], out.at[me], out.at[2 * px + py], (px, py, c)))
        if small is not None:
            dev = 4 * x + 2 * y + c
            gs, outs_ = ins[nb], outs[nb]
            local.append((gs, outs_.at[dev]))
            for mask in range(1, 8):
                px, py, pc = x ^ (mask >> 2), y ^ ((mask >> 1) & 1), c ^ (mask & 1)
                remote.append((gs, outs_.at[dev], outs_.at[4 * px + 2 * py + pc], (px, py, pc)))
        return remote, local

    ins = list(Gs) + ([small] if small is not None else [])
    shapes = [jax.ShapeDtypeStruct(g.shape, g.dtype) for g in Gs]
    if small is not None:
        shapes.append(jax.ShapeDtypeStruct((8,) + small.shape, small.dtype))
    extra = 1 if small is not None else 0
    return _Copies(ins, shapes, copies_fn, 3 * nb + 7 * extra, nb + extra)


SWAP_STREAMS = 8


def _row_chunks(rows):
    k = SWAP_STREAMS
    if rows % (8 * k) == 0 and rows >= 64 * k:
        return [(q * (rows // k), rows // k) for q in range(k)]
    return [(0, rows)]


def _swap_copies(parts):
    chunks = [_row_chunks(p.shape[0]) for p in parts]
    n = sum(len(ch) for ch in chunks)

    def copies_fn(ins, outs):
        x, y, c = _coords()
        remote, local = [], []
        for src, out, ch in zip(ins, outs, chunks):
            for r0, nr in ch:
                rows = pl.ds(r0, nr)
                local.append((src.at[rows], out.at[c, rows]))
                remote.append((src.at[rows], out.at[c, rows], out.at[1 - c, rows], (x, y, 1 - c)))
        return remote, local

    shapes = [jax.ShapeDtypeStruct((2,) + p.shape, p.dtype) for p in parts]
    return _Copies(parts, shapes, copies_fn, n, n)


def _merge_copies(sets):
    ins = [a for s in sets for a in s.ins]
    shapes = [o for s in sets for o in s.out_shapes]

    def copies_fn(in_refs, out_refs):
        remote, local, pi, po = [], [], 0, 0
        for s in sets:
            r, l = s.copies_fn(in_refs[pi:pi + len(s.ins)], out_refs[po:po + len(s.out_shapes)])
            remote += r
            local += l
            pi += len(s.ins)
            po += len(s.out_shapes)
        return remote, local

    return _Copies(ins, shapes, copies_fn, sum(s.n_remote for s in sets), sum(s.n_local for s in sets))


def _row_tile(rows):
    best = rows
    for d in range(8, min(rows, 256) + 1, 8):
        if rows % d == 0:
            best = d
    return best


def _sum_slots(name, recv):
    n, R, W = recv.shape
    tr = _row_tile(R)

    def kern(r_ref, o_ref):
        acc = r_ref[0].astype(F32)
        for s in range(1, n):
            acc = acc + r_ref[s].astype(F32)
        o_ref[...] = acc

    return pl.pallas_call(
        kern, name=name, grid=(R // tr,),
        in_specs=[pl.BlockSpec((n, tr, W), lambda i: (0, i, 0))],
        out_specs=pl.BlockSpec((tr, W), lambda i: (i, 0)),
        out_shape=jax.ShapeDtypeStruct((R, W), F32),
        compiler_params=_cparams(("arbitrary",)),
    )(recv)


def _adamw_call(name, slots, w, m, v):
    nl = len(slots)
    n, R, W = slots[0].shape
    tr = _row_tile(R)
    nr = R // tr

    def kern(*refs):
        s_refs = refs[:nl]
        w_ref, m_ref, v_ref, g_ref, d_ref, nm_ref, nv_ref = refs[nl:]
        layer = pl.program_id(0)
        g = s_refs[0][0]
        for s in range(1, n):
            g = g + s_refs[0][s]
        for l in range(1, nl):
            gl = s_refs[l][0]
            for s in range(1, n):
                gl = gl + s_refs[l][s]
            g = jnp.where(layer == l, gl, g)
        m_ = ADAM_B1 * m_ref[...] + (1.0 - ADAM_B1) * g
        v_ = ADAM_B2 * v_ref[...] + (1.0 - ADAM_B2) * (g * g)
        m_hat = m_ / (1.0 - ADAM_B1 ** ADAM_STEP)
        v_hat = v_ / (1.0 - ADAM_B2 ** ADAM_STEP)
        g_ref[...] = g
        d_ref[...] = -ADAM_LR * (m_hat / (jnp.sqrt(v_hat) + ADAM_EPS) + ADAM_WD * w_ref[...])
        nm_ref[...] = m_
        nv_ref[...] = v_

    blk = pl.BlockSpec((None, tr, W), lambda l, i: (l, i, 0))
    return pl.pallas_call(
        kern, name=name, grid=(nl, nr),
        in_specs=[pl.BlockSpec((n, tr, W), lambda l, i: (0, i, 0)) for _ in slots] + [blk, blk, blk],
        out_specs=[blk, blk, blk, blk],
        out_shape=[jax.ShapeDtypeStruct((nl, R, W), F32)] * 4,
        compiler_params=_cparams(("arbitrary", "arbitrary")),
    )(*slots, w, m, v)


def _pack(arrs, width, row_mult):
    flat = jnp.concatenate([a.reshape(-1) for a in arrs])
    n = flat.shape[0]
    rows = -(-n // width)
    rows = -(-rows // row_mult) * row_mult
    return jnp.pad(flat, (0, rows * width - n)).reshape(rows, width)


def _unpack(buf, shapes):
    flat = buf.reshape(-1)
    out, pos = [], 0
    for s in shapes:
        n = int(np.prod(s))
        out.append(flat[pos:pos + n].reshape(s))
        pos += n
    return out


def _rope_angles(L, rot_dim):
    rows = L // GRID_W
    row = jnp.repeat(jnp.arange(rows), GRID_W).astype(F32)
    col = jnp.tile(jnp.arange(GRID_W), rows).astype(F32)
    sec = rot_dim // 2
    inv_freq = ROPE_BASE ** (-jnp.arange(0, sec, 2, dtype=F32) / sec)
    ang_r = row[:, None] * inv_freq
    ang_c = col[:, None] * inv_freq
    ang = jnp.concatenate([ang_r, ang_r, ang_c, ang_c], axis=-1)
    return jnp.cos(ang), jnp.sin(ang)


def _rot_matrix(r):
    R = np.zeros((r, r), np.float32)
    q = r // 4
    for s in range(2):
        for t in range(q):
            lo = s * (r // 2) + t
            hi = lo + q
            R[hi, lo] = -1.0
            R[lo, hi] = 1.0
    return R


def _place_tables(L, cos, sin, width, offsets):
    r = cos.shape[1]
    Rm = np.zeros((width, width), np.float32)
    R = _rot_matrix(r)
    cs, ss, pos = [], [], 0
    for o in list(offsets) + [width]:
        if o > pos:
            cs.append(jnp.ones((L, o - pos), F32))
            ss.append(jnp.zeros((L, o - pos), F32))
        if o < width:
            cs.append(cos)
            ss.append(sin)
            Rm[o:o + r, o:o + r] = R
        pos = o + r
    return jnp.concatenate(cs, axis=1), jnp.concatenate(ss, axis=1), jnp.asarray(Rm)


def _head_mean_matrix(width, stride, n):
    M = np.zeros((width, width), np.float32)
    for o in range(0, width, stride):
        M[o:o + n, o:o + n] = 1.0 / n
    return jnp.asarray(M)


def _pad_heads(w, n_heads, real, padded, axis):
    parts = jnp.split(w, n_heads, axis=axis)
    padw = [(0, 0)] * w.ndim
    padw[axis] = (0, padded - real)
    return jnp.concatenate([jnp.pad(p, padw) for p in parts], axis=axis)


def _row128(v):
    v = v.reshape(1, -1)
    return jnp.pad(v, ((0, 0), (0, LANE - v.shape[1])))


def _conv_w8(w, b):
    C = w.shape[1]
    rows = [w, jnp.zeros((1, C), F32) if b is None else b.reshape(1, C), jnp.zeros((4, C), F32)]
    return jnp.concatenate(rows, axis=0)


def _build_layer(W):
    w_in = W['w_in']
    o = 0
    cols = {}
    for name, n in [('a_cq', A_Q_LORA), ('a_ckv', A_KV_LORA), ('a_kr', A_ROPE), ('b_q', 256), ('b_k', 128),
                    ('b_v', 128), ('c_z', 256), ('c_xbc', 512), ('c_dt', 8), ('d_qkv', 768), ('d_z', 256),
                    ('d_b', 8), ('d_a', 8)]:
        cols[name] = w_in[:, o:o + n]
        o += n
    padc = lambda a, lo, width: jnp.pad(a, ((0, 0), (lo, width - lo - a.shape[1])))
    pieces = {
        'b_q': _pad_heads(cols['b_q'], 4, HD, LANE, 1), 'c_xbc': cols['c_xbc'], 'a_cq': padc(cols['a_cq'], 0, 256),
        'b_k': _pad_heads(cols['b_k'], 2, HD, LANE, 1), 'd_qkv': cols['d_qkv'],
        'b_v': _pad_heads(cols['b_v'], 2, HD, LANE, 1), 'c_z': cols['c_z'], 'd_z': cols['d_z'],
        'a_ckv': cols['a_ckv'], 'a_kr': padc(cols['a_kr'], A_NOPE, LANE), 'c_dt': padc(cols['c_dt'], 0, LANE),
        'd_b': padc(cols['d_b'], 0, LANE), 'd_a': padc(cols['d_a'], 0, LANE),
        'pad': jnp.zeros((D_MODEL, LANE), w_in.dtype)}
    out = {'w_in': jnp.concatenate([pieces[n] for n, _, _ in P_LAYOUT], axis=1)}
    out['a_q_norm'] = padc(W['a_q_norm'].reshape(1, -1), 0, 256)
    wuq = jnp.pad(W['a_w_uq'], ((0, 256 - A_Q_LORA), (0, 0)))
    out['a_w_uq'] = _pad_heads(wuq, 4, A_NOPE + A_ROPE, LANE, 1)
    out['a_kv_norm'] = W['a_kv_norm'].reshape(1, -1)
    ukv = W['a_w_ukv'].reshape(A_KV_LORA, HEADS, 2, HD)
    out['a_w_uk'] = _pad_heads(ukv[:, :, 0, :].reshape(A_KV_LORA, 256), 4, HD, LANE, 1)
    out['a_w_uv'] = _pad_heads(ukv[:, :, 1, :].reshape(A_KV_LORA, 256), 4, HD, LANE, 1)
    out['a_out_norm'] = _pad_heads(W['a_out_norm'].reshape(1, -1), 4, HD, LANE, 1)
    out['b_q_norm'] = _pad_heads(jnp.tile(W['b_q_norm'].reshape(1, -1), (1, 4)), 4, HD, LANE, 1)
    out['b_k_norm'] = _pad_heads(jnp.tile(W['b_k_norm'].reshape(1, -1), (1, 2)), 2, HD, LANE, 1)
    out['b_out_norm'] = _pad_heads(W['b_out_norm'].reshape(1, -1), 4, HD, LANE, 1)
    out['c_conv'] = _conv_w8(W['c_conv_w'], W['c_conv_b'])
    out['c_a_log'] = _row128(W['c_a_log'])
    out['c_dt_bias'] = _row128(W['c_dt_bias'])
    out['c_d_skip'] = jnp.repeat(W['c_d_skip'], HD).reshape(1, -1)
    out['c_out_norm'] = W['c_out_norm'].reshape(1, -1)
    out['d_conv'] = _conv_w8(W['d_conv_w'], None)
    out['d_a_log'] = _row128(W['d_a_log'])
    out['d_dt_bias'] = _row128(W['d_dt_bias'])
    out['d_out_norm'] = jnp.tile(W['d_out_norm'].reshape(1, -1), (1, 4))
    wo = W['w_out']
    out['w_out'] = jnp.concatenate([_pad_heads(wo[0:256], 4, HD, LANE, 0), _pad_heads(wo[256:512], 4, HD, LANE, 0),
                                    wo[512:1024]], axis=0)
    for n in ['pre_mix_norm', 'post_mix_norm', 'pre_ffn_norm', 'post_ffn_norm']:
        out[n] = W[n].reshape(1, -1)
    out['f_w_in'] = W['f_w_in']
    out['f_conv'] = _conv_w8(W['f_conv_w'], W['f_conv_b'])
    out['f_w_out'] = W['f_w_out']
    return out


def _fn_norm_in(a, p):
    return [_rms(a[0], p[0])]


def _fn_resid_norm2(a, p):
    x1 = a[0] + _rms(a[1], p[0])
    return [x1, _rms(x1, p[1])]


def _fn_resid_norm(a, p):
    return [a[0] + _rms(a[1], p[0])]


def _fn_a_prep(a, p):
    cq, ckv, kr, cosk, sink = a
    q_norm, w_uq, kv_norm, w_uk, w_uv, rq, rk = p
    cosq = jnp.concatenate([cosk] * HEADS, axis=1)
    sinq = jnp.concatenate([sink] * HEADS, axis=1)
    q = _nn(_rms(cq, q_norm, A_Q_LORA), w_uq)
    q = q * cosq + _nn_h3(q, rq) * sinq
    kvn = _rms(ckv, kv_norm)
    kr_r = kr * cosk + _nn_h3(kr, rk) * sink
    kk = _nn(kvn, w_uk) + jnp.concatenate([kr_r] * HEADS, axis=1)
    return [q, kk, _nn(kvn, w_uv)]


def _fn_b_prep(a, p):
    q, k, v, cos1, sin1 = a
    q_norm, k_norm, mq, mk, rq, rk = p
    cosq, sinq = (jnp.concatenate([t] * 4, axis=1) for t in (cos1, sin1))
    cosk, sink = (jnp.concatenate([t] * 2, axis=1) for t in (cos1, sin1))
    qn = q * lax.rsqrt(_nn_h3(q * q, mq) + EPS) * q_norm
    kn = k * lax.rsqrt(_nn_h3(k * k, mk) + EPS) * k_norm
    return [qn * cosq + _nn_h3(qn, rq) * sinq, kn * cosk + _nn_h3(kn, rk) * sink, v]


def _fn_mixer_post(a, p):
    oa, ob, yc0, yc1, xs, zc, od0, od1, zd = a
    a_norm, b_norm, dskip, c_norm, d_norm, m64 = p
    oc = _rms((yc0 + yc1 + xs * dskip) * _silu(zc), c_norm)
    od = od0 + od1
    odn = od * lax.rsqrt(_nn_h3(od * od, m64) + EPS) * d_norm * _silu(zd)
    return [jnp.concatenate([_rms(oa, a_norm, GROUP_W), _rms(ob, b_norm, GROUP_W), oc, odn], axis=1)]


def _fn_assemble(a, p):
    (dbq, dxbc, dcq, dbk, dqkv, dbv, dzc, dzd, dckv, dkr, ddt0, ddt1, db0, db1, da0, da1) = a
    return [jnp.concatenate([dbq, dxbc, dcq, dbk, dqkv, dbv, dzc, dzd, dckv, dkr, ddt0 + ddt1, db0 + db1,
                             da0 + da1, jnp.zeros_like(dckv)], axis=1)]


def _pspec(T, name):
    off, w = P_OFF[name]
    return _spec2(T, w, off // w)


def _layer_fwd(l, x, h, K, tabs, L, T, next_norm, side_a=None, late=None, side_b=None):
    n = f"l{l}_"
    sv = {'x': x, 'h': h}
    p = _mm(n + "in_proj", h, K['w_in'].astype(BF16), 'nn', F32, 1024, 1280, 1024)
    sv['p'] = p
    a_acts = [(p, _pspec(T, 'a_cq')), (p, _pspec(T, 'a_ckv')), (p, _pspec(T, 'a_kr')),
              (tabs['a_c'], _spec2(T, LANE)), (tabs['a_s'], _spec2(T, LANE))]
    a_pars = [K['a_q_norm'], K['a_w_uq'], K['a_kv_norm'], K['a_w_uk'], K['a_w_uv'], tabs['a_rq'], tabs['a_rk']]
    qa, ka, va = _tw_fwd(n + "a_prep", _fn_a_prep, a_acts, a_pars, [(512, BF16)] * 3, L, T)
    oa, lse_a, got_a = _flash_fwd(n + "a_attn", qa, ka, va, HEADS, 1, (A_NOPE + A_ROPE) ** -0.5, L, side_a)
    if late is not None:
        K = {**K, **late(got_a)}
    sv.update(a_acts=a_acts, a_pars=a_pars, qa=qa, ka=ka, va=va, oa=oa, lse_a=lse_a, K=K)
    b_acts = [(p, _pspec(T, 'b_q')), (p, _pspec(T, 'b_k')), (p, _pspec(T, 'b_v')),
              (tabs['b_c'], _spec2(T, LANE)), (tabs['b_s'], _spec2(T, LANE))]
    b_pars = [K['b_q_norm'], K['b_k_norm'], tabs['b_mq'], tabs['b_mk'], tabs['b_rq'], tabs['b_rk']]
    qb, kb, vb = _tw_fwd(n + "b_prep", _fn_b_prep, b_acts, b_pars, [(512, BF16), (256, BF16), (256, BF16)], L, T)
    ob, lse_b, sv['side'] = _flash_fwd(n + "b_attn", qb, kb, vb, HEADS, 2, HD ** -0.5, L, side_b)
    sv.update(b_acts=b_acts, b_pars=b_pars, qb=qb, kb=kb, vb=vb, ob=ob, lse_b=lse_b)
    xbc = _conv_fwd(n + "c_conv", p, P_OFF['c_xbc'][0], C_XBC, K['c_conv'], True, L, 512)
    c_seqs = [(xbc, C_XBC, 0, [(0, HD, 4), (256, HD, 2), (384, HD, 2)]),
              (p, LANE, P_OFF['c_dt'][0] // LANE, None)]
    c_rows = [K['c_a_log'], K['c_dt_bias']]
    yc0, yc1, sc0, sc1 = _scan_fwd(n + "c_ssd", _ssd_chunk, c_seqs, c_rows, C_CHUNK, L, C_PER_STEP)
    sv.update(xbc=xbc, c_seqs=c_seqs, c_rows=c_rows, sc=(sc0, sc1))
    qkv = _conv_fwd(n + "d_conv", p, P_OFF['d_qkv'][0], D_QKV, K['d_conv'], True, L, 768)
    d_seqs = [(qkv, D_QKV, 0, [(0, HD, 4), (256, HD, 4), (512, HD, 4)]),
              (p, LANE, P_OFF['d_b'][0] // LANE, None), (p, LANE, P_OFF['d_a'][0] // LANE, None)]
    d_rows = [K['d_a_log'], K['d_dt_bias']]
    od0, od1, sd0, sd1 = _scan_fwd(n + "d_delta", _delta_chunk, d_seqs, d_rows, D_CHUNK, L, D_PER_STEP,
                                   [(HEADS * D_CHUNK, HEADS * D_CHUNK)])
    sv.update(qkv=qkv, d_seqs=d_seqs, d_rows=d_rows, sd=(sd0, sd1))
    m_acts = [(oa, _spec2(T, 512)), (ob, _spec2(T, 512)), (yc0, _spec2(T, 256)), (yc1, _spec2(T, 256)),
              (xbc, _spec2(T, 256, 0)), (p, _pspec(T, 'c_z')), (od0, _spec2(T, 256)), (od1, _spec2(T, 256)),
              (p, _pspec(T, 'd_z'))]
    m_pars = [K['a_out_norm'], K['b_out_norm'], K['c_d_skip'], K['c_out_norm'], K['d_out_norm'], tabs['m64']]
    (o,) = _tw_fwd(n + "mixer_post", _fn_mixer_post, m_acts, m_pars, [(O_COLS, BF16)], L, T)
    f1 = _mm(n + "out_proj", o, K['w_out'].astype(BF16), 'nn', F32, 1024, 1024, 1536)
    r1_pars = [K['post_mix_norm'], K['pre_ffn_norm']]
    x1, h2 = _tw_fwd(n + "resid_mix", _fn_resid_norm2, [(x, _spec2(T, D_MODEL)), (f1, _spec2(T, D_MODEL))], r1_pars,
                     [(D_MODEL, F32), (D_MODEL, BF16)], L, T)
    sv.update(m_acts=m_acts, m_pars=m_pars, o=o, f1=f1, r1_pars=r1_pars, x1=x1, h2=h2)
    u = _mm(n + "ffn_in", h2, K['f_w_in'].astype(BF16), 'nn', F32, 1024, 1408, 1024)
    act = _ffn_gate_fwd(n + "ffn_gate", u, K['f_conv'], L)
    f2 = _mm(n + "ffn_out", act, K['f_w_out'].astype(BF16), 'nn', F32, 1024, 1024, 1408)
    sv.update(u=u, act=act, f2=f2)
    xf = [(x1, _spec2(T, D_MODEL)), (f2, _spec2(T, D_MODEL))]
    if next_norm is None:
        (x2,) = _tw_fwd(n + "resid_ffn", _fn_resid_norm, xf, [K['post_ffn_norm']], [(D_MODEL, F32)], L, T)
        hn = None
    else:
        x2, hn = _tw_fwd(n + "resid_ffn", _fn_resid_norm2, xf, [K['post_ffn_norm'], next_norm],
                         [(D_MODEL, F32), (D_MODEL, BF16)], L, T)
    return x2, hn, sv


def _layer_bwd(l, dx2, dhn, K, sv, tabs, L, T, next_norm, hosts=None):
    n = f"l{l}b_"
    dK = {}
    hosts = hosts or {}
    got = {}
    side = lambda name: hosts[name](dK, got) if name in hosts else None
    s2 = lambda w, cb=0: _spec2(T, w, cb)
    xf = [(sv['x1'], s2(D_MODEL)), (sv['f2'], s2(D_MODEL))]
    if next_norm is None:
        (dx1a, df2), (dK['post_ffn_norm'],) = _tw_bwd(n + "resid_ffn", _fn_resid_norm, xf, [K['post_ffn_norm']],
                                                      [(dx2, s2(D_MODEL))], L, T, [True, True], [True])
        dnext = None
    else:
        (dx1a, df2), (dK['post_ffn_norm'], dnext) = _tw_bwd(
            n + "resid_ffn", _fn_resid_norm2, xf, [K['post_ffn_norm'], next_norm],
            [(dx2, s2(D_MODEL)), (dhn, s2(D_MODEL))], L, T, [True, True], [True, True])
    dact = _mm(n + "ffn_out_dx", df2, K['f_w_out'].astype(BF16), 'nt', F32, 1024, 1408, 1024)
    dK['f_w_out'] = _mm(n + "ffn_out_dw", sv['act'], df2, 'tn', F32, 1408, 1024, 1024)
    du, dK['f_conv'] = _ffn_gate_bwd(n + "ffn_gate", sv['u'], K['f_conv'], dact, L)
    dh2 = _mm(n + "ffn_in_dx", du, K['f_w_in'].astype(BF16), 'nt', F32, 1024, 1024, 1408)
    dK['f_w_in'] = _mm(n + "ffn_in_dw", sv['h2'], du, 'tn', F32, 1024, 1408, 1024)
    (dxa, df1), (dK['post_mix_norm'], dK['pre_ffn_norm']) = _tw_bwd(
        n + "resid_mix", _fn_resid_norm2, [(sv['x'], s2(D_MODEL)), (sv['f1'], s2(D_MODEL))], sv['r1_pars'],
        [(dx1a, s2(D_MODEL)), (dh2, s2(D_MODEL))], L, T, [True, True], [True, True])
    do = _mm(n + "out_proj_dx", df1, K['w_out'].astype(BF16), 'nt', F32, 1024, 1536, 1024)
    dK['w_out'] = _mm(n + "out_proj_dw", sv['o'], df1, 'tn', F32, 1536, 1024, 1024)
    (doa, dob, dyc0, _, dxs_skip, dzc, dod0, _, dzd), mp = _tw_bwd(
        n + "mixer_post", _fn_mixer_post, sv['m_acts'], sv['m_pars'], [(do, s2(O_COLS))], L, T,
        [True] * 9, [True] * 5 + [False])
    dK['a_out_norm'], dK['b_out_norm'], dK['c_d_skip'], dK['c_out_norm'], dK['d_out_norm'] = mp
    (dqkv0, db0, da0), (dqkv1, db1, da1), (dK['d_a_log'], dK['d_dt_bias']), got['d_delta'] = _scan_bwd(
        n + "d_delta", _delta_chunk, sv['d_seqs'], sv['d_rows'], sv['sd'], dod0, None, D_CHUNK, L, D_PER_STEP,
        side('d_delta'))
    dqkv, dK['d_conv'] = _conv_bwd(n + "d_conv", sv['p'], P_OFF['d_qkv'][0], D_QKV, K['d_conv'], True,
                                   [(dqkv0, None), (dqkv1, None)], L, 768)
    (dxbc0, ddt0), (dxbc1, ddt1), (dK['c_a_log'], dK['c_dt_bias']), _ = _scan_bwd(
        n + "c_ssd", _ssd_chunk, sv['c_seqs'], sv['c_rows'], sv['sc'], dyc0, dxs_skip, C_CHUNK, L, C_PER_STEP)
    dxbc, dK['c_conv'] = _conv_bwd(n + "c_conv", sv['p'], P_OFF['c_xbc'][0], C_XBC, K['c_conv'], True,
                                   [(dxbc0, None), (dxbc1, None)], L, 512)
    dqb, dkb, dvb, got['b_attn'] = _flash_bwd(n + "b_attn", sv['qb'], sv['kb'], sv['vb'], sv['ob'], sv['lse_b'],
                                              dob, HEADS, 2, HD ** -0.5, L, side('b_attn'))
    (dbq, dbk, dbv), (dK['b_q_norm'], dK['b_k_norm']) = _tw_bwd(
        n + "b_prep", _fn_b_prep, sv['b_acts'], sv['b_pars'], [(dqb, s2(512)), (dkb, s2(256)), (dvb, s2(256))],
        L, T, [True] * 3 + [False] * 2, [True, True] + [False] * 4)
    dqa, dka, dva, got['a_attn'] = _flash_bwd(n + "a_attn", sv['qa'], sv['ka'], sv['va'], sv['oa'], sv['lse_a'],
                                              doa, HEADS, 1, (A_NOPE + A_ROPE) ** -0.5, L, side('a_attn'))
    (dcq, dckv, dkr), ap = _tw_bwd(
        n + "a_prep", _fn_a_prep, sv['a_acts'], sv['a_pars'], [(dqa, s2(512)), (dka, s2(512)), (dva, s2(512))],
        L, T, [True] * 3 + [False] * 2, [True] * 5 + [False] * 2)
    dK['a_q_norm'], dK['a_w_uq'], dK['a_kv_norm'], dK['a_w_uk'], dK['a_w_uv'] = ap
    pieces = [(dbq, s2(512)), (dxbc, s2(512)), (dcq, s2(256)), (dbk, s2(256)), (dqkv, s2(768)), (dbv, s2(256)),
              (dzc, s2(256)), (dzd, s2(256)), (dckv, s2(LANE)), (dkr, s2(LANE)),
              (ddt0, s2(LANE)), (ddt1, s2(LANE)), (db0, s2(LANE)), (db1, s2(LANE)), (da0, s2(LANE)),
              (da1, s2(LANE))]
    (dp,) = _tw_fwd(n + "assemble_dp", _fn_assemble, pieces, [], [(P_COLS, BF16)], L, T)
    dh = _mm(n + "in_proj_dx", dp, K['w_in'].astype(BF16), 'nt', F32, 1024, 1024, 1280)
    dK['w_in'] = _mm(n + "in_proj_dw", sv['h'], dp, 'tn', F32, 1024, 1280, 1024)
    return dxa, dh, dK, dnext, got


def _tables(L):
    ca, sa = _rope_angles(L, A_ROPE)
    cb, sb = _rope_angles(L, HD)
    t = {}
    t['a_c'], t['a_s'], t['a_rk'] = _place_tables(L, ca, sa, LANE, [A_NOPE])
    t['b_c'], t['b_s'], _ = _place_tables(L, cb, sb, LANE, [0])
    t['a_rq'] = _place_tables(8, ca[:8], sa[:8], 512, [LANE * h + A_NOPE for h in range(4)])[2]
    t['b_rq'] = _place_tables(8, cb[:8], sb[:8], 512, [LANE * h for h in range(4)])[2]
    t['b_rk'] = _place_tables(8, cb[:8], sb[:8], 256, [LANE * h for h in range(2)])[2]
    t['b_mq'] = _head_mean_matrix(512, LANE, HD)
    t['b_mk'] = _head_mean_matrix(256, LANE, HD)
    t['m64'] = _head_mean_matrix(256, HD, HD)
    return t


def kernel(x, pre_mix_norm, w_in, a_q_norm, a_w_uq, a_kv_norm, a_w_ukv, a_out_norm, b_q_norm, b_k_norm, b_out_norm, c_conv_w, c_conv_b, c_a_log, c_dt_bias, c_d_skip, c_out_norm, d_conv_w, d_a_log, d_dt_bias, d_out_norm, w_out, post_mix_norm, pre_ffn_norm, f_w_in, f_conv_w, f_conv_b, f_w_out, post_ffn_norm, loss_target, m_pre_mix_norm, m_w_in, m_a_q_norm, m_a_w_uq, m_a_kv_norm, m_a_w_ukv, m_a_out_norm, m_b_q_norm, m_b_k_norm, m_b_out_norm, m_c_conv_w, m_c_conv_b, m_c_a_log, m_c_dt_bias, m_c_d_skip, m_c_out_norm, m_d_conv_w, m_d_a_log, m_d_dt_bias, m_d_out_norm, m_w_out, m_post_mix_norm, m_pre_ffn_norm, m_f_w_in, m_f_conv_w, m_f_conv_b, m_f_w_out, m_post_ffn_norm, v_pre_mix_norm, v_w_in, v_a_q_norm, v_a_w_uq, v_a_kv_norm, v_a_w_ukv, v_a_out_norm, v_b_q_norm, v_b_k_norm, v_b_out_norm, v_c_conv_w, v_c_conv_b, v_c_a_log, v_c_dt_bias, v_c_d_skip, v_c_out_norm, v_d_conv_w, v_d_a_log, v_d_dt_bias, v_d_out_norm, v_w_out, v_post_mix_norm, v_pre_ffn_norm, v_f_w_in, v_f_conv_w, v_f_conv_b, v_f_w_out, v_post_ffn_norm):
    loc = locals()
    Wl = {n: loc[n] for n in WEIGHTS}
    Ml = {n: loc['m_' + n] for n in WEIGHTS}
    Vl = {n: loc['v_' + n] for n in WEIGHTS}
    L = x.shape[1]
    T = min(512, L)
    x0 = x.reshape(L, D_MODEL)
    tgt = loss_target.reshape(L, D_MODEL)

    first = ['w_in', 'a_w_uq', 'a_w_ukv', 'c_conv_w', 'd_conv_w']
    later = [n for n in SHARDED if n not in first]
    late_keys = ['w_out', 'f_w_in', 'f_conv', 'f_w_out']

    def shards(l, names):
        return [Wl[n][l].astype(BF16) if n in MXU_WEIGHTS else Wl[n][l] for n in names]

    def layer_weights(l, names, gathered):
        W = {n: Wl[n][l] for n in SMALL}
        for n in SHARDED:
            W[n] = jnp.zeros(layer_shape(n), BF16 if n in MXU_WEIGHTS else F32)
        for n, g in zip(names, gathered):
            W[n] = jnp.concatenate([g[j] for j in range(4)], axis=SHARD_AXIS[n] - 1)
        return W

    def chip_blocks(g, n):
        return jnp.stack(jnp.split(g, 4, axis=SHARD_AXIS[n] - 1))

    tabs = _tables(L)
    norm_in = [Wl['pre_mix_norm'][l].reshape(1, -1) for l in range(DEPTH)]
    def layer_shape(n):
        s = list(Wl[n].shape[1:])
        if n in SHARD_AXIS:
            s[SHARD_AXIS[n] - 1] *= 4
        return tuple(s)

    unbuild = jax.vjp(_build_layer, {n: jnp.zeros(layer_shape(n), F32) for n in WEIGHTS})[1]

    (h,) = _tw_fwd("l0_norm_in", _fn_norm_in, [(x0, _spec2(T, D_MODEL))], [norm_in[0]], [(D_MODEL, BF16)], L, T)
    gathered = _gather_copies(shards(0, first)).call("gather_l0")
    xs, saves, Ks = x0, [], []
    for l in range(DEPTH):
        last = l + 1 == DEPTH
        nxt = None if last else _gather_copies(shards(l + 1, SHARDED))
        if l == 0:
            def late(got):
                K_late = _build_layer(layer_weights(0, later, got))
                return {k: K_late[k] for k in late_keys}

            xs, h, sv = _layer_fwd(l, xs, h, _build_layer(layer_weights(0, first, gathered)), tabs, L, T,
                                   None if last else norm_in[l + 1], _gather_copies(shards(0, later)), late, nxt)
        else:
            xs, h, sv = _layer_fwd(l, xs, h, _build_layer(layer_weights(l, SHARDED, gathered)), tabs, L, T,
                                   None if last else norm_in[l + 1], None, None, nxt)
        gathered = sv['side']
        Ks.append(sv['K'])
        saves.append(sv)
    dy, loss_acc = _loss_call(xs, tgt, L)
    loss = lax.psum(loss_acc[0, 0], ("x", "y", "c"))

    ffn = ['f_w_in', 'f_conv_w', 'f_w_out', 'w_out']
    rest = [n for n in SHARDED if n not in ffn]

    def ffn_side(dK):
        only_w_out = {k: (dK[k] if k == 'w_out' else jnp.zeros(v.shape, F32)) for k, v in Ks[0].items()}
        g = {'f_w_in': dK['f_w_in'], 'f_conv_w': dK['f_conv'][0:3], 'f_w_out': dK['f_w_out'],
             'w_out': unbuild(only_w_out)[0]['w_out']}
        return _scatter_copies([chip_blocks(g[n], n) for n in ffn], None)

    def rest_blocks(dK):
        full = dict(dK)
        full.setdefault('pre_mix_norm', jnp.zeros((1, D_MODEL), F32))
        (g,) = unbuild(full)
        return [chip_blocks(g[n], n) for n in rest]

    def chip_sums(l, names, recvs):
        return [_sum_slots(f"sum_{n}_{l}", r.reshape(4, -1, r.shape[-1])) for n, r in zip(names, recvs)]

    grads = [None] * DEPTH
    pairs = {}
    dx, dhn = dy, None
    for l in reversed(range(DEPTH)):
        last = l + 1 == DEPTH

        def host_scatter(dK, got, up=None if last else grads[l + 1]):
            sets = [ffn_side(dK)] + ([] if up is None else [_scatter_copies(rest_blocks(up), None)])
            return _merge_copies(sets)

        def host_swap(dK, got, l=l, last=last):
            r = got['d_delta']
            parts = chip_sums(l, ffn, r[:len(ffn)]) + ([] if last else chip_sums(l + 1, rest, r[len(ffn):]))
            return _swap_copies(parts)

        dxa, dh, dK, dnext, got = _layer_bwd(l, dx, dhn, Ks[l], saves[l], tabs, L, T,
                                             None if last else norm_in[l + 1],
                                             {'d_delta': host_scatter, 'b_attn': host_swap})
        pairs.update({(l, n): p for n, p in zip(ffn, got['b_attn'])})
        if not last:
            pairs.update({(l + 1, n): p for n, p in zip(rest, got['b_attn'][len(ffn):])})
            grads[l + 1]['pre_mix_norm'] = dnext
        grads[l] = dK
        dx, dhn = dxa, dh
    (dx_in,), (grads[0]['pre_mix_norm'],) = _tw_bwd(
        "l0b_norm_in", _fn_norm_in, [(x0, _spec2(T, D_MODEL))], [norm_in[0]], [(dhn, _spec2(T, D_MODEL))], L, T,
        [True], [True], addto={0: (dx, _spec2(T, D_MODEL))})
    small_shapes = [Wl[n].shape for n in SMALL]
    gfull = [unbuild(grads[l])[0] for l in range(DEPTH)]
    gs = _pack([jnp.stack([gfull[l][n] for l in range(DEPTH)]) for n in SMALL], LANE, 8)
    *got0, recv_small = _scatter_copies([b.astype(BF16) for b in rest_blocks(grads[0])], gs).call("scatter_last")
    pairs.update({(0, n): p for n, p in zip(rest, _swap_copies(chip_sums(0, rest, got0)).call("swap_last"))})

    kinds = ['grad', 'delta', 'new_m', 'new_v']
    res = {}
    for n in SHARDED:
        upd = _adamw_call("adamw_" + n, [pairs[l, n] for l in range(DEPTH)], Wl[n], Ml[n], Vl[n])
        for kind, a in zip(kinds, upd):
            res[kind, n] = a
    small = _adamw_call("adamw_small", [recv_small], *[_pack([W_[n] for n in SMALL], LANE, 8)[None]
                                                       for W_ in (Wl, Ml, Vl)])
    for kind, s in zip(kinds, small):
        for n, a in zip(SMALL, _unpack(s, small_shapes)):
            res[kind, n] = a
    outs = [loss, dx_in.reshape(x.shape)]
    for kind in ['grad', 'delta', 'new_m', 'new_v']:
        outs += [res[kind, n] for n in WEIGHTS]
    return tuple(outs)
```

```python
import functools
import math

import numpy as np
import jax
import jax.numpy as jnp
from jax import lax
from jax.experimental import pallas as pl
from jax.experimental.pallas import tpu as pltpu

F32 = jnp.float32
BF16 = jnp.bfloat16
MESH = pl.DeviceIdType.MESH
VMEM_LIMIT = 48 * 1024 * 1024
LANE = 128

D_MODEL = 1024
DEPTH = 2
GRID_W = 64
ROPE_BASE = 10000.0
EPS = 1e-6
GROUP_W = 256
HEADS = 4
HD = 64
A_NOPE, A_ROPE, A_Q_LORA, A_KV_LORA = 64, 32, 192, 128
A_COLS = A_Q_LORA + A_KV_LORA + A_ROPE
B_COLS = 512
C_XBC = 512
C_COLS = GROUP_W + C_XBC + 8
D_QKV = 768
D_COLS = D_QKV + GROUP_W + 16
IN_COLS = A_COLS + B_COLS + C_COLS + D_COLS
C_CHUNK = 128
D_CHUNK = 64
C_PER_STEP = 1
D_PER_STEP = 2
D_FF = 2816
ADAM_LR, ADAM_B1, ADAM_B2, ADAM_EPS, ADAM_WD, ADAM_STEP = 0.001, 0.9, 0.999, 1e-08, 0.01, 10

WEIGHTS = ['pre_mix_norm', 'w_in', 'a_q_norm', 'a_w_uq', 'a_kv_norm', 'a_w_ukv', 'a_out_norm', 'b_q_norm',
           'b_k_norm', 'b_out_norm', 'c_conv_w', 'c_conv_b', 'c_a_log', 'c_dt_bias', 'c_d_skip', 'c_out_norm',
           'd_conv_w', 'd_a_log', 'd_dt_bias', 'd_out_norm', 'w_out', 'post_mix_norm', 'pre_ffn_norm', 'f_w_in',
           'f_conv_w', 'f_conv_b', 'f_w_out', 'post_ffn_norm']
SHARD_AXIS = {'w_in': 2, 'a_w_uq': 2, 'a_w_ukv': 2, 'c_conv_w': 2, 'd_conv_w': 2, 'w_out': 1, 'f_w_in': 2,
              'f_conv_w': 2, 'f_w_out': 1}
SHARDED = [n for n in WEIGHTS if n in SHARD_AXIS]
SMALL = [n for n in WEIGHTS if n not in SHARD_AXIS]
MXU_WEIGHTS = ('w_in', 'a_w_uq', 'a_w_ukv', 'w_out', 'f_w_in', 'f_w_out')

P_LAYOUT = [('b_q', 0, 512), ('c_xbc', 512, 512), ('a_cq', 1024, 256), ('b_k', 1280, 256), ('d_qkv', 1536, 768),
            ('b_v', 2304, 256), ('c_z', 2560, 256), ('d_z', 2816, 256), ('a_ckv', 3072, 128), ('a_kr', 3200, 128),
            ('c_dt', 3328, 128), ('d_b', 3456, 128), ('d_a', 3584, 128), ('pad', 3712, 128)]
P_OFF = {n: (o, w) for n, o, w in P_LAYOUT}
P_COLS = 3840
O_COLS = 1536


def _cparams(sem):
    return pltpu.CompilerParams(dimension_semantics=sem, vmem_limit_bytes=VMEM_LIMIT)


def _tile(n, target):
    best = None
    for d in range(LANE, min(n, target) + 1, LANE):
        if n % d == 0:
            best = d
    return best if best is not None else n


_NN = ((1,), (0,))
_NT = ((1,), (1,))
_TN = ((0,), (0,))


def _raw_dot(a, b, dims, hi):
    if hi:
        prec = lax.Precision.HIGH if hi == 'high' else lax.Precision.HIGHEST
        return lax.dot_general(a, b, (dims, ((), ())), precision=prec, preferred_element_type=F32)
    return lax.dot_general(a.astype(BF16), b.astype(BF16), (dims, ((), ())), preferred_element_type=F32)


def _make_dots(hi):
    @jax.custom_vjp
    def nn(a, b):
        return _raw_dot(a, b, _NN, hi)

    @jax.custom_vjp
    def nt(a, b):
        return _raw_dot(a, b, _NT, hi)

    @jax.custom_vjp
    def tn(a, b):
        return _raw_dot(a, b, _TN, hi)

    nn.defvjp(lambda a, b: (nn(a, b), (a, b)), lambda r, g: (nt(g, r[1]), tn(r[0], g)))
    nt.defvjp(lambda a, b: (nt(a, b), (a, b)), lambda r, g: (nn(g, r[1]), tn(g, r[0])))
    tn.defvjp(lambda a, b: (tn(a, b), (a, b)), lambda r, g: (nt(r[1], g), nn(r[0], g)))
    return nn, nt, tn


_nn, _nt, _tn = _make_dots(False)
_nn_hi, _nt_hi, _tn_hi = _make_dots(True)
_nn_h3, _nt_h3, _tn_h3 = _make_dots('high')


def _sigmoid(x):
    return 1.0 / (1.0 + jnp.exp(-x))


def _silu(x):
    return x * _sigmoid(x)


def _softplus(x):
    return jnp.maximum(x, 0.0) + jnp.log(1.0 + jnp.exp(-jnp.abs(x)))


def _rms(x, w, n=None):
    n = x.shape[-1] if n is None else n
    ms = jnp.sum(x * x, axis=-1, keepdims=True) * (1.0 / n)
    return x * lax.rsqrt(ms + EPS) * w


def _spec2(T, w, cb=0):
    return pl.BlockSpec((T, w), lambda i: (i, cb))


def _full_spec(a):
    nd = a.ndim
    return pl.BlockSpec(a.shape, lambda i: (0,) * nd)


def _tw_fwd(name, fn, acts, params, outs, L, T):
    na, npar = len(acts), len(params)

    def kern(*refs):
        a = [r[...].astype(F32) for r in refs[:na]]
        p = [r[...].astype(F32) for r in refs[na:na + npar]]
        res = fn(a, p)
        for r, o in zip(refs[na + npar:], res):
            r[...] = o.astype(r.dtype)

    return pl.pallas_call(
        kern, name=name, grid=(L // T,),
        in_specs=[s for _, s in acts] + [_full_spec(p) for p in params],
        out_specs=[_spec2(T, w) for w, _ in outs],
        out_shape=[jax.ShapeDtypeStruct((L, w), dt) for w, dt in outs],
        compiler_params=_cparams(("arbitrary",)),
    )(*[a for a, _ in acts], *params)


def _tw_bwd(name, fn, acts, params, douts, L, T, act_grad, par_grad, addto=None):
    na, npar, nd = len(acts), len(params), len(douts)
    addto = addto or {}
    add_keys = sorted(addto)
    ga = [k for k in range(na) if act_grad[k]]
    gp = [k for k in range(npar) if par_grad[k]]

    def kern(*refs):
        i = pl.program_id(0)
        a = [r[...].astype(F32) for r in refs[:na]]
        p = [r[...].astype(F32) for r in refs[na:na + npar]]
        g = [r[...].astype(F32) for r in refs[na + npar:na + npar + nd]]
        pos = na + npar + nd
        adds = [r[...].astype(F32) for r in refs[pos:pos + len(add_keys)]]
        pos += len(add_keys)
        da_refs = refs[pos:pos + len(ga)]
        dp_refs = refs[pos + len(ga):]

        def f(ad, pd):
            af, pf = list(a), list(p)
            for k, v in zip(ga, ad):
                af[k] = v
            for k, v in zip(gp, pd):
                pf[k] = v
            return fn(af, pf)

        _, vjp = jax.vjp(f, [a[k] for k in ga], [p[k] for k in gp])
        dad, dpd = vjp(list(g))
        for n, (r, d) in enumerate(zip(da_refs, dad)):
            if n in addto:
                d = d + adds[add_keys.index(n)]
            r[...] = d.astype(r.dtype)

        @pl.when(i == 0)
        def _():
            for r in dp_refs:
                r[...] = jnp.zeros(r.shape, F32)

        for r, d in zip(dp_refs, dpd):
            r[...] += d

    def width(spec):
        return spec.block_shape[-1]

    res = pl.pallas_call(
        kern, name=name, grid=(L // T,),
        in_specs=[s for _, s in acts] + [_full_spec(p) for p in params] + [s for _, s in douts]
        + [addto[k][1] for k in add_keys],
        out_specs=[_spec2(T, width(acts[k][1])) for k in ga] + [_full_spec(params[k]) for k in gp],
        out_shape=[jax.ShapeDtypeStruct((L, width(acts[k][1])), F32) for k in ga]
        + [jax.ShapeDtypeStruct(params[k].shape, F32) for k in gp],
        compiler_params=_cparams(("arbitrary",)),
    )(*[a for a, _ in acts], *params, *[a for a, _ in douts], *[addto[k][0] for k in add_keys])
    return list(res[:len(ga)]), list(res[len(ga):])


def _mm(name, a, b, mode, out_dtype, tm, tn, tk):
    halves_a = a.shape[-1] if (a.ndim == 3 and mode == 'nt') else None
    halves_b = b.shape[-1] if (b.ndim == 3 and mode == 'tn') else None
    if mode == 'nn':
        (M, K), N = a.shape, b.shape[1]
    elif mode == 'nt':
        M, K, N = a.shape[-2], (2 * halves_a if halves_a else a.shape[1]), b.shape[0]
    else:
        (K, M), N = a.shape, (2 * halves_b if halves_b else b.shape[1])
    tm = _tile(M, tm)
    tn = _tile(halves_b or N, tn)
    tk = _tile(halves_a or K, tk)
    nk = K // tk
    if mode == 'nn':
        a_spec = pl.BlockSpec((tm, tk), lambda i, j, k: (i, k))
        b_spec = pl.BlockSpec((tk, tn), lambda i, j, k: (k, j))
        dims = _NN
    elif mode == 'nt':
        a_spec = pl.BlockSpec((tm, tk), lambda i, j, k: (i, k))
        if halves_a:
            per = halves_a // tk
            a_spec = pl.BlockSpec((None, tm, tk), lambda i, j, k: (k // per, i, k % per))
        b_spec = pl.BlockSpec((tn, tk), lambda i, j, k: (j, k))
        dims = _NT
    else:
        a_spec = pl.BlockSpec((tk, tm), lambda i, j, k: (k, i))
        b_spec = pl.BlockSpec((tk, tn), lambda i, j, k: (k, j))
        if halves_b:
            per = halves_b // tn
            b_spec = pl.BlockSpec((None, tk, tn), lambda i, j, k: (j // per, k, j % per))
        dims = _TN

    def kern(a_ref, b_ref, o_ref, acc):
        k = pl.program_id(2)

        @pl.when(k == 0)
        def _():
            acc[...] = jnp.zeros(acc.shape, F32)

        acc[...] += lax.dot_general(a_ref[...].astype(BF16), b_ref[...].astype(BF16), (dims, ((), ())),
                                    preferred_element_type=F32)

        @pl.when(k == nk - 1)
        def _():
            o_ref[...] = acc[...].astype(o_ref.dtype)

    return pl.pallas_call(
        kern, name=name, grid=(M // tm, N // tn, nk),
        in_specs=[a_spec, b_spec],
        out_specs=pl.BlockSpec((tm, tn), lambda i, j, k: (i, j)),
        out_shape=jax.ShapeDtypeStruct((M, N), out_dtype),
        scratch_shapes=[pltpu.VMEM((tm, tn), F32)],
        compiler_params=_cparams(("arbitrary", "arbitrary", "arbitrary")),
    )(a, b)


def _host(kern, n_in, n_out, side, grid, n_scratch=0):
    if side is None:
        return kern, [], [], [], [], []
    ni, no = len(side.ins), len(side.out_shapes)

    def hosted(*refs):
        ins, s_in = refs[:n_in], refs[n_in:n_in + ni]
        pos = n_in + ni
        outs, s_out = refs[pos:pos + n_out], refs[pos + n_out:pos + n_out + no]
        pos += n_out + no
        own, sems = refs[pos:pos + n_scratch], refs[pos + n_scratch:]
        ids = [pl.program_id(d) for d in range(len(grid))]
        first = functools.reduce(lambda a, b: a & b, [i == 0 for i in ids])
        last = functools.reduce(lambda a, b: a & b, [i == g - 1 for i, g in zip(ids, grid)])

        @pl.when(first)
        def _():
            side.start(s_in, s_out, sems)

        kern(*ins, *outs, *own)

        @pl.when(last)
        def _():
            side.finish(s_in, s_out, sems)

    return hosted, [_ANY] * ni, [_ANY] * no, side.out_shapes, side.scratch(), side.ins


def _flash_fwd(name, q, k, v, H, rep, scale, L, side=None):
    tq = min(512, L)
    nq = L // tq
    KC = min(2048, L)
    nkc = L // KC
    log2e = 1.0 / math.log(2.0)

    def kern(q_ref, k_ref, v_ref, o_ref, lse_ref):
        qb = q_ref[...]
        m = jnp.full((tq, 1), -1e30, F32)
        l = jnp.zeros((tq, 1), F32)
        acc = jnp.zeros((tq, LANE), F32)
        for c in range(nkc):
            kb = k_ref[c * KC:(c + 1) * KC, :]
            vb = v_ref[c * KC:(c + 1) * KC, :]
            s = lax.dot_general(qb, kb, (_NT, ((), ())), preferred_element_type=F32) * (scale * log2e)
            mn = jnp.maximum(m, jnp.max(s, axis=-1, keepdims=True))
            al = jnp.exp2(m - mn)
            p = jnp.exp2(s - mn)
            l = al * l + jnp.sum(p, axis=-1, keepdims=True)
            acc = al * acc + lax.dot_general(p.astype(BF16), vb, (_NN, ((), ())), preferred_element_type=F32)
            m = mn
        o_ref[...] = acc / l
        lse_ref[...] = m * math.log(2.0) + jnp.log(l)

    kern, s_in, s_out, s_shapes, s_scratch, s_args = _host(kern, 3, 2, side, (H, nq))
    res = pl.pallas_call(
        kern, name=name, grid=(H, nq),
        in_specs=[pl.BlockSpec((tq, LANE), lambda h, i: (i, h)),
                  pl.BlockSpec((L, LANE), lambda h, i: (0, h // rep)),
                  pl.BlockSpec((L, LANE), lambda h, i: (0, h // rep))] + s_in,
        out_specs=[pl.BlockSpec((tq, LANE), lambda h, i: (i, h)),
                   pl.BlockSpec((tq, 1), lambda h, i: (h * nq + i, 0))] + s_out,
        out_shape=[jax.ShapeDtypeStruct((L, H * LANE), F32), jax.ShapeDtypeStruct((H * L, 1), F32)] + s_shapes,
        scratch_shapes=s_scratch,
        compiler_params=_cparams(("arbitrary", "arbitrary")),
    )(q, k, v, *s_args)
    return res[0], res[1], list(res[2:])


def _flash_bwd(name, q, k, v, o, lse, do, H, rep, scale, L, side=None):
    tq = min(512, L)
    nq = L // tq
    KC = min(1024, L)
    nkc = L // KC
    Hkv = H // rep

    def kern(q_ref, k_ref, v_ref, o_ref, lse_ref, do_ref, dq_ref, dk_ref, dv_ref):
        h = pl.program_id(0)
        i = pl.program_id(1)

        @pl.when((i == 0) & (h % rep == 0))
        def _():
            dk_ref[...] = jnp.zeros(dk_ref.shape, F32)
            dv_ref[...] = jnp.zeros(dv_ref.shape, F32)

        qb = q_ref[...]
        do = do_ref[...]
        dob = do.astype(BF16)
        delta = jnp.sum(do * o_ref[...], axis=-1, keepdims=True)
        lse = lse_ref[...]
        dq = jnp.zeros((tq, LANE), F32)
        for c in range(nkc):
            sl = slice(c * KC, (c + 1) * KC)
            kb = k_ref[sl, :]
            vb = v_ref[sl, :]
            s = lax.dot_general(qb, kb, (_NT, ((), ())), preferred_element_type=F32) * scale
            p = jnp.exp(s - lse)
            dp = lax.dot_general(dob, vb, (_NT, ((), ())), preferred_element_type=F32)
            ds = (p * (dp - delta) * scale).astype(BF16)
            dq = dq + lax.dot_general(ds, kb, (_NN, ((), ())), preferred_element_type=F32)
            dk_ref[sl, :] += lax.dot_general(ds, qb, (_TN, ((), ())), preferred_element_type=F32)
            dv_ref[sl, :] += lax.dot_general(p.astype(BF16), dob, (_TN, ((), ())), preferred_element_type=F32)
        dq_ref[...] = dq

    kern, s_in, s_out, s_shapes, s_scratch, s_args = _host(kern, 6, 3, side, (H, nq))
    res = pl.pallas_call(
        kern, name=name, grid=(H, nq),
        in_specs=[pl.BlockSpec((tq, LANE), lambda h, i: (i, h)),
                  pl.BlockSpec((L, LANE), lambda h, i: (0, h // rep)),
                  pl.BlockSpec((L, LANE), lambda h, i: (0, h // rep)),
                  pl.BlockSpec((tq, LANE), lambda h, i: (i, h)),
                  pl.BlockSpec((tq, 1), lambda h, i: (h * nq + i, 0)),
                  pl.BlockSpec((tq, LANE), lambda h, i: (i, h))] + s_in,
        out_specs=[pl.BlockSpec((tq, LANE), lambda h, i: (i, h)),
                   pl.BlockSpec((L, LANE), lambda h, i: (0, h // rep)),
                   pl.BlockSpec((L, LANE), lambda h, i: (0, h // rep))] + s_out,
        out_shape=[jax.ShapeDtypeStruct((L, H * LANE), F32), jax.ShapeDtypeStruct((L, Hkv * LANE), F32),
                   jax.ShapeDtypeStruct((L, Hkv * LANE), F32)] + s_shapes,
        scratch_shapes=s_scratch,
        compiler_params=_cparams(("arbitrary", "arbitrary")),
    )(q, k, v, o, lse, do, *s_args)
    return res[0], res[1], res[2], list(res[3:])


def _shift_dn(x, first_row):
    row = lax.broadcasted_iota(jnp.int32, x.shape, 0)
    return jnp.where(row == 0, first_row, pltpu.roll(x, 1, 0))


def _shift_up(x, last_row):
    n = x.shape[0]
    row = lax.broadcasted_iota(jnp.int32, x.shape, 0)
    return jnp.where(row == n - 1, last_row, pltpu.roll(x, n - 1, 0))


def _halo_specs(ndim, lead, T, tc, cb0, L):
    r8 = T // 8
    last8 = L // 8 - 1
    if ndim == 2:
        return [pl.BlockSpec((T, tc), lambda j, i: (i, cb0 + j)),
                pl.BlockSpec((8, tc), lambda j, i: (jnp.maximum(i * r8 - 1, 0), cb0 + j)),
                pl.BlockSpec((8, tc), lambda j, i: (jnp.minimum((i + 1) * r8, last8), cb0 + j))]
    return [pl.BlockSpec((None, T, tc), lambda j, i: (lead, i, cb0 + j)),
            pl.BlockSpec((None, 8, tc), lambda j, i: (lead, jnp.maximum(i * r8 - 1, 0), cb0 + j)),
            pl.BlockSpec((None, 8, tc), lambda j, i: (lead, jnp.minimum((i + 1) * r8, last8), cb0 + j))]


def _conv_rows(x_ref, xp_ref, xn_ref, w, first, last):
    x = x_ref[...]
    T = x.shape[0]
    w0, w1, w2, b = w[0:1], w[1:2], w[2:3], w[3:4]
    pr = jnp.where(first, 0.0, xp_ref[7:8, :])
    pr2 = jnp.where(first, 0.0, xp_ref[6:7, :])
    nr = jnp.where(last, 0.0, xn_ref[0:1, :])
    nr2 = jnp.where(last, 0.0, xn_ref[1:2, :])
    xm1 = _shift_dn(x, pr)
    xp1 = _shift_up(x, nr)
    pre = xm1 * w0 + x * w1 + xp1 * w2 + b
    pre_m1 = pr2 * w0 + pr * w1 + x[0:1] * w2 + b
    pre_T = x[T - 1:T] * w0 + nr * w1 + nr2 * w2 + b
    return x, xm1, xp1, pre, pre_m1, pre_T


def _conv_grads(dpre, dpre_m1, dpre_T, x, xm1, xp1, w):
    dx = _shift_up(dpre, dpre_T) * w[0:1] + dpre * w[1:2] + _shift_dn(dpre, dpre_m1) * w[2:3]
    row = lax.broadcasted_iota(jnp.int32, (8, x.shape[1]), 0)
    dw = (jnp.where(row == 0, jnp.sum(dpre * xm1, axis=0, keepdims=True), 0.0)
          + jnp.where(row == 1, jnp.sum(dpre * x, axis=0, keepdims=True), 0.0)
          + jnp.where(row == 2, jnp.sum(dpre * xp1, axis=0, keepdims=True), 0.0)
          + jnp.where(row == 3, jnp.sum(dpre, axis=0, keepdims=True), 0.0))
    return dx, dw


def _conv_fwd(name, x, col0, C, w8, act, L, tc):
    T = min(256, L)
    nt = L // T
    cb0 = col0 // tc

    def kern(x_ref, xp_ref, xn_ref, w_ref, o_ref):
        i = pl.program_id(1)
        x = x_ref[...]
        w = w_ref[...]
        pr = jnp.where(i == 0, 0.0, xp_ref[7:8, :])
        nr = jnp.where(i == nt - 1, 0.0, xn_ref[0:1, :])
        pre = _shift_dn(x, pr) * w[0:1] + x * w[1:2] + _shift_up(x, nr) * w[2:3] + w[3:4]
        o_ref[...] = _silu(pre) if act else pre

    return pl.pallas_call(
        kern, name=name, grid=(C // tc, nt),
        in_specs=_halo_specs(2, None, T, tc, cb0, L) + [pl.BlockSpec((8, tc), lambda j, i: (0, j))],
        out_specs=pl.BlockSpec((T, tc), lambda j, i: (i, j)),
        out_shape=jax.ShapeDtypeStruct((L, C), F32),
        compiler_params=_cparams(("arbitrary", "arbitrary")),
    )(x, x, x, w8)


def _conv_bwd(name, x, col0, C, w8, act, gs, L, tc):
    T = min(256, L)
    nt = L // T
    cb0 = col0 // tc
    ng = len(gs)

    def dact(pre, g):
        if not act:
            return g
        s = _sigmoid(pre)
        return g * (s * (1.0 + pre * (1.0 - s)))

    def kern(*refs):
        x_ref, xp_ref, xn_ref, w_ref = refs[:4]
        g_refs = refs[4:4 + 3 * ng]
        dx_ref, dw_ref = refs[4 + 3 * ng:]
        i = pl.program_id(1)
        first = i == 0
        last = i == nt - 1
        w = w_ref[...]
        g = g_refs[0][...]
        gp = g_refs[1][7:8, :]
        gn = g_refs[2][0:1, :]
        for n in range(1, ng):
            g = g + g_refs[3 * n][...]
            gp = gp + g_refs[3 * n + 1][7:8, :]
            gn = gn + g_refs[3 * n + 2][0:1, :]
        x, xm1, xp1, pre, pre_m1, pre_T = _conv_rows(x_ref, xp_ref, xn_ref, w, first, last)
        dpre_m1 = jnp.where(first, 0.0, dact(pre_m1, gp))
        dpre_T = jnp.where(last, 0.0, dact(pre_T, gn))
        dx_ref[...], dw = _conv_grads(dact(pre, g), dpre_m1, dpre_T, x, xm1, xp1, w)

        @pl.when(first)
        def _():
            dw_ref[...] = jnp.zeros((8, tc), F32)

        dw_ref[...] += dw

    g_specs, g_args = [], []
    for arr, lead in gs:
        g_specs += _halo_specs(arr.ndim, lead, T, tc, 0, L)
        g_args += [arr, arr, arr]
    return pl.pallas_call(
        kern, name=name, grid=(C // tc, nt),
        in_specs=_halo_specs(2, None, T, tc, cb0, L) + [pl.BlockSpec((8, tc), lambda j, i: (0, j))] + g_specs,
        out_specs=[pl.BlockSpec((T, tc), lambda j, i: (i, j)), pl.BlockSpec((8, tc), lambda j, i: (0, j))],
        out_shape=[jax.ShapeDtypeStruct((L, C), F32), jax.ShapeDtypeStruct((8, C), F32)],
        compiler_params=_cparams(("arbitrary", "arbitrary")),
    )(x, x, x, w8, *g_args)


FFN_TC = 1408


def _ffn_gate_fwd(name, u, w8, L):
    T = min(256, L)
    nt = L // T
    ncb = D_FF // FFN_TC

    def kern(xg, xgp, xgn, xu, xup, xun, wg_ref, wu_ref, o_ref):
        i = pl.program_id(1)
        pre_g = _conv_rows(xg, xgp, xgn, wg_ref[...], i == 0, i == nt - 1)[3]
        pre_u = _conv_rows(xu, xup, xun, wu_ref[...], i == 0, i == nt - 1)[3]
        o_ref[...] = (_silu(pre_g) * pre_u).astype(BF16)

    return pl.pallas_call(
        kern, name=name, grid=(ncb, nt),
        in_specs=_halo_specs(2, None, T, FFN_TC, 0, L) + _halo_specs(2, None, T, FFN_TC, ncb, L)
        + [pl.BlockSpec((8, FFN_TC), lambda j, i: (0, j)), pl.BlockSpec((8, FFN_TC), lambda j, i: (0, j + ncb))],
        out_specs=pl.BlockSpec((T, FFN_TC), lambda j, i: (i, j)),
        out_shape=jax.ShapeDtypeStruct((L, D_FF), BF16),
        compiler_params=_cparams(("arbitrary", "arbitrary")),
    )(u, u, u, u, u, u, w8, w8)


def _ffn_gate_bwd(name, u, w8, da, L):
    T = min(128, L)
    nt = L // T
    ncb = D_FF // FFN_TC

    def kern(xg, xgp, xgn, xu, xup, xun, wg_ref, wu_ref, d_ref, dp_ref, dn_ref, du_ref, dwg_ref, dwu_ref):
        i = pl.program_id(1)
        first = i == 0
        last = i == nt - 1
        wg = wg_ref[...]
        wu = wu_ref[...]
        g, gm1, gp1, pg, pg_m1, pg_T = _conv_rows(xg, xgp, xgn, wg, first, last)
        v, vm1, vp1, pu, pu_m1, pu_T = _conv_rows(xu, xup, xun, wu, first, last)

        def dpre(pg_, pu_, d):
            s = _sigmoid(pg_)
            return d * pu_ * (s * (1.0 + pg_ * (1.0 - s))), d * (pg_ * s)

        dg, dv = dpre(pg, pu, d_ref[...])
        dg_m1, dv_m1 = dpre(pg_m1, pu_m1, jnp.where(first, 0.0, dp_ref[7:8, :]))
        dg_T, dv_T = dpre(pg_T, pu_T, jnp.where(last, 0.0, dn_ref[0:1, :]))
        du_ref[0], dwg = _conv_grads(dg, dg_m1, dg_T, g, gm1, gp1, wg)
        du_ref[1], dwu = _conv_grads(dv, dv_m1, dv_T, v, vm1, vp1, wu)

        @pl.when(first)
        def _():
            dwg_ref[...] = jnp.zeros(dwg_ref.shape, F32)
            dwu_ref[...] = jnp.zeros(dwu_ref.shape, F32)

        dwg_ref[...] += dwg
        dwu_ref[...] += dwu

    wspec = pl.BlockSpec((8, FFN_TC), lambda j, i: (0, j))
    du, dwg, dwu = pl.pallas_call(
        kern, name=name, grid=(ncb, nt),
        in_specs=_halo_specs(2, None, T, FFN_TC, 0, L) + _halo_specs(2, None, T, FFN_TC, ncb, L)
        + [wspec, pl.BlockSpec((8, FFN_TC), lambda j, i: (0, j + ncb))] + _halo_specs(2, None, T, FFN_TC, 0, L),
        out_specs=[pl.BlockSpec((2, T, FFN_TC), lambda j, i: (0, i, j)), wspec, wspec],
        out_shape=[jax.ShapeDtypeStruct((2, L, D_FF), F32), jax.ShapeDtypeStruct((8, D_FF), F32),
                   jax.ShapeDtypeStruct((8, D_FF), F32)],
        compiler_params=_cparams(("arbitrary", "arbitrary")),
    )(u, u, u, u, u, u, w8, w8, da, da, da)
    return du, jnp.concatenate([dwg, dwu], axis=1)


def _masks(Q, rev):
    ri = lax.broadcasted_iota(jnp.int32, (Q, Q), 0)
    ci = lax.broadcasted_iota(jnp.int32, (Q, Q), 1)
    diff = (ri - ci) * (1 - 2 * rev)
    return diff >= 0, diff > 0


def _lane_pick(v, sel):
    return jnp.sum(v * sel, axis=-1, keepdims=True)


def _head_rows(v_all, Q, rev):
    r = lax.broadcasted_iota(jnp.int32, (HEADS * Q, LANE), 0)
    l = lax.broadcasted_iota(jnp.int32, (HEADS * Q, LANE), 1)
    pick = jnp.zeros((HEADS * Q, LANE), F32)
    for h in range(HEADS):
        pick = jnp.where((r >= h * Q) & (r < (h + 1) * Q) & (l == rev * 4 + h), 1.0, pick)
    return _nt_hi(pick, v_all)


def _ssd_chunk(S, x, B, C, dtraw, alog, dtb, rev, kept=None):
    Q = dtraw.shape[0]
    incl, _ = _masks(Q, rev)
    tri = incl.astype(F32)
    dt = _softplus(dtraw + dtb)
    a_all = dt * (-jnp.exp(alog))
    acum_all = _nn_hi(tri, a_all)
    total_all = jnp.sum(a_all, axis=0, keepdims=True)
    lane = lax.broadcasted_iota(jnp.int32, (1, LANE), 1)
    rows_all = _head_rows(acum_all, Q, rev)
    ys, Sn = [], []
    for h in range(HEADS):
        g = h // 2
        sel = (lane == rev * 4 + h).astype(F32)
        acum = _lane_pick(acum_all, sel)
        dth = _lane_pick(dt, sel)
        tot = _lane_pick(total_all, sel)
        seg = acum - rows_all[h * Q:(h + 1) * Q, :]
        decay = jnp.exp(jnp.where(incl, seg, -1e30))
        xdt = x[h] * dth
        Sh = S[HD * h:HD * (h + 1), :]
        scores = _nt(C[g], B[g]) * decay
        y_diag = _nn(scores, xdt)
        states = _tn(xdt, B[g] * jnp.exp(tot - acum))
        y_off = _nt(C[g], Sh) * jnp.exp(acum)
        ys.append(y_diag + y_off)
        Sn.append(Sh * jnp.exp(tot) + states)
    return ys, jnp.concatenate(Sn, axis=0), []


def _inv_unit_raw(Lm):
    N = Lm.shape[0]
    Q = D_CHUNK
    ri = lax.broadcasted_iota(jnp.int32, (N, N), 0)
    ci = lax.broadcasted_iota(jnp.int32, (N, N), 1)
    X = (ri == ci).astype(F32) - Lm
    P = _raw_dot(Lm, Lm, _NN, False)
    n = 2
    while n < Q:
        X = X + _raw_dot(X, P, _NN, False)
        n *= 2
        if n < Q:
            P = _raw_dot(P, P, _NN, False)
    return X


@jax.custom_vjp
def _inv_unit(Lm, T_saved):
    return _inv_unit_raw(Lm) if T_saved is None else T_saved


def _inv_unit_f(Lm, T_saved):
    T = _inv_unit_raw(Lm) if T_saved is None else T_saved
    return T, T


def _inv_unit_b(T, g):
    return -_raw_dot(_raw_dot(T, g, _TN, False), T, _NT, False), None


_inv_unit.defvjp(_inv_unit_f, _inv_unit_b)


def _delta_chunk(S, q, k, v, braw, araw, alog, dtb, rev, kept=None):
    Q = braw.shape[0]
    N = HEADS * Q
    tri = _masks(Q, rev)[0].astype(F32)
    ri = lax.broadcasted_iota(jnp.int32, (N, N), 0)
    ci = lax.broadcasted_iota(jnp.int32, (N, N), 1)
    sh = int(math.log2(Q))
    same = (ri >> sh) == (ci >> sh)
    diff = (ri - ci) * (1 - 2 * rev)
    incl = same & (diff >= 0)
    strict = same & (diff > 0)
    beta_all = _sigmoid(braw)
    g_all = -jnp.exp(alog) * _softplus(araw + dtb)
    G_all = _nn_hi(tri, g_all)
    Gtot_all = jnp.sum(g_all, axis=0, keepdims=True)
    r = lax.broadcasted_iota(jnp.int32, (N, LANE), 0)
    l = lax.broadcasted_iota(jnp.int32, (N, LANE), 1)
    selm = (l == rev * 4 + (r >> sh)).astype(F32)
    rows4 = lambda a: jnp.concatenate([a] * HEADS, axis=0)
    XG = rows4(G_all) * selm
    G = jnp.sum(XG, axis=-1, keepdims=True)
    bt = jnp.sum(rows4(beta_all) * selm, axis=-1, keepdims=True)
    Gtot = jnp.sum(Gtot_all * selm, axis=-1, keepdims=True)
    decay = jnp.exp(jnp.where(incl, G - _nt_h3(jnp.ones((N, LANE), F32), XG), -1e30))
    qs, ks, vs = (jnp.concatenate(t, axis=0) for t in (q, k, v))
    qn = qs * lax.rsqrt(jnp.sum(qs * qs, axis=-1, keepdims=True) + 1e-6)
    kn = ks * lax.rsqrt(jnp.sum(ks * ks, axis=-1, keepdims=True) + 1e-6)
    qc = qn * (HD ** -0.5)
    kb = kn * bt
    T = _inv_unit(jnp.where(strict, _nt(kb, kn) * decay, 0.0), None if kept is None else kept[0])
    eG = jnp.exp(G)
    u = _nn(T, vs * bt)
    w = _nn(T, kb * eG)
    qk = _nt(qc, kn) * decay
    spread = (lax.broadcasted_iota(jnp.int32, (HD, N), 0)
              == (lax.broadcasted_iota(jnp.int32, (HD, N), 1) & (HD - 1))).astype(F32)
    wide = lambda a: jnp.where(same, _nn(a, spread), 0.0)
    v_new = u - _nn(wide(w), S)
    o = _nn(wide(qc * eG), S) + _nn(qk, v_new)
    S_new = S * jnp.exp(Gtot) + _tn(wide(kn * jnp.exp(Gtot - G)), v_new)
    return [o[Q * h:Q * (h + 1), :] for h in range(HEADS)], S_new, [T]


def _seq_pieces(ref, r0, Q, splits):
    if splits is None:
        return ref[r0:r0 + Q, :]
    return [[ref[r0:r0 + Q, o + w * t:o + w * (t + 1)] for t in range(n)] for o, w, n in splits]


def _store_pieces(ref, r0, Q, splits, vals, extra=None):
    if splits is None:
        ref[r0:r0 + Q, :] = vals
        return
    for g, (o, w, n) in enumerate(splits):
        for t in range(n):
            v = vals[g][t]
            if extra is not None and g == 0:
                v = v + extra[r0:r0 + Q, o + w * t:o + w * (t + 1)]
            ref[r0:r0 + Q, o + w * t:o + w * (t + 1)] = v


def _flat(ins):
    out = []
    for v in ins:
        if isinstance(v, list):
            out.extend(v)
        else:
            out.append(v)
    return out


def _scan_fwd(name, chunk_fn, seqs, rows, Q, L, CH, kept_shapes=()):
    nc = L // Q
    nb = nc // CH
    ns, nr = len(seqs), len(rows)
    nk = 1 + len(kept_shapes)
    BQ = Q * CH

    def kern(*refs):
        s_refs = (refs[:ns], refs[ns:2 * ns])
        r_refs = refs[2 * ns:2 * ns + nr]
        pos = 2 * ns + nr
        y_refs = refs[pos:pos + 2]
        k_refs = (refs[pos + 2:pos + 2 + nk], refs[pos + 2 + nk:pos + 2 + 2 * nk])
        S_scr = refs[pos + 2 + 2 * nk]
        i = pl.program_id(0)

        @pl.when(i == 0)
        def _():
            S_scr[...] = jnp.zeros(S_scr.shape, F32)

        rws = [r[...] for r in r_refs]
        for d in (0, 1):
            S = S_scr[d]
            for cc in range(CH):
                c = cc if d == 0 else CH - 1 - cc
                k_refs[d][0][c] = S
                ins = [_seq_pieces(r, c * Q, Q, sp) for r, (_, _, _, sp) in zip(s_refs[d], seqs)]
                ys, S, kept = chunk_fn(S, *_flat(ins), *rws, d)
                for r, v in zip(k_refs[d][1:], kept):
                    r[c] = v
                for h in range(HEADS):
                    y_refs[d][c * Q:(c + 1) * Q, HD * h:HD * (h + 1)] = ys[h]
            S_scr[d] = S

    fwd_specs = [pl.BlockSpec((BQ, w), functools.partial(lambda i, cb: (i, cb), cb=cb)) for _, w, cb, _ in seqs]
    rev_specs = [pl.BlockSpec((BQ, w), functools.partial(lambda i, cb: (nb - 1 - i, cb), cb=cb))
                 for _, w, cb, _ in seqs]
    arrs = [a for a, _, _, _ in seqs]
    k_shapes = [(GROUP_W, HD)] + list(kept_shapes)
    res = pl.pallas_call(
        kern, name=name, grid=(nb,),
        in_specs=fwd_specs + rev_specs + [pl.BlockSpec((1, LANE), lambda i: (0, 0)) for _ in rows],
        out_specs=[pl.BlockSpec((BQ, GROUP_W), lambda i: (i, 0)),
                   pl.BlockSpec((BQ, GROUP_W), lambda i: (nb - 1 - i, 0))]
        + [pl.BlockSpec((CH,) + s, lambda i: (i, 0, 0)) for s in k_shapes]
        + [pl.BlockSpec((CH,) + s, lambda i: (nb - 1 - i, 0, 0)) for s in k_shapes],
        out_shape=[jax.ShapeDtypeStruct((L, GROUP_W), F32)] * 2
        + [jax.ShapeDtypeStruct((nc,) + s, F32) for s in k_shapes] * 2,
        scratch_shapes=[pltpu.VMEM((2, GROUP_W, HD), F32)],
        compiler_params=_cparams(("arbitrary",)),
    )(*arrs, *arrs, *rows)
    return res[0], res[1], list(res[2:2 + nk]), list(res[2 + nk:])


def _scan_bwd(name, chunk_fn, seqs, rows, ssaves, dy, extra, Q, L, CH, side=None):
    nc = L // Q
    nb = nc // CH
    BQ = Q * CH
    ns, nr = len(seqs), len(rows)
    nk = len(ssaves[0])
    has_extra = extra is not None

    def kern(*refs):
        s_refs = (refs[:ns], refs[ns:2 * ns])
        pos = 2 * ns
        r_refs = refs[pos:pos + nr]
        pos += nr
        k_refs = (refs[pos:pos + nk], refs[pos + nk:pos + 2 * nk])
        pos += 2 * nk
        dy_refs = refs[pos:pos + 2]
        pos += 2
        ex_ref = refs[pos] if has_extra else None
        pos += 1 if has_extra else 0
        ds_refs = (refs[pos:pos + ns], refs[pos + ns:pos + 2 * ns])
        pos += 2 * ns
        dr_refs = refs[pos:pos + nr]
        dS_scr = refs[pos + nr]
        i = pl.program_id(0)

        @pl.when(i == 0)
        def _():
            dS_scr[...] = jnp.zeros(dS_scr.shape, F32)
            for r in dr_refs:
                r[...] = jnp.zeros(r.shape, F32)

        rws = [r[...] for r in r_refs]
        dr_acc = [jnp.zeros((1, LANE), F32) for _ in rows]
        for d in (0, 1):
            dS = dS_scr[d]
            for cc in range(CH):
                c = CH - 1 - cc if d == 0 else cc
                S = k_refs[d][0][c]
                kept = [r[c] for r in k_refs[d][1:]]
                dys = [dy_refs[d][c * Q:(c + 1) * Q, HD * h:HD * (h + 1)] for h in range(HEADS)]
                ins = [_seq_pieces(r, c * Q, Q, sp) for r, (_, _, _, sp) in zip(s_refs[d], seqs)]
                _, vjp = jax.vjp(
                    functools.partial(
                        lambda S_, ins_, rws_, d_, kept_: chunk_fn(S_, *_flat(ins_), *rws_, d_, kept_)[:2],
                        d_=d, kept_=kept),
                    S, ins, rws)
                dS, dins, drws = vjp((dys, dS))
                for n_, (r, (_, _, _, sp)) in enumerate(zip(ds_refs[d], seqs)):
                    _store_pieces(r, c * Q, Q, sp, dins[n_],
                                  extra=ex_ref if (has_extra and d == 0 and n_ == 0) else None)
                dr_acc = [a + g for a, g in zip(dr_acc, drws)]
            dS_scr[d] = dS
        for r, g in zip(dr_refs, dr_acc):
            r[...] += g

    def blk(shape, rev, cb=0):
        nd = len(shape)
        if rev:
            return pl.BlockSpec(shape, lambda i: (i, cb) + (0,) * (nd - 2))
        return pl.BlockSpec(shape, lambda i: (nb - 1 - i, cb) + (0,) * (nd - 2))

    arrs = [a for a, _, _, _ in seqs]
    in_specs = [blk((BQ, w), False, cb) for _, w, cb, _ in seqs] + [blk((BQ, w), True, cb) for _, w, cb, _ in seqs]
    in_specs += [pl.BlockSpec((1, LANE), lambda i: (0, 0)) for _ in rows]
    in_specs += [blk((CH,) + a.shape[1:], False) for a in ssaves[0]]
    in_specs += [blk((CH,) + a.shape[1:], True) for a in ssaves[1]]
    in_specs += [blk((BQ, GROUP_W), False), blk((BQ, GROUP_W), True)]
    args = arrs + arrs + list(rows) + list(ssaves[0]) + list(ssaves[1]) + [dy, dy]
    if has_extra:
        in_specs.append(blk((BQ, GROUP_W), False))
        args.append(extra)
    kern, s_in, s_out, s_shapes, s_scratch, s_args = _host(kern, len(args), 2 * ns + nr, side, (nb,), 1)
    res = pl.pallas_call(
        kern, name=name, grid=(nb,),
        in_specs=in_specs + s_in,
        out_specs=[blk((BQ, w), False) for _, w, _, _ in seqs] + [blk((BQ, w), True) for _, w, _, _ in seqs]
        + [pl.BlockSpec((1, LANE), lambda i: (0, 0)) for _ in rows] + s_out,
        out_shape=[jax.ShapeDtypeStruct((L, w), F32) for _, w, _, _ in seqs] * 2
        + [jax.ShapeDtypeStruct((1, LANE), F32) for _ in rows] + s_shapes,
        scratch_shapes=[pltpu.VMEM((2, GROUP_W, HD), F32)] + s_scratch,
        compiler_params=_cparams(("arbitrary",)),
    )(*args, *s_args)
    return list(res[:ns]), list(res[ns:2 * ns]), list(res[2 * ns:2 * ns + nr]), list(res[2 * ns + nr:])


def _loss_call(y, tgt, L):
    T = min(256, L)

    def kern(y_ref, t_ref, dy_ref, l_ref):
        i = pl.program_id(0)
        e = y_ref[...] - t_ref[...]
        dy_ref[...] = e * (1.0 / D_MODEL)

        @pl.when(i == 0)
        def _():
            l_ref[...] = jnp.zeros(l_ref.shape, F32)

        part = 0.5 * jnp.sum(jnp.sum(e * e, axis=-1, keepdims=True) * (1.0 / D_MODEL), axis=0, keepdims=True)
        l_ref[...] += jnp.broadcast_to(part, l_ref.shape)

    return pl.pallas_call(
        kern, name="loss_head", grid=(L // T,),
        in_specs=[_spec2(T, D_MODEL), _spec2(T, D_MODEL)],
        out_specs=[_spec2(T, D_MODEL), pl.BlockSpec((8, LANE), lambda i: (0, 0))],
        out_shape=[jax.ShapeDtypeStruct((L, D_MODEL), F32), jax.ShapeDtypeStruct((8, LANE), F32)],
        compiler_params=_cparams(("arbitrary",)),
    )(y, tgt)


_ANY = pl.BlockSpec(memory_space=pl.ANY)


def _coords():
    return lax.axis_index("x"), lax.axis_index("y"), lax.axis_index("c")


class _Copies:
    def __init__(self, ins, out_shapes, copies_fn, n_remote, n_local):
        self.ins, self.out_shapes, self.copies_fn = list(ins), list(out_shapes), copies_fn
        self.n_remote, self.n_local = n_remote, n_local

    def scratch(self):
        return [pltpu.SemaphoreType.DMA((self.n_remote,)), pltpu.SemaphoreType.DMA((self.n_remote,)),
                pltpu.SemaphoreType.DMA((self.n_local,))]

    def _descr(self, in_refs, out_refs, sems):
        send_sems, recv_sems, lsems = sems
        remote, local = self.copies_fn(list(in_refs), list(out_refs))
        assert len(remote) == self.n_remote and len(local) == self.n_local
        mk = lambda k, src, dst, peer: pltpu.make_async_remote_copy(
            src_ref=src, dst_ref=dst, send_sem=send_sems.at[k], recv_sem=recv_sems.at[k], device_id=peer,
            device_id_type=MESH)
        sends = [mk(k, src, dst, peer) for k, (src, dst, _, peer) in enumerate(remote)]
        recvs = [mk(k, src, land, peer) for k, (src, _, land, peer) in enumerate(remote)]
        locs = [pltpu.make_async_copy(src, dst, lsems.at[k]) for k, (src, dst) in enumerate(local)]
        return sends, recvs, locs

    def start(self, in_refs, out_refs, sems):
        sends, _, locs = self._descr(in_refs, out_refs, sems)
        for c in locs + sends:
            c.start()

    def finish(self, in_refs, out_refs, sems):
        sends, recvs, locs = self._descr(in_refs, out_refs, sems)
        for c in recvs:
            c.wait_recv()
        for c in sends:
            c.wait_send()
        for c in locs:
            c.wait()

    def call(self, name):
        ni, no = len(self.ins), len(self.out_shapes)

        def body(*refs):
            self.start(refs[:ni], refs[ni:ni + no], refs[ni + no:])
            self.finish(refs[:ni], refs[ni:ni + no], refs[ni + no:])

        return pl.pallas_call(body, name=name, in_specs=[_ANY] * ni, out_specs=[_ANY] * no,
                              out_shape=self.out_shapes, scratch_shapes=self.scratch())(*self.ins)


def _chip_peers(x, y):
    return [(1 - x, y), (x, 1 - y), (1 - x, 1 - y)]


def _gather_copies(arrs):
    def copies_fn(ins, outs):
        x, y, c = _coords()
        me = 2 * x + y
        remote, local = [], []
        for src, out in zip(ins, outs):
            local.append((src, out.at[me]))
            for px, py in _chip_peers(x, y):
                remote.append((src, out.at[me], out.at[2 * px + py], (px, py, c)))
        return remote, local

    shapes = [jax.ShapeDtypeStruct((4,) + a.shape, a.dtype) for a in arrs]
    return _Copies(arrs, shapes, copies_fn, 3 * len(arrs), len(arrs))


def _scatter_copies(Gs, small):
    nb = len(Gs)

    def copies_fn(ins, outs):
        x, y, c = _coords()
        me = 2 * x + y
        remote, local = [], []
        for g, out in zip(ins[:nb], outs[:nb]):
            local.append((g.at[me], out.at[me]))
            for px, py in _chip_peers(x, y):
                remote.append((g.at[2 * px + py], out.at[me], out.at[2 * px + py], (px, py, c)))
        if small is not None:
            dev = 4 * x + 2 * y + c
            gs, outs_ = ins[nb], outs[nb]
            local.append((gs, outs_.at[dev]))
            for mask in range(1, 8):
                px, py, pc = x ^ (mask >> 2), y ^ ((mask >> 1) & 1), c ^ (mask & 1)
                remote.append((gs, outs_.at[dev], outs_.at[4 * px + 2 * py + pc], (px, py, pc)))
        return remote, local

    ins = list(Gs) + ([small] if small is not None else [])
    shapes = [jax.ShapeDtypeStruct(g.shape, g.dtype) for g in Gs]
    if small is not None:
        shapes.append(jax.ShapeDtypeStruct((8,) + small.shape, small.dtype))
    extra = 1 if small is not None else 0
    return _Copies(ins, shapes, copies_fn, 3 * nb + 7 * extra, nb + extra)


SWAP_STREAMS = 8


def _row_chunks(rows):
    k = SWAP_STREAMS
    if rows % (8 * k) == 0 and rows >= 64 * k:
        return [(q * (rows // k), rows // k) for q in range(k)]
    return [(0, rows)]


def _swap_copies(parts):
    chunks = [_row_chunks(p.shape[0]) for p in parts]
    n = sum(len(ch) for ch in chunks)

    def copies_fn(ins, outs):
        x, y, c = _coords()
        remote, local = [], []
        for src, out, ch in zip(ins, outs, chunks):
            for r0, nr in ch:
                rows = pl.ds(r0, nr)
                local.append((src.at[rows], out.at[c, rows]))
                remote.append((src.at[rows], out.at[c, rows], out.at[1 - c, rows], (x, y, 1 - c)))
        return remote, local

    shapes = [jax.ShapeDtypeStruct((2,) + p.shape, p.dtype) for p in parts]
    return _Copies(parts, shapes, copies_fn, n, n)


def _merge_copies(sets):
    ins = [a for s in sets for a in s.ins]
    shapes = [o for s in sets for o in s.out_shapes]

    def copies_fn(in_refs, out_refs):
        remote, local, pi, po = [], [], 0, 0
        for s in sets:
            r, l = s.copies_fn(in_refs[pi:pi + len(s.ins)], out_refs[po:po + len(s.out_shapes)])
            remote += r
            local += l
            pi += len(s.ins)
            po += len(s.out_shapes)
        return remote, local

    return _Copies(ins, shapes, copies_fn, sum(s.n_remote for s in sets), sum(s.n_local for s in sets))


def _row_tile(rows):
    best = rows
    for d in range(8, min(rows, 256) + 1, 8):
        if rows % d == 0:
            best = d
    return best


def _sum_slots(name, recv):
    n, R, W = recv.shape
    tr = _row_tile(R)

    def kern(r_ref, o_ref):
        acc = r_ref[0].astype(F32)
        for s in range(1, n):
            acc = acc + r_ref[s].astype(F32)
        o_ref[...] = acc

    return pl.pallas_call(
        kern, name=name, grid=(R // tr,),
        in_specs=[pl.BlockSpec((n, tr, W), lambda i: (0, i, 0))],
        out_specs=pl.BlockSpec((tr, W), lambda i: (i, 0)),
        out_shape=jax.ShapeDtypeStruct((R, W), F32),
        compiler_params=_cparams(("arbitrary",)),
    )(recv)


def _adamw_call(name, slots, w, m, v):
    nl = len(slots)
    n, R, W = slots[0].shape
    tr = _row_tile(R)
    nr = R // tr

    def kern(*refs):
        s_refs = refs[:nl]
        w_ref, m_ref, v_ref, g_ref, d_ref, nm_ref, nv_ref = refs[nl:]
        layer = pl.program_id(0)
        g = s_refs[0][0]
        for s in range(1, n):
            g = g + s_refs[0][s]
        for l in range(1, nl):
            gl = s_refs[l][0]
            for s in range(1, n):
                gl = gl + s_refs[l][s]
            g = jnp.where(layer == l, gl, g)
        m_ = ADAM_B1 * m_ref[...] + (1.0 - ADAM_B1) * g
        v_ = ADAM_B2 * v_ref[...] + (1.0 - ADAM_B2) * (g * g)
        m_hat = m_ / (1.0 - ADAM_B1 ** ADAM_STEP)
        v_hat = v_ / (1.0 - ADAM_B2 ** ADAM_STEP)
        g_ref[...] = g
        d_ref[...] = -ADAM_LR * (m_hat / (jnp.sqrt(v_hat) + ADAM_EPS) + ADAM_WD * w_ref[...])
        nm_ref[...] = m_
        nv_ref[...] = v_

    blk = pl.BlockSpec((None, tr, W), lambda l, i: (l, i, 0))
    return pl.pallas_call(
        kern, name=name, grid=(nl, nr),
        in_specs=[pl.BlockSpec((n, tr, W), lambda l, i: (0, i, 0)) for _ in slots] + [blk, blk, blk],
        out_specs=[blk, blk, blk, blk],
        out_shape=[jax.ShapeDtypeStruct((nl, R, W), F32)] * 4,
        compiler_params=_cparams(("arbitrary", "arbitrary")),
    )(*slots, w, m, v)


def _pack(arrs, width, row_mult):
    flat = jnp.concatenate([a.reshape(-1) for a in arrs])
    n = flat.shape[0]
    rows = -(-n // width)
    rows = -(-rows // row_mult) * row_mult
    return jnp.pad(flat, (0, rows * width - n)).reshape(rows, width)


def _unpack(buf, shapes):
    flat = buf.reshape(-1)
    out, pos = [], 0
    for s in shapes:
        n = int(np.prod(s))
        out.append(flat[pos:pos + n].reshape(s))
        pos += n
    return out


def _rope_angles(L, rot_dim):
    rows = L // GRID_W
    row = jnp.repeat(jnp.arange(rows), GRID_W).astype(F32)
    col = jnp.tile(jnp.arange(GRID_W), rows).astype(F32)
    sec = rot_dim // 2
    inv_freq = ROPE_BASE ** (-jnp.arange(0, sec, 2, dtype=F32) / sec)
    ang_r = row[:, None] * inv_freq
    ang_c = col[:, None] * inv_freq
    ang = jnp.concatenate([ang_r, ang_r, ang_c, ang_c], axis=-1)
    return jnp.cos(ang), jnp.sin(ang)


def _rot_matrix(r):
    R = np.zeros((r, r), np.float32)
    q = r // 4
    for s in range(2):
        for t in range(q):
            lo = s * (r // 2) + t
            hi = lo + q
            R[hi, lo] = -1.0
            R[lo, hi] = 1.0
    return R


def _place_tables(L, cos, sin, width, offsets):
    r = cos.shape[1]
    Rm = np.zeros((width, width), np.float32)
    R = _rot_matrix(r)
    cs, ss, pos = [], [], 0
    for o in list(offsets) + [width]:
        if o > pos:
            cs.append(jnp.ones((L, o - pos), F32))
            ss.append(jnp.zeros((L, o - pos), F32))
        if o < width:
            cs.append(cos)
            ss.append(sin)
            Rm[o:o + r, o:o + r] = R
        pos = o + r
    return jnp.concatenate(cs, axis=1), jnp.concatenate(ss, axis=1), jnp.asarray(Rm)


def _head_mean_matrix(width, stride, n):
    M = np.zeros((width, width), np.float32)
    for o in range(0, width, stride):
        M[o:o + n, o:o + n] = 1.0 / n
    return jnp.asarray(M)


def _pad_heads(w, n_heads, real, padded, axis):
    parts = jnp.split(w, n_heads, axis=axis)
    padw = [(0, 0)] * w.ndim
    padw[axis] = (0, padded - real)
    return jnp.concatenate([jnp.pad(p, padw) for p in parts], axis=axis)


def _row128(v):
    v = v.reshape(1, -1)
    return jnp.pad(v, ((0, 0), (0, LANE - v.shape[1])))


def _conv_w8(w, b):
    C = w.shape[1]
    rows = [w, jnp.zeros((1, C), F32) if b is None else b.reshape(1, C), jnp.zeros((4, C), F32)]
    return jnp.concatenate(rows, axis=0)


def _build_layer(W):
    w_in = W['w_in']
    o = 0
    cols = {}
    for name, n in [('a_cq', A_Q_LORA), ('a_ckv', A_KV_LORA), ('a_kr', A_ROPE), ('b_q', 256), ('b_k', 128),
                    ('b_v', 128), ('c_z', 256), ('c_xbc', 512), ('c_dt', 8), ('d_qkv', 768), ('d_z', 256),
                    ('d_b', 8), ('d_a', 8)]:
        cols[name] = w_in[:, o:o + n]
        o += n
    padc = lambda a, lo, width: jnp.pad(a, ((0, 0), (lo, width - lo - a.shape[1])))
    pieces = {
        'b_q': _pad_heads(cols['b_q'], 4, HD, LANE, 1), 'c_xbc': cols['c_xbc'], 'a_cq': padc(cols['a_cq'], 0, 256),
        'b_k': _pad_heads(cols['b_k'], 2, HD, LANE, 1), 'd_qkv': cols['d_qkv'],
        'b_v': _pad_heads(cols['b_v'], 2, HD, LANE, 1), 'c_z': cols['c_z'], 'd_z': cols['d_z'],
        'a_ckv': cols['a_ckv'], 'a_kr': padc(cols['a_kr'], A_NOPE, LANE), 'c_dt': padc(cols['c_dt'], 0, LANE),
        'd_b': padc(cols['d_b'], 0, LANE), 'd_a': padc(cols['d_a'], 0, LANE),
        'pad': jnp.zeros((D_MODEL, LANE), w_in.dtype)}
    out = {'w_in': jnp.concatenate([pieces[n] for n, _, _ in P_LAYOUT], axis=1)}
    out['a_q_norm'] = padc(W['a_q_norm'].reshape(1, -1), 0, 256)
    wuq = jnp.pad(W['a_w_uq'], ((0, 256 - A_Q_LORA), (0, 0)))
    out['a_w_uq'] = _pad_heads(wuq, 4, A_NOPE + A_ROPE, LANE, 1)
    out['a_kv_norm'] = W['a_kv_norm'].reshape(1, -1)
    ukv = W['a_w_ukv'].reshape(A_KV_LORA, HEADS, 2, HD)
    out['a_w_uk'] = _pad_heads(ukv[:, :, 0, :].reshape(A_KV_LORA, 256), 4, HD, LANE, 1)
    out['a_w_uv'] = _pad_heads(ukv[:, :, 1, :].reshape(A_KV_LORA, 256), 4, HD, LANE, 1)
    out['a_out_norm'] = _pad_heads(W['a_out_norm'].reshape(1, -1), 4, HD, LANE, 1)
    out['b_q_norm'] = _pad_heads(jnp.tile(W['b_q_norm'].reshape(1, -1), (1, 4)), 4, HD, LANE, 1)
    out['b_k_norm'] = _pad_heads(jnp.tile(W['b_k_norm'].reshape(1, -1), (1, 2)), 2, HD, LANE, 1)
    out['b_out_norm'] = _pad_heads(W['b_out_norm'].reshape(1, -1), 4, HD, LANE, 1)
    out['c_conv'] = _conv_w8(W['c_conv_w'], W['c_conv_b'])
    out['c_a_log'] = _row128(W['c_a_log'])
    out['c_dt_bias'] = _row128(W['c_dt_bias'])
    out['c_d_skip'] = jnp.repeat(W['c_d_skip'], HD).reshape(1, -1)
    out['c_out_norm'] = W['c_out_norm'].reshape(1, -1)
    out['d_conv'] = _conv_w8(W['d_conv_w'], None)
    out['d_a_log'] = _row128(W['d_a_log'])
    out['d_dt_bias'] = _row128(W['d_dt_bias'])
    out['d_out_norm'] = jnp.tile(W['d_out_norm'].reshape(1, -1), (1, 4))
    wo = W['w_out']
    out['w_out'] = jnp.concatenate([_pad_heads(wo[0:256], 4, HD, LANE, 0), _pad_heads(wo[256:512], 4, HD, LANE, 0),
                                    wo[512:1024]], axis=0)
    for n in ['pre_mix_norm', 'post_mix_norm', 'pre_ffn_norm', 'post_ffn_norm']:
        out[n] = W[n].reshape(1, -1)
    out['f_w_in'] = W['f_w_in']
    out['f_conv'] = _conv_w8(W['f_conv_w'], W['f_conv_b'])
    out['f_w_out'] = W['f_w_out']
    return out


def _fn_norm_in(a, p):
    return [_rms(a[0], p[0])]


def _fn_resid_norm2(a, p):
    x1 = a[0] + _rms(a[1], p[0])
    return [x1, _rms(x1, p[1])]


def _fn_resid_norm(a, p):
    return [a[0] + _rms(a[1], p[0])]


def _fn_a_prep(a, p):
    cq, ckv, kr, cosk, sink = a
    q_norm, w_uq, kv_norm, w_uk, w_uv, rq, rk = p
    cosq = jnp.concatenate([cosk] * HEADS, axis=1)
    sinq = jnp.concatenate([sink] * HEADS, axis=1)
    q = _nn(_rms(cq, q_norm, A_Q_LORA), w_uq)
    q = q * cosq + _nn_h3(q, rq) * sinq
    kvn = _rms(ckv, kv_norm)
    kr_r = kr * cosk + _nn_h3(kr, rk) * sink
    kk = _nn(kvn, w_uk) + jnp.concatenate([kr_r] * HEADS, axis=1)
    return [q, kk, _nn(kvn, w_uv)]


def _fn_b_prep(a, p):
    q, k, v, cos1, sin1 = a
    q_norm, k_norm, mq, mk, rq, rk = p
    cosq, sinq = (jnp.concatenate([t] * 4, axis=1) for t in (cos1, sin1))
    cosk, sink = (jnp.concatenate([t] * 2, axis=1) for t in (cos1, sin1))
    qn = q * lax.rsqrt(_nn_h3(q * q, mq) + EPS) * q_norm
    kn = k * lax.rsqrt(_nn_h3(k * k, mk) + EPS) * k_norm
    return [qn * cosq + _nn_h3(qn, rq) * sinq, kn * cosk + _nn_h3(kn, rk) * sink, v]


def _fn_mixer_post(a, p):
    oa, ob, yc0, yc1, xs, zc, od0, od1, zd = a
    a_norm, b_norm, dskip, c_norm, d_norm, m64 = p
    oc = _rms((yc0 + yc1 + xs * dskip) * _silu(zc), c_norm)
    od = od0 + od1
    odn = od * lax.rsqrt(_nn_h3(od * od, m64) + EPS) * d_norm * _silu(zd)
    return [jnp.concatenate([_rms(oa, a_norm, GROUP_W), _rms(ob, b_norm, GROUP_W), oc, odn], axis=1)]


def _fn_assemble(a, p):
    (dbq, dxbc, dcq, dbk, dqkv, dbv, dzc, dzd, dckv, dkr, ddt0, ddt1, db0, db1, da0, da1) = a
    return [jnp.concatenate([dbq, dxbc, dcq, dbk, dqkv, dbv, dzc, dzd, dckv, dkr, ddt0 + ddt1, db0 + db1,
                             da0 + da1, jnp.zeros_like(dckv)], axis=1)]


def _pspec(T, name):
    off, w = P_OFF[name]
    return _spec2(T, w, off // w)


def _layer_fwd(l, x, h, K, tabs, L, T, next_norm, side_a=None, late=None, side_b=None):
    n = f"l{l}_"
    sv = {'x': x, 'h': h}
    p = _mm(n + "in_proj", h, K['w_in'].astype(BF16), 'nn', F32, 1024, 1280, 1024)
    sv['p'] = p
    a_acts = [(p, _pspec(T, 'a_cq')), (p, _pspec(T, 'a_ckv')), (p, _pspec(T, 'a_kr')),
              (tabs['a_c'], _spec2(T, LANE)), (tabs['a_s'], _spec2(T, LANE))]
    a_pars = [K['a_q_norm'], K['a_w_uq'], K['a_kv_norm'], K['a_w_uk'], K['a_w_uv'], tabs['a_rq'], tabs['a_rk']]
    qa, ka, va = _tw_fwd(n + "a_prep", _fn_a_prep, a_acts, a_pars, [(512, BF16)] * 3, L, T)
    oa, lse_a, got_a = _flash_fwd(n + "a_attn", qa, ka, va, HEADS, 1, (A_NOPE + A_ROPE) ** -0.5, L, side_a)
    if late is not None:
        K = {**K, **late(got_a)}
    sv.update(a_acts=a_acts, a_pars=a_pars, qa=qa, ka=ka, va=va, oa=oa, lse_a=lse_a, K=K)
    b_acts = [(p, _pspec(T, 'b_q')), (p, _pspec(T, 'b_k')), (p, _pspec(T, 'b_v')),
              (tabs['b_c'], _spec2(T, LANE)), (tabs['b_s'], _spec2(T, LANE))]
    b_pars = [K['b_q_norm'], K['b_k_norm'], tabs['b_mq'], tabs['b_mk'], tabs['b_rq'], tabs['b_rk']]
    qb, kb, vb = _tw_fwd(n + "b_prep", _fn_b_prep, b_acts, b_pars, [(512, BF16), (256, BF16), (256, BF16)], L, T)
    ob, lse_b, sv['side'] = _flash_fwd(n + "b_attn", qb, kb, vb, HEADS, 2, HD ** -0.5, L, side_b)
    sv.update(b_acts=b_acts, b_pars=b_pars, qb=qb, kb=kb, vb=vb, ob=ob, lse_b=lse_b)
    xbc = _conv_fwd(n + "c_conv", p, P_OFF['c_xbc'][0], C_XBC, K['c_conv'], True, L, 512)
    c_seqs = [(xbc, C_XBC, 0, [(0, HD, 4), (256, HD, 2), (384, HD, 2)]),
              (p, LANE, P_OFF['c_dt'][0] // LANE, None)]
    c_rows = [K['c_a_log'], K['c_dt_bias']]
    yc0, yc1, sc0, sc1 = _scan_fwd(n + "c_ssd", _ssd_chunk, c_seqs, c_rows, C_CHUNK, L, C_PER_STEP)
    sv.update(xbc=xbc, c_seqs=c_seqs, c_rows=c_rows, sc=(sc0, sc1))
    qkv = _conv_fwd(n + "d_conv", p, P_OFF['d_qkv'][0], D_QKV, K['d_conv'], True, L, 768)
    d_seqs = [(qkv, D_QKV, 0, [(0, HD, 4), (256, HD, 4), (512, HD, 4)]),
              (p, LANE, P_OFF['d_b'][0] // LANE, None), (p, LANE, P_OFF['d_a'][0] // LANE, None)]
    d_rows = [K['d_a_log'], K['d_dt_bias']]
    od0, od1, sd0, sd1 = _scan_fwd(n + "d_delta", _delta_chunk, d_seqs, d_rows, D_CHUNK, L, D_PER_STEP,
                                   [(HEADS * D_CHUNK, HEADS * D_CHUNK)])
    sv.update(qkv=qkv, d_seqs=d_seqs, d_rows=d_rows, sd=(sd0, sd1))
    m_acts = [(oa, _spec2(T, 512)), (ob, _spec2(T, 512)), (yc0, _spec2(T, 256)), (yc1, _spec2(T, 256)),
              (xbc, _spec2(T, 256, 0)), (p, _pspec(T, 'c_z')), (od0, _spec2(T, 256)), (od1, _spec2(T, 256)),
              (p, _pspec(T, 'd_z'))]
    m_pars = [K['a_out_norm'], K['b_out_norm'], K['c_d_skip'], K['c_out_norm'], K['d_out_norm'], tabs['m64']]
    (o,) = _tw_fwd(n + "mixer_post", _fn_mixer_post, m_acts, m_pars, [(O_COLS, BF16)], L, T)
    f1 = _mm(n + "out_proj", o, K['w_out'].astype(BF16), 'nn', F32, 1024, 1024, 1536)
    r1_pars = [K['post_mix_norm'], K['pre_ffn_norm']]
    x1, h2 = _tw_fwd(n + "resid_mix", _fn_resid_norm2, [(x, _spec2(T, D_MODEL)), (f1, _spec2(T, D_MODEL))], r1_pars,
                     [(D_MODEL, F32), (D_MODEL, BF16)], L, T)
    sv.update(m_acts=m_acts, m_pars=m_pars, o=o, f1=f1, r1_pars=r1_pars, x1=x1, h2=h2)
    u = _mm(n + "ffn_in", h2, K['f_w_in'].astype(BF16), 'nn', F32, 1024, 1408, 1024)
    act = _ffn_gate_fwd(n + "ffn_gate", u, K['f_conv'], L)
    f2 = _mm(n + "ffn_out", act, K['f_w_out'].astype(BF16), 'nn', F32, 1024, 1024, 1408)
    sv.update(u=u, act=act, f2=f2)
    xf = [(x1, _spec2(T, D_MODEL)), (f2, _spec2(T, D_MODEL))]
    if next_norm is None:
        (x2,) = _tw_fwd(n + "resid_ffn", _fn_resid_norm, xf, [K['post_ffn_norm']], [(D_MODEL, F32)], L, T)
        hn = None
    else:
        x2, hn = _tw_fwd(n + "resid_ffn", _fn_resid_norm2, xf, [K['post_ffn_norm'], next_norm],
                         [(D_MODEL, F32), (D_MODEL, BF16)], L, T)
    return x2, hn, sv


def _layer_bwd(l, dx2, dhn, K, sv, tabs, L, T, next_norm, hosts=None):
    n = f"l{l}b_"
    dK = {}
    hosts = hosts or {}
    got = {}
    side = lambda name: hosts[name](dK, got) if name in hosts else None
    s2 = lambda w, cb=0: _spec2(T, w, cb)
    xf = [(sv['x1'], s2(D_MODEL)), (sv['f2'], s2(D_MODEL))]
    if next_norm is None:
        (dx1a, df2), (dK['post_ffn_norm'],) = _tw_bwd(n + "resid_ffn", _fn_resid_norm, xf, [K['post_ffn_norm']],
                                                      [(dx2, s2(D_MODEL))], L, T, [True, True], [True])
        dnext = None
    else:
        (dx1a, df2), (dK['post_ffn_norm'], dnext) = _tw_bwd(
            n + "resid_ffn", _fn_resid_norm2, xf, [K['post_ffn_norm'], next_norm],
            [(dx2, s2(D_MODEL)), (dhn, s2(D_MODEL))], L, T, [True, True], [True, True])
    dact = _mm(n + "ffn_out_dx", df2, K['f_w_out'].astype(BF16), 'nt', F32, 1024, 1408, 1024)
    dK['f_w_out'] = _mm(n + "ffn_out_dw", sv['act'], df2, 'tn', F32, 1408, 1024, 1024)
    du, dK['f_conv'] = _ffn_gate_bwd(n + "ffn_gate", sv['u'], K['f_conv'], dact, L)
    dh2 = _mm(n + "ffn_in_dx", du, K['f_w_in'].astype(BF16), 'nt', F32, 1024, 1024, 1408)
    dK['f_w_in'] = _mm(n + "ffn_in_dw", sv['h2'], du, 'tn', F32, 1024, 1408, 1024)
    (dxa, df1), (dK['post_mix_norm'], dK['pre_ffn_norm']) = _tw_bwd(
        n + "resid_mix", _fn_resid_norm2, [(sv['x'], s2(D_MODEL)), (sv['f1'], s2(D_MODEL))], sv['r1_pars'],
        [(dx1a, s2(D_MODEL)), (dh2, s2(D_MODEL))], L, T, [True, True], [True, True])
    do = _mm(n + "out_proj_dx", df1, K['w_out'].astype(BF16), 'nt', F32, 1024, 1536, 1024)
    dK['w_out'] = _mm(n + "out_proj_dw", sv['o'], df1, 'tn', F32, 1536, 1024, 1024)
    (doa, dob, dyc0, _, dxs_skip, dzc, dod0, _, dzd), mp = _tw_bwd(
        n + "mixer_post", _fn_mixer_post, sv['m_acts'], sv['m_pars'], [(do, s2(O_COLS))], L, T,
        [True] * 9, [True] * 5 + [False])
    dK['a_out_norm'], dK['b_out_norm'], dK['c_d_skip'], dK['c_out_norm'], dK['d_out_norm'] = mp
    (dqkv0, db0, da0), (dqkv1, db1, da1), (dK['d_a_log'], dK['d_dt_bias']), got['d_delta'] = _scan_bwd(
        n + "d_delta", _delta_chunk, sv['d_seqs'], sv['d_rows'], sv['sd'], dod0, None, D_CHUNK, L, D_PER_STEP,
        side('d_delta'))
    dqkv, dK['d_conv'] = _conv_bwd(n + "d_conv", sv['p'], P_OFF['d_qkv'][0], D_QKV, K['d_conv'], True,
                                   [(dqkv0, None), (dqkv1, None)], L, 768)
    (dxbc0, ddt0), (dxbc1, ddt1), (dK['c_a_log'], dK['c_dt_bias']), _ = _scan_bwd(
        n + "c_ssd", _ssd_chunk, sv['c_seqs'], sv['c_rows'], sv['sc'], dyc0, dxs_skip, C_CHUNK, L, C_PER_STEP)
    dxbc, dK['c_conv'] = _conv_bwd(n + "c_conv", sv['p'], P_OFF['c_xbc'][0], C_XBC, K['c_conv'], True,
                                   [(dxbc0, None), (dxbc1, None)], L, 512)
    dqb, dkb, dvb, got['b_attn'] = _flash_bwd(n + "b_attn", sv['qb'], sv['kb'], sv['vb'], sv['ob'], sv['lse_b'],
                                              dob, HEADS, 2, HD ** -0.5, L, side('b_attn'))
    (dbq, dbk, dbv), (dK['b_q_norm'], dK['b_k_norm']) = _tw_bwd(
        n + "b_prep", _fn_b_prep, sv['b_acts'], sv['b_pars'], [(dqb, s2(512)), (dkb, s2(256)), (dvb, s2(256))],
        L, T, [True] * 3 + [False] * 2, [True, True] + [False] * 4)
    dqa, dka, dva, got['a_attn'] = _flash_bwd(n + "a_attn", sv['qa'], sv['ka'], sv['va'], sv['oa'], sv['lse_a'],
                                              doa, HEADS, 1, (A_NOPE + A_ROPE) ** -0.5, L, side('a_attn'))
    (dcq, dckv, dkr), ap = _tw_bwd(
        n + "a_prep", _fn_a_prep, sv['a_acts'], sv['a_pars'], [(dqa, s2(512)), (dka, s2(512)), (dva, s2(512))],
        L, T, [True] * 3 + [False] * 2, [True] * 5 + [False] * 2)
    dK['a_q_norm'], dK['a_w_uq'], dK['a_kv_norm'], dK['a_w_uk'], dK['a_w_uv'] = ap
    pieces = [(dbq, s2(512)), (dxbc, s2(512)), (dcq, s2(256)), (dbk, s2(256)), (dqkv, s2(768)), (dbv, s2(256)),
              (dzc, s2(256)), (dzd, s2(256)), (dckv, s2(LANE)), (dkr, s2(LANE)),
              (ddt0, s2(LANE)), (ddt1, s2(LANE)), (db0, s2(LANE)), (db1, s2(LANE)), (da0, s2(LANE)),
              (da1, s2(LANE))]
    (dp,) = _tw_fwd(n + "assemble_dp", _fn_assemble, pieces, [], [(P_COLS, BF16)], L, T)
    dh = _mm(n + "in_proj_dx", dp, K['w_in'].astype(BF16), 'nt', F32, 1024, 1024, 1280)
    dK['w_in'] = _mm(n + "in_proj_dw", sv['h'], dp, 'tn', F32, 1024, 1280, 1024)
    return dxa, dh, dK, dnext, got


def _tables(L):
    ca, sa = _rope_angles(L, A_ROPE)
    cb, sb = _rope_angles(L, HD)
    t = {}
    t['a_c'], t['a_s'], t['a_rk'] = _place_tables(L, ca, sa, LANE, [A_NOPE])
    t['b_c'], t['b_s'], _ = _place_tables(L, cb, sb, LANE, [0])
    t['a_rq'] = _place_tables(8, ca[:8], sa[:8], 512, [LANE * h + A_NOPE for h in range(4)])[2]
    t['b_rq'] = _place_tables(8, cb[:8], sb[:8], 512, [LANE * h for h in range(4)])[2]
    t['b_rk'] = _place_tables(8, cb[:8], sb[:8], 256, [LANE * h for h in range(2)])[2]
    t['b_mq'] = _head_mean_matrix(512, LANE, HD)
    t['b_mk'] = _head_mean_matrix(256, LANE, HD)
    t['m64'] = _head_mean_matrix(256, HD, HD)
    return t


def kernel(x, pre_mix_norm, w_in, a_q_norm, a_w_uq, a_kv_norm, a_w_ukv, a_out_norm, b_q_norm, b_k_norm, b_out_norm, c_conv_w, c_conv_b, c_a_log, c_dt_bias, c_d_skip, c_out_norm, d_conv_w, d_a_log, d_dt_bias, d_out_norm, w_out, post_mix_norm, pre_ffn_norm, f_w_in, f_conv_w, f_conv_b, f_w_out, post_ffn_norm, loss_target, m_pre_mix_norm, m_w_in, m_a_q_norm, m_a_w_uq, m_a_kv_norm, m_a_w_ukv, m_a_out_norm, m_b_q_norm, m_b_k_norm, m_b_out_norm, m_c_conv_w, m_c_conv_b, m_c_a_log, m_c_dt_bias, m_c_d_skip, m_c_out_norm, m_d_conv_w, m_d_a_log, m_d_dt_bias, m_d_out_norm, m_w_out, m_post_mix_norm, m_pre_ffn_norm, m_f_w_in, m_f_conv_w, m_f_conv_b, m_f_w_out, m_post_ffn_norm, v_pre_mix_norm, v_w_in, v_a_q_norm, v_a_w_uq, v_a_kv_norm, v_a_w_ukv, v_a_out_norm, v_b_q_norm, v_b_k_norm, v_b_out_norm, v_c_conv_w, v_c_conv_b, v_c_a_log, v_c_dt_bias, v_c_d_skip, v_c_out_norm, v_d_conv_w, v_d_a_log, v_d_dt_bias, v_d_out_norm, v_w_out, v_post_mix_norm, v_pre_ffn_norm, v_f_w_in, v_f_conv_w, v_f_conv_b, v_f_w_out, v_post_ffn_norm):
    loc = locals()
    Wl = {n: loc[n] for n in WEIGHTS}
    Ml = {n: loc['m_' + n] for n in WEIGHTS}
    Vl = {n: loc['v_' + n] for n in WEIGHTS}
    L = x.shape[1]
    T = min(512, L)
    x0 = x.reshape(L, D_MODEL)
    tgt = loss_target.reshape(L, D_MODEL)

    first = ['w_in', 'a_w_uq', 'a_w_ukv', 'c_conv_w', 'd_conv_w']
    later = [n for n in SHARDED if n not in first]
    late_keys = ['w_out', 'f_w_in', 'f_conv', 'f_w_out']

    def shards(l, names):
        return [Wl[n][l].astype(BF16) if n in MXU_WEIGHTS else Wl[n][l] for n in names]

    def layer_weights(l, names, gathered):
        W = {n: Wl[n][l] for n in SMALL}
        for n in SHARDED:
            W[n] = jnp.zeros(layer_shape(n), BF16 if n in MXU_WEIGHTS else F32)
        for n, g in zip(names, gathered):
            W[n] = jnp.concatenate([g[j] for j in range(4)], axis=SHARD_AXIS[n] - 1)
        return W

    def chip_blocks(g, n):
        return jnp.stack(jnp.split(g, 4, axis=SHARD_AXIS[n] - 1))

    tabs = _tables(L)
    norm_in = [Wl['pre_mix_norm'][l].reshape(1, -1) for l in range(DEPTH)]
    def layer_shape(n):
        s = list(Wl[n].shape[1:])
        if n in SHARD_AXIS:
            s[SHARD_AXIS[n] - 1] *= 4
        return tuple(s)

    unbuild = jax.vjp(_build_layer, {n: jnp.zeros(layer_shape(n), F32) for n in WEIGHTS})[1]

    (h,) = _tw_fwd("l0_norm_in", _fn_norm_in, [(x0, _spec2(T, D_MODEL))], [norm_in[0]], [(D_MODEL, BF16)], L, T)
    gathered = _gather_copies(shards(0, first)).call("gather_l0")
    xs, saves, Ks = x0, [], []
    for l in range(DEPTH):
        last = l + 1 == DEPTH
        nxt = None if last else _gather_copies(shards(l + 1, SHARDED))
        if l == 0:
            def late(got):
                K_late = _build_layer(layer_weights(0, later, got))
                return {k: K_late[k] for k in late_keys}

            xs, h, sv = _layer_fwd(l, xs, h, _build_layer(layer_weights(0, first, gathered)), tabs, L, T,
                                   None if last else norm_in[l + 1], _gather_copies(shards(0, later)), late, nxt)
        else:
            xs, h, sv = _layer_fwd(l, xs, h, _build_layer(layer_weights(l, SHARDED, gathered)), tabs, L, T,
                                   None if last else norm_in[l + 1], None, None, nxt)
        gathered = sv['side']
        Ks.append(sv['K'])
        saves.append(sv)
    dy, loss_acc = _loss_call(xs, tgt, L)
    loss = lax.psum(loss_acc[0, 0], ("x", "y", "c"))

    ffn = ['f_w_in', 'f_conv_w', 'f_w_out', 'w_out']
    rest = [n for n in SHARDED if n not in ffn]

    def ffn_side(dK):
        only_w_out = {k: (dK[k] if k == 'w_out' else jnp.zeros(v.shape, F32)) for k, v in Ks[0].items()}
        g = {'f_w_in': dK['f_w_in'], 'f_conv_w': dK['f_conv'][0:3], 'f_w_out': dK['f_w_out'],
             'w_out': unbuild(only_w_out)[0]['w_out']}
        return _scatter_copies([chip_blocks(g[n], n) for n in ffn], None)

    def rest_blocks(dK):
        full = dict(dK)
        full.setdefault('pre_mix_norm', jnp.zeros((1, D_MODEL), F32))
        (g,) = unbuild(full)
        return [chip_blocks(g[n], n) for n in rest]

    def chip_sums(l, names, recvs):
        return [_sum_slots(f"sum_{n}_{l}", r.reshape(4, -1, r.shape[-1])) for n, r in zip(names, recvs)]

    grads = [None] * DEPTH
    pairs = {}
    dx, dhn = dy, None
    for l in reversed(range(DEPTH)):
        last = l + 1 == DEPTH

        def host_scatter(dK, got, up=None if last else grads[l + 1]):
            sets = [ffn_side(dK)] + ([] if up is None else [_scatter_copies(rest_blocks(up), None)])
            return _merge_copies(sets)

        def host_swap(dK, got, l=l, last=last):
            r = got['d_delta']
            parts = chip_sums(l, ffn, r[:len(ffn)]) + ([] if last else chip_sums(l + 1, rest, r[len(ffn):]))
            return _swap_copies(parts)

        dxa, dh, dK, dnext, got = _layer_bwd(l, dx, dhn, Ks[l], saves[l], tabs, L, T,
                                             None if last else norm_in[l + 1],
                                             {'d_delta': host_scatter, 'b_attn': host_swap})
        pairs.update({(l, n): p for n, p in zip(ffn, got['b_attn'])})
        if not last:
            pairs.update({(l + 1, n): p for n, p in zip(rest, got['b_attn'][len(ffn):])})
            grads[l + 1]['pre_mix_norm'] = dnext
        grads[l] = dK
        dx, dhn = dxa, dh
    (dx_in,), (grads[0]['pre_mix_norm'],) = _tw_bwd(
        "l0b_norm_in", _fn_norm_in, [(x0, _spec2(T, D_MODEL))], [norm_in[0]], [(dhn, _spec2(T, D_MODEL))], L, T,
        [True], [True], addto={0: (dx, _spec2(T, D_MODEL))})
    small_shapes = [Wl[n].shape for n in SMALL]
    gfull = [unbuild(grads[l])[0] for l in range(DEPTH)]
    gs = _pack([jnp.stack([gfull[l][n] for l in range(DEPTH)]) for n in SMALL], LANE, 8)
    *got0, recv_small = _scatter_copies([b.astype(BF16) for b in rest_blocks(grads[0])], gs).call("scatter_last")
    pairs.update({(0, n): p for n, p in zip(rest, _swap_copies(chip_sums(0, rest, got0)).call("swap_last"))})

    kinds = ['grad', 'delta', 'new_m', 'new_v']
    res = {}
    for n in SHARDED:
        upd = _adamw_call("adamw_" + n, [pairs[l, n] for l in range(DEPTH)], Wl[n], Ml[n], Vl[n])
        for kind, a in zip(kinds, upd):
            res[kind, n] = a
    small = _adamw_call("adamw_small", [recv_small], *[_pack([W_[n] for n in SMALL], LANE, 8)[None]
                                                       for W_ in (Wl, Ml, Vl)])
    for kind, s in zip(kinds, small):
        for n, a in zip(SMALL, _unpack(s, small_shapes)):
            res[kind, n] = a
    outs = [loss, dx_in.reshape(x.shape)]
    for kind in ['grad', 'delta', 'new_m', 'new_v']:
        outs += [res[kind, n] for n in WEIGHTS]
    return tuple(outs)
```

```python
import functools
import math

import numpy as np
import jax
import jax.numpy as jnp
from jax import lax
from jax.experimental import pallas as pl
from jax.experimental.pallas import tpu as pltpu

F32 = jnp.float32
BF16 = jnp.bfloat16
MESH = pl.DeviceIdType.MESH
VMEM_LIMIT = 48 * 1024 * 1024
LANE = 128

D_MODEL = 1024
DEPTH = 2
GRID_W = 64
ROPE_BASE = 10000.0
EPS = 1e-6
GROUP_W = 256
HEADS = 4
HD = 64
A_NOPE, A_ROPE, A_Q_LORA, A_KV_LORA = 64, 32, 192, 128
A_COLS = A_Q_LORA + A_KV_LORA + A_ROPE
B_COLS = 512
C_XBC = 512
C_COLS = GROUP_W + C_XBC + 8
D_QKV = 768
D_COLS = D_QKV + GROUP_W + 16
IN_COLS = A_COLS + B_COLS + C_COLS + D_COLS
C_CHUNK = 128
D_CHUNK = 64
C_PER_STEP = 1
D_PER_STEP = 2
D_FF = 2816
ADAM_LR, ADAM_B1, ADAM_B2, ADAM_EPS, ADAM_WD, ADAM_STEP = 0.001, 0.9, 0.999, 1e-08, 0.01, 10

WEIGHTS = ['pre_mix_norm', 'w_in', 'a_q_norm', 'a_w_uq', 'a_kv_norm', 'a_w_ukv', 'a_out_norm', 'b_q_norm',
           'b_k_norm', 'b_out_norm', 'c_conv_w', 'c_conv_b', 'c_a_log', 'c_dt_bias', 'c_d_skip', 'c_out_norm',
           'd_conv_w', 'd_a_log', 'd_dt_bias', 'd_out_norm', 'w_out', 'post_mix_norm', 'pre_ffn_norm', 'f_w_in',
           'f_conv_w', 'f_conv_b', 'f_w_out', 'post_ffn_norm']
SHARD_AXIS = {'w_in': 2, 'a_w_uq': 2, 'a_w_ukv': 2, 'c_conv_w': 2, 'd_conv_w': 2, 'w_out': 1, 'f_w_in': 2,
              'f_conv_w': 2, 'f_w_out': 1}
SHARDED = [n for n in WEIGHTS if n in SHARD_AXIS]
SMALL = [n for n in WEIGHTS if n not in SHARD_AXIS]
MXU_WEIGHTS = ('w_in', 'a_w_uq', 'a_w_ukv', 'w_out', 'f_w_in', 'f_w_out')

P_LAYOUT = [('b_q', 0, 512), ('c_xbc', 512, 512), ('a_cq', 1024, 256), ('b_k', 1280, 256), ('d_qkv', 1536, 768),
            ('b_v', 2304, 256), ('c_z', 2560, 256), ('d_z', 2816, 256), ('a_ckv', 3072, 128), ('a_kr', 3200, 128),
            ('c_dt', 3328, 128), ('d_b', 3456, 128), ('d_a', 3584, 128), ('pad', 3712, 128)]
P_OFF = {n: (o, w) for n, o, w in P_LAYOUT}
P_COLS = 3840
O_COLS = 1536


def _cparams(sem):
    return pltpu.CompilerParams(dimension_semantics=sem, vmem_limit_bytes=VMEM_LIMIT)


def _tile(n, target):
    best = None
    for d in range(LANE, min(n, target) + 1, LANE):
        if n % d == 0:
            best = d
    return best if best is not None else n


_NN = ((1,), (0,))
_NT = ((1,), (1,))
_TN = ((0,), (0,))


def _raw_dot(a, b, dims, hi):
    if hi:
        prec = lax.Precision.HIGH if hi == 'high' else lax.Precision.HIGHEST
        return lax.dot_general(a, b, (dims, ((), ())), precision=prec, preferred_element_type=F32)
    return lax.dot_general(a.astype(BF16), b.astype(BF16), (dims, ((), ())), preferred_element_type=F32)


def _make_dots(hi):
    @jax.custom_vjp
    def nn(a, b):
        return _raw_dot(a, b, _NN, hi)

    @jax.custom_vjp
    def nt(a, b):
        return _raw_dot(a, b, _NT, hi)

    @jax.custom_vjp
    def tn(a, b):
        return _raw_dot(a, b, _TN, hi)

    nn.defvjp(lambda a, b: (nn(a, b), (a, b)), lambda r, g: (nt(g, r[1]), tn(r[0], g)))
    nt.defvjp(lambda a, b: (nt(a, b), (a, b)), lambda r, g: (nn(g, r[1]), tn(g, r[0])))
    tn.defvjp(lambda a, b: (tn(a, b), (a, b)), lambda r, g: (nt(r[1], g), nn(r[0], g)))
    return nn, nt, tn


_nn, _nt, _tn = _make_dots(False)
_nn_hi, _nt_hi, _tn_hi = _make_dots(True)
_nn_h3, _nt_h3, _tn_h3 = _make_dots('high')


def _sigmoid(x):
    return 1.0 / (1.0 + jnp.exp(-x))


def _silu(x):
    return x * _sigmoid(x)


def _softplus(x):
    return jnp.maximum(x, 0.0) + jnp.log(1.0 + jnp.exp(-jnp.abs(x)))


def _rms(x, w, n=None):
    n = x.shape[-1] if n is None else n
    ms = jnp.sum(x * x, axis=-1, keepdims=True) * (1.0 / n)
    return x * lax.rsqrt(ms + EPS) * w


def _spec2(T, w, cb=0):
    return pl.BlockSpec((T, w), lambda i: (i, cb))


def _full_spec(a):
    nd = a.ndim
    return pl.BlockSpec(a.shape, lambda i: (0,) * nd)


def _tw_fwd(name, fn, acts, params, outs, L, T):
    na, npar = len(acts), len(params)

    def kern(*refs):
        a = [r[...].astype(F32) for r in refs[:na]]
        p = [r[...].astype(F32) for r in refs[na:na + npar]]
        res = fn(a, p)
        for r, o in zip(refs[na + npar:], res):
            r[...] = o.astype(r.dtype)

    return pl.pallas_call(
        kern, name=name, grid=(L // T,),
        in_specs=[s for _, s in acts] + [_full_spec(p) for p in params],
        out_specs=[_spec2(T, w) for w, _ in outs],
        out_shape=[jax.ShapeDtypeStruct((L, w), dt) for w, dt in outs],
        compiler_params=_cparams(("arbitrary",)),
    )(*[a for a, _ in acts], *params)


def _tw_bwd(name, fn, acts, params, douts, L, T, act_grad, par_grad, addto=None):
    na, npar, nd = len(acts), len(params), len(douts)
    addto = addto or {}
    add_keys = sorted(addto)
    ga = [k for k in range(na) if act_grad[k]]
    gp = [k for k in range(npar) if par_grad[k]]

    def kern(*refs):
        i = pl.program_id(0)
        a = [r[...].astype(F32) for r in refs[:na]]
        p = [r[...].astype(F32) for r in refs[na:na + npar]]
        g = [r[...].astype(F32) for r in refs[na + npar:na + npar + nd]]
        pos = na + npar + nd
        adds = [r[...].astype(F32) for r in refs[pos:pos + len(add_keys)]]
        pos += len(add_keys)
        da_refs = refs[pos:pos + len(ga)]
        dp_refs = refs[pos + len(ga):]

        def f(ad, pd):
            af, pf = list(a), list(p)
            for k, v in zip(ga, ad):
                af[k] = v
            for k, v in zip(gp, pd):
                pf[k] = v
            return fn(af, pf)

        _, vjp = jax.vjp(f, [a[k] for k in ga], [p[k] for k in gp])
        dad, dpd = vjp(list(g))
        for n, (r, d) in enumerate(zip(da_refs, dad)):
            if n in addto:
                d = d + adds[add_keys.index(n)]
            r[...] = d.astype(r.dtype)

        @pl.when(i == 0)
        def _():
            for r in dp_refs:
                r[...] = jnp.zeros(r.shape, F32)

        for r, d in zip(dp_refs, dpd):
            r[...] += d

    def width(spec):
        return spec.block_shape[-1]

    res = pl.pallas_call(
        kern, name=name, grid=(L // T,),
        in_specs=[s for _, s in acts] + [_full_spec(p) for p in params] + [s for _, s in douts]
        + [addto[k][1] for k in add_keys],
        out_specs=[_spec2(T, width(acts[k][1])) for k in ga] + [_full_spec(params[k]) for k in gp],
        out_shape=[jax.ShapeDtypeStruct((L, width(acts[k][1])), F32) for k in ga]
        + [jax.ShapeDtypeStruct(params[k].shape, F32) for k in gp],
        compiler_params=_cparams(("arbitrary",)),
    )(*[a for a, _ in acts], *params, *[a for a, _ in douts], *[addto[k][0] for k in add_keys])
    return list(res[:len(ga)]), list(res[len(ga):])


def _mm(name, a, b, mode, out_dtype, tm, tn, tk):
    halves_a = a.shape[-1] if (a.ndim == 3 and mode == 'nt') else None
    halves_b = b.shape[-1] if (b.ndim == 3 and mode == 'tn') else None
    if mode == 'nn':
        (M, K), N = a.shape, b.shape[1]
    elif mode == 'nt':
        M, K, N = a.shape[-2], (2 * halves_a if halves_a else a.shape[1]), b.shape[0]
    else:
        (K, M), N = a.shape, (2 * halves_b if halves_b else b.shape[1])
    tm = _tile(M, tm)
    tn = _tile(halves_b or N, tn)
    tk = _tile(halves_a or K, tk)
    nk = K // tk
    if mode == 'nn':
        a_spec = pl.BlockSpec((tm, tk), lambda i, j, k: (i, k))
        b_spec = pl.BlockSpec((tk, tn), lambda i, j, k: (k, j))
        dims = _NN
    elif mode == 'nt':
        a_spec = pl.BlockSpec((tm, tk), lambda i, j, k: (i, k))
        if halves_a:
            per = halves_a // tk
            a_spec = pl.BlockSpec((None, tm, tk), lambda i, j, k: (k // per, i, k % per))
        b_spec = pl.BlockSpec((tn, tk), lambda i, j, k: (j, k))
        dims = _NT
    else:
        a_spec = pl.BlockSpec((tk, tm), lambda i, j, k: (k, i))
        b_spec = pl.BlockSpec((tk, tn), lambda i, j, k: (k, j))
        if halves_b:
            per = halves_b // tn
            b_spec = pl.BlockSpec((None, tk, tn), lambda i, j, k: (j // per, k, j % per))
        dims = _TN

    def kern(a_ref, b_ref, o_ref, acc):
        k = pl.program_id(2)

        @pl.when(k == 0)
        def _():
            acc[...] = jnp.zeros(acc.shape, F32)

        acc[...] += lax.dot_general(a_ref[...].astype(BF16), b_ref[...].astype(BF16), (dims, ((), ())),
                                    preferred_element_type=F32)

        @pl.when(k == nk - 1)
        def _():
            o_ref[...] = acc[...].astype(o_ref.dtype)

    return pl.pallas_call(
        kern, name=name, grid=(M // tm, N // tn, nk),
        in_specs=[a_spec, b_spec],
        out_specs=pl.BlockSpec((tm, tn), lambda i, j, k: (i, j)),
        out_shape=jax.ShapeDtypeStruct((M, N), out_dtype),
        scratch_shapes=[pltpu.VMEM((tm, tn), F32)],
        compiler_params=_cparams(("arbitrary", "arbitrary", "arbitrary")),
    )(a, b)


def _host(kern, n_in, n_out, side, grid, n_scratch=0):
    if side is None:
        return kern, [], [], [], [], []
    ni, no = len(side.ins), len(side.out_shapes)

    def hosted(*refs):
        ins, s_in = refs[:n_in], refs[n_in:n_in + ni]
        pos = n_in + ni
        outs, s_out = refs[pos:pos + n_out], refs[pos + n_out:pos + n_out + no]
        pos += n_out + no
        own, sems = refs[pos:pos + n_scratch], refs[pos + n_scratch:]
        ids = [pl.program_id(d) for d in range(len(grid))]
        first = functools.reduce(lambda a, b: a & b, [i == 0 for i in ids])
        last = functools.reduce(lambda a, b: a & b, [i == g - 1 for i, g in zip(ids, grid)])

        @pl.when(first)
        def _():
            side.start(s_in, s_out, sems)

        kern(*ins, *outs, *own)

        @pl.when(last)
        def _():
            side.finish(s_in, s_out, sems)

    return hosted, [_ANY] * ni, [_ANY] * no, side.out_shapes, side.scratch(), side.ins


def _flash_fwd(name, q, k, v, H, rep, scale, L, side=None):
    tq = min(512, L)
    nq = L // tq
    KC = min(2048, L)
    nkc = L // KC
    log2e = 1.0 / math.log(2.0)

    def kern(q_ref, k_ref, v_ref, o_ref, lse_ref):
        qb = q_ref[...]
        m = jnp.full((tq, 1), -1e30, F32)
        l = jnp.zeros((tq, 1), F32)
        acc = jnp.zeros((tq, LANE), F32)
        for c in range(nkc):
            kb = k_ref[c * KC:(c + 1) * KC, :]
            vb = v_ref[c * KC:(c + 1) * KC, :]
            s = lax.dot_general(qb, kb, (_NT, ((), ())), preferred_element_type=F32) * (scale * log2e)
            mn = jnp.maximum(m, jnp.max(s, axis=-1, keepdims=True))
            al = jnp.exp2(m - mn)
            p = jnp.exp2(s - mn)
            l = al * l + jnp.sum(p, axis=-1, keepdims=True)
            acc = al * acc + lax.dot_general(p.astype(BF16), vb, (_NN, ((), ())), preferred_element_type=F32)
            m = mn
        o_ref[...] = acc / l
        lse_ref[...] = m * math.log(2.0) + jnp.log(l)

    kern, s_in, s_out, s_shapes, s_scratch, s_args = _host(kern, 3, 2, side, (H, nq))
    res = pl.pallas_call(
        kern, name=name, grid=(H, nq),
        in_specs=[pl.BlockSpec((tq, LANE), lambda h, i: (i, h)),
                  pl.BlockSpec((L, LANE), lambda h, i: (0, h // rep)),
                  pl.BlockSpec((L, LANE), lambda h, i: (0, h // rep))] + s_in,
        out_specs=[pl.BlockSpec((tq, LANE), lambda h, i: (i, h)),
                   pl.BlockSpec((tq, 1), lambda h, i: (h * nq + i, 0))] + s_out,
        out_shape=[jax.ShapeDtypeStruct((L, H * LANE), F32), jax.ShapeDtypeStruct((H * L, 1), F32)] + s_shapes,
        scratch_shapes=s_scratch,
        compiler_params=_cparams(("arbitrary", "arbitrary")),
    )(q, k, v, *s_args)
    return res[0], res[1], list(res[2:])


def _flash_bwd(name, q, k, v, o, lse, do, H, rep, scale, L, side=None):
    tq = min(512, L)
    nq = L // tq
    KC = min(1024, L)
    nkc = L // KC
    Hkv = H // rep

    def kern(q_ref, k_ref, v_ref, o_ref, lse_ref, do_ref, dq_ref, dk_ref, dv_ref):
        h = pl.program_id(0)
        i = pl.program_id(1)

        @pl.when((i == 0) & (h % rep == 0))
        def _():
            dk_ref[...] = jnp.zeros(dk_ref.shape, F32)
            dv_ref[...] = jnp.zeros(dv_ref.shape, F32)

        qb = q_ref[...]
        do = do_ref[...]
        dob = do.astype(BF16)
        delta = jnp.sum(do * o_ref[...], axis=-1, keepdims=True)
        lse = lse_ref[...]
        dq = jnp.zeros((tq, LANE), F32)
        for c in range(nkc):
            sl = slice(c * KC, (c + 1) * KC)
            kb = k_ref[sl, :]
            vb = v_ref[sl, :]
            s = lax.dot_general(qb, kb, (_NT, ((), ())), preferred_element_type=F32) * scale
            p = jnp.exp(s - lse)
            dp = lax.dot_general(dob, vb, (_NT, ((), ())), preferred_element_type=F32)
            ds = (p * (dp - delta) * scale).astype(BF16)
            dq = dq + lax.dot_general(ds, kb, (_NN, ((), ())), preferred_element_type=F32)
            dk_ref[sl, :] += lax.dot_general(ds, qb, (_TN, ((), ())), preferred_element_type=F32)
            dv_ref[sl, :] += lax.dot_general(p.astype(BF16), dob, (_TN, ((), ())), preferred_element_type=F32)
        dq_ref[...] = dq

    kern, s_in, s_out, s_shapes, s_scratch, s_args = _host(kern, 6, 3, side, (H, nq))
    res = pl.pallas_call(
        kern, name=name, grid=(H, nq),
        in_specs=[pl.BlockSpec((tq, LANE), lambda h, i: (i, h)),
                  pl.BlockSpec((L, LANE), lambda h, i: (0, h // rep)),
                  pl.BlockSpec((L, LANE), lambda h, i: (0, h // rep)),
                  pl.BlockSpec((tq, LANE), lambda h, i: (i, h)),
                  pl.BlockSpec((tq, 1), lambda h, i: (h * nq + i, 0)),
                  pl.BlockSpec((tq, LANE), lambda h, i: (i, h))] + s_in,
        out_specs=[pl.BlockSpec((tq, LANE), lambda h, i: (i, h)),
                   pl.BlockSpec((L, LANE), lambda h, i: (0, h // rep)),
                   pl.BlockSpec((L, LANE), lambda h, i: (0, h // rep))] + s_out,
        out_shape=[jax.ShapeDtypeStruct((L, H * LANE), F32), jax.ShapeDtypeStruct((L, Hkv * LANE), F32),
                   jax.ShapeDtypeStruct((L, Hkv * LANE), F32)] + s_shapes,
        scratch_shapes=s_scratch,
        compiler_params=_cparams(("arbitrary", "arbitrary")),
    )(q, k, v, o, lse, do, *s_args)
    return res[0], res[1], res[2], list(res[3:])


def _shift_dn(x, first_row):
    row = lax.broadcasted_iota(jnp.int32, x.shape, 0)
    return jnp.where(row == 0, first_row, pltpu.roll(x, 1, 0))


def _shift_up(x, last_row):
    n = x.shape[0]
    row = lax.broadcasted_iota(jnp.int32, x.shape, 0)
    return jnp.where(row == n - 1, last_row, pltpu.roll(x, n - 1, 0))


def _halo_specs(ndim, lead, T, tc, cb0, L):
    r8 = T // 8
    last8 = L // 8 - 1
    if ndim == 2:
        return [pl.BlockSpec((T, tc), lambda j, i: (i, cb0 + j)),
                pl.BlockSpec((8, tc), lambda j, i: (jnp.maximum(i * r8 - 1, 0), cb0 + j)),
                pl.BlockSpec((8, tc), lambda j, i: (jnp.minimum((i + 1) * r8, last8), cb0 + j))]
    return [pl.BlockSpec((None, T, tc), lambda j, i: (lead, i, cb0 + j)),
            pl.BlockSpec((None, 8, tc), lambda j, i: (lead, jnp.maximum(i * r8 - 1, 0), cb0 + j)),
            pl.BlockSpec((None, 8, tc), lambda j, i: (lead, jnp.minimum((i + 1) * r8, last8), cb0 + j))]


def _conv_rows(x_ref, xp_ref, xn_ref, w, first, last):
    x = x_ref[...]
    T = x.shape[0]
    w0, w1, w2, b = w[0:1], w[1:2], w[2:3], w[3:4]
    pr = jnp.where(first, 0.0, xp_ref[7:8, :])
    pr2 = jnp.where(first, 0.0, xp_ref[6:7, :])
    nr = jnp.where(last, 0.0, xn_ref[0:1, :])
    nr2 = jnp.where(last, 0.0, xn_ref[1:2, :])
    xm1 = _shift_dn(x, pr)
    xp1 = _shift_up(x, nr)
    pre = xm1 * w0 + x * w1 + xp1 * w2 + b
    pre_m1 = pr2 * w0 + pr * w1 + x[0:1] * w2 + b
    pre_T = x[T - 1:T] * w0 + nr * w1 + nr2 * w2 + b
    return x, xm1, xp1, pre, pre_m1, pre_T


def _conv_grads(dpre, dpre_m1, dpre_T, x, xm1, xp1, w):
    dx = _shift_up(dpre, dpre_T) * w[0:1] + dpre * w[1:2] + _shift_dn(dpre, dpre_m1) * w[2:3]
    row = lax.broadcasted_iota(jnp.int32, (8, x.shape[1]), 0)
    dw = (jnp.where(row == 0, jnp.sum(dpre * xm1, axis=0, keepdims=True), 0.0)
          + jnp.where(row == 1, jnp.sum(dpre * x, axis=0, keepdims=True), 0.0)
          + jnp.where(row == 2, jnp.sum(dpre * xp1, axis=0, keepdims=True), 0.0)
          + jnp.where(row == 3, jnp.sum(dpre, axis=0, keepdims=True), 0.0))
    return dx, dw


def _conv_fwd(name, x, col0, C, w8, act, L, tc):
    T = min(256, L)
    nt = L // T
    cb0 = col0 // tc

    def kern(x_ref, xp_ref, xn_ref, w_ref, o_ref):
        i = pl.program_id(1)
        x = x_ref[...]
        w = w_ref[...]
        pr = jnp.where(i == 0, 0.0, xp_ref[7:8, :])
        nr = jnp.where(i == nt - 1, 0.0, xn_ref[0:1, :])
        pre = _shift_dn(x, pr) * w[0:1] + x * w[1:2] + _shift_up(x, nr) * w[2:3] + w[3:4]
        o_ref[...] = _silu(pre) if act else pre

    return pl.pallas_call(
        kern, name=name, grid=(C // tc, nt),
        in_specs=_halo_specs(2, None, T, tc, cb0, L) + [pl.BlockSpec((8, tc), lambda j, i: (0, j))],
        out_specs=pl.BlockSpec((T, tc), lambda j, i: (i, j)),
        out_shape=jax.ShapeDtypeStruct((L, C), F32),
        compiler_params=_cparams(("arbitrary", "arbitrary")),
    )(x, x, x, w8)


def _conv_bwd(name, x, col0, C, w8, act, gs, L, tc):
    T = min(256, L)
    nt = L // T
    cb0 = col0 // tc
    ng = len(gs)

    def dact(pre, g):
        if not act:
            return g
        s = _sigmoid(pre)
        return g * (s * (1.0 + pre * (1.0 - s)))

    def kern(*refs):
        x_ref, xp_ref, xn_ref, w_ref = refs[:4]
        g_refs = refs[4:4 + 3 * ng]
        dx_ref, dw_ref = refs[4 + 3 * ng:]
        i = pl.program_id(1)
        first = i == 0
        last = i == nt - 1
        w = w_ref[...]
        g = g_refs[0][...]
        gp = g_refs[1][7:8, :]
        gn = g_refs[2][0:1, :]
        for n in range(1, ng):
            g = g + g_refs[3 * n][...]
            gp = gp + g_refs[3 * n + 1][7:8, :]
            gn = gn + g_refs[3 * n + 2][0:1, :]
        x, xm1, xp1, pre, pre_m1, pre_T = _conv_rows(x_ref, xp_ref, xn_ref, w, first, last)
        dpre_m1 = jnp.where(first, 0.0, dact(pre_m1, gp))
        dpre_T = jnp.where(last, 0.0, dact(pre_T, gn))
        dx_ref[...], dw = _conv_grads(dact(pre, g), dpre_m1, dpre_T, x, xm1, xp1, w)

        @pl.when(first)
        def _():
            dw_ref[...] = jnp.zeros((8, tc), F32)

        dw_ref[...] += dw

    g_specs, g_args = [], []
    for arr, lead in gs:
        g_specs += _halo_specs(arr.ndim, lead, T, tc, 0, L)
        g_args += [arr, arr, arr]
    return pl.pallas_call(
        kern, name=name, grid=(C // tc, nt),
        in_specs=_halo_specs(2, None, T, tc, cb0, L) + [pl.BlockSpec((8, tc), lambda j, i: (0, j))] + g_specs,
        out_specs=[pl.BlockSpec((T, tc), lambda j, i: (i, j)), pl.BlockSpec((8, tc), lambda j, i: (0, j))],
        out_shape=[jax.ShapeDtypeStruct((L, C), F32), jax.ShapeDtypeStruct((8, C), F32)],
        compiler_params=_cparams(("arbitrary", "arbitrary")),
    )(x, x, x, w8, *g_args)


FFN_TC = 1408


def _ffn_gate_fwd(name, u, w8, L):
    T = min(256, L)
    nt = L // T
    ncb = D_FF // FFN_TC

    def kern(xg, xgp, xgn, xu, xup, xun, wg_ref, wu_ref, o_ref):
        i = pl.program_id(1)
        pre_g = _conv_rows(xg, xgp, xgn, wg_ref[...], i == 0, i == nt - 1)[3]
        pre_u = _conv_rows(xu, xup, xun, wu_ref[...], i == 0, i == nt - 1)[3]
        o_ref[...] = (_silu(pre_g) * pre_u).astype(BF16)

    return pl.pallas_call(
        kern, name=name, grid=(ncb, nt),
        in_specs=_halo_specs(2, None, T, FFN_TC, 0, L) + _halo_specs(2, None, T, FFN_TC, ncb, L)
        + [pl.BlockSpec((8, FFN_TC), lambda j, i: (0, j)), pl.BlockSpec((8, FFN_TC), lambda j, i: (0, j + ncb))],
        out_specs=pl.BlockSpec((T, FFN_TC), lambda j, i: (i, j)),
        out_shape=jax.ShapeDtypeStruct((L, D_FF), BF16),
        compiler_params=_cparams(("arbitrary", "arbitrary")),
    )(u, u, u, u, u, u, w8, w8)


def _ffn_gate_bwd(name, u, w8, da, L):
    T = min(128, L)
    nt = L // T
    ncb = D_FF // FFN_TC

    def kern(xg, xgp, xgn, xu, xup, xun, wg_ref, wu_ref, d_ref, dp_ref, dn_ref, du_ref, dwg_ref, dwu_ref):
        i = pl.program_id(1)
        first = i == 0
        last = i == nt - 1
        wg = wg_ref[...]
        wu = wu_ref[...]
        g, gm1, gp1, pg, pg_m1, pg_T = _conv_rows(xg, xgp, xgn, wg, first, last)
        v, vm1, vp1, pu, pu_m1, pu_T = _conv_rows(xu, xup, xun, wu, first, last)

        def dpre(pg_, pu_, d):
            s = _sigmoid(pg_)
            return d * pu_ * (s * (1.0 + pg_ * (1.0 - s))), d * (pg_ * s)

        dg, dv = dpre(pg, pu, d_ref[...])
        dg_m1, dv_m1 = dpre(pg_m1, pu_m1, jnp.where(first, 0.0, dp_ref[7:8, :]))
        dg_T, dv_T = dpre(pg_T, pu_T, jnp.where(last, 0.0, dn_ref[0:1, :]))
        du_ref[0], dwg = _conv_grads(dg, dg_m1, dg_T, g, gm1, gp1, wg)
        du_ref[1], dwu = _conv_grads(dv, dv_m1, dv_T, v, vm1, vp1, wu)

        @pl.when(first)
        def _():
            dwg_ref[...] = jnp.zeros(dwg_ref.shape, F32)
            dwu_ref[...] = jnp.zeros(dwu_ref.shape, F32)

        dwg_ref[...] += dwg
        dwu_ref[...] += dwu

    wspec = pl.BlockSpec((8, FFN_TC), lambda j, i: (0, j))
    du, dwg, dwu = pl.pallas_call(
        kern, name=name, grid=(ncb, nt),
        in_specs=_halo_specs(2, None, T, FFN_TC, 0, L) + _halo_specs(2, None, T, FFN_TC, ncb, L)
        + [wspec, pl.BlockSpec((8, FFN_TC), lambda j, i: (0, j + ncb))] + _halo_specs(2, None, T, FFN_TC, 0, L),
        out_specs=[pl.BlockSpec((2, T, FFN_TC), lambda j, i: (0, i, j)), wspec, wspec],
        out_shape=[jax.ShapeDtypeStruct((2, L, D_FF), F32), jax.ShapeDtypeStruct((8, D_FF), F32),
                   jax.ShapeDtypeStruct((8, D_FF), F32)],
        compiler_params=_cparams(("arbitrary", "arbitrary")),
    )(u, u, u, u, u, u, w8, w8, da, da, da)
    return du, jnp.concatenate([dwg, dwu], axis=1)


def _masks(Q, rev):
    ri = lax.broadcasted_iota(jnp.int32, (Q, Q), 0)
    ci = lax.broadcasted_iota(jnp.int32, (Q, Q), 1)
    diff = (ri - ci) * (1 - 2 * rev)
    return diff >= 0, diff > 0


def _lane_pick(v, sel):
    return jnp.sum(v * sel, axis=-1, keepdims=True)


def _ssd_chunk(S, x, B, C, dtraw, alog, dtb, rev, kept=None):
    Q = dtraw.shape[0]
    incl, _ = _masks(Q, rev)
    tri = incl.astype(F32)
    dt = _softplus(dtraw + dtb)
    a_all = dt * (-jnp.exp(alog))
    acum_all = _nn_hi(tri, a_all)
    total_all = jnp.sum(a_all, axis=0, keepdims=True)
    lane = lax.broadcasted_iota(jnp.int32, (1, LANE), 1)
    acum_t = acum_all.T
    sub = lax.broadcasted_iota(jnp.int32, (LANE, 1), 0)
    ys, Sn = [], []
    for h in range(HEADS):
        g = h // 2
        sel = (lane == rev * 4 + h).astype(F32)
        acum = _lane_pick(acum_all, sel)
        dth = _lane_pick(dt, sel)
        tot = _lane_pick(total_all, sel)
        seg = acum - jnp.sum(acum_t * (sub == rev * 4 + h).astype(F32), axis=0, keepdims=True)
        decay = jnp.exp(jnp.where(incl, seg, -1e30))
        xdt = x[h] * dth
        Sh = S[HD * h:HD * (h + 1), :]
        scores = _nt(C[g], B[g]) * decay
        y_diag = _nn(scores, xdt)
        states = _tn(xdt, B[g] * jnp.exp(tot - acum))
        y_off = _nt(C[g], Sh) * jnp.exp(acum)
        ys.append(y_diag + y_off)
        Sn.append(Sh * jnp.exp(tot) + states)
    return ys, jnp.concatenate(Sn, axis=0), []


def _inv_unit_raw(Lm):
    N = Lm.shape[0]
    Q = D_CHUNK
    ri = lax.broadcasted_iota(jnp.int32, (N, N), 0)
    ci = lax.broadcasted_iota(jnp.int32, (N, N), 1)
    X = (ri == ci).astype(F32) - Lm
    P = _raw_dot(Lm, Lm, _NN, False)
    n = 2
    while n < Q:
        X = X + _raw_dot(X, P, _NN, False)
        n *= 2
        if n < Q:
            P = _raw_dot(P, P, _NN, False)
    return X


@jax.custom_vjp
def _inv_unit(Lm, T_saved):
    return _inv_unit_raw(Lm) if T_saved is None else T_saved


def _inv_unit_f(Lm, T_saved):
    T = _inv_unit_raw(Lm) if T_saved is None else T_saved
    return T, T


def _inv_unit_b(T, g):
    return -_raw_dot(_raw_dot(T, g, _TN, False), T, _NT, False), None


_inv_unit.defvjp(_inv_unit_f, _inv_unit_b)


def _delta_chunk(S, q, k, v, braw, araw, alog, dtb, rev, kept=None):
    Q = braw.shape[0]
    N = HEADS * Q
    tri = _masks(Q, rev)[0].astype(F32)
    ri = lax.broadcasted_iota(jnp.int32, (N, N), 0)
    ci = lax.broadcasted_iota(jnp.int32, (N, N), 1)
    sh = int(math.log2(Q))
    same = (ri >> sh) == (ci >> sh)
    diff = (ri - ci) * (1 - 2 * rev)
    incl = same & (diff >= 0)
    strict = same & (diff > 0)
    beta_all = _sigmoid(braw)
    g_all = -jnp.exp(alog) * _softplus(araw + dtb)
    G_all = _nn_hi(tri, g_all)
    Gtot_all = jnp.sum(g_all, axis=0, keepdims=True)
    r = lax.broadcasted_iota(jnp.int32, (N, LANE), 0)
    l = lax.broadcasted_iota(jnp.int32, (N, LANE), 1)
    selm = (l == rev * 4 + (r >> sh)).astype(F32)
    rows4 = lambda a: jnp.concatenate([a] * HEADS, axis=0)
    XG = rows4(G_all) * selm
    G = jnp.sum(XG, axis=-1, keepdims=True)
    bt = jnp.sum(rows4(beta_all) * selm, axis=-1, keepdims=True)
    Gtot = jnp.sum(Gtot_all * selm, axis=-1, keepdims=True)
    decay = jnp.exp(jnp.where(incl, G - _nt_h3(jnp.ones((N, LANE), F32), XG), -1e30))
    qs, ks, vs = (jnp.concatenate(t, axis=0) for t in (q, k, v))
    qn = qs * lax.rsqrt(jnp.sum(qs * qs, axis=-1, keepdims=True) + 1e-6)
    kn = ks * lax.rsqrt(jnp.sum(ks * ks, axis=-1, keepdims=True) + 1e-6)
    qc = qn * (HD ** -0.5)
    kb = kn * bt
    T = _inv_unit(jnp.where(strict, _nt(kb, kn) * decay, 0.0), None if kept is None else kept[0])
    eG = jnp.exp(G)
    u = _nn(T, vs * bt)
    w = _nn(T, kb * eG)
    qk = _nt(qc, kn) * decay
    spread = (lax.broadcasted_iota(jnp.int32, (HD, N), 0)
              == (lax.broadcasted_iota(jnp.int32, (HD, N), 1) & (HD - 1))).astype(F32)
    wide = lambda a: jnp.where(same, _nn(a, spread), 0.0)
    v_new = u - _nn(wide(w), S)
    o = _nn(wide(qc * eG), S) + _nn(qk, v_new)
    S_new = S * jnp.exp(Gtot) + _tn(wide(kn * jnp.exp(Gtot - G)), v_new)
    return [o[Q * h:Q * (h + 1), :] for h in range(HEADS)], S_new, [T]


def _seq_pieces(ref, r0, Q, splits):
    if splits is None:
        return ref[r0:r0 + Q, :]
    return [[ref[r0:r0 + Q, o + w * t:o + w * (t + 1)] for t in range(n)] for o, w, n in splits]


def _store_pieces(ref, r0, Q, splits, vals, extra=None):
    if splits is None:
        ref[r0:r0 + Q, :] = vals
        return
    for g, (o, w, n) in enumerate(splits):
        for t in range(n):
            v = vals[g][t]
            if extra is not None and g == 0:
                v = v + extra[r0:r0 + Q, o + w * t:o + w * (t + 1)]
            ref[r0:r0 + Q, o + w * t:o + w * (t + 1)] = v


def _flat(ins):
    out = []
    for v in ins:
        if isinstance(v, list):
            out.extend(v)
        else:
            out.append(v)
    return out


def _scan_fwd(name, chunk_fn, seqs, rows, Q, L, CH, kept_shapes=()):
    nc = L // Q
    nb = nc // CH
    ns, nr = len(seqs), len(rows)
    nk = 1 + len(kept_shapes)
    BQ = Q * CH

    def kern(*refs):
        s_refs = (refs[:ns], refs[ns:2 * ns])
        r_refs = refs[2 * ns:2 * ns + nr]
        pos = 2 * ns + nr
        y_refs = refs[pos:pos + 2]
        k_refs = (refs[pos + 2:pos + 2 + nk], refs[pos + 2 + nk:pos + 2 + 2 * nk])
        S_scr = refs[pos + 2 + 2 * nk]
        i = pl.program_id(0)

        @pl.when(i == 0)
        def _():
            S_scr[...] = jnp.zeros(S_scr.shape, F32)

        rws = [r[...] for r in r_refs]
        for d in (0, 1):
            S = S_scr[d]
            for cc in range(CH):
                c = cc if d == 0 else CH - 1 - cc
                k_refs[d][0][c] = S
                ins = [_seq_pieces(r, c * Q, Q, sp) for r, (_, _, _, sp) in zip(s_refs[d], seqs)]
                ys, S, kept = chunk_fn(S, *_flat(ins), *rws, d)
                for r, v in zip(k_refs[d][1:], kept):
                    r[c] = v
                for h in range(HEADS):
                    y_refs[d][c * Q:(c + 1) * Q, HD * h:HD * (h + 1)] = ys[h]
            S_scr[d] = S

    fwd_specs = [pl.BlockSpec((BQ, w), functools.partial(lambda i, cb: (i, cb), cb=cb)) for _, w, cb, _ in seqs]
    rev_specs = [pl.BlockSpec((BQ, w), functools.partial(lambda i, cb: (nb - 1 - i, cb), cb=cb))
                 for _, w, cb, _ in seqs]
    arrs = [a for a, _, _, _ in seqs]
    k_shapes = [(GROUP_W, HD)] + list(kept_shapes)
    res = pl.pallas_call(
        kern, name=name, grid=(nb,),
        in_specs=fwd_specs + rev_specs + [pl.BlockSpec((1, LANE), lambda i: (0, 0)) for _ in rows],
        out_specs=[pl.BlockSpec((BQ, GROUP_W), lambda i: (i, 0)),
                   pl.BlockSpec((BQ, GROUP_W), lambda i: (nb - 1 - i, 0))]
        + [pl.BlockSpec((CH,) + s, lambda i: (i, 0, 0)) for s in k_shapes]
        + [pl.BlockSpec((CH,) + s, lambda i: (nb - 1 - i, 0, 0)) for s in k_shapes],
        out_shape=[jax.ShapeDtypeStruct((L, GROUP_W), F32)] * 2
        + [jax.ShapeDtypeStruct((nc,) + s, F32) for s in k_shapes] * 2,
        scratch_shapes=[pltpu.VMEM((2, GROUP_W, HD), F32)],
        compiler_params=_cparams(("arbitrary",)),
    )(*arrs, *arrs, *rows)
    return res[0], res[1], list(res[2:2 + nk]), list(res[2 + nk:])


def _scan_bwd(name, chunk_fn, seqs, rows, ssaves, dy, extra, Q, L, CH, side=None):
    nc = L // Q
    nb = nc // CH
    BQ = Q * CH
    ns, nr = len(seqs), len(rows)
    nk = len(ssaves[0])
    has_extra = extra is not None

    def kern(*refs):
        s_refs = (refs[:ns], refs[ns:2 * ns])
        pos = 2 * ns
        r_refs = refs[pos:pos + nr]
        pos += nr
        k_refs = (refs[pos:pos + nk], refs[pos + nk:pos + 2 * nk])
        pos += 2 * nk
        dy_refs = refs[pos:pos + 2]
        pos += 2
        ex_ref = refs[pos] if has_extra else None
        pos += 1 if has_extra else 0
        ds_refs = (refs[pos:pos + ns], refs[pos + ns:pos + 2 * ns])
        pos += 2 * ns
        dr_refs = refs[pos:pos + nr]
        dS_scr = refs[pos + nr]
        i = pl.program_id(0)

        @pl.when(i == 0)
        def _():
            dS_scr[...] = jnp.zeros(dS_scr.shape, F32)
            for r in dr_refs:
                r[...] = jnp.zeros(r.shape, F32)

        rws = [r[...] for r in r_refs]
        dr_acc = [jnp.zeros((1, LANE), F32) for _ in rows]
        for d in (0, 1):
            dS = dS_scr[d]
            for cc in range(CH):
                c = CH - 1 - cc if d == 0 else cc
                S = k_refs[d][0][c]
                kept = [r[c] for r in k_refs[d][1:]]
                dys = [dy_refs[d][c * Q:(c + 1) * Q, HD * h:HD * (h + 1)] for h in range(HEADS)]
                ins = [_seq_pieces(r, c * Q, Q, sp) for r, (_, _, _, sp) in zip(s_refs[d], seqs)]
                _, vjp = jax.vjp(
                    functools.partial(
                        lambda S_, ins_, rws_, d_, kept_: chunk_fn(S_, *_flat(ins_), *rws_, d_, kept_)[:2],
                        d_=d, kept_=kept),
                    S, ins, rws)
                dS, dins, drws = vjp((dys, dS))
                for n_, (r, (_, _, _, sp)) in enumerate(zip(ds_refs[d], seqs)):
                    _store_pieces(r, c * Q, Q, sp, dins[n_],
                                  extra=ex_ref if (has_extra and d == 0 and n_ == 0) else None)
                dr_acc = [a + g for a, g in zip(dr_acc, drws)]
            dS_scr[d] = dS
        for r, g in zip(dr_refs, dr_acc):
            r[...] += g

    def blk(shape, rev, cb=0):
        nd = len(shape)
        if rev:
            return pl.BlockSpec(shape, lambda i: (i, cb) + (0,) * (nd - 2))
        return pl.BlockSpec(shape, lambda i: (nb - 1 - i, cb) + (0,) * (nd - 2))

    arrs = [a for a, _, _, _ in seqs]
    in_specs = [blk((BQ, w), False, cb) for _, w, cb, _ in seqs] + [blk((BQ, w), True, cb) for _, w, cb, _ in seqs]
    in_specs += [pl.BlockSpec((1, LANE), lambda i: (0, 0)) for _ in rows]
    in_specs += [blk((CH,) + a.shape[1:], False) for a in ssaves[0]]
    in_specs += [blk((CH,) + a.shape[1:], True) for a in ssaves[1]]
    in_specs += [blk((BQ, GROUP_W), False), blk((BQ, GROUP_W), True)]
    args = arrs + arrs + list(rows) + list(ssaves[0]) + list(ssaves[1]) + [dy, dy]
    if has_extra:
        in_specs.append(blk((BQ, GROUP_W), False))
        args.append(extra)
    kern, s_in, s_out, s_shapes, s_scratch, s_args = _host(kern, len(args), 2 * ns + nr, side, (nb,), 1)
    res = pl.pallas_call(
        kern, name=name, grid=(nb,),
        in_specs=in_specs + s_in,
        out_specs=[blk((BQ, w), False) for _, w, _, _ in seqs] + [blk((BQ, w), True) for _, w, _, _ in seqs]
        + [pl.BlockSpec((1, LANE), lambda i: (0, 0)) for _ in rows] + s_out,
        out_shape=[jax.ShapeDtypeStruct((L, w), F32) for _, w, _, _ in seqs] * 2
        + [jax.ShapeDtypeStruct((1, LANE), F32) for _ in rows] + s_shapes,
        scratch_shapes=[pltpu.VMEM((2, GROUP_W, HD), F32)] + s_scratch,
        compiler_params=_cparams(("arbitrary",)),
    )(*args, *s_args)
    return list(res[:ns]), list(res[ns:2 * ns]), list(res[2 * ns:2 * ns + nr]), list(res[2 * ns + nr:])


def _loss_call(y, tgt, L):
    T = min(256, L)

    def kern(y_ref, t_ref, dy_ref, l_ref):
        i = pl.program_id(0)
        e = y_ref[...] - t_ref[...]
        dy_ref[...] = e * (1.0 / D_MODEL)

        @pl.when(i == 0)
        def _():
            l_ref[...] = jnp.zeros(l_ref.shape, F32)

        part = 0.5 * jnp.sum(jnp.sum(e * e, axis=-1, keepdims=True) * (1.0 / D_MODEL), axis=0, keepdims=True)
        l_ref[...] += jnp.broadcast_to(part, l_ref.shape)

    return pl.pallas_call(
        kern, name="loss_head", grid=(L // T,),
        in_specs=[_spec2(T, D_MODEL), _spec2(T, D_MODEL)],
        out_specs=[_spec2(T, D_MODEL), pl.BlockSpec((8, LANE), lambda i: (0, 0))],
        out_shape=[jax.ShapeDtypeStruct((L, D_MODEL), F32), jax.ShapeDtypeStruct((8, LANE), F32)],
        compiler_params=_cparams(("arbitrary",)),
    )(y, tgt)


_ANY = pl.BlockSpec(memory_space=pl.ANY)


def _coords():
    return lax.axis_index("x"), lax.axis_index("y"), lax.axis_index("c")


class _Copies:
    def __init__(self, ins, out_shapes, copies_fn, n_remote, n_local):
        self.ins, self.out_shapes, self.copies_fn = list(ins), list(out_shapes), copies_fn
        self.n_remote, self.n_local = n_remote, n_local

    def scratch(self):
        return [pltpu.SemaphoreType.DMA((self.n_remote,)), pltpu.SemaphoreType.DMA((self.n_remote,)),
                pltpu.SemaphoreType.DMA((self.n_local,))]

    def _descr(self, in_refs, out_refs, sems):
        send_sems, recv_sems, lsems = sems
        remote, local = self.copies_fn(list(in_refs), list(out_refs))
        assert len(remote) == self.n_remote and len(local) == self.n_local
        mk = lambda k, src, dst, peer: pltpu.make_async_remote_copy(
            src_ref=src, dst_ref=dst, send_sem=send_sems.at[k], recv_sem=recv_sems.at[k], device_id=peer,
            device_id_type=MESH)
        sends = [mk(k, src, dst, peer) for k, (src, dst, _, peer) in enumerate(remote)]
        recvs = [mk(k, src, land, peer) for k, (src, _, land, peer) in enumerate(remote)]
        locs = [pltpu.make_async_copy(src, dst, lsems.at[k]) for k, (src, dst) in enumerate(local)]
        return sends, recvs, locs

    def start(self, in_refs, out_refs, sems):
        sends, _, locs = self._descr(in_refs, out_refs, sems)
        for c in locs + sends:
            c.start()

    def finish(self, in_refs, out_refs, sems):
        sends, recvs, locs = self._descr(in_refs, out_refs, sems)
        for c in recvs:
            c.wait_recv()
        for c in sends:
            c.wait_send()
        for c in locs:
            c.wait()

    def call(self, name):
        ni, no = len(self.ins), len(self.out_shapes)

        def body(*refs):
            self.start(refs[:ni], refs[ni:ni + no], refs[ni + no:])
            self.finish(refs[:ni], refs[ni:ni + no], refs[ni + no:])

        return pl.pallas_call(body, name=name, in_specs=[_ANY] * ni, out_specs=[_ANY] * no,
                              out_shape=self.out_shapes, scratch_shapes=self.scratch())(*self.ins)


def _chip_peers(x, y):
    return [(1 - x, y), (x, 1 - y), (1 - x, 1 - y)]


def _gather_copies(arrs):
    def copies_fn(ins, outs):
        x, y, c = _coords()
        me = 2 * x + y
        remote, local = [], []
        for src, out in zip(ins, outs):
            local.append((src, out.at[me]))
            for px, py in _chip_peers(x, y):
                remote.append((src, out.at[me], out.at[2 * px + py], (px, py, c)))
        return remote, local

    shapes = [jax.ShapeDtypeStruct((4,) + a.shape, a.dtype) for a in arrs]
    return _Copies(arrs, shapes, copies_fn, 3 * len(arrs), len(arrs))


def _scatter_copies(Gs, small):
    nb = len(Gs)

    def copies_fn(ins, outs):
        x, y, c = _coords()
        me = 2 * x + y
        remote, local = [], []
        for g, out in zip(ins[:nb], outs[:nb]):
            local.append((g.at[me], out.at[me]))
            for px, py in _chip_peers(x, y):
                remote.append((g.at[2 * px + py], out.at[me], out.at[2 * px + py], (px, py, c)))
        if small is not None:
            dev = 4 * x + 2 * y + c
            gs, outs_ = ins[nb], outs[nb]
            local.append((gs, outs_.at[dev]))
            for mask in range(1, 8):
                px, py, pc = x ^ (mask >> 2), y ^ ((mask >> 1) & 1), c ^ (mask & 1)
                remote.append((gs, outs_.at[dev], outs_.at[4 * px + 2 * py + pc], (px, py, pc)))
        return remote, local

    ins = list(Gs) + ([small] if small is not None else [])
    shapes = [jax.ShapeDtypeStruct(g.shape, g.dtype) for g in Gs]
    if small is not None:
        shapes.append(jax.ShapeDtypeStruct((8,) + small.shape, small.dtype))
    extra = 1 if small is not None else 0
    return _Copies(ins, shapes, copies_fn, 3 * nb + 7 * extra, nb + extra)


SWAP_STREAMS = 8


def _row_chunks(rows):
    k = SWAP_STREAMS
    if rows % (8 * k) == 0 and rows >= 64 * k:
        return [(q * (rows // k), rows // k) for q in range(k)]
    return [(0, rows)]


def _swap_copies(parts):
    chunks = [_row_chunks(p.shape[0]) for p in parts]
    n = sum(len(ch) for ch in chunks)

    def copies_fn(ins, outs):
        x, y, c = _coords()
        remote, local = [], []
        for src, out, ch in zip(ins, outs, chunks):
            for r0, nr in ch:
                rows = pl.ds(r0, nr)
                local.append((src.at[rows], out.at[c, rows]))
                remote.append((src.at[rows], out.at[c, rows], out.at[1 - c, rows], (x, y, 1 - c)))
        return remote, local

    shapes = [jax.ShapeDtypeStruct((2,) + p.shape, p.dtype) for p in parts]
    return _Copies(parts, shapes, copies_fn, n, n)


def _merge_copies(sets):
    ins = [a for s in sets for a in s.ins]
    shapes = [o for s in sets for o in s.out_shapes]

    def copies_fn(in_refs, out_refs):
        remote, local, pi, po = [], [], 0, 0
        for s in sets:
            r, l = s.copies_fn(in_refs[pi:pi + len(s.ins)], out_refs[po:po + len(s.out_shapes)])
            remote += r
            local += l
            pi += len(s.ins)
            po += len(s.out_shapes)
        return remote, local

    return _Copies(ins, shapes, copies_fn, sum(s.n_remote for s in sets), sum(s.n_local for s in sets))


def _row_tile(rows):
    best = rows
    for d in range(8, min(rows, 256) + 1, 8):
        if rows % d == 0:
            best = d
    return best


def _sum_slots(name, recv, out_dtype=F32):
    n, R, W = recv.shape
    tr = _row_tile(R)

    def kern(r_ref, o_ref):
        acc = r_ref[0].astype(F32)
        for s in range(1, n):
            acc = acc + r_ref[s].astype(F32)
        o_ref[...] = acc.astype(out_dtype)

    return pl.pallas_call(
        kern, name=name, grid=(R // tr,),
        in_specs=[pl.BlockSpec((n, tr, W), lambda i: (0, i, 0))],
        out_specs=pl.BlockSpec((tr, W), lambda i: (i, 0)),
        out_shape=jax.ShapeDtypeStruct((R, W), out_dtype),
        compiler_params=_cparams(("arbitrary",)),
    )(recv)


def _adamw_call(name, slots, w, m, v):
    nl = len(slots)
    n, R, W = slots[0].shape
    tr = _row_tile(R)
    nr = R // tr

    def kern(*refs):
        s_refs = refs[:nl]
        w_ref, m_ref, v_ref, g_ref, d_ref, nm_ref, nv_ref = refs[nl:]
        layer = pl.program_id(0)
        g = s_refs[0][0].astype(F32)
        for s in range(1, n):
            g = g + s_refs[0][s].astype(F32)
        for l in range(1, nl):
            gl = s_refs[l][0].astype(F32)
            for s in range(1, n):
                gl = gl + s_refs[l][s].astype(F32)
            g = jnp.where(layer == l, gl, g)
        m_ = ADAM_B1 * m_ref[...] + (1.0 - ADAM_B1) * g
        v_ = ADAM_B2 * v_ref[...] + (1.0 - ADAM_B2) * (g * g)
        m_hat = m_ / (1.0 - ADAM_B1 ** ADAM_STEP)
        v_hat = v_ / (1.0 - ADAM_B2 ** ADAM_STEP)
        g_ref[...] = g
        d_ref[...] = -ADAM_LR * (m_hat / (jnp.sqrt(v_hat) + ADAM_EPS) + ADAM_WD * w_ref[...])
        nm_ref[...] = m_
        nv_ref[...] = v_

    blk = pl.BlockSpec((None, tr, W), lambda l, i: (l, i, 0))
    return pl.pallas_call(
        kern, name=name, grid=(nl, nr),
        in_specs=[pl.BlockSpec((n, tr, W), lambda l, i: (0, i, 0)) for _ in slots] + [blk, blk, blk],
        out_specs=[blk, blk, blk, blk],
        out_shape=[jax.ShapeDtypeStruct((nl, R, W), F32)] * 4,
        compiler_params=_cparams(("arbitrary", "arbitrary")),
    )(*slots, w, m, v)


def _pack(arrs, width, row_mult):
    flat = jnp.concatenate([a.reshape(-1) for a in arrs])
    n = flat.shape[0]
    rows = -(-n // width)
    rows = -(-rows // row_mult) * row_mult
    return jnp.pad(flat, (0, rows * width - n)).reshape(rows, width)


def _unpack(buf, shapes):
    flat = buf.reshape(-1)
    out, pos = [], 0
    for s in shapes:
        n = int(np.prod(s))
        out.append(flat[pos:pos + n].reshape(s))
        pos += n
    return out


def _rope_angles(L, rot_dim):
    rows = L // GRID_W
    row = jnp.repeat(jnp.arange(rows), GRID_W).astype(F32)
    col = jnp.tile(jnp.arange(GRID_W), rows).astype(F32)
    sec = rot_dim // 2
    inv_freq = ROPE_BASE ** (-jnp.arange(0, sec, 2, dtype=F32) / sec)
    ang_r = row[:, None] * inv_freq
    ang_c = col[:, None] * inv_freq
    ang = jnp.concatenate([ang_r, ang_r, ang_c, ang_c], axis=-1)
    return jnp.cos(ang), jnp.sin(ang)


def _rot_matrix(r):
    R = np.zeros((r, r), np.float32)
    q = r // 4
    for s in range(2):
        for t in range(q):
            lo = s * (r // 2) + t
            hi = lo + q
            R[hi, lo] = -1.0
            R[lo, hi] = 1.0
    return R


def _place_tables(L, cos, sin, width, offsets):
    r = cos.shape[1]
    Rm = np.zeros((width, width), np.float32)
    R = _rot_matrix(r)
    cs, ss, pos = [], [], 0
    for o in list(offsets) + [width]:
        if o > pos:
            cs.append(jnp.ones((L, o - pos), F32))
            ss.append(jnp.zeros((L, o - pos), F32))
        if o < width:
            cs.append(cos)
            ss.append(sin)
            Rm[o:o + r, o:o + r] = R
        pos = o + r
    return jnp.concatenate(cs, axis=1), jnp.concatenate(ss, axis=1), jnp.asarray(Rm)


def _head_mean_matrix(width, stride, n):
    M = np.zeros((width, width), np.float32)
    for o in range(0, width, stride):
        M[o:o + n, o:o + n] = 1.0 / n
    return jnp.asarray(M)


def _pad_heads(w, n_heads, real, padded, axis):
    parts = jnp.split(w, n_heads, axis=axis)
    padw = [(0, 0)] * w.ndim
    padw[axis] = (0, padded - real)
    return jnp.concatenate([jnp.pad(p, padw) for p in parts], axis=axis)


def _row128(v):
    v = v.reshape(1, -1)
    return jnp.pad(v, ((0, 0), (0, LANE - v.shape[1])))


def _conv_w8(w, b):
    C = w.shape[1]
    rows = [w, jnp.zeros((1, C), F32) if b is None else b.reshape(1, C), jnp.zeros((4, C), F32)]
    return jnp.concatenate(rows, axis=0)


def _build_layer(W):
    w_in = W['w_in']
    o = 0
    cols = {}
    for name, n in [('a_cq', A_Q_LORA), ('a_ckv', A_KV_LORA), ('a_kr', A_ROPE), ('b_q', 256), ('b_k', 128),
                    ('b_v', 128), ('c_z', 256), ('c_xbc', 512), ('c_dt', 8), ('d_qkv', 768), ('d_z', 256),
                    ('d_b', 8), ('d_a', 8)]:
        cols[name] = w_in[:, o:o + n]
        o += n
    padc = lambda a, lo, width: jnp.pad(a, ((0, 0), (lo, width - lo - a.shape[1])))
    pieces = {
        'b_q': _pad_heads(cols['b_q'], 4, HD, LANE, 1), 'c_xbc': cols['c_xbc'], 'a_cq': padc(cols['a_cq'], 0, 256),
        'b_k': _pad_heads(cols['b_k'], 2, HD, LANE, 1), 'd_qkv': cols['d_qkv'],
        'b_v': _pad_heads(cols['b_v'], 2, HD, LANE, 1), 'c_z': cols['c_z'], 'd_z': cols['d_z'],
        'a_ckv': cols['a_ckv'], 'a_kr': padc(cols['a_kr'], A_NOPE, LANE), 'c_dt': padc(cols['c_dt'], 0, LANE),
        'd_b': padc(cols['d_b'], 0, LANE), 'd_a': padc(cols['d_a'], 0, LANE),
        'pad': jnp.zeros((D_MODEL, LANE), w_in.dtype)}
    out = {'w_in': jnp.concatenate([pieces[n] for n, _, _ in P_LAYOUT], axis=1)}
    out['a_q_norm'] = padc(W['a_q_norm'].reshape(1, -1), 0, 256)
    wuq = jnp.pad(W['a_w_uq'], ((0, 256 - A_Q_LORA), (0, 0)))
    out['a_w_uq'] = _pad_heads(wuq, 4, A_NOPE + A_ROPE, LANE, 1)
    out['a_kv_norm'] = W['a_kv_norm'].reshape(1, -1)
    ukv = W['a_w_ukv'].reshape(A_KV_LORA, HEADS, 2, HD)
    out['a_w_uk'] = _pad_heads(ukv[:, :, 0, :].reshape(A_KV_LORA, 256), 4, HD, LANE, 1)
    out['a_w_uv'] = _pad_heads(ukv[:, :, 1, :].reshape(A_KV_LORA, 256), 4, HD, LANE, 1)
    out['a_out_norm'] = _pad_heads(W['a_out_norm'].reshape(1, -1), 4, HD, LANE, 1)
    out['b_q_norm'] = _pad_heads(jnp.tile(W['b_q_norm'].reshape(1, -1), (1, 4)), 4, HD, LANE, 1)
    out['b_k_norm'] = _pad_heads(jnp.tile(W['b_k_norm'].reshape(1, -1), (1, 2)), 2, HD, LANE, 1)
    out['b_out_norm'] = _pad_heads(W['b_out_norm'].reshape(1, -1), 4, HD, LANE, 1)
    out['c_conv'] = _conv_w8(W['c_conv_w'], W['c_conv_b'])
    out['c_a_log'] = _row128(W['c_a_log'])
    out['c_dt_bias'] = _row128(W['c_dt_bias'])
    out['c_d_skip'] = jnp.repeat(W['c_d_skip'], HD).reshape(1, -1)
    out['c_out_norm'] = W['c_out_norm'].reshape(1, -1)
    out['d_conv'] = _conv_w8(W['d_conv_w'], None)
    out['d_a_log'] = _row128(W['d_a_log'])
    out['d_dt_bias'] = _row128(W['d_dt_bias'])
    out['d_out_norm'] = jnp.tile(W['d_out_norm'].reshape(1, -1), (1, 4))
    wo = W['w_out']
    out['w_out'] = jnp.concatenate([_pad_heads(wo[0:256], 4, HD, LANE, 0), _pad_heads(wo[256:512], 4, HD, LANE, 0),
                                    wo[512:1024]], axis=0)
    for n in ['pre_mix_norm', 'post_mix_norm', 'pre_ffn_norm', 'post_ffn_norm']:
        out[n] = W[n].reshape(1, -1)
    out['f_w_in'] = W['f_w_in']
    out['f_conv'] = _conv_w8(W['f_conv_w'], W['f_conv_b'])
    out['f_w_out'] = W['f_w_out']
    return out


def _fn_norm_in(a, p):
    return [_rms(a[0], p[0])]


def _fn_resid_norm2(a, p):
    x1 = a[0] + _rms(a[1], p[0])
    return [x1, _rms(x1, p[1])]


def _fn_resid_norm(a, p):
    return [a[0] + _rms(a[1], p[0])]


def _fn_a_prep(a, p):
    cq, ckv, kr, cosk, sink = a
    q_norm, w_uq, kv_norm, w_uk, w_uv, rq, rk = p
    cosq = jnp.concatenate([cosk] * HEADS, axis=1)
    sinq = jnp.concatenate([sink] * HEADS, axis=1)
    q = _nn(_rms(cq, q_norm, A_Q_LORA), w_uq)
    q = q * cosq + _nn_h3(q, rq) * sinq
    kvn = _rms(ckv, kv_norm)
    kr_r = kr * cosk + _nn_h3(kr, rk) * sink
    kk = _nn(kvn, w_uk) + jnp.concatenate([kr_r] * HEADS, axis=1)
    return [q, kk, _nn(kvn, w_uv)]


def _fn_b_prep(a, p):
    q, k, v, cos1, sin1 = a
    q_norm, k_norm, mq, mk, rq, rk = p
    cosq, sinq = (jnp.concatenate([t] * 4, axis=1) for t in (cos1, sin1))
    cosk, sink = (jnp.concatenate([t] * 2, axis=1) for t in (cos1, sin1))
    qn = q * lax.rsqrt(_nn_h3(q * q, mq) + EPS) * q_norm
    kn = k * lax.rsqrt(_nn_h3(k * k, mk) + EPS) * k_norm
    return [qn * cosq + _nn_h3(qn, rq) * sinq, kn * cosk + _nn_h3(kn, rk) * sink, v]


def _fn_mixer_post(a, p):
    oa, ob, yc0, yc1, xs, zc, od0, od1, zd = a
    a_norm, b_norm, dskip, c_norm, d_norm, m64 = p
    oc = _rms((yc0 + yc1 + xs * dskip) * _silu(zc), c_norm)
    od = od0 + od1
    odn = od * lax.rsqrt(_nn_h3(od * od, m64) + EPS) * d_norm * _silu(zd)
    return [jnp.concatenate([_rms(oa, a_norm, GROUP_W), _rms(ob, b_norm, GROUP_W), oc, odn], axis=1)]


def _fn_assemble(a, p):
    (dbq, dxbc, dcq, dbk, dqkv, dbv, dzc, dzd, dckv, dkr, ddt0, ddt1, db0, db1, da0, da1) = a
    return [jnp.concatenate([dbq, dxbc, dcq, dbk, dqkv, dbv, dzc, dzd, dckv, dkr, ddt0 + ddt1, db0 + db1,
                             da0 + da1, jnp.zeros_like(dckv)], axis=1)]


def _pspec(T, name):
    off, w = P_OFF[name]
    return _spec2(T, w, off // w)


def _layer_fwd(l, x, h, K, tabs, L, T, next_norm, side_a=None, late=None, side_b=None):
    n = f"l{l}_"
    sv = {'x': x, 'h': h}
    p = _mm(n + "in_proj", h, K['w_in'].astype(BF16), 'nn', F32, 1024, 1280, 1024)
    sv['p'] = p
    a_acts = [(p, _pspec(T, 'a_cq')), (p, _pspec(T, 'a_ckv')), (p, _pspec(T, 'a_kr')),
              (tabs['a_c'], _spec2(T, LANE)), (tabs['a_s'], _spec2(T, LANE))]
    a_pars = [K['a_q_norm'], K['a_w_uq'], K['a_kv_norm'], K['a_w_uk'], K['a_w_uv'], tabs['a_rq'], tabs['a_rk']]
    qa, ka, va = _tw_fwd(n + "a_prep", _fn_a_prep, a_acts, a_pars, [(512, BF16)] * 3, L, T)
    oa, lse_a, got_a = _flash_fwd(n + "a_attn", qa, ka, va, HEADS, 1, (A_NOPE + A_ROPE) ** -0.5, L, side_a)
    if late is not None:
        K = {**K, **late(got_a)}
    sv.update(a_acts=a_acts, a_pars=a_pars, qa=qa, ka=ka, va=va, oa=oa, lse_a=lse_a, K=K)
    b_acts = [(p, _pspec(T, 'b_q')), (p, _pspec(T, 'b_k')), (p, _pspec(T, 'b_v')),
              (tabs['b_c'], _spec2(T, LANE)), (tabs['b_s'], _spec2(T, LANE))]
    b_pars = [K['b_q_norm'], K['b_k_norm'], tabs['b_mq'], tabs['b_mk'], tabs['b_rq'], tabs['b_rk']]
    qb, kb, vb = _tw_fwd(n + "b_prep", _fn_b_prep, b_acts, b_pars, [(512, BF16), (256, BF16), (256, BF16)], L, T)
    ob, lse_b, sv['side'] = _flash_fwd(n + "b_attn", qb, kb, vb, HEADS, 2, HD ** -0.5, L, side_b)
    sv.update(b_acts=b_acts, b_pars=b_pars, qb=qb, kb=kb, vb=vb, ob=ob, lse_b=lse_b)
    xbc = _conv_fwd(n + "c_conv", p, P_OFF['c_xbc'][0], C_XBC, K['c_conv'], True, L, 512)
    c_seqs = [(xbc, C_XBC, 0, [(0, HD, 4), (256, HD, 2), (384, HD, 2)]),
              (p, LANE, P_OFF['c_dt'][0] // LANE, None)]
    c_rows = [K['c_a_log'], K['c_dt_bias']]
    yc0, yc1, sc0, sc1 = _scan_fwd(n + "c_ssd", _ssd_chunk, c_seqs, c_rows, C_CHUNK, L, C_PER_STEP)
    sv.update(xbc=xbc, c_seqs=c_seqs, c_rows=c_rows, sc=(sc0, sc1))
    qkv = _conv_fwd(n + "d_conv", p, P_OFF['d_qkv'][0], D_QKV, K['d_conv'], True, L, 768)
    d_seqs = [(qkv, D_QKV, 0, [(0, HD, 4), (256, HD, 4), (512, HD, 4)]),
              (p, LANE, P_OFF['d_b'][0] // LANE, None), (p, LANE, P_OFF['d_a'][0] // LANE, None)]
    d_rows = [K['d_a_log'], K['d_dt_bias']]
    od0, od1, sd0, sd1 = _scan_fwd(n + "d_delta", _delta_chunk, d_seqs, d_rows, D_CHUNK, L, D_PER_STEP,
                                   [(HEADS * D_CHUNK, HEADS * D_CHUNK)])
    sv.update(qkv=qkv, d_seqs=d_seqs, d_rows=d_rows, sd=(sd0, sd1))
    m_acts = [(oa, _spec2(T, 512)), (ob, _spec2(T, 512)), (yc0, _spec2(T, 256)), (yc1, _spec2(T, 256)),
              (xbc, _spec2(T, 256, 0)), (p, _pspec(T, 'c_z')), (od0, _spec2(T, 256)), (od1, _spec2(T, 256)),
              (p, _pspec(T, 'd_z'))]
    m_pars = [K['a_out_norm'], K['b_out_norm'], K['c_d_skip'], K['c_out_norm'], K['d_out_norm'], tabs['m64']]
    (o,) = _tw_fwd(n + "mixer_post", _fn_mixer_post, m_acts, m_pars, [(O_COLS, BF16)], L, T)
    f1 = _mm(n + "out_proj", o, K['w_out'].astype(BF16), 'nn', F32, 1024, 1024, 1536)
    r1_pars = [K['post_mix_norm'], K['pre_ffn_norm']]
    x1, h2 = _tw_fwd(n + "resid_mix", _fn_resid_norm2, [(x, _spec2(T, D_MODEL)), (f1, _spec2(T, D_MODEL))], r1_pars,
                     [(D_MODEL, F32), (D_MODEL, BF16)], L, T)
    sv.update(m_acts=m_acts, m_pars=m_pars, o=o, f1=f1, r1_pars=r1_pars, x1=x1, h2=h2)
    u = _mm(n + "ffn_in", h2, K['f_w_in'].astype(BF16), 'nn', F32, 1024, 1408, 1024)
    act = _ffn_gate_fwd(n + "ffn_gate", u, K['f_conv'], L)
    f2 = _mm(n + "ffn_out", act, K['f_w_out'].astype(BF16), 'nn', F32, 1024, 1024, 1408)
    sv.update(u=u, act=act, f2=f2)
    xf = [(x1, _spec2(T, D_MODEL)), (f2, _spec2(T, D_MODEL))]
    if next_norm is None:
        (x2,) = _tw_fwd(n + "resid_ffn", _fn_resid_norm, xf, [K['post_ffn_norm']], [(D_MODEL, F32)], L, T)
        hn = None
    else:
        x2, hn = _tw_fwd(n + "resid_ffn", _fn_resid_norm2, xf, [K['post_ffn_norm'], next_norm],
                         [(D_MODEL, F32), (D_MODEL, BF16)], L, T)
    return x2, hn, sv


def _layer_bwd(l, dx2, dhn, K, sv, tabs, L, T, next_norm, hosts=None):
    n = f"l{l}b_"
    dK = {}
    hosts = hosts or {}
    got = {}
    side = lambda name: hosts[name](dK, got) if name in hosts else None
    s2 = lambda w, cb=0: _spec2(T, w, cb)
    xf = [(sv['x1'], s2(D_MODEL)), (sv['f2'], s2(D_MODEL))]
    if next_norm is None:
        (dx1a, df2), (dK['post_ffn_norm'],) = _tw_bwd(n + "resid_ffn", _fn_resid_norm, xf, [K['post_ffn_norm']],
                                                      [(dx2, s2(D_MODEL))], L, T, [True, True], [True])
        dnext = None
    else:
        (dx1a, df2), (dK['post_ffn_norm'], dnext) = _tw_bwd(
            n + "resid_ffn", _fn_resid_norm2, xf, [K['post_ffn_norm'], next_norm],
            [(dx2, s2(D_MODEL)), (dhn, s2(D_MODEL))], L, T, [True, True], [True, True])
    dact = _mm(n + "ffn_out_dx", df2, K['f_w_out'].astype(BF16), 'nt', F32, 1024, 1408, 1024)
    dK['f_w_out'] = _mm(n + "ffn_out_dw", sv['act'], df2, 'tn', F32, 1408, 1024, 1024)
    du, dK['f_conv'] = _ffn_gate_bwd(n + "ffn_gate", sv['u'], K['f_conv'], dact, L)
    dh2 = _mm(n + "ffn_in_dx", du, K['f_w_in'].astype(BF16), 'nt', F32, 1024, 1024, 1408)
    dK['f_w_in'] = _mm(n + "ffn_in_dw", sv['h2'], du, 'tn', F32, 1024, 1408, 1024)
    (dxa, df1), (dK['post_mix_norm'], dK['pre_ffn_norm']) = _tw_bwd(
        n + "resid_mix", _fn_resid_norm2, [(sv['x'], s2(D_MODEL)), (sv['f1'], s2(D_MODEL))], sv['r1_pars'],
        [(dx1a, s2(D_MODEL)), (dh2, s2(D_MODEL))], L, T, [True, True], [True, True])
    do = _mm(n + "out_proj_dx", df1, K['w_out'].astype(BF16), 'nt', F32, 1024, 1536, 1024)
    dK['w_out'] = _mm(n + "out_proj_dw", sv['o'], df1, 'tn', F32, 1536, 1024, 1024)
    (doa, dob, dyc0, _, dxs_skip, dzc, dod0, _, dzd), mp = _tw_bwd(
        n + "mixer_post", _fn_mixer_post, sv['m_acts'], sv['m_pars'], [(do, s2(O_COLS))], L, T,
        [True] * 9, [True] * 5 + [False])
    dK['a_out_norm'], dK['b_out_norm'], dK['c_d_skip'], dK['c_out_norm'], dK['d_out_norm'] = mp
    (dqkv0, db0, da0), (dqkv1, db1, da1), (dK['d_a_log'], dK['d_dt_bias']), got['d_delta'] = _scan_bwd(
        n + "d_delta", _delta_chunk, sv['d_seqs'], sv['d_rows'], sv['sd'], dod0, None, D_CHUNK, L, D_PER_STEP,
        side('d_delta'))
    dqkv, dK['d_conv'] = _conv_bwd(n + "d_conv", sv['p'], P_OFF['d_qkv'][0], D_QKV, K['d_conv'], True,
                                   [(dqkv0, None), (dqkv1, None)], L, 768)
    (dxbc0, ddt0), (dxbc1, ddt1), (dK['c_a_log'], dK['c_dt_bias']), _ = _scan_bwd(
        n + "c_ssd", _ssd_chunk, sv['c_seqs'], sv['c_rows'], sv['sc'], dyc0, dxs_skip, C_CHUNK, L, C_PER_STEP)
    dxbc, dK['c_conv'] = _conv_bwd(n + "c_conv", sv['p'], P_OFF['c_xbc'][0], C_XBC, K['c_conv'], True,
                                   [(dxbc0, None), (dxbc1, None)], L, 512)
    dqb, dkb, dvb, got['b_attn'] = _flash_bwd(n + "b_attn", sv['qb'], sv['kb'], sv['vb'], sv['ob'], sv['lse_b'],
                                              dob, HEADS, 2, HD ** -0.5, L, side('b_attn'))
    (dbq, dbk, dbv), (dK['b_q_norm'], dK['b_k_norm']) = _tw_bwd(
        n + "b_prep", _fn_b_prep, sv['b_acts'], sv['b_pars'], [(dqb, s2(512)), (dkb, s2(256)), (dvb, s2(256))],
        L, T, [True] * 3 + [False] * 2, [True, True] + [False] * 4)
    dqa, dka, dva, got['a_attn'] = _flash_bwd(n + "a_attn", sv['qa'], sv['ka'], sv['va'], sv['oa'], sv['lse_a'],
                                              doa, HEADS, 1, (A_NOPE + A_ROPE) ** -0.5, L, side('a_attn'))
    (dcq, dckv, dkr), ap = _tw_bwd(
        n + "a_prep", _fn_a_prep, sv['a_acts'], sv['a_pars'], [(dqa, s2(512)), (dka, s2(512)), (dva, s2(512))],
        L, T, [True] * 3 + [False] * 2, [True] * 5 + [False] * 2)
    dK['a_q_norm'], dK['a_w_uq'], dK['a_kv_norm'], dK['a_w_uk'], dK['a_w_uv'] = ap
    pieces = [(dbq, s2(512)), (dxbc, s2(512)), (dcq, s2(256)), (dbk, s2(256)), (dqkv, s2(768)), (dbv, s2(256)),
              (dzc, s2(256)), (dzd, s2(256)), (dckv, s2(LANE)), (dkr, s2(LANE)),
              (ddt0, s2(LANE)), (ddt1, s2(LANE)), (db0, s2(LANE)), (db1, s2(LANE)), (da0, s2(LANE)),
              (da1, s2(LANE))]
    (dp,) = _tw_fwd(n + "assemble_dp", _fn_assemble, pieces, [], [(P_COLS, BF16)], L, T)
    dh = _mm(n + "in_proj_dx", dp, K['w_in'].astype(BF16), 'nt', F32, 1024, 1024, 1280)
    dK['w_in'] = _mm(n + "in_proj_dw", sv['h'], dp, 'tn', F32, 1024, 1280, 1024)
    return dxa, dh, dK, dnext, got


def _tables(L):
    ca, sa = _rope_angles(L, A_ROPE)
    cb, sb = _rope_angles(L, HD)
    t = {}
    t['a_c'], t['a_s'], t['a_rk'] = _place_tables(L, ca, sa, LANE, [A_NOPE])
    t['b_c'], t['b_s'], _ = _place_tables(L, cb, sb, LANE, [0])
    t['a_rq'] = _place_tables(8, ca[:8], sa[:8], 512, [LANE * h + A_NOPE for h in range(4)])[2]
    t['b_rq'] = _place_tables(8, cb[:8], sb[:8], 512, [LANE * h for h in range(4)])[2]
    t['b_rk'] = _place_tables(8, cb[:8], sb[:8], 256, [LANE * h for h in range(2)])[2]
    t['b_mq'] = _head_mean_matrix(512, LANE, HD)
    t['b_mk'] = _head_mean_matrix(256, LANE, HD)
    t['m64'] = _head_mean_matrix(256, HD, HD)
    return t


def kernel(x, pre_mix_norm, w_in, a_q_norm, a_w_uq, a_kv_norm, a_w_ukv, a_out_norm, b_q_norm, b_k_norm, b_out_norm, c_conv_w, c_conv_b, c_a_log, c_dt_bias, c_d_skip, c_out_norm, d_conv_w, d_a_log, d_dt_bias, d_out_norm, w_out, post_mix_norm, pre_ffn_norm, f_w_in, f_conv_w, f_conv_b, f_w_out, post_ffn_norm, loss_target, m_pre_mix_norm, m_w_in, m_a_q_norm, m_a_w_uq, m_a_kv_norm, m_a_w_ukv, m_a_out_norm, m_b_q_norm, m_b_k_norm, m_b_out_norm, m_c_conv_w, m_c_conv_b, m_c_a_log, m_c_dt_bias, m_c_d_skip, m_c_out_norm, m_d_conv_w, m_d_a_log, m_d_dt_bias, m_d_out_norm, m_w_out, m_post_mix_norm, m_pre_ffn_norm, m_f_w_in, m_f_conv_w, m_f_conv_b, m_f_w_out, m_post_ffn_norm, v_pre_mix_norm, v_w_in, v_a_q_norm, v_a_w_uq, v_a_kv_norm, v_a_w_ukv, v_a_out_norm, v_b_q_norm, v_b_k_norm, v_b_out_norm, v_c_conv_w, v_c_conv_b, v_c_a_log, v_c_dt_bias, v_c_d_skip, v_c_out_norm, v_d_conv_w, v_d_a_log, v_d_dt_bias, v_d_out_norm, v_w_out, v_post_mix_norm, v_pre_ffn_norm, v_f_w_in, v_f_conv_w, v_f_conv_b, v_f_w_out, v_post_ffn_norm):
    loc = locals()
    Wl = {n: loc[n] for n in WEIGHTS}
    Ml = {n: loc['m_' + n] for n in WEIGHTS}
    Vl = {n: loc['v_' + n] for n in WEIGHTS}
    L = x.shape[1]
    T = min(512, L)
    x0 = x.reshape(L, D_MODEL)
    tgt = loss_target.reshape(L, D_MODEL)

    first = ['w_in', 'a_w_uq', 'a_w_ukv', 'c_conv_w', 'd_conv_w']
    later = [n for n in SHARDED if n not in first]
    late_keys = ['w_out', 'f_w_in', 'f_conv', 'f_w_out']

    def shards(l, names):
        return [Wl[n][l].astype(BF16) if n in MXU_WEIGHTS else Wl[n][l] for n in names]

    def layer_weights(l, names, gathered):
        W = {n: Wl[n][l] for n in SMALL}
        for n in SHARDED:
            W[n] = jnp.zeros(layer_shape(n), BF16 if n in MXU_WEIGHTS else F32)
        for n, g in zip(names, gathered):
            W[n] = jnp.concatenate([g[j] for j in range(4)], axis=SHARD_AXIS[n] - 1)
        return W

    def chip_blocks(g, n):
        return jnp.stack(jnp.split(g, 4, axis=SHARD_AXIS[n] - 1))

    tabs = _tables(L)
    norm_in = [Wl['pre_mix_norm'][l].reshape(1, -1) for l in range(DEPTH)]
    def layer_shape(n):
        s = list(Wl[n].shape[1:])
        if n in SHARD_AXIS:
            s[SHARD_AXIS[n] - 1] *= 4
        return tuple(s)

    unbuild = jax.vjp(_build_layer, {n: jnp.zeros(layer_shape(n), F32) for n in WEIGHTS})[1]

    (h,) = _tw_fwd("l0_norm_in", _fn_norm_in, [(x0, _spec2(T, D_MODEL))], [norm_in[0]], [(D_MODEL, BF16)], L, T)
    gathered = _gather_copies(shards(0, first)).call("gather_l0")
    xs, saves, Ks = x0, [], []
    for l in range(DEPTH):
        last = l + 1 == DEPTH
        nxt = None if last else _gather_copies(shards(l + 1, SHARDED))
        if l == 0:
            def late(got):
                K_late = _build_layer(layer_weights(0, later, got))
                return {k: K_late[k] for k in late_keys}

            xs, h, sv = _layer_fwd(l, xs, h, _build_layer(layer_weights(0, first, gathered)), tabs, L, T,
                                   None if last else norm_in[l + 1], _gather_copies(shards(0, later)), late, nxt)
        else:
            xs, h, sv = _layer_fwd(l, xs, h, _build_layer(layer_weights(l, SHARDED, gathered)), tabs, L, T,
                                   None if last else norm_in[l + 1], None, None, nxt)
        gathered = sv['side']
        Ks.append(sv['K'])
        saves.append(sv)
    dy, loss_acc = _loss_call(xs, tgt, L)
    loss = lax.psum(loss_acc[0, 0], ("x", "y", "c"))

    ffn = ['f_w_in', 'f_conv_w', 'f_w_out', 'w_out']
    rest = [n for n in SHARDED if n not in ffn]

    def ffn_side(dK):
        only_w_out = {k: (dK[k] if k == 'w_out' else jnp.zeros(v.shape, F32)) for k, v in Ks[0].items()}
        g = {'f_w_in': dK['f_w_in'], 'f_conv_w': dK['f_conv'][0:3], 'f_w_out': dK['f_w_out'],
             'w_out': unbuild(only_w_out)[0]['w_out']}
        return _scatter_copies([chip_blocks(g[n], n) for n in ffn], None)

    def rest_blocks(dK):
        full = dict(dK)
        full.setdefault('pre_mix_norm', jnp.zeros((1, D_MODEL), F32))
        (g,) = unbuild(full)
        return [chip_blocks(g[n], n) for n in rest]

    def chip_sums(l, names, recvs, dtype=F32):
        return [_sum_slots(f"sum_{n}_{l}", r.reshape(4, -1, r.shape[-1]), dtype) for n, r in zip(names, recvs)]

    grads = [None] * DEPTH
    pairs = {}
    dx, dhn = dy, None
    for l in reversed(range(DEPTH)):
        last = l + 1 == DEPTH

        def host_scatter(dK, got, up=None if last else grads[l + 1]):
            sets = [ffn_side(dK)] + ([] if up is None else [_scatter_copies(rest_blocks(up), None)])
            return _merge_copies(sets)

        def host_swap(dK, got, l=l, last=last):
            r = got['d_delta']
            parts = chip_sums(l, ffn, r[:len(ffn)]) + ([] if last else chip_sums(l + 1, rest, r[len(ffn):]))
            return _swap_copies(parts)

        dxa, dh, dK, dnext, got = _layer_bwd(l, dx, dhn, Ks[l], saves[l], tabs, L, T,
                                             None if last else norm_in[l + 1],
                                             {'d_delta': host_scatter, 'b_attn': host_swap})
        pairs.update({(l, n): p for n, p in zip(ffn, got['b_attn'])})
        if not last:
            pairs.update({(l + 1, n): p for n, p in zip(rest, got['b_attn'][len(ffn):])})
            grads[l + 1]['pre_mix_norm'] = dnext
        grads[l] = dK
        dx, dhn = dxa, dh
    (dx_in,), (grads[0]['pre_mix_norm'],) = _tw_bwd(
        "l0b_norm_in", _fn_norm_in, [(x0, _spec2(T, D_MODEL))], [norm_in[0]], [(dhn, _spec2(T, D_MODEL))], L, T,
        [True], [True], addto={0: (dx, _spec2(T, D_MODEL))})
    small_shapes = [Wl[n].shape for n in SMALL]
    gfull = [unbuild(grads[l])[0] for l in range(DEPTH)]
    gs = _pack([jnp.stack([gfull[l][n] for l in range(DEPTH)]) for n in SMALL], LANE, 8)
    *got0, recv_small = _scatter_copies([b.astype(BF16) for b in rest_blocks(grads[0])], gs).call("scatter_last")
    pairs.update({(0, n): p for n, p in zip(rest, _swap_copies(chip_sums(0, rest, got0, BF16)).call("swap_last"))})

    kinds = ['grad', 'delta', 'new_m', 'new_v']
    res = {}
    for n in SHARDED:
        upd = _adamw_call("adamw_" + n, [pairs[l, n] for l in range(DEPTH)], Wl[n], Ml[n], Vl[n])
        for kind, a in zip(kinds, upd):
            res[kind, n] = a
    small = _adamw_call("adamw_small", [recv_small], *[_pack([W_[n] for n in SMALL], LANE, 8)[None]
                                                       for W_ in (Wl, Ml, Vl)])
    for kind, s in zip(kinds, small):
        for n, a in zip(SMALL, _unpack(s, small_shapes)):
            res[kind, n] = a
    outs = [loss, dx_in.reshape(x.shape)]
    for kind in ['grad', 'delta', 'new_m', 'new_v']:
        outs += [res[kind, n] for n in WEIGHTS]
    return tuple(outs)
```

```python
import functools
import math

import numpy as np
import jax
import jax.numpy as jnp
from jax import lax
from jax.experimental import pallas as pl
from jax.experimental.pallas import tpu as pltpu

F32 = jnp.float32
BF16 = jnp.bfloat16
MESH = pl.DeviceIdType.MESH
VMEM_LIMIT = 48 * 1024 * 1024
LANE = 128

D_MODEL = 1024
DEPTH = 2
GRID_W = 64
ROPE_BASE = 10000.0
EPS = 1e-6
GROUP_W = 256
HEADS = 4
HD = 64
A_NOPE, A_ROPE, A_Q_LORA, A_KV_LORA = 64, 32, 192, 128
A_COLS = A_Q_LORA + A_KV_LORA + A_ROPE
B_COLS = 512
C_XBC = 512
C_COLS = GROUP_W + C_XBC + 8
D_QKV = 768
D_COLS = D_QKV + GROUP_W + 16
IN_COLS = A_COLS + B_COLS + C_COLS + D_COLS
C_CHUNK = 128
D_CHUNK = 64
C_PER_STEP = 1
D_PER_STEP = 4
D_FF = 2816
ADAM_LR, ADAM_B1, ADAM_B2, ADAM_EPS, ADAM_WD, ADAM_STEP = 0.001, 0.9, 0.999, 1e-08, 0.01, 10

WEIGHTS = ['pre_mix_norm', 'w_in', 'a_q_norm', 'a_w_uq', 'a_kv_norm', 'a_w_ukv', 'a_out_norm', 'b_q_norm',
           'b_k_norm', 'b_out_norm', 'c_conv_w', 'c_conv_b', 'c_a_log', 'c_dt_bias', 'c_d_skip', 'c_out_norm',
           'd_conv_w', 'd_a_log', 'd_dt_bias', 'd_out_norm', 'w_out', 'post_mix_norm', 'pre_ffn_norm', 'f_w_in',
           'f_conv_w', 'f_conv_b', 'f_w_out', 'post_ffn_norm']
SHARD_AXIS = {'w_in': 2, 'a_w_uq': 2, 'a_w_ukv': 2, 'c_conv_w': 2, 'd_conv_w': 2, 'w_out': 1, 'f_w_in': 2,
              'f_conv_w': 2, 'f_w_out': 1}
SHARDED = [n for n in WEIGHTS if n in SHARD_AXIS]
SMALL = [n for n in WEIGHTS if n not in SHARD_AXIS]
MXU_WEIGHTS = ('w_in', 'a_w_uq', 'a_w_ukv', 'w_out', 'f_w_in', 'f_w_out')

P_LAYOUT = [('b_q', 0, 512), ('c_xbc', 512, 512), ('a_cq', 1024, 256), ('b_k', 1280, 256), ('d_qkv', 1536, 768),
            ('b_v', 2304, 256), ('c_z', 2560, 256), ('d_z', 2816, 256), ('a_ckv', 3072, 128), ('a_kr', 3200, 128),
            ('c_dt', 3328, 128), ('d_b', 3456, 128), ('d_a', 3584, 128), ('pad', 3712, 128)]
P_OFF = {n: (o, w) for n, o, w in P_LAYOUT}
P_COLS = 3840
O_COLS = 1536


def _cparams(sem):
    return pltpu.CompilerParams(dimension_semantics=sem, vmem_limit_bytes=VMEM_LIMIT)


def _tile(n, target):
    best = None
    for d in range(LANE, min(n, target) + 1, LANE):
        if n % d == 0:
            best = d
    return best if best is not None else n


_NN = ((1,), (0,))
_NT = ((1,), (1,))
_TN = ((0,), (0,))


def _raw_dot(a, b, dims, hi):
    if hi:
        prec = lax.Precision.HIGH if hi == 'high' else lax.Precision.HIGHEST
        return lax.dot_general(a, b, (dims, ((), ())), precision=prec, preferred_element_type=F32)
    return lax.dot_general(a.astype(BF16), b.astype(BF16), (dims, ((), ())), preferred_element_type=F32)


def _make_dots(hi):
    @jax.custom_vjp
    def nn(a, b):
        return _raw_dot(a, b, _NN, hi)

    @jax.custom_vjp
    def nt(a, b):
        return _raw_dot(a, b, _NT, hi)

    @jax.custom_vjp
    def tn(a, b):
        return _raw_dot(a, b, _TN, hi)

    nn.defvjp(lambda a, b: (nn(a, b), (a, b)), lambda r, g: (nt(g, r[1]), tn(r[0], g)))
    nt.defvjp(lambda a, b: (nt(a, b), (a, b)), lambda r, g: (nn(g, r[1]), tn(g, r[0])))
    tn.defvjp(lambda a, b: (tn(a, b), (a, b)), lambda r, g: (nt(r[1], g), nn(r[0], g)))
    return nn, nt, tn


_nn, _nt, _tn = _make_dots(False)
_nn_hi, _nt_hi, _tn_hi = _make_dots(True)
_nn_h3, _nt_h3, _tn_h3 = _make_dots('high')


def _sigmoid(x):
    return 1.0 / (1.0 + jnp.exp(-x))


def _silu(x):
    return x * _sigmoid(x)


def _softplus(x):
    return jnp.maximum(x, 0.0) + jnp.log(1.0 + jnp.exp(-jnp.abs(x)))


def _rms(x, w, n=None):
    n = x.shape[-1] if n is None else n
    ms = jnp.sum(x * x, axis=-1, keepdims=True) * (1.0 / n)
    return x * lax.rsqrt(ms + EPS) * w


def _spec2(T, w, cb=0):
    return pl.BlockSpec((T, w), lambda i: (i, cb))


def _full_spec(a):
    nd = a.ndim
    return pl.BlockSpec(a.shape, lambda i: (0,) * nd)


def _tw_fwd(name, fn, acts, params, outs, L, T):
    na, npar = len(acts), len(params)

    def kern(*refs):
        a = [r[...].astype(F32) for r in refs[:na]]
        p = [r[...].astype(F32) for r in refs[na:na + npar]]
        res = fn(a, p)
        for r, o in zip(refs[na + npar:], res):
            r[...] = o.astype(r.dtype)

    return pl.pallas_call(
        kern, name=name, grid=(L // T,),
        in_specs=[s for _, s in acts] + [_full_spec(p) for p in params],
        out_specs=[_spec2(T, w) for w, _ in outs],
        out_shape=[jax.ShapeDtypeStruct((L, w), dt) for w, dt in outs],
        compiler_params=_cparams(("arbitrary",)),
    )(*[a for a, _ in acts], *params)


def _tw_bwd(name, fn, acts, params, douts, L, T, act_grad, par_grad, addto=None):
    na, npar, nd = len(acts), len(params), len(douts)
    addto = addto or {}
    add_keys = sorted(addto)
    ga = [k for k in range(na) if act_grad[k]]
    gp = [k for k in range(npar) if par_grad[k]]

    def kern(*refs):
        i = pl.program_id(0)
        a = [r[...].astype(F32) for r in refs[:na]]
        p = [r[...].astype(F32) for r in refs[na:na + npar]]
        g = [r[...].astype(F32) for r in refs[na + npar:na + npar + nd]]
        pos = na + npar + nd
        adds = [r[...].astype(F32) for r in refs[pos:pos + len(add_keys)]]
        pos += len(add_keys)
        da_refs = refs[pos:pos + len(ga)]
        dp_refs = refs[pos + len(ga):]

        def f(ad, pd):
            af, pf = list(a), list(p)
            for k, v in zip(ga, ad):
                af[k] = v
            for k, v in zip(gp, pd):
                pf[k] = v
            return fn(af, pf)

        _, vjp = jax.vjp(f, [a[k] for k in ga], [p[k] for k in gp])
        dad, dpd = vjp(list(g))
        for n, (r, d) in enumerate(zip(da_refs, dad)):
            if n in addto:
                d = d + adds[add_keys.index(n)]
            r[...] = d.astype(r.dtype)

        @pl.when(i == 0)
        def _():
            for r in dp_refs:
                r[...] = jnp.zeros(r.shape, F32)

        for r, d in zip(dp_refs, dpd):
            r[...] += d

    def width(spec):
        return spec.block_shape[-1]

    res = pl.pallas_call(
        kern, name=name, grid=(L // T,),
        in_specs=[s for _, s in acts] + [_full_spec(p) for p in params] + [s for _, s in douts]
        + [addto[k][1] for k in add_keys],
        out_specs=[_spec2(T, width(acts[k][1])) for k in ga] + [_full_spec(params[k]) for k in gp],
        out_shape=[jax.ShapeDtypeStruct((L, width(acts[k][1])), F32) for k in ga]
        + [jax.ShapeDtypeStruct(params[k].shape, F32) for k in gp],
        compiler_params=_cparams(("arbitrary",)),
    )(*[a for a, _ in acts], *params, *[a for a, _ in douts], *[addto[k][0] for k in add_keys])
    return list(res[:len(ga)]), list(res[len(ga):])


def _mm(name, a, b, mode, out_dtype, tm, tn, tk):
    halves_a = a.shape[-1] if (a.ndim == 3 and mode == 'nt') else None
    halves_b = b.shape[-1] if (b.ndim == 3 and mode == 'tn') else None
    if mode == 'nn':
        (M, K), N = a.shape, b.shape[1]
    elif mode == 'nt':
        M, K, N = a.shape[-2], (2 * halves_a if halves_a else a.shape[1]), b.shape[0]
    else:
        (K, M), N = a.shape, (2 * halves_b if halves_b else b.shape[1])
    tm = _tile(M, tm)
    tn = _tile(halves_b or N, tn)
    tk = _tile(halves_a or K, tk)
    nk = K // tk
    if mode == 'nn':
        a_spec = pl.BlockSpec((tm, tk), lambda i, j, k: (i, k))
        b_spec = pl.BlockSpec((tk, tn), lambda i, j, k: (k, j))
        dims = _NN
    elif mode == 'nt':
        a_spec = pl.BlockSpec((tm, tk), lambda i, j, k: (i, k))
        if halves_a:
            per = halves_a // tk
            a_spec = pl.BlockSpec((None, tm, tk), lambda i, j, k: (k // per, i, k % per))
        b_spec = pl.BlockSpec((tn, tk), lambda i, j, k: (j, k))
        dims = _NT
    else:
        a_spec = pl.BlockSpec((tk, tm), lambda i, j, k: (k, i))
        b_spec = pl.BlockSpec((tk, tn), lambda i, j, k: (k, j))
        if halves_b:
            per = halves_b // tn
            b_spec = pl.BlockSpec((None, tk, tn), lambda i, j, k: (j // per, k, j % per))
        dims = _TN

    def kern(a_ref, b_ref, o_ref, acc):
        k = pl.program_id(2)

        @pl.when(k == 0)
        def _():
            acc[...] = jnp.zeros(acc.shape, F32)

        acc[...] += lax.dot_general(a_ref[...].astype(BF16), b_ref[...].astype(BF16), (dims, ((), ())),
                                    preferred_element_type=F32)

        @pl.when(k == nk - 1)
        def _():
            o_ref[...] = acc[...].astype(o_ref.dtype)

    return pl.pallas_call(
        kern, name=name, grid=(M // tm, N // tn, nk),
        in_specs=[a_spec, b_spec],
        out_specs=pl.BlockSpec((tm, tn), lambda i, j, k: (i, j)),
        out_shape=jax.ShapeDtypeStruct((M, N), out_dtype),
        scratch_shapes=[pltpu.VMEM((tm, tn), F32)],
        compiler_params=_cparams(("arbitrary", "arbitrary", "arbitrary")),
    )(a, b)


def _host(kern, n_in, n_out, side, grid, n_scratch=0):
    if side is None:
        return kern, [], [], [], [], []
    ni, no = len(side.ins), len(side.out_shapes)

    def hosted(*refs):
        ins, s_in = refs[:n_in], refs[n_in:n_in + ni]
        pos = n_in + ni
        outs, s_out = refs[pos:pos + n_out], refs[pos + n_out:pos + n_out + no]
        pos += n_out + no
        own, sems = refs[pos:pos + n_scratch], refs[pos + n_scratch:]
        ids = [pl.program_id(d) for d in range(len(grid))]
        first = functools.reduce(lambda a, b: a & b, [i == 0 for i in ids])
        last = functools.reduce(lambda a, b: a & b, [i == g - 1 for i, g in zip(ids, grid)])

        @pl.when(first)
        def _():
            side.start(s_in, s_out, sems)

        kern(*ins, *outs, *own)

        @pl.when(last)
        def _():
            side.finish(s_in, s_out, sems)

    return hosted, [_ANY] * ni, [_ANY] * no, side.out_shapes, side.scratch(), side.ins


def _flash_fwd(name, q, k, v, H, rep, scale, L, side=None):
    tq = min(512, L)
    nq = L // tq
    KC = min(2048, L)
    nkc = L // KC
    log2e = 1.0 / math.log(2.0)

    def kern(q_ref, k_ref, v_ref, o_ref, lse_ref):
        qb = q_ref[...]
        m = jnp.full((tq, 1), -1e30, F32)
        l = jnp.zeros((tq, 1), F32)
        acc = jnp.zeros((tq, LANE), F32)
        for c in range(nkc):
            kb = k_ref[c * KC:(c + 1) * KC, :]
            vb = v_ref[c * KC:(c + 1) * KC, :]
            s = lax.dot_general(qb, kb, (_NT, ((), ())), preferred_element_type=F32) * (scale * log2e)
            mn = jnp.maximum(m, jnp.max(s, axis=-1, keepdims=True))
            al = jnp.exp2(m - mn)
            p = jnp.exp2(s - mn)
            l = al * l + jnp.sum(p, axis=-1, keepdims=True)
            acc = al * acc + lax.dot_general(p.astype(BF16), vb, (_NN, ((), ())), preferred_element_type=F32)
            m = mn
        o_ref[...] = acc / l
        lse_ref[...] = m * math.log(2.0) + jnp.log(l)

    kern, s_in, s_out, s_shapes, s_scratch, s_args = _host(kern, 3, 2, side, (H, nq))
    res = pl.pallas_call(
        kern, name=name, grid=(H, nq),
        in_specs=[pl.BlockSpec((tq, LANE), lambda h, i: (i, h)),
                  pl.BlockSpec((L, LANE), lambda h, i: (0, h // rep)),
                  pl.BlockSpec((L, LANE), lambda h, i: (0, h // rep))] + s_in,
        out_specs=[pl.BlockSpec((tq, LANE), lambda h, i: (i, h)),
                   pl.BlockSpec((tq, 1), lambda h, i: (h * nq + i, 0))] + s_out,
        out_shape=[jax.ShapeDtypeStruct((L, H * LANE), F32), jax.ShapeDtypeStruct((H * L, 1), F32)] + s_shapes,
        scratch_shapes=s_scratch,
        compiler_params=_cparams(("arbitrary", "arbitrary")),
    )(q, k, v, *s_args)
    return res[0], res[1], list(res[2:])


def _flash_bwd(name, q, k, v, o, lse, do, H, rep, scale, L, side=None):
    tq = min(512, L)
    nq = L // tq
    KC = min(1024, L)
    nkc = L // KC
    Hkv = H // rep

    def kern(q_ref, k_ref, v_ref, o_ref, lse_ref, do_ref, dq_ref, dk_ref, dv_ref):
        h = pl.program_id(0)
        i = pl.program_id(1)

        @pl.when((i == 0) & (h % rep == 0))
        def _():
            dk_ref[...] = jnp.zeros(dk_ref.shape, F32)
            dv_ref[...] = jnp.zeros(dv_ref.shape, F32)

        qb = q_ref[...]
        do = do_ref[...]
        dob = do.astype(BF16)
        delta = jnp.sum(do * o_ref[...], axis=-1, keepdims=True)
        lse = lse_ref[...]
        dq = jnp.zeros((tq, LANE), F32)
        for c in range(nkc):
            sl = slice(c * KC, (c + 1) * KC)
            kb = k_ref[sl, :]
            vb = v_ref[sl, :]
            s = lax.dot_general(qb, kb, (_NT, ((), ())), preferred_element_type=F32) * scale
            p = jnp.exp(s - lse)
            dp = lax.dot_general(dob, vb, (_NT, ((), ())), preferred_element_type=F32)
            ds = (p * (dp - delta) * scale).astype(BF16)
            dq = dq + lax.dot_general(ds, kb, (_NN, ((), ())), preferred_element_type=F32)
            dk_ref[sl, :] += lax.dot_general(ds, qb, (_TN, ((), ())), preferred_element_type=F32)
            dv_ref[sl, :] += lax.dot_general(p.astype(BF16), dob, (_TN, ((), ())), preferred_element_type=F32)
        dq_ref[...] = dq

    kern, s_in, s_out, s_shapes, s_scratch, s_args = _host(kern, 6, 3, side, (H, nq))
    res = pl.pallas_call(
        kern, name=name, grid=(H, nq),
        in_specs=[pl.BlockSpec((tq, LANE), lambda h, i: (i, h)),
                  pl.BlockSpec((L, LANE), lambda h, i: (0, h // rep)),
                  pl.BlockSpec((L, LANE), lambda h, i: (0, h // rep)),
                  pl.BlockSpec((tq, LANE), lambda h, i: (i, h)),
                  pl.BlockSpec((tq, 1), lambda h, i: (h * nq + i, 0)),
                  pl.BlockSpec((tq, LANE), lambda h, i: (i, h))] + s_in,
        out_specs=[pl.BlockSpec((tq, LANE), lambda h, i: (i, h)),
                   pl.BlockSpec((L, LANE), lambda h, i: (0, h // rep)),
                   pl.BlockSpec((L, LANE), lambda h, i: (0, h // rep))] + s_out,
        out_shape=[jax.ShapeDtypeStruct((L, H * LANE), F32), jax.ShapeDtypeStruct((L, Hkv * LANE), F32),
                   jax.ShapeDtypeStruct((L, Hkv * LANE), F32)] + s_shapes,
        scratch_shapes=s_scratch,
        compiler_params=_cparams(("arbitrary", "arbitrary")),
    )(q, k, v, o, lse, do, *s_args)
    return res[0], res[1], res[2], list(res[3:])


def _shift_dn(x, first_row):
    row = lax.broadcasted_iota(jnp.int32, x.shape, 0)
    return jnp.where(row == 0, first_row, pltpu.roll(x, 1, 0))


def _shift_up(x, last_row):
    n = x.shape[0]
    row = lax.broadcasted_iota(jnp.int32, x.shape, 0)
    return jnp.where(row == n - 1, last_row, pltpu.roll(x, n - 1, 0))


def _halo_specs(ndim, lead, T, tc, cb0, L):
    r8 = T // 8
    last8 = L // 8 - 1
    if ndim == 2:
        return [pl.BlockSpec((T, tc), lambda j, i: (i, cb0 + j)),
                pl.BlockSpec((8, tc), lambda j, i: (jnp.maximum(i * r8 - 1, 0), cb0 + j)),
                pl.BlockSpec((8, tc), lambda j, i: (jnp.minimum((i + 1) * r8, last8), cb0 + j))]
    return [pl.BlockSpec((None, T, tc), lambda j, i: (lead, i, cb0 + j)),
            pl.BlockSpec((None, 8, tc), lambda j, i: (lead, jnp.maximum(i * r8 - 1, 0), cb0 + j)),
            pl.BlockSpec((None, 8, tc), lambda j, i: (lead, jnp.minimum((i + 1) * r8, last8), cb0 + j))]


def _conv_rows(x_ref, xp_ref, xn_ref, w, first, last):
    x = x_ref[...]
    T = x.shape[0]
    w0, w1, w2, b = w[0:1], w[1:2], w[2:3], w[3:4]
    pr = jnp.where(first, 0.0, xp_ref[7:8, :])
    pr2 = jnp.where(first, 0.0, xp_ref[6:7, :])
    nr = jnp.where(last, 0.0, xn_ref[0:1, :])
    nr2 = jnp.where(last, 0.0, xn_ref[1:2, :])
    xm1 = _shift_dn(x, pr)
    xp1 = _shift_up(x, nr)
    pre = xm1 * w0 + x * w1 + xp1 * w2 + b
    pre_m1 = pr2 * w0 + pr * w1 + x[0:1] * w2 + b
    pre_T = x[T - 1:T] * w0 + nr * w1 + nr2 * w2 + b
    return x, xm1, xp1, pre, pre_m1, pre_T


def _conv_grads(dpre, dpre_m1, dpre_T, x, xm1, xp1, w):
    dx = _shift_up(dpre, dpre_T) * w[0:1] + dpre * w[1:2] + _shift_dn(dpre, dpre_m1) * w[2:3]
    row = lax.broadcasted_iota(jnp.int32, (8, x.shape[1]), 0)
    dw = (jnp.where(row == 0, jnp.sum(dpre * xm1, axis=0, keepdims=True), 0.0)
          + jnp.where(row == 1, jnp.sum(dpre * x, axis=0, keepdims=True), 0.0)
          + jnp.where(row == 2, jnp.sum(dpre * xp1, axis=0, keepdims=True), 0.0)
          + jnp.where(row == 3, jnp.sum(dpre, axis=0, keepdims=True), 0.0))
    return dx, dw


def _conv_fwd(name, x, col0, C, w8, act, L, tc):
    T = min(256, L)
    nt = L // T
    cb0 = col0 // tc

    def kern(x_ref, xp_ref, xn_ref, w_ref, o_ref):
        i = pl.program_id(1)
        x = x_ref[...]
        w = w_ref[...]
        pr = jnp.where(i == 0, 0.0, xp_ref[7:8, :])
        nr = jnp.where(i == nt - 1, 0.0, xn_ref[0:1, :])
        pre = _shift_dn(x, pr) * w[0:1] + x * w[1:2] + _shift_up(x, nr) * w[2:3] + w[3:4]
        o_ref[...] = _silu(pre) if act else pre

    return pl.pallas_call(
        kern, name=name, grid=(C // tc, nt),
        in_specs=_halo_specs(2, None, T, tc, cb0, L) + [pl.BlockSpec((8, tc), lambda j, i: (0, j))],
        out_specs=pl.BlockSpec((T, tc), lambda j, i: (i, j)),
        out_shape=jax.ShapeDtypeStruct((L, C), F32),
        compiler_params=_cparams(("arbitrary", "arbitrary")),
    )(x, x, x, w8)


def _conv_bwd(name, x, col0, C, w8, act, gs, L, tc):
    T = min(256, L)
    nt = L // T
    cb0 = col0 // tc
    ng = len(gs)

    def dact(pre, g):
        if not act:
            return g
        s = _sigmoid(pre)
        return g * (s * (1.0 + pre * (1.0 - s)))

    def kern(*refs):
        x_ref, xp_ref, xn_ref, w_ref = refs[:4]
        g_refs = refs[4:4 + 3 * ng]
        dx_ref, dw_ref = refs[4 + 3 * ng:]
        i = pl.program_id(1)
        first = i == 0
        last = i == nt - 1
        w = w_ref[...]
        g = g_refs[0][...]
        gp = g_refs[1][7:8, :]
        gn = g_refs[2][0:1, :]
        for n in range(1, ng):
            g = g + g_refs[3 * n][...]
            gp = gp + g_refs[3 * n + 1][7:8, :]
            gn = gn + g_refs[3 * n + 2][0:1, :]
        x, xm1, xp1, pre, pre_m1, pre_T = _conv_rows(x_ref, xp_ref, xn_ref, w, first, last)
        dpre_m1 = jnp.where(first, 0.0, dact(pre_m1, gp))
        dpre_T = jnp.where(last, 0.0, dact(pre_T, gn))
        dx_ref[...], dw = _conv_grads(dact(pre, g), dpre_m1, dpre_T, x, xm1, xp1, w)

        @pl.when(first)
        def _():
            dw_ref[...] = jnp.zeros((8, tc), F32)

        dw_ref[...] += dw

    g_specs, g_args = [], []
    for arr, lead in gs:
        g_specs += _halo_specs(arr.ndim, lead, T, tc, 0, L)
        g_args += [arr, arr, arr]
    return pl.pallas_call(
        kern, name=name, grid=(C // tc, nt),
        in_specs=_halo_specs(2, None, T, tc, cb0, L) + [pl.BlockSpec((8, tc), lambda j, i: (0, j))] + g_specs,
        out_specs=[pl.BlockSpec((T, tc), lambda j, i: (i, j)), pl.BlockSpec((8, tc), lambda j, i: (0, j))],
        out_shape=[jax.ShapeDtypeStruct((L, C), F32), jax.ShapeDtypeStruct((8, C), F32)],
        compiler_params=_cparams(("arbitrary", "arbitrary")),
    )(x, x, x, w8, *g_args)


FFN_TC = 1408


def _ffn_gate_fwd(name, u, w8, L):
    T = min(256, L)
    nt = L // T
    ncb = D_FF // FFN_TC

    def kern(xg, xgp, xgn, xu, xup, xun, wg_ref, wu_ref, o_ref):
        i = pl.program_id(1)
        pre_g = _conv_rows(xg, xgp, xgn, wg_ref[...], i == 0, i == nt - 1)[3]
        pre_u = _conv_rows(xu, xup, xun, wu_ref[...], i == 0, i == nt - 1)[3]
        o_ref[...] = (_silu(pre_g) * pre_u).astype(BF16)

    return pl.pallas_call(
        kern, name=name, grid=(ncb, nt),
        in_specs=_halo_specs(2, None, T, FFN_TC, 0, L) + _halo_specs(2, None, T, FFN_TC, ncb, L)
        + [pl.BlockSpec((8, FFN_TC), lambda j, i: (0, j)), pl.BlockSpec((8, FFN_TC), lambda j, i: (0, j + ncb))],
        out_specs=pl.BlockSpec((T, FFN_TC), lambda j, i: (i, j)),
        out_shape=jax.ShapeDtypeStruct((L, D_FF), BF16),
        compiler_params=_cparams(("arbitrary", "arbitrary")),
    )(u, u, u, u, u, u, w8, w8)


def _ffn_gate_bwd(name, u, w8, da, L):
    T = min(128, L)
    nt = L // T
    ncb = D_FF // FFN_TC

    def kern(xg, xgp, xgn, xu, xup, xun, wg_ref, wu_ref, d_ref, dp_ref, dn_ref, du_ref, dwg_ref, dwu_ref):
        i = pl.program_id(1)
        first = i == 0
        last = i == nt - 1
        wg = wg_ref[...]
        wu = wu_ref[...]
        g, gm1, gp1, pg, pg_m1, pg_T = _conv_rows(xg, xgp, xgn, wg, first, last)
        v, vm1, vp1, pu, pu_m1, pu_T = _conv_rows(xu, xup, xun, wu, first, last)

        def dpre(pg_, pu_, d):
            s = _sigmoid(pg_)
            return d * pu_ * (s * (1.0 + pg_ * (1.0 - s))), d * (pg_ * s)

        dg, dv = dpre(pg, pu, d_ref[...])
        dg_m1, dv_m1 = dpre(pg_m1, pu_m1, jnp.where(first, 0.0, dp_ref[7:8, :]))
        dg_T, dv_T = dpre(pg_T, pu_T, jnp.where(last, 0.0, dn_ref[0:1, :]))
        du_ref[0], dwg = _conv_grads(dg, dg_m1, dg_T, g, gm1, gp1, wg)
        du_ref[1], dwu = _conv_grads(dv, dv_m1, dv_T, v, vm1, vp1, wu)

        @pl.when(first)
        def _():
            dwg_ref[...] = jnp.zeros(dwg_ref.shape, F32)
            dwu_ref[...] = jnp.zeros(dwu_ref.shape, F32)

        dwg_ref[...] += dwg
        dwu_ref[...] += dwu

    wspec = pl.BlockSpec((8, FFN_TC), lambda j, i: (0, j))
    du, dwg, dwu = pl.pallas_call(
        kern, name=name, grid=(ncb, nt),
        in_specs=_halo_specs(2, None, T, FFN_TC, 0, L) + _halo_specs(2, None, T, FFN_TC, ncb, L)
        + [wspec, pl.BlockSpec((8, FFN_TC), lambda j, i: (0, j + ncb))] + _halo_specs(2, None, T, FFN_TC, 0, L),
        out_specs=[pl.BlockSpec((2, T, FFN_TC), lambda j, i: (0, i, j)), wspec, wspec],
        out_shape=[jax.ShapeDtypeStruct((2, L, D_FF), F32), jax.ShapeDtypeStruct((8, D_FF), F32),
                   jax.ShapeDtypeStruct((8, D_FF), F32)],
        compiler_params=_cparams(("arbitrary", "arbitrary")),
    )(u, u, u, u, u, u, w8, w8, da, da, da)
    return du, jnp.concatenate([dwg, dwu], axis=1)


def _masks(Q, rev):
    ri = lax.broadcasted_iota(jnp.int32, (Q, Q), 0)
    ci = lax.broadcasted_iota(jnp.int32, (Q, Q), 1)
    diff = (ri - ci) * (1 - 2 * rev)
    return diff >= 0, diff > 0


def _lane_pick(v, sel):
    return jnp.sum(v * sel, axis=-1, keepdims=True)


def _ssd_chunk(S, x, B, C, dtraw, alog, dtb, rev, kept=None):
    Q = dtraw.shape[0]
    incl, _ = _masks(Q, rev)
    tri = incl.astype(F32)
    dt = _softplus(dtraw + dtb)
    a_all = dt * (-jnp.exp(alog))
    acum_all = _nn_hi(tri, a_all)
    total_all = jnp.sum(a_all, axis=0, keepdims=True)
    lane = lax.broadcasted_iota(jnp.int32, (1, LANE), 1)
    acum_t = acum_all.T
    sub = lax.broadcasted_iota(jnp.int32, (LANE, 1), 0)
    ys, Sn = [], []
    for h in range(HEADS):
        g = h // 2
        sel = (lane == rev * 4 + h).astype(F32)
        acum = _lane_pick(acum_all, sel)
        dth = _lane_pick(dt, sel)
        tot = _lane_pick(total_all, sel)
        seg = acum - jnp.sum(acum_t * (sub == rev * 4 + h).astype(F32), axis=0, keepdims=True)
        decay = jnp.exp(jnp.where(incl, seg, -1e30))
        xdt = x[h] * dth
        Sh = S[HD * h:HD * (h + 1), :]
        scores = _nt(C[g], B[g]) * decay
        y_diag = _nn(scores, xdt)
        states = _tn(xdt, B[g] * jnp.exp(tot - acum))
        y_off = _nt(C[g], Sh) * jnp.exp(acum)
        ys.append(y_diag + y_off)
        Sn.append(Sh * jnp.exp(tot) + states)
    return ys, jnp.concatenate(Sn, axis=0), []


def _inv_unit_raw(Lm):
    N = Lm.shape[0]
    Q = D_CHUNK
    ri = lax.broadcasted_iota(jnp.int32, (N, N), 0)
    ci = lax.broadcasted_iota(jnp.int32, (N, N), 1)
    X = (ri == ci).astype(F32) - Lm
    P = _raw_dot(Lm, Lm, _NN, False)
    n = 2
    while n < Q:
        X = X + _raw_dot(X, P, _NN, False)
        n *= 2
        if n < Q:
            P = _raw_dot(P, P, _NN, False)
    return X


@jax.custom_vjp
def _inv_unit(Lm, T_saved):
    return _inv_unit_raw(Lm) if T_saved is None else T_saved


def _inv_unit_f(Lm, T_saved):
    T = _inv_unit_raw(Lm) if T_saved is None else T_saved
    return T, T


def _inv_unit_b(T, g):
    return -_raw_dot(_raw_dot(T, g, _TN, False), T, _NT, False), None


_inv_unit.defvjp(_inv_unit_f, _inv_unit_b)


def _delta_chunk(S, q, k, v, braw, araw, alog, dtb, rev, kept=None):
    Q = braw.shape[0]
    N = HEADS * Q
    tri = _masks(Q, rev)[0].astype(F32)
    ri = lax.broadcasted_iota(jnp.int32, (N, N), 0)
    ci = lax.broadcasted_iota(jnp.int32, (N, N), 1)
    sh = int(math.log2(Q))
    same = (ri >> sh) == (ci >> sh)
    diff = (ri - ci) * (1 - 2 * rev)
    incl = same & (diff >= 0)
    strict = same & (diff > 0)
    beta_all = _sigmoid(braw)
    g_all = -jnp.exp(alog) * _softplus(araw + dtb)
    G_all = _nn_hi(tri, g_all)
    Gtot_all = jnp.sum(g_all, axis=0, keepdims=True)
    r = lax.broadcasted_iota(jnp.int32, (N, LANE), 0)
    l = lax.broadcasted_iota(jnp.int32, (N, LANE), 1)
    selm = (l == rev * 4 + (r >> sh)).astype(F32)
    rows4 = lambda a: jnp.concatenate([a] * HEADS, axis=0)
    XG = rows4(G_all) * selm
    G = jnp.sum(XG, axis=-1, keepdims=True)
    bt = jnp.sum(rows4(beta_all) * selm, axis=-1, keepdims=True)
    Gtot = jnp.sum(Gtot_all * selm, axis=-1, keepdims=True)
    decay = jnp.exp(jnp.where(incl, G - _nt_h3(jnp.ones((N, LANE), F32), XG), -1e30))
    qs, ks, vs = (jnp.concatenate(t, axis=0) for t in (q, k, v))
    qn = qs * lax.rsqrt(jnp.sum(qs * qs, axis=-1, keepdims=True) + 1e-6)
    kn = ks * lax.rsqrt(jnp.sum(ks * ks, axis=-1, keepdims=True) + 1e-6)
    qc = qn * (HD ** -0.5)
    kb = kn * bt
    T = _inv_unit(jnp.where(strict, _nt(kb, kn) * decay, 0.0), None if kept is None else kept[0])
    eG = jnp.exp(G)
    u = _nn(T, vs * bt)
    w = _nn(T, kb * eG)
    qk = _nt(qc, kn) * decay
    spread = (lax.broadcasted_iota(jnp.int32, (HD, N), 0)
              == (lax.broadcasted_iota(jnp.int32, (HD, N), 1) & (HD - 1))).astype(F32)
    wide = lambda a: jnp.where(same, _nn(a, spread), 0.0)
    v_new = u - _nn(wide(w), S)
    o = _nn(wide(qc * eG), S) + _nn(qk, v_new)
    S_new = S * jnp.exp(Gtot) + _tn(wide(kn * jnp.exp(Gtot - G)), v_new)
    return [o[Q * h:Q * (h + 1), :] for h in range(HEADS)], S_new, [T]


def _seq_pieces(ref, r0, Q, splits):
    if splits is None:
        return ref[r0:r0 + Q, :]
    return [[ref[r0:r0 + Q, o + w * t:o + w * (t + 1)] for t in range(n)] for o, w, n in splits]


def _store_pieces(ref, r0, Q, splits, vals, extra=None):
    if splits is None:
        ref[r0:r0 + Q, :] = vals
        return
    for g, (o, w, n) in enumerate(splits):
        for t in range(n):
            v = vals[g][t]
            if extra is not None and g == 0:
                v = v + extra[r0:r0 + Q, o + w * t:o + w * (t + 1)]
            ref[r0:r0 + Q, o + w * t:o + w * (t + 1)] = v


def _flat(ins):
    out = []
    for v in ins:
        if isinstance(v, list):
            out.extend(v)
        else:
            out.append(v)
    return out


def _scan_fwd(name, chunk_fn, seqs, rows, Q, L, CH, kept_shapes=()):
    nc = L // Q
    nb = nc // CH
    ns, nr = len(seqs), len(rows)
    nk = 1 + len(kept_shapes)
    BQ = Q * CH

    def kern(*refs):
        s_refs = (refs[:ns], refs[ns:2 * ns])
        r_refs = refs[2 * ns:2 * ns + nr]
        pos = 2 * ns + nr
        y_refs = refs[pos:pos + 2]
        k_refs = (refs[pos + 2:pos + 2 + nk], refs[pos + 2 + nk:pos + 2 + 2 * nk])
        S_scr = refs[pos + 2 + 2 * nk]
        i = pl.program_id(0)

        @pl.when(i == 0)
        def _():
            S_scr[...] = jnp.zeros(S_scr.shape, F32)

        rws = [r[...] for r in r_refs]
        for d in (0, 1):
            S = S_scr[d]
            for cc in range(CH):
                c = cc if d == 0 else CH - 1 - cc
                k_refs[d][0][c] = S
                ins = [_seq_pieces(r, c * Q, Q, sp) for r, (_, _, _, sp) in zip(s_refs[d], seqs)]
                ys, S, kept = chunk_fn(S, *_flat(ins), *rws, d)
                for r, v in zip(k_refs[d][1:], kept):
                    r[c] = v
                for h in range(HEADS):
                    y_refs[d][c * Q:(c + 1) * Q, HD * h:HD * (h + 1)] = ys[h]
            S_scr[d] = S

    fwd_specs = [pl.BlockSpec((BQ, w), functools.partial(lambda i, cb: (i, cb), cb=cb)) for _, w, cb, _ in seqs]
    rev_specs = [pl.BlockSpec((BQ, w), functools.partial(lambda i, cb: (nb - 1 - i, cb), cb=cb))
                 for _, w, cb, _ in seqs]
    arrs = [a for a, _, _, _ in seqs]
    k_shapes = [(GROUP_W, HD)] + list(kept_shapes)
    res = pl.pallas_call(
        kern, name=name, grid=(nb,),
        in_specs=fwd_specs + rev_specs + [pl.BlockSpec((1, LANE), lambda i: (0, 0)) for _ in rows],
        out_specs=[pl.BlockSpec((BQ, GROUP_W), lambda i: (i, 0)),
                   pl.BlockSpec((BQ, GROUP_W), lambda i: (nb - 1 - i, 0))]
        + [pl.BlockSpec((CH,) + s, lambda i: (i, 0, 0)) for s in k_shapes]
        + [pl.BlockSpec((CH,) + s, lambda i: (nb - 1 - i, 0, 0)) for s in k_shapes],
        out_shape=[jax.ShapeDtypeStruct((L, GROUP_W), F32)] * 2
        + [jax.ShapeDtypeStruct((nc,) + s, F32) for s in k_shapes] * 2,
        scratch_shapes=[pltpu.VMEM((2, GROUP_W, HD), F32)],
        compiler_params=_cparams(("arbitrary",)),
    )(*arrs, *arrs, *rows)
    return res[0], res[1], list(res[2:2 + nk]), list(res[2 + nk:])


def _scan_bwd(name, chunk_fn, seqs, rows, ssaves, dy, extra, Q, L, CH, side=None):
    nc = L // Q
    nb = nc // CH
    BQ = Q * CH
    ns, nr = len(seqs), len(rows)
    nk = len(ssaves[0])
    has_extra = extra is not None

    def kern(*refs):
        s_refs = (refs[:ns], refs[ns:2 * ns])
        pos = 2 * ns
        r_refs = refs[pos:pos + nr]
        pos += nr
        k_refs = (refs[pos:pos + nk], refs[pos + nk:pos + 2 * nk])
        pos += 2 * nk
        dy_refs = refs[pos:pos + 2]
        pos += 2
        ex_ref = refs[pos] if has_extra else None
        pos += 1 if has_extra else 0
        ds_refs = (refs[pos:pos + ns], refs[pos + ns:pos + 2 * ns])
        pos += 2 * ns
        dr_refs = refs[pos:pos + nr]
        dS_scr = refs[pos + nr]
        i = pl.program_id(0)

        @pl.when(i == 0)
        def _():
            dS_scr[...] = jnp.zeros(dS_scr.shape, F32)
            for r in dr_refs:
                r[...] = jnp.zeros(r.shape, F32)

        rws = [r[...] for r in r_refs]
        dr_acc = [jnp.zeros((1, LANE), F32) for _ in rows]
        for d in (0, 1):
            dS = dS_scr[d]
            for cc in range(CH):
                c = CH - 1 - cc if d == 0 else cc
                S = k_refs[d][0][c]
                kept = [r[c] for r in k_refs[d][1:]]
                dys = [dy_refs[d][c * Q:(c + 1) * Q, HD * h:HD * (h + 1)] for h in range(HEADS)]
                ins = [_seq_pieces(r, c * Q, Q, sp) for r, (_, _, _, sp) in zip(s_refs[d], seqs)]
                _, vjp = jax.vjp(
                    functools.partial(
                        lambda S_, ins_, rws_, d_, kept_: chunk_fn(S_, *_flat(ins_), *rws_, d_, kept_)[:2],
                        d_=d, kept_=kept),
                    S, ins, rws)
                dS, dins, drws = vjp((dys, dS))
                for n_, (r, (_, _, _, sp)) in enumerate(zip(ds_refs[d], seqs)):
                    _store_pieces(r, c * Q, Q, sp, dins[n_],
                                  extra=ex_ref if (has_extra and d == 0 and n_ == 0) else None)
                dr_acc = [a + g for a, g in zip(dr_acc, drws)]
            dS_scr[d] = dS
        for r, g in zip(dr_refs, dr_acc):
            r[...] += g

    def blk(shape, rev, cb=0):
        nd = len(shape)
        if rev:
            return pl.BlockSpec(shape, lambda i: (i, cb) + (0,) * (nd - 2))
        return pl.BlockSpec(shape, lambda i: (nb - 1 - i, cb) + (0,) * (nd - 2))

    arrs = [a for a, _, _, _ in seqs]
    in_specs = [blk((BQ, w), False, cb) for _, w, cb, _ in seqs] + [blk((BQ, w), True, cb) for _, w, cb, _ in seqs]
    in_specs += [pl.BlockSpec((1, LANE), lambda i: (0, 0)) for _ in rows]
    in_specs += [blk((CH,) + a.shape[1:], False) for a in ssaves[0]]
    in_specs += [blk((CH,) + a.shape[1:], True) for a in ssaves[1]]
    in_specs += [blk((BQ, GROUP_W), False), blk((BQ, GROUP_W), True)]
    args = arrs + arrs + list(rows) + list(ssaves[0]) + list(ssaves[1]) + [dy, dy]
    if has_extra:
        in_specs.append(blk((BQ, GROUP_W), False))
        args.append(extra)
    kern, s_in, s_out, s_shapes, s_scratch, s_args = _host(kern, len(args), 2 * ns + nr, side, (nb,), 1)
    res = pl.pallas_call(
        kern, name=name, grid=(nb,),
        in_specs=in_specs + s_in,
        out_specs=[blk((BQ, w), False) for _, w, _, _ in seqs] + [blk((BQ, w), True) for _, w, _, _ in seqs]
        + [pl.BlockSpec((1, LANE), lambda i: (0, 0)) for _ in rows] + s_out,
        out_shape=[jax.ShapeDtypeStruct((L, w), F32) for _, w, _, _ in seqs] * 2
        + [jax.ShapeDtypeStruct((1, LANE), F32) for _ in rows] + s_shapes,
        scratch_shapes=[pltpu.VMEM((2, GROUP_W, HD), F32)] + s_scratch,
        compiler_params=_cparams(("arbitrary",)),
    )(*args, *s_args)
    return list(res[:ns]), list(res[ns:2 * ns]), list(res[2 * ns:2 * ns + nr]), list(res[2 * ns + nr:])


def _loss_call(y, tgt, L):
    T = min(256, L)

    def kern(y_ref, t_ref, dy_ref, l_ref):
        i = pl.program_id(0)
        e = y_ref[...] - t_ref[...]
        dy_ref[...] = e * (1.0 / D_MODEL)

        @pl.when(i == 0)
        def _():
            l_ref[...] = jnp.zeros(l_ref.shape, F32)

        part = 0.5 * jnp.sum(jnp.sum(e * e, axis=-1, keepdims=True) * (1.0 / D_MODEL), axis=0, keepdims=True)
        l_ref[...] += jnp.broadcast_to(part, l_ref.shape)

    return pl.pallas_call(
        kern, name="loss_head", grid=(L // T,),
        in_specs=[_spec2(T, D_MODEL), _spec2(T, D_MODEL)],
        out_specs=[_spec2(T, D_MODEL), pl.BlockSpec((8, LANE), lambda i: (0, 0))],
        out_shape=[jax.ShapeDtypeStruct((L, D_MODEL), F32), jax.ShapeDtypeStruct((8, LANE), F32)],
        compiler_params=_cparams(("arbitrary",)),
    )(y, tgt)


_ANY = pl.BlockSpec(memory_space=pl.ANY)


def _coords():
    return lax.axis_index("x"), lax.axis_index("y"), lax.axis_index("c")


class _Copies:
    def __init__(self, ins, out_shapes, copies_fn, n_remote, n_local):
        self.ins, self.out_shapes, self.copies_fn = list(ins), list(out_shapes), copies_fn
        self.n_remote, self.n_local = n_remote, n_local

    def scratch(self):
        return [pltpu.SemaphoreType.DMA((self.n_remote,)), pltpu.SemaphoreType.DMA((self.n_remote,)),
                pltpu.SemaphoreType.DMA((self.n_local,))]

    def _descr(self, in_refs, out_refs, sems):
        send_sems, recv_sems, lsems = sems
        remote, local = self.copies_fn(list(in_refs), list(out_refs))
        assert len(remote) == self.n_remote and len(local) == self.n_local
        mk = lambda k, src, dst, peer: pltpu.make_async_remote_copy(
            src_ref=src, dst_ref=dst, send_sem=send_sems.at[k], recv_sem=recv_sems.at[k], device_id=peer,
            device_id_type=MESH)
        sends = [mk(k, src, dst, peer) for k, (src, dst, _, peer) in enumerate(remote)]
        recvs = [mk(k, src, land, peer) for k, (src, _, land, peer) in enumerate(remote)]
        locs = [pltpu.make_async_copy(src, dst, lsems.at[k]) for k, (src, dst) in enumerate(local)]
        return sends, recvs, locs

    def start(self, in_refs, out_refs, sems):
        sends, _, locs = self._descr(in_refs, out_refs, sems)
        for c in locs + sends:
            c.start()

    def finish(self, in_refs, out_refs, sems):
        sends, recvs, locs = self._descr(in_refs, out_refs, sems)
        for c in recvs:
            c.wait_recv()
        for c in sends:
            c.wait_send()
        for c in locs:
            c.wait()

    def call(self, name):
        ni, no = len(self.ins), len(self.out_shapes)

        def body(*refs):
            self.start(refs[:ni], refs[ni:ni + no], refs[ni + no:])
            self.finish(refs[:ni], refs[ni:ni + no], refs[ni + no:])

        return pl.pallas_call(body, name=name, in_specs=[_ANY] * ni, out_specs=[_ANY] * no,
                              out_shape=self.out_shapes, scratch_shapes=self.scratch())(*self.ins)


def _chip_peers(x, y):
    return [(1 - x, y), (x, 1 - y), (1 - x, 1 - y)]


def _gather_copies(arrs):
    def copies_fn(ins, outs):
        x, y, c = _coords()
        me = 2 * x + y
        remote, local = [], []
        for src, out in zip(ins, outs):
            local.append((src, out.at[me]))
            for px, py in _chip_peers(x, y):
                remote.append((src, out.at[me], out.at[2 * px + py], (px, py, c)))
        return remote, local

    shapes = [jax.ShapeDtypeStruct((4,) + a.shape, a.dtype) for a in arrs]
    return _Copies(arrs, shapes, copies_fn, 3 * len(arrs), len(arrs))


def _scatter_copies(Gs, small):
    nb = len(Gs)

    def copies_fn(ins, outs):
        x, y, c = _coords()
        me = 2 * x + y
        remote, local = [], []
        for g, out in zip(ins[:nb], outs[:nb]):
            local.append((g.at[me], out.at[me]))
            for px, py in _chip_peers(x, y):
                remote.append((g.at[2 * px + py], out.at[me], out.at[2 * px + py], (px, py, c)))
        if small is not None:
            dev = 4 * x + 2 * y + c
            gs, outs_ = ins[nb], outs[nb]
            local.append((gs, outs_.at[dev]))
            for mask in range(1, 8):
                px, py, pc = x ^ (mask >> 2), y ^ ((mask >> 1) & 1), c ^ (mask & 1)
                remote.append((gs, outs_.at[dev], outs_.at[4 * px + 2 * py + pc], (px, py, pc)))
        return remote, local

    ins = list(Gs) + ([small] if small is not None else [])
    shapes = [jax.ShapeDtypeStruct(g.shape, g.dtype) for g in Gs]
    if small is not None:
        shapes.append(jax.ShapeDtypeStruct((8,) + small.shape, small.dtype))
    extra = 1 if small is not None else 0
    return _Copies(ins, shapes, copies_fn, 3 * nb + 7 * extra, nb + extra)


SWAP_STREAMS = 8


def _row_chunks(rows):
    k = SWAP_STREAMS
    if rows % (8 * k) == 0 and rows >= 64 * k:
        return [(q * (rows // k), rows // k) for q in range(k)]
    return [(0, rows)]


def _swap_copies(parts):
    chunks = [_row_chunks(p.shape[0]) for p in parts]
    n = sum(len(ch) for ch in chunks)

    def copies_fn(ins, outs):
        x, y, c = _coords()
        remote, local = [], []
        for src, out, ch in zip(ins, outs, chunks):
            for r0, nr in ch:
                rows = pl.ds(r0, nr)
                local.append((src.at[rows], out.at[c, rows]))
                remote.append((src.at[rows], out.at[c, rows], out.at[1 - c, rows], (x, y, 1 - c)))
        return remote, local

    shapes = [jax.ShapeDtypeStruct((2,) + p.shape, p.dtype) for p in parts]
    return _Copies(parts, shapes, copies_fn, n, n)


def _merge_copies(sets):
    ins = [a for s in sets for a in s.ins]
    shapes = [o for s in sets for o in s.out_shapes]

    def copies_fn(in_refs, out_refs):
        remote, local, pi, po = [], [], 0, 0
        for s in sets:
            r, l = s.copies_fn(in_refs[pi:pi + len(s.ins)], out_refs[po:po + len(s.out_shapes)])
            remote += r
            local += l
            pi += len(s.ins)
            po += len(s.out_shapes)
        return remote, local

    return _Copies(ins, shapes, copies_fn, sum(s.n_remote for s in sets), sum(s.n_local for s in sets))


def _row_tile(rows):
    best = rows
    for d in range(8, min(rows, 256) + 1, 8):
        if rows % d == 0:
            best = d
    return best


def _sum_slots(name, recv, out_dtype=F32):
    n, R, W = recv.shape
    tr = _row_tile(R)

    def kern(r_ref, o_ref):
        acc = r_ref[0].astype(F32)
        for s in range(1, n):
            acc = acc + r_ref[s].astype(F32)
        o_ref[...] = acc.astype(out_dtype)

    return pl.pallas_call(
        kern, name=name, grid=(R // tr,),
        in_specs=[pl.BlockSpec((n, tr, W), lambda i: (0, i, 0))],
        out_specs=pl.BlockSpec((tr, W), lambda i: (i, 0)),
        out_shape=jax.ShapeDtypeStruct((R, W), out_dtype),
        compiler_params=_cparams(("arbitrary",)),
    )(recv)


def _adamw_call(name, slots, w, m, v):
    nl = len(slots)
    n, R, W = slots[0].shape
    tr = _row_tile(R)
    nr = R // tr

    def kern(*refs):
        s_refs = refs[:nl]
        w_ref, m_ref, v_ref, g_ref, d_ref, nm_ref, nv_ref = refs[nl:]
        layer = pl.program_id(0)
        g = s_refs[0][0].astype(F32)
        for s in range(1, n):
            g = g + s_refs[0][s].astype(F32)
        for l in range(1, nl):
            gl = s_refs[l][0].astype(F32)
            for s in range(1, n):
                gl = gl + s_refs[l][s].astype(F32)
            g = jnp.where(layer == l, gl, g)
        m_ = ADAM_B1 * m_ref[...] + (1.0 - ADAM_B1) * g
        v_ = ADAM_B2 * v_ref[...] + (1.0 - ADAM_B2) * (g * g)
        m_hat = m_ / (1.0 - ADAM_B1 ** ADAM_STEP)
        v_hat = v_ / (1.0 - ADAM_B2 ** ADAM_STEP)
        g_ref[...] = g
        d_ref[...] = -ADAM_LR * (m_hat / (jnp.sqrt(v_hat) + ADAM_EPS) + ADAM_WD * w_ref[...])
        nm_ref[...] = m_
        nv_ref[...] = v_

    blk = pl.BlockSpec((None, tr, W), lambda l, i: (l, i, 0))
    return pl.pallas_call(
        kern, name=name, grid=(nl, nr),
        in_specs=[pl.BlockSpec((n, tr, W), lambda l, i: (0, i, 0)) for _ in slots] + [blk, blk, blk],
        out_specs=[blk, blk, blk, blk],
        out_shape=[jax.ShapeDtypeStruct((nl, R, W), F32)] * 4,
        compiler_params=_cparams(("arbitrary", "arbitrary")),
    )(*slots, w, m, v)


def _pack(arrs, width, row_mult):
    flat = jnp.concatenate([a.reshape(-1) for a in arrs])
    n = flat.shape[0]
    rows = -(-n // width)
    rows = -(-rows // row_mult) * row_mult
    return jnp.pad(flat, (0, rows * width - n)).reshape(rows, width)


def _unpack(buf, shapes):
    flat = buf.reshape(-1)
    out, pos = [], 0
    for s in shapes:
        n = int(np.prod(s))
        out.append(flat[pos:pos + n].reshape(s))
        pos += n
    return out


def _rope_angles(L, rot_dim):
    rows = L // GRID_W
    row = jnp.repeat(jnp.arange(rows), GRID_W).astype(F32)
    col = jnp.tile(jnp.arange(GRID_W), rows).astype(F32)
    sec = rot_dim // 2
    inv_freq = ROPE_BASE ** (-jnp.arange(0, sec, 2, dtype=F32) / sec)
    ang_r = row[:, None] * inv_freq
    ang_c = col[:, None] * inv_freq
    ang = jnp.concatenate([ang_r, ang_r, ang_c, ang_c], axis=-1)
    return jnp.cos(ang), jnp.sin(ang)


def _rot_matrix(r):
    R = np.zeros((r, r), np.float32)
    q = r // 4
    for s in range(2):
        for t in range(q):
            lo = s * (r // 2) + t
            hi = lo + q
            R[hi, lo] = -1.0
            R[lo, hi] = 1.0
    return R


def _place_tables(L, cos, sin, width, offsets):
    r = cos.shape[1]
    Rm = np.zeros((width, width), np.float32)
    R = _rot_matrix(r)
    cs, ss, pos = [], [], 0
    for o in list(offsets) + [width]:
        if o > pos:
            cs.append(jnp.ones((L, o - pos), F32))
            ss.append(jnp.zeros((L, o - pos), F32))
        if o < width:
            cs.append(cos)
            ss.append(sin)
            Rm[o:o + r, o:o + r] = R
        pos = o + r
    return jnp.concatenate(cs, axis=1), jnp.concatenate(ss, axis=1), jnp.asarray(Rm)


def _head_mean_matrix(width, stride, n):
    M = np.zeros((width, width), np.float32)
    for o in range(0, width, stride):
        M[o:o + n, o:o + n] = 1.0 / n
    return jnp.asarray(M)


def _pad_heads(w, n_heads, real, padded, axis):
    parts = jnp.split(w, n_heads, axis=axis)
    padw = [(0, 0)] * w.ndim
    padw[axis] = (0, padded - real)
    return jnp.concatenate([jnp.pad(p, padw) for p in parts], axis=axis)


def _row128(v):
    v = v.reshape(1, -1)
    return jnp.pad(v, ((0, 0), (0, LANE - v.shape[1])))


def _conv_w8(w, b):
    C = w.shape[1]
    rows = [w, jnp.zeros((1, C), F32) if b is None else b.reshape(1, C), jnp.zeros((4, C), F32)]
    return jnp.concatenate(rows, axis=0)


def _build_layer(W):
    w_in = W['w_in']
    o = 0
    cols = {}
    for name, n in [('a_cq', A_Q_LORA), ('a_ckv', A_KV_LORA), ('a_kr', A_ROPE), ('b_q', 256), ('b_k', 128),
                    ('b_v', 128), ('c_z', 256), ('c_xbc', 512), ('c_dt', 8), ('d_qkv', 768), ('d_z', 256),
                    ('d_b', 8), ('d_a', 8)]:
        cols[name] = w_in[:, o:o + n]
        o += n
    padc = lambda a, lo, width: jnp.pad(a, ((0, 0), (lo, width - lo - a.shape[1])))
    pieces = {
        'b_q': _pad_heads(cols['b_q'], 4, HD, LANE, 1), 'c_xbc': cols['c_xbc'], 'a_cq': padc(cols['a_cq'], 0, 256),
        'b_k': _pad_heads(cols['b_k'], 2, HD, LANE, 1), 'd_qkv': cols['d_qkv'],
        'b_v': _pad_heads(cols['b_v'], 2, HD, LANE, 1), 'c_z': cols['c_z'], 'd_z': cols['d_z'],
        'a_ckv': cols['a_ckv'], 'a_kr': padc(cols['a_kr'], A_NOPE, LANE), 'c_dt': padc(cols['c_dt'], 0, LANE),
        'd_b': padc(cols['d_b'], 0, LANE), 'd_a': padc(cols['d_a'], 0, LANE),
        'pad': jnp.zeros((D_MODEL, LANE), w_in.dtype)}
    out = {'w_in': jnp.concatenate([pieces[n] for n, _, _ in P_LAYOUT], axis=1)}
    out['a_q_norm'] = padc(W['a_q_norm'].reshape(1, -1), 0, 256)
    wuq = jnp.pad(W['a_w_uq'], ((0, 256 - A_Q_LORA), (0, 0)))
    out['a_w_uq'] = _pad_heads(wuq, 4, A_NOPE + A_ROPE, LANE, 1)
    out['a_kv_norm'] = W['a_kv_norm'].reshape(1, -1)
    ukv = W['a_w_ukv'].reshape(A_KV_LORA, HEADS, 2, HD)
    out['a_w_uk'] = _pad_heads(ukv[:, :, 0, :].reshape(A_KV_LORA, 256), 4, HD, LANE, 1)
    out['a_w_uv'] = _pad_heads(ukv[:, :, 1, :].reshape(A_KV_LORA, 256), 4, HD, LANE, 1)
    out['a_out_norm'] = _pad_heads(W['a_out_norm'].reshape(1, -1), 4, HD, LANE, 1)
    out['b_q_norm'] = _pad_heads(jnp.tile(W['b_q_norm'].reshape(1, -1), (1, 4)), 4, HD, LANE, 1)
    out['b_k_norm'] = _pad_heads(jnp.tile(W['b_k_norm'].reshape(1, -1), (1, 2)), 2, HD, LANE, 1)
    out['b_out_norm'] = _pad_heads(W['b_out_norm'].reshape(1, -1), 4, HD, LANE, 1)
    out['c_conv'] = _conv_w8(W['c_conv_w'], W['c_conv_b'])
    out['c_a_log'] = _row128(W['c_a_log'])
    out['c_dt_bias'] = _row128(W['c_dt_bias'])
    out['c_d_skip'] = jnp.repeat(W['c_d_skip'], HD).reshape(1, -1)
    out['c_out_norm'] = W['c_out_norm'].reshape(1, -1)
    out['d_conv'] = _conv_w8(W['d_conv_w'], None)
    out['d_a_log'] = _row128(W['d_a_log'])
    out['d_dt_bias'] = _row128(W['d_dt_bias'])
    out['d_out_norm'] = jnp.tile(W['d_out_norm'].reshape(1, -1), (1, 4))
    wo = W['w_out']
    out['w_out'] = jnp.concatenate([_pad_heads(wo[0:256], 4, HD, LANE, 0), _pad_heads(wo[256:512], 4, HD, LANE, 0),
                                    wo[512:1024]], axis=0)
    for n in ['pre_mix_norm', 'post_mix_norm', 'pre_ffn_norm', 'post_ffn_norm']:
        out[n] = W[n].reshape(1, -1)
    out['f_w_in'] = W['f_w_in']
    out['f_conv'] = _conv_w8(W['f_conv_w'], W['f_conv_b'])
    out['f_w_out'] = W['f_w_out']
    return out


def _fn_norm_in(a, p):
    return [_rms(a[0], p[0])]


def _fn_resid_norm2(a, p):
    x1 = a[0] + _rms(a[1], p[0])
    return [x1, _rms(x1, p[1])]


def _fn_resid_norm(a, p):
    return [a[0] + _rms(a[1], p[0])]


def _fn_a_prep(a, p):
    cq, ckv, kr, cosk, sink = a
    q_norm, w_uq, kv_norm, w_uk, w_uv, rq, rk = p
    cosq = jnp.concatenate([cosk] * HEADS, axis=1)
    sinq = jnp.concatenate([sink] * HEADS, axis=1)
    q = _nn(_rms(cq, q_norm, A_Q_LORA), w_uq)
    q = q * cosq + _nn_h3(q, rq) * sinq
    kvn = _rms(ckv, kv_norm)
    kr_r = kr * cosk + _nn_h3(kr, rk) * sink
    kk = _nn(kvn, w_uk) + jnp.concatenate([kr_r] * HEADS, axis=1)
    return [q, kk, _nn(kvn, w_uv)]


def _fn_b_prep(a, p):
    q, k, v, cos1, sin1 = a
    q_norm, k_norm, mq, mk, rq, rk = p
    cosq, sinq = (jnp.concatenate([t] * 4, axis=1) for t in (cos1, sin1))
    cosk, sink = (jnp.concatenate([t] * 2, axis=1) for t in (cos1, sin1))
    qn = q * lax.rsqrt(_nn_h3(q * q, mq) + EPS) * q_norm
    kn = k * lax.rsqrt(_nn_h3(k * k, mk) + EPS) * k_norm
    return [qn * cosq + _nn_h3(qn, rq) * sinq, kn * cosk + _nn_h3(kn, rk) * sink, v]


def _fn_mixer_post(a, p):
    oa, ob, yc0, yc1, xs, zc, od0, od1, zd = a
    a_norm, b_norm, dskip, c_norm, d_norm, m64 = p
    oc = _rms((yc0 + yc1 + xs * dskip) * _silu(zc), c_norm)
    od = od0 + od1
    odn = od * lax.rsqrt(_nn_h3(od * od, m64) + EPS) * d_norm * _silu(zd)
    return [jnp.concatenate([_rms(oa, a_norm, GROUP_W), _rms(ob, b_norm, GROUP_W), oc, odn], axis=1)]


def _fn_assemble(a, p):
    (dbq, dxbc, dcq, dbk, dqkv, dbv, dzc, dzd, dckv, dkr, ddt0, ddt1, db0, db1, da0, da1) = a
    return [jnp.concatenate([dbq, dxbc, dcq, dbk, dqkv, dbv, dzc, dzd, dckv, dkr, ddt0 + ddt1, db0 + db1,
                             da0 + da1, jnp.zeros_like(dckv)], axis=1)]


def _pspec(T, name):
    off, w = P_OFF[name]
    return _spec2(T, w, off // w)


def _layer_fwd(l, x, h, K, tabs, L, T, next_norm, side_a=None, late=None, side_b=None):
    n = f"l{l}_"
    sv = {'x': x, 'h': h}
    p = _mm(n + "in_proj", h, K['w_in'].astype(BF16), 'nn', F32, 1024, 1280, 1024)
    sv['p'] = p
    a_acts = [(p, _pspec(T, 'a_cq')), (p, _pspec(T, 'a_ckv')), (p, _pspec(T, 'a_kr')),
              (tabs['a_c'], _spec2(T, LANE)), (tabs['a_s'], _spec2(T, LANE))]
    a_pars = [K['a_q_norm'], K['a_w_uq'], K['a_kv_norm'], K['a_w_uk'], K['a_w_uv'], tabs['a_rq'], tabs['a_rk']]
    qa, ka, va = _tw_fwd(n + "a_prep", _fn_a_prep, a_acts, a_pars, [(512, BF16)] * 3, L, T)
    oa, lse_a, got_a = _flash_fwd(n + "a_attn", qa, ka, va, HEADS, 1, (A_NOPE + A_ROPE) ** -0.5, L, side_a)
    if late is not None:
        K = {**K, **late(got_a)}
    sv.update(a_acts=a_acts, a_pars=a_pars, qa=qa, ka=ka, va=va, oa=oa, lse_a=lse_a, K=K)
    b_acts = [(p, _pspec(T, 'b_q')), (p, _pspec(T, 'b_k')), (p, _pspec(T, 'b_v')),
              (tabs['b_c'], _spec2(T, LANE)), (tabs['b_s'], _spec2(T, LANE))]
    b_pars = [K['b_q_norm'], K['b_k_norm'], tabs['b_mq'], tabs['b_mk'], tabs['b_rq'], tabs['b_rk']]
    qb, kb, vb = _tw_fwd(n + "b_prep", _fn_b_prep, b_acts, b_pars, [(512, BF16), (256, BF16), (256, BF16)], L, T)
    ob, lse_b, sv['side'] = _flash_fwd(n + "b_attn", qb, kb, vb, HEADS, 2, HD ** -0.5, L, side_b)
    sv.update(b_acts=b_acts, b_pars=b_pars, qb=qb, kb=kb, vb=vb, ob=ob, lse_b=lse_b)
    xbc = _conv_fwd(n + "c_conv", p, P_OFF['c_xbc'][0], C_XBC, K['c_conv'], True, L, 512)
    c_seqs = [(xbc, C_XBC, 0, [(0, HD, 4), (256, HD, 2), (384, HD, 2)]),
              (p, LANE, P_OFF['c_dt'][0] // LANE, None)]
    c_rows = [K['c_a_log'], K['c_dt_bias']]
    yc0, yc1, sc0, sc1 = _scan_fwd(n + "c_ssd", _ssd_chunk, c_seqs, c_rows, C_CHUNK, L, C_PER_STEP)
    sv.update(xbc=xbc, c_seqs=c_seqs, c_rows=c_rows, sc=(sc0, sc1))
    qkv = _conv_fwd(n + "d_conv", p, P_OFF['d_qkv'][0], D_QKV, K['d_conv'], True, L, 768)
    d_seqs = [(qkv, D_QKV, 0, [(0, HD, 4), (256, HD, 4), (512, HD, 4)]),
              (p, LANE, P_OFF['d_b'][0] // LANE, None), (p, LANE, P_OFF['d_a'][0] // LANE, None)]
    d_rows = [K['d_a_log'], K['d_dt_bias']]
    od0, od1, sd0, sd1 = _scan_fwd(n + "d_delta", _delta_chunk, d_seqs, d_rows, D_CHUNK, L, D_PER_STEP,
                                   [(HEADS * D_CHUNK, HEADS * D_CHUNK)])
    sv.update(qkv=qkv, d_seqs=d_seqs, d_rows=d_rows, sd=(sd0, sd1))
    m_acts = [(oa, _spec2(T, 512)), (ob, _spec2(T, 512)), (yc0, _spec2(T, 256)), (yc1, _spec2(T, 256)),
              (xbc, _spec2(T, 256, 0)), (p, _pspec(T, 'c_z')), (od0, _spec2(T, 256)), (od1, _spec2(T, 256)),
              (p, _pspec(T, 'd_z'))]
    m_pars = [K['a_out_norm'], K['b_out_norm'], K['c_d_skip'], K['c_out_norm'], K['d_out_norm'], tabs['m64']]
    (o,) = _tw_fwd(n + "mixer_post", _fn_mixer_post, m_acts, m_pars, [(O_COLS, BF16)], L, T)
    f1 = _mm(n + "out_proj", o, K['w_out'].astype(BF16), 'nn', F32, 1024, 1024, 1536)
    r1_pars = [K['post_mix_norm'], K['pre_ffn_norm']]
    x1, h2 = _tw_fwd(n + "resid_mix", _fn_resid_norm2, [(x, _spec2(T, D_MODEL)), (f1, _spec2(T, D_MODEL))], r1_pars,
                     [(D_MODEL, F32), (D_MODEL, BF16)], L, T)
    sv.update(m_acts=m_acts, m_pars=m_pars, o=o, f1=f1, r1_pars=r1_pars, x1=x1, h2=h2)
    u = _mm(n + "ffn_in", h2, K['f_w_in'].astype(BF16), 'nn', F32, 1024, 1408, 1024)
    act = _ffn_gate_fwd(n + "ffn_gate", u, K['f_conv'], L)
    f2 = _mm(n + "ffn_out", act, K['f_w_out'].astype(BF16), 'nn', F32, 1024, 1024, 1408)
    sv.update(u=u, act=act, f2=f2)
    xf = [(x1, _spec2(T, D_MODEL)), (f2, _spec2(T, D_MODEL))]
    if next_norm is None:
        (x2,) = _tw_fwd(n + "resid_ffn", _fn_resid_norm, xf, [K['post_ffn_norm']], [(D_MODEL, F32)], L, T)
        hn = None
    else:
        x2, hn = _tw_fwd(n + "resid_ffn", _fn_resid_norm2, xf, [K['post_ffn_norm'], next_norm],
                         [(D_MODEL, F32), (D_MODEL, BF16)], L, T)
    return x2, hn, sv


def _layer_bwd(l, dx2, dhn, K, sv, tabs, L, T, next_norm, hosts=None):
    n = f"l{l}b_"
    dK = {}
    hosts = hosts or {}
    got = {}
    side = lambda name: hosts[name](dK, got) if name in hosts else None
    s2 = lambda w, cb=0: _spec2(T, w, cb)
    xf = [(sv['x1'], s2(D_MODEL)), (sv['f2'], s2(D_MODEL))]
    if next_norm is None:
        (dx1a, df2), (dK['post_ffn_norm'],) = _tw_bwd(n + "resid_ffn", _fn_resid_norm, xf, [K['post_ffn_norm']],
                                                      [(dx2, s2(D_MODEL))], L, T, [True, True], [True])
        dnext = None
    else:
        (dx1a, df2), (dK['post_ffn_norm'], dnext) = _tw_bwd(
            n + "resid_ffn", _fn_resid_norm2, xf, [K['post_ffn_norm'], next_norm],
            [(dx2, s2(D_MODEL)), (dhn, s2(D_MODEL))], L, T, [True, True], [True, True])
    dact = _mm(n + "ffn_out_dx", df2, K['f_w_out'].astype(BF16), 'nt', F32, 1024, 1408, 1024)
    dK['f_w_out'] = _mm(n + "ffn_out_dw", sv['act'], df2, 'tn', F32, 1408, 1024, 1024)
    du, dK['f_conv'] = _ffn_gate_bwd(n + "ffn_gate", sv['u'], K['f_conv'], dact, L)
    dh2 = _mm(n + "ffn_in_dx", du, K['f_w_in'].astype(BF16), 'nt', F32, 1024, 1024, 1408)
    dK['f_w_in'] = _mm(n + "ffn_in_dw", sv['h2'], du, 'tn', F32, 1024, 1408, 1024)
    (dxa, df1), (dK['post_mix_norm'], dK['pre_ffn_norm']) = _tw_bwd(
        n + "resid_mix", _fn_resid_norm2, [(sv['x'], s2(D_MODEL)), (sv['f1'], s2(D_MODEL))], sv['r1_pars'],
        [(dx1a, s2(D_MODEL)), (dh2, s2(D_MODEL))], L, T, [True, True], [True, True])
    do = _mm(n + "out_proj_dx", df1, K['w_out'].astype(BF16), 'nt', F32, 1024, 1536, 1024)
    dK['w_out'] = _mm(n + "out_proj_dw", sv['o'], df1, 'tn', F32, 1536, 1024, 1024)
    (doa, dob, dyc0, _, dxs_skip, dzc, dod0, _, dzd), mp = _tw_bwd(
        n + "mixer_post", _fn_mixer_post, sv['m_acts'], sv['m_pars'], [(do, s2(O_COLS))], L, T,
        [True] * 9, [True] * 5 + [False])
    dK['a_out_norm'], dK['b_out_norm'], dK['c_d_skip'], dK['c_out_norm'], dK['d_out_norm'] = mp
    (dqkv0, db0, da0), (dqkv1, db1, da1), (dK['d_a_log'], dK['d_dt_bias']), got['d_delta'] = _scan_bwd(
        n + "d_delta", _delta_chunk, sv['d_seqs'], sv['d_rows'], sv['sd'], dod0, None, D_CHUNK, L, D_PER_STEP,
        side('d_delta'))
    dqkv, dK['d_conv'] = _conv_bwd(n + "d_conv", sv['p'], P_OFF['d_qkv'][0], D_QKV, K['d_conv'], True,
                                   [(dqkv0, None), (dqkv1, None)], L, 768)
    (dxbc0, ddt0), (dxbc1, ddt1), (dK['c_a_log'], dK['c_dt_bias']), _ = _scan_bwd(
        n + "c_ssd", _ssd_chunk, sv['c_seqs'], sv['c_rows'], sv['sc'], dyc0, dxs_skip, C_CHUNK, L, C_PER_STEP)
    dxbc, dK['c_conv'] = _conv_bwd(n + "c_conv", sv['p'], P_OFF['c_xbc'][0], C_XBC, K['c_conv'], True,
                                   [(dxbc0, None), (dxbc1, None)], L, 512)
    dqb, dkb, dvb, got['b_attn'] = _flash_bwd(n + "b_attn", sv['qb'], sv['kb'], sv['vb'], sv['ob'], sv['lse_b'],
                                              dob, HEADS, 2, HD ** -0.5, L, side('b_attn'))
    (dbq, dbk, dbv), (dK['b_q_norm'], dK['b_k_norm']) = _tw_bwd(
        n + "b_prep", _fn_b_prep, sv['b_acts'], sv['b_pars'], [(dqb, s2(512)), (dkb, s2(256)), (dvb, s2(256))],
        L, T, [True] * 3 + [False] * 2, [True, True] + [False] * 4)
    dqa, dka, dva, got['a_attn'] = _flash_bwd(n + "a_attn", sv['qa'], sv['ka'], sv['va'], sv['oa'], sv['lse_a'],
                                              doa, HEADS, 1, (A_NOPE + A_ROPE) ** -0.5, L, side('a_attn'))
    (dcq, dckv, dkr), ap = _tw_bwd(
        n + "a_prep", _fn_a_prep, sv['a_acts'], sv['a_pars'], [(dqa, s2(512)), (dka, s2(512)), (dva, s2(512))],
        L, T, [True] * 3 + [False] * 2, [True] * 5 + [False] * 2)
    dK['a_q_norm'], dK['a_w_uq'], dK['a_kv_norm'], dK['a_w_uk'], dK['a_w_uv'] = ap
    pieces = [(dbq, s2(512)), (dxbc, s2(512)), (dcq, s2(256)), (dbk, s2(256)), (dqkv, s2(768)), (dbv, s2(256)),
              (dzc, s2(256)), (dzd, s2(256)), (dckv, s2(LANE)), (dkr, s2(LANE)),
              (ddt0, s2(LANE)), (ddt1, s2(LANE)), (db0, s2(LANE)), (db1, s2(LANE)), (da0, s2(LANE)),
              (da1, s2(LANE))]
    (dp,) = _tw_fwd(n + "assemble_dp", _fn_assemble, pieces, [], [(P_COLS, BF16)], L, T)
    dh = _mm(n + "in_proj_dx", dp, K['w_in'].astype(BF16), 'nt', F32, 1024, 1024, 1280)
    dK['w_in'] = _mm(n + "in_proj_dw", sv['h'], dp, 'tn', F32, 1024, 1280, 1024)
    return dxa, dh, dK, dnext, got


def _tables(L):
    ca, sa = _rope_angles(L, A_ROPE)
    cb, sb = _rope_angles(L, HD)
    t = {}
    t['a_c'], t['a_s'], t['a_rk'] = _place_tables(L, ca, sa, LANE, [A_NOPE])
    t['b_c'], t['b_s'], _ = _place_tables(L, cb, sb, LANE, [0])
    t['a_rq'] = _place_tables(8, ca[:8], sa[:8], 512, [LANE * h + A_NOPE for h in range(4)])[2]
    t['b_rq'] = _place_tables(8, cb[:8], sb[:8], 512, [LANE * h for h in range(4)])[2]
    t['b_rk'] = _place_tables(8, cb[:8], sb[:8], 256, [LANE * h for h in range(2)])[2]
    t['b_mq'] = _head_mean_matrix(512, LANE, HD)
    t['b_mk'] = _head_mean_matrix(256, LANE, HD)
    t['m64'] = _head_mean_matrix(256, HD, HD)
    return t


def kernel(x, pre_mix_norm, w_in, a_q_norm, a_w_uq, a_kv_norm, a_w_ukv, a_out_norm, b_q_norm, b_k_norm, b_out_norm, c_conv_w, c_conv_b, c_a_log, c_dt_bias, c_d_skip, c_out_norm, d_conv_w, d_a_log, d_dt_bias, d_out_norm, w_out, post_mix_norm, pre_ffn_norm, f_w_in, f_conv_w, f_conv_b, f_w_out, post_ffn_norm, loss_target, m_pre_mix_norm, m_w_in, m_a_q_norm, m_a_w_uq, m_a_kv_norm, m_a_w_ukv, m_a_out_norm, m_b_q_norm, m_b_k_norm, m_b_out_norm, m_c_conv_w, m_c_conv_b, m_c_a_log, m_c_dt_bias, m_c_d_skip, m_c_out_norm, m_d_conv_w, m_d_a_log, m_d_dt_bias, m_d_out_norm, m_w_out, m_post_mix_norm, m_pre_ffn_norm, m_f_w_in, m_f_conv_w, m_f_conv_b, m_f_w_out, m_post_ffn_norm, v_pre_mix_norm, v_w_in, v_a_q_norm, v_a_w_uq, v_a_kv_norm, v_a_w_ukv, v_a_out_norm, v_b_q_norm, v_b_k_norm, v_b_out_norm, v_c_conv_w, v_c_conv_b, v_c_a_log, v_c_dt_bias, v_c_d_skip, v_c_out_norm, v_d_conv_w, v_d_a_log, v_d_dt_bias, v_d_out_norm, v_w_out, v_post_mix_norm, v_pre_ffn_norm, v_f_w_in, v_f_conv_w, v_f_conv_b, v_f_w_out, v_post_ffn_norm):
    loc = locals()
    Wl = {n: loc[n] for n in WEIGHTS}
    Ml = {n: loc['m_' + n] for n in WEIGHTS}
    Vl = {n: loc['v_' + n] for n in WEIGHTS}
    L = x.shape[1]
    T = min(512, L)
    x0 = x.reshape(L, D_MODEL)
    tgt = loss_target.reshape(L, D_MODEL)

    first = ['w_in', 'a_w_uq', 'a_w_ukv', 'c_conv_w', 'd_conv_w']
    later = [n for n in SHARDED if n not in first]
    late_keys = ['w_out', 'f_w_in', 'f_conv', 'f_w_out']

    def shards(l, names):
        return [Wl[n][l].astype(BF16) if n in MXU_WEIGHTS else Wl[n][l] for n in names]

    def layer_weights(l, names, gathered):
        W = {n: Wl[n][l] for n in SMALL}
        for n in SHARDED:
            W[n] = jnp.zeros(layer_shape(n), BF16 if n in MXU_WEIGHTS else F32)
        for n, g in zip(names, gathered):
            W[n] = jnp.concatenate([g[j] for j in range(4)], axis=SHARD_AXIS[n] - 1)
        return W

    def chip_blocks(g, n):
        return jnp.stack(jnp.split(g, 4, axis=SHARD_AXIS[n] - 1))

    tabs = _tables(L)
    norm_in = [Wl['pre_mix_norm'][l].reshape(1, -1) for l in range(DEPTH)]
    def layer_shape(n):
        s = list(Wl[n].shape[1:])
        if n in SHARD_AXIS:
            s[SHARD_AXIS[n] - 1] *= 4
        return tuple(s)

    unbuild = jax.vjp(_build_layer, {n: jnp.zeros(layer_shape(n), F32) for n in WEIGHTS})[1]

    (h,) = _tw_fwd("l0_norm_in", _fn_norm_in, [(x0, _spec2(T, D_MODEL))], [norm_in[0]], [(D_MODEL, BF16)], L, T)
    gathered = _gather_copies(shards(0, first)).call("gather_l0")
    xs, saves, Ks = x0, [], []
    for l in range(DEPTH):
        last = l + 1 == DEPTH
        nxt = None if last else _gather_copies(shards(l + 1, SHARDED))
        if l == 0:
            def late(got):
                K_late = _build_layer(layer_weights(0, later, got))
                return {k: K_late[k] for k in late_keys}

            xs, h, sv = _layer_fwd(l, xs, h, _build_layer(layer_weights(0, first, gathered)), tabs, L, T,
                                   None if last else norm_in[l + 1], _gather_copies(shards(0, later)), late, nxt)
        else:
            xs, h, sv = _layer_fwd(l, xs, h, _build_layer(layer_weights(l, SHARDED, gathered)), tabs, L, T,
                                   None if last else norm_in[l + 1], None, None, nxt)
        gathered = sv['side']
        Ks.append(sv['K'])
        saves.append(sv)
    dy, loss_acc = _loss_call(xs, tgt, L)
    loss = lax.psum(loss_acc[0, 0], ("x", "y", "c"))

    ffn = ['f_w_in', 'f_conv_w', 'f_w_out', 'w_out']
    rest = [n for n in SHARDED if n not in ffn]

    def ffn_side(dK):
        only_w_out = {k: (dK[k] if k == 'w_out' else jnp.zeros(v.shape, F32)) for k, v in Ks[0].items()}
        g = {'f_w_in': dK['f_w_in'], 'f_conv_w': dK['f_conv'][0:3], 'f_w_out': dK['f_w_out'],
             'w_out': unbuild(only_w_out)[0]['w_out']}
        return _scatter_copies([chip_blocks(g[n], n) for n in ffn], None)

    def rest_blocks(dK):
        full = dict(dK)
        full.setdefault('pre_mix_norm', jnp.zeros((1, D_MODEL), F32))
        (g,) = unbuild(full)
        return [chip_blocks(g[n], n) for n in rest]

    def chip_sums(l, names, recvs, dtype=F32):
        return [_sum_slots(f"sum_{n}_{l}", r.reshape(4, -1, r.shape[-1]), dtype) for n, r in zip(names, recvs)]

    grads = [None] * DEPTH
    pairs = {}
    dx, dhn = dy, None
    for l in reversed(range(DEPTH)):
        last = l + 1 == DEPTH

        def host_scatter(dK, got, up=None if last else grads[l + 1]):
            sets = [ffn_side(dK)] + ([] if up is None else [_scatter_copies(rest_blocks(up), None)])
            return _merge_copies(sets)

        def host_swap(dK, got, l=l, last=last):
            r = got['d_delta']
            parts = chip_sums(l, ffn, r[:len(ffn)]) + ([] if last else chip_sums(l + 1, rest, r[len(ffn):]))
            return _swap_copies(parts)

        dxa, dh, dK, dnext, got = _layer_bwd(l, dx, dhn, Ks[l], saves[l], tabs, L, T,
                                             None if last else norm_in[l + 1],
                                             {'d_delta': host_scatter, 'b_attn': host_swap})
        pairs.update({(l, n): p for n, p in zip(ffn, got['b_attn'])})
        if not last:
            pairs.update({(l + 1, n): p for n, p in zip(rest, got['b_attn'][len(ffn):])})
            grads[l + 1]['pre_mix_norm'] = dnext
        grads[l] = dK
        dx, dhn = dxa, dh
    (dx_in,), (grads[0]['pre_mix_norm'],) = _tw_bwd(
        "l0b_norm_in", _fn_norm_in, [(x0, _spec2(T, D_MODEL))], [norm_in[0]], [(dhn, _spec2(T, D_MODEL))], L, T,
        [True], [True], addto={0: (dx, _spec2(T, D_MODEL))})
    small_shapes = [Wl[n].shape for n in SMALL]
    gfull = [unbuild(grads[l])[0] for l in range(DEPTH)]
    gs = _pack([jnp.stack([gfull[l][n] for l in range(DEPTH)]) for n in SMALL], LANE, 8)
    *got0, recv_small = _scatter_copies([b.astype(BF16) for b in rest_blocks(grads[0])], gs).call("scatter_last")
    pairs.update({(0, n): p for n, p in zip(rest, _swap_copies(chip_sums(0, rest, got0, BF16)).call("swap_last"))})

    kinds = ['grad', 'delta', 'new_m', 'new_v']
    res = {}
    for n in SHARDED:
        upd = _adamw_call("adamw_" + n, [pairs[l, n] for l in range(DEPTH)], Wl[n], Ml[n], Vl[n])
        for kind, a in zip(kinds, upd):
            res[kind, n] = a
    small = _adamw_call("adamw_small", [recv_small], *[_pack([W_[n] for n in SMALL], LANE, 8)[None]
                                                       for W_ in (Wl, Ml, Vl)])
    for kind, s in zip(kinds, small):
        for n, a in zip(SMALL, _unpack(s, small_shapes)):
            res[kind, n] = a
    outs = [loss, dx_in.reshape(x.shape)]
    for kind in ['grad', 'delta', 'new_m', 'new_v']:
        outs += [res[kind, n] for n in WEIGHTS]
    return tuple(outs)
```

```python
import functools
import math

import numpy as np
import jax
import jax.numpy as jnp
from jax import lax
from jax.experimental import pallas as pl
from jax.experimental.pallas import tpu as pltpu

F32 = jnp.float32
BF16 = jnp.bfloat16
MESH = pl.DeviceIdType.MESH
VMEM_LIMIT = 48 * 1024 * 1024
LANE = 128

D_MODEL = 1024
DEPTH = 2
GRID_W = 64
ROPE_BASE = 10000.0
EPS = 1e-6
GROUP_W = 256
HEADS = 4
HD = 64
A_NOPE, A_ROPE, A_Q_LORA, A_KV_LORA = 64, 32, 192, 128
A_COLS = A_Q_LORA + A_KV_LORA + A_ROPE
B_COLS = 512
C_XBC = 512
C_COLS = GROUP_W + C_XBC + 8
D_QKV = 768
D_COLS = D_QKV + GROUP_W + 16
IN_COLS = A_COLS + B_COLS + C_COLS + D_COLS
C_CHUNK = 128
D_CHUNK = 64
C_PER_STEP = 1
D_PER_STEP = 4
D_FF = 2816
ADAM_LR, ADAM_B1, ADAM_B2, ADAM_EPS, ADAM_WD, ADAM_STEP = 0.001, 0.9, 0.999, 1e-08, 0.01, 10

WEIGHTS = ['pre_mix_norm', 'w_in', 'a_q_norm', 'a_w_uq', 'a_kv_norm', 'a_w_ukv', 'a_out_norm', 'b_q_norm',
           'b_k_norm', 'b_out_norm', 'c_conv_w', 'c_conv_b', 'c_a_log', 'c_dt_bias', 'c_d_skip', 'c_out_norm',
           'd_conv_w', 'd_a_log', 'd_dt_bias', 'd_out_norm', 'w_out', 'post_mix_norm', 'pre_ffn_norm', 'f_w_in',
           'f_conv_w', 'f_conv_b', 'f_w_out', 'post_ffn_norm']
SHARD_AXIS = {'w_in': 2, 'a_w_uq': 2, 'a_w_ukv': 2, 'c_conv_w': 2, 'd_conv_w': 2, 'w_out': 1, 'f_w_in': 2,
              'f_conv_w': 2, 'f_w_out': 1}
SHARDED = [n for n in WEIGHTS if n in SHARD_AXIS]
SMALL = [n for n in WEIGHTS if n not in SHARD_AXIS]
MXU_WEIGHTS = ('w_in', 'a_w_uq', 'a_w_ukv', 'w_out', 'f_w_in', 'f_w_out')

P_LAYOUT = [('b_q', 0, 512), ('c_xbc', 512, 512), ('a_cq', 1024, 256), ('b_k', 1280, 256), ('d_qkv', 1536, 768),
            ('b_v', 2304, 256), ('c_z', 2560, 256), ('d_z', 2816, 256), ('a_ckv', 3072, 128), ('a_kr', 3200, 128),
            ('c_dt', 3328, 128), ('d_b', 3456, 128), ('d_a', 3584, 128), ('pad', 3712, 128)]
P_OFF = {n: (o, w) for n, o, w in P_LAYOUT}
P_COLS = 3840
O_COLS = 1536


def _cparams(sem):
    return pltpu.CompilerParams(dimension_semantics=sem, vmem_limit_bytes=VMEM_LIMIT)


def _tile(n, target):
    best = None
    for d in range(LANE, min(n, target) + 1, LANE):
        if n % d == 0:
            best = d
    return best if best is not None else n


_NN = ((1,), (0,))
_NT = ((1,), (1,))
_TN = ((0,), (0,))


def _raw_dot(a, b, dims, hi):
    if hi:
        prec = lax.Precision.HIGH if hi == 'high' else lax.Precision.HIGHEST
        return lax.dot_general(a, b, (dims, ((), ())), precision=prec, preferred_element_type=F32)
    return lax.dot_general(a.astype(BF16), b.astype(BF16), (dims, ((), ())), preferred_element_type=F32)


def _make_dots(hi):
    @jax.custom_vjp
    def nn(a, b):
        return _raw_dot(a, b, _NN, hi)

    @jax.custom_vjp
    def nt(a, b):
        return _raw_dot(a, b, _NT, hi)

    @jax.custom_vjp
    def tn(a, b):
        return _raw_dot(a, b, _TN, hi)

    nn.defvjp(lambda a, b: (nn(a, b), (a, b)), lambda r, g: (nt(g, r[1]), tn(r[0], g)))
    nt.defvjp(lambda a, b: (nt(a, b), (a, b)), lambda r, g: (nn(g, r[1]), tn(g, r[0])))
    tn.defvjp(lambda a, b: (tn(a, b), (a, b)), lambda r, g: (nt(r[1], g), nn(r[0], g)))
    return nn, nt, tn


_nn, _nt, _tn = _make_dots(False)
_nn_hi, _nt_hi, _tn_hi = _make_dots(True)
_nn_h3, _nt_h3, _tn_h3 = _make_dots('high')


def _sigmoid(x):
    return 1.0 / (1.0 + jnp.exp(-x))


def _silu(x):
    return x * _sigmoid(x)


def _softplus(x):
    return jnp.maximum(x, 0.0) + jnp.log(1.0 + jnp.exp(-jnp.abs(x)))


def _rms(x, w, n=None):
    n = x.shape[-1] if n is None else n
    ms = jnp.sum(x * x, axis=-1, keepdims=True) * (1.0 / n)
    return x * lax.rsqrt(ms + EPS) * w


def _spec2(T, w, cb=0):
    return pl.BlockSpec((T, w), lambda i: (i, cb))


def _full_spec(a):
    nd = a.ndim
    return pl.BlockSpec(a.shape, lambda i: (0,) * nd)


def _tw_fwd(name, fn, acts, params, outs, L, T):
    na, npar = len(acts), len(params)

    def kern(*refs):
        a = [r[...].astype(F32) for r in refs[:na]]
        p = [r[...].astype(F32) for r in refs[na:na + npar]]
        res = fn(a, p)
        for r, o in zip(refs[na + npar:], res):
            r[...] = o.astype(r.dtype)

    return pl.pallas_call(
        kern, name=name, grid=(L // T,),
        in_specs=[s for _, s in acts] + [_full_spec(p) for p in params],
        out_specs=[_spec2(T, w) for w, _ in outs],
        out_shape=[jax.ShapeDtypeStruct((L, w), dt) for w, dt in outs],
        compiler_params=_cparams(("arbitrary",)),
    )(*[a for a, _ in acts], *params)


def _tw_bwd(name, fn, acts, params, douts, L, T, act_grad, par_grad, addto=None):
    na, npar, nd = len(acts), len(params), len(douts)
    addto = addto or {}
    add_keys = sorted(addto)
    ga = [k for k in range(na) if act_grad[k]]
    gp = [k for k in range(npar) if par_grad[k]]

    def kern(*refs):
        i = pl.program_id(0)
        a = [r[...].astype(F32) for r in refs[:na]]
        p = [r[...].astype(F32) for r in refs[na:na + npar]]
        g = [r[...].astype(F32) for r in refs[na + npar:na + npar + nd]]
        pos = na + npar + nd
        adds = [r[...].astype(F32) for r in refs[pos:pos + len(add_keys)]]
        pos += len(add_keys)
        da_refs = refs[pos:pos + len(ga)]
        dp_refs = refs[pos + len(ga):]

        def f(ad, pd):
            af, pf = list(a), list(p)
            for k, v in zip(ga, ad):
                af[k] = v
            for k, v in zip(gp, pd):
                pf[k] = v
            return fn(af, pf)

        _, vjp = jax.vjp(f, [a[k] for k in ga], [p[k] for k in gp])
        dad, dpd = vjp(list(g))
        for n, (r, d) in enumerate(zip(da_refs, dad)):
            if n in addto:
                d = d + adds[add_keys.index(n)]
            r[...] = d.astype(r.dtype)

        @pl.when(i == 0)
        def _():
            for r in dp_refs:
                r[...] = jnp.zeros(r.shape, F32)

        for r, d in zip(dp_refs, dpd):
            r[...] += d

    def width(spec):
        return spec.block_shape[-1]

    res = pl.pallas_call(
        kern, name=name, grid=(L // T,),
        in_specs=[s for _, s in acts] + [_full_spec(p) for p in params] + [s for _, s in douts]
        + [addto[k][1] for k in add_keys],
        out_specs=[_spec2(T, width(acts[k][1])) for k in ga] + [_full_spec(params[k]) for k in gp],
        out_shape=[jax.ShapeDtypeStruct((L, width(acts[k][1])), F32) for k in ga]
        + [jax.ShapeDtypeStruct(params[k].shape, F32) for k in gp],
        compiler_params=_cparams(("arbitrary",)),
    )(*[a for a, _ in acts], *params, *[a for a, _ in douts], *[addto[k][0] for k in add_keys])
    return list(res[:len(ga)]), list(res[len(ga):])


def _mm(name, a, b, mode, out_dtype, tm, tn, tk):
    halves_a = a.shape[-1] if (a.ndim == 3 and mode == 'nt') else None
    halves_b = b.shape[-1] if (b.ndim == 3 and mode == 'tn') else None
    if mode == 'nn':
        (M, K), N = a.shape, b.shape[1]
    elif mode == 'nt':
        M, K, N = a.shape[-2], (2 * halves_a if halves_a else a.shape[1]), b.shape[0]
    else:
        (K, M), N = a.shape, (2 * halves_b if halves_b else b.shape[1])
    tm = _tile(M, tm)
    tn = _tile(halves_b or N, tn)
    tk = _tile(halves_a or K, tk)
    nk = K // tk
    if mode == 'nn':
        a_spec = pl.BlockSpec((tm, tk), lambda i, j, k: (i, k))
        b_spec = pl.BlockSpec((tk, tn), lambda i, j, k: (k, j))
        dims = _NN
    elif mode == 'nt':
        a_spec = pl.BlockSpec((tm, tk), lambda i, j, k: (i, k))
        if halves_a:
            per = halves_a // tk
            a_spec = pl.BlockSpec((None, tm, tk), lambda i, j, k: (k // per, i, k % per))
        b_spec = pl.BlockSpec((tn, tk), lambda i, j, k: (j, k))
        dims = _NT
    else:
        a_spec = pl.BlockSpec((tk, tm), lambda i, j, k: (k, i))
        b_spec = pl.BlockSpec((tk, tn), lambda i, j, k: (k, j))
        if halves_b:
            per = halves_b // tn
            b_spec = pl.BlockSpec((None, tk, tn), lambda i, j, k: (j // per, k, j % per))
        dims = _TN

    def kern(a_ref, b_ref, o_ref, acc):
        k = pl.program_id(2)

        @pl.when(k == 0)
        def _():
            acc[...] = jnp.zeros(acc.shape, F32)

        acc[...] += lax.dot_general(a_ref[...].astype(BF16), b_ref[...].astype(BF16), (dims, ((), ())),
                                    preferred_element_type=F32)

        @pl.when(k == nk - 1)
        def _():
            o_ref[...] = acc[...].astype(o_ref.dtype)

    return pl.pallas_call(
        kern, name=name, grid=(M // tm, N // tn, nk),
        in_specs=[a_spec, b_spec],
        out_specs=pl.BlockSpec((tm, tn), lambda i, j, k: (i, j)),
        out_shape=jax.ShapeDtypeStruct((M, N), out_dtype),
        scratch_shapes=[pltpu.VMEM((tm, tn), F32)],
        compiler_params=_cparams(("arbitrary", "arbitrary", "arbitrary")),
    )(a, b)


def _host(kern, n_in, n_out, side, grid, n_scratch=0):
    if side is None:
        return kern, [], [], [], [], []
    ni, no = len(side.ins), len(side.out_shapes)

    def hosted(*refs):
        ins, s_in = refs[:n_in], refs[n_in:n_in + ni]
        pos = n_in + ni
        outs, s_out = refs[pos:pos + n_out], refs[pos + n_out:pos + n_out + no]
        pos += n_out + no
        own, sems = refs[pos:pos + n_scratch], refs[pos + n_scratch:]
        ids = [pl.program_id(d) for d in range(len(grid))]
        first = functools.reduce(lambda a, b: a & b, [i == 0 for i in ids])
        last = functools.reduce(lambda a, b: a & b, [i == g - 1 for i, g in zip(ids, grid)])

        @pl.when(first)
        def _():
            side.start(s_in, s_out, sems)

        kern(*ins, *outs, *own)

        @pl.when(last)
        def _():
            side.finish(s_in, s_out, sems)

    return hosted, [_ANY] * ni, [_ANY] * no, side.out_shapes, side.scratch(), side.ins


def _flash_fwd(name, q, k, v, H, rep, scale, L, side=None):
    tq = min(512, L)
    nq = L // tq
    KC = min(2048, L)
    nkc = L // KC
    log2e = 1.0 / math.log(2.0)

    def kern(q_ref, k_ref, v_ref, o_ref, lse_ref):
        qb = q_ref[...]
        m = jnp.full((tq, 1), -1e30, F32)
        l = jnp.zeros((tq, 1), F32)
        acc = jnp.zeros((tq, LANE), F32)
        for c in range(nkc):
            kb = k_ref[c * KC:(c + 1) * KC, :]
            vb = v_ref[c * KC:(c + 1) * KC, :]
            s = lax.dot_general(qb, kb, (_NT, ((), ())), preferred_element_type=F32) * (scale * log2e)
            mn = jnp.maximum(m, jnp.max(s, axis=-1, keepdims=True))
            al = jnp.exp2(m - mn)
            p = jnp.exp2(s - mn)
            l = al * l + jnp.sum(p, axis=-1, keepdims=True)
            acc = al * acc + lax.dot_general(p.astype(BF16), vb, (_NN, ((), ())), preferred_element_type=F32)
            m = mn
        o_ref[...] = acc / l
        lse_ref[...] = m * math.log(2.0) + jnp.log(l)

    kern, s_in, s_out, s_shapes, s_scratch, s_args = _host(kern, 3, 2, side, (H, nq))
    res = pl.pallas_call(
        kern, name=name, grid=(H, nq),
        in_specs=[pl.BlockSpec((tq, LANE), lambda h, i: (i, h)),
                  pl.BlockSpec((L, LANE), lambda h, i: (0, h // rep)),
                  pl.BlockSpec((L, LANE), lambda h, i: (0, h // rep))] + s_in,
        out_specs=[pl.BlockSpec((tq, LANE), lambda h, i: (i, h)),
                   pl.BlockSpec((tq, 1), lambda h, i: (h * nq + i, 0))] + s_out,
        out_shape=[jax.ShapeDtypeStruct((L, H * LANE), F32), jax.ShapeDtypeStruct((H * L, 1), F32)] + s_shapes,
        scratch_shapes=s_scratch,
        compiler_params=_cparams(("arbitrary", "arbitrary")),
    )(q, k, v, *s_args)
    return res[0], res[1], list(res[2:])


def _flash_bwd(name, q, k, v, o, lse, do, H, rep, scale, L, side=None):
    tq = min(512, L)
    nq = L // tq
    KC = min(1024, L)
    nkc = L // KC
    Hkv = H // rep

    def kern(q_ref, k_ref, v_ref, o_ref, lse_ref, do_ref, dq_ref, dk_ref, dv_ref):
        h = pl.program_id(0)
        i = pl.program_id(1)

        @pl.when((i == 0) & (h % rep == 0))
        def _():
            dk_ref[...] = jnp.zeros(dk_ref.shape, F32)
            dv_ref[...] = jnp.zeros(dv_ref.shape, F32)

        qb = q_ref[...]
        do = do_ref[...]
        dob = do.astype(BF16)
        delta = jnp.sum(do * o_ref[...], axis=-1, keepdims=True)
        lse = lse_ref[...]
        dq = jnp.zeros((tq, LANE), F32)
        for c in range(nkc):
            sl = slice(c * KC, (c + 1) * KC)
            kb = k_ref[sl, :]
            vb = v_ref[sl, :]
            s = lax.dot_general(qb, kb, (_NT, ((), ())), preferred_element_type=F32) * scale
            p = jnp.exp(s - lse)
            dp = lax.dot_general(dob, vb, (_NT, ((), ())), preferred_element_type=F32)
            ds = (p * (dp - delta) * scale).astype(BF16)
            dq = dq + lax.dot_general(ds, kb, (_NN, ((), ())), preferred_element_type=F32)
            dk_ref[sl, :] += lax.dot_general(ds, qb, (_TN, ((), ())), preferred_element_type=F32)
            dv_ref[sl, :] += lax.dot_general(p.astype(BF16), dob, (_TN, ((), ())), preferred_element_type=F32)
        dq_ref[...] = dq

    kern, s_in, s_out, s_shapes, s_scratch, s_args = _host(kern, 6, 3, side, (H, nq))
    res = pl.pallas_call(
        kern, name=name, grid=(H, nq),
        in_specs=[pl.BlockSpec((tq, LANE), lambda h, i: (i, h)),
                  pl.BlockSpec((L, LANE), lambda h, i: (0, h // rep)),
                  pl.BlockSpec((L, LANE), lambda h, i: (0, h // rep)),
                  pl.BlockSpec((tq, LANE), lambda h, i: (i, h)),
                  pl.BlockSpec((tq, 1), lambda h, i: (h * nq + i, 0)),
                  pl.BlockSpec((tq, LANE), lambda h, i: (i, h))] + s_in,
        out_specs=[pl.BlockSpec((tq, LANE), lambda h, i: (i, h)),
                   pl.BlockSpec((L, LANE), lambda h, i: (0, h // rep)),
                   pl.BlockSpec((L, LANE), lambda h, i: (0, h // rep))] + s_out,
        out_shape=[jax.ShapeDtypeStruct((L, H * LANE), F32), jax.ShapeDtypeStruct((L, Hkv * LANE), F32),
                   jax.ShapeDtypeStruct((L, Hkv * LANE), F32)] + s_shapes,
        scratch_shapes=s_scratch,
        compiler_params=_cparams(("arbitrary", "arbitrary")),
    )(q, k, v, o, lse, do, *s_args)
    return res[0], res[1], res[2], list(res[3:])


def _shift_dn(x, first_row):
    row = lax.broadcasted_iota(jnp.int32, x.shape, 0)
    return jnp.where(row == 0, first_row, pltpu.roll(x, 1, 0))


def _shift_up(x, last_row):
    n = x.shape[0]
    row = lax.broadcasted_iota(jnp.int32, x.shape, 0)
    return jnp.where(row == n - 1, last_row, pltpu.roll(x, n - 1, 0))


def _halo_specs(ndim, lead, T, tc, cb0, L):
    r8 = T // 8
    last8 = L // 8 - 1
    if ndim == 2:
        return [pl.BlockSpec((T, tc), lambda j, i: (i, cb0 + j)),
                pl.BlockSpec((8, tc), lambda j, i: (jnp.maximum(i * r8 - 1, 0), cb0 + j)),
                pl.BlockSpec((8, tc), lambda j, i: (jnp.minimum((i + 1) * r8, last8), cb0 + j))]
    return [pl.BlockSpec((None, T, tc), lambda j, i: (lead, i, cb0 + j)),
            pl.BlockSpec((None, 8, tc), lambda j, i: (lead, jnp.maximum(i * r8 - 1, 0), cb0 + j)),
            pl.BlockSpec((None, 8, tc), lambda j, i: (lead, jnp.minimum((i + 1) * r8, last8), cb0 + j))]


def _conv_rows(x_ref, xp_ref, xn_ref, w, first, last):
    x = x_ref[...]
    T = x.shape[0]
    w0, w1, w2, b = w[0:1], w[1:2], w[2:3], w[3:4]
    pr = jnp.where(first, 0.0, xp_ref[7:8, :])
    pr2 = jnp.where(first, 0.0, xp_ref[6:7, :])
    nr = jnp.where(last, 0.0, xn_ref[0:1, :])
    nr2 = jnp.where(last, 0.0, xn_ref[1:2, :])
    xm1 = _shift_dn(x, pr)
    xp1 = _shift_up(x, nr)
    pre = xm1 * w0 + x * w1 + xp1 * w2 + b
    pre_m1 = pr2 * w0 + pr * w1 + x[0:1] * w2 + b
    pre_T = x[T - 1:T] * w0 + nr * w1 + nr2 * w2 + b
    return x, xm1, xp1, pre, pre_m1, pre_T


def _conv_grads(dpre, dpre_m1, dpre_T, x, xm1, xp1, w):
    dx = _shift_up(dpre, dpre_T) * w[0:1] + dpre * w[1:2] + _shift_dn(dpre, dpre_m1) * w[2:3]
    row = lax.broadcasted_iota(jnp.int32, (8, x.shape[1]), 0)
    dw = (jnp.where(row == 0, jnp.sum(dpre * xm1, axis=0, keepdims=True), 0.0)
          + jnp.where(row == 1, jnp.sum(dpre * x, axis=0, keepdims=True), 0.0)
          + jnp.where(row == 2, jnp.sum(dpre * xp1, axis=0, keepdims=True), 0.0)
          + jnp.where(row == 3, jnp.sum(dpre, axis=0, keepdims=True), 0.0))
    return dx, dw


def _conv_fwd(name, x, col0, C, w8, act, L, tc):
    T = min(256, L)
    nt = L // T
    cb0 = col0 // tc

    def kern(x_ref, xp_ref, xn_ref, w_ref, o_ref):
        i = pl.program_id(1)
        x = x_ref[...]
        w = w_ref[...]
        pr = jnp.where(i == 0, 0.0, xp_ref[7:8, :])
        nr = jnp.where(i == nt - 1, 0.0, xn_ref[0:1, :])
        pre = _shift_dn(x, pr) * w[0:1] + x * w[1:2] + _shift_up(x, nr) * w[2:3] + w[3:4]
        o_ref[...] = _silu(pre) if act else pre

    return pl.pallas_call(
        kern, name=name, grid=(C // tc, nt),
        in_specs=_halo_specs(2, None, T, tc, cb0, L) + [pl.BlockSpec((8, tc), lambda j, i: (0, j))],
        out_specs=pl.BlockSpec((T, tc), lambda j, i: (i, j)),
        out_shape=jax.ShapeDtypeStruct((L, C), F32),
        compiler_params=_cparams(("arbitrary", "arbitrary")),
    )(x, x, x, w8)


def _conv_bwd(name, x, col0, C, w8, act, gs, L, tc):
    T = min(256, L)
    nt = L // T
    cb0 = col0 // tc
    ng = len(gs)

    def dact(pre, g):
        if not act:
            return g
        s = _sigmoid(pre)
        return g * (s * (1.0 + pre * (1.0 - s)))

    def kern(*refs):
        x_ref, xp_ref, xn_ref, w_ref = refs[:4]
        g_refs = refs[4:4 + 3 * ng]
        dx_ref, dw_ref = refs[4 + 3 * ng:]
        i = pl.program_id(1)
        first = i == 0
        last = i == nt - 1
        w = w_ref[...]
        g = g_refs[0][...]
        gp = g_refs[1][7:8, :]
        gn = g_refs[2][0:1, :]
        for n in range(1, ng):
            g = g + g_refs[3 * n][...]
            gp = gp + g_refs[3 * n + 1][7:8, :]
            gn = gn + g_refs[3 * n + 2][0:1, :]
        x, xm1, xp1, pre, pre_m1, pre_T = _conv_rows(x_ref, xp_ref, xn_ref, w, first, last)
        dpre_m1 = jnp.where(first, 0.0, dact(pre_m1, gp))
        dpre_T = jnp.where(last, 0.0, dact(pre_T, gn))
        dx_ref[...], dw = _conv_grads(dact(pre, g), dpre_m1, dpre_T, x, xm1, xp1, w)

        @pl.when(first)
        def _():
            dw_ref[...] = jnp.zeros((8, tc), F32)

        dw_ref[...] += dw

    g_specs, g_args = [], []
    for arr, lead in gs:
        g_specs += _halo_specs(arr.ndim, lead, T, tc, 0, L)
        g_args += [arr, arr, arr]
    return pl.pallas_call(
        kern, name=name, grid=(C // tc, nt),
        in_specs=_halo_specs(2, None, T, tc, cb0, L) + [pl.BlockSpec((8, tc), lambda j, i: (0, j))] + g_specs,
        out_specs=[pl.BlockSpec((T, tc), lambda j, i: (i, j)), pl.BlockSpec((8, tc), lambda j, i: (0, j))],
        out_shape=[jax.ShapeDtypeStruct((L, C), F32), jax.ShapeDtypeStruct((8, C), F32)],
        compiler_params=_cparams(("arbitrary", "arbitrary")),
    )(x, x, x, w8, *g_args)


FFN_TC = 1408


def _ffn_gate_fwd(name, u, w8, L):
    T = min(256, L)
    nt = L // T
    ncb = D_FF // FFN_TC

    def kern(xg, xgp, xgn, xu, xup, xun, wg_ref, wu_ref, o_ref):
        i = pl.program_id(1)
        pre_g = _conv_rows(xg, xgp, xgn, wg_ref[...], i == 0, i == nt - 1)[3]
        pre_u = _conv_rows(xu, xup, xun, wu_ref[...], i == 0, i == nt - 1)[3]
        o_ref[...] = (_silu(pre_g) * pre_u).astype(BF16)

    return pl.pallas_call(
        kern, name=name, grid=(ncb, nt),
        in_specs=_halo_specs(2, None, T, FFN_TC, 0, L) + _halo_specs(2, None, T, FFN_TC, ncb, L)
        + [pl.BlockSpec((8, FFN_TC), lambda j, i: (0, j)), pl.BlockSpec((8, FFN_TC), lambda j, i: (0, j + ncb))],
        out_specs=pl.BlockSpec((T, FFN_TC), lambda j, i: (i, j)),
        out_shape=jax.ShapeDtypeStruct((L, D_FF), BF16),
        compiler_params=_cparams(("arbitrary", "arbitrary")),
    )(u, u, u, u, u, u, w8, w8)


def _ffn_gate_bwd(name, u, w8, da, L):
    T = min(128, L)
    nt = L // T
    ncb = D_FF // FFN_TC

    def kern(xg, xgp, xgn, xu, xup, xun, wg_ref, wu_ref, d_ref, dp_ref, dn_ref, du_ref, dwg_ref, dwu_ref):
        i = pl.program_id(1)
        first = i == 0
        last = i == nt - 1
        wg = wg_ref[...]
        wu = wu_ref[...]
        g, gm1, gp1, pg, pg_m1, pg_T = _conv_rows(xg, xgp, xgn, wg, first, last)
        v, vm1, vp1, pu, pu_m1, pu_T = _conv_rows(xu, xup, xun, wu, first, last)

        def dpre(pg_, pu_, d):
            s = _sigmoid(pg_)
            return d * pu_ * (s * (1.0 + pg_ * (1.0 - s))), d * (pg_ * s)

        dg, dv = dpre(pg, pu, d_ref[...])
        dg_m1, dv_m1 = dpre(pg_m1, pu_m1, jnp.where(first, 0.0, dp_ref[7:8, :]))
        dg_T, dv_T = dpre(pg_T, pu_T, jnp.where(last, 0.0, dn_ref[0:1, :]))
        du_ref[0], dwg = _conv_grads(dg, dg_m1, dg_T, g, gm1, gp1, wg)
        du_ref[1], dwu = _conv_grads(dv, dv_m1, dv_T, v, vm1, vp1, wu)

        @pl.when(first)
        def _():
            dwg_ref[...] = jnp.zeros(dwg_ref.shape, F32)
            dwu_ref[...] = jnp.zeros(dwu_ref.shape, F32)

        dwg_ref[...] += dwg
        dwu_ref[...] += dwu

    wspec = pl.BlockSpec((8, FFN_TC), lambda j, i: (0, j))
    du, dwg, dwu = pl.pallas_call(
        kern, name=name, grid=(ncb, nt),
        in_specs=_halo_specs(2, None, T, FFN_TC, 0, L) + _halo_specs(2, None, T, FFN_TC, ncb, L)
        + [wspec, pl.BlockSpec((8, FFN_TC), lambda j, i: (0, j + ncb))] + _halo_specs(2, None, T, FFN_TC, 0, L),
        out_specs=[pl.BlockSpec((2, T, FFN_TC), lambda j, i: (0, i, j)), wspec, wspec],
        out_shape=[jax.ShapeDtypeStruct((2, L, D_FF), F32), jax.ShapeDtypeStruct((8, D_FF), F32),
                   jax.ShapeDtypeStruct((8, D_FF), F32)],
        compiler_params=_cparams(("arbitrary", "arbitrary")),
    )(u, u, u, u, u, u, w8, w8, da, da, da)
    return du, jnp.concatenate([dwg, dwu], axis=1)


def _masks(Q, rev):
    ri = lax.broadcasted_iota(jnp.int32, (Q, Q), 0)
    ci = lax.broadcasted_iota(jnp.int32, (Q, Q), 1)
    diff = (ri - ci) * (1 - 2 * rev)
    return diff >= 0, diff > 0


def _lane_pick(v, sel):
    return jnp.sum(v * sel, axis=-1, keepdims=True)


def _ssd_chunk(S, x, B, C, dtraw, alog, dtb, rev, kept=None):
    Q = dtraw.shape[0]
    incl, _ = _masks(Q, rev)
    tri = incl.astype(F32)
    dt = _softplus(dtraw + dtb)
    a_all = dt * (-jnp.exp(alog))
    acum_all = _nn_hi(tri, a_all)
    total_all = jnp.sum(a_all, axis=0, keepdims=True)
    lane = lax.broadcasted_iota(jnp.int32, (1, LANE), 1)
    acum_t = acum_all.T
    sub = lax.broadcasted_iota(jnp.int32, (LANE, 1), 0)
    ys, Sn = [], []
    for h in range(HEADS):
        g = h // 2
        sel = (lane == rev * 4 + h).astype(F32)
        acum = _lane_pick(acum_all, sel)
        dth = _lane_pick(dt, sel)
        tot = _lane_pick(total_all, sel)
        seg = acum - jnp.sum(acum_t * (sub == rev * 4 + h).astype(F32), axis=0, keepdims=True)
        decay = jnp.exp(jnp.where(incl, seg, -1e30))
        xdt = x[h] * dth
        Sh = S[HD * h:HD * (h + 1), :]
        scores = _nt(C[g], B[g]) * decay
        y_diag = _nn(scores, xdt)
        states = _tn(xdt, B[g] * jnp.exp(tot - acum))
        y_off = _nt(C[g], Sh) * jnp.exp(acum)
        ys.append(y_diag + y_off)
        Sn.append(Sh * jnp.exp(tot) + states)
    return ys, jnp.concatenate(Sn, axis=0), []


def _inv_unit_raw(Lm):
    N = Lm.shape[0]
    Q = D_CHUNK
    ri = lax.broadcasted_iota(jnp.int32, (N, N), 0)
    ci = lax.broadcasted_iota(jnp.int32, (N, N), 1)
    X = (ri == ci).astype(F32) - Lm
    P = _raw_dot(Lm, Lm, _NN, False)
    n = 2
    while n < Q:
        X = X + _raw_dot(X, P, _NN, False)
        n *= 2
        if n < Q:
            P = _raw_dot(P, P, _NN, False)
    return X


@jax.custom_vjp
def _inv_unit(Lm, T_saved):
    return _inv_unit_raw(Lm) if T_saved is None else T_saved


def _inv_unit_f(Lm, T_saved):
    T = _inv_unit_raw(Lm) if T_saved is None else T_saved
    return T, T


def _inv_unit_b(T, g):
    return -_raw_dot(_raw_dot(T, g, _TN, False), T, _NT, False), None


_inv_unit.defvjp(_inv_unit_f, _inv_unit_b)


def _delta_chunk(S, q, k, v, braw, araw, alog, dtb, rev, kept=None):
    Q = braw.shape[0]
    N = HEADS * Q
    tri = _masks(Q, rev)[0].astype(F32)
    ri = lax.broadcasted_iota(jnp.int32, (N, N), 0)
    ci = lax.broadcasted_iota(jnp.int32, (N, N), 1)
    sh = int(math.log2(Q))
    same = (ri >> sh) == (ci >> sh)
    diff = (ri - ci) * (1 - 2 * rev)
    incl = same & (diff >= 0)
    strict = same & (diff > 0)
    beta_all = _sigmoid(braw)
    g_all = -jnp.exp(alog) * _softplus(araw + dtb)
    G_all = _nn_hi(tri, g_all)
    Gtot_all = jnp.sum(g_all, axis=0, keepdims=True)
    r = lax.broadcasted_iota(jnp.int32, (N, LANE), 0)
    l = lax.broadcasted_iota(jnp.int32, (N, LANE), 1)
    selm = (l == rev * 4 + (r >> sh)).astype(F32)
    rows4 = lambda a: jnp.concatenate([a] * HEADS, axis=0)
    XG = rows4(G_all) * selm
    G = jnp.sum(XG, axis=-1, keepdims=True)
    bt = jnp.sum(rows4(beta_all) * selm, axis=-1, keepdims=True)
    Gtot = jnp.sum(Gtot_all * selm, axis=-1, keepdims=True)
    decay = jnp.exp(jnp.where(incl, G - _nt_h3(jnp.ones((N, LANE), F32), XG), -1e30))
    qs, ks, vs = (jnp.concatenate(t, axis=0) for t in (q, k, v))
    qn = qs * lax.rsqrt(jnp.sum(qs * qs, axis=-1, keepdims=True) + 1e-6)
    kn = ks * lax.rsqrt(jnp.sum(ks * ks, axis=-1, keepdims=True) + 1e-6)
    qc = qn * (HD ** -0.5)
    kb = kn * bt
    T = _inv_unit(jnp.where(strict, _nt(kb, kn) * decay, 0.0), None if kept is None else kept[0])
    eG = jnp.exp(G)
    u = _nn(T, vs * bt)
    w = _nn(T, kb * eG)
    qk = _nt(qc, kn) * decay
    spread = (lax.broadcasted_iota(jnp.int32, (HD, N), 0)
              == (lax.broadcasted_iota(jnp.int32, (HD, N), 1) & (HD - 1))).astype(F32)
    wide = lambda a: jnp.where(same, _nn(a, spread), 0.0)
    v_new = u - _nn(wide(w), S)
    o = _nn(wide(qc * eG), S) + _nn(qk, v_new)
    S_new = S * jnp.exp(Gtot) + _tn(wide(kn * jnp.exp(Gtot - G)), v_new)
    return [o[Q * h:Q * (h + 1), :] for h in range(HEADS)], S_new, [T]


def _seq_pieces(ref, r0, Q, splits):
    if splits is None:
        return ref[r0:r0 + Q, :]
    return [[ref[r0:r0 + Q, o + w * t:o + w * (t + 1)] for t in range(n)] for o, w, n in splits]


def _store_pieces(ref, r0, Q, splits, vals, extra=None):
    if splits is None:
        ref[r0:r0 + Q, :] = vals
        return
    for g, (o, w, n) in enumerate(splits):
        for t in range(n):
            v = vals[g][t]
            if extra is not None and g == 0:
                v = v + extra[r0:r0 + Q, o + w * t:o + w * (t + 1)]
            ref[r0:r0 + Q, o + w * t:o + w * (t + 1)] = v


def _flat(ins):
    out = []
    for v in ins:
        if isinstance(v, list):
            out.extend(v)
        else:
            out.append(v)
    return out


def _scan_fwd(name, chunk_fn, seqs, rows, Q, L, CH, kept_shapes=()):
    nc = L // Q
    nb = nc // CH
    ns, nr = len(seqs), len(rows)
    nk = 1 + len(kept_shapes)
    BQ = Q * CH

    def kern(*refs):
        s_refs = (refs[:ns], refs[ns:2 * ns])
        r_refs = refs[2 * ns:2 * ns + nr]
        pos = 2 * ns + nr
        y_refs = refs[pos:pos + 2]
        k_refs = (refs[pos + 2:pos + 2 + nk], refs[pos + 2 + nk:pos + 2 + 2 * nk])
        S_scr = refs[pos + 2 + 2 * nk]
        i = pl.program_id(0)

        @pl.when(i == 0)
        def _():
            S_scr[...] = jnp.zeros(S_scr.shape, F32)

        rws = [r[...] for r in r_refs]
        S = [S_scr[0], S_scr[1]]
        for cc in range(CH):
            for d in (0, 1):
                c = cc if d == 0 else CH - 1 - cc
                k_refs[d][0][c] = S[d]
                ins = [_seq_pieces(r, c * Q, Q, sp) for r, (_, _, _, sp) in zip(s_refs[d], seqs)]
                ys, S[d], kept = chunk_fn(S[d], *_flat(ins), *rws, d)
                for r, v in zip(k_refs[d][1:], kept):
                    r[c] = v
                for h in range(HEADS):
                    y_refs[d][c * Q:(c + 1) * Q, HD * h:HD * (h + 1)] = ys[h]
        S_scr[0] = S[0]
        S_scr[1] = S[1]

    fwd_specs = [pl.BlockSpec((BQ, w), functools.partial(lambda i, cb: (i, cb), cb=cb)) for _, w, cb, _ in seqs]
    rev_specs = [pl.BlockSpec((BQ, w), functools.partial(lambda i, cb: (nb - 1 - i, cb), cb=cb))
                 for _, w, cb, _ in seqs]
    arrs = [a for a, _, _, _ in seqs]
    k_shapes = [(GROUP_W, HD)] + list(kept_shapes)
    res = pl.pallas_call(
        kern, name=name, grid=(nb,),
        in_specs=fwd_specs + rev_specs + [pl.BlockSpec((1, LANE), lambda i: (0, 0)) for _ in rows],
        out_specs=[pl.BlockSpec((BQ, GROUP_W), lambda i: (i, 0)),
                   pl.BlockSpec((BQ, GROUP_W), lambda i: (nb - 1 - i, 0))]
        + [pl.BlockSpec((CH,) + s, lambda i: (i, 0, 0)) for s in k_shapes]
        + [pl.BlockSpec((CH,) + s, lambda i: (nb - 1 - i, 0, 0)) for s in k_shapes],
        out_shape=[jax.ShapeDtypeStruct((L, GROUP_W), F32)] * 2
        + [jax.ShapeDtypeStruct((nc,) + s, F32) for s in k_shapes] * 2,
        scratch_shapes=[pltpu.VMEM((2, GROUP_W, HD), F32)],
        compiler_params=_cparams(("arbitrary",)),
    )(*arrs, *arrs, *rows)
    return res[0], res[1], list(res[2:2 + nk]), list(res[2 + nk:])


def _scan_bwd(name, chunk_fn, seqs, rows, ssaves, dy, extra, Q, L, CH, side=None):
    nc = L // Q
    nb = nc // CH
    BQ = Q * CH
    ns, nr = len(seqs), len(rows)
    nk = len(ssaves[0])
    has_extra = extra is not None

    def kern(*refs):
        s_refs = (refs[:ns], refs[ns:2 * ns])
        pos = 2 * ns
        r_refs = refs[pos:pos + nr]
        pos += nr
        k_refs = (refs[pos:pos + nk], refs[pos + nk:pos + 2 * nk])
        pos += 2 * nk
        dy_refs = refs[pos:pos + 2]
        pos += 2
        ex_ref = refs[pos] if has_extra else None
        pos += 1 if has_extra else 0
        ds_refs = (refs[pos:pos + ns], refs[pos + ns:pos + 2 * ns])
        pos += 2 * ns
        dr_refs = refs[pos:pos + nr]
        dS_scr = refs[pos + nr]
        i = pl.program_id(0)

        @pl.when(i == 0)
        def _():
            dS_scr[...] = jnp.zeros(dS_scr.shape, F32)
            for r in dr_refs:
                r[...] = jnp.zeros(r.shape, F32)

        rws = [r[...] for r in r_refs]
        dr_acc = [jnp.zeros((1, LANE), F32) for _ in rows]
        for d in (0, 1):
            dS = dS_scr[d]
            for cc in range(CH):
                c = CH - 1 - cc if d == 0 else cc
                S = k_refs[d][0][c]
                kept = [r[c] for r in k_refs[d][1:]]
                dys = [dy_refs[d][c * Q:(c + 1) * Q, HD * h:HD * (h + 1)] for h in range(HEADS)]
                ins = [_seq_pieces(r, c * Q, Q, sp) for r, (_, _, _, sp) in zip(s_refs[d], seqs)]
                _, vjp = jax.vjp(
                    functools.partial(
                        lambda S_, ins_, rws_, d_, kept_: chunk_fn(S_, *_flat(ins_), *rws_, d_, kept_)[:2],
                        d_=d, kept_=kept),
                    S, ins, rws)
                dS, dins, drws = vjp((dys, dS))
                for n_, (r, (_, _, _, sp)) in enumerate(zip(ds_refs[d], seqs)):
                    _store_pieces(r, c * Q, Q, sp, dins[n_],
                                  extra=ex_ref if (has_extra and d == 0 and n_ == 0) else None)
                dr_acc = [a + g for a, g in zip(dr_acc, drws)]
            dS_scr[d] = dS
        for r, g in zip(dr_refs, dr_acc):
            r[...] += g

    def blk(shape, rev, cb=0):
        nd = len(shape)
        if rev:
            return pl.BlockSpec(shape, lambda i: (i, cb) + (0,) * (nd - 2))
        return pl.BlockSpec(shape, lambda i: (nb - 1 - i, cb) + (0,) * (nd - 2))

    arrs = [a for a, _, _, _ in seqs]
    in_specs = [blk((BQ, w), False, cb) for _, w, cb, _ in seqs] + [blk((BQ, w), True, cb) for _, w, cb, _ in seqs]
    in_specs += [pl.BlockSpec((1, LANE), lambda i: (0, 0)) for _ in rows]
    in_specs += [blk((CH,) + a.shape[1:], False) for a in ssaves[0]]
    in_specs += [blk((CH,) + a.shape[1:], True) for a in ssaves[1]]
    in_specs += [blk((BQ, GROUP_W), False), blk((BQ, GROUP_W), True)]
    args = arrs + arrs + list(rows) + list(ssaves[0]) + list(ssaves[1]) + [dy, dy]
    if has_extra:
        in_specs.append(blk((BQ, GROUP_W), False))
        args.append(extra)
    kern, s_in, s_out, s_shapes, s_scratch, s_args = _host(kern, len(args), 2 * ns + nr, side, (nb,), 1)
    res = pl.pallas_call(
        kern, name=name, grid=(nb,),
        in_specs=in_specs + s_in,
        out_specs=[blk((BQ, w), False) for _, w, _, _ in seqs] + [blk((BQ, w), True) for _, w, _, _ in seqs]
        + [pl.BlockSpec((1, LANE), lambda i: (0, 0)) for _ in rows] + s_out,
        out_shape=[jax.ShapeDtypeStruct((L, w), F32) for _, w, _, _ in seqs] * 2
        + [jax.ShapeDtypeStruct((1, LANE), F32) for _ in rows] + s_shapes,
        scratch_shapes=[pltpu.VMEM((2, GROUP_W, HD), F32)] + s_scratch,
        compiler_params=_cparams(("arbitrary",)),
    )(*args, *s_args)
    return list(res[:ns]), list(res[ns:2 * ns]), list(res[2 * ns:2 * ns + nr]), list(res[2 * ns + nr:])


def _loss_call(y, tgt, L):
    T = min(256, L)

    def kern(y_ref, t_ref, dy_ref, l_ref):
        i = pl.program_id(0)
        e = y_ref[...] - t_ref[...]
        dy_ref[...] = e * (1.0 / D_MODEL)

        @pl.when(i == 0)
        def _():
            l_ref[...] = jnp.zeros(l_ref.shape, F32)

        part = 0.5 * jnp.sum(jnp.sum(e * e, axis=-1, keepdims=True) * (1.0 / D_MODEL), axis=0, keepdims=True)
        l_ref[...] += jnp.broadcast_to(part, l_ref.shape)

    return pl.pallas_call(
        kern, name="loss_head", grid=(L // T,),
        in_specs=[_spec2(T, D_MODEL), _spec2(T, D_MODEL)],
        out_specs=[_spec2(T, D_MODEL), pl.BlockSpec((8, LANE), lambda i: (0, 0))],
        out_shape=[jax.ShapeDtypeStruct((L, D_MODEL), F32), jax.ShapeDtypeStruct((8, LANE), F32)],
        compiler_params=_cparams(("arbitrary",)),
    )(y, tgt)


_ANY = pl.BlockSpec(memory_space=pl.ANY)


def _coords():
    return lax.axis_index("x"), lax.axis_index("y"), lax.axis_index("c")


class _Copies:
    def __init__(self, ins, out_shapes, copies_fn, n_remote, n_local):
        self.ins, self.out_shapes, self.copies_fn = list(ins), list(out_shapes), copies_fn
        self.n_remote, self.n_local = n_remote, n_local

    def scratch(self):
        return [pltpu.SemaphoreType.DMA((self.n_remote,)), pltpu.SemaphoreType.DMA((self.n_remote,)),
                pltpu.SemaphoreType.DMA((self.n_local,))]

    def _descr(self, in_refs, out_refs, sems):
        send_sems, recv_sems, lsems = sems
        remote, local = self.copies_fn(list(in_refs), list(out_refs))
        assert len(remote) == self.n_remote and len(local) == self.n_local
        mk = lambda k, src, dst, peer: pltpu.make_async_remote_copy(
            src_ref=src, dst_ref=dst, send_sem=send_sems.at[k], recv_sem=recv_sems.at[k], device_id=peer,
            device_id_type=MESH)
        sends = [mk(k, src, dst, peer) for k, (src, dst, _, peer) in enumerate(remote)]
        recvs = [mk(k, src, land, peer) for k, (src, _, land, peer) in enumerate(remote)]
        locs = [pltpu.make_async_copy(src, dst, lsems.at[k]) for k, (src, dst) in enumerate(local)]
        return sends, recvs, locs

    def start(self, in_refs, out_refs, sems):
        sends, _, locs = self._descr(in_refs, out_refs, sems)
        for c in locs + sends:
            c.start()

    def finish(self, in_refs, out_refs, sems):
        sends, recvs, locs = self._descr(in_refs, out_refs, sems)
        for c in recvs:
            c.wait_recv()
        for c in sends:
            c.wait_send()
        for c in locs:
            c.wait()

    def call(self, name):
        ni, no = len(self.ins), len(self.out_shapes)

        def body(*refs):
            self.start(refs[:ni], refs[ni:ni + no], refs[ni + no:])
            self.finish(refs[:ni], refs[ni:ni + no], refs[ni + no:])

        return pl.pallas_call(body, name=name, in_specs=[_ANY] * ni, out_specs=[_ANY] * no,
                              out_shape=self.out_shapes, scratch_shapes=self.scratch())(*self.ins)


def _chip_peers(x, y):
    return [(1 - x, y), (x, 1 - y), (1 - x, 1 - y)]


def _gather_copies(arrs):
    def copies_fn(ins, outs):
        x, y, c = _coords()
        me = 2 * x + y
        remote, local = [], []
        for src, out in zip(ins, outs):
            local.append((src, out.at[me]))
            for px, py in _chip_peers(x, y):
                remote.append((src, out.at[me], out.at[2 * px + py], (px, py, c)))
        return remote, local

    shapes = [jax.ShapeDtypeStruct((4,) + a.shape, a.dtype) for a in arrs]
    return _Copies(arrs, shapes, copies_fn, 3 * len(arrs), len(arrs))


def _scatter_copies(Gs, small):
    nb = len(Gs)

    def copies_fn(ins, outs):
        x, y, c = _coords()
        me = 2 * x + y
        remote, local = [], []
        for g, out in zip(ins[:nb], outs[:nb]):
            local.append((g.at[me], out.at[me]))
            for px, py in _chip_peers(x, y):
                remote.append((g.at[2 * px + py], out.at[me], out.at[2 * px + py], (px, py, c)))
        if small is not None:
            dev = 4 * x + 2 * y + c
            gs, outs_ = ins[nb], outs[nb]
            local.append((gs, outs_.at[dev]))
            for mask in range(1, 8):
                px, py, pc = x ^ (mask >> 2), y ^ ((mask >> 1) & 1), c ^ (mask & 1)
                remote.append((gs, outs_.at[dev], outs_.at[4 * px + 2 * py + pc], (px, py, pc)))
        return remote, local

    ins = list(Gs) + ([small] if small is not None else [])
    shapes = [jax.ShapeDtypeStruct(g.shape, g.dtype) for g in Gs]
    if small is not None:
        shapes.append(jax.ShapeDtypeStruct((8,) + small.shape, small.dtype))
    extra = 1 if small is not None else 0
    return _Copies(ins, shapes, copies_fn, 3 * nb + 7 * extra, nb + extra)


SWAP_STREAMS = 8


def _row_chunks(rows):
    k = SWAP_STREAMS
    if rows % (8 * k) == 0 and rows >= 64 * k:
        return [(q * (rows // k), rows // k) for q in range(k)]
    return [(0, rows)]


def _swap_copies(parts):
    chunks = [_row_chunks(p.shape[0]) for p in parts]
    n = sum(len(ch) for ch in chunks)

    def copies_fn(ins, outs):
        x, y, c = _coords()
        remote, local = [], []
        for src, out, ch in zip(ins, outs, chunks):
            for r0, nr in ch:
                rows = pl.ds(r0, nr)
                local.append((src.at[rows], out.at[c, rows]))
                remote.append((src.at[rows], out.at[c, rows], out.at[1 - c, rows], (x, y, 1 - c)))
        return remote, local

    shapes = [jax.ShapeDtypeStruct((2,) + p.shape, p.dtype) for p in parts]
    return _Copies(parts, shapes, copies_fn, n, n)


def _merge_copies(sets):
    ins = [a for s in sets for a in s.ins]
    shapes = [o for s in sets for o in s.out_shapes]

    def copies_fn(in_refs, out_refs):
        remote, local, pi, po = [], [], 0, 0
        for s in sets:
            r, l = s.copies_fn(in_refs[pi:pi + len(s.ins)], out_refs[po:po + len(s.out_shapes)])
            remote += r
            local += l
            pi += len(s.ins)
            po += len(s.out_shapes)
        return remote, local

    return _Copies(ins, shapes, copies_fn, sum(s.n_remote for s in sets), sum(s.n_local for s in sets))


def _row_tile(rows):
    best = rows
    for d in range(8, min(rows, 256) + 1, 8):
        if rows % d == 0:
            best = d
    return best


def _sum_slots(name, recv, out_dtype=F32):
    n, R, W = recv.shape
    tr = _row_tile(R)

    def kern(r_ref, o_ref):
        acc = r_ref[0].astype(F32)
        for s in range(1, n):
            acc = acc + r_ref[s].astype(F32)
        o_ref[...] = acc.astype(out_dtype)

    return pl.pallas_call(
        kern, name=name, grid=(R // tr,),
        in_specs=[pl.BlockSpec((n, tr, W), lambda i: (0, i, 0))],
        out_specs=pl.BlockSpec((tr, W), lambda i: (i, 0)),
        out_shape=jax.ShapeDtypeStruct((R, W), out_dtype),
        compiler_params=_cparams(("arbitrary",)),
    )(recv)


def _adamw_call(name, slots, w, m, v):
    nl = len(slots)
    n, R, W = slots[0].shape
    tr = _row_tile(R)
    nr = R // tr

    def kern(*refs):
        s_refs = refs[:nl]
        w_ref, m_ref, v_ref, g_ref, d_ref, nm_ref, nv_ref = refs[nl:]
        layer = pl.program_id(0)
        g = s_refs[0][0].astype(F32)
        for s in range(1, n):
            g = g + s_refs[0][s].astype(F32)
        for l in range(1, nl):
            gl = s_refs[l][0].astype(F32)
            for s in range(1, n):
                gl = gl + s_refs[l][s].astype(F32)
            g = jnp.where(layer == l, gl, g)
        m_ = ADAM_B1 * m_ref[...] + (1.0 - ADAM_B1) * g
        v_ = ADAM_B2 * v_ref[...] + (1.0 - ADAM_B2) * (g * g)
        m_hat = m_ / (1.0 - ADAM_B1 ** ADAM_STEP)
        v_hat = v_ / (1.0 - ADAM_B2 ** ADAM_STEP)
        g_ref[...] = g
        d_ref[...] = -ADAM_LR * (m_hat / (jnp.sqrt(v_hat) + ADAM_EPS) + ADAM_WD * w_ref[...])
        nm_ref[...] = m_
        nv_ref[...] = v_

    blk = pl.BlockSpec((None, tr, W), lambda l, i: (l, i, 0))
    return pl.pallas_call(
        kern, name=name, grid=(nl, nr),
        in_specs=[pl.BlockSpec((n, tr, W), lambda l, i: (0, i, 0)) for _ in slots] + [blk, blk, blk],
        out_specs=[blk, blk, blk, blk],
        out_shape=[jax.ShapeDtypeStruct((nl, R, W), F32)] * 4,
        compiler_params=_cparams(("arbitrary", "arbitrary")),
    )(*slots, w, m, v)


def _pack(arrs, width, row_mult):
    flat = jnp.concatenate([a.reshape(-1) for a in arrs])
    n = flat.shape[0]
    rows = -(-n // width)
    rows = -(-rows // row_mult) * row_mult
    return jnp.pad(flat, (0, rows * width - n)).reshape(rows, width)


def _unpack(buf, shapes):
    flat = buf.reshape(-1)
    out, pos = [], 0
    for s in shapes:
        n = int(np.prod(s))
        out.append(flat[pos:pos + n].reshape(s))
        pos += n
    return out


def _rope_angles(L, rot_dim):
    rows = L // GRID_W
    row = jnp.repeat(jnp.arange(rows), GRID_W).astype(F32)
    col = jnp.tile(jnp.arange(GRID_W), rows).astype(F32)
    sec = rot_dim // 2
    inv_freq = ROPE_BASE ** (-jnp.arange(0, sec, 2, dtype=F32) / sec)
    ang_r = row[:, None] * inv_freq
    ang_c = col[:, None] * inv_freq
    ang = jnp.concatenate([ang_r, ang_r, ang_c, ang_c], axis=-1)
    return jnp.cos(ang), jnp.sin(ang)


def _rot_matrix(r):
    R = np.zeros((r, r), np.float32)
    q = r // 4
    for s in range(2):
        for t in range(q):
            lo = s * (r // 2) + t
            hi = lo + q
            R[hi, lo] = -1.0
            R[lo, hi] = 1.0
    return R


def _place_tables(L, cos, sin, width, offsets):
    r = cos.shape[1]
    Rm = np.zeros((width, width), np.float32)
    R = _rot_matrix(r)
    cs, ss, pos = [], [], 0
    for o in list(offsets) + [width]:
        if o > pos:
            cs.append(jnp.ones((L, o - pos), F32))
            ss.append(jnp.zeros((L, o - pos), F32))
        if o < width:
            cs.append(cos)
            ss.append(sin)
            Rm[o:o + r, o:o + r] = R
        pos = o + r
    return jnp.concatenate(cs, axis=1), jnp.concatenate(ss, axis=1), jnp.asarray(Rm)


def _head_mean_matrix(width, stride, n):
    M = np.zeros((width, width), np.float32)
    for o in range(0, width, stride):
        M[o:o + n, o:o + n] = 1.0 / n
    return jnp.asarray(M)


def _pad_heads(w, n_heads, real, padded, axis):
    parts = jnp.split(w, n_heads, axis=axis)
    padw = [(0, 0)] * w.ndim
    padw[axis] = (0, padded - real)
    return jnp.concatenate([jnp.pad(p, padw) for p in parts], axis=axis)


def _row128(v):
    v = v.reshape(1, -1)
    return jnp.pad(v, ((0, 0), (0, LANE - v.shape[1])))


def _conv_w8(w, b):
    C = w.shape[1]
    rows = [w, jnp.zeros((1, C), F32) if b is None else b.reshape(1, C), jnp.zeros((4, C), F32)]
    return jnp.concatenate(rows, axis=0)


def _build_layer(W):
    w_in = W['w_in']
    o = 0
    cols = {}
    for name, n in [('a_cq', A_Q_LORA), ('a_ckv', A_KV_LORA), ('a_kr', A_ROPE), ('b_q', 256), ('b_k', 128),
                    ('b_v', 128), ('c_z', 256), ('c_xbc', 512), ('c_dt', 8), ('d_qkv', 768), ('d_z', 256),
                    ('d_b', 8), ('d_a', 8)]:
        cols[name] = w_in[:, o:o + n]
        o += n
    padc = lambda a, lo, width: jnp.pad(a, ((0, 0), (lo, width - lo - a.shape[1])))
    pieces = {
        'b_q': _pad_heads(cols['b_q'], 4, HD, LANE, 1), 'c_xbc': cols['c_xbc'], 'a_cq': padc(cols['a_cq'], 0, 256),
        'b_k': _pad_heads(cols['b_k'], 2, HD, LANE, 1), 'd_qkv': cols['d_qkv'],
        'b_v': _pad_heads(cols['b_v'], 2, HD, LANE, 1), 'c_z': cols['c_z'], 'd_z': cols['d_z'],
        'a_ckv': cols['a_ckv'], 'a_kr': padc(cols['a_kr'], A_NOPE, LANE), 'c_dt': padc(cols['c_dt'], 0, LANE),
        'd_b': padc(cols['d_b'], 0, LANE), 'd_a': padc(cols['d_a'], 0, LANE),
        'pad': jnp.zeros((D_MODEL, LANE), w_in.dtype)}
    out = {'w_in': jnp.concatenate([pieces[n] for n, _, _ in P_LAYOUT], axis=1)}
    out['a_q_norm'] = padc(W['a_q_norm'].reshape(1, -1), 0, 256)
    wuq = jnp.pad(W['a_w_uq'], ((0, 256 - A_Q_LORA), (0, 0)))
    out['a_w_uq'] = _pad_heads(wuq, 4, A_NOPE + A_ROPE, LANE, 1)
    out['a_kv_norm'] = W['a_kv_norm'].reshape(1, -1)
    ukv = W['a_w_ukv'].reshape(A_KV_LORA, HEADS, 2, HD)
    out['a_w_uk'] = _pad_heads(ukv[:, :, 0, :].reshape(A_KV_LORA, 256), 4, HD, LANE, 1)
    out['a_w_uv'] = _pad_heads(ukv[:, :, 1, :].reshape(A_KV_LORA, 256), 4, HD, LANE, 1)
    out['a_out_norm'] = _pad_heads(W['a_out_norm'].reshape(1, -1), 4, HD, LANE, 1)
    out['b_q_norm'] = _pad_heads(jnp.tile(W['b_q_norm'].reshape(1, -1), (1, 4)), 4, HD, LANE, 1)
    out['b_k_norm'] = _pad_heads(jnp.tile(W['b_k_norm'].reshape(1, -1), (1, 2)), 2, HD, LANE, 1)
    out['b_out_norm'] = _pad_heads(W['b_out_norm'].reshape(1, -1), 4, HD, LANE, 1)
    out['c_conv'] = _conv_w8(W['c_conv_w'], W['c_conv_b'])
    out['c_a_log'] = _row128(W['c_a_log'])
    out['c_dt_bias'] = _row128(W['c_dt_bias'])
    out['c_d_skip'] = jnp.repeat(W['c_d_skip'], HD).reshape(1, -1)
    out['c_out_norm'] = W['c_out_norm'].reshape(1, -1)
    out['d_conv'] = _conv_w8(W['d_conv_w'], None)
    out['d_a_log'] = _row128(W['d_a_log'])
    out['d_dt_bias'] = _row128(W['d_dt_bias'])
    out['d_out_norm'] = jnp.tile(W['d_out_norm'].reshape(1, -1), (1, 4))
    wo = W['w_out']
    out['w_out'] = jnp.concatenate([_pad_heads(wo[0:256], 4, HD, LANE, 0), _pad_heads(wo[256:512], 4, HD, LANE, 0),
                                    wo[512:1024]], axis=0)
    for n in ['pre_mix_norm', 'post_mix_norm', 'pre_ffn_norm', 'post_ffn_norm']:
        out[n] = W[n].reshape(1, -1)
    out['f_w_in'] = W['f_w_in']
    out['f_conv'] = _conv_w8(W['f_conv_w'], W['f_conv_b'])
    out['f_w_out'] = W['f_w_out']
    return out


def _fn_norm_in(a, p):
    return [_rms(a[0], p[0])]


def _fn_resid_norm2(a, p):
    x1 = a[0] + _rms(a[1], p[0])
    return [x1, _rms(x1, p[1])]


def _fn_resid_norm(a, p):
    return [a[0] + _rms(a[1], p[0])]


def _fn_a_prep(a, p):
    cq, ckv, kr, cosk, sink = a
    q_norm, w_uq, kv_norm, w_uk, w_uv, rq, rk = p
    cosq = jnp.concatenate([cosk] * HEADS, axis=1)
    sinq = jnp.concatenate([sink] * HEADS, axis=1)
    q = _nn(_rms(cq, q_norm, A_Q_LORA), w_uq)
    q = q * cosq + _nn_h3(q, rq) * sinq
    kvn = _rms(ckv, kv_norm)
    kr_r = kr * cosk + _nn_h3(kr, rk) * sink
    kk = _nn(kvn, w_uk) + jnp.concatenate([kr_r] * HEADS, axis=1)
    return [q, kk, _nn(kvn, w_uv)]


def _fn_b_prep(a, p):
    q, k, v, cos1, sin1 = a
    q_norm, k_norm, mq, mk, rq, rk = p
    cosq, sinq = (jnp.concatenate([t] * 4, axis=1) for t in (cos1, sin1))
    cosk, sink = (jnp.concatenate([t] * 2, axis=1) for t in (cos1, sin1))
    qn = q * lax.rsqrt(_nn_h3(q * q, mq) + EPS) * q_norm
    kn = k * lax.rsqrt(_nn_h3(k * k, mk) + EPS) * k_norm
    return [qn * cosq + _nn_h3(qn, rq) * sinq, kn * cosk + _nn_h3(kn, rk) * sink, v]


def _fn_mixer_post(a, p):
    oa, ob, yc0, yc1, xs, zc, od0, od1, zd = a
    a_norm, b_norm, dskip, c_norm, d_norm, m64 = p
    oc = _rms((yc0 + yc1 + xs * dskip) * _silu(zc), c_norm)
    od = od0 + od1
    odn = od * lax.rsqrt(_nn_h3(od * od, m64) + EPS) * d_norm * _silu(zd)
    return [jnp.concatenate([_rms(oa, a_norm, GROUP_W), _rms(ob, b_norm, GROUP_W), oc, odn], axis=1)]


def _fn_assemble(a, p):
    (dbq, dxbc, dcq, dbk, dqkv, dbv, dzc, dzd, dckv, dkr, ddt0, ddt1, db0, db1, da0, da1) = a
    return [jnp.concatenate([dbq, dxbc, dcq, dbk, dqkv, dbv, dzc, dzd, dckv, dkr, ddt0 + ddt1, db0 + db1,
                             da0 + da1, jnp.zeros_like(dckv)], axis=1)]


def _pspec(T, name):
    off, w = P_OFF[name]
    return _spec2(T, w, off // w)


def _layer_fwd(l, x, h, K, tabs, L, T, next_norm, side_a=None, late=None, side_b=None):
    n = f"l{l}_"
    sv = {'x': x, 'h': h}
    p = _mm(n + "in_proj", h, K['w_in'].astype(BF16), 'nn', F32, 1024, 1280, 1024)
    sv['p'] = p
    a_acts = [(p, _pspec(T, 'a_cq')), (p, _pspec(T, 'a_ckv')), (p, _pspec(T, 'a_kr')),
              (tabs['a_c'], _spec2(T, LANE)), (tabs['a_s'], _spec2(T, LANE))]
    a_pars = [K['a_q_norm'], K['a_w_uq'], K['a_kv_norm'], K['a_w_uk'], K['a_w_uv'], tabs['a_rq'], tabs['a_rk']]
    qa, ka, va = _tw_fwd(n + "a_prep", _fn_a_prep, a_acts, a_pars, [(512, BF16)] * 3, L, T)
    oa, lse_a, got_a = _flash_fwd(n + "a_attn", qa, ka, va, HEADS, 1, (A_NOPE + A_ROPE) ** -0.5, L, side_a)
    if late is not None:
        K = {**K, **late(got_a)}
    sv.update(a_acts=a_acts, a_pars=a_pars, qa=qa, ka=ka, va=va, oa=oa, lse_a=lse_a, K=K)
    b_acts = [(p, _pspec(T, 'b_q')), (p, _pspec(T, 'b_k')), (p, _pspec(T, 'b_v')),
              (tabs['b_c'], _spec2(T, LANE)), (tabs['b_s'], _spec2(T, LANE))]
    b_pars = [K['b_q_norm'], K['b_k_norm'], tabs['b_mq'], tabs['b_mk'], tabs['b_rq'], tabs['b_rk']]
    qb, kb, vb = _tw_fwd(n + "b_prep", _fn_b_prep, b_acts, b_pars, [(512, BF16), (256, BF16), (256, BF16)], L, T)
    ob, lse_b, sv['side'] = _flash_fwd(n + "b_attn", qb, kb, vb, HEADS, 2, HD ** -0.5, L, side_b)
    sv.update(b_acts=b_acts, b_pars=b_pars, qb=qb, kb=kb, vb=vb, ob=ob, lse_b=lse_b)
    xbc = _conv_fwd(n + "c_conv", p, P_OFF['c_xbc'][0], C_XBC, K['c_conv'], True, L, 512)
    c_seqs = [(xbc, C_XBC, 0, [(0, HD, 4), (256, HD, 2), (384, HD, 2)]),
              (p, LANE, P_OFF['c_dt'][0] // LANE, None)]
    c_rows = [K['c_a_log'], K['c_dt_bias']]
    yc0, yc1, sc0, sc1 = _scan_fwd(n + "c_ssd", _ssd_chunk, c_seqs, c_rows, C_CHUNK, L, C_PER_STEP)
    sv.update(xbc=xbc, c_seqs=c_seqs, c_rows=c_rows, sc=(sc0, sc1))
    qkv = _conv_fwd(n + "d_conv", p, P_OFF['d_qkv'][0], D_QKV, K['d_conv'], True, L, 768)
    d_seqs = [(qkv, D_QKV, 0, [(0, HD, 4), (256, HD, 4), (512, HD, 4)]),
              (p, LANE, P_OFF['d_b'][0] // LANE, None), (p, LANE, P_OFF['d_a'][0] // LANE, None)]
    d_rows = [K['d_a_log'], K['d_dt_bias']]
    od0, od1, sd0, sd1 = _scan_fwd(n + "d_delta", _delta_chunk, d_seqs, d_rows, D_CHUNK, L, D_PER_STEP,
                                   [(HEADS * D_CHUNK, HEADS * D_CHUNK)])
    sv.update(qkv=qkv, d_seqs=d_seqs, d_rows=d_rows, sd=(sd0, sd1))
    m_acts = [(oa, _spec2(T, 512)), (ob, _spec2(T, 512)), (yc0, _spec2(T, 256)), (yc1, _spec2(T, 256)),
              (xbc, _spec2(T, 256, 0)), (p, _pspec(T, 'c_z')), (od0, _spec2(T, 256)), (od1, _spec2(T, 256)),
              (p, _pspec(T, 'd_z'))]
    m_pars = [K['a_out_norm'], K['b_out_norm'], K['c_d_skip'], K['c_out_norm'], K['d_out_norm'], tabs['m64']]
    (o,) = _tw_fwd(n + "mixer_post", _fn_mixer_post, m_acts, m_pars, [(O_COLS, BF16)], L, T)
    f1 = _mm(n + "out_proj", o, K['w_out'].astype(BF16), 'nn', F32, 1024, 1024, 1536)
    r1_pars = [K['post_mix_norm'], K['pre_ffn_norm']]
    x1, h2 = _tw_fwd(n + "resid_mix", _fn_resid_norm2, [(x, _spec2(T, D_MODEL)), (f1, _spec2(T, D_MODEL))], r1_pars,
                     [(D_MODEL, F32), (D_MODEL, BF16)], L, T)
    sv.update(m_acts=m_acts, m_pars=m_pars, o=o, f1=f1, r1_pars=r1_pars, x1=x1, h2=h2)
    u = _mm(n + "ffn_in", h2, K['f_w_in'].astype(BF16), 'nn', F32, 1024, 1408, 1024)
    act = _ffn_gate_fwd(n + "ffn_gate", u, K['f_conv'], L)
    f2 = _mm(n + "ffn_out", act, K['f_w_out'].astype(BF16), 'nn', F32, 1024, 1024, 1408)
    sv.update(u=u, act=act, f2=f2)
    xf = [(x1, _spec2(T, D_MODEL)), (f2, _spec2(T, D_MODEL))]
    if next_norm is None:
        (x2,) = _tw_fwd(n + "resid_ffn", _fn_resid_norm, xf, [K['post_ffn_norm']], [(D_MODEL, F32)], L, T)
        hn = None
    else:
        x2, hn = _tw_fwd(n + "resid_ffn", _fn_resid_norm2, xf, [K['post_ffn_norm'], next_norm],
                         [(D_MODEL, F32), (D_MODEL, BF16)], L, T)
    return x2, hn, sv


def _layer_bwd(l, dx2, dhn, K, sv, tabs, L, T, next_norm, hosts=None):
    n = f"l{l}b_"
    dK = {}
    hosts = hosts or {}
    got = {}
    side = lambda name: hosts[name](dK, got) if name in hosts else None
    s2 = lambda w, cb=0: _spec2(T, w, cb)
    xf = [(sv['x1'], s2(D_MODEL)), (sv['f2'], s2(D_MODEL))]
    if next_norm is None:
        (dx1a, df2), (dK['post_ffn_norm'],) = _tw_bwd(n + "resid_ffn", _fn_resid_norm, xf, [K['post_ffn_norm']],
                                                      [(dx2, s2(D_MODEL))], L, T, [True, True], [True])
        dnext = None
    else:
        (dx1a, df2), (dK['post_ffn_norm'], dnext) = _tw_bwd(
            n + "resid_ffn", _fn_resid_norm2, xf, [K['post_ffn_norm'], next_norm],
            [(dx2, s2(D_MODEL)), (dhn, s2(D_MODEL))], L, T, [True, True], [True, True])
    dact = _mm(n + "ffn_out_dx", df2, K['f_w_out'].astype(BF16), 'nt', F32, 1024, 1408, 1024)
    dK['f_w_out'] = _mm(n + "ffn_out_dw", sv['act'], df2, 'tn', F32, 1408, 1024, 1024)
    du, dK['f_conv'] = _ffn_gate_bwd(n + "ffn_gate", sv['u'], K['f_conv'], dact, L)
    dh2 = _mm(n + "ffn_in_dx", du, K['f_w_in'].astype(BF16), 'nt', F32, 1024, 1024, 1408)
    dK['f_w_in'] = _mm(n + "ffn_in_dw", sv['h2'], du, 'tn', F32, 1024, 1408, 1024)
    (dxa, df1), (dK['post_mix_norm'], dK['pre_ffn_norm']) = _tw_bwd(
        n + "resid_mix", _fn_resid_norm2, [(sv['x'], s2(D_MODEL)), (sv['f1'], s2(D_MODEL))], sv['r1_pars'],
        [(dx1a, s2(D_MODEL)), (dh2, s2(D_MODEL))], L, T, [True, True], [True, True])
    do = _mm(n + "out_proj_dx", df1, K['w_out'].astype(BF16), 'nt', F32, 1024, 1536, 1024)
    dK['w_out'] = _mm(n + "out_proj_dw", sv['o'], df1, 'tn', F32, 1536, 1024, 1024)
    (doa, dob, dyc0, _, dxs_skip, dzc, dod0, _, dzd), mp = _tw_bwd(
        n + "mixer_post", _fn_mixer_post, sv['m_acts'], sv['m_pars'], [(do, s2(O_COLS))], L, T,
        [True] * 9, [True] * 5 + [False])
    dK['a_out_norm'], dK['b_out_norm'], dK['c_d_skip'], dK['c_out_norm'], dK['d_out_norm'] = mp
    (dqkv0, db0, da0), (dqkv1, db1, da1), (dK['d_a_log'], dK['d_dt_bias']), got['d_delta'] = _scan_bwd(
        n + "d_delta", _delta_chunk, sv['d_seqs'], sv['d_rows'], sv['sd'], dod0, None, D_CHUNK, L, D_PER_STEP,
        side('d_delta'))
    dqkv, dK['d_conv'] = _conv_bwd(n + "d_conv", sv['p'], P_OFF['d_qkv'][0], D_QKV, K['d_conv'], True,
                                   [(dqkv0, None), (dqkv1, None)], L, 768)
    (dxbc0, ddt0), (dxbc1, ddt1), (dK['c_a_log'], dK['c_dt_bias']), _ = _scan_bwd(
        n + "c_ssd", _ssd_chunk, sv['c_seqs'], sv['c_rows'], sv['sc'], dyc0, dxs_skip, C_CHUNK, L, C_PER_STEP)
    dxbc, dK['c_conv'] = _conv_bwd(n + "c_conv", sv['p'], P_OFF['c_xbc'][0], C_XBC, K['c_conv'], True,
                                   [(dxbc0, None), (dxbc1, None)], L, 512)
    dqb, dkb, dvb, got['b_attn'] = _flash_bwd(n + "b_attn", sv['qb'], sv['kb'], sv['vb'], sv['ob'], sv['lse_b'],
                                              dob, HEADS, 2, HD ** -0.5, L, side('b_attn'))
    (dbq, dbk, dbv), (dK['b_q_norm'], dK['b_k_norm']) = _tw_bwd(
        n + "b_prep", _fn_b_prep, sv['b_acts'], sv['b_pars'], [(dqb, s2(512)), (dkb, s2(256)), (dvb, s2(256))],
        L, T, [True] * 3 + [False] * 2, [True, True] + [False] * 4)
    dqa, dka, dva, got['a_attn'] = _flash_bwd(n + "a_attn", sv['qa'], sv['ka'], sv['va'], sv['oa'], sv['lse_a'],
                                              doa, HEADS, 1, (A_NOPE + A_ROPE) ** -0.5, L, side('a_attn'))
    (dcq, dckv, dkr), ap = _tw_bwd(
        n + "a_prep", _fn_a_prep, sv['a_acts'], sv['a_pars'], [(dqa, s2(512)), (dka, s2(512)), (dva, s2(512))],
        L, T, [True] * 3 + [False] * 2, [True] * 5 + [False] * 2)
    dK['a_q_norm'], dK['a_w_uq'], dK['a_kv_norm'], dK['a_w_uk'], dK['a_w_uv'] = ap
    pieces = [(dbq, s2(512)), (dxbc, s2(512)), (dcq, s2(256)), (dbk, s2(256)), (dqkv, s2(768)), (dbv, s2(256)),
              (dzc, s2(256)), (dzd, s2(256)), (dckv, s2(LANE)), (dkr, s2(LANE)),
              (ddt0, s2(LANE)), (ddt1, s2(LANE)), (db0, s2(LANE)), (db1, s2(LANE)), (da0, s2(LANE)),
              (da1, s2(LANE))]
    (dp,) = _tw_fwd(n + "assemble_dp", _fn_assemble, pieces, [], [(P_COLS, BF16)], L, T)
    dh = _mm(n + "in_proj_dx", dp, K['w_in'].astype(BF16), 'nt', F32, 1024, 1024, 1280)
    dK['w_in'] = _mm(n + "in_proj_dw", sv['h'], dp, 'tn', F32, 1024, 1280, 1024)
    return dxa, dh, dK, dnext, got


def _tables(L):
    ca, sa = _rope_angles(L, A_ROPE)
    cb, sb = _rope_angles(L, HD)
    t = {}
    t['a_c'], t['a_s'], t['a_rk'] = _place_tables(L, ca, sa, LANE, [A_NOPE])
    t['b_c'], t['b_s'], _ = _place_tables(L, cb, sb, LANE, [0])
    t['a_rq'] = _place_tables(8, ca[:8], sa[:8], 512, [LANE * h + A_NOPE for h in range(4)])[2]
    t['b_rq'] = _place_tables(8, cb[:8], sb[:8], 512, [LANE * h for h in range(4)])[2]
    t['b_rk'] = _place_tables(8, cb[:8], sb[:8], 256, [LANE * h for h in range(2)])[2]
    t['b_mq'] = _head_mean_matrix(512, LANE, HD)
    t['b_mk'] = _head_mean_matrix(256, LANE, HD)
    t['m64'] = _head_mean_matrix(256, HD, HD)
    return t


def kernel(x, pre_mix_norm, w_in, a_q_norm, a_w_uq, a_kv_norm, a_w_ukv, a_out_norm, b_q_norm, b_k_norm, b_out_norm, c_conv_w, c_conv_b, c_a_log, c_dt_bias, c_d_skip, c_out_norm, d_conv_w, d_a_log, d_dt_bias, d_out_norm, w_out, post_mix_norm, pre_ffn_norm, f_w_in, f_conv_w, f_conv_b, f_w_out, post_ffn_norm, loss_target, m_pre_mix_norm, m_w_in, m_a_q_norm, m_a_w_uq, m_a_kv_norm, m_a_w_ukv, m_a_out_norm, m_b_q_norm, m_b_k_norm, m_b_out_norm, m_c_conv_w, m_c_conv_b, m_c_a_log, m_c_dt_bias, m_c_d_skip, m_c_out_norm, m_d_conv_w, m_d_a_log, m_d_dt_bias, m_d_out_norm, m_w_out, m_post_mix_norm, m_pre_ffn_norm, m_f_w_in, m_f_conv_w, m_f_conv_b, m_f_w_out, m_post_ffn_norm, v_pre_mix_norm, v_w_in, v_a_q_norm, v_a_w_uq, v_a_kv_norm, v_a_w_ukv, v_a_out_norm, v_b_q_norm, v_b_k_norm, v_b_out_norm, v_c_conv_w, v_c_conv_b, v_c_a_log, v_c_dt_bias, v_c_d_skip, v_c_out_norm, v_d_conv_w, v_d_a_log, v_d_dt_bias, v_d_out_norm, v_w_out, v_post_mix_norm, v_pre_ffn_norm, v_f_w_in, v_f_conv_w, v_f_conv_b, v_f_w_out, v_post_ffn_norm):
    loc = locals()
    Wl = {n: loc[n] for n in WEIGHTS}
    Ml = {n: loc['m_' + n] for n in WEIGHTS}
    Vl = {n: loc['v_' + n] for n in WEIGHTS}
    L = x.shape[1]
    T = min(512, L)
    x0 = x.reshape(L, D_MODEL)
    tgt = loss_target.reshape(L, D_MODEL)

    first = ['w_in', 'a_w_uq', 'a_w_ukv', 'c_conv_w', 'd_conv_w']
    later = [n for n in SHARDED if n not in first]
    late_keys = ['w_out', 'f_w_in', 'f_conv', 'f_w_out']

    def shards(l, names):
        return [Wl[n][l].astype(BF16) if n in MXU_WEIGHTS else Wl[n][l] for n in names]

    def layer_weights(l, names, gathered):
        W = {n: Wl[n][l] for n in SMALL}
        for n in SHARDED:
            W[n] = jnp.zeros(layer_shape(n), BF16 if n in MXU_WEIGHTS else F32)
        for n, g in zip(names, gathered):
            W[n] = jnp.concatenate([g[j] for j in range(4)], axis=SHARD_AXIS[n] - 1)
        return W

    def chip_blocks(g, n):
        return jnp.stack(jnp.split(g, 4, axis=SHARD_AXIS[n] - 1))

    tabs = _tables(L)
    norm_in = [Wl['pre_mix_norm'][l].reshape(1, -1) for l in range(DEPTH)]
    def layer_shape(n):
        s = list(Wl[n].shape[1:])
        if n in SHARD_AXIS:
            s[SHARD_AXIS[n] - 1] *= 4
        return tuple(s)

    unbuild = jax.vjp(_build_layer, {n: jnp.zeros(layer_shape(n), F32) for n in WEIGHTS})[1]

    (h,) = _tw_fwd("l0_norm_in", _fn_norm_in, [(x0, _spec2(T, D_MODEL))], [norm_in[0]], [(D_MODEL, BF16)], L, T)
    gathered = _gather_copies(shards(0, first)).call("gather_l0")
    xs, saves, Ks = x0, [], []
    for l in range(DEPTH):
        last = l + 1 == DEPTH
        nxt = None if last else _gather_copies(shards(l + 1, SHARDED))
        if l == 0:
            def late(got):
                K_late = _build_layer(layer_weights(0, later, got))
                return {k: K_late[k] for k in late_keys}

            xs, h, sv = _layer_fwd(l, xs, h, _build_layer(layer_weights(0, first, gathered)), tabs, L, T,
                                   None if last else norm_in[l + 1], _gather_copies(shards(0, later)), late, nxt)
        else:
            xs, h, sv = _layer_fwd(l, xs, h, _build_layer(layer_weights(l, SHARDED, gathered)), tabs, L, T,
                                   None if last else norm_in[l + 1], None, None, nxt)
        gathered = sv['side']
        Ks.append(sv['K'])
        saves.append(sv)
    dy, loss_acc = _loss_call(xs, tgt, L)
    loss = lax.psum(loss_acc[0, 0], ("x", "y", "c"))

    ffn = ['f_w_in', 'f_conv_w', 'f_w_out', 'w_out']
    rest = [n for n in SHARDED if n not in ffn]

    def ffn_side(dK):
        only_w_out = {k: (dK[k] if k == 'w_out' else jnp.zeros(v.shape, F32)) for k, v in Ks[0].items()}
        g = {'f_w_in': dK['f_w_in'], 'f_conv_w': dK['f_conv'][0:3], 'f_w_out': dK['f_w_out'],
             'w_out': unbuild(only_w_out)[0]['w_out']}
        return _scatter_copies([chip_blocks(g[n], n) for n in ffn], None)

    def rest_blocks(dK):
        full = dict(dK)
        full.setdefault('pre_mix_norm', jnp.zeros((1, D_MODEL), F32))
        (g,) = unbuild(full)
        return [chip_blocks(g[n], n) for n in rest]

    def chip_sums(l, names, recvs, dtype=F32):
        return [_sum_slots(f"sum_{n}_{l}", r.reshape(4, -1, r.shape[-1]), dtype) for n, r in zip(names, recvs)]

    grads = [None] * DEPTH
    pairs = {}
    dx, dhn = dy, None
    for l in reversed(range(DEPTH)):
        last = l + 1 == DEPTH

        def host_scatter(dK, got, up=None if last else grads[l + 1]):
            sets = [ffn_side(dK)] + ([] if up is None else [_scatter_copies(rest_blocks(up), None)])
            return _merge_copies(sets)

        def host_swap(dK, got, l=l, last=last):
            r = got['d_delta']
            parts = chip_sums(l, ffn, r[:len(ffn)]) + ([] if last else chip_sums(l + 1, rest, r[len(ffn):]))
            return _swap_copies(parts)

        dxa, dh, dK, dnext, got = _layer_bwd(l, dx, dhn, Ks[l], saves[l], tabs, L, T,
                                             None if last else norm_in[l + 1],
                                             {'d_delta': host_scatter, 'b_attn': host_swap})
        pairs.update({(l, n): p for n, p in zip(ffn, got['b_attn'])})
        if not last:
            pairs.update({(l + 1, n): p for n, p in zip(rest, got['b_attn'][len(ffn):])})
            grads[l + 1]['pre_mix_norm'] = dnext
        grads[l] = dK
        dx, dhn = dxa, dh
    (dx_in,), (grads[0]['pre_mix_norm'],) = _tw_bwd(
        "l0b_norm_in", _fn_norm_in, [(x0, _spec2(T, D_MODEL))], [norm_in[0]], [(dhn, _spec2(T, D_MODEL))], L, T,
        [True], [True], addto={0: (dx, _spec2(T, D_MODEL))})
    small_shapes = [Wl[n].shape for n in SMALL]
    gfull = [unbuild(grads[l])[0] for l in range(DEPTH)]
    gs = _pack([jnp.stack([gfull[l][n] for l in range(DEPTH)]) for n in SMALL], LANE, 8)
    *got0, recv_small = _scatter_copies([b.astype(BF16) for b in rest_blocks(grads[0])], gs).call("scatter_last")
    pairs.update({(0, n): p for n, p in zip(rest, _swap_copies(chip_sums(0, rest, got0, BF16)).call("swap_last"))})

    kinds = ['grad', 'delta', 'new_m', 'new_v']
    res = {}
    for n in SHARDED:
        upd = _adamw_call("adamw_" + n, [pairs[l, n] for l in range(DEPTH)], Wl[n], Ml[n], Vl[n])
        for kind, a in zip(kinds, upd):
            res[kind, n] = a
    small = _adamw_call("adamw_small", [recv_small], *[_pack([W_[n] for n in SMALL], LANE, 8)[None]
                                                       for W_ in (Wl, Ml, Vl)])
    for kind, s in zip(kinds, small):
        for n, a in zip(SMALL, _unpack(s, small_shapes)):
            res[kind, n] = a
    outs = [loss, dx_in.reshape(x.shape)]
    for kind in ['grad', 'delta', 'new_m', 'new_v']:
        outs += [res[kind, n] for n in WEIGHTS]
    return tuple(outs)
```
